```python
import math
import functools
import numpy as np
import jax
import jax.numpy as jnp
from jax import lax

D_MODEL = 1024
BATCH = 1
SEQ = 16384
DEPTH = 2

CTX_LEN = 256
GRID_W = 64
N_BRANCH = 4
N_HEADS = 4
HEAD_V = 128
BRANCH_W = N_HEADS * HEAD_V
GDN_DK = 128
CONV_K = 5
MLA_Q_RANK = 384
MLA_KV_RANK = 256
MLA_NOPE = 128
MLA_ROPE = 64
MLA_QK = MLA_NOPE + MLA_ROPE
MLSTM_DK = 64
RET_DK = 64
CHUNK = 64
Q_BLOCK = 128
ROPE_BASE = 10000.0
D_FF = 3584
N_EXPERTS = 8
TOP_K = 2
MOE_BLOCK = 256
N_DENSE = (DEPTH + 1) // 2
N_MOE = DEPTH // 2
EPS = 1e-6
F32 = jnp.float32

IN_WIDTHS = (
    N_BRANCH * D_MODEL,
    N_HEADS * GDN_DK, N_HEADS * GDN_DK, BRANCH_W, BRANCH_W, 2 * N_HEADS, 2 * N_HEADS,
    MLA_Q_RANK, MLA_KV_RANK, MLA_ROPE,
    N_HEADS * MLSTM_DK, N_HEADS * MLSTM_DK, BRANCH_W, BRANCH_W, 2 * N_HEADS, 2 * N_HEADS,
    N_HEADS * RET_DK, N_HEADS * RET_DK, BRANCH_W, BRANCH_W,
)
IN_COLS = sum(IN_WIDTHS)

kernel_name = 'hybrid_flow_backbone'


def rmsnorm(x, g):
    xf = x.astype(F32)
    y = xf * lax.rsqrt(jnp.mean(xf * xf, axis=-1, keepdims=True) + EPS)
    return (y * g.astype(F32)).astype(x.dtype)


def l2norm(t):
    return t * lax.rsqrt(jnp.sum(t * t, axis=-1, keepdims=True) + EPS)


def split_cols(z, widths):
    offsets = [int(o) for o in np.cumsum(widths)[:-1]]
    return jnp.split(z, offsets, axis=-1)


def to_heads(t):
    b, l, _ = t.shape
    return t.reshape(b, l, N_HEADS, -1).transpose(0, 2, 1, 3)


def merge_heads(t):
    b, h, l, d = t.shape
    return t.transpose(0, 2, 1, 3).reshape(b, l, h * d)


def head_norm(o, g, centre=False):
    if centre:
        o = o - jnp.mean(o, axis=-1, keepdims=True)
    o = o * lax.rsqrt(jnp.mean(o * o, axis=-1, keepdims=True) + EPS)
    return merge_heads(o) * g.astype(F32)


def rotate(t, pos):
    d = t.shape[-1]
    inv = ROPE_BASE ** (-jnp.arange(0, d, 2, dtype=F32) / d)
    ang = pos.astype(F32)[:, None] * inv[None, :]
    cos, sin = jnp.cos(ang), jnp.sin(ang)
    tf = t.astype(F32)
    t1, t2 = tf[..., : d // 2], tf[..., d // 2:]
    return jnp.concatenate([t1 * cos - t2 * sin, t1 * sin + t2 * cos], axis=-1).astype(t.dtype)


def axial_rotate(t, rows):
    half = t.shape[-1] // 2
    row = jnp.repeat(jnp.arange(rows), GRID_W)
    col = jnp.tile(jnp.arange(GRID_W), rows)
    return jnp.concatenate([rotate(t[..., :half], row), rotate(t[..., half:], col)], axis=-1)


def short_conv(x, w):
    pad = CONV_K // 2
    y = lax.conv_general_dilated(x, w[:, None, :].astype(x.dtype), (1,), [(pad, pad)],
                                 dimension_numbers=('NWC', 'WIO', 'NWC'), feature_group_count=x.shape[-1])
    return jax.nn.silu(y)


def chunked(t):
    b, h, l = t.shape[:3]
    return jnp.moveaxis(t.reshape((b, h, l // CHUNK, CHUNK) + t.shape[3:]), 2, 0)


def unchunked(t):
    n, b, h, c, d = t.shape
    return jnp.moveaxis(t, 0, 2).reshape(b, h, n * c, d)


def chunk_masks():
    i = jnp.arange(CHUNK)
    return i[:, None] >= i[None, :], i[:, None] > i[None, :]


def gdn_scan(inputs, state):
    q, k, v, log_a, beta = (chunked(t) for t in inputs)
    incl, strict = chunk_masks()
    g = jnp.cumsum(log_a, axis=-1)
    decay = jnp.exp(jnp.where(incl, g[..., :, None] - g[..., None, :], -jnp.inf))
    kb = k * beta[..., None]
    a = jnp.where(strict, jnp.einsum('nbhcd,nbhsd->nbhcs', kb, k) * decay, 0.0)
    lhs = jnp.eye(CHUNK, dtype=a.dtype) + a
    rhs = jnp.concatenate([v * beta[..., None], kb * jnp.exp(g)[..., None]], axis=-1)
    sol = lax.linalg.triangular_solve(lhs, rhs, left_side=True, lower=True)
    dv = v.shape[-1]
    u, w = sol[..., :dv], sol[..., dv:]
    qk = jnp.einsum('nbhcd,nbhsd->nbhcs', q, k) * decay
    q_in = q * jnp.exp(g)[..., None]
    k_out = k * jnp.exp(g[..., -1:] - g)[..., None]
    chunk_decay = jnp.exp(g[..., -1])

    def step(s, xs):
        q_c, k_c, u_c, w_c, qk_c, d_c = xs
        v_new = u_c - w_c @ s
        o = q_c @ s + qk_c @ v_new
        s = s * d_c[..., None, None] + jnp.swapaxes(k_c, -1, -2) @ v_new
        return s, o

    state, o = lax.scan(step, state, (q_in, k_out, u, w, qk, chunk_decay))
    return unchunked(o), state


def mlstm_scan(inputs, state):
    q, k, v, log_i, log_f = (chunked(t) for t in inputs)
    incl, _ = chunk_masks()
    b = jnp.cumsum(log_f, axis=-1)
    d_log = jnp.where(incl, b[..., :, None] - b[..., None, :] + log_i[..., None, :], -jnp.inf)
    m_intra = jnp.max(d_log, axis=-1)
    e_log = b[..., -1:] - b + log_i
    m_end = jnp.max(e_log, axis=-1)
    qk = jnp.einsum('nbhcd,nbhsd->nbhcs', q, k)

    def step(carry, xs):
        c_s, n_s, m_s = carry
        q_c, k_c, v_c, b_c, d_c, mi_c, e_c, me_c, qk_c = xs
        m_t = jnp.maximum(b_c + m_s[..., None], mi_c)
        inter = jnp.exp(b_c + m_s[..., None] - m_t)
        w_qk = jnp.exp(d_c - m_t[..., None]) * qk_c
        num = inter[..., None] * (q_c @ c_s) + w_qk @ v_c
        den = inter * jnp.einsum('bhcd,bhd->bhc', q_c, n_s) + jnp.sum(w_qk, axis=-1)
        h = num / jnp.maximum(jnp.abs(den), jnp.exp(-m_t))[..., None]
        m_new = jnp.maximum(b_c[..., -1] + m_s, me_c)
        carry_decay = jnp.exp(b_c[..., -1] + m_s - m_new)
        k_w = k_c * jnp.exp(e_c - m_new[..., None])[..., None]
        c_s = carry_decay[..., None, None] * c_s + jnp.swapaxes(k_w, -1, -2) @ v_c
        n_s = carry_decay[..., None] * n_s + jnp.sum(k_w, axis=-2)
        return (c_s, n_s, m_new), h

    state, h = lax.scan(step, state, (q, k, v, b, d_log, m_intra, e_log, m_end, qk))
    return unchunked(h), state


def retention_scan(log_gamma, inputs, state):
    q, k, v = (chunked(t) for t in inputs)
    incl, _ = chunk_masks()
    pos = jnp.arange(CHUNK, dtype=F32)
    lg = log_gamma[:, None]
    decay = jnp.exp(jnp.where(incl, lg[..., None] * (pos[:, None] - pos[None, :]), -jnp.inf))
    o_intra = (jnp.einsum('nbhcd,nbhsd->nbhcs', q, k) * decay) @ v
    q_in = q * jnp.exp(lg * (pos + 1.0))[..., None]
    k_out = k * jnp.exp(lg * (CHUNK - 1.0 - pos))[..., None]
    chunk_decay = jnp.exp(log_gamma * CHUNK)[:, None, None]

    def step(s, xs):
        q_c, k_c, v_c = xs
        o = q_c @ s
        s = chunk_decay * s + jnp.swapaxes(k_c, -1, -2) @ v_c
        return s, o

    state, o_inter = lax.scan(step, state, (q_in, k_out, v))
    return unchunked(o_intra + o_inter), state


def flip_time(ts):
    return tuple(jnp.flip(t, axis=2) for t in ts)


def bidirectional_scan(scan_fn, ctx_f, ctx_b, lat_f, lat_b, state0):
    oc_f, st_f = scan_fn(ctx_f, state0)
    ol_f, _ = scan_fn(lat_f, st_f)
    oc_b, st_b = scan_fn(flip_time(ctx_b), state0)
    ol_b, _ = scan_fn(flip_time(lat_b), st_b)
    return oc_f + jnp.flip(oc_b, axis=2), ol_f + jnp.flip(ol_b, axis=2)


def block_attention(q, k, v):
    b, h, lq, dq = q.shape
    nb = lq // Q_BLOCK
    qb = jnp.moveaxis(q.reshape(b, h, nb, Q_BLOCK, dq), 2, 0)
    scale = dq ** -0.5

    def one(qi):
        s = jnp.einsum('bhqd,bhkd->bhqk', qi, k, preferred_element_type=F32) * scale
        p = jax.nn.softmax(s, axis=-1)
        return jnp.einsum('bhqk,bhkd->bhqd', p.astype(v.dtype), v)

    o = lax.map(one, qb)
    return jnp.moveaxis(o, 0, 2).reshape(b, h, lq, -1)


def gdn_branch(parts_c, parts_l, conv_w, a_log, dt_bias, norm_g, with_ctx):
    def prep(parts):
        q, k, v, _, a, bt = parts
        qkv = short_conv(jnp.concatenate([q, k, v], axis=-1), conv_w)
        q, k, v = split_cols(qkv, (N_HEADS * GDN_DK, N_HEADS * GDN_DK, BRANCH_W))
        q = l2norm(to_heads(q).astype(F32)) * GDN_DK ** -0.5
        k = l2norm(to_heads(k).astype(F32))
        v = to_heads(v).astype(F32)
        bsz, l, _ = a.shape
        a = a.astype(F32).reshape(bsz, l, 2, N_HEADS)
        bt = bt.astype(F32).reshape(bsz, l, 2, N_HEADS)
        log_a = (-jnp.exp(a_log.astype(F32)) * jax.nn.softplus(a + dt_bias.astype(F32))).transpose(2, 0, 3, 1)
        beta = jax.nn.sigmoid(bt).transpose(2, 0, 3, 1)
        return (q, k, v, log_a[0], beta[0]), (q, k, v, log_a[1], beta[1])

    cf, cb = prep(parts_c)
    lf, lb = prep(parts_l)
    state0 = jnp.zeros((parts_l[0].shape[0], N_HEADS, GDN_DK, HEAD_V), F32)
    oc, ol = bidirectional_scan(gdn_scan, cf, cb, lf, lb, state0)

    def post(o, z):
        return (head_norm(o, norm_g) * jax.nn.silu(z.astype(F32))).astype(z.dtype)

    return (post(oc, parts_c[3]) if with_ctx else None), post(ol, parts_l[3])


def mla_branch(parts_c, parts_l, q_norm_g, kv_norm_g, w_uq, w_ukv, rows, with_ctx):
    def project(parts, rotary):
        cq, ckv, kr = parts
        q = to_heads(rmsnorm(cq, q_norm_g) @ w_uq)
        kv = to_heads(rmsnorm(ckv, kv_norm_g) @ w_ukv)
        k_nope, v = kv[..., :MLA_NOPE], kv[..., MLA_NOPE:]
        q_nope, q_rope = q[..., :MLA_NOPE], q[..., MLA_NOPE:]
        kr = kr[:, None]
        if rotary:
            q_rope, kr = axial_rotate(q_rope, rows), axial_rotate(kr, rows)
        q = jnp.concatenate([q_nope, q_rope], axis=-1)
        k = jnp.concatenate([k_nope, jnp.broadcast_to(kr, k_nope.shape[:3] + (MLA_ROPE,))], axis=-1)
        return q, k, v

    qc, kc, vc = project(parts_c, False)
    ql, kl, vl = project(parts_l, True)
    out_l = merge_heads(block_attention(ql, jnp.concatenate([kl, kc], axis=2), jnp.concatenate([vl, vc], axis=2)))
    out_c = merge_heads(block_attention(qc, kc, vc)) if with_ctx else None
    return out_c, out_l


def mlstm_branch(parts_c, parts_l, gate_b, norm_g, with_ctx):
    def prep(parts):
        q, k, v, _, ig, fg = parts
        q = to_heads(q).astype(F32) * MLSTM_DK ** -0.5
        k = to_heads(k).astype(F32)
        v = to_heads(v).astype(F32)
        bsz, l, _ = ig.shape
        log_i = (ig.astype(F32).reshape(bsz, l, 2, N_HEADS) + gate_b[0].astype(F32)).transpose(2, 0, 3, 1)
        log_f = jax.nn.log_sigmoid(fg.astype(F32).reshape(bsz, l, 2, N_HEADS) + gate_b[1].astype(F32)).transpose(2, 0, 3, 1)
        return (q, k, v, log_i[0], log_f[0]), (q, k, v, log_i[1], log_f[1])

    cf, cb = prep(parts_c)
    lf, lb = prep(parts_l)
    bsz = parts_l[0].shape[0]
    state0 = (jnp.zeros((bsz, N_HEADS, MLSTM_DK, HEAD_V), F32),
              jnp.zeros((bsz, N_HEADS, MLSTM_DK), F32),
              jnp.zeros((bsz, N_HEADS), F32))
    oc, ol = bidirectional_scan(mlstm_scan, cf, cb, lf, lb, state0)

    def post(o, og):
        return (head_norm(o, norm_g) * jax.nn.sigmoid(og.astype(F32))).astype(og.dtype)

    return (post(oc, parts_c[3]) if with_ctx else None), post(ol, parts_l[3])


def retention_branch(parts_c, parts_l, norm_g, with_ctx):
    log_gamma = jnp.log(1.0 - 2.0 ** (-5.0 - jnp.arange(N_HEADS, dtype=F32)))

    def prep(parts, rotary):
        q, k, v, _ = parts
        q = to_heads(q).astype(F32)
        k = to_heads(k).astype(F32) * RET_DK ** -0.5
        if rotary:
            pos = jnp.arange(q.shape[2])
            q, k = rotate(q, pos), rotate(k, pos)
        return (q, k, to_heads(v).astype(F32))

    xc = prep(parts_c, False)
    xl = prep(parts_l, True)
    state0 = jnp.zeros((parts_l[0].shape[0], N_HEADS, RET_DK, HEAD_V), F32)
    oc, ol = bidirectional_scan(functools.partial(retention_scan, log_gamma), xc, xc, xl, xl, state0)

    def post(o, g):
        return (head_norm(o, norm_g, centre=True) * jax.nn.silu(g.astype(F32))).astype(g.dtype)

    return (post(oc, parts_c[3]) if with_ctx else None), post(ol, parts_l[3])


def token_mixing(h_c, h_l, rows, with_ctx, w_in, w_branch, w_out, gdn_conv_w, gdn_a_log, gdn_dt_bias,
                 gdn_norm_g, mla_q_norm_g, mla_kv_norm_g, mla_w_uq, mla_w_ukv, mlstm_gate_b, mlstm_norm_g,
                 ret_norm_g):
    zc = split_cols(h_c @ w_in, IN_WIDTHS)
    zl = split_cols(h_l @ w_in, IN_WIDTHS)
    a_c, a_l = gdn_branch(zc[1:7], zl[1:7], gdn_conv_w, gdn_a_log, gdn_dt_bias, gdn_norm_g, with_ctx)
    b_c, b_l = mla_branch(zc[7:10], zl[7:10], mla_q_norm_g, mla_kv_norm_g, mla_w_uq, mla_w_ukv, rows, with_ctx)
    c_c, c_l = mlstm_branch(zc[10:16], zl[10:16], mlstm_gate_b, mlstm_norm_g, with_ctx)
    d_c, d_l = retention_branch(zc[16:20], zl[16:20], ret_norm_g, with_ctx)

    def merge(gate_pre, outs):
        br = jnp.stack(outs, axis=0)
        proj = jnp.einsum('nblc,ncd->blnd', br, w_branch)
        bsz, l = gate_pre.shape[:2]
        gate = jax.nn.sigmoid(gate_pre.reshape(bsz, l, N_BRANCH, D_MODEL))
        return jnp.sum(gate * proj, axis=2) @ w_out

    out_l = merge(zl[0], [a_l, b_l, c_l, d_l])
    out_c = merge(zc[0], [a_c, b_c, c_c, d_c]) if with_ctx else None
    return out_c, out_l


def swiglu(h, w_gu, w_down):
    g, u = jnp.split(h @ w_gu, 2, axis=-1)
    return (jax.nn.silu(g) * u) @ w_down


def moe_ffn(h, w_router, b_router, w_gu, w_down):
    bsz, l, d = h.shape
    tokens = h.reshape(-1, d)
    n = tokens.shape[0]
    logits = (tokens @ w_router).astype(F32) + b_router.astype(F32)
    top_logit, top_e = lax.top_k(logits, TOP_K)
    weight = jax.nn.softmax(top_logit, axis=-1)
    n_assign = n * TOP_K
    flat_e = top_e.reshape(-1)
    flat_tok = jnp.arange(n_assign, dtype=jnp.int32) // TOP_K
    order = jnp.argsort(flat_e)
    sorted_e = flat_e[order]
    counts = jnp.bincount(flat_e, length=N_EXPERTS)
    padded = (counts + MOE_BLOCK - 1) // MOE_BLOCK * MOE_BLOCK
    pad_end = jnp.cumsum(padded)
    rank = jnp.arange(n_assign, dtype=jnp.int32) - (jnp.cumsum(counts) - counts)[sorted_e]
    dest = (pad_end - padded)[sorted_e] + rank
    n_blocks = -(-n_assign // MOE_BLOCK) + N_EXPERTS
    cap = n_blocks * MOE_BLOCK
    slot_tok = jnp.full((cap,), n, jnp.int32).at[dest].set(flat_tok[order])
    slot_w = jnp.zeros((cap,), F32).at[dest].set(weight.reshape(-1)[order])
    block_e = jnp.minimum(jnp.searchsorted(pad_end, jnp.arange(n_blocks) * MOE_BLOCK, side='right'), N_EXPERTS - 1)
    tokens_pad = jnp.concatenate([tokens, jnp.zeros((1, d), tokens.dtype)], axis=0)
    xb = tokens_pad[slot_tok].reshape(n_blocks, MOE_BLOCK, d)
    yb = lax.map(lambda a: swiglu(a[0], w_gu[a[1]], w_down[a[1]]), (xb, block_e))
    y = yb.reshape(cap, d) * slot_w[:, None].astype(h.dtype)
    out = jnp.zeros((n + 1, d), h.dtype).at[slot_tok].add(y)[:n]
    return out.reshape(bsz, l, d)


def modulation(cond, w_mod, b_mod):
    return jnp.split(jax.nn.silu(cond) @ w_mod + b_mod, 6, axis=-1)


def setup_inputs(seed: int = 0) -> dict:
    key = jax.random.key(seed)
    ks = iter(jax.random.split(key, 40))
    nrm = lambda shape, scale: jax.random.normal(next(ks), shape, F32) * scale
    gain = lambda shape: 1.0 + nrm(shape, 0.05)
    x = nrm((BATCH, SEQ, D_MODEL), 1.0)
    c = nrm((BATCH, D_MODEL), 1.0)
    ctx = nrm((BATCH, CTX_LEN, D_MODEL), 1.0)
    c_ctx = nrm((D_MODEL,), 1.0)
    w_mod = nrm((DEPTH, D_MODEL, 6 * D_MODEL), 0.5 * D_MODEL ** -0.5)
    b_mod = nrm((DEPTH, 6 * D_MODEL), 0.01)
    norm1_g = gain((DEPTH, D_MODEL))
    norm2_g = gain((DEPTH, D_MODEL))
    w_in = nrm((DEPTH, D_MODEL, IN_COLS), D_MODEL ** -0.5)
    gdn_conv_w = nrm((DEPTH, CONV_K, 2 * N_HEADS * GDN_DK + BRANCH_W), CONV_K ** -0.5)
    gdn_a_log = jnp.log(jax.random.uniform(next(ks), (DEPTH, 2, N_HEADS), F32, 1.0, 16.0))
    dt = jnp.exp(jax.random.uniform(next(ks), (DEPTH, 2, N_HEADS), F32, math.log(1e-3), math.log(1e-1)))
    gdn_dt_bias = dt + jnp.log(-jnp.expm1(-dt))
    gdn_norm_g = gain((DEPTH, BRANCH_W))
    mla_q_norm_g = gain((DEPTH, MLA_Q_RANK))
    mla_kv_norm_g = gain((DEPTH, MLA_KV_RANK))
    mla_w_uq = nrm((DEPTH, MLA_Q_RANK, N_HEADS * MLA_QK), MLA_Q_RANK ** -0.5)
    mla_w_ukv = nrm((DEPTH, MLA_KV_RANK, N_HEADS * (MLA_NOPE + HEAD_V)), MLA_KV_RANK ** -0.5)
    i_bias = nrm((DEPTH, 2, N_HEADS), 0.1)
    f_bias = jnp.linspace(3.0, 6.0, N_HEADS, dtype=F32) + nrm((DEPTH, 2, N_HEADS), 0.1)
    mlstm_gate_b = jnp.stack([i_bias, f_bias], axis=1)
    mlstm_norm_g = gain((DEPTH, BRANCH_W))
    ret_norm_g = gain((DEPTH, BRANCH_W))
    w_branch = nrm((DEPTH, N_BRANCH, BRANCH_W, D_MODEL), BRANCH_W ** -0.5)
    w_out = nrm((DEPTH, D_MODEL, D_MODEL), D_MODEL ** -0.5)
    ffn_w_in = nrm((N_DENSE, D_MODEL, 2 * D_FF), D_MODEL ** -0.5)
    ffn_w_down = nrm((N_DENSE, D_FF, D_MODEL), D_FF ** -0.5)
    moe_w_router = nrm((N_MOE, D_MODEL, N_EXPERTS), D_MODEL ** -0.5)
    moe_b_router = nrm((N_MOE, N_EXPERTS), 0.01)
    moe_w_in = nrm((N_MOE, N_EXPERTS, D_MODEL, 2 * D_FF), D_MODEL ** -0.5)
    moe_w_down = nrm((N_MOE, N_EXPERTS, D_FF, D_MODEL), D_FF ** -0.5)
    final_norm_g = gain((D_MODEL,))
    return {'x': x, 'c': c, 'ctx': ctx, 'c_ctx': c_ctx, 'w_mod': w_mod, 'b_mod': b_mod,
            'norm1_g': norm1_g, 'norm2_g': norm2_g, 'w_in': w_in, 'gdn_conv_w': gdn_conv_w,
            'gdn_a_log': gdn_a_log, 'gdn_dt_bias': gdn_dt_bias, 'gdn_norm_g': gdn_norm_g,
            'mla_q_norm_g': mla_q_norm_g, 'mla_kv_norm_g': mla_kv_norm_g, 'mla_w_uq': mla_w_uq,
            'mla_w_ukv': mla_w_ukv, 'mlstm_gate_b': mlstm_gate_b, 'mlstm_norm_g': mlstm_norm_g,
            'ret_norm_g': ret_norm_g, 'w_branch': w_branch, 'w_out': w_out, 'ffn_w_in': ffn_w_in,
            'ffn_w_down': ffn_w_down, 'moe_w_router': moe_w_router, 'moe_b_router': moe_b_router,
            'moe_w_in': moe_w_in, 'moe_w_down': moe_w_down, 'final_norm_g': final_norm_g}


def reference(x, c, ctx, c_ctx, w_mod, b_mod, norm1_g, norm2_g, w_in, gdn_conv_w, gdn_a_log, gdn_dt_bias,
              gdn_norm_g, mla_q_norm_g, mla_kv_norm_g, mla_w_uq, mla_w_ukv, mlstm_gate_b, mlstm_norm_g,
              ret_norm_g, w_branch, w_out, ffn_w_in, ffn_w_down, moe_w_router, moe_b_router, moe_w_in,
              moe_w_down, final_norm_g):
    rows = x.shape[1] // GRID_W
    n_ctx = ctx.shape[1]
    x_l, x_c = x, ctx
    for li in range(DEPTH):
        last = li == DEPTH - 1
        sh1, sc1, g1, sh2, sc2, g2 = [t[:, None, :] for t in modulation(c, w_mod[li], b_mod[li])]
        csh1, csc1, cg1, csh2, csc2, cg2 = modulation(c_ctx, w_mod[li], b_mod[li])
        h_l = rmsnorm(x_l, norm1_g[li]) * (1 + sc1) + sh1
        h_c = rmsnorm(x_c, norm1_g[li]) * (1 + csc1) + csh1
        m_c, m_l = token_mixing(h_c, h_l, rows, not last, w_in[li], w_branch[li], w_out[li], gdn_conv_w[li],
                                gdn_a_log[li], gdn_dt_bias[li], gdn_norm_g[li], mla_q_norm_g[li],
                                mla_kv_norm_g[li], mla_w_uq[li], mla_w_ukv[li], mlstm_gate_b[li],
                                mlstm_norm_g[li], ret_norm_g[li])
        x_l = x_l + g1 * m_l
        h = rmsnorm(x_l, norm2_g[li]) * (1 + sc2) + sh2
        if not last:
            x_c = x_c + cg1 * m_c
            h = jnp.concatenate([rmsnorm(x_c, norm2_g[li]) * (1 + csc2) + csh2, h], axis=1)
        if li % 2 == 0:
            f = swiglu(h, ffn_w_in[li // 2], ffn_w_down[li // 2])
        else:
            f = moe_ffn(h, moe_w_router[li // 2], moe_b_router[li // 2], moe_w_in[li // 2], moe_w_down[li // 2])
        if not last:
            x_c = x_c + cg2 * f[:, :n_ctx]
            f = f[:, n_ctx:]
        x_l = x_l + g2 * f
    return rmsnorm(x_l, final_norm_g)
```

```python
import functools
import math

import numpy as np
import jax
import jax.numpy as jnp
from jax import lax
from jax.experimental import pallas as pl
from jax.experimental.pallas import tpu as pltpu

F32 = jnp.float32
BF16 = jnp.bfloat16

D_MODEL = 1024
DEPTH = 2
GRID_W = 64
N_BRANCH = 4
N_HEADS = 4
HEAD_V = 128
BRANCH_W = N_HEADS * HEAD_V
GDN_DK = 128
CONV_K = 5
MLA_Q_RANK = 384
MLA_KV_RANK = 256
MLA_NOPE = 128
MLA_ROPE = 64
MLA_QK = MLA_NOPE + MLA_ROPE
MLA_QK_PAD = 256
MLSTM_DK = 64
RET_DK = 64
CHUNK = 64
ROPE_BASE = 10000.0
D_FF = 3584
N_EXPERTS = 8
TOP_K = 2
EPS = 1e-6

IN_WIDTHS = (
    N_BRANCH * D_MODEL,
    N_HEADS * GDN_DK, N_HEADS * GDN_DK, BRANCH_W, BRANCH_W, 2 * N_HEADS, 2 * N_HEADS,
    MLA_Q_RANK, MLA_KV_RANK, MLA_ROPE,
    N_HEADS * MLSTM_DK, N_HEADS * MLSTM_DK, BRANCH_W, BRANCH_W, 2 * N_HEADS, 2 * N_HEADS,
    N_HEADS * RET_DK, N_HEADS * RET_DK, BRANCH_W, BRANCH_W,
)
_IN_OFF = [0] + [int(o) for o in np.cumsum(IN_WIDTHS)]

VMEM_LIMIT = 48 * 1024 * 1024
MOE_ROWS = 1024


def _cols(group):
    return np.arange(_IN_OFF[group], _IN_OFF[group + 1])


def _rope_swap(cols):
    q = MLA_ROPE // 4
    return np.concatenate([cols[q:2 * q], cols[:q], cols[3 * q:], cols[2 * q:3 * q]])


_MAIN_PERM = np.concatenate([
    _cols(0),
    _cols(1), _cols(2), _cols(3),
    _cols(4),
    _cols(12), _cols(13),
    _cols(10), _cols(11),
    _cols(18), _cols(19),
    _cols(16), _cols(17),
    _cols(7), _cols(8), _cols(9), _rope_swap(_cols(9)),
])
N_MAIN = int(_MAIN_PERM.shape[0])
_SMALL_PERM = np.concatenate([_cols(5), _cols(6), _cols(14), _cols(15), np.zeros(96, np.int64)])
OFF_GATE, OFF_GDN_QKV, OFF_GDN_Z = 0, 4096, 5632
OFF_MLSTM_V, OFF_MLSTM_O, OFF_MLSTM_Q, OFF_MLSTM_K = 6144, 6656, 7168, 7424
OFF_RET_V, OFF_RET_G, OFF_RET_Q, OFF_RET_K = 7680, 8192, 8704, 8960
OFF_MLA = 9216
MLA_IN_W = 768


def _cparams(sem):
    return pltpu.CompilerParams(dimension_semantics=sem, vmem_limit_bytes=VMEM_LIMIT)


def _rms(x):
    return x * lax.rsqrt(jnp.mean(x * x, axis=-1, keepdims=True) + EPS)


def _mod_kernel(c_ref, w_ref, b_ref, o_ref):
    c = c_ref[...]
    s = c * jax.nn.sigmoid(c)
    o_ref[...] = jnp.dot(s, w_ref[...], preferred_element_type=F32) + b_ref[...]


def modulation_vectors(cond, w_mod, b_mod):
    n = w_mod.shape[1]
    tn = 1536
    return pl.pallas_call(
        _mod_kernel,
        out_shape=jax.ShapeDtypeStruct((8, n), F32),
        grid=(n // tn,),
        in_specs=[pl.BlockSpec((8, D_MODEL), lambda j: (0, 0)),
                  pl.BlockSpec((D_MODEL, tn), lambda j: (0, j)),
                  pl.BlockSpec((1, tn), lambda j: (0, j))],
        out_specs=pl.BlockSpec((8, tn), lambda j: (0, j)),
        compiler_params=_cparams(("arbitrary",)),
        name="modulation",
    )(cond, w_mod, b_mod.reshape(1, n))


def _norm_proj_kernel(x_ref, pv_ref, w_ref, o_ref, h_ref):
    @pl.when(pl.program_id(1) == 0)
    def _():
        pv = pv_ref[...]
        h_ref[...] = (_rms(x_ref[...]) * pv[0:1] * pv[1:2] + pv[2:3]).astype(BF16)

    o_ref[...] = jnp.dot(h_ref[...], w_ref[...], preferred_element_type=F32).astype(o_ref.dtype)


def norm_proj(x, pv, w, tn, out_dtype):
    m, d = x.shape
    n = w.shape[1]
    tm = min(m, 1024)
    return pl.pallas_call(
        _norm_proj_kernel,
        out_shape=jax.ShapeDtypeStruct((m, n), out_dtype),
        grid=(m // tm, n // tn),
        in_specs=[pl.BlockSpec((tm, d), lambda i, j: (i, 0)),
                  pl.BlockSpec((8, d), lambda i, j: (0, 0)),
                  pl.BlockSpec((d, tn), lambda i, j: (0, j))],
        out_specs=pl.BlockSpec((tm, tn), lambda i, j: (i, j)),
        scratch_shapes=[pltpu.VMEM((tm, d), BF16)],
        compiler_params=_cparams(("parallel", "arbitrary")),
        name="norm_proj",
    )(x, pv, w)


def _mla_proj_kernel(z_ref, cos_ref, sin_ref, gq_ref, gkv_ref, wq_ref, wkv_ref, q_ref, k_ref, v_ref):
    z = z_ref[...].astype(F32)
    tm = z.shape[0]
    cq = z[:, :MLA_Q_RANK]
    ckv = z[:, MLA_Q_RANK:MLA_Q_RANK + MLA_KV_RANK]
    kr = z[:, MLA_Q_RANK + MLA_KV_RANK:]
    cos = cos_ref[...]
    sin = sin_ref[...]
    qn = (_rms(cq) * gq_ref[...]).astype(BF16)
    kvn = (_rms(ckv) * gkv_ref[...]).astype(BF16)
    qf = jnp.dot(qn, wq_ref[...], preferred_element_type=F32) * (MLA_QK ** -0.5)
    kvf = jnp.dot(kvn, wkv_ref[...], preferred_element_type=F32)
    kr_rot = kr[:, :MLA_ROPE] * cos + kr[:, MLA_ROPE:] * sin
    pad = jnp.zeros((tm, MLA_QK_PAD - MLA_QK), F32)
    for h in range(N_HEADS):
        b = h * 256
        q_rot = qf[:, b + 128:b + 192] * cos + qf[:, b + 192:b + 256] * sin
        q_ref[h] = jnp.concatenate([qf[:, b:b + 128], q_rot, pad], axis=-1).astype(BF16)
        k_ref[h] = jnp.concatenate([kvf[:, b:b + 128], kr_rot, pad], axis=-1).astype(BF16)
        v_ref[h] = kvf[:, b + 128:b + 256].astype(BF16)


def mla_project(zmain, cos, sin, gq, gkv, wq_ext, wkv):
    m = zmain.shape[0]
    tm = min(m, 1024)
    full = lambda shape: pl.BlockSpec(shape, lambda i: tuple(0 for _ in shape))
    return pl.pallas_call(
        _mla_proj_kernel,
        out_shape=(jax.ShapeDtypeStruct((N_HEADS, m, MLA_QK_PAD), BF16),
                   jax.ShapeDtypeStruct((N_HEADS, m, MLA_QK_PAD), BF16),
                   jax.ShapeDtypeStruct((N_HEADS, m, HEAD_V), BF16)),
        grid=(m // tm,),
        in_specs=[pl.BlockSpec((tm, MLA_IN_W), lambda i: (i, OFF_MLA // MLA_IN_W)),
                  pl.BlockSpec((tm, MLA_ROPE), lambda i: (i, 0)),
                  pl.BlockSpec((tm, MLA_ROPE), lambda i: (i, 0)),
                  full((1, MLA_Q_RANK)), full((1, MLA_KV_RANK)),
                  full((MLA_Q_RANK, N_HEADS * 256)), full((MLA_KV_RANK, N_HEADS * 256))],
        out_specs=(pl.BlockSpec((N_HEADS, tm, MLA_QK_PAD), lambda i: (0, i, 0)),
                   pl.BlockSpec((N_HEADS, tm, MLA_QK_PAD), lambda i: (0, i, 0)),
                   pl.BlockSpec((N_HEADS, tm, HEAD_V), lambda i: (0, i, 0))),
        compiler_params=_cparams(("parallel",)),
        name="mla_project",
    )(zmain, cos, sin, gq, gkv, wq_ext, wkv)


def _attn_kernel(*refs, segs):
    q_ref, o_ref = refs[0], refs[-1]
    q = q_ref[0]
    tq = q.shape[0]
    carry = (jnp.full((tq, 1), -1e30, F32), jnp.zeros((tq, 1), F32), jnp.zeros((tq, HEAD_V), F32))
    for si, (tk, nk) in enumerate(segs):
        k_ref, v_ref = refs[1 + 2 * si], refs[2 + 2 * si]

        def body(c, carry, k_ref=k_ref, v_ref=v_ref, tk=tk):
            m, l, acc = carry
            start = pl.multiple_of(c * tk, tk)
            s = lax.dot_general(q, k_ref[0, pl.ds(start, tk), :], (((1,), (1,)), ((), ())),
                                preferred_element_type=F32)
            m_new = jnp.maximum(m, jnp.max(s, axis=-1, keepdims=True))
            p = jnp.exp(s - m_new)
            alpha = jnp.exp(m - m_new)
            l = alpha * l + jnp.sum(p, axis=-1, keepdims=True)
            acc = alpha * acc + jnp.dot(p.astype(BF16), v_ref[0, pl.ds(start, tk), :],
                                        preferred_element_type=F32)
            return m_new, l, acc

        carry = lax.fori_loop(0, nk, body, carry)
    _, l, acc = carry
    o_ref[...] = (acc / l).astype(o_ref.dtype)


def attention(q, kvs):
    _, lq, _ = q.shape
    tq = min(lq, 512)
    segs, args, specs = [], [], []
    for k, v in kvs:
        lk = k.shape[1]
        tk = min(lk, 1024)
        segs.append((tk, lk // tk))
        args += [k, v]
        specs += [pl.BlockSpec((1, lk, MLA_QK_PAD), lambda h, i: (h, 0, 0)),
                  pl.BlockSpec((1, lk, HEAD_V), lambda h, i: (h, 0, 0))]
    return pl.pallas_call(
        functools.partial(_attn_kernel, segs=tuple(segs)),
        out_shape=jax.ShapeDtypeStruct((lq, N_HEADS * HEAD_V), BF16),
        grid=(N_HEADS, lq // tq),
        in_specs=[pl.BlockSpec((1, tq, MLA_QK_PAD), lambda h, i: (h, i, 0))] + specs,
        out_specs=pl.BlockSpec((tq, HEAD_V), lambda h, i: (i, h)),
        compiler_params=_cparams(("parallel", "arbitrary")),
        name="attention",
    )(q, *args)


def _merge_kernel(x_ref, gate_ref, a_ref, b_ref, c_ref, d_ref, wb_ref, wo_ref, g_ref, o_ref):
    s = None
    for n, br in enumerate((a_ref, b_ref, c_ref, d_ref)):
        proj = jnp.dot(br[...], wb_ref[n], preferred_element_type=F32)
        gate = jax.nn.sigmoid(gate_ref[:, n * D_MODEL:(n + 1) * D_MODEL].astype(F32))
        s = gate * proj if s is None else s + gate * proj
    m = jnp.dot(s.astype(BF16), wo_ref[...], preferred_element_type=F32)
    o_ref[...] = x_ref[...] + g_ref[0:1] * m


def merge_branches(x, zmain, branches, w_branch, w_out, gvec):
    m = x.shape[0]
    tm = min(m, 512)
    row = lambda w: pl.BlockSpec((tm, w), lambda i: (i, 0))
    return pl.pallas_call(
        _merge_kernel,
        out_shape=jax.ShapeDtypeStruct((m, D_MODEL), F32),
        grid=(m // tm,),
        in_specs=[row(D_MODEL), row(N_BRANCH * D_MODEL)] + [row(BRANCH_W)] * 4
                 + [pl.BlockSpec((N_BRANCH, BRANCH_W, D_MODEL), lambda i: (0, 0, 0)),
                    pl.BlockSpec((D_MODEL, D_MODEL), lambda i: (0, 0)),
                    pl.BlockSpec((8, D_MODEL), lambda i: (0, 0))],
        out_specs=row(D_MODEL),
        compiler_params=_cparams(("parallel",)),
        name="merge_branches",
    )(x, zmain, *branches, w_branch, w_out, gvec)


def _ffn_kernel(x_ref, pv_ref, wg_ref, wu_ref, wd_ref, o_ref, h_ref, acc_ref):
    f = pl.program_id(1)

    @pl.when(f == 0)
    def _():
        pv = pv_ref[...]
        h_ref[...] = (_rms(x_ref[...]) * pv[0:1] * pv[1:2] + pv[2:3]).astype(BF16)
        acc_ref[...] = jnp.zeros_like(acc_ref)

    h = h_ref[...]
    g = jnp.dot(h, wg_ref[...], preferred_element_type=F32)
    u = jnp.dot(h, wu_ref[...], preferred_element_type=F32)
    act = (g * jax.nn.sigmoid(g) * u).astype(BF16)
    acc_ref[...] += jnp.dot(act, wd_ref[...], preferred_element_type=F32)

    @pl.when(f == pl.num_programs(1) - 1)
    def _():
        o_ref[...] = x_ref[...] + pv_ref[3:4] * acc_ref[...]


def dense_ffn(x, pv, w_gu, w_down):
    m = x.shape[0]
    tm = min(m, 1024)
    tf = 512
    nf = D_FF // tf
    return pl.pallas_call(
        _ffn_kernel,
        out_shape=jax.ShapeDtypeStruct((m, D_MODEL), F32),
        grid=(m // tm, nf),
        in_specs=[pl.BlockSpec((tm, D_MODEL), lambda i, f: (i, 0)),
                  pl.BlockSpec((8, D_MODEL), lambda i, f: (0, 0)),
                  pl.BlockSpec((D_MODEL, tf), lambda i, f: (0, f)),
                  pl.BlockSpec((D_MODEL, tf), lambda i, f: (0, f + nf)),
                  pl.BlockSpec((tf, D_MODEL), lambda i, f: (f, 0))],
        out_specs=pl.BlockSpec((tm, D_MODEL), lambda i, f: (i, 0)),
        scratch_shapes=[pltpu.VMEM((tm, D_MODEL), BF16), pltpu.VMEM((tm, D_MODEL), F32)],
        compiler_params=_cparams(("parallel", "arbitrary")),
        name="dense_ffn",
    )(x, pv, w_gu, w_gu, w_down)


def _router_kernel(x_ref, pv_ref, w_ref, b_ref, h_ref, logit_ref):
    pv = pv_ref[...]
    h = _rms(x_ref[...]) * pv[0:1] * pv[1:2] + pv[2:3]
    h_hi = h.astype(BF16)
    h_lo = (h - h_hi.astype(F32)).astype(BF16)
    w = w_ref[...]
    w_hi = w.astype(BF16)
    w_lo = (w - w_hi.astype(F32)).astype(BF16)
    logits = (jnp.dot(h_hi, w_hi, preferred_element_type=F32)
              + jnp.dot(h_hi, w_lo, preferred_element_type=F32)
              + jnp.dot(h_lo, w_hi, preferred_element_type=F32))
    h_ref[...] = h_hi
    logit_ref[...] = logits + b_ref[...]


def moe_router(x, pv, w_router_pad, b_router_pad):
    m = x.shape[0]
    tm = min(m, 1024)
    return pl.pallas_call(
        _router_kernel,
        out_shape=(jax.ShapeDtypeStruct((m, D_MODEL), BF16), jax.ShapeDtypeStruct((m, 128), F32)),
        grid=(m // tm,),
        in_specs=[pl.BlockSpec((tm, D_MODEL), lambda i: (i, 0)),
                  pl.BlockSpec((8, D_MODEL), lambda i: (0, 0)),
                  pl.BlockSpec((D_MODEL, 128), lambda i: (0, 0)),
                  pl.BlockSpec((1, 128), lambda i: (0, 0))],
        out_specs=(pl.BlockSpec((tm, D_MODEL), lambda i: (i, 0)), pl.BlockSpec((tm, 128), lambda i: (i, 0))),
        compiler_params=_cparams(("parallel",)),
        name="moe_router",
    )(x, pv, w_router_pad, b_router_pad)


def _expert_kernel(be_ref, nb_ref, x_ref, wg_ref, wu_ref, wd_ref, o_ref, acc_ref):
    b = pl.program_id(0)
    f = pl.program_id(1)

    @pl.when(b < nb_ref[0])
    def _():
        @pl.when(f == 0)
        def _():
            acc_ref[...] = jnp.zeros_like(acc_ref)

        x = x_ref[...]
        g = jnp.dot(x, wg_ref[0].astype(BF16), preferred_element_type=F32)
        u = jnp.dot(x, wu_ref[0].astype(BF16), preferred_element_type=F32)
        act = (g * jax.nn.sigmoid(g) * u).astype(BF16)
        acc_ref[...] += jnp.dot(act, wd_ref[0].astype(BF16), preferred_element_type=F32)

        @pl.when(f == pl.num_programs(1) - 1)
        def _():
            o_ref[...] = acc_ref[...]


def expert_ffn(block_e, n_used, xb, w_gu, w_down):
    cap = xb.shape[0]
    nb = cap // MOE_ROWS
    tf = 512
    nf = D_FF // tf

    def live(b, nbr):
        return jnp.minimum(b, nbr[0] - 1)

    def fsel(b, f, nbr):
        return jnp.where(b < nbr[0], f, nf - 1)

    grid_spec = pltpu.PrefetchScalarGridSpec(
        num_scalar_prefetch=2,
        grid=(nb, nf),
        in_specs=[pl.BlockSpec((MOE_ROWS, D_MODEL), lambda b, f, be, nbr: (live(b, nbr), 0)),
                  pl.BlockSpec((1, D_MODEL, tf), lambda b, f, be, nbr: (be[live(b, nbr)], 0, fsel(b, f, nbr))),
                  pl.BlockSpec((1, D_MODEL, tf), lambda b, f, be, nbr: (be[live(b, nbr)], 0, fsel(b, f, nbr) + nf)),
                  pl.BlockSpec((1, tf, D_MODEL), lambda b, f, be, nbr: (be[live(b, nbr)], fsel(b, f, nbr), 0))],
        out_specs=pl.BlockSpec((MOE_ROWS, D_MODEL), lambda b, f, be, nbr: (live(b, nbr), 0)),
        scratch_shapes=[pltpu.VMEM((MOE_ROWS, D_MODEL), F32)],
    )
    return pl.pallas_call(
        _expert_kernel,
        out_shape=jax.ShapeDtypeStruct((cap, D_MODEL), F32),
        grid_spec=grid_spec,
        compiler_params=_cparams(("arbitrary", "arbitrary")),
        name="expert_ffn",
    )(block_e, n_used, xb, w_gu, w_gu, w_down)


def _final_kernel(x_ref, y0_ref, y1_ref, w_ref, pv_ref, o_ref):
    w = w_ref[...]
    f = w[:, 0:1] * y0_ref[...] + w[:, 1:2] * y1_ref[...]
    x = x_ref[...] + pv_ref[0:1] * f
    o_ref[...] = _rms(x) * pv_ref[1:2]


def final_combine(x, y0, y1, w, pv):
    m = x.shape[0]
    tm = min(m, 1024)
    row = lambda wd: pl.BlockSpec((tm, wd), lambda i: (i, 0))
    return pl.pallas_call(
        _final_kernel,
        out_shape=jax.ShapeDtypeStruct((m, D_MODEL), F32),
        grid=(m // tm,),
        in_specs=[row(D_MODEL), row(D_MODEL), row(D_MODEL), row(128),
                  pl.BlockSpec((8, D_MODEL), lambda i: (0, 0))],
        out_specs=row(D_MODEL),
        compiler_params=_cparams(("parallel",)),
        name="final_combine",
    )(x, y0, y1, w, pv)


def _to_heads(t):
    b, l, _ = t.shape
    return t.reshape(b, l, N_HEADS, -1).transpose(0, 2, 1, 3)


def _merge_heads(t):
    b, h, l, d = t.shape
    return t.transpose(0, 2, 1, 3).reshape(b, l, h * d)


def _l2norm(t):
    return t * lax.rsqrt(jnp.sum(t * t, axis=-1, keepdims=True) + EPS)


def _head_norm(o, g, centre=False):
    if centre:
        o = o - jnp.mean(o, axis=-1, keepdims=True)
    o = o * lax.rsqrt(jnp.mean(o * o, axis=-1, keepdims=True) + EPS)
    return _merge_heads(o) * g.astype(F32)


def _rotate(t, pos):
    d = t.shape[-1]
    inv = ROPE_BASE ** (-jnp.arange(0, d, 2, dtype=F32) / d)
    ang = pos.astype(F32)[:, None] * inv[None, :]
    cos, sin = jnp.cos(ang), jnp.sin(ang)
    t1, t2 = t[..., : d // 2], t[..., d // 2:]
    return jnp.concatenate([t1 * cos - t2 * sin, t1 * sin + t2 * cos], axis=-1)


def _short_conv(x, w):
    pad = CONV_K // 2
    y = lax.conv_general_dilated(x, w[:, None, :].astype(x.dtype), (1,), [(pad, pad)],
                                 dimension_numbers=('NWC', 'WIO', 'NWC'), feature_group_count=x.shape[-1])
    return jax.nn.silu(y)


def _chunked(t):
    b, h, l = t.shape[:3]
    return jnp.moveaxis(t.reshape((b, h, l // CHUNK, CHUNK) + t.shape[3:]), 2, 0)


def _unchunked(t):
    n, b, h, c, d = t.shape
    return jnp.moveaxis(t, 0, 2).reshape(b, h, n * c, d)


def _chunk_masks():
    i = jnp.arange(CHUNK)
    return i[:, None] >= i[None, :], i[:, None] > i[None, :]


def _gdn_scan(inputs, state):
    q, k, v, log_a, beta = (_chunked(t) for t in inputs)
    incl, strict = _chunk_masks()
    g = jnp.cumsum(log_a, axis=-1)
    decay = jnp.exp(jnp.where(incl, g[..., :, None] - g[..., None, :], -jnp.inf))
    kb = k * beta[..., None]
    a = jnp.where(strict, jnp.einsum('nbhcd,nbhsd->nbhcs', kb, k) * decay, 0.0)
    lhs = jnp.eye(CHUNK, dtype=a.dtype) + a
    rhs = jnp.concatenate([v * beta[..., None], kb * jnp.exp(g)[..., None]], axis=-1)
    sol = lax.linalg.triangular_solve(lhs, rhs, left_side=True, lower=True)
    dv = v.shape[-1]
    u, w = sol[..., :dv], sol[..., dv:]
    qk = jnp.einsum('nbhcd,nbhsd->nbhcs', q, k) * decay
    q_in = q * jnp.exp(g)[..., None]
    k_out = k * jnp.exp(g[..., -1:] - g)[..., None]
    chunk_decay = jnp.exp(g[..., -1])

    def step(s, xs):
        q_c, k_c, u_c, w_c, qk_c, d_c = xs
        v_new = u_c - w_c @ s
        o = q_c @ s + qk_c @ v_new
        s = s * d_c[..., None, None] + jnp.swapaxes(k_c, -1, -2) @ v_new
        return s, o

    state, o = lax.scan(step, state, (q_in, k_out, u, w, qk, chunk_decay))
    return _unchunked(o), state


def _mlstm_scan(inputs, state):
    q, k, v, log_i, log_f = (_chunked(t) for t in inputs)
    incl, _ = _chunk_masks()
    b = jnp.cumsum(log_f, axis=-1)
    d_log = jnp.where(incl, b[..., :, None] - b[..., None, :] + log_i[..., None, :], -jnp.inf)
    m_intra = jnp.max(d_log, axis=-1)
    e_log = b[..., -1:] - b + log_i
    m_end = jnp.max(e_log, axis=-1)
    qk = jnp.einsum('nbhcd,nbhsd->nbhcs', q, k)

    def step(carry, xs):
        c_s, n_s, m_s = carry
        q_c, k_c, v_c, b_c, d_c, mi_c, e_c, me_c, qk_c = xs
        m_t = jnp.maximum(b_c + m_s[..., None], mi_c)
        inter = jnp.exp(b_c + m_s[..., None] - m_t)
        w_qk = jnp.exp(d_c - m_t[..., None]) * qk_c
        num = inter[..., None] * (q_c @ c_s) + w_qk @ v_c
        den = inter * jnp.einsum('bhcd,bhd->bhc', q_c, n_s) + jnp.sum(w_qk, axis=-1)
        h = num / jnp.maximum(jnp.abs(den), jnp.exp(-m_t))[..., None]
        m_new = jnp.maximum(b_c[..., -1] + m_s, me_c)
        carry_decay = jnp.exp(b_c[..., -1] + m_s - m_new)
        k_w = k_c * jnp.exp(e_c - m_new[..., None])[..., None]
        c_s = carry_decay[..., None, None] * c_s + jnp.swapaxes(k_w, -1, -2) @ v_c
        n_s = carry_decay[..., None] * n_s + jnp.sum(k_w, axis=-2)
        return (c_s, n_s, m_new), h

    state, h = lax.scan(step, state, (q, k, v, b, d_log, m_intra, e_log, m_end, qk))
    return _unchunked(h), state


def _retention_scan(log_gamma, inputs, state):
    q, k, v = (_chunked(t) for t in inputs)
    incl, _ = _chunk_masks()
    pos = jnp.arange(CHUNK, dtype=F32)
    lg = log_gamma[:, None]
    decay = jnp.exp(jnp.where(incl, lg[..., None] * (pos[:, None] - pos[None, :]), -jnp.inf))
    o_intra = (jnp.einsum('nbhcd,nbhsd->nbhcs', q, k) * decay) @ v
    q_in = q * jnp.exp(lg * (pos + 1.0))[..., None]
    k_out = k * jnp.exp(lg * (CHUNK - 1.0 - pos))[..., None]
    chunk_decay = jnp.exp(log_gamma * CHUNK)[:, None, None]

    def step(s, xs):
        q_c, k_c, v_c = xs
        o = q_c @ s
        s = chunk_decay * s + jnp.swapaxes(k_c, -1, -2) @ v_c
        return s, o

    state, o_inter = lax.scan(step, state, (q_in, k_out, v))
    return _unchunked(o_intra + o_inter), state


def _flip_time(ts):
    return tuple(jnp.flip(t, axis=2) for t in ts)


def _bidirectional_scan(scan_fn, ctx_f, ctx_b, lat_f, lat_b, state0):
    oc_f, st_f = scan_fn(ctx_f, state0)
    ol_f, _ = scan_fn(lat_f, st_f)
    oc_b, st_b = scan_fn(_flip_time(ctx_b), state0)
    ol_b, _ = scan_fn(_flip_time(lat_b), st_b)
    return oc_f + jnp.flip(oc_b, axis=2), ol_f + jnp.flip(ol_b, axis=2)


def _gdn_branch(zc, sc, zl, sl, conv_w, a_log, dt_bias, norm_g):
    def prep(z, small):
        qkv = _short_conv(z[None, :, OFF_GDN_QKV:OFF_GDN_QKV + 1536].astype(F32), conv_w)
        q = _l2norm(_to_heads(qkv[..., :512])) * GDN_DK ** -0.5
        k = _l2norm(_to_heads(qkv[..., 512:1024]))
        v = _to_heads(qkv[..., 1024:])
        l = z.shape[0]
        a = small[None, :, 0:8].reshape(1, l, 2, N_HEADS)
        bt = small[None, :, 8:16].reshape(1, l, 2, N_HEADS)
        log_a = (-jnp.exp(a_log) * jax.nn.softplus(a + dt_bias)).transpose(2, 0, 3, 1)
        beta = jax.nn.sigmoid(bt).transpose(2, 0, 3, 1)
        return (q, k, v, log_a[0], beta[0]), (q, k, v, log_a[1], beta[1])

    cf, cb = prep(zc, sc)
    lf, lb = prep(zl, sl)
    state0 = jnp.zeros((1, N_HEADS, GDN_DK, HEAD_V), F32)
    oc, ol = _bidirectional_scan(_gdn_scan, cf, cb, lf, lb, state0)

    def post(o, z):
        zz = z[None, :, OFF_GDN_Z:OFF_GDN_Z + 512].astype(F32)
        return (_head_norm(o, norm_g) * jax.nn.silu(zz))[0].astype(BF16)

    return post(oc, zc), post(ol, zl)


def _mlstm_branch(zc, sc, zl, sl, gate_b, norm_g):
    def prep(z, small):
        q = _to_heads(z[None, :, OFF_MLSTM_Q:OFF_MLSTM_Q + 256].astype(F32)) * MLSTM_DK ** -0.5
        k = _to_heads(z[None, :, OFF_MLSTM_K:OFF_MLSTM_K + 256].astype(F32))
        v = _to_heads(z[None, :, OFF_MLSTM_V:OFF_MLSTM_V + 512].astype(F32))
        l = z.shape[0]
        ig = small[None, :, 16:24].reshape(1, l, 2, N_HEADS)
        fg = small[None, :, 24:32].reshape(1, l, 2, N_HEADS)
        log_i = (ig + gate_b[0]).transpose(2, 0, 3, 1)
        log_f = jax.nn.log_sigmoid(fg + gate_b[1]).transpose(2, 0, 3, 1)
        return (q, k, v, log_i[0], log_f[0]), (q, k, v, log_i[1], log_f[1])

    cf, cb = prep(zc, sc)
    lf, lb = prep(zl, sl)
    state0 = (jnp.zeros((1, N_HEADS, MLSTM_DK, HEAD_V), F32), jnp.zeros((1, N_HEADS, MLSTM_DK), F32),
              jnp.zeros((1, N_HEADS), F32))
    oc, ol = _bidirectional_scan(_mlstm_scan, cf, cb, lf, lb, state0)

    def post(o, z):
        og = z[None, :, OFF_MLSTM_O:OFF_MLSTM_O + 512].astype(F32)
        return (_head_norm(o, norm_g) * jax.nn.sigmoid(og))[0].astype(BF16)

    return post(oc, zc), post(ol, zl)


def _retention_branch(zc, zl, norm_g):
    log_gamma = jnp.log(1.0 - 2.0 ** (-5.0 - jnp.arange(N_HEADS, dtype=F32)))

    def prep(z, rotary):
        q = _to_heads(z[None, :, OFF_RET_Q:OFF_RET_Q + 256].astype(F32))
        k = _to_heads(z[None, :, OFF_RET_K:OFF_RET_K + 256].astype(F32)) * RET_DK ** -0.5
        if rotary:
            pos = jnp.arange(q.shape[2])
            q, k = _rotate(q, pos), _rotate(k, pos)
        return (q, k, _to_heads(z[None, :, OFF_RET_V:OFF_RET_V + 512].astype(F32)))

    xc = prep(zc, False)
    xl = prep(zl, True)
    state0 = jnp.zeros((1, N_HEADS, RET_DK, HEAD_V), F32)
    oc, ol = _bidirectional_scan(functools.partial(_retention_scan, log_gamma), xc, xc, xl, xl, state0)

    def post(o, z):
        g = z[None, :, OFF_RET_G:OFF_RET_G + 512].astype(F32)
        return (_head_norm(o, norm_g, centre=True) * jax.nn.silu(g))[0].astype(BF16)

    return post(oc, zc), post(ol, zl)


def _rope_tables(n_lat):
    q = MLA_ROPE // 4
    inv = ROPE_BASE ** (-jnp.arange(0, 2 * q, 2, dtype=F32) / (2 * q))
    t = jnp.arange(n_lat)
    row = (t // GRID_W).astype(F32)[:, None] * inv[None, :]
    col = (t % GRID_W).astype(F32)[:, None] * inv[None, :]
    cos = jnp.concatenate([jnp.cos(row), jnp.cos(row), jnp.cos(col), jnp.cos(col)], axis=-1)
    sin = jnp.concatenate([-jnp.sin(row), jnp.sin(row), -jnp.sin(col), jnp.sin(col)], axis=-1)
    return cos, sin


def _pad_rows(v, n=8):
    rows = [jnp.reshape(r, (1, -1)).astype(F32) for r in v]
    d = rows[0].shape[1]
    return jnp.concatenate(rows + [jnp.zeros((n - len(rows), d), F32)], axis=0)


def _mla_weights(w_uq, w_ukv):
    wq = w_uq.reshape(MLA_Q_RANK, N_HEADS, MLA_QK)
    rope = wq[:, :, MLA_NOPE:]
    swapped = rope[:, :, _rope_swap(np.arange(MLA_ROPE))]
    wq_ext = jnp.concatenate([wq[:, :, :MLA_NOPE], rope, swapped], axis=-1).reshape(MLA_Q_RANK, N_HEADS * 256)
    return wq_ext.astype(BF16), w_ukv.astype(BF16)


def _moe(x_l, pv2, g2, w_router, b_router, w_gu, w_down, final_g):
    n = x_l.shape[0]
    w_pad = jnp.concatenate([w_router, jnp.zeros((D_MODEL, 128 - N_EXPERTS), F32)], axis=1)
    b_pad = jnp.concatenate([b_router, jnp.full((128 - N_EXPERTS,), -1e30, F32)]).reshape(1, 128)
    h2, logits = moe_router(x_l, pv2, w_pad, b_pad)
    top_logit, top_e = lax.top_k(logits[:, :N_EXPERTS], TOP_K)
    weight = jax.nn.softmax(top_logit, axis=-1)
    n_assign = n * TOP_K
    flat_e = top_e.reshape(-1)
    onehot = (flat_e[:, None] == jnp.arange(N_EXPERTS)[None, :]).astype(jnp.int32)
    csum = jnp.cumsum(onehot, axis=0)
    counts = csum[-1]
    rank = jnp.take_along_axis(csum, flat_e[:, None], axis=1)[:, 0] - 1
    padded = (counts + MOE_ROWS - 1) // MOE_ROWS * MOE_ROWS
    pad_end = jnp.cumsum(padded)
    dest = (pad_end - padded)[flat_e] + rank
    nb = n_assign // MOE_ROWS + N_EXPERTS
    cap = nb * MOE_ROWS
    slot_tok = jnp.zeros((cap,), jnp.int32).at[dest].set(jnp.arange(n_assign, dtype=jnp.int32) // TOP_K)
    block_e = jnp.minimum(jnp.searchsorted(pad_end, jnp.arange(nb) * MOE_ROWS, side='right'),
                          N_EXPERTS - 1).astype(jnp.int32)
    n_used = (pad_end[-1] // MOE_ROWS).astype(jnp.int32).reshape(1)
    xb = h2[slot_tok]
    yb = expert_ffn(block_e, n_used, xb, w_gu, w_down)
    d2 = dest.reshape(n, TOP_K)
    w128 = jnp.concatenate([weight, jnp.zeros((n, 126), F32)], axis=1)
    return final_combine(x_l, yb[d2[:, 0]], yb[d2[:, 1]], w128, _pad_rows([g2, final_g]))


def kernel(x, c, ctx, c_ctx, w_mod, b_mod, norm1_g, norm2_g, w_in, gdn_conv_w, gdn_a_log, gdn_dt_bias, gdn_norm_g, mla_q_norm_g, mla_kv_norm_g, mla_w_uq, mla_w_ukv, mlstm_gate_b, mlstm_norm_g, ret_norm_g, w_branch, w_out, ffn_w_in, ffn_w_down, moe_w_router, moe_b_router, moe_w_in, moe_w_down, final_norm_g):
    n_lat = x.shape[1]
    n_ctx = ctx.shape[1]
    x_l, x_c = x[0], ctx[0]
    cond = _pad_rows([c_ctx, c[0]])
    cos_l, sin_l = _rope_tables(n_lat)
    cos_c, sin_c = jnp.ones((n_ctx, MLA_ROPE), F32), jnp.zeros((n_ctx, MLA_ROPE), F32)
    out = None
    for li in range(DEPTH):
        last = li == DEPTH - 1
        mod = modulation_vectors(cond, w_mod[li], b_mod[li])
        csh1, csc1, cg1, csh2, csc2, cg2 = jnp.split(mod[0], 6)
        sh1, sc1, g1, sh2, sc2, g2 = jnp.split(mod[1], 6)
        w_main = w_in[li][:, _MAIN_PERM].astype(BF16)
        w_small = w_in[li][:, _SMALL_PERM].astype(BF16)
        pv_l = _pad_rows([norm1_g[li], 1 + sc1, sh1])
        pv_c = _pad_rows([norm1_g[li], 1 + csc1, csh1])
        zl = norm_proj(x_l, pv_l, w_main, 768, BF16)
        zc = norm_proj(x_c, pv_c, w_main, 768, BF16)
        sl = norm_proj(x_l, pv_l, w_small, 128, F32)
        sc = norm_proj(x_c, pv_c, w_small, 128, F32)

        a_c, a_l = _gdn_branch(zc, sc, zl, sl, gdn_conv_w[li], gdn_a_log[li], gdn_dt_bias[li], gdn_norm_g[li])
        c_c, c_l = _mlstm_branch(zc, sc, zl, sl, mlstm_gate_b[li], mlstm_norm_g[li])
        d_c, d_l = _retention_branch(zc, zl, ret_norm_g[li])

        wq_ext, wkv = _mla_weights(mla_w_uq[li], mla_w_ukv[li])
        gq, gkv = mla_q_norm_g[li].reshape(1, -1), mla_kv_norm_g[li].reshape(1, -1)
        ql, kl, vl = mla_project(zl, cos_l, sin_l, gq, gkv, wq_ext, wkv)
        qc, kc, vc = mla_project(zc, cos_c, sin_c, gq, gkv, wq_ext, wkv)
        b_l = attention(ql, [(kl, vl), (kc, vc)])

        wb = w_branch[li].astype(BF16)
        wo = w_out[li].astype(BF16)
        x_l = merge_branches(x_l, zl, (a_l, b_l, c_l, d_l), wb, wo, _pad_rows([g1]))
        if not last:
            b_c = attention(qc, [(kc, vc)])
            x_c = merge_branches(x_c, zc, (a_c, b_c, c_c, d_c), wb, wo, _pad_rows([cg1]))

        if li % 2 == 0:
            w_gu = ffn_w_in[li // 2].astype(BF16)
            w_dn = ffn_w_down[li // 2].astype(BF16)
            x_l = dense_ffn(x_l, _pad_rows([norm2_g[li], 1 + sc2, sh2, g2]), w_gu, w_dn)
            if not last:
                x_c = dense_ffn(x_c, _pad_rows([norm2_g[li], 1 + csc2, csh2, cg2]), w_gu, w_dn)
            if last:
                out = final_combine(x_l, jnp.zeros_like(x_l), jnp.zeros_like(x_l),
                                    jnp.zeros((n_lat, 128), F32), _pad_rows([jnp.zeros_like(g2), final_norm_g]))
        else:
            assert last
            out = _moe(x_l, _pad_rows([norm2_g[li], 1 + sc2, sh2]), g2, moe_w_router[li // 2],
                       moe_b_router[li // 2], moe_w_in[li // 2], moe_w_down[li // 2], final_norm_g)
    return out[None]
```

```python
import functools
import math

import numpy as np
import jax
import jax.numpy as jnp
from jax import lax
from jax.experimental import pallas as pl
from jax.experimental.pallas import tpu as pltpu

F32 = jnp.float32
BF16 = jnp.bfloat16

D_MODEL = 1024
DEPTH = 2
GRID_W = 64
N_BRANCH = 4
N_HEADS = 4
HEAD_V = 128
BRANCH_W = N_HEADS * HEAD_V
GDN_DK = 128
CONV_K = 5
MLA_Q_RANK = 384
MLA_KV_RANK = 256
MLA_NOPE = 128
MLA_ROPE = 64
MLA_QK = MLA_NOPE + MLA_ROPE
MLA_QK_PAD = 256
MLSTM_DK = 64
RET_DK = 64
ROPE_BASE = 10000.0
D_FF = 3584
N_EXPERTS = 8
TOP_K = 2
EPS = 1e-6

IN_WIDTHS = (
    N_BRANCH * D_MODEL,
    N_HEADS * GDN_DK, N_HEADS * GDN_DK, BRANCH_W, BRANCH_W, 2 * N_HEADS, 2 * N_HEADS,
    MLA_Q_RANK, MLA_KV_RANK, MLA_ROPE,
    N_HEADS * MLSTM_DK, N_HEADS * MLSTM_DK, BRANCH_W, BRANCH_W, 2 * N_HEADS, 2 * N_HEADS,
    N_HEADS * RET_DK, N_HEADS * RET_DK, BRANCH_W, BRANCH_W,
)
_IN_OFF = [0] + [int(o) for o in np.cumsum(IN_WIDTHS)]

VMEM_LIMIT = 48 * 1024 * 1024
MOE_ROWS = 1024


def _cols(group):
    return np.arange(_IN_OFF[group], _IN_OFF[group + 1])


def _rope_swap(cols):
    q = MLA_ROPE // 4
    return np.concatenate([cols[q:2 * q], cols[:q], cols[3 * q:], cols[2 * q:3 * q]])


_MAIN_PERM = np.concatenate([
    _cols(1), _cols(2), _cols(3),
    _cols(4),
    _cols(0),
    _cols(12), _cols(13),
    _cols(10), _cols(11),
    _cols(18), _cols(19),
    _cols(16), _cols(17),
    _cols(7), _cols(8), _cols(9), _rope_swap(_cols(9)),
])
N_MAIN = int(_MAIN_PERM.shape[0])
_SMALL_PERM = np.concatenate([_cols(5), _cols(6), _cols(14), _cols(15), np.zeros(96, np.int64)])
A_LANE, B_LANE, I_LANE, F_LANE = 0, 8, 16, 24
OFF_GDN_QKV, OFF_GDN_Z, OFF_GATE = 0, 1536, 2048
OFF_MLSTM_V, OFF_MLSTM_O, OFF_MLSTM_Q, OFF_MLSTM_K = 6144, 6656, 7168, 7424
OFF_RET_V, OFF_RET_G, OFF_RET_Q, OFF_RET_K = 7680, 8192, 8704, 8960
OFF_MLA = 9216
MLA_IN_W = 768


def _cparams(sem):
    return pltpu.CompilerParams(dimension_semantics=sem, vmem_limit_bytes=VMEM_LIMIT)


def _rms(x):
    return x * lax.rsqrt(jnp.mean(x * x, axis=-1, keepdims=True) + EPS)


def _dot(a, b):
    return jnp.dot(a, b, preferred_element_type=F32)


def _dot_nt(a, b):
    return lax.dot_general(a, b, (((1,), (1,)), ((), ())), preferred_element_type=F32)


def _mod_kernel(c_ref, w_ref, b_ref, o_ref):
    c = c_ref[...]
    s = c * jax.nn.sigmoid(c)
    o_ref[...] = jnp.dot(s, w_ref[...], preferred_element_type=F32) + b_ref[...]


def modulation_vectors(cond, w_mod, b_mod):
    n = w_mod.shape[1]
    tn = 1536
    return pl.pallas_call(
        _mod_kernel,
        out_shape=jax.ShapeDtypeStruct((8, n), F32),
        grid=(n // tn,),
        in_specs=[pl.BlockSpec((8, D_MODEL), lambda j: (0, 0)),
                  pl.BlockSpec((D_MODEL, tn), lambda j: (0, j)),
                  pl.BlockSpec((1, tn), lambda j: (0, j))],
        out_specs=pl.BlockSpec((8, tn), lambda j: (0, j)),
        compiler_params=_cparams(("arbitrary",)),
        name="modulation",
    )(cond, w_mod, b_mod.reshape(1, n))


def _norm_proj_kernel(x_ref, pv_ref, w_ref, o_ref, h_ref):
    @pl.when(pl.program_id(1) == 0)
    def _():
        pv = pv_ref[...]
        h_ref[...] = (_rms(x_ref[...]) * pv[0:1] * pv[1:2] + pv[2:3]).astype(BF16)

    o_ref[...] = _dot(h_ref[...], w_ref[...]).astype(o_ref.dtype)


def norm_proj(x, pv, w, tn, out_dtype):
    m, d = x.shape
    n = w.shape[1]
    tm = min(m, 1024)
    return pl.pallas_call(
        _norm_proj_kernel,
        out_shape=jax.ShapeDtypeStruct((m, n), out_dtype),
        grid=(m // tm, n // tn),
        in_specs=[pl.BlockSpec((tm, d), lambda i, j: (i, 0)),
                  pl.BlockSpec((8, d), lambda i, j: (0, 0)),
                  pl.BlockSpec((d, tn), lambda i, j: (0, j))],
        out_specs=pl.BlockSpec((tm, tn), lambda i, j: (i, j)),
        scratch_shapes=[pltpu.VMEM((tm, d), BF16)],
        compiler_params=_cparams(("parallel", "arbitrary")),
        name="norm_proj",
    )(x, pv, w)


def _mla_proj_kernel(z_ref, cos_ref, sin_ref, gq_ref, gkv_ref, wq_ref, wkv_ref, q_ref, k_ref, v_ref):
    z = z_ref[...].astype(F32)
    tm = z.shape[0]
    cq = z[:, :MLA_Q_RANK]
    ckv = z[:, MLA_Q_RANK:MLA_Q_RANK + MLA_KV_RANK]
    kr = z[:, MLA_Q_RANK + MLA_KV_RANK:]
    cos = cos_ref[...]
    sin = sin_ref[...]
    qn = (_rms(cq) * gq_ref[...]).astype(BF16)
    kvn = (_rms(ckv) * gkv_ref[...]).astype(BF16)
    qf = _dot(qn, wq_ref[...]) * (MLA_QK ** -0.5)
    kvf = _dot(kvn, wkv_ref[...])
    kr_rot = kr[:, :MLA_ROPE] * cos + kr[:, MLA_ROPE:] * sin
    pad = jnp.zeros((tm, MLA_QK_PAD - MLA_QK), F32)
    for h in range(N_HEADS):
        b = h * 256
        q_rot = qf[:, b + 128:b + 192] * cos + qf[:, b + 192:b + 256] * sin
        q_ref[h] = jnp.concatenate([qf[:, b:b + 128], q_rot, pad], axis=-1).astype(BF16)
        k_ref[h] = jnp.concatenate([kvf[:, b:b + 128], kr_rot, pad], axis=-1).astype(BF16)
        v_ref[h] = kvf[:, b + 128:b + 256].astype(BF16)


def mla_project(zmain, cos, sin, gq, gkv, wq_ext, wkv):
    m = zmain.shape[0]
    tm = min(m, 1024)
    full = lambda shape: pl.BlockSpec(shape, lambda i: tuple(0 for _ in shape))
    return pl.pallas_call(
        _mla_proj_kernel,
        out_shape=(jax.ShapeDtypeStruct((N_HEADS, m, MLA_QK_PAD), BF16),
                   jax.ShapeDtypeStruct((N_HEADS, m, MLA_QK_PAD), BF16),
                   jax.ShapeDtypeStruct((N_HEADS, m, HEAD_V), BF16)),
        grid=(m // tm,),
        in_specs=[pl.BlockSpec((tm, MLA_IN_W), lambda i: (i, OFF_MLA // MLA_IN_W)),
                  pl.BlockSpec((tm, MLA_ROPE), lambda i: (i, 0)),
                  pl.BlockSpec((tm, MLA_ROPE), lambda i: (i, 0)),
                  full((1, MLA_Q_RANK)), full((1, MLA_KV_RANK)),
                  full((MLA_Q_RANK, N_HEADS * 256)), full((MLA_KV_RANK, N_HEADS * 256))],
        out_specs=(pl.BlockSpec((N_HEADS, tm, MLA_QK_PAD), lambda i: (0, i, 0)),
                   pl.BlockSpec((N_HEADS, tm, MLA_QK_PAD), lambda i: (0, i, 0)),
                   pl.BlockSpec((N_HEADS, tm, HEAD_V), lambda i: (0, i, 0))),
        compiler_params=_cparams(("parallel",)),
        name="mla_project",
    )(zmain, cos, sin, gq, gkv, wq_ext, wkv)


def _attn_kernel(*refs, segs):
    q_ref, o_ref = refs[0], refs[-1]
    q = q_ref[0]
    tq = q.shape[0]
    carry = (jnp.full((tq, 1), -1e30, F32), jnp.zeros((tq, 1), F32), jnp.zeros((tq, HEAD_V), F32))
    for si, (tk, nk) in enumerate(segs):
        k_ref, v_ref = refs[1 + 2 * si], refs[2 + 2 * si]

        def body(c, carry, k_ref=k_ref, v_ref=v_ref, tk=tk):
            m, l, acc = carry
            start = pl.multiple_of(c * tk, tk)
            s = _dot_nt(q, k_ref[0, pl.ds(start, tk), :])
            m_new = jnp.maximum(m, jnp.max(s, axis=-1, keepdims=True))
            p = jnp.exp(s - m_new)
            alpha = jnp.exp(m - m_new)
            l = alpha * l + jnp.sum(p, axis=-1, keepdims=True)
            acc = alpha * acc + _dot(p.astype(BF16), v_ref[0, pl.ds(start, tk), :])
            return m_new, l, acc

        carry = lax.fori_loop(0, nk, body, carry)
    _, l, acc = carry
    o_ref[...] = (acc / l).astype(o_ref.dtype)


def attention(q, kvs):
    _, lq, _ = q.shape
    tq = min(lq, 512)
    segs, args, specs = [], [], []
    for k, v in kvs:
        lk = k.shape[1]
        tk = min(lk, 1024)
        segs.append((tk, lk // tk))
        args += [k, v]
        specs += [pl.BlockSpec((1, lk, MLA_QK_PAD), lambda h, i: (h, 0, 0)),
                  pl.BlockSpec((1, lk, HEAD_V), lambda h, i: (h, 0, 0))]
    return pl.pallas_call(
        functools.partial(_attn_kernel, segs=tuple(segs)),
        out_shape=jax.ShapeDtypeStruct((lq, N_HEADS * HEAD_V), BF16),
        grid=(N_HEADS, lq // tq),
        in_specs=[pl.BlockSpec((1, tq, MLA_QK_PAD), lambda h, i: (h, i, 0))] + specs,
        out_specs=pl.BlockSpec((tq, HEAD_V), lambda h, i: (i, h)),
        compiler_params=_cparams(("parallel", "arbitrary")),
        name="attention",
    )(q, *args)


def _merge_kernel(x_ref, g0_ref, g1_ref, g2_ref, g3_ref, a_ref, b_ref, c_ref, d_ref, wb_ref, wo_ref, g_ref, o_ref):
    s = None
    for n, (br, gate_ref) in enumerate(zip((a_ref, b_ref, c_ref, d_ref), (g0_ref, g1_ref, g2_ref, g3_ref))):
        proj = _dot(br[...], wb_ref[n])
        gate = jax.nn.sigmoid(gate_ref[...].astype(F32))
        s = gate * proj if s is None else s + gate * proj
    m = _dot(s.astype(BF16), wo_ref[...])
    o_ref[...] = x_ref[...] + g_ref[0:1] * m


def merge_branches(x, zmain, branches, w_branch, w_out, gvec):
    m = x.shape[0]
    tm = min(m, 512)
    row = lambda w: pl.BlockSpec((tm, w), lambda i: (i, 0))
    return pl.pallas_call(
        _merge_kernel,
        out_shape=jax.ShapeDtypeStruct((m, D_MODEL), F32),
        grid=(m // tm,),
        in_specs=[row(D_MODEL)]
                 + [pl.BlockSpec((tm, D_MODEL), lambda i, n=n: (i, OFF_GATE // D_MODEL + n)) for n in range(N_BRANCH)]
                 + [row(BRANCH_W)] * 4
                 + [pl.BlockSpec((N_BRANCH, BRANCH_W, D_MODEL), lambda i: (0, 0, 0)),
                    pl.BlockSpec((D_MODEL, D_MODEL), lambda i: (0, 0)),
                    pl.BlockSpec((8, D_MODEL), lambda i: (0, 0))],
        out_specs=row(D_MODEL),
        compiler_params=_cparams(("parallel",)),
        name="merge_branches",
    )(x, zmain, zmain, zmain, zmain, *branches, w_branch, w_out, gvec)


def _ffn_kernel(x_ref, pv_ref, wg_ref, wu_ref, wd_ref, o_ref, h_ref, acc_ref):
    f = pl.program_id(1)

    @pl.when(f == 0)
    def _():
        pv = pv_ref[...]
        h_ref[...] = (_rms(x_ref[...]) * pv[0:1] * pv[1:2] + pv[2:3]).astype(BF16)
        acc_ref[...] = jnp.zeros_like(acc_ref)

    h = h_ref[...]
    g = _dot(h, wg_ref[...])
    u = _dot(h, wu_ref[...])
    act = (g * jax.nn.sigmoid(g) * u).astype(BF16)
    acc_ref[...] += _dot(act, wd_ref[...])

    @pl.when(f == pl.num_programs(1) - 1)
    def _():
        o_ref[...] = x_ref[...] + pv_ref[3:4] * acc_ref[...]


def dense_ffn(x, pv, w_gu, w_down):
    m = x.shape[0]
    tm = min(m, 1024)
    tf = 512
    nf = D_FF // tf
    return pl.pallas_call(
        _ffn_kernel,
        out_shape=jax.ShapeDtypeStruct((m, D_MODEL), F32),
        grid=(m // tm, nf),
        in_specs=[pl.BlockSpec((tm, D_MODEL), lambda i, f: (i, 0)),
                  pl.BlockSpec((8, D_MODEL), lambda i, f: (0, 0)),
                  pl.BlockSpec((D_MODEL, tf), lambda i, f: (0, f)),
                  pl.BlockSpec((D_MODEL, tf), lambda i, f: (0, f + nf)),
                  pl.BlockSpec((tf, D_MODEL), lambda i, f: (f, 0))],
        out_specs=pl.BlockSpec((tm, D_MODEL), lambda i, f: (i, 0)),
        scratch_shapes=[pltpu.VMEM((tm, D_MODEL), BF16), pltpu.VMEM((tm, D_MODEL), F32)],
        compiler_params=_cparams(("parallel", "arbitrary")),
        name="dense_ffn",
    )(x, pv, w_gu, w_gu, w_down)


def _router_kernel(x_ref, pv_ref, w_ref, b_ref, h_ref, logit_ref):
    pv = pv_ref[...]
    h = _rms(x_ref[...]) * pv[0:1] * pv[1:2] + pv[2:3]
    h_hi = h.astype(BF16)
    h_lo = (h - h_hi.astype(F32)).astype(BF16)
    w = w_ref[...]
    w_hi = w.astype(BF16)
    w_lo = (w - w_hi.astype(F32)).astype(BF16)
    logits = _dot(h_hi, w_hi) + _dot(h_hi, w_lo) + _dot(h_lo, w_hi)
    h_ref[...] = h_hi
    logit_ref[...] = logits + b_ref[...]


def moe_router(x, pv, w_router_pad, b_router_pad):
    m = x.shape[0]
    tm = min(m, 1024)
    return pl.pallas_call(
        _router_kernel,
        out_shape=(jax.ShapeDtypeStruct((m, D_MODEL), BF16), jax.ShapeDtypeStruct((m, 128), F32)),
        grid=(m // tm,),
        in_specs=[pl.BlockSpec((tm, D_MODEL), lambda i: (i, 0)),
                  pl.BlockSpec((8, D_MODEL), lambda i: (0, 0)),
                  pl.BlockSpec((D_MODEL, 128), lambda i: (0, 0)),
                  pl.BlockSpec((1, 128), lambda i: (0, 0))],
        out_specs=(pl.BlockSpec((tm, D_MODEL), lambda i: (i, 0)), pl.BlockSpec((tm, 128), lambda i: (i, 0))),
        compiler_params=_cparams(("parallel",)),
        name="moe_router",
    )(x, pv, w_router_pad, b_router_pad)


def _expert_kernel(be_ref, nb_ref, x_ref, wg_ref, wu_ref, wd_ref, o_ref, acc_ref):
    b = pl.program_id(0)
    f = pl.program_id(1)

    @pl.when(b < nb_ref[0])
    def _():
        @pl.when(f == 0)
        def _():
            acc_ref[...] = jnp.zeros_like(acc_ref)

        x = x_ref[...]
        g = _dot(x, wg_ref[0].astype(BF16))
        u = _dot(x, wu_ref[0].astype(BF16))
        act = (g * jax.nn.sigmoid(g) * u).astype(BF16)
        acc_ref[...] += _dot(act, wd_ref[0].astype(BF16))

        @pl.when(f == pl.num_programs(1) - 1)
        def _():
            o_ref[...] = acc_ref[...]


def expert_ffn(block_e, n_used, xb, w_gu, w_down):
    cap = xb.shape[0]
    nb = cap // MOE_ROWS
    tf = 512
    nf = D_FF // tf

    def live(b, nbr):
        return jnp.minimum(b, nbr[0] - 1)

    def fsel(b, f, nbr):
        return jnp.where(b < nbr[0], f, nf - 1)

    grid_spec = pltpu.PrefetchScalarGridSpec(
        num_scalar_prefetch=2,
        grid=(nb, nf),
        in_specs=[pl.BlockSpec((MOE_ROWS, D_MODEL), lambda b, f, be, nbr: (live(b, nbr), 0)),
                  pl.BlockSpec((1, D_MODEL, tf), lambda b, f, be, nbr: (be[live(b, nbr)], 0, fsel(b, f, nbr))),
                  pl.BlockSpec((1, D_MODEL, tf), lambda b, f, be, nbr: (be[live(b, nbr)], 0, fsel(b, f, nbr) + nf)),
                  pl.BlockSpec((1, tf, D_MODEL), lambda b, f, be, nbr: (be[live(b, nbr)], fsel(b, f, nbr), 0))],
        out_specs=pl.BlockSpec((MOE_ROWS, D_MODEL), lambda b, f, be, nbr: (live(b, nbr), 0)),
        scratch_shapes=[pltpu.VMEM((MOE_ROWS, D_MODEL), F32)],
    )
    return pl.pallas_call(
        _expert_kernel,
        out_shape=jax.ShapeDtypeStruct((cap, D_MODEL), F32),
        grid_spec=grid_spec,
        compiler_params=_cparams(("arbitrary", "arbitrary")),
        name="expert_ffn",
    )(block_e, n_used, xb, w_gu, w_gu, w_down)


def _final_kernel(x_ref, y0_ref, y1_ref, w_ref, pv_ref, o_ref):
    w = w_ref[...]
    f = w[:, 0:1] * y0_ref[...] + w[:, 1:2] * y1_ref[...]
    x = x_ref[...] + pv_ref[0:1] * f
    o_ref[...] = _rms(x) * pv_ref[1:2]


def final_combine(x, y0, y1, w, pv):
    m = x.shape[0]
    tm = min(m, 1024)
    row = lambda wd: pl.BlockSpec((tm, wd), lambda i: (i, 0))
    return pl.pallas_call(
        _final_kernel,
        out_shape=jax.ShapeDtypeStruct((m, D_MODEL), F32),
        grid=(m // tm,),
        in_specs=[row(D_MODEL), row(D_MODEL), row(D_MODEL), row(128),
                  pl.BlockSpec((8, D_MODEL), lambda i: (0, 0))],
        out_specs=row(D_MODEL),
        compiler_params=_cparams(("parallel",)),
        name="final_combine",
    )(x, y0, y1, w, pv)


CH = 128
NEG = -1e30


def _masks(reverse):
    r = lax.broadcasted_iota(jnp.int32, (CH, CH), 0)
    c = lax.broadcasted_iota(jnp.int32, (CH, CH), 1)
    return (r <= c, r < c) if reverse else (r >= c, r > c)


def _cumsum_time(incl, x):
    m = jnp.where(incl, 1.0, 0.0).astype(BF16)
    hi = x.astype(BF16)
    lo = (x - hi.astype(F32)).astype(BF16)
    return _dot(m, hi) + _dot(m, lo)


def _dir_specs(nblk, width, col_block):
    return [pl.BlockSpec((CH, width), lambda n: (n, col_block)),
            pl.BlockSpec((CH, width), lambda n: (nblk - 1 - n, col_block))]


def _whole(a):
    return pl.BlockSpec(a.shape, lambda n: tuple(0 for _ in a.shape))


def _ret_consts():
    log_gamma = np.log(1.0 - 2.0 ** (-5.0 - np.arange(N_HEADS, dtype=np.float64)))
    pos = np.arange(CH, dtype=np.float64)
    diff = pos[:, None] - pos[None, :]
    dec_f = np.where(diff >= 0, np.exp(log_gamma[:, None, None] * diff), 0.0)
    dec = np.stack([dec_f, np.transpose(dec_f, (0, 2, 1))])
    qs_f = np.exp(log_gamma[None, :] * (pos[:, None] + 1.0))
    ks_f = np.exp(log_gamma[None, :] * (CH - 1.0 - pos[:, None]))
    qs_b = np.exp(log_gamma[None, :] * (CH - pos[:, None]))
    ks_b = np.exp(log_gamma[None, :] * pos[:, None])
    rep = lambda a: np.repeat(a, RET_DK, axis=1)
    qs = np.stack([rep(qs_f), rep(qs_b)])
    ks = np.stack([rep(ks_f), rep(ks_b)])
    chunk_decay = [float(np.exp(lg * CH)) for lg in log_gamma]
    return (jnp.asarray(dec, F32), jnp.asarray(qs, F32), jnp.asarray(ks, F32)), chunk_decay


def _ret_kernel(*refs, rotary, chunk_decay):
    if rotary:
        (qf, qb, kf, kb, vf, vb, cosf, cosb, sinf, sinb, dec_ref, qs_ref, ks_ref, s0_ref,
         of_ref, ob_ref, s_ref) = refs
        tabs = ((cosf, sinf), (cosb, sinb))
    else:
        qf, qb, kf, kb, vf, vb, dec_ref, qs_ref, ks_ref, s0_ref, of_ref, ob_ref, s_ref = refs
        tabs = (None, None)

    @pl.when(pl.program_id(0) == 0)
    def _():
        s_ref[...] = s0_ref[...]

    lane = lax.broadcasted_iota(jnp.int32, (CH, N_HEADS * RET_DK), 1)
    first_half = (lane & (RET_DK - 1)) < RET_DK // 2

    for d, (q_ref, k_ref, v_ref, o_ref) in enumerate(((qf, kf, vf, of_ref), (qb, kb, vb, ob_ref))):
        q = q_ref[...].astype(F32)
        k = k_ref[...].astype(F32) * RET_DK ** -0.5
        if rotary:
            cos, sin = tabs[d][0][...], tabs[d][1][...]

            def rot(x):
                swapped = jnp.where(first_half, pltpu.roll(x, N_HEADS * RET_DK - RET_DK // 2, 1),
                                    pltpu.roll(x, RET_DK // 2, 1))
                return x * cos + swapped * sin

            q, k = rot(q), rot(k)
        qb16, kb16 = q.astype(BF16), k.astype(BF16)
        q_in = (q * qs_ref[d]).astype(BF16)
        k_out = k * ks_ref[d]
        k_t = [k_out[:, :128].T, k_out[:, 128:].T]
        for h in range(N_HEADS):
            sl = slice(h * RET_DK, (h + 1) * RET_DK)
            vh = v_ref[:, h * HEAD_V:(h + 1) * HEAD_V]
            p = (_dot_nt(qb16[:, sl], kb16[:, sl]) * dec_ref[d, h]).astype(BF16)
            s = s_ref[d, h]
            o_ref[:, h * HEAD_V:(h + 1) * HEAD_V] = _dot(p, vh) + _dot(q_in[:, sl], s.astype(BF16))
            k_th = k_t[h // 2][(h % 2) * RET_DK:(h % 2 + 1) * RET_DK, :].astype(BF16)
            s_ref[d, h] = chunk_decay[h] * s + _dot(k_th, vh)


def retention_scan(z, s0, tables):
    l = z.shape[0]
    nblk = l // CH
    consts, chunk_decay = _ret_consts()
    rotary = tables is not None
    w = N_HEADS * RET_DK
    specs = (_dir_specs(nblk, w, OFF_RET_Q // w) + _dir_specs(nblk, w, OFF_RET_K // w)
             + _dir_specs(nblk, BRANCH_W, OFF_RET_V // BRANCH_W))
    args = [z] * 6
    if rotary:
        specs += _dir_specs(nblk, w, 0) + _dir_specs(nblk, w, 0)
        args += [tables[0], tables[0], tables[1], tables[1]]
    specs += [_whole(c) for c in consts] + [_whole(s0)]
    args += list(consts) + [s0]
    return pl.pallas_call(
        functools.partial(_ret_kernel, rotary=rotary, chunk_decay=chunk_decay),
        out_shape=(jax.ShapeDtypeStruct((l, BRANCH_W), F32), jax.ShapeDtypeStruct((l, BRANCH_W), F32),
                   jax.ShapeDtypeStruct(s0.shape, F32)),
        grid=(nblk,),
        in_specs=specs,
        out_specs=(pl.BlockSpec((CH, BRANCH_W), lambda n: (n, 0)),
                   pl.BlockSpec((CH, BRANCH_W), lambda n: (nblk - 1 - n, 0)), _whole(s0)),
        compiler_params=_cparams(("arbitrary",)),
        name="retention_scan",
    )(*args)


def _mlstm_kernel(qf, qb, kf, kb, vf, vb, smf, smb, bias_ref, c0_ref, m0_ref, of_ref, ob_ref, c_ref, m_ref):
    @pl.when(pl.program_id(0) == 0)
    def _():
        c_ref[...] = c0_ref[...]
        m_ref[...] = m0_ref[...]

    lane = lax.broadcasted_iota(jnp.int32, (CH, 128), 1)
    is_forget = (lane >= F_LANE) & (lane < F_LANE + 2 * N_HEADS)
    ones_col = jnp.where(lane == 0, 1.0, 0.0).astype(BF16)

    for d, (q_ref, k_ref, v_ref, sm_ref, o_ref) in enumerate(((qf, kf, vf, smf, of_ref), (qb, kb, vb, smb, ob_ref))):
        reverse = d == 1
        incl, _ = _masks(reverse)
        last = 0 if reverse else CH - 1
        pre = sm_ref[...] + bias_ref[0:1]
        x = jnp.where(is_forget, jax.nn.log_sigmoid(pre), pre)
        b = _cumsum_time(incl, x)
        x_t, b_t = x.T, b.T
        q = (q_ref[...].astype(F32) * MLSTM_DK ** -0.5).astype(BF16)
        kf32 = k_ref[...].astype(F32)
        k_t = [kf32[:, :128].T, kf32[:, 128:].T]
        for h in range(N_HEADS):
            gi, gf = I_LANE + d * N_HEADS + h, F_LANE + d * N_HEADS + h
            sl = slice(h * MLSTM_DK, (h + 1) * MLSTM_DK)
            b_col, b_row = b[:, gf:gf + 1], b_t[gf:gf + 1, :]
            li_row = x_t[gi:gi + 1, :]
            b_last = b[last:last + 1, gf:gf + 1]
            d_log = jnp.where(incl, b_col - b_row + li_row, NEG)
            m_intra = jnp.max(d_log, axis=1, keepdims=True)
            qk = _dot_nt(q[:, sl], k_ref[:, sl])
            p = (jnp.exp(d_log - m_intra) * qk).astype(BF16)
            v_aug = jnp.concatenate([v_ref[:, h * HEAD_V:(h + 1) * HEAD_V], ones_col], axis=1)
            pv = _dot(p, v_aug)
            e_row = b_last - b_row + li_row
            m_end = jnp.max(e_row, axis=1, keepdims=True)
            m_s = m_ref[d, h][0:1, 0:1]
            c_aug = c_ref[d, h]
            m_t = jnp.maximum(b_col + m_s, m_intra)
            inter = jnp.exp(b_col + m_s - m_t)
            r = jnp.exp(m_intra - m_t)
            tot = inter * _dot(q[:, sl], c_aug.astype(BF16)) + r * pv
            den = jnp.maximum(jnp.abs(tot[:, HEAD_V:HEAD_V + 1]), jnp.exp(-m_t))
            o_ref[:, h * HEAD_V:(h + 1) * HEAD_V] = tot[:, :HEAD_V] / den
            m_new = jnp.maximum(b_last + m_s, m_end)
            carry_decay = jnp.exp(b_last + m_s - m_new)
            k_w = (k_t[h // 2][(h % 2) * MLSTM_DK:(h % 2 + 1) * MLSTM_DK, :] * jnp.exp(e_row - m_new)).astype(BF16)
            c_ref[d, h] = carry_decay * c_aug + _dot(k_w, v_aug)
            m_ref[d, h] = jnp.broadcast_to(m_new, (8, 128))


def mlstm_scan(z, small, bias, c0, m0):
    l = z.shape[0]
    nblk = l // CH
    w = N_HEADS * MLSTM_DK
    specs = (_dir_specs(nblk, w, OFF_MLSTM_Q // w) + _dir_specs(nblk, w, OFF_MLSTM_K // w)
             + _dir_specs(nblk, BRANCH_W, OFF_MLSTM_V // BRANCH_W) + _dir_specs(nblk, 128, 0))
    specs += [_whole(bias), _whole(c0), _whole(m0)]
    return pl.pallas_call(
        _mlstm_kernel,
        out_shape=(jax.ShapeDtypeStruct((l, BRANCH_W), F32), jax.ShapeDtypeStruct((l, BRANCH_W), F32),
                   jax.ShapeDtypeStruct(c0.shape, F32), jax.ShapeDtypeStruct(m0.shape, F32)),
        grid=(nblk,),
        in_specs=specs,
        out_specs=(pl.BlockSpec((CH, BRANCH_W), lambda n: (n, 0)),
                   pl.BlockSpec((CH, BRANCH_W), lambda n: (nblk - 1 - n, 0)), _whole(c0), _whole(m0)),
        compiler_params=_cparams(("arbitrary",)),
        name="mlstm_scan",
    )(z, z, z, z, z, z, small, small, bias, c0, m0)


QKV_W = 3 * N_HEADS * GDN_DK
HALO = 8


def _gdn_prep_kernel(x_ref, prev_ref, next_ref, w_ref, q_ref, k_ref, v_ref):
    i = pl.program_id(0)
    tm = x_ref.shape[0]
    x = x_ref[...].astype(F32)
    prev = jnp.where(i > 0, prev_ref[...].astype(F32), 0.0)
    nxt = jnp.where(i < pl.num_programs(0) - 1, next_ref[...].astype(F32), 0.0)
    xe = jnp.concatenate([prev, x, nxt], axis=0)
    w = w_ref[...]
    y = None
    for tap in range(CONV_K):
        off = HALO + tap - CONV_K // 2
        term = w[tap:tap + 1] * xe[off:off + tm]
        y = term if y is None else y + term
    y = y * jax.nn.sigmoid(y)
    hw = N_HEADS * GDN_DK
    for h in range(N_HEADS):
        sl = slice(h * GDN_DK, (h + 1) * GDN_DK)
        qh = y[:, sl]
        kh = y[:, hw + h * GDN_DK:hw + (h + 1) * GDN_DK]
        q_ref[:, sl] = (qh * lax.rsqrt(jnp.sum(qh * qh, axis=-1, keepdims=True) + EPS) * GDN_DK ** -0.5).astype(BF16)
        k_ref[:, sl] = (kh * lax.rsqrt(jnp.sum(kh * kh, axis=-1, keepdims=True) + EPS)).astype(BF16)
    v_ref[...] = y[:, 2 * hw:].astype(BF16)


def gdn_prep(z, conv_w):
    l = z.shape[0]
    tm = min(l, 256)
    nb = l // tm
    r8 = tm // HALO
    cb = OFF_GDN_QKV // QKV_W
    w8 = jnp.concatenate([conv_w.astype(F32), jnp.zeros((8 - CONV_K, QKV_W), F32)], axis=0)
    return pl.pallas_call(
        _gdn_prep_kernel,
        out_shape=tuple(jax.ShapeDtypeStruct((l, BRANCH_W), BF16) for _ in range(3)),
        grid=(nb,),
        in_specs=[pl.BlockSpec((tm, QKV_W), lambda i: (i, cb)),
                  pl.BlockSpec((HALO, QKV_W), lambda i: (jnp.maximum(i * r8 - 1, 0), cb)),
                  pl.BlockSpec((HALO, QKV_W), lambda i: (jnp.minimum((i + 1) * r8, nb * r8 - 1), cb)),
                  pl.BlockSpec((8, QKV_W), lambda i: (0, 0))],
        out_specs=tuple(pl.BlockSpec((tm, BRANCH_W), lambda i: (i, 0)) for _ in range(3)),
        compiler_params=_cparams(("parallel",)),
        name="gdn_prep",
    )(z, z, z, w8)


N_LEVELS = 7


def _gdn_kernel(qf, qb, kf, kb, vf, vb, smf, smb, par_ref, s0_ref, of_ref, ob_ref, s_ref):
    @pl.when(pl.program_id(0) == 0)
    def _():
        s_ref[...] = s0_ref[...]

    lane = lax.broadcasted_iota(jnp.int32, (CH, 128), 1)
    is_decay = lane < B_LANE
    ri = lax.broadcasted_iota(jnp.int32, (CH, CH), 0)
    ci = lax.broadcasted_iota(jnp.int32, (CH, CH), 1)
    eye = jnp.where(ri == ci, 1.0, 0.0)
    pair_masks = [((ri >> (l + 1)) == (ci >> (l + 1))) & ((ri >> l) != (ci >> l)) for l in range(N_LEVELS)]

    for d, (q_ref, k_ref, v_ref, sm_ref, o_ref) in enumerate(((qf, kf, vf, smf, of_ref), (qb, kb, vb, smb, ob_ref))):
        reverse = d == 1
        incl, strict = _masks(reverse)
        last = 0 if reverse else CH - 1
        sm = sm_ref[...]
        log_a = -jnp.exp(par_ref[1:2]) * jax.nn.softplus(sm + par_ref[0:1])
        x = jnp.where(is_decay, log_a, jax.nn.sigmoid(sm))
        g = _cumsum_time(incl, x)
        g_t = g.T
        for h in range(N_HEADS):
            ga, gb = A_LANE + d * N_HEADS + h, B_LANE + d * N_HEADS + h
            sl = slice(h * GDN_DK, (h + 1) * GDN_DK)
            g_col, g_row = g[:, ga:ga + 1], g_t[ga:ga + 1, :]
            g_last = g[last:last + 1, ga:ga + 1]
            beta = x[:, gb:gb + 1]
            decay = jnp.where(incl, jnp.exp(jnp.minimum(g_col - g_row, 0.0)), 0.0)
            kh = k_ref[:, sl]
            kf32 = kh.astype(F32)
            kbeta = kf32 * beta
            a = jnp.where(strict, _dot_nt(kbeta.astype(BF16), kh) * decay, 0.0)
            x_inv = eye - jnp.where(pair_masks[0], a, 0.0)
            for pm in pair_masks[1:]:
                y = _dot(jnp.where(pm, a, 0.0).astype(BF16), x_inv.astype(BF16))
                x_inv = x_inv - _dot(x_inv.astype(BF16), y.astype(BF16))
            e_g = jnp.exp(g_col)
            rhs = jnp.concatenate([v_ref[:, sl].astype(F32) * beta, kbeta * e_g], axis=1)
            sol = rhs + _dot((x_inv - eye).astype(BF16), rhs.astype(BF16))
            u, w = sol[:, :HEAD_V], sol[:, HEAD_V:]
            qk = (_dot_nt(q_ref[:, sl], kh) * decay).astype(BF16)
            q_in = (q_ref[:, sl].astype(F32) * e_g).astype(BF16)
            k_out_t = (kf32.T * jnp.exp(g_last - g_row)).astype(BF16)
            s = s_ref[d, h]
            s16 = s.astype(BF16)
            v_new = (u - _dot(w.astype(BF16), s16)).astype(BF16)
            o_ref[:, sl] = _dot(q_in, s16) + _dot(qk, v_new)
            s_ref[d, h] = s * jnp.exp(g_last) + _dot(k_out_t, v_new)


def gdn_scan(qn, kn, vn, small, par, s0):
    l = qn.shape[0]
    nblk = l // CH
    specs = _dir_specs(nblk, BRANCH_W, 0) * 3 + _dir_specs(nblk, 128, 0)
    specs += [_whole(par), _whole(s0)]
    return pl.pallas_call(
        _gdn_kernel,
        out_shape=(jax.ShapeDtypeStruct((l, BRANCH_W), F32), jax.ShapeDtypeStruct((l, BRANCH_W), F32),
                   jax.ShapeDtypeStruct(s0.shape, F32)),
        grid=(nblk,),
        in_specs=specs,
        out_specs=(pl.BlockSpec((CH, BRANCH_W), lambda n: (n, 0)),
                   pl.BlockSpec((CH, BRANCH_W), lambda n: (nblk - 1 - n, 0)), _whole(s0)),
        compiler_params=_cparams(("arbitrary",)),
        name="gdn_scan",
    )(qn, qn, kn, kn, vn, vn, small, small, par, s0)


def _post_kernel(of_ref, ob_ref, gate_ref, g_ref, o_ref, *, centre, silu_gate):
    o = of_ref[...] + ob_ref[...]
    gate = gate_ref[...].astype(F32)
    sig = jax.nn.sigmoid(gate)
    act = gate * sig if silu_gate else sig
    for h in range(N_HEADS):
        sl = slice(h * HEAD_V, (h + 1) * HEAD_V)
        oh = o[:, sl]
        if centre:
            oh = oh - jnp.mean(oh, axis=-1, keepdims=True)
        oh = oh * lax.rsqrt(jnp.mean(oh * oh, axis=-1, keepdims=True) + EPS)
        o_ref[:, sl] = (oh * g_ref[:, sl] * act[:, sl]).astype(BF16)


def head_post(o_f, o_b, z, gate_off, norm_g, centre, silu_gate):
    l = o_f.shape[0]
    tm = min(l, 1024)
    row = pl.BlockSpec((tm, BRANCH_W), lambda i: (i, 0))
    return pl.pallas_call(
        functools.partial(_post_kernel, centre=centre, silu_gate=silu_gate),
        out_shape=jax.ShapeDtypeStruct((l, BRANCH_W), BF16),
        grid=(l // tm,),
        in_specs=[row, row, pl.BlockSpec((tm, BRANCH_W), lambda i: (i, gate_off // BRANCH_W)),
                  pl.BlockSpec((1, BRANCH_W), lambda i: (0, 0))],
        out_specs=row,
        compiler_params=_cparams(("parallel",)),
        name="head_post",
    )(o_f, o_b, z, norm_g.reshape(1, BRANCH_W).astype(F32))


def _lane_rows(entries):
    r = jnp.zeros((8, 128), F32)
    for row, lane, vals in entries:
        r = r.at[row, lane:lane + vals.shape[0]].set(vals.astype(F32))
    return r


def gdn_branch(zc, sc, zl, sl, conv_w, a_log, dt_bias, norm_g, with_ctx):
    par = _lane_rows([(0, A_LANE, dt_bias.reshape(-1)), (1, A_LANE, a_log.reshape(-1))])
    s0 = jnp.zeros((2, N_HEADS, GDN_DK, HEAD_V), F32)
    ofc, obc, s1 = gdn_scan(*gdn_prep(zc, conv_w), sc, par, s0)
    ofl, obl, _ = gdn_scan(*gdn_prep(zl, conv_w), sl, par, s1)
    out_c = head_post(ofc, obc, zc, OFF_GDN_Z, norm_g, False, True) if with_ctx else None
    return out_c, head_post(ofl, obl, zl, OFF_GDN_Z, norm_g, False, True)


def mlstm_branch(zc, sc, zl, sl, gate_b, norm_g, with_ctx):
    bias = _lane_rows([(0, I_LANE, gate_b[0].reshape(-1)), (0, F_LANE, gate_b[1].reshape(-1))])
    c0 = jnp.zeros((2, N_HEADS, MLSTM_DK, 2 * HEAD_V), F32)
    m0 = jnp.zeros((2, N_HEADS, 8, 128), F32)
    ofc, obc, c1, m1 = mlstm_scan(zc, sc, bias, c0, m0)
    ofl, obl, _, _ = mlstm_scan(zl, sl, bias, c1, m1)
    out_c = head_post(ofc, obc, zc, OFF_MLSTM_O, norm_g, False, False) if with_ctx else None
    return out_c, head_post(ofl, obl, zl, OFF_MLSTM_O, norm_g, False, False)


def retention_branch(zc, zl, tables, norm_g, with_ctx):
    s0 = jnp.zeros((2, N_HEADS, RET_DK, HEAD_V), F32)
    ofc, obc, s1 = retention_scan(zc, s0, None)
    ofl, obl, _ = retention_scan(zl, s1, tables)
    out_c = head_post(ofc, obc, zc, OFF_RET_G, norm_g, True, True) if with_ctx else None
    return out_c, head_post(ofl, obl, zl, OFF_RET_G, norm_g, True, True)


def _ret_rope_tables(n_lat):
    inv = ROPE_BASE ** (-jnp.arange(0, RET_DK, 2, dtype=F32) / RET_DK)
    ang = jnp.arange(n_lat, dtype=F32)[:, None] * inv[None, :]
    cos = jnp.concatenate([jnp.cos(ang), jnp.cos(ang)], axis=-1)
    sin = jnp.concatenate([-jnp.sin(ang), jnp.sin(ang)], axis=-1)
    return jnp.tile(cos, (1, N_HEADS)), jnp.tile(sin, (1, N_HEADS))


def _rope_tables(n_lat):
    q = MLA_ROPE // 4
    inv = ROPE_BASE ** (-jnp.arange(0, 2 * q, 2, dtype=F32) / (2 * q))
    t = jnp.arange(n_lat)
    row = (t // GRID_W).astype(F32)[:, None] * inv[None, :]
    col = (t % GRID_W).astype(F32)[:, None] * inv[None, :]
    cos = jnp.concatenate([jnp.cos(row), jnp.cos(row), jnp.cos(col), jnp.cos(col)], axis=-1)
    sin = jnp.concatenate([-jnp.sin(row), jnp.sin(row), -jnp.sin(col), jnp.sin(col)], axis=-1)
    return cos, sin


def _pad_rows(v, n=8):
    rows = [jnp.reshape(r, (1, -1)).astype(F32) for r in v]
    d = rows[0].shape[1]
    return jnp.concatenate(rows + [jnp.zeros((n - len(rows), d), F32)], axis=0)


def _mla_weights(w_uq, w_ukv):
    wq = w_uq.reshape(MLA_Q_RANK, N_HEADS, MLA_QK)
    rope = wq[:, :, MLA_NOPE:]
    swapped = rope[:, :, _rope_swap(np.arange(MLA_ROPE))]
    wq_ext = jnp.concatenate([wq[:, :, :MLA_NOPE], rope, swapped], axis=-1).reshape(MLA_Q_RANK, N_HEADS * 256)
    return wq_ext.astype(BF16), w_ukv.astype(BF16)


def _moe(x_l, pv2, g2, w_router, b_router, w_gu, w_down, final_g):
    n = x_l.shape[0]
    w_pad = jnp.concatenate([w_router, jnp.zeros((D_MODEL, 128 - N_EXPERTS), F32)], axis=1)
    b_pad = jnp.concatenate([b_router, jnp.full((128 - N_EXPERTS,), -1e30, F32)]).reshape(1, 128)
    h2, logits = moe_router(x_l, pv2, w_pad, b_pad)
    top_logit, top_e = lax.top_k(logits[:, :N_EXPERTS], TOP_K)
    weight = jax.nn.softmax(top_logit, axis=-1)
    n_assign = n * TOP_K
    flat_e = top_e.reshape(-1)
    onehot = (flat_e[:, None] == jnp.arange(N_EXPERTS)[None, :]).astype(jnp.int32)
    csum = jnp.cumsum(onehot, axis=0)
    counts = csum[-1]
    rank = jnp.take_along_axis(csum, flat_e[:, None], axis=1)[:, 0] - 1
    padded = (counts + MOE_ROWS - 1) // MOE_ROWS * MOE_ROWS
    pad_end = jnp.cumsum(padded)
    dest = (pad_end - padded)[flat_e] + rank
    nb = n_assign // MOE_ROWS + N_EXPERTS
    cap = nb * MOE_ROWS
    slot_tok = jnp.zeros((cap,), jnp.int32).at[dest].set(jnp.arange(n_assign, dtype=jnp.int32) // TOP_K)
    block_e = jnp.minimum(jnp.searchsorted(pad_end, jnp.arange(nb) * MOE_ROWS, side='right'),
                          N_EXPERTS - 1).astype(jnp.int32)
    n_used = (pad_end[-1] // MOE_ROWS).astype(jnp.int32).reshape(1)
    xb = h2[slot_tok]
    yb = expert_ffn(block_e, n_used, xb, w_gu, w_down)
    d2 = dest.reshape(n, TOP_K)
    w128 = jnp.concatenate([weight, jnp.zeros((n, 126), F32)], axis=1)
    return final_combine(x_l, yb[d2[:, 0]], yb[d2[:, 1]], w128, _pad_rows([g2, final_g]))


def kernel(x, c, ctx, c_ctx, w_mod, b_mod, norm1_g, norm2_g, w_in, gdn_conv_w, gdn_a_log, gdn_dt_bias, gdn_norm_g, mla_q_norm_g, mla_kv_norm_g, mla_w_uq, mla_w_ukv, mlstm_gate_b, mlstm_norm_g, ret_norm_g, w_branch, w_out, ffn_w_in, ffn_w_down, moe_w_router, moe_b_router, moe_w_in, moe_w_down, final_norm_g):
    n_lat = x.shape[1]
    n_ctx = ctx.shape[1]
    x_l, x_c = x[0], ctx[0]
    cond = _pad_rows([c_ctx, c[0]])
    cos_l, sin_l = _rope_tables(n_lat)
    cos_c, sin_c = jnp.ones((n_ctx, MLA_ROPE), F32), jnp.zeros((n_ctx, MLA_ROPE), F32)
    ret_tables = _ret_rope_tables(n_lat)
    out = None
    for li in range(DEPTH):
        last = li == DEPTH - 1
        mod = modulation_vectors(cond, w_mod[li], b_mod[li])
        csh1, csc1, cg1, csh2, csc2, cg2 = jnp.split(mod[0], 6)
        sh1, sc1, g1, sh2, sc2, g2 = jnp.split(mod[1], 6)
        w_main = w_in[li][:, _MAIN_PERM].astype(BF16)
        w_small = w_in[li][:, _SMALL_PERM].astype(BF16)
        pv_l = _pad_rows([norm1_g[li], 1 + sc1, sh1])
        pv_c = _pad_rows([norm1_g[li], 1 + csc1, csh1])
        zl = norm_proj(x_l, pv_l, w_main, 768, BF16)
        zc = norm_proj(x_c, pv_c, w_main, 768, BF16)
        sl = norm_proj(x_l, pv_l, w_small, 128, F32)
        sc = norm_proj(x_c, pv_c, w_small, 128, F32)

        a_c, a_l = gdn_branch(zc, sc, zl, sl, gdn_conv_w[li], gdn_a_log[li], gdn_dt_bias[li], gdn_norm_g[li], not last)
        c_c, c_l = mlstm_branch(zc, sc, zl, sl, mlstm_gate_b[li], mlstm_norm_g[li], not last)
        d_c, d_l = retention_branch(zc, zl, ret_tables, ret_norm_g[li], not last)

        wq_ext, wkv = _mla_weights(mla_w_uq[li], mla_w_ukv[li])
        gq, gkv = mla_q_norm_g[li].reshape(1, -1), mla_kv_norm_g[li].reshape(1, -1)
        ql, kl, vl = mla_project(zl, cos_l, sin_l, gq, gkv, wq_ext, wkv)
        qc, kc, vc = mla_project(zc, cos_c, sin_c, gq, gkv, wq_ext, wkv)
        b_l = attention(ql, [(kl, vl), (kc, vc)])

        wb = w_branch[li].astype(BF16)
        wo = w_out[li].astype(BF16)
        x_l = merge_branches(x_l, zl, (a_l, b_l, c_l, d_l), wb, wo, _pad_rows([g1]))
        if not last:
            b_c = attention(qc, [(kc, vc)])
            x_c = merge_branches(x_c, zc, (a_c, b_c, c_c, d_c), wb, wo, _pad_rows([cg1]))

        if li % 2 == 0:
            w_gu = ffn_w_in[li // 2].astype(BF16)
            w_dn = ffn_w_down[li // 2].astype(BF16)
            x_l = dense_ffn(x_l, _pad_rows([norm2_g[li], 1 + sc2, sh2, g2]), w_gu, w_dn)
            if not last:
                x_c = dense_ffn(x_c, _pad_rows([norm2_g[li], 1 + csc2, csh2, cg2]), w_gu, w_dn)
            if last:
                out = final_combine(x_l, jnp.zeros_like(x_l), jnp.zeros_like(x_l),
                                    jnp.zeros((n_lat, 128), F32), _pad_rows([jnp.zeros_like(g2), final_norm_g]))
        else:
            assert last
            out = _moe(x_l, _pad_rows([norm2_g[li], 1 + sc2, sh2]), g2, moe_w_router[li // 2],
                       moe_b_router[li // 2], moe_w_in[li // 2], moe_w_down[li // 2], final_norm_g)
    return out[None]
```

```python
import functools
import math

import numpy as np
import jax
import jax.numpy as jnp
from jax import lax
from jax.experimental import pallas as pl
from jax.experimental.pallas import tpu as pltpu

F32 = jnp.float32
BF16 = jnp.bfloat16

D_MODEL = 1024
DEPTH = 2
GRID_W = 64
N_BRANCH = 4
N_HEADS = 4
HEAD_V = 128
BRANCH_W = N_HEADS * HEAD_V
GDN_DK = 128
CONV_K = 5
MLA_Q_RANK = 384
MLA_KV_RANK = 256
MLA_NOPE = 128
MLA_ROPE = 64
MLA_QK = MLA_NOPE + MLA_ROPE
MLA_QK_PAD = 256
MLA_V_PAD = 256
MLSTM_DK = 64
RET_DK = 64
ROPE_BASE = 10000.0
D_FF = 3584
N_EXPERTS = 8
TOP_K = 2
EPS = 1e-6

IN_WIDTHS = (
    N_BRANCH * D_MODEL,
    N_HEADS * GDN_DK, N_HEADS * GDN_DK, BRANCH_W, BRANCH_W, 2 * N_HEADS, 2 * N_HEADS,
    MLA_Q_RANK, MLA_KV_RANK, MLA_ROPE,
    N_HEADS * MLSTM_DK, N_HEADS * MLSTM_DK, BRANCH_W, BRANCH_W, 2 * N_HEADS, 2 * N_HEADS,
    N_HEADS * RET_DK, N_HEADS * RET_DK, BRANCH_W, BRANCH_W,
)
_IN_OFF = [0] + [int(o) for o in np.cumsum(IN_WIDTHS)]

VMEM_LIMIT = 48 * 1024 * 1024
MOE_ROWS = 1024


def _cols(group):
    return np.arange(_IN_OFF[group], _IN_OFF[group + 1])


def _rope_swap(cols):
    q = MLA_ROPE // 4
    return np.concatenate([cols[q:2 * q], cols[:q], cols[3 * q:], cols[2 * q:3 * q]])


_MAIN_PERM = np.concatenate([
    _cols(1), _cols(2), _cols(3),
    _cols(4),
    _cols(0),
    _cols(12), _cols(13),
    _cols(10), _cols(11),
    _cols(18), _cols(19),
    _cols(16), _cols(17),
    _cols(7), _cols(8), _cols(9), _rope_swap(_cols(9)),
])
N_MAIN = int(_MAIN_PERM.shape[0])
_SMALL_PERM = np.concatenate([_cols(5), _cols(6), _cols(14), _cols(15), np.zeros(96, np.int64)])
A_LANE, B_LANE, I_LANE, F_LANE = 0, 8, 16, 24
OFF_GDN_QKV, OFF_GDN_Z, OFF_GATE = 0, 1536, 2048
OFF_MLSTM_V, OFF_MLSTM_O, OFF_MLSTM_Q, OFF_MLSTM_K = 6144, 6656, 7168, 7424
OFF_RET_V, OFF_RET_G, OFF_RET_Q, OFF_RET_K = 7680, 8192, 8704, 8960
OFF_MLA = 9216
MLA_IN_W = 768


def _cparams(sem):
    return pltpu.CompilerParams(dimension_semantics=sem, vmem_limit_bytes=VMEM_LIMIT)


def _rms(x):
    return x * lax.rsqrt(jnp.mean(x * x, axis=-1, keepdims=True) + EPS)


def _dot(a, b):
    return jnp.dot(a, b, preferred_element_type=F32)


def _dot_nt(a, b):
    return lax.dot_general(a, b, (((1,), (1,)), ((), ())), preferred_element_type=F32)


def _mod_kernel(c_ref, w_ref, b_ref, o_ref):
    c = c_ref[...]
    s = c * jax.nn.sigmoid(c)
    o_ref[...] = jnp.dot(s, w_ref[...], preferred_element_type=F32) + b_ref[...]


def modulation_vectors(cond, w_mod, b_mod):
    n = w_mod.shape[1]
    tn = 1536
    return pl.pallas_call(
        _mod_kernel,
        out_shape=jax.ShapeDtypeStruct((8, n), F32),
        grid=(n // tn,),
        in_specs=[pl.BlockSpec((8, D_MODEL), lambda j: (0, 0)),
                  pl.BlockSpec((D_MODEL, tn), lambda j: (0, j)),
                  pl.BlockSpec((1, tn), lambda j: (0, j))],
        out_specs=pl.BlockSpec((8, tn), lambda j: (0, j)),
        compiler_params=_cparams(("arbitrary",)),
        name="modulation",
    )(cond, w_mod, b_mod.reshape(1, n))


def _norm_proj_kernel(x_ref, pv_ref, w_ref, o_ref, h_ref):
    @pl.when(pl.program_id(1) == 0)
    def _():
        pv = pv_ref[...]
        h_ref[...] = (_rms(x_ref[...]) * pv[0:1] * pv[1:2] + pv[2:3]).astype(BF16)

    o_ref[...] = _dot(h_ref[...], w_ref[...]).astype(o_ref.dtype)


def norm_proj(x, pv, w, tn, out_dtype):
    m, d = x.shape
    n = w.shape[1]
    tm = min(m, 1024)
    return pl.pallas_call(
        _norm_proj_kernel,
        out_shape=jax.ShapeDtypeStruct((m, n), out_dtype),
        grid=(m // tm, n // tn),
        in_specs=[pl.BlockSpec((tm, d), lambda i, j: (i, 0)),
                  pl.BlockSpec((8, d), lambda i, j: (0, 0)),
                  pl.BlockSpec((d, tn), lambda i, j: (0, j))],
        out_specs=pl.BlockSpec((tm, tn), lambda i, j: (i, j)),
        scratch_shapes=[pltpu.VMEM((tm, d), BF16)],
        compiler_params=_cparams(("parallel", "arbitrary")),
        name="norm_proj",
    )(x, pv, w)


def _mla_proj_kernel(z_ref, cos_ref, sin_ref, gq_ref, gkv_ref, wq_ref, wkv_ref, q_ref, k_ref, v_ref):
    z = z_ref[...].astype(F32)
    tm = z.shape[0]
    cq = z[:, :MLA_Q_RANK]
    ckv = z[:, MLA_Q_RANK:MLA_Q_RANK + MLA_KV_RANK]
    kr = z[:, MLA_Q_RANK + MLA_KV_RANK:]
    cos = cos_ref[...]
    sin = sin_ref[...]
    qn = (_rms(cq) * gq_ref[...]).astype(BF16)
    kvn = (_rms(ckv) * gkv_ref[...]).astype(BF16)
    qf = _dot(qn, wq_ref[...]) * (MLA_QK ** -0.5 * math.log2(math.e))
    kvf = _dot(kvn, wkv_ref[...])
    kr_rot = kr[:, :MLA_ROPE] * cos + kr[:, MLA_ROPE:] * sin
    pad = jnp.zeros((tm, MLA_QK_PAD - MLA_QK), F32)
    lane = lax.broadcasted_iota(jnp.int32, (tm, MLA_V_PAD - HEAD_V), 1)
    ones_col = jnp.where(lane == 0, 1.0, 0.0).astype(BF16)
    for h in range(N_HEADS):
        b = h * 256
        q_rot = qf[:, b + 128:b + 192] * cos + qf[:, b + 192:b + 256] * sin
        q_ref[h] = jnp.concatenate([qf[:, b:b + 128], q_rot, pad], axis=-1).astype(BF16)
        k_ref[h] = jnp.concatenate([kvf[:, b:b + 128], kr_rot, pad], axis=-1).astype(BF16)
        v_ref[h] = jnp.concatenate([kvf[:, b + 128:b + 256].astype(BF16), ones_col], axis=-1)


def mla_project(zmain, cos, sin, gq, gkv, wq_ext, wkv):
    m = zmain.shape[0]
    tm = min(m, 1024)
    full = lambda shape: pl.BlockSpec(shape, lambda i: tuple(0 for _ in shape))
    return pl.pallas_call(
        _mla_proj_kernel,
        out_shape=(jax.ShapeDtypeStruct((N_HEADS, m, MLA_QK_PAD), BF16),
                   jax.ShapeDtypeStruct((N_HEADS, m, MLA_QK_PAD), BF16),
                   jax.ShapeDtypeStruct((N_HEADS, m, MLA_V_PAD), BF16)),
        grid=(m // tm,),
        in_specs=[pl.BlockSpec((tm, MLA_IN_W), lambda i: (i, OFF_MLA // MLA_IN_W)),
                  pl.BlockSpec((tm, MLA_ROPE), lambda i: (i, 0)),
                  pl.BlockSpec((tm, MLA_ROPE), lambda i: (i, 0)),
                  full((1, MLA_Q_RANK)), full((1, MLA_KV_RANK)),
                  full((MLA_Q_RANK, N_HEADS * 256)), full((MLA_KV_RANK, N_HEADS * 256))],
        out_specs=(pl.BlockSpec((N_HEADS, tm, MLA_QK_PAD), lambda i: (0, i, 0)),
                   pl.BlockSpec((N_HEADS, tm, MLA_QK_PAD), lambda i: (0, i, 0)),
                   pl.BlockSpec((N_HEADS, tm, MLA_V_PAD), lambda i: (0, i, 0))),
        compiler_params=_cparams(("parallel",)),
        name="mla_project",
    )(zmain, cos, sin, gq, gkv, wq_ext, wkv)


ATTN_TQ, ATTN_TK = 512, 512
ATTN_UNROLL = 16


def _attn_kernel(*refs, segs):
    q_ref, o_ref = refs[0], refs[-1]
    q = q_ref[0]
    tq = q.shape[0]
    carry = (jnp.full((tq, 1), -1e30, F32), jnp.zeros((tq, MLA_V_PAD), F32))
    for si, (tk, nk) in enumerate(segs):
        k_ref, v_ref = refs[1 + 2 * si], refs[2 + 2 * si]

        def body(c, carry, k_ref=k_ref, v_ref=v_ref, tk=tk):
            m, acc = carry
            start = pl.multiple_of(c * tk, tk)
            s = _dot_nt(q, k_ref[0, pl.ds(start, tk), :])
            m_new = jnp.maximum(m, jnp.max(s, axis=-1, keepdims=True))
            p = jnp.exp2(s - m_new).astype(BF16)
            acc = jnp.exp2(m - m_new) * acc + _dot(p, v_ref[0, pl.ds(start, tk), :])
            return m_new, acc

        carry = lax.fori_loop(0, nk, body, carry, unroll=min(ATTN_UNROLL, nk))
    _, acc = carry
    o_ref[...] = (acc[:, :HEAD_V] / acc[:, HEAD_V:HEAD_V + 1]).astype(o_ref.dtype)


def attention(q, kvs):
    _, lq, _ = q.shape
    tq = min(lq, ATTN_TQ)
    segs, args, specs = [], [], []
    for k, v in kvs:
        lk = k.shape[1]
        tk = min(lk, ATTN_TK)
        segs.append((tk, lk // tk))
        args += [k, v]
        specs += [pl.BlockSpec((1, lk, MLA_QK_PAD), lambda h, i: (h, 0, 0)),
                  pl.BlockSpec((1, lk, MLA_V_PAD), lambda h, i: (h, 0, 0))]
    return pl.pallas_call(
        functools.partial(_attn_kernel, segs=tuple(segs)),
        out_shape=jax.ShapeDtypeStruct((lq, N_HEADS * HEAD_V), BF16),
        grid=(N_HEADS, lq // tq),
        in_specs=[pl.BlockSpec((1, tq, MLA_QK_PAD), lambda h, i: (h, i, 0))] + specs,
        out_specs=pl.BlockSpec((tq, HEAD_V), lambda h, i: (i, h)),
        compiler_params=_cparams(("parallel", "arbitrary")),
        name="attention",
    )(q, *args)


def _merge_kernel(x_ref, g0_ref, g1_ref, g2_ref, g3_ref, a_ref, b_ref, c_ref, d_ref, wb_ref, wo_ref, g_ref, o_ref):
    s = None
    for n, (br, gate_ref) in enumerate(zip((a_ref, b_ref, c_ref, d_ref), (g0_ref, g1_ref, g2_ref, g3_ref))):
        proj = _dot(br[...], wb_ref[n])
        gate = jax.nn.sigmoid(gate_ref[...].astype(F32))
        s = gate * proj if s is None else s + gate * proj
    m = _dot(s.astype(BF16), wo_ref[...])
    o_ref[...] = x_ref[...] + g_ref[0:1] * m


def merge_branches(x, zmain, branches, w_branch, w_out, gvec):
    m = x.shape[0]
    tm = min(m, 512)
    row = lambda w: pl.BlockSpec((tm, w), lambda i: (i, 0))
    return pl.pallas_call(
        _merge_kernel,
        out_shape=jax.ShapeDtypeStruct((m, D_MODEL), F32),
        grid=(m // tm,),
        in_specs=[row(D_MODEL)]
                 + [pl.BlockSpec((tm, D_MODEL), lambda i, n=n: (i, OFF_GATE // D_MODEL + n)) for n in range(N_BRANCH)]
                 + [row(BRANCH_W)] * 4
                 + [pl.BlockSpec((N_BRANCH, BRANCH_W, D_MODEL), lambda i: (0, 0, 0)),
                    pl.BlockSpec((D_MODEL, D_MODEL), lambda i: (0, 0)),
                    pl.BlockSpec((8, D_MODEL), lambda i: (0, 0))],
        out_specs=row(D_MODEL),
        compiler_params=_cparams(("parallel",)),
        name="merge_branches",
    )(x, zmain, zmain, zmain, zmain, *branches, w_branch, w_out, gvec)


def _ffn_kernel(x_ref, pv_ref, wg_ref, wu_ref, wd_ref, o_ref, h_ref, acc_ref):
    f = pl.program_id(1)

    @pl.when(f == 0)
    def _():
        pv = pv_ref[...]
        h_ref[...] = (_rms(x_ref[...]) * pv[0:1] * pv[1:2] + pv[2:3]).astype(BF16)
        acc_ref[...] = jnp.zeros_like(acc_ref)

    h = h_ref[...]
    g = _dot(h, wg_ref[...])
    u = _dot(h, wu_ref[...])
    act = (g * jax.nn.sigmoid(g) * u).astype(BF16)
    acc_ref[...] += _dot(act, wd_ref[...])

    @pl.when(f == pl.num_programs(1) - 1)
    def _():
        o_ref[...] = x_ref[...] + pv_ref[3:4] * acc_ref[...]


def dense_ffn(x, pv, w_gu, w_down):
    m = x.shape[0]
    tm = min(m, 1024)
    tf = 512
    nf = D_FF // tf
    return pl.pallas_call(
        _ffn_kernel,
        out_shape=jax.ShapeDtypeStruct((m, D_MODEL), F32),
        grid=(m // tm, nf),
        in_specs=[pl.BlockSpec((tm, D_MODEL), lambda i, f: (i, 0)),
                  pl.BlockSpec((8, D_MODEL), lambda i, f: (0, 0)),
                  pl.BlockSpec((D_MODEL, tf), lambda i, f: (0, f)),
                  pl.BlockSpec((D_MODEL, tf), lambda i, f: (0, f + nf)),
                  pl.BlockSpec((tf, D_MODEL), lambda i, f: (f, 0))],
        out_specs=pl.BlockSpec((tm, D_MODEL), lambda i, f: (i, 0)),
        scratch_shapes=[pltpu.VMEM((tm, D_MODEL), BF16), pltpu.VMEM((tm, D_MODEL), F32)],
        compiler_params=_cparams(("parallel", "arbitrary")),
        name="dense_ffn",
    )(x, pv, w_gu, w_gu, w_down)


def _router_kernel(x_ref, pv_ref, w_ref, b_ref, h_ref, logit_ref):
    pv = pv_ref[...]
    h = _rms(x_ref[...]) * pv[0:1] * pv[1:2] + pv[2:3]
    h_hi = h.astype(BF16)
    h_lo = (h - h_hi.astype(F32)).astype(BF16)
    w = w_ref[...]
    w_hi = w.astype(BF16)
    w_lo = (w - w_hi.astype(F32)).astype(BF16)
    logits = _dot(h_hi, w_hi) + _dot(h_hi, w_lo) + _dot(h_lo, w_hi)
    h_ref[...] = h_hi
    logit_ref[...] = logits + b_ref[...]


def moe_router(x, pv, w_router_pad, b_router_pad):
    m = x.shape[0]
    tm = min(m, 1024)
    return pl.pallas_call(
        _router_kernel,
        out_shape=(jax.ShapeDtypeStruct((m, D_MODEL), BF16), jax.ShapeDtypeStruct((m, 128), F32)),
        grid=(m // tm,),
        in_specs=[pl.BlockSpec((tm, D_MODEL), lambda i: (i, 0)),
                  pl.BlockSpec((8, D_MODEL), lambda i: (0, 0)),
                  pl.BlockSpec((D_MODEL, 128), lambda i: (0, 0)),
                  pl.BlockSpec((1, 128), lambda i: (0, 0))],
        out_specs=(pl.BlockSpec((tm, D_MODEL), lambda i: (i, 0)), pl.BlockSpec((tm, 128), lambda i: (i, 0))),
        compiler_params=_cparams(("parallel",)),
        name="moe_router",
    )(x, pv, w_router_pad, b_router_pad)


def _expert_kernel(be_ref, nb_ref, x_ref, wg_ref, wu_ref, wd_ref, o_ref, acc_ref):
    b = pl.program_id(0)
    f = pl.program_id(1)

    @pl.when(b < nb_ref[0])
    def _():
        @pl.when(f == 0)
        def _():
            acc_ref[...] = jnp.zeros_like(acc_ref)

        x = x_ref[...]
        g = _dot(x, wg_ref[0].astype(BF16))
        u = _dot(x, wu_ref[0].astype(BF16))
        act = (g * jax.nn.sigmoid(g) * u).astype(BF16)
        acc_ref[...] += _dot(act, wd_ref[0].astype(BF16))

        @pl.when(f == pl.num_programs(1) - 1)
        def _():
            o_ref[...] = acc_ref[...]

    @pl.when((b >= nb_ref[0]) & (f == pl.num_programs(1) - 1))
    def _():
        o_ref[...] = jnp.zeros_like(o_ref)


def expert_ffn(block_e, n_used, xb, w_gu, w_down):
    cap = xb.shape[0]
    nb = cap // MOE_ROWS
    tf = 512
    nf = D_FF // tf

    def live(b, nbr):
        return jnp.minimum(b, nbr[0] - 1)

    def fsel(b, f, nbr):
        return jnp.where(b < nbr[0], f, nf - 1)

    grid_spec = pltpu.PrefetchScalarGridSpec(
        num_scalar_prefetch=2,
        grid=(nb, nf),
        in_specs=[pl.BlockSpec((MOE_ROWS, D_MODEL), lambda b, f, be, nbr: (live(b, nbr), 0)),
                  pl.BlockSpec((1, D_MODEL, tf), lambda b, f, be, nbr: (be[live(b, nbr)], 0, fsel(b, f, nbr))),
                  pl.BlockSpec((1, D_MODEL, tf), lambda b, f, be, nbr: (be[live(b, nbr)], 0, fsel(b, f, nbr) + nf)),
                  pl.BlockSpec((1, tf, D_MODEL), lambda b, f, be, nbr: (be[live(b, nbr)], fsel(b, f, nbr), 0))],
        out_specs=pl.BlockSpec((MOE_ROWS, D_MODEL), lambda b, f, be, nbr: (b, 0)),
        scratch_shapes=[pltpu.VMEM((MOE_ROWS, D_MODEL), F32)],
    )
    return pl.pallas_call(
        _expert_kernel,
        out_shape=jax.ShapeDtypeStruct((cap, D_MODEL), F32),
        grid_spec=grid_spec,
        compiler_params=_cparams(("arbitrary", "arbitrary")),
        name="expert_ffn",
    )(block_e, n_used, xb, w_gu, w_gu, w_down)


def _final_kernel(x_ref, y0_ref, y1_ref, w_ref, pv_ref, o_ref):
    w = w_ref[...]
    f = w[:, 0:1] * y0_ref[...] + w[:, 1:2] * y1_ref[...]
    x = x_ref[...] + pv_ref[0:1] * f
    o_ref[...] = _rms(x) * pv_ref[1:2]


def final_combine(x, y0, y1, w, pv):
    m = x.shape[0]
    tm = min(m, 1024)
    row = lambda wd: pl.BlockSpec((tm, wd), lambda i: (i, 0))
    return pl.pallas_call(
        _final_kernel,
        out_shape=jax.ShapeDtypeStruct((m, D_MODEL), F32),
        grid=(m // tm,),
        in_specs=[row(D_MODEL), row(D_MODEL), row(D_MODEL), row(128),
                  pl.BlockSpec((8, D_MODEL), lambda i: (0, 0))],
        out_specs=row(D_MODEL),
        compiler_params=_cparams(("parallel",)),
        name="final_combine",
    )(x, y0, y1, w, pv)


CH = 128
NEG = -1e30


def _masks(reverse):
    r = lax.broadcasted_iota(jnp.int32, (CH, CH), 0)
    c = lax.broadcasted_iota(jnp.int32, (CH, CH), 1)
    return (r <= c, r < c) if reverse else (r >= c, r > c)


def _cumsum_time(incl, x):
    m = jnp.where(incl, 1.0, 0.0).astype(BF16)
    hi = x.astype(BF16)
    lo = (x - hi.astype(F32)).astype(BF16)
    return _dot(m, hi) + _dot(m, lo)


def _dir_specs(nblk, width, col_block):
    return [pl.BlockSpec((CH, width), lambda n: (n, col_block)),
            pl.BlockSpec((CH, width), lambda n: (nblk - 1 - n, col_block))]


def _whole(a):
    return pl.BlockSpec(a.shape, lambda n: tuple(0 for _ in a.shape))


def _ret_consts():
    log_gamma = np.log(1.0 - 2.0 ** (-5.0 - np.arange(N_HEADS, dtype=np.float64)))
    pos = np.arange(CH, dtype=np.float64)
    diff = pos[:, None] - pos[None, :]
    dec_f = np.where(diff >= 0, np.exp(log_gamma[:, None, None] * diff), 0.0)
    dec = np.stack([dec_f, np.transpose(dec_f, (0, 2, 1))])
    qs_f = np.exp(log_gamma[None, :] * (pos[:, None] + 1.0))
    ks_f = np.exp(log_gamma[None, :] * (CH - 1.0 - pos[:, None]))
    qs_b = np.exp(log_gamma[None, :] * (CH - pos[:, None]))
    ks_b = np.exp(log_gamma[None, :] * pos[:, None])
    rep = lambda a: np.repeat(a, RET_DK, axis=1)
    qs = np.stack([rep(qs_f), rep(qs_b)])
    ks = np.stack([rep(ks_f), rep(ks_b)])
    chunk_decay = [float(np.exp(lg * CH)) for lg in log_gamma]
    return (jnp.asarray(dec, F32), jnp.asarray(qs, F32), jnp.asarray(ks, F32)), chunk_decay


def _ret_kernel(*refs, rotary, chunk_decay):
    if rotary:
        (qf, qb, kf, kb, vf, vb, cosf, cosb, sinf, sinb, dec_ref, qs_ref, ks_ref, s0_ref,
         of_ref, ob_ref, s_ref) = refs
        tabs = ((cosf, sinf), (cosb, sinb))
    else:
        qf, qb, kf, kb, vf, vb, dec_ref, qs_ref, ks_ref, s0_ref, of_ref, ob_ref, s_ref = refs
        tabs = (None, None)

    @pl.when(pl.program_id(0) == 0)
    def _():
        s_ref[...] = s0_ref[...]

    lane = lax.broadcasted_iota(jnp.int32, (CH, N_HEADS * RET_DK), 1)
    first_half = (lane & (RET_DK - 1)) < RET_DK // 2

    chains = []
    for d, (q_ref, k_ref, v_ref, o_ref) in enumerate(((qf, kf, vf, of_ref), (qb, kb, vb, ob_ref))):
        q = q_ref[...].astype(F32)
        k = k_ref[...].astype(F32) * RET_DK ** -0.5
        if rotary:
            cos, sin = tabs[d][0][...], tabs[d][1][...]

            def rot(x):
                swapped = jnp.where(first_half, pltpu.roll(x, N_HEADS * RET_DK - RET_DK // 2, 1),
                                    pltpu.roll(x, RET_DK // 2, 1))
                return x * cos + swapped * sin

            q, k = rot(q), rot(k)
        qb16, kb16 = q.astype(BF16), k.astype(BF16)
        q_in = (q * qs_ref[d]).astype(BF16)
        k_out = k * ks_ref[d]
        k_t = [k_out[:, :128].T, k_out[:, 128:].T]
        for h in range(N_HEADS):
            sl = slice(h * RET_DK, (h + 1) * RET_DK)
            chains.append(dict(
                d=d, h=h, o_ref=o_ref, vh=v_ref[:, h * HEAD_V:(h + 1) * HEAD_V], q=qb16[:, sl], k=kb16[:, sl],
                q_in=q_in[:, sl], k_th=k_t[h // 2][(h % 2) * RET_DK:(h % 2 + 1) * RET_DK, :].astype(BF16)))

    ps = [(_dot_nt(c["q"], c["k"]) * dec_ref[c["d"], c["h"]]).astype(BF16) for c in chains]
    states = [s_ref[c["d"], c["h"]] for c in chains]
    inters = [_dot(c["q_in"], s.astype(BF16)) for c, s in zip(chains, states)]
    updates = [_dot(c["k_th"], c["vh"]) for c in chains]
    for c, p, inter in zip(chains, ps, inters):
        h = c["h"]
        c["o_ref"][:, h * HEAD_V:(h + 1) * HEAD_V] = _dot(p, c["vh"]) + inter
    for c, s, update in zip(chains, states, updates):
        s_ref[c["d"], c["h"]] = chunk_decay[c["h"]] * s + update


def retention_scan(z, s0, tables):
    l = z.shape[0]
    nblk = l // CH
    consts, chunk_decay = _ret_consts()
    rotary = tables is not None
    w = N_HEADS * RET_DK
    specs = (_dir_specs(nblk, w, OFF_RET_Q // w) + _dir_specs(nblk, w, OFF_RET_K // w)
             + _dir_specs(nblk, BRANCH_W, OFF_RET_V // BRANCH_W))
    args = [z] * 6
    if rotary:
        specs += _dir_specs(nblk, w, 0) + _dir_specs(nblk, w, 0)
        args += [tables[0], tables[0], tables[1], tables[1]]
    specs += [_whole(c) for c in consts] + [_whole(s0)]
    args += list(consts) + [s0]
    return pl.pallas_call(
        functools.partial(_ret_kernel, rotary=rotary, chunk_decay=chunk_decay),
        out_shape=(jax.ShapeDtypeStruct((l, BRANCH_W), F32), jax.ShapeDtypeStruct((l, BRANCH_W), F32),
                   jax.ShapeDtypeStruct(s0.shape, F32)),
        grid=(nblk,),
        in_specs=specs,
        out_specs=(pl.BlockSpec((CH, BRANCH_W), lambda n: (n, 0)),
                   pl.BlockSpec((CH, BRANCH_W), lambda n: (nblk - 1 - n, 0)), _whole(s0)),
        compiler_params=_cparams(("arbitrary",)),
        name="retention_scan",
    )(*args)


def _mlstm_kernel(qf, qb, kf, kb, vf, vb, smf, smb, bias_ref, c0_ref, m0_ref, of_ref, ob_ref, c_ref, m_ref):
    @pl.when(pl.program_id(0) == 0)
    def _():
        c_ref[...] = c0_ref[...]
        m_ref[...] = m0_ref[...]

    lane = lax.broadcasted_iota(jnp.int32, (CH, 128), 1)
    is_forget = (lane >= F_LANE) & (lane < F_LANE + 2 * N_HEADS)
    ones_col = jnp.where(lane == 0, 1.0, 0.0).astype(BF16)

    chains = []
    for d, (q_ref, k_ref, v_ref, sm_ref, o_ref) in enumerate(((qf, kf, vf, smf, of_ref), (qb, kb, vb, smb, ob_ref))):
        reverse = d == 1
        incl, _ = _masks(reverse)
        last = 0 if reverse else CH - 1
        pre = sm_ref[...] + bias_ref[0:1]
        x = jnp.where(is_forget, jax.nn.log_sigmoid(pre), pre)
        b = _cumsum_time(incl, x)
        x_t, b_t = x.T, b.T
        q = (q_ref[...].astype(F32) * MLSTM_DK ** -0.5).astype(BF16)
        kf32 = k_ref[...].astype(F32)
        k_t = [kf32[:, :128].T, kf32[:, 128:].T]
        for h in range(N_HEADS):
            gi, gf = I_LANE + d * N_HEADS + h, F_LANE + d * N_HEADS + h
            sl = slice(h * MLSTM_DK, (h + 1) * MLSTM_DK)
            b_col, b_row = b[:, gf:gf + 1], b_t[gf:gf + 1, :]
            li_row = x_t[gi:gi + 1, :]
            b_last = b[last:last + 1, gf:gf + 1]
            d_log = jnp.where(incl, b_col - b_row + li_row, NEG)
            m_intra = jnp.max(d_log, axis=1, keepdims=True)
            e_row = b_last - b_row + li_row
            chains.append(dict(
                d=d, h=h, o_ref=o_ref, q=q[:, sl], k=k_ref[:, sl], b_col=b_col, b_last=b_last, m_intra=m_intra,
                e_row=e_row, m_end=jnp.max(e_row, axis=1, keepdims=True),
                w_intra=jnp.exp(d_log - m_intra),
                k_th=k_t[h // 2][(h % 2) * MLSTM_DK:(h % 2 + 1) * MLSTM_DK, :],
                v_aug=jnp.concatenate([v_ref[:, h * HEAD_V:(h + 1) * HEAD_V], ones_col], axis=1)))

    ps = [(c["w_intra"] * _dot_nt(c["q"], c["k"])).astype(BF16) for c in chains]
    pvs = [_dot(p, c["v_aug"]) for c, p in zip(chains, ps)]
    c_augs = [c_ref[c["d"], c["h"]] for c in chains]
    m_ss = [m_ref[c["d"], c["h"]][0:1, 0:1] for c in chains]
    qcs = [_dot(c["q"], c_aug.astype(BF16)) for c, c_aug in zip(chains, c_augs)]
    m_news = [jnp.maximum(c["b_last"] + m_s, c["m_end"]) for c, m_s in zip(chains, m_ss)]
    k_ws = [(c["k_th"] * jnp.exp(c["e_row"] - m_new)).astype(BF16) for c, m_new in zip(chains, m_news)]
    updates = [_dot(k_w, c["v_aug"]) for c, k_w in zip(chains, k_ws)]
    for c, pv, qc, m_s in zip(chains, pvs, qcs, m_ss):
        h = c["h"]
        m_t = jnp.maximum(c["b_col"] + m_s, c["m_intra"])
        inter = jnp.exp(c["b_col"] + m_s - m_t)
        r = jnp.exp(c["m_intra"] - m_t)
        tot = inter * qc + r * pv
        den = jnp.maximum(jnp.abs(tot[:, HEAD_V:HEAD_V + 1]), jnp.exp(-m_t))
        c["o_ref"][:, h * HEAD_V:(h + 1) * HEAD_V] = tot[:, :HEAD_V] / den
    for c, c_aug, m_s, m_new, update in zip(chains, c_augs, m_ss, m_news, updates):
        c_ref[c["d"], c["h"]] = jnp.exp(c["b_last"] + m_s - m_new) * c_aug + update
        m_ref[c["d"], c["h"]] = jnp.broadcast_to(m_new, (8, 128))


def mlstm_scan(z, small, bias, c0, m0):
    l = z.shape[0]
    nblk = l // CH
    w = N_HEADS * MLSTM_DK
    specs = (_dir_specs(nblk, w, OFF_MLSTM_Q // w) + _dir_specs(nblk, w, OFF_MLSTM_K // w)
             + _dir_specs(nblk, BRANCH_W, OFF_MLSTM_V // BRANCH_W) + _dir_specs(nblk, 128, 0))
    specs += [_whole(bias), _whole(c0), _whole(m0)]
    return pl.pallas_call(
        _mlstm_kernel,
        out_shape=(jax.ShapeDtypeStruct((l, BRANCH_W), F32), jax.ShapeDtypeStruct((l, BRANCH_W), F32),
                   jax.ShapeDtypeStruct(c0.shape, F32), jax.ShapeDtypeStruct(m0.shape, F32)),
        grid=(nblk,),
        in_specs=specs,
        out_specs=(pl.BlockSpec((CH, BRANCH_W), lambda n: (n, 0)),
                   pl.BlockSpec((CH, BRANCH_W), lambda n: (nblk - 1 - n, 0)), _whole(c0), _whole(m0)),
        compiler_params=_cparams(("arbitrary",)),
        name="mlstm_scan",
    )(z, z, z, z, z, z, small, small, bias, c0, m0)


QKV_W = 3 * N_HEADS * GDN_DK
HALO = 8


def _gdn_prep_kernel(x_ref, prev_ref, next_ref, w_ref, q_ref, k_ref, v_ref):
    i = pl.program_id(0)
    tm = x_ref.shape[0]
    x = x_ref[...].astype(F32)
    prev = jnp.where(i > 0, prev_ref[...].astype(F32), 0.0)
    nxt = jnp.where(i < pl.num_programs(0) - 1, next_ref[...].astype(F32), 0.0)
    xe = jnp.concatenate([prev, x, nxt], axis=0)
    w = w_ref[...]
    y = None
    for tap in range(CONV_K):
        off = HALO + tap - CONV_K // 2
        term = w[tap:tap + 1] * xe[off:off + tm]
        y = term if y is None else y + term
    y = y * jax.nn.sigmoid(y)
    hw = N_HEADS * GDN_DK
    for h in range(N_HEADS):
        sl = slice(h * GDN_DK, (h + 1) * GDN_DK)
        qh = y[:, sl]
        kh = y[:, hw + h * GDN_DK:hw + (h + 1) * GDN_DK]
        q_ref[:, sl] = (qh * lax.rsqrt(jnp.sum(qh * qh, axis=-1, keepdims=True) + EPS) * GDN_DK ** -0.5).astype(BF16)
        k_ref[:, sl] = (kh * lax.rsqrt(jnp.sum(kh * kh, axis=-1, keepdims=True) + EPS)).astype(BF16)
    v_ref[...] = y[:, 2 * hw:].astype(BF16)


def gdn_prep(z, conv_w):
    l = z.shape[0]
    tm = min(l, 256)
    nb = l // tm
    r8 = tm // HALO
    cb = OFF_GDN_QKV // QKV_W
    w8 = jnp.concatenate([conv_w.astype(F32), jnp.zeros((8 - CONV_K, QKV_W), F32)], axis=0)
    return pl.pallas_call(
        _gdn_prep_kernel,
        out_shape=tuple(jax.ShapeDtypeStruct((l, BRANCH_W), BF16) for _ in range(3)),
        grid=(nb,),
        in_specs=[pl.BlockSpec((tm, QKV_W), lambda i: (i, cb)),
                  pl.BlockSpec((HALO, QKV_W), lambda i: (jnp.maximum(i * r8 - 1, 0), cb)),
                  pl.BlockSpec((HALO, QKV_W), lambda i: (jnp.minimum((i + 1) * r8, nb * r8 - 1), cb)),
                  pl.BlockSpec((8, QKV_W), lambda i: (0, 0))],
        out_specs=tuple(pl.BlockSpec((tm, BRANCH_W), lambda i: (i, 0)) for _ in range(3)),
        compiler_params=_cparams(("parallel",)),
        name="gdn_prep",
    )(z, z, z, w8)


N_LEVELS = 7


def _gdn_kernel(qf, qb, kf, kb, vf, vb, smf, smb, par_ref, s0_ref, of_ref, ob_ref, s_ref):
    @pl.when(pl.program_id(0) == 0)
    def _():
        s_ref[...] = s0_ref[...]

    lane = lax.broadcasted_iota(jnp.int32, (CH, 128), 1)
    is_decay = lane < B_LANE
    ri = lax.broadcasted_iota(jnp.int32, (CH, CH), 0)
    ci = lax.broadcasted_iota(jnp.int32, (CH, CH), 1)
    eye = jnp.where(ri == ci, 1.0, 0.0)
    pair_masks = [((ri >> (l + 1)) == (ci >> (l + 1))) & ((ri >> l) != (ci >> l)) for l in range(N_LEVELS)]

    chains = []
    for d, (q_ref, k_ref, v_ref, sm_ref, o_ref) in enumerate(((qf, kf, vf, smf, of_ref), (qb, kb, vb, smb, ob_ref))):
        reverse = d == 1
        incl, strict = _masks(reverse)
        last = 0 if reverse else CH - 1
        sm = sm_ref[...]
        log_a = -jnp.exp(par_ref[1:2]) * jax.nn.softplus(sm + par_ref[0:1])
        x = jnp.where(is_decay, log_a, jax.nn.sigmoid(sm))
        g = _cumsum_time(incl, x)
        g_t = g.T
        for h in range(N_HEADS):
            ga, gb = A_LANE + d * N_HEADS + h, B_LANE + d * N_HEADS + h
            sl = slice(h * GDN_DK, (h + 1) * GDN_DK)
            g_col, g_row = g[:, ga:ga + 1], g_t[ga:ga + 1, :]
            g_last = g[last:last + 1, ga:ga + 1]
            beta = x[:, gb:gb + 1]
            decay = jnp.where(incl, jnp.exp(jnp.minimum(g_col - g_row, 0.0)), 0.0)
            kh = k_ref[:, sl]
            kf32 = kh.astype(F32)
            kbeta = kf32 * beta
            a = jnp.where(strict, _dot_nt(kbeta.astype(BF16), kh) * decay, 0.0)
            e_g = jnp.exp(g_col)
            rhs = jnp.concatenate([v_ref[:, sl].astype(F32) * beta, kbeta * e_g], axis=1)
            chains.append(dict(
                d=d, h=h, sl=sl, o_ref=o_ref, a=a, rhs=rhs, s_decay=jnp.exp(g_last),
                qk=(_dot_nt(q_ref[:, sl], kh) * decay).astype(BF16),
                q_in=(q_ref[:, sl].astype(F32) * e_g).astype(BF16),
                k_out_t=(kf32.T * jnp.exp(g_last - g_row)).astype(BF16)))

    xs = [eye - jnp.where(pair_masks[0], c["a"], 0.0) for c in chains]
    for pm in pair_masks[1:]:
        ys = [_dot(jnp.where(pm, c["a"], 0.0).astype(BF16), x.astype(BF16)) for c, x in zip(chains, xs)]
        xs = [x - _dot(x.astype(BF16), y.astype(BF16)) for x, y in zip(xs, ys)]
    sols = [c["rhs"] + _dot((x - eye).astype(BF16), c["rhs"].astype(BF16)) for c, x in zip(chains, xs)]
    states = [s_ref[c["d"], c["h"]] for c in chains]
    s16s = [s.astype(BF16) for s in states]
    v_news = [(sol[:, :HEAD_V] - _dot(sol[:, HEAD_V:].astype(BF16), s16)).astype(BF16) for sol, s16 in zip(sols, s16s)]
    updates = [_dot(c["k_out_t"], v_new) for c, v_new in zip(chains, v_news)]
    for c, s16, v_new in zip(chains, s16s, v_news):
        c["o_ref"][:, c["sl"]] = _dot(c["q_in"], s16) + _dot(c["qk"], v_new)
    for c, s, update in zip(chains, states, updates):
        s_ref[c["d"], c["h"]] = s * c["s_decay"] + update


def gdn_scan(qn, kn, vn, small, par, s0):
    l = qn.shape[0]
    nblk = l // CH
    specs = _dir_specs(nblk, BRANCH_W, 0) * 3 + _dir_specs(nblk, 128, 0)
    specs += [_whole(par), _whole(s0)]
    return pl.pallas_call(
        _gdn_kernel,
        out_shape=(jax.ShapeDtypeStruct((l, BRANCH_W), F32), jax.ShapeDtypeStruct((l, BRANCH_W), F32),
                   jax.ShapeDtypeStruct(s0.shape, F32)),
        grid=(nblk,),
        in_specs=specs,
        out_specs=(pl.BlockSpec((CH, BRANCH_W), lambda n: (n, 0)),
                   pl.BlockSpec((CH, BRANCH_W), lambda n: (nblk - 1 - n, 0)), _whole(s0)),
        compiler_params=_cparams(("arbitrary",)),
        name="gdn_scan",
    )(qn, qn, kn, kn, vn, vn, small, small, par, s0)


def _post_kernel(of_ref, ob_ref, gate_ref, g_ref, o_ref, *, centre, silu_gate):
    o = of_ref[...] + ob_ref[...]
    gate = gate_ref[...].astype(F32)
    sig = jax.nn.sigmoid(gate)
    act = gate * sig if silu_gate else sig
    for h in range(N_HEADS):
        sl = slice(h * HEAD_V, (h + 1) * HEAD_V)
        oh = o[:, sl]
        if centre:
            oh = oh - jnp.mean(oh, axis=-1, keepdims=True)
        oh = oh * lax.rsqrt(jnp.mean(oh * oh, axis=-1, keepdims=True) + EPS)
        o_ref[:, sl] = (oh * g_ref[:, sl] * act[:, sl]).astype(BF16)


def head_post(o_f, o_b, z, gate_off, norm_g, centre, silu_gate):
    l = o_f.shape[0]
    tm = min(l, 1024)
    row = pl.BlockSpec((tm, BRANCH_W), lambda i: (i, 0))
    return pl.pallas_call(
        functools.partial(_post_kernel, centre=centre, silu_gate=silu_gate),
        out_shape=jax.ShapeDtypeStruct((l, BRANCH_W), BF16),
        grid=(l // tm,),
        in_specs=[row, row, pl.BlockSpec((tm, BRANCH_W), lambda i: (i, gate_off // BRANCH_W)),
                  pl.BlockSpec((1, BRANCH_W), lambda i: (0, 0))],
        out_specs=row,
        compiler_params=_cparams(("parallel",)),
        name="head_post",
    )(o_f, o_b, z, norm_g.reshape(1, BRANCH_W).astype(F32))


def _lane_rows(entries):
    r = jnp.zeros((8, 128), F32)
    for row, lane, vals in entries:
        r = r.at[row, lane:lane + vals.shape[0]].set(vals.astype(F32))
    return r


def gdn_branch(zc, sc, zl, sl, conv_w, a_log, dt_bias, norm_g, with_ctx):
    par = _lane_rows([(0, A_LANE, dt_bias.reshape(-1)), (1, A_LANE, a_log.reshape(-1))])
    s0 = jnp.zeros((2, N_HEADS, GDN_DK, HEAD_V), F32)
    ofc, obc, s1 = gdn_scan(*gdn_prep(zc, conv_w), sc, par, s0)
    ofl, obl, _ = gdn_scan(*gdn_prep(zl, conv_w), sl, par, s1)
    out_c = head_post(ofc, obc, zc, OFF_GDN_Z, norm_g, False, True) if with_ctx else None
    return out_c, head_post(ofl, obl, zl, OFF_GDN_Z, norm_g, False, True)


def mlstm_branch(zc, sc, zl, sl, gate_b, norm_g, with_ctx):
    bias = _lane_rows([(0, I_LANE, gate_b[0].reshape(-1)), (0, F_LANE, gate_b[1].reshape(-1))])
    c0 = jnp.zeros((2, N_HEADS, MLSTM_DK, 2 * HEAD_V), F32)
    m0 = jnp.zeros((2, N_HEADS, 8, 128), F32)
    ofc, obc, c1, m1 = mlstm_scan(zc, sc, bias, c0, m0)
    ofl, obl, _, _ = mlstm_scan(zl, sl, bias, c1, m1)
    out_c = head_post(ofc, obc, zc, OFF_MLSTM_O, norm_g, False, False) if with_ctx else None
    return out_c, head_post(ofl, obl, zl, OFF_MLSTM_O, norm_g, False, False)


def retention_branch(zc, zl, tables, norm_g, with_ctx):
    s0 = jnp.zeros((2, N_HEADS, RET_DK, HEAD_V), F32)
    ofc, obc, s1 = retention_scan(zc, s0, None)
    ofl, obl, _ = retention_scan(zl, s1, tables)
    out_c = head_post(ofc, obc, zc, OFF_RET_G, norm_g, True, True) if with_ctx else None
    return out_c, head_post(ofl, obl, zl, OFF_RET_G, norm_g, True, True)


def _ret_rope_tables(n_lat):
    inv = ROPE_BASE ** (-jnp.arange(0, RET_DK, 2, dtype=F32) / RET_DK)
    ang = jnp.arange(n_lat, dtype=F32)[:, None] * inv[None, :]
    cos = jnp.concatenate([jnp.cos(ang), jnp.cos(ang)], axis=-1)
    sin = jnp.concatenate([-jnp.sin(ang), jnp.sin(ang)], axis=-1)
    return jnp.tile(cos, (1, N_HEADS)), jnp.tile(sin, (1, N_HEADS))


def _rope_tables(n_lat):
    q = MLA_ROPE // 4
    inv = ROPE_BASE ** (-jnp.arange(0, 2 * q, 2, dtype=F32) / (2 * q))
    t = jnp.arange(n_lat)
    row = (t // GRID_W).astype(F32)[:, None] * inv[None, :]
    col = (t % GRID_W).astype(F32)[:, None] * inv[None, :]
    cos = jnp.concatenate([jnp.cos(row), jnp.cos(row), jnp.cos(col), jnp.cos(col)], axis=-1)
    sin = jnp.concatenate([-jnp.sin(row), jnp.sin(row), -jnp.sin(col), jnp.sin(col)], axis=-1)
    return cos, sin


def _pad_rows(v, n=8):
    rows = [jnp.reshape(r, (1, -1)).astype(F32) for r in v]
    d = rows[0].shape[1]
    return jnp.concatenate(rows + [jnp.zeros((n - len(rows), d), F32)], axis=0)


def _mla_weights(w_uq, w_ukv):
    wq = w_uq.reshape(MLA_Q_RANK, N_HEADS, MLA_QK)
    rope = wq[:, :, MLA_NOPE:]
    swapped = rope[:, :, _rope_swap(np.arange(MLA_ROPE))]
    wq_ext = jnp.concatenate([wq[:, :, :MLA_NOPE], rope, swapped], axis=-1).reshape(MLA_Q_RANK, N_HEADS * 256)
    return wq_ext.astype(BF16), w_ukv.astype(BF16)


def _moe(x_l, pv2, g2, w_router, b_router, w_gu, w_down, final_g):
    n = x_l.shape[0]
    w_pad = jnp.concatenate([w_router, jnp.zeros((D_MODEL, 128 - N_EXPERTS), F32)], axis=1)
    b_pad = jnp.concatenate([b_router, jnp.full((128 - N_EXPERTS,), -1e30, F32)]).reshape(1, 128)
    h2, logits = moe_router(x_l, pv2, w_pad, b_pad)
    top_logit, top_e = lax.top_k(logits[:, :N_EXPERTS], TOP_K)
    weight = jax.nn.softmax(top_logit, axis=-1)
    n_assign = n * TOP_K
    flat_e = top_e.reshape(-1)
    onehot = (flat_e[:, None] == jnp.arange(N_EXPERTS)[None, :]).astype(jnp.int32)
    csum = jnp.cumsum(onehot, axis=0)
    counts = csum[-1]
    rank = jnp.take_along_axis(csum, flat_e[:, None], axis=1)[:, 0] - 1
    padded = (counts + MOE_ROWS - 1) // MOE_ROWS * MOE_ROWS
    pad_end = jnp.cumsum(padded)
    dest = (pad_end - padded)[flat_e] + rank
    nb = n_assign // MOE_ROWS + N_EXPERTS
    cap = nb * MOE_ROWS
    slot_tok = jnp.zeros((cap,), jnp.int32).at[dest].set(jnp.arange(n_assign, dtype=jnp.int32) // TOP_K)
    block_e = jnp.minimum(jnp.searchsorted(pad_end, jnp.arange(nb) * MOE_ROWS, side='right'),
                          N_EXPERTS - 1).astype(jnp.int32)
    n_used = (pad_end[-1] // MOE_ROWS).astype(jnp.int32).reshape(1)
    xb = h2[slot_tok]
    yb = expert_ffn(block_e, n_used, xb, w_gu, w_down)
    d2 = dest.reshape(n, TOP_K)
    w128 = jnp.concatenate([weight, jnp.zeros((n, 126), F32)], axis=1)
    return final_combine(x_l, yb[d2[:, 0]], yb[d2[:, 1]], w128, _pad_rows([g2, final_g]))


def kernel(x, c, ctx, c_ctx, w_mod, b_mod, norm1_g, norm2_g, w_in, gdn_conv_w, gdn_a_log, gdn_dt_bias, gdn_norm_g, mla_q_norm_g, mla_kv_norm_g, mla_w_uq, mla_w_ukv, mlstm_gate_b, mlstm_norm_g, ret_norm_g, w_branch, w_out, ffn_w_in, ffn_w_down, moe_w_router, moe_b_router, moe_w_in, moe_w_down, final_norm_g):
    n_lat = x.shape[1]
    n_ctx = ctx.shape[1]
    x_l, x_c = x[0], ctx[0]
    cond = _pad_rows([c_ctx, c[0]])
    cos_l, sin_l = _rope_tables(n_lat)
    cos_c, sin_c = jnp.ones((n_ctx, MLA_ROPE), F32), jnp.zeros((n_ctx, MLA_ROPE), F32)
    ret_tables = _ret_rope_tables(n_lat)
    out = None
    for li in range(DEPTH):
        last = li == DEPTH - 1
        mod = modulation_vectors(cond, w_mod[li], b_mod[li])
        csh1, csc1, cg1, csh2, csc2, cg2 = jnp.split(mod[0], 6)
        sh1, sc1, g1, sh2, sc2, g2 = jnp.split(mod[1], 6)
        w_main = w_in[li][:, _MAIN_PERM].astype(BF16)
        w_small = w_in[li][:, _SMALL_PERM].astype(BF16)
        pv_l = _pad_rows([norm1_g[li], 1 + sc1, sh1])
        pv_c = _pad_rows([norm1_g[li], 1 + csc1, csh1])
        zl = norm_proj(x_l, pv_l, w_main, 768, BF16)
        zc = norm_proj(x_c, pv_c, w_main, 768, BF16)
        sl = norm_proj(x_l, pv_l, w_small, 128, F32)
        sc = norm_proj(x_c, pv_c, w_small, 128, F32)

        a_c, a_l = gdn_branch(zc, sc, zl, sl, gdn_conv_w[li], gdn_a_log[li], gdn_dt_bias[li], gdn_norm_g[li], not last)
        c_c, c_l = mlstm_branch(zc, sc, zl, sl, mlstm_gate_b[li], mlstm_norm_g[li], not last)
        d_c, d_l = retention_branch(zc, zl, ret_tables, ret_norm_g[li], not last)

        wq_ext, wkv = _mla_weights(mla_w_uq[li], mla_w_ukv[li])
        gq, gkv = mla_q_norm_g[li].reshape(1, -1), mla_kv_norm_g[li].reshape(1, -1)
        ql, kl, vl = mla_project(zl, cos_l, sin_l, gq, gkv, wq_ext, wkv)
        qc, kc, vc = mla_project(zc, cos_c, sin_c, gq, gkv, wq_ext, wkv)
        b_l = attention(ql, [(kl, vl), (kc, vc)])

        wb = w_branch[li].astype(BF16)
        wo = w_out[li].astype(BF16)
        x_l = merge_branches(x_l, zl, (a_l, b_l, c_l, d_l), wb, wo, _pad_rows([g1]))
        if not last:
            b_c = attention(qc, [(kc, vc)])
            x_c = merge_branches(x_c, zc, (a_c, b_c, c_c, d_c), wb, wo, _pad_rows([cg1]))

        if li % 2 == 0:
            w_gu = ffn_w_in[li // 2].astype(BF16)
            w_dn = ffn_w_down[li // 2].astype(BF16)
            x_l = dense_ffn(x_l, _pad_rows([norm2_g[li], 1 + sc2, sh2, g2]), w_gu, w_dn)
            if not last:
                x_c = dense_ffn(x_c, _pad_rows([norm2_g[li], 1 + csc2, csh2, cg2]), w_gu, w_dn)
            if last:
                out = final_combine(x_l, jnp.zeros_like(x_l), jnp.zeros_like(x_l),
                                    jnp.zeros((n_lat, 128), F32), _pad_rows([jnp.zeros_like(g2), final_norm_g]))
        else:
            assert last
            out = _moe(x_l, _pad_rows([norm2_g[li], 1 + sc2, sh2]), g2, moe_w_router[li // 2],
                       moe_b_router[li // 2], moe_w_in[li // 2], moe_w_down[li // 2], final_norm_g)
    return out[None]
```

```python
import functools
import math

import numpy as np
import jax
import jax.numpy as jnp
from jax import lax
from jax.experimental import pallas as pl
from jax.experimental.pallas import tpu as pltpu

F32 = jnp.float32
BF16 = jnp.bfloat16

D_MODEL = 1024
DEPTH = 2
GRID_W = 64
N_BRANCH = 4
N_HEADS = 4
HEAD_V = 128
BRANCH_W = N_HEADS * HEAD_V
GDN_DK = 128
CONV_K = 5
MLA_Q_RANK = 384
MLA_KV_RANK = 256
MLA_NOPE = 128
MLA_ROPE = 64
MLA_QK = MLA_NOPE + MLA_ROPE
MLA_QK_PAD = 256
MLA_V_PAD = 256
MLSTM_DK = 64
RET_DK = 64
ROPE_BASE = 10000.0
D_FF = 3584
N_EXPERTS = 8
TOP_K = 2
EPS = 1e-6

IN_WIDTHS = (
    N_BRANCH * D_MODEL,
    N_HEADS * GDN_DK, N_HEADS * GDN_DK, BRANCH_W, BRANCH_W, 2 * N_HEADS, 2 * N_HEADS,
    MLA_Q_RANK, MLA_KV_RANK, MLA_ROPE,
    N_HEADS * MLSTM_DK, N_HEADS * MLSTM_DK, BRANCH_W, BRANCH_W, 2 * N_HEADS, 2 * N_HEADS,
    N_HEADS * RET_DK, N_HEADS * RET_DK, BRANCH_W, BRANCH_W,
)
_IN_OFF = [0] + [int(o) for o in np.cumsum(IN_WIDTHS)]

VMEM_LIMIT = 48 * 1024 * 1024
MOE_ROWS = 1024


def _cols(group):
    return np.arange(_IN_OFF[group], _IN_OFF[group + 1])


def _rope_swap(cols):
    q = MLA_ROPE // 4
    return np.concatenate([cols[q:2 * q], cols[:q], cols[3 * q:], cols[2 * q:3 * q]])


_MAIN_PERM = np.concatenate([
    _cols(1), _cols(2), _cols(3),
    _cols(4),
    _cols(0),
    _cols(12), _cols(13),
    _cols(10), _cols(11),
    _cols(18), _cols(19),
    _cols(16), _cols(17),
    _cols(7), _cols(8), _cols(9), _rope_swap(_cols(9)),
])
N_MAIN = int(_MAIN_PERM.shape[0])
_SMALL_PERM = np.concatenate([_cols(5), _cols(6), _cols(14), _cols(15), np.zeros(96, np.int64)])
A_LANE, B_LANE, I_LANE, F_LANE = 0, 8, 16, 24
OFF_GDN_QKV, OFF_GDN_Z, OFF_GATE = 0, 1536, 2048
OFF_MLSTM_V, OFF_MLSTM_O, OFF_MLSTM_Q, OFF_MLSTM_K = 6144, 6656, 7168, 7424
OFF_RET_V, OFF_RET_G, OFF_RET_Q, OFF_RET_K = 7680, 8192, 8704, 8960
OFF_MLA = 9216
MLA_IN_W = 768


def _cparams(sem):
    return pltpu.CompilerParams(dimension_semantics=sem, vmem_limit_bytes=VMEM_LIMIT)


def _rms(x):
    return x * lax.rsqrt(jnp.mean(x * x, axis=-1, keepdims=True) + EPS)


def _dot(a, b):
    return jnp.dot(a, b, preferred_element_type=F32)


def _dot_nt(a, b):
    return lax.dot_general(a, b, (((1,), (1,)), ((), ())), preferred_element_type=F32)


def _mod_kernel(c_ref, w_ref, b_ref, o_ref):
    c = c_ref[...]
    s = c * jax.nn.sigmoid(c)
    o_ref[...] = jnp.dot(s, w_ref[...], preferred_element_type=F32) + b_ref[...]


def modulation_vectors(cond, w_mod, b_mod):
    n = w_mod.shape[1]
    tn = 1536
    return pl.pallas_call(
        _mod_kernel,
        out_shape=jax.ShapeDtypeStruct((8, n), F32),
        grid=(n // tn,),
        in_specs=[pl.BlockSpec((8, D_MODEL), lambda j: (0, 0)),
                  pl.BlockSpec((D_MODEL, tn), lambda j: (0, j)),
                  pl.BlockSpec((1, tn), lambda j: (0, j))],
        out_specs=pl.BlockSpec((8, tn), lambda j: (0, j)),
        compiler_params=_cparams(("arbitrary",)),
        name="modulation",
    )(cond, w_mod, b_mod.reshape(1, n))


def _norm_proj_kernel(x_ref, pv_ref, w_ref, o_ref, h_ref):
    @pl.when(pl.program_id(1) == 0)
    def _():
        pv = pv_ref[...]
        h_ref[...] = (_rms(x_ref[...]) * pv[0:1] * pv[1:2] + pv[2:3]).astype(BF16)

    o_ref[...] = _dot(h_ref[...], w_ref[...]).astype(o_ref.dtype)


def norm_proj(x, pv, w, tn, out_dtype):
    m, d = x.shape
    n = w.shape[1]
    tm = min(m, 1024)
    return pl.pallas_call(
        _norm_proj_kernel,
        out_shape=jax.ShapeDtypeStruct((m, n), out_dtype),
        grid=(m // tm, n // tn),
        in_specs=[pl.BlockSpec((tm, d), lambda i, j: (i, 0)),
                  pl.BlockSpec((8, d), lambda i, j: (0, 0)),
                  pl.BlockSpec((d, tn), lambda i, j: (0, j))],
        out_specs=pl.BlockSpec((tm, tn), lambda i, j: (i, j)),
        scratch_shapes=[pltpu.VMEM((tm, d), BF16)],
        compiler_params=_cparams(("parallel", "arbitrary")),
        name="norm_proj",
    )(x, pv, w)


def _mla_proj_kernel(z_ref, cos_ref, sin_ref, gq_ref, gkv_ref, wq_ref, wkv_ref, q_ref, k_ref, v_ref):
    z = z_ref[...].astype(F32)
    tm = z.shape[0]
    cq = z[:, :MLA_Q_RANK]
    ckv = z[:, MLA_Q_RANK:MLA_Q_RANK + MLA_KV_RANK]
    kr = z[:, MLA_Q_RANK + MLA_KV_RANK:]
    cos = cos_ref[...]
    sin = sin_ref[...]
    qn = (_rms(cq) * gq_ref[...]).astype(BF16)
    kvn = (_rms(ckv) * gkv_ref[...]).astype(BF16)
    qf = _dot(qn, wq_ref[...]) * (MLA_QK ** -0.5 * math.log2(math.e))
    kvf = _dot(kvn, wkv_ref[...])
    kr_rot = kr[:, :MLA_ROPE] * cos + kr[:, MLA_ROPE:] * sin
    pad = jnp.zeros((tm, MLA_QK_PAD - MLA_QK), F32)
    lane = lax.broadcasted_iota(jnp.int32, (tm, MLA_V_PAD - HEAD_V), 1)
    ones_col = jnp.where(lane == 0, 1.0, 0.0).astype(BF16)
    for h in range(N_HEADS):
        b = h * 256
        q_rot = qf[:, b + 128:b + 192] * cos + qf[:, b + 192:b + 256] * sin
        q_ref[h] = jnp.concatenate([qf[:, b:b + 128], q_rot, pad], axis=-1).astype(BF16)
        k_ref[h] = jnp.concatenate([kvf[:, b:b + 128], kr_rot, pad], axis=-1).astype(BF16)
        v_ref[h] = jnp.concatenate([kvf[:, b + 128:b + 256].astype(BF16), ones_col], axis=-1)


def mla_project(zmain, cos, sin, gq, gkv, wq_ext, wkv):
    m = zmain.shape[0]
    tm = min(m, 1024)
    full = lambda shape: pl.BlockSpec(shape, lambda i: tuple(0 for _ in shape))
    return pl.pallas_call(
        _mla_proj_kernel,
        out_shape=(jax.ShapeDtypeStruct((N_HEADS, m, MLA_QK_PAD), BF16),
                   jax.ShapeDtypeStruct((N_HEADS, m, MLA_QK_PAD), BF16),
                   jax.ShapeDtypeStruct((N_HEADS, m, MLA_V_PAD), BF16)),
        grid=(m // tm,),
        in_specs=[pl.BlockSpec((tm, MLA_IN_W), lambda i: (i, OFF_MLA // MLA_IN_W)),
                  pl.BlockSpec((tm, MLA_ROPE), lambda i: (i, 0)),
                  pl.BlockSpec((tm, MLA_ROPE), lambda i: (i, 0)),
                  full((1, MLA_Q_RANK)), full((1, MLA_KV_RANK)),
                  full((MLA_Q_RANK, N_HEADS * 256)), full((MLA_KV_RANK, N_HEADS * 256))],
        out_specs=(pl.BlockSpec((N_HEADS, tm, MLA_QK_PAD), lambda i: (0, i, 0)),
                   pl.BlockSpec((N_HEADS, tm, MLA_QK_PAD), lambda i: (0, i, 0)),
                   pl.BlockSpec((N_HEADS, tm, MLA_V_PAD), lambda i: (0, i, 0))),
        compiler_params=_cparams(("parallel",)),
        name="mla_project",
    )(zmain, cos, sin, gq, gkv, wq_ext, wkv)


ATTN_TQ, ATTN_TK = 512, 512
ATTN_UNROLL = 16


def _attn_kernel(*refs, segs):
    q_ref, o_ref = refs[0], refs[-1]
    q = q_ref[0]
    tq = q.shape[0]
    carry = (jnp.full((tq, 1), -1e30, F32), jnp.zeros((tq, MLA_V_PAD), F32))
    for si, (tk, nk) in enumerate(segs):
        k_ref, v_ref = refs[1 + 2 * si], refs[2 + 2 * si]

        def body(c, carry, k_ref=k_ref, v_ref=v_ref, tk=tk):
            m, acc = carry
            start = pl.multiple_of(c * tk, tk)
            s = _dot_nt(q, k_ref[0, pl.ds(start, tk), :])
            m_new = jnp.maximum(m, jnp.max(s, axis=-1, keepdims=True))
            p = jnp.exp2(s - m_new).astype(BF16)
            acc = jnp.exp2(m - m_new) * acc + _dot(p, v_ref[0, pl.ds(start, tk), :])
            return m_new, acc

        carry = lax.fori_loop(0, nk, body, carry, unroll=min(ATTN_UNROLL, nk))
    _, acc = carry
    o_ref[...] = (acc[:, :HEAD_V] / acc[:, HEAD_V:HEAD_V + 1]).astype(o_ref.dtype)


def attention(q, kvs):
    _, lq, _ = q.shape
    tq = min(lq, ATTN_TQ)
    segs, args, specs = [], [], []
    for k, v in kvs:
        lk = k.shape[1]
        tk = min(lk, ATTN_TK)
        segs.append((tk, lk // tk))
        args += [k, v]
        specs += [pl.BlockSpec((1, lk, MLA_QK_PAD), lambda h, i: (h, 0, 0)),
                  pl.BlockSpec((1, lk, MLA_V_PAD), lambda h, i: (h, 0, 0))]
    return pl.pallas_call(
        functools.partial(_attn_kernel, segs=tuple(segs)),
        out_shape=jax.ShapeDtypeStruct((lq, N_HEADS * HEAD_V), BF16),
        grid=(N_HEADS, lq // tq),
        in_specs=[pl.BlockSpec((1, tq, MLA_QK_PAD), lambda h, i: (h, i, 0))] + specs,
        out_specs=pl.BlockSpec((tq, HEAD_V), lambda h, i: (i, h)),
        compiler_params=_cparams(("parallel", "arbitrary")),
        name="attention",
    )(q, *args)


def _merge_kernel(x_ref, g0_ref, g1_ref, g2_ref, g3_ref, a_ref, b_ref, c_ref, d_ref, wb_ref, wo_ref, g_ref, o_ref):
    s = None
    for n, (br, gate_ref) in enumerate(zip((a_ref, b_ref, c_ref, d_ref), (g0_ref, g1_ref, g2_ref, g3_ref))):
        proj = _dot(br[...], wb_ref[n])
        gate = jax.nn.sigmoid(gate_ref[...].astype(F32))
        s = gate * proj if s is None else s + gate * proj
    m = _dot(s.astype(BF16), wo_ref[...])
    o_ref[...] = x_ref[...] + g_ref[0:1] * m


def merge_branches(x, zmain, branches, w_branch, w_out, gvec):
    m = x.shape[0]
    tm = min(m, 512)
    row = lambda w: pl.BlockSpec((tm, w), lambda i: (i, 0))
    return pl.pallas_call(
        _merge_kernel,
        out_shape=jax.ShapeDtypeStruct((m, D_MODEL), F32),
        grid=(m // tm,),
        in_specs=[row(D_MODEL)]
                 + [pl.BlockSpec((tm, D_MODEL), lambda i, n=n: (i, OFF_GATE // D_MODEL + n)) for n in range(N_BRANCH)]
                 + [row(BRANCH_W)] * 4
                 + [pl.BlockSpec((N_BRANCH, BRANCH_W, D_MODEL), lambda i: (0, 0, 0)),
                    pl.BlockSpec((D_MODEL, D_MODEL), lambda i: (0, 0)),
                    pl.BlockSpec((8, D_MODEL), lambda i: (0, 0))],
        out_specs=row(D_MODEL),
        compiler_params=_cparams(("parallel",)),
        name="merge_branches",
    )(x, zmain, zmain, zmain, zmain, *branches, w_branch, w_out, gvec)


def _ffn_kernel(x_ref, pv_ref, wg_ref, wu_ref, wd_ref, o_ref, h_ref, acc_ref):
    f = pl.program_id(1)

    @pl.when(f == 0)
    def _():
        pv = pv_ref[...]
        h_ref[...] = (_rms(x_ref[...]) * pv[0:1] * pv[1:2] + pv[2:3]).astype(BF16)
        acc_ref[...] = jnp.zeros_like(acc_ref)

    h = h_ref[...]
    g = _dot(h, wg_ref[...])
    u = _dot(h, wu_ref[...])
    act = (g * jax.nn.sigmoid(g) * u).astype(BF16)
    acc_ref[...] += _dot(act, wd_ref[...])

    @pl.when(f == pl.num_programs(1) - 1)
    def _():
        o_ref[...] = x_ref[...] + pv_ref[3:4] * acc_ref[...]


def dense_ffn(x, pv, w_gu, w_down):
    m = x.shape[0]
    tm = min(m, 1024)
    tf = 512
    nf = D_FF // tf
    return pl.pallas_call(
        _ffn_kernel,
        out_shape=jax.ShapeDtypeStruct((m, D_MODEL), F32),
        grid=(m // tm, nf),
        in_specs=[pl.BlockSpec((tm, D_MODEL), lambda i, f: (i, 0)),
                  pl.BlockSpec((8, D_MODEL), lambda i, f: (0, 0)),
                  pl.BlockSpec((D_MODEL, tf), lambda i, f: (0, f)),
                  pl.BlockSpec((D_MODEL, tf), lambda i, f: (0, f + nf)),
                  pl.BlockSpec((tf, D_MODEL), lambda i, f: (f, 0))],
        out_specs=pl.BlockSpec((tm, D_MODEL), lambda i, f: (i, 0)),
        scratch_shapes=[pltpu.VMEM((tm, D_MODEL), BF16), pltpu.VMEM((tm, D_MODEL), F32)],
        compiler_params=_cparams(("parallel", "arbitrary")),
        name="dense_ffn",
    )(x, pv, w_gu, w_gu, w_down)


def _router_kernel(x_ref, pv_ref, w_ref, b_ref, h_ref, logit_ref):
    pv = pv_ref[...]
    h = _rms(x_ref[...]) * pv[0:1] * pv[1:2] + pv[2:3]
    h_hi = h.astype(BF16)
    h_lo = (h - h_hi.astype(F32)).astype(BF16)
    w = w_ref[...]
    w_hi = w.astype(BF16)
    w_lo = (w - w_hi.astype(F32)).astype(BF16)
    logits = _dot(h_hi, w_hi) + _dot(h_hi, w_lo) + _dot(h_lo, w_hi)
    h_ref[...] = h
    logit_ref[...] = logits + b_ref[...]


def moe_router(x, pv, w_router_pad, b_router_pad):
    m = x.shape[0]
    tm = min(m, 1024)
    return pl.pallas_call(
        _router_kernel,
        out_shape=(jax.ShapeDtypeStruct((m, D_MODEL), F32), jax.ShapeDtypeStruct((m, 128), F32)),
        grid=(m // tm,),
        in_specs=[pl.BlockSpec((tm, D_MODEL), lambda i: (i, 0)),
                  pl.BlockSpec((8, D_MODEL), lambda i: (0, 0)),
                  pl.BlockSpec((D_MODEL, 128), lambda i: (0, 0)),
                  pl.BlockSpec((1, 128), lambda i: (0, 0))],
        out_specs=(pl.BlockSpec((tm, D_MODEL), lambda i: (i, 0)), pl.BlockSpec((tm, 128), lambda i: (i, 0))),
        compiler_params=_cparams(("parallel",)),
        name="moe_router",
    )(x, pv, w_router_pad, b_router_pad)


def _expert_kernel(be_ref, nb_ref, x_ref, wg_ref, wu_ref, wd_ref, o_ref, acc_ref):
    b = pl.program_id(0)
    f = pl.program_id(1)

    @pl.when(b < nb_ref[0])
    def _():
        @pl.when(f == 0)
        def _():
            acc_ref[...] = jnp.zeros_like(acc_ref)

        x = x_ref[...].astype(BF16)
        g = _dot(x, wg_ref[0].astype(BF16))
        u = _dot(x, wu_ref[0].astype(BF16))
        act = (g * jax.nn.sigmoid(g) * u).astype(BF16)
        acc_ref[...] += _dot(act, wd_ref[0].astype(BF16))

        @pl.when(f == pl.num_programs(1) - 1)
        def _():
            o_ref[...] = acc_ref[...]

    @pl.when((b >= nb_ref[0]) & (f == pl.num_programs(1) - 1))
    def _():
        o_ref[...] = jnp.zeros_like(o_ref)


DISPATCH_TOKENS = 2048
DISPATCH_WINDOW = 32


def _dispatch_kernel(dest_ref, h_hbm, xb0_hbm, xb_hbm, sem):
    del xb0_hbm
    nt = dest_ref.shape[-1] // TOP_K
    base = pl.program_id(0) * nt

    def copies(t):
        src = h_hbm.at[pl.ds(base + t, 1)]
        return [pltpu.make_async_copy(src, xb_hbm.at[pl.ds(dest_ref[0, 0, TOP_K * t + j], 1)], sem)
                for j in range(TOP_K)]

    def issue(t, carry):
        for cp in copies(t):
            cp.start()

        @pl.when(t >= DISPATCH_WINDOW)
        def _():
            for cp in copies(t - DISPATCH_WINDOW):
                cp.wait()
        return carry

    def drain(t, carry):
        for cp in copies(t):
            cp.wait()
        return carry

    lax.fori_loop(0, nt, issue, 0)
    lax.fori_loop(nt - DISPATCH_WINDOW, nt, drain, 0)


def moe_dispatch(h, dest, cap):
    n = h.shape[0]
    nt = min(n, DISPATCH_TOKENS)
    dest3 = dest.reshape(n // nt, 1, nt * TOP_K)
    return pl.pallas_call(
        _dispatch_kernel,
        out_shape=jax.ShapeDtypeStruct((cap, D_MODEL), F32),
        grid=(n // nt,),
        in_specs=[pl.BlockSpec((1, 1, nt * TOP_K), lambda i: (i, 0, 0), memory_space=pltpu.SMEM),
                  pl.BlockSpec(memory_space=pl.ANY), pl.BlockSpec(memory_space=pl.ANY)],
        out_specs=pl.BlockSpec(memory_space=pl.ANY),
        scratch_shapes=[pltpu.SemaphoreType.DMA],
        input_output_aliases={2: 0},
        compiler_params=_cparams(("arbitrary",)),
        name="moe_dispatch",
    )(dest3, h, jnp.zeros((cap, D_MODEL), F32))


def expert_ffn(block_e, n_used, xb, w_gu, w_down):
    cap = xb.shape[0]
    nb = cap // MOE_ROWS
    tf = 512
    nf = D_FF // tf

    def live(b, nbr):
        return jnp.minimum(b, nbr[0] - 1)

    def fsel(b, f, nbr):
        return jnp.where(b < nbr[0], f, nf - 1)

    grid_spec = pltpu.PrefetchScalarGridSpec(
        num_scalar_prefetch=2,
        grid=(nb, nf),
        in_specs=[pl.BlockSpec((MOE_ROWS, D_MODEL), lambda b, f, be, nbr: (live(b, nbr), 0)),
                  pl.BlockSpec((1, D_MODEL, tf), lambda b, f, be, nbr: (be[live(b, nbr)], 0, fsel(b, f, nbr))),
                  pl.BlockSpec((1, D_MODEL, tf), lambda b, f, be, nbr: (be[live(b, nbr)], 0, fsel(b, f, nbr) + nf)),
                  pl.BlockSpec((1, tf, D_MODEL), lambda b, f, be, nbr: (be[live(b, nbr)], fsel(b, f, nbr), 0))],
        out_specs=pl.BlockSpec((MOE_ROWS, D_MODEL), lambda b, f, be, nbr: (b, 0)),
        scratch_shapes=[pltpu.VMEM((MOE_ROWS, D_MODEL), F32)],
    )
    return pl.pallas_call(
        _expert_kernel,
        out_shape=jax.ShapeDtypeStruct((cap, D_MODEL), F32),
        grid_spec=grid_spec,
        compiler_params=_cparams(("arbitrary", "arbitrary")),
        name="expert_ffn",
    )(block_e, n_used, xb, w_gu, w_gu, w_down)


def _final_kernel(x_ref, y0_ref, y1_ref, w_ref, pv_ref, o_ref):
    w = w_ref[...]
    f = w[:, 0:1] * y0_ref[...] + w[:, 1:2] * y1_ref[...]
    x = x_ref[...] + pv_ref[0:1] * f
    o_ref[...] = _rms(x) * pv_ref[1:2]


def final_combine(x, y0, y1, w, pv):
    m = x.shape[0]
    tm = min(m, 1024)
    row = lambda wd: pl.BlockSpec((tm, wd), lambda i: (i, 0))
    return pl.pallas_call(
        _final_kernel,
        out_shape=jax.ShapeDtypeStruct((m, D_MODEL), F32),
        grid=(m // tm,),
        in_specs=[row(D_MODEL), row(D_MODEL), row(D_MODEL), row(128),
                  pl.BlockSpec((8, D_MODEL), lambda i: (0, 0))],
        out_specs=row(D_MODEL),
        compiler_params=_cparams(("parallel",)),
        name="final_combine",
    )(x, y0, y1, w, pv)


CH = 128
NEG = -1e30


def _masks(reverse):
    r = lax.broadcasted_iota(jnp.int32, (CH, CH), 0)
    c = lax.broadcasted_iota(jnp.int32, (CH, CH), 1)
    return (r <= c, r < c) if reverse else (r >= c, r > c)


def _cumsum_time(incl, x):
    m = jnp.where(incl, 1.0, 0.0).astype(BF16)
    hi = x.astype(BF16)
    lo = (x - hi.astype(F32)).astype(BF16)
    return _dot(m, hi) + _dot(m, lo)


def _dir_specs(nblk, width, col_block):
    return [pl.BlockSpec((CH, width), lambda n: (n, col_block)),
            pl.BlockSpec((CH, width), lambda n: (nblk - 1 - n, col_block))]


def _whole(a):
    return pl.BlockSpec(a.shape, lambda n: tuple(0 for _ in a.shape))


def _ret_consts():
    log_gamma = np.log(1.0 - 2.0 ** (-5.0 - np.arange(N_HEADS, dtype=np.float64)))
    pos = np.arange(CH, dtype=np.float64)
    diff = pos[:, None] - pos[None, :]
    dec_f = np.where(diff >= 0, np.exp(log_gamma[:, None, None] * diff), 0.0)
    dec = np.stack([dec_f, np.transpose(dec_f, (0, 2, 1))])
    qs_f = np.exp(log_gamma[None, :] * (pos[:, None] + 1.0))
    ks_f = np.exp(log_gamma[None, :] * (CH - 1.0 - pos[:, None]))
    qs_b = np.exp(log_gamma[None, :] * (CH - pos[:, None]))
    ks_b = np.exp(log_gamma[None, :] * pos[:, None])
    rep = lambda a: np.repeat(a, RET_DK, axis=1)
    qs = np.stack([rep(qs_f), rep(qs_b)])
    ks = np.stack([rep(ks_f), rep(ks_b)])
    chunk_decay = [float(np.exp(lg * CH)) for lg in log_gamma]
    return (jnp.asarray(dec, F32), jnp.asarray(qs, F32), jnp.asarray(ks, F32)), chunk_decay


def _ret_kernel(*refs, rotary, chunk_decay):
    if rotary:
        (qf, qb, kf, kb, vf, vb, cosf, cosb, sinf, sinb, dec_ref, qs_ref, ks_ref, s0_ref,
         of_ref, ob_ref, s_ref) = refs
        tabs = ((cosf, sinf), (cosb, sinb))
    else:
        qf, qb, kf, kb, vf, vb, dec_ref, qs_ref, ks_ref, s0_ref, of_ref, ob_ref, s_ref = refs
        tabs = (None, None)

    @pl.when(pl.program_id(0) == 0)
    def _():
        s_ref[...] = s0_ref[...]

    lane = lax.broadcasted_iota(jnp.int32, (CH, N_HEADS * RET_DK), 1)
    first_half = (lane & (RET_DK - 1)) < RET_DK // 2

    chains = []
    for d, (q_ref, k_ref, v_ref, o_ref) in enumerate(((qf, kf, vf, of_ref), (qb, kb, vb, ob_ref))):
        q = q_ref[...].astype(F32)
        k = k_ref[...].astype(F32) * RET_DK ** -0.5
        if rotary:
            cos, sin = tabs[d][0][...], tabs[d][1][...]

            def rot(x):
                swapped = jnp.where(first_half, pltpu.roll(x, N_HEADS * RET_DK - RET_DK // 2, 1),
                                    pltpu.roll(x, RET_DK // 2, 1))
                return x * cos + swapped * sin

            q, k = rot(q), rot(k)
        qb16, kb16 = q.astype(BF16), k.astype(BF16)
        q_in = (q * qs_ref[d]).astype(BF16)
        k_out = k * ks_ref[d]
        k_t = [k_out[:, :128].T, k_out[:, 128:].T]
        for h in range(N_HEADS):
            sl = slice(h * RET_DK, (h + 1) * RET_DK)
            chains.append(dict(
                d=d, h=h, o_ref=o_ref, vh=v_ref[:, h * HEAD_V:(h + 1) * HEAD_V], q=qb16[:, sl], k=kb16[:, sl],
                q_in=q_in[:, sl], k_th=k_t[h // 2][(h % 2) * RET_DK:(h % 2 + 1) * RET_DK, :].astype(BF16)))

    ps = [(_dot_nt(c["q"], c["k"]) * dec_ref[c["d"], c["h"]]).astype(BF16) for c in chains]
    states = [s_ref[c["d"], c["h"]] for c in chains]
    inters = [_dot(c["q_in"], s.astype(BF16)) for c, s in zip(chains, states)]
    updates = [_dot(c["k_th"], c["vh"]) for c in chains]
    for c, p, inter in zip(chains, ps, inters):
        h = c["h"]
        c["o_ref"][:, h * HEAD_V:(h + 1) * HEAD_V] = _dot(p, c["vh"]) + inter
    for c, s, update in zip(chains, states, updates):
        s_ref[c["d"], c["h"]] = chunk_decay[c["h"]] * s + update


def retention_scan(z, s0, tables):
    l = z.shape[0]
    nblk = l // CH
    consts, chunk_decay = _ret_consts()
    rotary = tables is not None
    w = N_HEADS * RET_DK
    specs = (_dir_specs(nblk, w, OFF_RET_Q // w) + _dir_specs(nblk, w, OFF_RET_K // w)
             + _dir_specs(nblk, BRANCH_W, OFF_RET_V // BRANCH_W))
    args = [z] * 6
    if rotary:
        specs += _dir_specs(nblk, w, 0) + _dir_specs(nblk, w, 0)
        args += [tables[0], tables[0], tables[1], tables[1]]
    specs += [_whole(c) for c in consts] + [_whole(s0)]
    args += list(consts) + [s0]
    return pl.pallas_call(
        functools.partial(_ret_kernel, rotary=rotary, chunk_decay=chunk_decay),
        out_shape=(jax.ShapeDtypeStruct((l, BRANCH_W), F32), jax.ShapeDtypeStruct((l, BRANCH_W), F32),
                   jax.ShapeDtypeStruct(s0.shape, F32)),
        grid=(nblk,),
        in_specs=specs,
        out_specs=(pl.BlockSpec((CH, BRANCH_W), lambda n: (n, 0)),
                   pl.BlockSpec((CH, BRANCH_W), lambda n: (nblk - 1 - n, 0)), _whole(s0)),
        compiler_params=_cparams(("arbitrary",)),
        name="retention_scan",
    )(*args)


def _mlstm_kernel(qf, qb, kf, kb, vf, vb, smf, smb, bias_ref, c0_ref, m0_ref, of_ref, ob_ref, c_ref, m_ref):
    @pl.when(pl.program_id(0) == 0)
    def _():
        c_ref[...] = c0_ref[...]
        m_ref[...] = m0_ref[...]

    lane = lax.broadcasted_iota(jnp.int32, (CH, 128), 1)
    is_forget = (lane >= F_LANE) & (lane < F_LANE + 2 * N_HEADS)
    ones_col = jnp.where(lane == 0, 1.0, 0.0).astype(BF16)

    chains = []
    for d, (q_ref, k_ref, v_ref, sm_ref, o_ref) in enumerate(((qf, kf, vf, smf, of_ref), (qb, kb, vb, smb, ob_ref))):
        reverse = d == 1
        incl, _ = _masks(reverse)
        last = 0 if reverse else CH - 1
        pre = sm_ref[...] + bias_ref[0:1]
        x = jnp.where(is_forget, jax.nn.log_sigmoid(pre), pre)
        b = _cumsum_time(incl, x)
        x_t, b_t = x.T, b.T
        q = (q_ref[...].astype(F32) * MLSTM_DK ** -0.5).astype(BF16)
        kf32 = k_ref[...].astype(F32)
        k_t = [kf32[:, :128].T, kf32[:, 128:].T]
        for h in range(N_HEADS):
            gi, gf = I_LANE + d * N_HEADS + h, F_LANE + d * N_HEADS + h
            sl = slice(h * MLSTM_DK, (h + 1) * MLSTM_DK)
            b_col, b_row = b[:, gf:gf + 1], b_t[gf:gf + 1, :]
            li_row = x_t[gi:gi + 1, :]
            b_last = b[last:last + 1, gf:gf + 1]
            d_log = jnp.where(incl, b_col - b_row + li_row, NEG)
            m_intra = jnp.max(d_log, axis=1, keepdims=True)
            e_row = b_last - b_row + li_row
            chains.append(dict(
                d=d, h=h, o_ref=o_ref, q=q[:, sl], k=k_ref[:, sl], b_col=b_col, b_last=b_last, m_intra=m_intra,
                e_row=e_row, m_end=jnp.max(e_row, axis=1, keepdims=True),
                w_intra=jnp.exp(d_log - m_intra),
                k_th=k_t[h // 2][(h % 2) * MLSTM_DK:(h % 2 + 1) * MLSTM_DK, :],
                v_aug=jnp.concatenate([v_ref[:, h * HEAD_V:(h + 1) * HEAD_V], ones_col], axis=1)))

    ps = [(c["w_intra"] * _dot_nt(c["q"], c["k"])).astype(BF16) for c in chains]
    pvs = [_dot(p, c["v_aug"]) for c, p in zip(chains, ps)]
    c_augs = [c_ref[c["d"], c["h"]] for c in chains]
    m_ss = [m_ref[c["d"], c["h"]][0:1, 0:1] for c in chains]
    qcs = [_dot(c["q"], c_aug.astype(BF16)) for c, c_aug in zip(chains, c_augs)]
    m_news = [jnp.maximum(c["b_last"] + m_s, c["m_end"]) for c, m_s in zip(chains, m_ss)]
    k_ws = [(c["k_th"] * jnp.exp(c["e_row"] - m_new)).astype(BF16) for c, m_new in zip(chains, m_news)]
    updates = [_dot(k_w, c["v_aug"]) for c, k_w in zip(chains, k_ws)]
    for c, pv, qc, m_s in zip(chains, pvs, qcs, m_ss):
        h = c["h"]
        m_t = jnp.maximum(c["b_col"] + m_s, c["m_intra"])
        inter = jnp.exp(c["b_col"] + m_s - m_t)
        r = jnp.exp(c["m_intra"] - m_t)
        tot = inter * qc + r * pv
        den = jnp.maximum(jnp.abs(tot[:, HEAD_V:HEAD_V + 1]), jnp.exp(-m_t))
        c["o_ref"][:, h * HEAD_V:(h + 1) * HEAD_V] = tot[:, :HEAD_V] / den
    for c, c_aug, m_s, m_new, update in zip(chains, c_augs, m_ss, m_news, updates):
        c_ref[c["d"], c["h"]] = jnp.exp(c["b_last"] + m_s - m_new) * c_aug + update
        m_ref[c["d"], c["h"]] = jnp.broadcast_to(m_new, (8, 128))


def mlstm_scan(z, small, bias, c0, m0):
    l = z.shape[0]
    nblk = l // CH
    w = N_HEADS * MLSTM_DK
    specs = (_dir_specs(nblk, w, OFF_MLSTM_Q // w) + _dir_specs(nblk, w, OFF_MLSTM_K // w)
             + _dir_specs(nblk, BRANCH_W, OFF_MLSTM_V // BRANCH_W) + _dir_specs(nblk, 128, 0))
    specs += [_whole(bias), _whole(c0), _whole(m0)]
    return pl.pallas_call(
        _mlstm_kernel,
        out_shape=(jax.ShapeDtypeStruct((l, BRANCH_W), F32), jax.ShapeDtypeStruct((l, BRANCH_W), F32),
                   jax.ShapeDtypeStruct(c0.shape, F32), jax.ShapeDtypeStruct(m0.shape, F32)),
        grid=(nblk,),
        in_specs=specs,
        out_specs=(pl.BlockSpec((CH, BRANCH_W), lambda n: (n, 0)),
                   pl.BlockSpec((CH, BRANCH_W), lambda n: (nblk - 1 - n, 0)), _whole(c0), _whole(m0)),
        compiler_params=_cparams(("arbitrary",)),
        name="mlstm_scan",
    )(z, z, z, z, z, z, small, small, bias, c0, m0)


QKV_W = 3 * N_HEADS * GDN_DK
HALO = 8


def _gdn_prep_kernel(x_ref, prev_ref, next_ref, w_ref, q_ref, k_ref, v_ref):
    i = pl.program_id(0)
    tm = x_ref.shape[0]
    x = x_ref[...].astype(F32)
    prev = jnp.where(i > 0, prev_ref[...].astype(F32), 0.0)
    nxt = jnp.where(i < pl.num_programs(0) - 1, next_ref[...].astype(F32), 0.0)
    xe = jnp.concatenate([prev, x, nxt], axis=0)
    w = w_ref[...]
    y = None
    for tap in range(CONV_K):
        off = HALO + tap - CONV_K // 2
        term = w[tap:tap + 1] * xe[off:off + tm]
        y = term if y is None else y + term
    y = y * jax.nn.sigmoid(y)
    hw = N_HEADS * GDN_DK
    for h in range(N_HEADS):
        sl = slice(h * GDN_DK, (h + 1) * GDN_DK)
        qh = y[:, sl]
        kh = y[:, hw + h * GDN_DK:hw + (h + 1) * GDN_DK]
        q_ref[:, sl] = (qh * lax.rsqrt(jnp.sum(qh * qh, axis=-1, keepdims=True) + EPS) * GDN_DK ** -0.5).astype(BF16)
        k_ref[:, sl] = (kh * lax.rsqrt(jnp.sum(kh * kh, axis=-1, keepdims=True) + EPS)).astype(BF16)
    v_ref[...] = y[:, 2 * hw:].astype(BF16)


def gdn_prep(z, conv_w):
    l = z.shape[0]
    tm = min(l, 256)
    nb = l // tm
    r8 = tm // HALO
    cb = OFF_GDN_QKV // QKV_W
    w8 = jnp.concatenate([conv_w.astype(F32), jnp.zeros((8 - CONV_K, QKV_W), F32)], axis=0)
    return pl.pallas_call(
        _gdn_prep_kernel,
        out_shape=tuple(jax.ShapeDtypeStruct((l, BRANCH_W), BF16) for _ in range(3)),
        grid=(nb,),
        in_specs=[pl.BlockSpec((tm, QKV_W), lambda i: (i, cb)),
                  pl.BlockSpec((HALO, QKV_W), lambda i: (jnp.maximum(i * r8 - 1, 0), cb)),
                  pl.BlockSpec((HALO, QKV_W), lambda i: (jnp.minimum((i + 1) * r8, nb * r8 - 1), cb)),
                  pl.BlockSpec((8, QKV_W), lambda i: (0, 0))],
        out_specs=tuple(pl.BlockSpec((tm, BRANCH_W), lambda i: (i, 0)) for _ in range(3)),
        compiler_params=_cparams(("parallel",)),
        name="gdn_prep",
    )(z, z, z, w8)


N_LEVELS = 7


def _gdn_kernel(qf, qb, kf, kb, vf, vb, smf, smb, par_ref, s0_ref, of_ref, ob_ref, s_ref):
    @pl.when(pl.program_id(0) == 0)
    def _():
        s_ref[...] = s0_ref[...]

    lane = lax.broadcasted_iota(jnp.int32, (CH, 128), 1)
    is_decay = lane < B_LANE
    ri = lax.broadcasted_iota(jnp.int32, (CH, CH), 0)
    ci = lax.broadcasted_iota(jnp.int32, (CH, CH), 1)
    eye = jnp.where(ri == ci, 1.0, 0.0)
    pair_masks = [((ri >> (l + 1)) == (ci >> (l + 1))) & ((ri >> l) != (ci >> l)) for l in range(N_LEVELS)]

    chains = []
    for d, (q_ref, k_ref, v_ref, sm_ref, o_ref) in enumerate(((qf, kf, vf, smf, of_ref), (qb, kb, vb, smb, ob_ref))):
        reverse = d == 1
        incl, strict = _masks(reverse)
        last = 0 if reverse else CH - 1
        sm = sm_ref[...]
        log_a = -jnp.exp(par_ref[1:2]) * jax.nn.softplus(sm + par_ref[0:1])
        x = jnp.where(is_decay, log_a, jax.nn.sigmoid(sm))
        g = _cumsum_time(incl, x)
        g_t = g.T
        for h in range(N_HEADS):
            ga, gb = A_LANE + d * N_HEADS + h, B_LANE + d * N_HEADS + h
            sl = slice(h * GDN_DK, (h + 1) * GDN_DK)
            g_col, g_row = g[:, ga:ga + 1], g_t[ga:ga + 1, :]
            g_last = g[last:last + 1, ga:ga + 1]
            beta = x[:, gb:gb + 1]
            decay = jnp.where(incl, jnp.exp(jnp.minimum(g_col - g_row, 0.0)), 0.0)
            kh = k_ref[:, sl]
            kf32 = kh.astype(F32)
            kbeta = kf32 * beta
            a = jnp.where(strict, _dot_nt(kbeta.astype(BF16), kh) * decay, 0.0)
            e_g = jnp.exp(g_col)
            rhs = jnp.concatenate([v_ref[:, sl].astype(F32) * beta, kbeta * e_g], axis=1)
            chains.append(dict(
                d=d, h=h, sl=sl, o_ref=o_ref, a=a, rhs=rhs, s_decay=jnp.exp(g_last),
                qk=(_dot_nt(q_ref[:, sl], kh) * decay).astype(BF16),
                q_in=(q_ref[:, sl].astype(F32) * e_g).astype(BF16),
                k_out_t=(kf32.T * jnp.exp(g_last - g_row)).astype(BF16)))

    xs = [eye - jnp.where(pair_masks[0], c["a"], 0.0) for c in chains]
    for pm in pair_masks[1:]:
        ys = [_dot(jnp.where(pm, c["a"], 0.0).astype(BF16), x.astype(BF16)) for c, x in zip(chains, xs)]
        xs = [x - _dot(x.astype(BF16), y.astype(BF16)) for x, y in zip(xs, ys)]
    sols = [c["rhs"] + _dot((x - eye).astype(BF16), c["rhs"].astype(BF16)) for c, x in zip(chains, xs)]
    states = [s_ref[c["d"], c["h"]] for c in chains]
    s16s = [s.astype(BF16) for s in states]
    v_news = [(sol[:, :HEAD_V] - _dot(sol[:, HEAD_V:].astype(BF16), s16)).astype(BF16) for sol, s16 in zip(sols, s16s)]
    updates = [_dot(c["k_out_t"], v_new) for c, v_new in zip(chains, v_news)]
    for c, s16, v_new in zip(chains, s16s, v_news):
        c["o_ref"][:, c["sl"]] = _dot(c["q_in"], s16) + _dot(c["qk"], v_new)
    for c, s, update in zip(chains, states, updates):
        s_ref[c["d"], c["h"]] = s * c["s_decay"] + update


def gdn_scan(qn, kn, vn, small, par, s0):
    l = qn.shape[0]
    nblk = l // CH
    specs = _dir_specs(nblk, BRANCH_W, 0) * 3 + _dir_specs(nblk, 128, 0)
    specs += [_whole(par), _whole(s0)]
    return pl.pallas_call(
        _gdn_kernel,
        out_shape=(jax.ShapeDtypeStruct((l, BRANCH_W), F32), jax.ShapeDtypeStruct((l, BRANCH_W), F32),
                   jax.ShapeDtypeStruct(s0.shape, F32)),
        grid=(nblk,),
        in_specs=specs,
        out_specs=(pl.BlockSpec((CH, BRANCH_W), lambda n: (n, 0)),
                   pl.BlockSpec((CH, BRANCH_W), lambda n: (nblk - 1 - n, 0)), _whole(s0)),
        compiler_params=_cparams(("arbitrary",)),
        name="gdn_scan",
    )(qn, qn, kn, kn, vn, vn, small, small, par, s0)


def _post_kernel(of_ref, ob_ref, gate_ref, g_ref, o_ref, *, centre, silu_gate):
    o = of_ref[...] + ob_ref[...]
    gate = gate_ref[...].astype(F32)
    sig = jax.nn.sigmoid(gate)
    act = gate * sig if silu_gate else sig
    for h in range(N_HEADS):
        sl = slice(h * HEAD_V, (h + 1) * HEAD_V)
        oh = o[:, sl]
        if centre:
            oh = oh - jnp.mean(oh, axis=-1, keepdims=True)
        oh = oh * lax.rsqrt(jnp.mean(oh * oh, axis=-1, keepdims=True) + EPS)
        o_ref[:, sl] = (oh * g_ref[:, sl] * act[:, sl]).astype(BF16)


def head_post(o_f, o_b, z, gate_off, norm_g, centre, silu_gate):
    l = o_f.shape[0]
    tm = min(l, 1024)
    row = pl.BlockSpec((tm, BRANCH_W), lambda i: (i, 0))
    return pl.pallas_call(
        functools.partial(_post_kernel, centre=centre, silu_gate=silu_gate),
        out_shape=jax.ShapeDtypeStruct((l, BRANCH_W), BF16),
        grid=(l // tm,),
        in_specs=[row, row, pl.BlockSpec((tm, BRANCH_W), lambda i: (i, gate_off // BRANCH_W)),
                  pl.BlockSpec((1, BRANCH_W), lambda i: (0, 0))],
        out_specs=row,
        compiler_params=_cparams(("parallel",)),
        name="head_post",
    )(o_f, o_b, z, norm_g.reshape(1, BRANCH_W).astype(F32))


def _lane_rows(entries):
    r = jnp.zeros((8, 128), F32)
    for row, lane, vals in entries:
        r = r.at[row, lane:lane + vals.shape[0]].set(vals.astype(F32))
    return r


def gdn_branch(zc, sc, zl, sl, conv_w, a_log, dt_bias, norm_g, with_ctx):
    par = _lane_rows([(0, A_LANE, dt_bias.reshape(-1)), (1, A_LANE, a_log.reshape(-1))])
    s0 = jnp.zeros((2, N_HEADS, GDN_DK, HEAD_V), F32)
    ofc, obc, s1 = gdn_scan(*gdn_prep(zc, conv_w), sc, par, s0)
    ofl, obl, _ = gdn_scan(*gdn_prep(zl, conv_w), sl, par, s1)
    out_c = head_post(ofc, obc, zc, OFF_GDN_Z, norm_g, False, True) if with_ctx else None
    return out_c, head_post(ofl, obl, zl, OFF_GDN_Z, norm_g, False, True)


def mlstm_branch(zc, sc, zl, sl, gate_b, norm_g, with_ctx):
    bias = _lane_rows([(0, I_LANE, gate_b[0].reshape(-1)), (0, F_LANE, gate_b[1].reshape(-1))])
    c0 = jnp.zeros((2, N_HEADS, MLSTM_DK, 2 * HEAD_V), F32)
    m0 = jnp.zeros((2, N_HEADS, 8, 128), F32)
    ofc, obc, c1, m1 = mlstm_scan(zc, sc, bias, c0, m0)
    ofl, obl, _, _ = mlstm_scan(zl, sl, bias, c1, m1)
    out_c = head_post(ofc, obc, zc, OFF_MLSTM_O, norm_g, False, False) if with_ctx else None
    return out_c, head_post(ofl, obl, zl, OFF_MLSTM_O, norm_g, False, False)


def retention_branch(zc, zl, tables, norm_g, with_ctx):
    s0 = jnp.zeros((2, N_HEADS, RET_DK, HEAD_V), F32)
    ofc, obc, s1 = retention_scan(zc, s0, None)
    ofl, obl, _ = retention_scan(zl, s1, tables)
    out_c = head_post(ofc, obc, zc, OFF_RET_G, norm_g, True, True) if with_ctx else None
    return out_c, head_post(ofl, obl, zl, OFF_RET_G, norm_g, True, True)


def _ret_rope_tables(n_lat):
    inv = ROPE_BASE ** (-jnp.arange(0, RET_DK, 2, dtype=F32) / RET_DK)
    ang = jnp.arange(n_lat, dtype=F32)[:, None] * inv[None, :]
    cos = jnp.concatenate([jnp.cos(ang), jnp.cos(ang)], axis=-1)
    sin = jnp.concatenate([-jnp.sin(ang), jnp.sin(ang)], axis=-1)
    return jnp.tile(cos, (1, N_HEADS)), jnp.tile(sin, (1, N_HEADS))


def _rope_tables(n_lat):
    q = MLA_ROPE // 4
    inv = ROPE_BASE ** (-jnp.arange(0, 2 * q, 2, dtype=F32) / (2 * q))
    t = jnp.arange(n_lat)
    row = (t // GRID_W).astype(F32)[:, None] * inv[None, :]
    col = (t % GRID_W).astype(F32)[:, None] * inv[None, :]
    cos = jnp.concatenate([jnp.cos(row), jnp.cos(row), jnp.cos(col), jnp.cos(col)], axis=-1)
    sin = jnp.concatenate([-jnp.sin(row), jnp.sin(row), -jnp.sin(col), jnp.sin(col)], axis=-1)
    return cos, sin


def _pad_rows(v, n=8):
    rows = [jnp.reshape(r, (1, -1)).astype(F32) for r in v]
    d = rows[0].shape[1]
    return jnp.concatenate(rows + [jnp.zeros((n - len(rows), d), F32)], axis=0)


def _mla_weights(w_uq, w_ukv):
    wq = w_uq.reshape(MLA_Q_RANK, N_HEADS, MLA_QK)
    rope = wq[:, :, MLA_NOPE:]
    swapped = rope[:, :, _rope_swap(np.arange(MLA_ROPE))]
    wq_ext = jnp.concatenate([wq[:, :, :MLA_NOPE], rope, swapped], axis=-1).reshape(MLA_Q_RANK, N_HEADS * 256)
    return wq_ext.astype(BF16), w_ukv.astype(BF16)


def _moe(x_l, pv2, g2, w_router, b_router, w_gu, w_down, final_g):
    n = x_l.shape[0]
    w_pad = jnp.concatenate([w_router, jnp.zeros((D_MODEL, 128 - N_EXPERTS), F32)], axis=1)
    b_pad = jnp.concatenate([b_router, jnp.full((128 - N_EXPERTS,), -1e30, F32)]).reshape(1, 128)
    h2, logits = moe_router(x_l, pv2, w_pad, b_pad)
    top_logit, top_e = lax.top_k(logits[:, :N_EXPERTS], TOP_K)
    weight = jax.nn.softmax(top_logit, axis=-1)
    n_assign = n * TOP_K
    flat_e = top_e.reshape(-1)
    onehot = (flat_e[:, None] == jnp.arange(N_EXPERTS)[None, :]).astype(jnp.int32)
    csum = jnp.cumsum(onehot, axis=0)
    counts = csum[-1]
    rank = jnp.take_along_axis(csum, flat_e[:, None], axis=1)[:, 0] - 1
    padded = (counts + MOE_ROWS - 1) // MOE_ROWS * MOE_ROWS
    pad_end = jnp.cumsum(padded)
    dest = (pad_end - padded)[flat_e] + rank
    nb = n_assign // MOE_ROWS + N_EXPERTS
    cap = nb * MOE_ROWS
    block_e = jnp.minimum(jnp.searchsorted(pad_end, jnp.arange(nb) * MOE_ROWS, side='right'),
                          N_EXPERTS - 1).astype(jnp.int32)
    n_used = (pad_end[-1] // MOE_ROWS).astype(jnp.int32).reshape(1)
    d2 = dest.reshape(n, TOP_K).astype(jnp.int32)
    xb = moe_dispatch(h2, d2, cap)
    yb = expert_ffn(block_e, n_used, xb, w_gu, w_down)
    w128 = jnp.concatenate([weight, jnp.zeros((n, 126), F32)], axis=1)
    return final_combine(x_l, yb[d2[:, 0]], yb[d2[:, 1]], w128, _pad_rows([g2, final_g]))


def kernel(x, c, ctx, c_ctx, w_mod, b_mod, norm1_g, norm2_g, w_in, gdn_conv_w, gdn_a_log, gdn_dt_bias, gdn_norm_g, mla_q_norm_g, mla_kv_norm_g, mla_w_uq, mla_w_ukv, mlstm_gate_b, mlstm_norm_g, ret_norm_g, w_branch, w_out, ffn_w_in, ffn_w_down, moe_w_router, moe_b_router, moe_w_in, moe_w_down, final_norm_g):
    n_lat = x.shape[1]
    n_ctx = ctx.shape[1]
    x_l, x_c = x[0], ctx[0]
    cond = _pad_rows([c_ctx, c[0]])
    cos_l, sin_l = _rope_tables(n_lat)
    cos_c, sin_c = jnp.ones((n_ctx, MLA_ROPE), F32), jnp.zeros((n_ctx, MLA_ROPE), F32)
    ret_tables = _ret_rope_tables(n_lat)
    out = None
    for li in range(DEPTH):
        last = li == DEPTH - 1
        mod = modulation_vectors(cond, w_mod[li], b_mod[li])
        csh1, csc1, cg1, csh2, csc2, cg2 = jnp.split(mod[0], 6)
        sh1, sc1, g1, sh2, sc2, g2 = jnp.split(mod[1], 6)
        w_main = w_in[li][:, _MAIN_PERM].astype(BF16)
        w_small = w_in[li][:, _SMALL_PERM].astype(BF16)
        pv_l = _pad_rows([norm1_g[li], 1 + sc1, sh1])
        pv_c = _pad_rows([norm1_g[li], 1 + csc1, csh1])
        zl = norm_proj(x_l, pv_l, w_main, 768, BF16)
        zc = norm_proj(x_c, pv_c, w_main, 768, BF16)
        sl = norm_proj(x_l, pv_l, w_small, 128, F32)
        sc = norm_proj(x_c, pv_c, w_small, 128, F32)

        a_c, a_l = gdn_branch(zc, sc, zl, sl, gdn_conv_w[li], gdn_a_log[li], gdn_dt_bias[li], gdn_norm_g[li], not last)
        c_c, c_l = mlstm_branch(zc, sc, zl, sl, mlstm_gate_b[li], mlstm_norm_g[li], not last)
        d_c, d_l = retention_branch(zc, zl, ret_tables, ret_norm_g[li], not last)

        wq_ext, wkv = _mla_weights(mla_w_uq[li], mla_w_ukv[li])
        gq, gkv = mla_q_norm_g[li].reshape(1, -1), mla_kv_norm_g[li].reshape(1, -1)
        ql, kl, vl = mla_project(zl, cos_l, sin_l, gq, gkv, wq_ext, wkv)
        qc, kc, vc = mla_project(zc, cos_c, sin_c, gq, gkv, wq_ext, wkv)
        b_l = attention(ql, [(kl, vl), (kc, vc)])

        wb = w_branch[li].astype(BF16)
        wo = w_out[li].astype(BF16)
        x_l = merge_branches(x_l, zl, (a_l, b_l, c_l, d_l), wb, wo, _pad_rows([g1]))
        if not last:
            b_c = attention(qc, [(kc, vc)])
            x_c = merge_branches(x_c, zc, (a_c, b_c, c_c, d_c), wb, wo, _pad_rows([cg1]))

        if li % 2 == 0:
            w_gu = ffn_w_in[li // 2].astype(BF16)
            w_dn = ffn_w_down[li // 2].astype(BF16)
            x_l = dense_ffn(x_l, _pad_rows([norm2_g[li], 1 + sc2, sh2, g2]), w_gu, w_dn)
            if not last:
                x_c = dense_ffn(x_c, _pad_rows([norm2_g[li], 1 + csc2, csh2, cg2]), w_gu, w_dn)
            if last:
                out = final_combine(x_l, jnp.zeros_like(x_l), jnp.zeros_like(x_l),
                                    jnp.zeros((n_lat, 128), F32), _pad_rows([jnp.zeros_like(g2), final_norm_g]))
        else:
            assert last
            out = _moe(x_l, _pad_rows([norm2_g[li], 1 + sc2, sh2]), g2, moe_w_router[li // 2],
                       moe_b_router[li // 2], moe_w_in[li // 2], moe_w_down[li // 2], final_norm_g)
    return out[None]
```

```python
import functools
import math

import numpy as np
import jax
import jax.numpy as jnp
from jax import lax
from jax.experimental import pallas as pl
from jax.experimental.pallas import tpu as pltpu

F32 = jnp.float32
BF16 = jnp.bfloat16

D_MODEL = 1024
DEPTH = 2
GRID_W = 64
N_BRANCH = 4
N_HEADS = 4
HEAD_V = 128
BRANCH_W = N_HEADS * HEAD_V
GDN_DK = 128
CONV_K = 5
MLA_Q_RANK = 384
MLA_KV_RANK = 256
MLA_NOPE = 128
MLA_ROPE = 64
MLA_QK = MLA_NOPE + MLA_ROPE
MLA_QK_PAD = 256
MLA_V_PAD = 256
MLSTM_DK = 64
RET_DK = 64
ROPE_BASE = 10000.0
D_FF = 3584
N_EXPERTS = 8
TOP_K = 2
EPS = 1e-6

IN_WIDTHS = (
    N_BRANCH * D_MODEL,
    N_HEADS * GDN_DK, N_HEADS * GDN_DK, BRANCH_W, BRANCH_W, 2 * N_HEADS, 2 * N_HEADS,
    MLA_Q_RANK, MLA_KV_RANK, MLA_ROPE,
    N_HEADS * MLSTM_DK, N_HEADS * MLSTM_DK, BRANCH_W, BRANCH_W, 2 * N_HEADS, 2 * N_HEADS,
    N_HEADS * RET_DK, N_HEADS * RET_DK, BRANCH_W, BRANCH_W,
)
_IN_OFF = [0] + [int(o) for o in np.cumsum(IN_WIDTHS)]

VMEM_LIMIT = 48 * 1024 * 1024
MOE_ROWS = 1024


def _cols(group):
    return np.arange(_IN_OFF[group], _IN_OFF[group + 1])


def _rope_swap(cols):
    q = MLA_ROPE // 4
    return np.concatenate([cols[q:2 * q], cols[:q], cols[3 * q:], cols[2 * q:3 * q]])


_MAIN_PERM = np.concatenate([
    _cols(1), _cols(2), _cols(3),
    _cols(4),
    _cols(0),
    _cols(12), _cols(13),
    _cols(10), _cols(11),
    _cols(18), _cols(19),
    _cols(16), _cols(17),
    _cols(7), _cols(8), _cols(9), _rope_swap(_cols(9)),
])
N_MAIN = int(_MAIN_PERM.shape[0])
_SMALL_PERM = np.concatenate([_cols(5), _cols(6), _cols(14), _cols(15), np.zeros(96, np.int64)])
A_LANE, B_LANE, I_LANE, F_LANE = 0, 8, 16, 24
OFF_GDN_QKV, OFF_GDN_Z, OFF_GATE = 0, 1536, 2048
OFF_MLSTM_V, OFF_MLSTM_O, OFF_MLSTM_Q, OFF_MLSTM_K = 6144, 6656, 7168, 7424
OFF_RET_V, OFF_RET_G, OFF_RET_Q, OFF_RET_K = 7680, 8192, 8704, 8960
OFF_MLA = 9216
MLA_IN_W = 768


def _cparams(sem):
    return pltpu.CompilerParams(dimension_semantics=sem, vmem_limit_bytes=VMEM_LIMIT)


def _rms(x):
    return x * lax.rsqrt(jnp.mean(x * x, axis=-1, keepdims=True) + EPS)


def _dot(a, b):
    return jnp.dot(a, b, preferred_element_type=F32)


def _dot_nt(a, b):
    return lax.dot_general(a, b, (((1,), (1,)), ((), ())), preferred_element_type=F32)


def _mod_kernel(c_ref, w_ref, b_ref, o_ref):
    c = c_ref[...]
    s = c * jax.nn.sigmoid(c)
    o_ref[...] = jnp.dot(s, w_ref[...], preferred_element_type=F32) + b_ref[...]


def modulation_vectors(cond, w_mod, b_mod):
    n = w_mod.shape[1]
    tn = 1536
    return pl.pallas_call(
        _mod_kernel,
        out_shape=jax.ShapeDtypeStruct((8, n), F32),
        grid=(n // tn,),
        in_specs=[pl.BlockSpec((8, D_MODEL), lambda j: (0, 0)),
                  pl.BlockSpec((D_MODEL, tn), lambda j: (0, j)),
                  pl.BlockSpec((1, tn), lambda j: (0, j))],
        out_specs=pl.BlockSpec((8, tn), lambda j: (0, j)),
        compiler_params=_cparams(("arbitrary",)),
        name="modulation",
    )(cond, w_mod, b_mod.reshape(1, n))


def _norm_proj_kernel(x_ref, pv_ref, w_ref, o_ref, h_ref):
    @pl.when(pl.program_id(1) == 0)
    def _():
        pv = pv_ref[...]
        h_ref[...] = (_rms(x_ref[...]) * pv[0:1] * pv[1:2] + pv[2:3]).astype(BF16)

    o_ref[...] = _dot(h_ref[...], w_ref[...]).astype(o_ref.dtype)


def norm_proj(x, pv, w, tn, out_dtype):
    m, d = x.shape
    n = w.shape[1]
    tm = min(m, 1024)
    return pl.pallas_call(
        _norm_proj_kernel,
        out_shape=jax.ShapeDtypeStruct((m, n), out_dtype),
        grid=(m // tm, n // tn),
        in_specs=[pl.BlockSpec((tm, d), lambda i, j: (i, 0)),
                  pl.BlockSpec((8, d), lambda i, j: (0, 0)),
                  pl.BlockSpec((d, tn), lambda i, j: (0, j))],
        out_specs=pl.BlockSpec((tm, tn), lambda i, j: (i, j)),
        scratch_shapes=[pltpu.VMEM((tm, d), BF16)],
        compiler_params=_cparams(("parallel", "arbitrary")),
        name="norm_proj",
    )(x, pv, w)


def _mla_proj_kernel(z_ref, cos_ref, sin_ref, gq_ref, gkv_ref, wq_ref, wkv_ref, q_ref, k_ref, v_ref):
    z = z_ref[...].astype(F32)
    tm = z.shape[0]
    cq = z[:, :MLA_Q_RANK]
    ckv = z[:, MLA_Q_RANK:MLA_Q_RANK + MLA_KV_RANK]
    kr = z[:, MLA_Q_RANK + MLA_KV_RANK:]
    cos = cos_ref[...]
    sin = sin_ref[...]
    qn = (_rms(cq) * gq_ref[...]).astype(BF16)
    kvn = (_rms(ckv) * gkv_ref[...]).astype(BF16)
    qf = _dot(qn, wq_ref[...]) * (MLA_QK ** -0.5 * math.log2(math.e))
    kvf = _dot(kvn, wkv_ref[...])
    kr_rot = kr[:, :MLA_ROPE] * cos + kr[:, MLA_ROPE:] * sin
    pad = jnp.zeros((tm, MLA_QK_PAD - MLA_QK), F32)
    lane = lax.broadcasted_iota(jnp.int32, (tm, MLA_V_PAD - HEAD_V), 1)
    ones_col = jnp.where(lane == 0, 1.0, 0.0).astype(BF16)
    for h in range(N_HEADS):
        b = h * 256
        q_rot = qf[:, b + 128:b + 192] * cos + qf[:, b + 192:b + 256] * sin
        q_ref[h] = jnp.concatenate([qf[:, b:b + 128], q_rot, pad], axis=-1).astype(BF16)
        k_ref[h] = jnp.concatenate([kvf[:, b:b + 128], kr_rot, pad], axis=-1).astype(BF16)
        v_ref[h] = jnp.concatenate([kvf[:, b + 128:b + 256].astype(BF16), ones_col], axis=-1)


def mla_project(zmain, cos, sin, gq, gkv, wq_ext, wkv):
    m = zmain.shape[0]
    tm = min(m, 1024)
    full = lambda shape: pl.BlockSpec(shape, lambda i: tuple(0 for _ in shape))
    return pl.pallas_call(
        _mla_proj_kernel,
        out_shape=(jax.ShapeDtypeStruct((N_HEADS, m, MLA_QK_PAD), BF16),
                   jax.ShapeDtypeStruct((N_HEADS, m, MLA_QK_PAD), BF16),
                   jax.ShapeDtypeStruct((N_HEADS, m, MLA_V_PAD), BF16)),
        grid=(m // tm,),
        in_specs=[pl.BlockSpec((tm, MLA_IN_W), lambda i: (i, OFF_MLA // MLA_IN_W)),
                  pl.BlockSpec((tm, MLA_ROPE), lambda i: (i, 0)),
                  pl.BlockSpec((tm, MLA_ROPE), lambda i: (i, 0)),
                  full((1, MLA_Q_RANK)), full((1, MLA_KV_RANK)),
                  full((MLA_Q_RANK, N_HEADS * 256)), full((MLA_KV_RANK, N_HEADS * 256))],
        out_specs=(pl.BlockSpec((N_HEADS, tm, MLA_QK_PAD), lambda i: (0, i, 0)),
                   pl.BlockSpec((N_HEADS, tm, MLA_QK_PAD), lambda i: (0, i, 0)),
                   pl.BlockSpec((N_HEADS, tm, MLA_V_PAD), lambda i: (0, i, 0))),
        compiler_params=_cparams(("parallel",)),
        name="mla_project",
    )(zmain, cos, sin, gq, gkv, wq_ext, wkv)


ATTN_TQ, ATTN_TK = 512, 512
ATTN_UNROLL = 16


def _attn_kernel(*refs, segs):
    q_ref, o_ref = refs[0], refs[-1]
    q = q_ref[0]
    tq = q.shape[0]
    carry = (jnp.full((tq, 1), -1e30, F32), jnp.zeros((tq, MLA_V_PAD), F32))
    for si, (tk, nk) in enumerate(segs):
        k_ref, v_ref = refs[1 + 2 * si], refs[2 + 2 * si]

        def body(c, carry, k_ref=k_ref, v_ref=v_ref, tk=tk):
            m, acc = carry
            start = pl.multiple_of(c * tk, tk)
            s = _dot_nt(q, k_ref[0, pl.ds(start, tk), :])
            m_new = jnp.maximum(m, jnp.max(s, axis=-1, keepdims=True))
            p = jnp.exp2(s - m_new).astype(BF16)
            acc = jnp.exp2(m - m_new) * acc + _dot(p, v_ref[0, pl.ds(start, tk), :])
            return m_new, acc

        carry = lax.fori_loop(0, nk, body, carry, unroll=min(ATTN_UNROLL, nk))
    _, acc = carry
    o_ref[...] = (acc[:, :HEAD_V] / acc[:, HEAD_V:HEAD_V + 1]).astype(o_ref.dtype)


def attention(q, kvs):
    _, lq, _ = q.shape
    tq = min(lq, ATTN_TQ)
    segs, args, specs = [], [], []
    for k, v in kvs:
        lk = k.shape[1]
        tk = min(lk, ATTN_TK)
        segs.append((tk, lk // tk))
        args += [k, v]
        specs += [pl.BlockSpec((1, lk, MLA_QK_PAD), lambda h, i: (h, 0, 0)),
                  pl.BlockSpec((1, lk, MLA_V_PAD), lambda h, i: (h, 0, 0))]
    return pl.pallas_call(
        functools.partial(_attn_kernel, segs=tuple(segs)),
        out_shape=jax.ShapeDtypeStruct((lq, N_HEADS * HEAD_V), BF16),
        grid=(N_HEADS, lq // tq),
        in_specs=[pl.BlockSpec((1, tq, MLA_QK_PAD), lambda h, i: (h, i, 0))] + specs,
        out_specs=pl.BlockSpec((tq, HEAD_V), lambda h, i: (i, h)),
        compiler_params=_cparams(("parallel", "arbitrary")),
        name="attention",
    )(q, *args)


def _merge_kernel(x_ref, g0_ref, g1_ref, g2_ref, g3_ref, a_ref, b_ref, c_ref, d_ref, wb_ref, wo_ref, g_ref, o_ref):
    s = None
    for n, (br, gate_ref) in enumerate(zip((a_ref, b_ref, c_ref, d_ref), (g0_ref, g1_ref, g2_ref, g3_ref))):
        proj = _dot(br[...], wb_ref[n])
        gate = jax.nn.sigmoid(gate_ref[...].astype(F32))
        s = gate * proj if s is None else s + gate * proj
    m = _dot(s.astype(BF16), wo_ref[...])
    o_ref[...] = x_ref[...] + g_ref[0:1] * m


def merge_branches(x, zmain, branches, w_branch, w_out, gvec):
    m = x.shape[0]
    tm = min(m, 512)
    row = lambda w: pl.BlockSpec((tm, w), lambda i: (i, 0))
    return pl.pallas_call(
        _merge_kernel,
        out_shape=jax.ShapeDtypeStruct((m, D_MODEL), F32),
        grid=(m // tm,),
        in_specs=[row(D_MODEL)]
                 + [pl.BlockSpec((tm, D_MODEL), lambda i, n=n: (i, OFF_GATE // D_MODEL + n)) for n in range(N_BRANCH)]
                 + [row(BRANCH_W)] * 4
                 + [pl.BlockSpec((N_BRANCH, BRANCH_W, D_MODEL), lambda i: (0, 0, 0)),
                    pl.BlockSpec((D_MODEL, D_MODEL), lambda i: (0, 0)),
                    pl.BlockSpec((8, D_MODEL), lambda i: (0, 0))],
        out_specs=row(D_MODEL),
        compiler_params=_cparams(("parallel",)),
        name="merge_branches",
    )(x, zmain, zmain, zmain, zmain, *branches, w_branch, w_out, gvec)


def _ffn_kernel(x_ref, pv_ref, wg_ref, wu_ref, wd_ref, o_ref, h_ref, acc_ref):
    f = pl.program_id(1)

    @pl.when(f == 0)
    def _():
        pv = pv_ref[...]
        h_ref[...] = (_rms(x_ref[...]) * pv[0:1] * pv[1:2] + pv[2:3]).astype(BF16)
        acc_ref[...] = jnp.zeros_like(acc_ref)

    h = h_ref[...]
    g = _dot(h, wg_ref[...])
    u = _dot(h, wu_ref[...])
    act = (g * jax.nn.sigmoid(g) * u).astype(BF16)
    acc_ref[...] += _dot(act, wd_ref[...])

    @pl.when(f == pl.num_programs(1) - 1)
    def _():
        o_ref[...] = x_ref[...] + pv_ref[3:4] * acc_ref[...]


def dense_ffn(x, pv, w_gu, w_down):
    m = x.shape[0]
    tm = min(m, 1024)
    tf = 512
    nf = D_FF // tf
    return pl.pallas_call(
        _ffn_kernel,
        out_shape=jax.ShapeDtypeStruct((m, D_MODEL), F32),
        grid=(m // tm, nf),
        in_specs=[pl.BlockSpec((tm, D_MODEL), lambda i, f: (i, 0)),
                  pl.BlockSpec((8, D_MODEL), lambda i, f: (0, 0)),
                  pl.BlockSpec((D_MODEL, tf), lambda i, f: (0, f)),
                  pl.BlockSpec((D_MODEL, tf), lambda i, f: (0, f + nf)),
                  pl.BlockSpec((tf, D_MODEL), lambda i, f: (f, 0))],
        out_specs=pl.BlockSpec((tm, D_MODEL), lambda i, f: (i, 0)),
        scratch_shapes=[pltpu.VMEM((tm, D_MODEL), BF16), pltpu.VMEM((tm, D_MODEL), F32)],
        compiler_params=_cparams(("parallel", "arbitrary")),
        name="dense_ffn",
    )(x, pv, w_gu, w_gu, w_down)


def _router_kernel(x_ref, pv_ref, w_ref, b_ref, h_ref, logit_ref):
    pv = pv_ref[...]
    h = _rms(x_ref[...]) * pv[0:1] * pv[1:2] + pv[2:3]
    h_hi = h.astype(BF16)
    h_lo = (h - h_hi.astype(F32)).astype(BF16)
    w = w_ref[...]
    w_hi = w.astype(BF16)
    w_lo = (w - w_hi.astype(F32)).astype(BF16)
    logits = _dot(h_hi, w_hi) + _dot(h_hi, w_lo) + _dot(h_lo, w_hi)
    h_ref[...] = h
    logit_ref[...] = logits + b_ref[...]


def moe_router(x, pv, w_router_pad, b_router_pad):
    m = x.shape[0]
    tm = min(m, 1024)
    return pl.pallas_call(
        _router_kernel,
        out_shape=(jax.ShapeDtypeStruct((m, D_MODEL), F32), jax.ShapeDtypeStruct((m, 128), F32)),
        grid=(m // tm,),
        in_specs=[pl.BlockSpec((tm, D_MODEL), lambda i: (i, 0)),
                  pl.BlockSpec((8, D_MODEL), lambda i: (0, 0)),
                  pl.BlockSpec((D_MODEL, 128), lambda i: (0, 0)),
                  pl.BlockSpec((1, 128), lambda i: (0, 0))],
        out_specs=(pl.BlockSpec((tm, D_MODEL), lambda i: (i, 0)), pl.BlockSpec((tm, 128), lambda i: (i, 0))),
        compiler_params=_cparams(("parallel",)),
        name="moe_router",
    )(x, pv, w_router_pad, b_router_pad)


def _expert_kernel(be_ref, nb_ref, x_ref, wg_ref, wu_ref, wd_ref, o_ref, acc_ref):
    b = pl.program_id(0)
    f = pl.program_id(1)

    @pl.when(b < nb_ref[0])
    def _():
        @pl.when(f == 0)
        def _():
            acc_ref[...] = jnp.zeros_like(acc_ref)

        x = x_ref[...].astype(BF16)
        g = _dot(x, wg_ref[0].astype(BF16))
        u = _dot(x, wu_ref[0].astype(BF16))
        act = (g * jax.nn.sigmoid(g) * u).astype(BF16)
        acc_ref[...] += _dot(act, wd_ref[0].astype(BF16))

        @pl.when(f == pl.num_programs(1) - 1)
        def _():
            o_ref[...] = acc_ref[...]

    @pl.when((b >= nb_ref[0]) & (f == pl.num_programs(1) - 1))
    def _():
        o_ref[...] = jnp.zeros_like(o_ref)


def expert_ffn(block_e, n_used, xb, w_gu, w_down):
    cap = xb.shape[0]
    nb = cap // MOE_ROWS
    tf = 512
    nf = D_FF // tf

    def live(b, nbr):
        return jnp.minimum(b, nbr[0] - 1)

    def fsel(b, f, nbr):
        return jnp.where(b < nbr[0], f, nf - 1)

    grid_spec = pltpu.PrefetchScalarGridSpec(
        num_scalar_prefetch=2,
        grid=(nb, nf),
        in_specs=[pl.BlockSpec((MOE_ROWS, D_MODEL), lambda b, f, be, nbr: (live(b, nbr), 0)),
                  pl.BlockSpec((1, D_MODEL, tf), lambda b, f, be, nbr: (be[live(b, nbr)], 0, fsel(b, f, nbr))),
                  pl.BlockSpec((1, D_MODEL, tf), lambda b, f, be, nbr: (be[live(b, nbr)], 0, fsel(b, f, nbr) + nf)),
                  pl.BlockSpec((1, tf, D_MODEL), lambda b, f, be, nbr: (be[live(b, nbr)], fsel(b, f, nbr), 0))],
        out_specs=pl.BlockSpec((MOE_ROWS, D_MODEL), lambda b, f, be, nbr: (b, 0)),
        scratch_shapes=[pltpu.VMEM((MOE_ROWS, D_MODEL), F32)],
    )
    return pl.pallas_call(
        _expert_kernel,
        out_shape=jax.ShapeDtypeStruct((cap, D_MODEL), F32),
        grid_spec=grid_spec,
        compiler_params=_cparams(("arbitrary", "arbitrary")),
        name="expert_ffn",
    )(block_e, n_used, xb, w_gu, w_gu, w_down)


def _final_kernel(x_ref, y0_ref, y1_ref, w_ref, pv_ref, o_ref):
    w = w_ref[...]
    f = w[:, 0:1] * y0_ref[...] + w[:, 1:2] * y1_ref[...]
    x = x_ref[...] + pv_ref[0:1] * f
    o_ref[...] = _rms(x) * pv_ref[1:2]


def final_combine(x, y0, y1, w, pv):
    m = x.shape[0]
    tm = min(m, 1024)
    row = lambda wd: pl.BlockSpec((tm, wd), lambda i: (i, 0))
    return pl.pallas_call(
        _final_kernel,
        out_shape=jax.ShapeDtypeStruct((m, D_MODEL), F32),
        grid=(m // tm,),
        in_specs=[row(D_MODEL), row(D_MODEL), row(D_MODEL), row(128),
                  pl.BlockSpec((8, D_MODEL), lambda i: (0, 0))],
        out_specs=row(D_MODEL),
        compiler_params=_cparams(("parallel",)),
        name="final_combine",
    )(x, y0, y1, w, pv)


CH = 128
NEG = -1e30


def _masks(reverse):
    r = lax.broadcasted_iota(jnp.int32, (CH, CH), 0)
    c = lax.broadcasted_iota(jnp.int32, (CH, CH), 1)
    return (r <= c, r < c) if reverse else (r >= c, r > c)


def _cumsum_time(incl, x):
    m = jnp.where(incl, 1.0, 0.0).astype(BF16)
    hi = x.astype(BF16)
    lo = (x - hi.astype(F32)).astype(BF16)
    return _dot(m, hi) + _dot(m, lo)


SCAN_SUB = 4


def _scan_grid(l):
    rows = min(l, SCAN_SUB * CH)
    return l // rows, rows


def _sweep(d, nsub):
    return list(range(nsub)) if d == 0 else list(range(nsub - 1, -1, -1))


def _dir_specs(nsteps, rows, width, col_block):
    return [pl.BlockSpec((rows, width), lambda n: (n, col_block)),
            pl.BlockSpec((rows, width), lambda n: (nsteps - 1 - n, col_block))]


def _whole(a):
    return pl.BlockSpec(a.shape, lambda n: tuple(0 for _ in a.shape))


def _ret_consts():
    log_gamma = np.log(1.0 - 2.0 ** (-5.0 - np.arange(N_HEADS, dtype=np.float64)))
    pos = np.arange(CH, dtype=np.float64)
    diff = pos[:, None] - pos[None, :]
    dec_f = np.where(diff >= 0, np.exp(log_gamma[:, None, None] * diff), 0.0)
    dec = np.stack([dec_f, np.transpose(dec_f, (0, 2, 1))])
    qs_f = np.exp(log_gamma[None, :] * (pos[:, None] + 1.0))
    ks_f = np.exp(log_gamma[None, :] * (CH - 1.0 - pos[:, None]))
    qs_b = np.exp(log_gamma[None, :] * (CH - pos[:, None]))
    ks_b = np.exp(log_gamma[None, :] * pos[:, None])
    rep = lambda a: np.repeat(a, RET_DK, axis=1)
    qs = np.stack([rep(qs_f), rep(qs_b)])
    ks = np.stack([rep(ks_f), rep(ks_b)])
    chunk_decay = [float(np.exp(lg * CH)) for lg in log_gamma]
    return (jnp.asarray(dec, F32), jnp.asarray(qs, F32), jnp.asarray(ks, F32)), chunk_decay


def _ret_kernel(*refs, rotary, chunk_decay):
    if rotary:
        (qf, qb, kf, kb, vf, vb, cosf, cosb, sinf, sinb, dec_ref, qs_ref, ks_ref, s0_ref,
         of_ref, ob_ref, s_ref) = refs
        tabs = ((cosf, sinf), (cosb, sinb))
    else:
        qf, qb, kf, kb, vf, vb, dec_ref, qs_ref, ks_ref, s0_ref, of_ref, ob_ref, s_ref = refs
        tabs = (None, None)

    @pl.when(pl.program_id(0) == 0)
    def _():
        s_ref[...] = s0_ref[...]

    lane = lax.broadcasted_iota(jnp.int32, (CH, N_HEADS * RET_DK), 1)
    first_half = (lane & (RET_DK - 1)) < RET_DK // 2

    nsub = qf.shape[0] // CH
    chains = {}
    for d, (q_ref, k_ref, v_ref, o_ref) in enumerate(((qf, kf, vf, of_ref), (qb, kb, vb, ob_ref))):
        for j in range(nsub):
            rows = slice(j * CH, (j + 1) * CH)
            q = q_ref[rows, :].astype(F32)
            k = k_ref[rows, :].astype(F32) * RET_DK ** -0.5
            if rotary:
                cos, sin = tabs[d][0][rows, :], tabs[d][1][rows, :]

                def rot(x, cos=cos, sin=sin):
                    swapped = jnp.where(first_half, pltpu.roll(x, N_HEADS * RET_DK - RET_DK // 2, 1),
                                        pltpu.roll(x, RET_DK // 2, 1))
                    return x * cos + swapped * sin

                q, k = rot(q), rot(k)
            qb16, kb16 = q.astype(BF16), k.astype(BF16)
            q_in = (q * qs_ref[d]).astype(BF16)
            k_out = k * ks_ref[d]
            k_t = [k_out[:, :128].T, k_out[:, 128:].T]
            for h in range(N_HEADS):
                sl = slice(h * RET_DK, (h + 1) * RET_DK)
                chains[d, j, h] = dict(
                    o_ref=o_ref, rows=rows, vh=v_ref[rows, h * HEAD_V:(h + 1) * HEAD_V], q=qb16[:, sl], k=kb16[:, sl],
                    q_in=q_in[:, sl], k_th=k_t[h // 2][(h % 2) * RET_DK:(h % 2 + 1) * RET_DK, :].astype(BF16))

    for (d, j, h), c in chains.items():
        c["p"] = (_dot_nt(c["q"], c["k"]) * dec_ref[d, h]).astype(BF16)
    for c in chains.values():
        c["intra"] = _dot(c["p"], c["vh"])
        c["update"] = _dot(c["k_th"], c["vh"])
    states = {(d, h): s_ref[d, h] for d in range(2) for h in range(N_HEADS)}
    for i in range(nsub):
        for (d, h), s in list(states.items()):
            c = chains[d, _sweep(d, nsub)[i], h]
            c["o_ref"][c["rows"], h * HEAD_V:(h + 1) * HEAD_V] = c["intra"] + _dot(c["q_in"], s.astype(BF16))
            states[d, h] = chunk_decay[h] * s + c["update"]
    for (d, h), s in states.items():
        s_ref[d, h] = s


def retention_scan(z, s0, tables):
    l = z.shape[0]
    nsteps, rows = _scan_grid(l)
    consts, chunk_decay = _ret_consts()
    rotary = tables is not None
    w = N_HEADS * RET_DK
    specs = (_dir_specs(nsteps, rows, w, OFF_RET_Q // w) + _dir_specs(nsteps, rows, w, OFF_RET_K // w)
             + _dir_specs(nsteps, rows, BRANCH_W, OFF_RET_V // BRANCH_W))
    args = [z] * 6
    if rotary:
        specs += _dir_specs(nsteps, rows, w, 0) + _dir_specs(nsteps, rows, w, 0)
        args += [tables[0], tables[0], tables[1], tables[1]]
    specs += [_whole(c) for c in consts] + [_whole(s0)]
    args += list(consts) + [s0]
    return pl.pallas_call(
        functools.partial(_ret_kernel, rotary=rotary, chunk_decay=chunk_decay),
        out_shape=(jax.ShapeDtypeStruct((l, BRANCH_W), F32), jax.ShapeDtypeStruct((l, BRANCH_W), F32),
                   jax.ShapeDtypeStruct(s0.shape, F32)),
        grid=(nsteps,),
        in_specs=specs,
        out_specs=tuple(_dir_specs(nsteps, rows, BRANCH_W, 0)) + (_whole(s0),),
        compiler_params=_cparams(("arbitrary",)),
        name="retention_scan",
    )(*args)


def _mlstm_kernel(qf, qb, kf, kb, vf, vb, smf, smb, bias_ref, c0_ref, m0_ref, of_ref, ob_ref, c_ref, m_ref):
    @pl.when(pl.program_id(0) == 0)
    def _():
        c_ref[...] = c0_ref[...]
        m_ref[...] = m0_ref[...]

    lane = lax.broadcasted_iota(jnp.int32, (CH, 128), 1)
    is_forget = (lane >= F_LANE) & (lane < F_LANE + 2 * N_HEADS)
    ones_col = jnp.where(lane == 0, 1.0, 0.0).astype(BF16)

    nsub = qf.shape[0] // CH
    chains = {}
    for d, (q_ref, k_ref, v_ref, sm_ref, o_ref) in enumerate(((qf, kf, vf, smf, of_ref), (qb, kb, vb, smb, ob_ref))):
        reverse = d == 1
        incl, _ = _masks(reverse)
        last = 0 if reverse else CH - 1
        for j in range(nsub):
            rows = slice(j * CH, (j + 1) * CH)
            pre = sm_ref[rows, :] + bias_ref[0:1]
            x = jnp.where(is_forget, jax.nn.log_sigmoid(pre), pre)
            b = _cumsum_time(incl, x)
            x_t, b_t = x.T, b.T
            q = (q_ref[rows, :].astype(F32) * MLSTM_DK ** -0.5).astype(BF16)
            kf32 = k_ref[rows, :].astype(F32)
            k_t = [kf32[:, :128].T, kf32[:, 128:].T]
            for h in range(N_HEADS):
                gi, gf = I_LANE + d * N_HEADS + h, F_LANE + d * N_HEADS + h
                sl = slice(h * MLSTM_DK, (h + 1) * MLSTM_DK)
                b_col, b_row = b[:, gf:gf + 1], b_t[gf:gf + 1, :]
                li_row = x_t[gi:gi + 1, :]
                b_last = b[last:last + 1, gf:gf + 1]
                d_log = jnp.where(incl, b_col - b_row + li_row, NEG)
                m_intra = jnp.max(d_log, axis=1, keepdims=True)
                e_row = b_last - b_row + li_row
                chains[d, j, h] = dict(
                    o_ref=o_ref, rows=rows, q=q[:, sl], k=k_ref[rows, sl], b_col=b_col, b_last=b_last, m_intra=m_intra,
                    e_row=e_row, m_end=jnp.max(e_row, axis=1, keepdims=True),
                    w_intra=jnp.exp(d_log - m_intra),
                    k_th=k_t[h // 2][(h % 2) * MLSTM_DK:(h % 2 + 1) * MLSTM_DK, :],
                    v_aug=jnp.concatenate([v_ref[rows, h * HEAD_V:(h + 1) * HEAD_V], ones_col], axis=1))

    for c in chains.values():
        c["p"] = (c["w_intra"] * _dot_nt(c["q"], c["k"])).astype(BF16)
    for c in chains.values():
        c["pv"] = _dot(c["p"], c["v_aug"])
    c_augs = {(d, h): c_ref[d, h] for d in range(2) for h in range(N_HEADS)}
    m_ss = {(d, h): m_ref[d, h][0:1, 0:1] for d in range(2) for h in range(N_HEADS)}
    for i in range(nsub):
        cur = {(d, h): chains[d, _sweep(d, nsub)[i], h] for (d, h) in c_augs}
        qcs = {key: _dot(c["q"], c_augs[key].astype(BF16)) for key, c in cur.items()}
        m_news = {key: jnp.maximum(c["b_last"] + m_ss[key], c["m_end"]) for key, c in cur.items()}
        updates = {key: _dot((c["k_th"] * jnp.exp(c["e_row"] - m_news[key])).astype(BF16), c["v_aug"])
                   for key, c in cur.items()}
        for (d, h), c in cur.items():
            m_s = m_ss[d, h]
            m_t = jnp.maximum(c["b_col"] + m_s, c["m_intra"])
            inter = jnp.exp(c["b_col"] + m_s - m_t)
            r = jnp.exp(c["m_intra"] - m_t)
            tot = inter * qcs[d, h] + r * c["pv"]
            den = jnp.maximum(jnp.abs(tot[:, HEAD_V:HEAD_V + 1]), jnp.exp(-m_t))
            c["o_ref"][c["rows"], h * HEAD_V:(h + 1) * HEAD_V] = tot[:, :HEAD_V] / den
        for key, c in cur.items():
            c_augs[key] = jnp.exp(c["b_last"] + m_ss[key] - m_news[key]) * c_augs[key] + updates[key]
            m_ss[key] = m_news[key]
    for (d, h) in c_augs:
        c_ref[d, h] = c_augs[d, h]
        m_ref[d, h] = jnp.broadcast_to(m_ss[d, h], (8, 128))


def mlstm_scan(z, small, bias, c0, m0):
    l = z.shape[0]
    nsteps, rows = _scan_grid(l)
    w = N_HEADS * MLSTM_DK
    specs = (_dir_specs(nsteps, rows, w, OFF_MLSTM_Q // w) + _dir_specs(nsteps, rows, w, OFF_MLSTM_K // w)
             + _dir_specs(nsteps, rows, BRANCH_W, OFF_MLSTM_V // BRANCH_W) + _dir_specs(nsteps, rows, 128, 0))
    specs += [_whole(bias), _whole(c0), _whole(m0)]
    return pl.pallas_call(
        _mlstm_kernel,
        out_shape=(jax.ShapeDtypeStruct((l, BRANCH_W), F32), jax.ShapeDtypeStruct((l, BRANCH_W), F32),
                   jax.ShapeDtypeStruct(c0.shape, F32), jax.ShapeDtypeStruct(m0.shape, F32)),
        grid=(nsteps,),
        in_specs=specs,
        out_specs=tuple(_dir_specs(nsteps, rows, BRANCH_W, 0)) + (_whole(c0), _whole(m0)),
        compiler_params=_cparams(("arbitrary",)),
        name="mlstm_scan",
    )(z, z, z, z, z, z, small, small, bias, c0, m0)


QKV_W = 3 * N_HEADS * GDN_DK
HALO = 8


def _gdn_prep_kernel(x_ref, prev_ref, next_ref, w_ref, q_ref, k_ref, v_ref):
    i = pl.program_id(0)
    tm = x_ref.shape[0]
    x = x_ref[...].astype(F32)
    prev = jnp.where(i > 0, prev_ref[...].astype(F32), 0.0)
    nxt = jnp.where(i < pl.num_programs(0) - 1, next_ref[...].astype(F32), 0.0)
    xe = jnp.concatenate([prev, x, nxt], axis=0)
    w = w_ref[...]
    y = None
    for tap in range(CONV_K):
        off = HALO + tap - CONV_K // 2
        term = w[tap:tap + 1] * xe[off:off + tm]
        y = term if y is None else y + term
    y = y * jax.nn.sigmoid(y)
    hw = N_HEADS * GDN_DK
    for h in range(N_HEADS):
        sl = slice(h * GDN_DK, (h + 1) * GDN_DK)
        qh = y[:, sl]
        kh = y[:, hw + h * GDN_DK:hw + (h + 1) * GDN_DK]
        q_ref[:, sl] = (qh * lax.rsqrt(jnp.sum(qh * qh, axis=-1, keepdims=True) + EPS) * GDN_DK ** -0.5).astype(BF16)
        k_ref[:, sl] = (kh * lax.rsqrt(jnp.sum(kh * kh, axis=-1, keepdims=True) + EPS)).astype(BF16)
    v_ref[...] = y[:, 2 * hw:].astype(BF16)


def gdn_prep(z, conv_w):
    l = z.shape[0]
    tm = min(l, 256)
    nb = l // tm
    r8 = tm // HALO
    cb = OFF_GDN_QKV // QKV_W
    w8 = jnp.concatenate([conv_w.astype(F32), jnp.zeros((8 - CONV_K, QKV_W), F32)], axis=0)
    return pl.pallas_call(
        _gdn_prep_kernel,
        out_shape=tuple(jax.ShapeDtypeStruct((l, BRANCH_W), BF16) for _ in range(3)),
        grid=(nb,),
        in_specs=[pl.BlockSpec((tm, QKV_W), lambda i: (i, cb)),
                  pl.BlockSpec((HALO, QKV_W), lambda i: (jnp.maximum(i * r8 - 1, 0), cb)),
                  pl.BlockSpec((HALO, QKV_W), lambda i: (jnp.minimum((i + 1) * r8, nb * r8 - 1), cb)),
                  pl.BlockSpec((8, QKV_W), lambda i: (0, 0))],
        out_specs=tuple(pl.BlockSpec((tm, BRANCH_W), lambda i: (i, 0)) for _ in range(3)),
        compiler_params=_cparams(("parallel",)),
        name="gdn_prep",
    )(z, z, z, w8)


N_LEVELS = 7


def _gdn_kernel(qf, qb, kf, kb, vf, vb, smf, smb, par_ref, s0_ref, of_ref, ob_ref, s_ref):
    @pl.when(pl.program_id(0) == 0)
    def _():
        s_ref[...] = s0_ref[...]

    lane = lax.broadcasted_iota(jnp.int32, (CH, 128), 1)
    is_decay = lane < B_LANE
    ri = lax.broadcasted_iota(jnp.int32, (CH, CH), 0)
    ci = lax.broadcasted_iota(jnp.int32, (CH, CH), 1)
    eye = jnp.where(ri == ci, 1.0, 0.0)
    pair_masks = [((ri >> (l + 1)) == (ci >> (l + 1))) & ((ri >> l) != (ci >> l)) for l in range(N_LEVELS)]

    nsub = qf.shape[0] // CH
    chains = {}
    for d, (q_ref, k_ref, v_ref, sm_ref, o_ref) in enumerate(((qf, kf, vf, smf, of_ref), (qb, kb, vb, smb, ob_ref))):
        reverse = d == 1
        incl, strict = _masks(reverse)
        last = 0 if reverse else CH - 1
        for j in range(nsub):
            rows = slice(j * CH, (j + 1) * CH)
            sm = sm_ref[rows, :]
            log_a = -jnp.exp(par_ref[1:2]) * jax.nn.softplus(sm + par_ref[0:1])
            x = jnp.where(is_decay, log_a, jax.nn.sigmoid(sm))
            g = _cumsum_time(incl, x)
            g_t = g.T
            for h in range(N_HEADS):
                ga, gb = A_LANE + d * N_HEADS + h, B_LANE + d * N_HEADS + h
                sl = slice(h * GDN_DK, (h + 1) * GDN_DK)
                g_col, g_row = g[:, ga:ga + 1], g_t[ga:ga + 1, :]
                g_last = g[last:last + 1, ga:ga + 1]
                beta = x[:, gb:gb + 1]
                decay = jnp.where(incl, jnp.exp(jnp.minimum(g_col - g_row, 0.0)), 0.0)
                kh = k_ref[rows, sl]
                kf32 = kh.astype(F32)
                kbeta = kf32 * beta
                a = jnp.where(strict, _dot_nt(kbeta.astype(BF16), kh) * decay, 0.0)
                e_g = jnp.exp(g_col)
                rhs = jnp.concatenate([v_ref[rows, sl].astype(F32) * beta, kbeta * e_g], axis=1)
                chains[d, j, h] = dict(
                    rows=rows, sl=sl, o_ref=o_ref, a=a, rhs=rhs, s_decay=jnp.exp(g_last),
                    qk=(_dot_nt(q_ref[rows, sl], kh) * decay).astype(BF16),
                    q_in=(q_ref[rows, sl].astype(F32) * e_g).astype(BF16),
                    k_out_t=(kf32.T * jnp.exp(g_last - g_row)).astype(BF16))

    cl = list(chains.values())
    xs = [eye - jnp.where(pair_masks[0], c["a"], 0.0) for c in cl]
    for pm in pair_masks[1:]:
        ys = [_dot(jnp.where(pm, c["a"], 0.0).astype(BF16), x.astype(BF16)) for c, x in zip(cl, xs)]
        xs = [x - _dot(x.astype(BF16), y.astype(BF16)) for x, y in zip(xs, ys)]
    for c, x in zip(cl, xs):
        c["sol"] = c["rhs"] + _dot((x - eye).astype(BF16), c["rhs"].astype(BF16))
    states = {(d, h): s_ref[d, h] for d in range(2) for h in range(N_HEADS)}
    for i in range(nsub):
        cur = {(d, h): chains[d, _sweep(d, nsub)[i], h] for (d, h) in states}
        s16s = {key: s.astype(BF16) for key, s in states.items()}
        v_news = {key: (c["sol"][:, :HEAD_V] - _dot(c["sol"][:, HEAD_V:].astype(BF16), s16s[key])).astype(BF16)
                  for key, c in cur.items()}
        updates = {key: _dot(c["k_out_t"], v_news[key]) for key, c in cur.items()}
        for key, c in cur.items():
            c["o_ref"][c["rows"], c["sl"]] = _dot(c["q_in"], s16s[key]) + _dot(c["qk"], v_news[key])
        for key, c in cur.items():
            states[key] = states[key] * c["s_decay"] + updates[key]
    for (d, h), s in states.items():
        s_ref[d, h] = s


def gdn_scan(qn, kn, vn, small, par, s0):
    l = qn.shape[0]
    nsteps, rows = _scan_grid(l)
    specs = _dir_specs(nsteps, rows, BRANCH_W, 0) * 3 + _dir_specs(nsteps, rows, 128, 0)
    specs += [_whole(par), _whole(s0)]
    return pl.pallas_call(
        _gdn_kernel,
        out_shape=(jax.ShapeDtypeStruct((l, BRANCH_W), F32), jax.ShapeDtypeStruct((l, BRANCH_W), F32),
                   jax.ShapeDtypeStruct(s0.shape, F32)),
        grid=(nsteps,),
        in_specs=specs,
        out_specs=tuple(_dir_specs(nsteps, rows, BRANCH_W, 0)) + (_whole(s0),),
        compiler_params=_cparams(("arbitrary",)),
        name="gdn_scan",
    )(qn, qn, kn, kn, vn, vn, small, small, par, s0)


def _post_kernel(of_ref, ob_ref, gate_ref, g_ref, o_ref, *, centre, silu_gate):
    o = of_ref[...] + ob_ref[...]
    gate = gate_ref[...].astype(F32)
    sig = jax.nn.sigmoid(gate)
    act = gate * sig if silu_gate else sig
    for h in range(N_HEADS):
        sl = slice(h * HEAD_V, (h + 1) * HEAD_V)
        oh = o[:, sl]
        if centre:
            oh = oh - jnp.mean(oh, axis=-1, keepdims=True)
        oh = oh * lax.rsqrt(jnp.mean(oh * oh, axis=-1, keepdims=True) + EPS)
        o_ref[:, sl] = (oh * g_ref[:, sl] * act[:, sl]).astype(BF16)


def head_post(o_f, o_b, z, gate_off, norm_g, centre, silu_gate):
    l = o_f.shape[0]
    tm = min(l, 1024)
    row = pl.BlockSpec((tm, BRANCH_W), lambda i: (i, 0))
    return pl.pallas_call(
        functools.partial(_post_kernel, centre=centre, silu_gate=silu_gate),
        out_shape=jax.ShapeDtypeStruct((l, BRANCH_W), BF16),
        grid=(l // tm,),
        in_specs=[row, row, pl.BlockSpec((tm, BRANCH_W), lambda i: (i, gate_off // BRANCH_W)),
                  pl.BlockSpec((1, BRANCH_W), lambda i: (0, 0))],
        out_specs=row,
        compiler_params=_cparams(("parallel",)),
        name="head_post",
    )(o_f, o_b, z, norm_g.reshape(1, BRANCH_W).astype(F32))


def _lane_rows(entries):
    r = jnp.zeros((8, 128), F32)
    for row, lane, vals in entries:
        r = r.at[row, lane:lane + vals.shape[0]].set(vals.astype(F32))
    return r


def gdn_branch(zc, sc, zl, sl, conv_w, a_log, dt_bias, norm_g, with_ctx):
    par = _lane_rows([(0, A_LANE, dt_bias.reshape(-1)), (1, A_LANE, a_log.reshape(-1))])
    s0 = jnp.zeros((2, N_HEADS, GDN_DK, HEAD_V), F32)
    ofc, obc, s1 = gdn_scan(*gdn_prep(zc, conv_w), sc, par, s0)
    ofl, obl, _ = gdn_scan(*gdn_prep(zl, conv_w), sl, par, s1)
    out_c = head_post(ofc, obc, zc, OFF_GDN_Z, norm_g, False, True) if with_ctx else None
    return out_c, head_post(ofl, obl, zl, OFF_GDN_Z, norm_g, False, True)


def mlstm_branch(zc, sc, zl, sl, gate_b, norm_g, with_ctx):
    bias = _lane_rows([(0, I_LANE, gate_b[0].reshape(-1)), (0, F_LANE, gate_b[1].reshape(-1))])
    c0 = jnp.zeros((2, N_HEADS, MLSTM_DK, 2 * HEAD_V), F32)
    m0 = jnp.zeros((2, N_HEADS, 8, 128), F32)
    ofc, obc, c1, m1 = mlstm_scan(zc, sc, bias, c0, m0)
    ofl, obl, _, _ = mlstm_scan(zl, sl, bias, c1, m1)
    out_c = head_post(ofc, obc, zc, OFF_MLSTM_O, norm_g, False, False) if with_ctx else None
    return out_c, head_post(ofl, obl, zl, OFF_MLSTM_O, norm_g, False, False)


def retention_branch(zc, zl, tables, norm_g, with_ctx):
    s0 = jnp.zeros((2, N_HEADS, RET_DK, HEAD_V), F32)
    ofc, obc, s1 = retention_scan(zc, s0, None)
    ofl, obl, _ = retention_scan(zl, s1, tables)
    out_c = head_post(ofc, obc, zc, OFF_RET_G, norm_g, True, True) if with_ctx else None
    return out_c, head_post(ofl, obl, zl, OFF_RET_G, norm_g, True, True)


def _ret_rope_tables(n_lat):
    inv = ROPE_BASE ** (-jnp.arange(0, RET_DK, 2, dtype=F32) / RET_DK)
    ang = jnp.arange(n_lat, dtype=F32)[:, None] * inv[None, :]
    cos = jnp.concatenate([jnp.cos(ang), jnp.cos(ang)], axis=-1)
    sin = jnp.concatenate([-jnp.sin(ang), jnp.sin(ang)], axis=-1)
    return jnp.tile(cos, (1, N_HEADS)), jnp.tile(sin, (1, N_HEADS))


def _rope_tables(n_lat):
    q = MLA_ROPE // 4
    inv = ROPE_BASE ** (-jnp.arange(0, 2 * q, 2, dtype=F32) / (2 * q))
    t = jnp.arange(n_lat)
    row = (t // GRID_W).astype(F32)[:, None] * inv[None, :]
    col = (t % GRID_W).astype(F32)[:, None] * inv[None, :]
    cos = jnp.concatenate([jnp.cos(row), jnp.cos(row), jnp.cos(col), jnp.cos(col)], axis=-1)
    sin = jnp.concatenate([-jnp.sin(row), jnp.sin(row), -jnp.sin(col), jnp.sin(col)], axis=-1)
    return cos, sin


def _pad_rows(v, n=8):
    rows = [jnp.reshape(r, (1, -1)).astype(F32) for r in v]
    d = rows[0].shape[1]
    return jnp.concatenate(rows + [jnp.zeros((n - len(rows), d), F32)], axis=0)


def _mla_weights(w_uq, w_ukv):
    wq = w_uq.reshape(MLA_Q_RANK, N_HEADS, MLA_QK)
    rope = wq[:, :, MLA_NOPE:]
    swapped = rope[:, :, _rope_swap(np.arange(MLA_ROPE))]
    wq_ext = jnp.concatenate([wq[:, :, :MLA_NOPE], rope, swapped], axis=-1).reshape(MLA_Q_RANK, N_HEADS * 256)
    return wq_ext.astype(BF16), w_ukv.astype(BF16)


def _moe(x_l, pv2, g2, w_router, b_router, w_gu, w_down, final_g):
    n = x_l.shape[0]
    w_pad = jnp.concatenate([w_router, jnp.zeros((D_MODEL, 128 - N_EXPERTS), F32)], axis=1)
    b_pad = jnp.concatenate([b_router, jnp.full((128 - N_EXPERTS,), -1e30, F32)]).reshape(1, 128)
    h2, logits = moe_router(x_l, pv2, w_pad, b_pad)
    top_logit, top_e = lax.top_k(logits[:, :N_EXPERTS], TOP_K)
    weight = jax.nn.softmax(top_logit, axis=-1)
    n_assign = n * TOP_K
    flat_e = top_e.reshape(-1)
    onehot = (flat_e[:, None] == jnp.arange(N_EXPERTS)[None, :]).astype(jnp.int32)
    csum = jnp.cumsum(onehot, axis=0)
    counts = csum[-1]
    rank = jnp.take_along_axis(csum, flat_e[:, None], axis=1)[:, 0] - 1
    padded = (counts + MOE_ROWS - 1) // MOE_ROWS * MOE_ROWS
    pad_end = jnp.cumsum(padded)
    dest = (pad_end - padded)[flat_e] + rank
    nb = n_assign // MOE_ROWS + N_EXPERTS
    cap = nb * MOE_ROWS
    block_e = jnp.minimum(jnp.searchsorted(pad_end, jnp.arange(nb) * MOE_ROWS, side='right'),
                          N_EXPERTS - 1).astype(jnp.int32)
    n_used = (pad_end[-1] // MOE_ROWS).astype(jnp.int32).reshape(1)
    d2 = dest.reshape(n, TOP_K)
    slot_tok = jnp.zeros((cap,), jnp.int32).at[dest].set(jnp.arange(n_assign, dtype=jnp.int32) // TOP_K)
    yb = expert_ffn(block_e, n_used, h2[slot_tok], w_gu, w_down)
    w128 = jnp.concatenate([weight, jnp.zeros((n, 126), F32)], axis=1)
    return final_combine(x_l, yb[d2[:, 0]], yb[d2[:, 1]], w128, _pad_rows([g2, final_g]))


def kernel(x, c, ctx, c_ctx, w_mod, b_mod, norm1_g, norm2_g, w_in, gdn_conv_w, gdn_a_log, gdn_dt_bias, gdn_norm_g, mla_q_norm_g, mla_kv_norm_g, mla_w_uq, mla_w_ukv, mlstm_gate_b, mlstm_norm_g, ret_norm_g, w_branch, w_out, ffn_w_in, ffn_w_down, moe_w_router, moe_b_router, moe_w_in, moe_w_down, final_norm_g):
    n_lat = x.shape[1]
    n_ctx = ctx.shape[1]
    x_l, x_c = x[0], ctx[0]
    cond = _pad_rows([c_ctx, c[0]])
    cos_l, sin_l = _rope_tables(n_lat)
    cos_c, sin_c = jnp.ones((n_ctx, MLA_ROPE), F32), jnp.zeros((n_ctx, MLA_ROPE), F32)
    ret_tables = _ret_rope_tables(n_lat)
    out = None
    for li in range(DEPTH):
        last = li == DEPTH - 1
        mod = modulation_vectors(cond, w_mod[li], b_mod[li])
        csh1, csc1, cg1, csh2, csc2, cg2 = jnp.split(mod[0], 6)
        sh1, sc1, g1, sh2, sc2, g2 = jnp.split(mod[1], 6)
        w_main = w_in[li][:, _MAIN_PERM].astype(BF16)
        w_small = w_in[li][:, _SMALL_PERM].astype(BF16)
        pv_l = _pad_rows([norm1_g[li], 1 + sc1, sh1])
        pv_c = _pad_rows([norm1_g[li], 1 + csc1, csh1])
        zl = norm_proj(x_l, pv_l, w_main, 768, BF16)
        zc = norm_proj(x_c, pv_c, w_main, 768, BF16)
        sl = norm_proj(x_l, pv_l, w_small, 128, F32)
        sc = norm_proj(x_c, pv_c, w_small, 128, F32)

        a_c, a_l = gdn_branch(zc, sc, zl, sl, gdn_conv_w[li], gdn_a_log[li], gdn_dt_bias[li], gdn_norm_g[li], not last)
        c_c, c_l = mlstm_branch(zc, sc, zl, sl, mlstm_gate_b[li], mlstm_norm_g[li], not last)
        d_c, d_l = retention_branch(zc, zl, ret_tables, ret_norm_g[li], not last)

        wq_ext, wkv = _mla_weights(mla_w_uq[li], mla_w_ukv[li])
        gq, gkv = mla_q_norm_g[li].reshape(1, -1), mla_kv_norm_g[li].reshape(1, -1)
        ql, kl, vl = mla_project(zl, cos_l, sin_l, gq, gkv, wq_ext, wkv)
        qc, kc, vc = mla_project(zc, cos_c, sin_c, gq, gkv, wq_ext, wkv)
        b_l = attention(ql, [(kl, vl), (kc, vc)])

        wb = w_branch[li].astype(BF16)
        wo = w_out[li].astype(BF16)
        x_l = merge_branches(x_l, zl, (a_l, b_l, c_l, d_l), wb, wo, _pad_rows([g1]))
        if not last:
            b_c = attention(qc, [(kc, vc)])
            x_c = merge_branches(x_c, zc, (a_c, b_c, c_c, d_c), wb, wo, _pad_rows([cg1]))

        if li % 2 == 0:
            w_gu = ffn_w_in[li // 2].astype(BF16)
            w_dn = ffn_w_down[li // 2].astype(BF16)
            x_l = dense_ffn(x_l, _pad_rows([norm2_g[li], 1 + sc2, sh2, g2]), w_gu, w_dn)
            if not last:
                x_c = dense_ffn(x_c, _pad_rows([norm2_g[li], 1 + csc2, csh2, cg2]), w_gu, w_dn)
            if last:
                out = final_combine(x_l, jnp.zeros_like(x_l), jnp.zeros_like(x_l),
                                    jnp.zeros((n_lat, 128), F32), _pad_rows([jnp.zeros_like(g2), final_norm_g]))
        else:
            assert last
            out = _moe(x_l, _pad_rows([norm2_g[li], 1 + sc2, sh2]), g2, moe_w_router[li // 2],
                       moe_b_router[li // 2], moe_w_in[li // 2], moe_w_down[li // 2], final_norm_g)
    return out[None]
```

```python
import functools
import math

import numpy as np
import jax
import jax.numpy as jnp
from jax import lax
from jax.experimental import pallas as pl
from jax.experimental.pallas import tpu as pltpu

F32 = jnp.float32
BF16 = jnp.bfloat16

D_MODEL = 1024
DEPTH = 2
GRID_W = 64
N_BRANCH = 4
N_HEADS = 4
HEAD_V = 128
BRANCH_W = N_HEADS * HEAD_V
GDN_DK = 128
CONV_K = 5
MLA_Q_RANK = 384
MLA_KV_RANK = 256
MLA_NOPE = 128
MLA_ROPE = 64
MLA_QK = MLA_NOPE + MLA_ROPE
MLA_QK_PAD = 256
MLA_V_PAD = 256
MLSTM_DK = 64
RET_DK = 64
ROPE_BASE = 10000.0
D_FF = 3584
N_EXPERTS = 8
TOP_K = 2
EPS = 1e-6

IN_WIDTHS = (
    N_BRANCH * D_MODEL,
    N_HEADS * GDN_DK, N_HEADS * GDN_DK, BRANCH_W, BRANCH_W, 2 * N_HEADS, 2 * N_HEADS,
    MLA_Q_RANK, MLA_KV_RANK, MLA_ROPE,
    N_HEADS * MLSTM_DK, N_HEADS * MLSTM_DK, BRANCH_W, BRANCH_W, 2 * N_HEADS, 2 * N_HEADS,
    N_HEADS * RET_DK, N_HEADS * RET_DK, BRANCH_W, BRANCH_W,
)
_IN_OFF = [0] + [int(o) for o in np.cumsum(IN_WIDTHS)]

VMEM_LIMIT = 48 * 1024 * 1024
MOE_ROWS = 1024


def _cols(group):
    return np.arange(_IN_OFF[group], _IN_OFF[group + 1])


def _rope_swap(cols):
    q = MLA_ROPE // 4
    return np.concatenate([cols[q:2 * q], cols[:q], cols[3 * q:], cols[2 * q:3 * q]])


_MAIN_PERM = np.concatenate([
    _cols(1), _cols(2), _cols(3),
    _cols(4),
    _cols(0),
    _cols(12), _cols(13),
    _cols(10), _cols(11),
    _cols(18), _cols(19),
    _cols(16), _cols(17),
    _cols(7), _cols(8), _cols(9), _rope_swap(_cols(9)),
])
N_MAIN = int(_MAIN_PERM.shape[0])
_SMALL_PERM = np.concatenate([_cols(5), _cols(6), _cols(14), _cols(15), np.zeros(96, np.int64)])
A_LANE, B_LANE, I_LANE, F_LANE = 0, 8, 16, 24
OFF_GDN_QKV, OFF_GDN_Z, OFF_GATE = 0, 1536, 2048
OFF_MLSTM_V, OFF_MLSTM_O, OFF_MLSTM_Q, OFF_MLSTM_K = 6144, 6656, 7168, 7424
OFF_RET_V, OFF_RET_G, OFF_RET_Q, OFF_RET_K = 7680, 8192, 8704, 8960
OFF_MLA = 9216
MLA_IN_W = 768


def _cparams(sem):
    return pltpu.CompilerParams(dimension_semantics=sem, vmem_limit_bytes=VMEM_LIMIT)


def _rms(x):
    return x * lax.rsqrt(jnp.mean(x * x, axis=-1, keepdims=True) + EPS)


def _dot(a, b):
    return jnp.dot(a, b, preferred_element_type=F32)


def _dot_nt(a, b):
    return lax.dot_general(a, b, (((1,), (1,)), ((), ())), preferred_element_type=F32)


def _mod_kernel(c_ref, w_ref, b_ref, o_ref):
    c = c_ref[...]
    s = c * jax.nn.sigmoid(c)
    o_ref[...] = jnp.dot(s, w_ref[...], preferred_element_type=F32) + b_ref[...]


def modulation_vectors(cond, w_mod, b_mod):
    n = w_mod.shape[1]
    tn = 1536
    return pl.pallas_call(
        _mod_kernel,
        out_shape=jax.ShapeDtypeStruct((8, n), F32),
        grid=(n // tn,),
        in_specs=[pl.BlockSpec((8, D_MODEL), lambda j: (0, 0)),
                  pl.BlockSpec((D_MODEL, tn), lambda j: (0, j)),
                  pl.BlockSpec((1, tn), lambda j: (0, j))],
        out_specs=pl.BlockSpec((8, tn), lambda j: (0, j)),
        compiler_params=_cparams(("arbitrary",)),
        name="modulation",
    )(cond, w_mod, b_mod.reshape(1, n))


def _norm_proj_kernel(x_ref, pv_ref, w_ref, o_ref, h_ref):
    @pl.when(pl.program_id(1) == 0)
    def _():
        pv = pv_ref[...]
        h_ref[...] = (_rms(x_ref[...]) * pv[0:1] * pv[1:2] + pv[2:3]).astype(BF16)

    o_ref[...] = _dot(h_ref[...], w_ref[...]).astype(o_ref.dtype)


def norm_proj(x, pv, w, tn, out_dtype):
    m, d = x.shape
    n = w.shape[1]
    tm = min(m, 1024)
    return pl.pallas_call(
        _norm_proj_kernel,
        out_shape=jax.ShapeDtypeStruct((m, n), out_dtype),
        grid=(m // tm, n // tn),
        in_specs=[pl.BlockSpec((tm, d), lambda i, j: (i, 0)),
                  pl.BlockSpec((8, d), lambda i, j: (0, 0)),
                  pl.BlockSpec((d, tn), lambda i, j: (0, j))],
        out_specs=pl.BlockSpec((tm, tn), lambda i, j: (i, j)),
        scratch_shapes=[pltpu.VMEM((tm, d), BF16)],
        compiler_params=_cparams(("parallel", "arbitrary")),
        name="norm_proj",
    )(x, pv, w)


def _mla_proj_kernel(z_ref, cos_ref, sin_ref, gq_ref, gkv_ref, wq_ref, wkv_ref, q_ref, k_ref, v_ref):
    z = z_ref[...].astype(F32)
    tm = z.shape[0]
    cq = z[:, :MLA_Q_RANK]
    ckv = z[:, MLA_Q_RANK:MLA_Q_RANK + MLA_KV_RANK]
    kr = z[:, MLA_Q_RANK + MLA_KV_RANK:]
    cos = cos_ref[...]
    sin = sin_ref[...]
    qn = (_rms(cq) * gq_ref[...]).astype(BF16)
    kvn = (_rms(ckv) * gkv_ref[...]).astype(BF16)
    qf = _dot(qn, wq_ref[...]) * (MLA_QK ** -0.5 * math.log2(math.e))
    kvf = _dot(kvn, wkv_ref[...])
    kr_rot = kr[:, :MLA_ROPE] * cos + kr[:, MLA_ROPE:] * sin
    pad = jnp.zeros((tm, MLA_QK_PAD - MLA_QK), F32)
    lane = lax.broadcasted_iota(jnp.int32, (tm, MLA_V_PAD - HEAD_V), 1)
    ones_col = jnp.where(lane == 0, 1.0, 0.0).astype(BF16)
    for h in range(N_HEADS):
        b = h * 256
        q_rot = qf[:, b + 128:b + 192] * cos + qf[:, b + 192:b + 256] * sin
        q_ref[h] = jnp.concatenate([qf[:, b:b + 128], q_rot, pad], axis=-1).astype(BF16)
        k_ref[h] = jnp.concatenate([kvf[:, b:b + 128], kr_rot, pad], axis=-1).astype(BF16)
        v_ref[h] = jnp.concatenate([kvf[:, b + 128:b + 256].astype(BF16), ones_col], axis=-1)


def mla_project(zmain, cos, sin, gq, gkv, wq_ext, wkv):
    m = zmain.shape[0]
    tm = min(m, 1024)
    full = lambda shape: pl.BlockSpec(shape, lambda i: tuple(0 for _ in shape))
    return pl.pallas_call(
        _mla_proj_kernel,
        out_shape=(jax.ShapeDtypeStruct((N_HEADS, m, MLA_QK_PAD), BF16),
                   jax.ShapeDtypeStruct((N_HEADS, m, MLA_QK_PAD), BF16),
                   jax.ShapeDtypeStruct((N_HEADS, m, MLA_V_PAD), BF16)),
        grid=(m // tm,),
        in_specs=[pl.BlockSpec((tm, MLA_IN_W), lambda i: (i, OFF_MLA // MLA_IN_W)),
                  pl.BlockSpec((tm, MLA_ROPE), lambda i: (i, 0)),
                  pl.BlockSpec((tm, MLA_ROPE), lambda i: (i, 0)),
                  full((1, MLA_Q_RANK)), full((1, MLA_KV_RANK)),
                  full((MLA_Q_RANK, N_HEADS * 256)), full((MLA_KV_RANK, N_HEADS * 256))],
        out_specs=(pl.BlockSpec((N_HEADS, tm, MLA_QK_PAD), lambda i: (0, i, 0)),
                   pl.BlockSpec((N_HEADS, tm, MLA_QK_PAD), lambda i: (0, i, 0)),
                   pl.BlockSpec((N_HEADS, tm, MLA_V_PAD), lambda i: (0, i, 0))),
        compiler_params=_cparams(("parallel",)),
        name="mla_project",
    )(zmain, cos, sin, gq, gkv, wq_ext, wkv)


ATTN_TQ, ATTN_TK = 512, 512
ATTN_UNROLL = 32


def _attn_kernel(*refs, segs):
    q_ref, o_ref = refs[0], refs[-1]
    q = q_ref[0]
    tq = q.shape[0]
    carry = (jnp.full((tq, 1), -1e30, F32), jnp.zeros((tq, MLA_V_PAD), F32))
    for si, (tk, nk) in enumerate(segs):
        k_ref, v_ref = refs[1 + 2 * si], refs[2 + 2 * si]

        def body(c, carry, k_ref=k_ref, v_ref=v_ref, tk=tk):
            m, acc = carry
            start = pl.multiple_of(c * tk, tk)
            s = _dot_nt(q, k_ref[0, pl.ds(start, tk), :])
            m_new = jnp.maximum(m, jnp.max(s, axis=-1, keepdims=True))
            p = jnp.exp2(s - m_new).astype(BF16)
            acc = jnp.exp2(m - m_new) * acc + _dot(p, v_ref[0, pl.ds(start, tk), :])
            return m_new, acc

        carry = lax.fori_loop(0, nk, body, carry, unroll=min(ATTN_UNROLL, nk))
    _, acc = carry
    o_ref[...] = (acc[:, :HEAD_V] / acc[:, HEAD_V:HEAD_V + 1]).astype(o_ref.dtype)


def attention(q, kvs):
    _, lq, _ = q.shape
    tq = min(lq, ATTN_TQ)
    segs, args, specs = [], [], []
    for k, v in kvs:
        lk = k.shape[1]
        tk = min(lk, ATTN_TK)
        segs.append((tk, lk // tk))
        args += [k, v]
        specs += [pl.BlockSpec((1, lk, MLA_QK_PAD), lambda h, i: (h, 0, 0)),
                  pl.BlockSpec((1, lk, MLA_V_PAD), lambda h, i: (h, 0, 0))]
    return pl.pallas_call(
        functools.partial(_attn_kernel, segs=tuple(segs)),
        out_shape=jax.ShapeDtypeStruct((lq, N_HEADS * HEAD_V), BF16),
        grid=(N_HEADS, lq // tq),
        in_specs=[pl.BlockSpec((1, tq, MLA_QK_PAD), lambda h, i: (h, i, 0))] + specs,
        out_specs=pl.BlockSpec((tq, HEAD_V), lambda h, i: (i, h)),
        compiler_params=_cparams(("parallel", "arbitrary")),
        name="attention",
    )(q, *args)


def _merge_kernel(x_ref, g0_ref, g1_ref, g2_ref, g3_ref, a_ref, b_ref, c_ref, d_ref, wb_ref, wo_ref, g_ref, o_ref):
    s = None
    for n, (br, gate_ref) in enumerate(zip((a_ref, b_ref, c_ref, d_ref), (g0_ref, g1_ref, g2_ref, g3_ref))):
        proj = _dot(br[...], wb_ref[n])
        gate = jax.nn.sigmoid(gate_ref[...].astype(F32))
        s = gate * proj if s is None else s + gate * proj
    m = _dot(s.astype(BF16), wo_ref[...])
    o_ref[...] = x_ref[...] + g_ref[0:1] * m


def merge_branches(x, zmain, branches, w_branch, w_out, gvec):
    m = x.shape[0]
    tm = min(m, 512)
    row = lambda w: pl.BlockSpec((tm, w), lambda i: (i, 0))
    return pl.pallas_call(
        _merge_kernel,
        out_shape=jax.ShapeDtypeStruct((m, D_MODEL), F32),
        grid=(m // tm,),
        in_specs=[row(D_MODEL)]
                 + [pl.BlockSpec((tm, D_MODEL), lambda i, n=n: (i, OFF_GATE // D_MODEL + n)) for n in range(N_BRANCH)]
                 + [row(BRANCH_W)] * 4
                 + [pl.BlockSpec((N_BRANCH, BRANCH_W, D_MODEL), lambda i: (0, 0, 0)),
                    pl.BlockSpec((D_MODEL, D_MODEL), lambda i: (0, 0)),
                    pl.BlockSpec((8, D_MODEL), lambda i: (0, 0))],
        out_specs=row(D_MODEL),
        compiler_params=_cparams(("parallel",)),
        name="merge_branches",
    )(x, zmain, zmain, zmain, zmain, *branches, w_branch, w_out, gvec)


def _ffn_kernel(x_ref, pv_ref, wg_ref, wu_ref, wd_ref, o_ref, h_ref, acc_ref):
    f = pl.program_id(1)

    @pl.when(f == 0)
    def _():
        pv = pv_ref[...]
        h_ref[...] = (_rms(x_ref[...]) * pv[0:1] * pv[1:2] + pv[2:3]).astype(BF16)
        acc_ref[...] = jnp.zeros_like(acc_ref)

    h = h_ref[...]
    g = _dot(h, wg_ref[...])
    u = _dot(h, wu_ref[...])
    act = (g * jax.nn.sigmoid(g) * u).astype(BF16)
    acc_ref[...] += _dot(act, wd_ref[...])

    @pl.when(f == pl.num_programs(1) - 1)
    def _():
        o_ref[...] = x_ref[...] + pv_ref[3:4] * acc_ref[...]


def dense_ffn(x, pv, w_gu, w_down):
    m = x.shape[0]
    tm = min(m, 1024)
    tf = 512
    nf = D_FF // tf
    return pl.pallas_call(
        _ffn_kernel,
        out_shape=jax.ShapeDtypeStruct((m, D_MODEL), F32),
        grid=(m // tm, nf),
        in_specs=[pl.BlockSpec((tm, D_MODEL), lambda i, f: (i, 0)),
                  pl.BlockSpec((8, D_MODEL), lambda i, f: (0, 0)),
                  pl.BlockSpec((D_MODEL, tf), lambda i, f: (0, f)),
                  pl.BlockSpec((D_MODEL, tf), lambda i, f: (0, f + nf)),
                  pl.BlockSpec((tf, D_MODEL), lambda i, f: (f, 0))],
        out_specs=pl.BlockSpec((tm, D_MODEL), lambda i, f: (i, 0)),
        scratch_shapes=[pltpu.VMEM((tm, D_MODEL), BF16), pltpu.VMEM((tm, D_MODEL), F32)],
        compiler_params=_cparams(("parallel", "arbitrary")),
        name="dense_ffn",
    )(x, pv, w_gu, w_gu, w_down)


def _router_kernel(x_ref, pv_ref, w_ref, b_ref, h_ref, logit_ref):
    pv = pv_ref[...]
    h = _rms(x_ref[...]) * pv[0:1] * pv[1:2] + pv[2:3]
    h_hi = h.astype(BF16)
    h_lo = (h - h_hi.astype(F32)).astype(BF16)
    w = w_ref[...]
    w_hi = w.astype(BF16)
    w_lo = (w - w_hi.astype(F32)).astype(BF16)
    logits = _dot(h_hi, w_hi) + _dot(h_hi, w_lo) + _dot(h_lo, w_hi)
    h_ref[...] = h
    logit_ref[...] = logits + b_ref[...]


def moe_router(x, pv, w_router_pad, b_router_pad):
    m = x.shape[0]
    tm = min(m, 1024)
    return pl.pallas_call(
        _router_kernel,
        out_shape=(jax.ShapeDtypeStruct((m, D_MODEL), F32), jax.ShapeDtypeStruct((m, 128), F32)),
        grid=(m // tm,),
        in_specs=[pl.BlockSpec((tm, D_MODEL), lambda i: (i, 0)),
                  pl.BlockSpec((8, D_MODEL), lambda i: (0, 0)),
                  pl.BlockSpec((D_MODEL, 128), lambda i: (0, 0)),
                  pl.BlockSpec((1, 128), lambda i: (0, 0))],
        out_specs=(pl.BlockSpec((tm, D_MODEL), lambda i: (i, 0)), pl.BlockSpec((tm, 128), lambda i: (i, 0))),
        compiler_params=_cparams(("parallel",)),
        name="moe_router",
    )(x, pv, w_router_pad, b_router_pad)


def _expert_kernel(be_ref, nb_ref, x_ref, wg_ref, wu_ref, wd_ref, o_ref, acc_ref):
    b = pl.program_id(0)
    f = pl.program_id(1)

    @pl.when(b < nb_ref[0])
    def _():
        @pl.when(f == 0)
        def _():
            acc_ref[...] = jnp.zeros_like(acc_ref)

        x = x_ref[...].astype(BF16)
        g = _dot(x, wg_ref[0].astype(BF16))
        u = _dot(x, wu_ref[0].astype(BF16))
        act = (g * jax.nn.sigmoid(g) * u).astype(BF16)
        acc_ref[...] += _dot(act, wd_ref[0].astype(BF16))

        @pl.when(f == pl.num_programs(1) - 1)
        def _():
            o_ref[...] = acc_ref[...]

    @pl.when((b >= nb_ref[0]) & (f == pl.num_programs(1) - 1))
    def _():
        o_ref[...] = jnp.zeros_like(o_ref)


def expert_ffn(block_e, n_used, xb, w_gu, w_down):
    cap = xb.shape[0]
    nb = cap // MOE_ROWS
    tf = 512
    nf = D_FF // tf

    def live(b, nbr):
        return jnp.minimum(b, nbr[0] - 1)

    def fsel(b, f, nbr):
        return jnp.where(b < nbr[0], f, nf - 1)

    grid_spec = pltpu.PrefetchScalarGridSpec(
        num_scalar_prefetch=2,
        grid=(nb, nf),
        in_specs=[pl.BlockSpec((MOE_ROWS, D_MODEL), lambda b, f, be, nbr: (live(b, nbr), 0)),
                  pl.BlockSpec((1, D_MODEL, tf), lambda b, f, be, nbr: (be[live(b, nbr)], 0, fsel(b, f, nbr))),
                  pl.BlockSpec((1, D_MODEL, tf), lambda b, f, be, nbr: (be[live(b, nbr)], 0, fsel(b, f, nbr) + nf)),
                  pl.BlockSpec((1, tf, D_MODEL), lambda b, f, be, nbr: (be[live(b, nbr)], fsel(b, f, nbr), 0))],
        out_specs=pl.BlockSpec((MOE_ROWS, D_MODEL), lambda b, f, be, nbr: (b, 0)),
        scratch_shapes=[pltpu.VMEM((MOE_ROWS, D_MODEL), F32)],
    )
    return pl.pallas_call(
        _expert_kernel,
        out_shape=jax.ShapeDtypeStruct((cap, D_MODEL), F32),
        grid_spec=grid_spec,
        compiler_params=_cparams(("arbitrary", "arbitrary")),
        name="expert_ffn",
    )(block_e, n_used, xb, w_gu, w_gu, w_down)


def _final_kernel(x_ref, y0_ref, y1_ref, w_ref, pv_ref, o_ref):
    w = w_ref[...]
    f = w[:, 0:1] * y0_ref[...] + w[:, 1:2] * y1_ref[...]
    x = x_ref[...] + pv_ref[0:1] * f
    o_ref[...] = _rms(x) * pv_ref[1:2]


def final_combine(x, y0, y1, w, pv):
    m = x.shape[0]
    tm = min(m, 1024)
    row = lambda wd: pl.BlockSpec((tm, wd), lambda i: (i, 0))
    return pl.pallas_call(
        _final_kernel,
        out_shape=jax.ShapeDtypeStruct((m, D_MODEL), F32),
        grid=(m // tm,),
        in_specs=[row(D_MODEL), row(D_MODEL), row(D_MODEL), row(128),
                  pl.BlockSpec((8, D_MODEL), lambda i: (0, 0))],
        out_specs=row(D_MODEL),
        compiler_params=_cparams(("parallel",)),
        name="final_combine",
    )(x, y0, y1, w, pv)


CH = 128
NEG = -1e30


def _masks(reverse):
    r = lax.broadcasted_iota(jnp.int32, (CH, CH), 0)
    c = lax.broadcasted_iota(jnp.int32, (CH, CH), 1)
    return (r <= c, r < c) if reverse else (r >= c, r > c)


def _cumsum_time(incl, x):
    m = jnp.where(incl, 1.0, 0.0).astype(BF16)
    hi = x.astype(BF16)
    lo = (x - hi.astype(F32)).astype(BF16)
    return _dot(m, hi) + _dot(m, lo)


SCAN_SUB = 4


def _scan_grid(l):
    rows = min(l, SCAN_SUB * CH)
    return l // rows, rows


def _sweep(d, nsub):
    return list(range(nsub)) if d == 0 else list(range(nsub - 1, -1, -1))


def _dir_specs(nsteps, rows, width, col_block):
    return [pl.BlockSpec((rows, width), lambda n: (n, col_block)),
            pl.BlockSpec((rows, width), lambda n: (nsteps - 1 - n, col_block))]


def _whole(a):
    return pl.BlockSpec(a.shape, lambda n: tuple(0 for _ in a.shape))


def _ret_consts():
    log_gamma = np.log(1.0 - 2.0 ** (-5.0 - np.arange(N_HEADS, dtype=np.float64)))
    pos = np.arange(CH, dtype=np.float64)
    diff = pos[:, None] - pos[None, :]
    dec_f = np.where(diff >= 0, np.exp(log_gamma[:, None, None] * diff), 0.0)
    dec = np.stack([dec_f, np.transpose(dec_f, (0, 2, 1))])
    qs_f = np.exp(log_gamma[None, :] * (pos[:, None] + 1.0))
    ks_f = np.exp(log_gamma[None, :] * (CH - 1.0 - pos[:, None]))
    qs_b = np.exp(log_gamma[None, :] * (CH - pos[:, None]))
    ks_b = np.exp(log_gamma[None, :] * pos[:, None])
    rep = lambda a: np.repeat(a, RET_DK, axis=1)
    qs = np.stack([rep(qs_f), rep(qs_b)])
    ks = np.stack([rep(ks_f), rep(ks_b)])
    chunk_decay = [float(np.exp(lg * CH)) for lg in log_gamma]
    return (jnp.asarray(dec, F32), jnp.asarray(qs, F32), jnp.asarray(ks, F32)), chunk_decay


def _ret_kernel(*refs, rotary, chunk_decay):
    if rotary:
        (qf, qb, kf, kb, vf, vb, cosf, cosb, sinf, sinb, dec_ref, qs_ref, ks_ref, s0_ref,
         of_ref, ob_ref, s_ref) = refs
        tabs = ((cosf, sinf), (cosb, sinb))
    else:
        qf, qb, kf, kb, vf, vb, dec_ref, qs_ref, ks_ref, s0_ref, of_ref, ob_ref, s_ref = refs
        tabs = (None, None)

    @pl.when(pl.program_id(0) == 0)
    def _():
        s_ref[...] = s0_ref[...]

    lane = lax.broadcasted_iota(jnp.int32, (CH, N_HEADS * RET_DK), 1)
    first_half = (lane & (RET_DK - 1)) < RET_DK // 2

    nsub = qf.shape[0] // CH
    chains = {}
    for d, (q_ref, k_ref, v_ref, o_ref) in enumerate(((qf, kf, vf, of_ref), (qb, kb, vb, ob_ref))):
        for j in range(nsub):
            rows = slice(j * CH, (j + 1) * CH)
            q = q_ref[rows, :].astype(F32)
            k = k_ref[rows, :].astype(F32) * RET_DK ** -0.5
            if rotary:
                cos, sin = tabs[d][0][rows, :], tabs[d][1][rows, :]

                def rot(x, cos=cos, sin=sin):
                    swapped = jnp.where(first_half, pltpu.roll(x, N_HEADS * RET_DK - RET_DK // 2, 1),
                                        pltpu.roll(x, RET_DK // 2, 1))
                    return x * cos + swapped * sin

                q, k = rot(q), rot(k)
            qb16, kb16 = q.astype(BF16), k.astype(BF16)
            q_in = (q * qs_ref[d]).astype(BF16)
            k_out = k * ks_ref[d]
            k_t = [k_out[:, :128].T, k_out[:, 128:].T]
            for h in range(N_HEADS):
                sl = slice(h * RET_DK, (h + 1) * RET_DK)
                chains[d, j, h] = dict(
                    o_ref=o_ref, rows=rows, vh=v_ref[rows, h * HEAD_V:(h + 1) * HEAD_V], q=qb16[:, sl], k=kb16[:, sl],
                    q_in=q_in[:, sl], k_th=k_t[h // 2][(h % 2) * RET_DK:(h % 2 + 1) * RET_DK, :].astype(BF16))

    for (d, j, h), c in chains.items():
        c["p"] = (_dot_nt(c["q"], c["k"]) * dec_ref[d, h]).astype(BF16)
    for c in chains.values():
        c["intra"] = _dot(c["p"], c["vh"])
        c["update"] = _dot(c["k_th"], c["vh"])
    states = {(d, h): s_ref[d, h] for d in range(2) for h in range(N_HEADS)}
    for i in range(nsub):
        for (d, h), s in list(states.items()):
            c = chains[d, _sweep(d, nsub)[i], h]
            c["o_ref"][c["rows"], h * HEAD_V:(h + 1) * HEAD_V] = c["intra"] + _dot(c["q_in"], s.astype(BF16))
            states[d, h] = chunk_decay[h] * s + c["update"]
    for (d, h), s in states.items():
        s_ref[d, h] = s


def retention_scan(z, s0, tables):
    l = z.shape[0]
    nsteps, rows = _scan_grid(l)
    consts, chunk_decay = _ret_consts()
    rotary = tables is not None
    w = N_HEADS * RET_DK
    specs = (_dir_specs(nsteps, rows, w, OFF_RET_Q // w) + _dir_specs(nsteps, rows, w, OFF_RET_K // w)
             + _dir_specs(nsteps, rows, BRANCH_W, OFF_RET_V // BRANCH_W))
    args = [z] * 6
    if rotary:
        specs += _dir_specs(nsteps, rows, w, 0) + _dir_specs(nsteps, rows, w, 0)
        args += [tables[0], tables[0], tables[1], tables[1]]
    specs += [_whole(c) for c in consts] + [_whole(s0)]
    args += list(consts) + [s0]
    return pl.pallas_call(
        functools.partial(_ret_kernel, rotary=rotary, chunk_decay=chunk_decay),
        out_shape=(jax.ShapeDtypeStruct((l, BRANCH_W), F32), jax.ShapeDtypeStruct((l, BRANCH_W), F32),
                   jax.ShapeDtypeStruct(s0.shape, F32)),
        grid=(nsteps,),
        in_specs=specs,
        out_specs=tuple(_dir_specs(nsteps, rows, BRANCH_W, 0)) + (_whole(s0),),
        compiler_params=_cparams(("arbitrary",)),
        name="retention_scan",
    )(*args)


def _mlstm_kernel(qf, qb, kf, kb, vf, vb, smf, smb, bias_ref, c0_ref, m0_ref, of_ref, ob_ref, c_ref, m_ref):
    @pl.when(pl.program_id(0) == 0)
    def _():
        c_ref[...] = c0_ref[...]
        m_ref[...] = m0_ref[...]

    lane = lax.broadcasted_iota(jnp.int32, (CH, 128), 1)
    is_forget = (lane >= F_LANE) & (lane < F_LANE + 2 * N_HEADS)
    ones_col = jnp.where(lane == 0, 1.0, 0.0).astype(BF16)

    nsub = qf.shape[0] // CH
    chains = {}
    for d, (q_ref, k_ref, v_ref, sm_ref, o_ref) in enumerate(((qf, kf, vf, smf, of_ref), (qb, kb, vb, smb, ob_ref))):
        reverse = d == 1
        incl, _ = _masks(reverse)
        last = 0 if reverse else CH - 1
        for j in range(nsub):
            rows = slice(j * CH, (j + 1) * CH)
            pre = sm_ref[rows, :] + bias_ref[0:1]
            x = jnp.where(is_forget, jax.nn.log_sigmoid(pre), pre)
            b = _cumsum_time(incl, x)
            x_t, b_t = x.T, b.T
            q = (q_ref[rows, :].astype(F32) * MLSTM_DK ** -0.5).astype(BF16)
            kf32 = k_ref[rows, :].astype(F32)
            k_t = [kf32[:, :128].T, kf32[:, 128:].T]
            for h in range(N_HEADS):
                gi, gf = I_LANE + d * N_HEADS + h, F_LANE + d * N_HEADS + h
                sl = slice(h * MLSTM_DK, (h + 1) * MLSTM_DK)
                b_col, b_row = b[:, gf:gf + 1], b_t[gf:gf + 1, :]
                li_row = x_t[gi:gi + 1, :]
                b_last = b[last:last + 1, gf:gf + 1]
                d_log = jnp.where(incl, b_col - b_row + li_row, NEG)
                m_intra = jnp.max(d_log, axis=1, keepdims=True)
                e_row = b_last - b_row + li_row
                chains[d, j, h] = dict(
                    o_ref=o_ref, rows=rows, q=q[:, sl], k=k_ref[rows, sl], b_col=b_col, b_last=b_last, m_intra=m_intra,
                    e_row=e_row, m_end=jnp.max(e_row, axis=1, keepdims=True),
                    w_intra=jnp.exp(d_log - m_intra),
                    k_th=k_t[h // 2][(h % 2) * MLSTM_DK:(h % 2 + 1) * MLSTM_DK, :],
                    v_aug=jnp.concatenate([v_ref[rows, h * HEAD_V:(h + 1) * HEAD_V], ones_col], axis=1))

    for c in chains.values():
        c["p"] = (c["w_intra"] * _dot_nt(c["q"], c["k"])).astype(BF16)
    for c in chains.values():
        c["pv"] = _dot(c["p"], c["v_aug"])
    c_augs = {(d, h): c_ref[d, h] for d in range(2) for h in range(N_HEADS)}
    m_ss = {(d, h): m_ref[d, h][0:1, 0:1] for d in range(2) for h in range(N_HEADS)}
    for i in range(nsub):
        cur = {(d, h): chains[d, _sweep(d, nsub)[i], h] for (d, h) in c_augs}
        qcs = {key: _dot(c["q"], c_augs[key].astype(BF16)) for key, c in cur.items()}
        m_news = {key: jnp.maximum(c["b_last"] + m_ss[key], c["m_end"]) for key, c in cur.items()}
        updates = {key: _dot((c["k_th"] * jnp.exp(c["e_row"] - m_news[key])).astype(BF16), c["v_aug"])
                   for key, c in cur.items()}
        for (d, h), c in cur.items():
            m_s = m_ss[d, h]
            m_t = jnp.maximum(c["b_col"] + m_s, c["m_intra"])
            inter = jnp.exp(c["b_col"] + m_s - m_t)
            r = jnp.exp(c["m_intra"] - m_t)
            tot = inter * qcs[d, h] + r * c["pv"]
            den = jnp.maximum(jnp.abs(tot[:, HEAD_V:HEAD_V + 1]), jnp.exp(-m_t))
            c["o_ref"][c["rows"], h * HEAD_V:(h + 1) * HEAD_V] = tot[:, :HEAD_V] / den
        for key, c in cur.items():
            c_augs[key] = jnp.exp(c["b_last"] + m_ss[key] - m_news[key]) * c_augs[key] + updates[key]
            m_ss[key] = m_news[key]
    for (d, h) in c_augs:
        c_ref[d, h] = c_augs[d, h]
        m_ref[d, h] = jnp.broadcast_to(m_ss[d, h], (8, 128))


def mlstm_scan(z, small, bias, c0, m0):
    l = z.shape[0]
    nsteps, rows = _scan_grid(l)
    w = N_HEADS * MLSTM_DK
    specs = (_dir_specs(nsteps, rows, w, OFF_MLSTM_Q // w) + _dir_specs(nsteps, rows, w, OFF_MLSTM_K // w)
             + _dir_specs(nsteps, rows, BRANCH_W, OFF_MLSTM_V // BRANCH_W) + _dir_specs(nsteps, rows, 128, 0))
    specs += [_whole(bias), _whole(c0), _whole(m0)]
    return pl.pallas_call(
        _mlstm_kernel,
        out_shape=(jax.ShapeDtypeStruct((l, BRANCH_W), F32), jax.ShapeDtypeStruct((l, BRANCH_W), F32),
                   jax.ShapeDtypeStruct(c0.shape, F32), jax.ShapeDtypeStruct(m0.shape, F32)),
        grid=(nsteps,),
        in_specs=specs,
        out_specs=tuple(_dir_specs(nsteps, rows, BRANCH_W, 0)) + (_whole(c0), _whole(m0)),
        compiler_params=_cparams(("arbitrary",)),
        name="mlstm_scan",
    )(z, z, z, z, z, z, small, small, bias, c0, m0)


QKV_W = 3 * N_HEADS * GDN_DK
HALO = 8


def _gdn_prep_kernel(x_ref, prev_ref, next_ref, w_ref, q_ref, k_ref, v_ref):
    i = pl.program_id(0)
    tm = x_ref.shape[0]
    x = x_ref[...].astype(F32)
    prev = jnp.where(i > 0, prev_ref[...].astype(F32), 0.0)
    nxt = jnp.where(i < pl.num_programs(0) - 1, next_ref[...].astype(F32), 0.0)
    xe = jnp.concatenate([prev, x, nxt], axis=0)
    w = w_ref[...]
    y = None
    for tap in range(CONV_K):
        off = HALO + tap - CONV_K // 2
        term = w[tap:tap + 1] * xe[off:off + tm]
        y = term if y is None else y + term
    y = y * jax.nn.sigmoid(y)
    hw = N_HEADS * GDN_DK
    for h in range(N_HEADS):
        sl = slice(h * GDN_DK, (h + 1) * GDN_DK)
        qh = y[:, sl]
        kh = y[:, hw + h * GDN_DK:hw + (h + 1) * GDN_DK]
        q_ref[:, sl] = (qh * lax.rsqrt(jnp.sum(qh * qh, axis=-1, keepdims=True) + EPS) * GDN_DK ** -0.5).astype(BF16)
        k_ref[:, sl] = (kh * lax.rsqrt(jnp.sum(kh * kh, axis=-1, keepdims=True) + EPS)).astype(BF16)
    v_ref[...] = y[:, 2 * hw:].astype(BF16)


def gdn_prep(z, conv_w):
    l = z.shape[0]
    tm = min(l, 256)
    nb = l // tm
    r8 = tm // HALO
    cb = OFF_GDN_QKV // QKV_W
    w8 = jnp.concatenate([conv_w.astype(F32), jnp.zeros((8 - CONV_K, QKV_W), F32)], axis=0)
    return pl.pallas_call(
        _gdn_prep_kernel,
        out_shape=tuple(jax.ShapeDtypeStruct((l, BRANCH_W), BF16) for _ in range(3)),
        grid=(nb,),
        in_specs=[pl.BlockSpec((tm, QKV_W), lambda i: (i, cb)),
                  pl.BlockSpec((HALO, QKV_W), lambda i: (jnp.maximum(i * r8 - 1, 0), cb)),
                  pl.BlockSpec((HALO, QKV_W), lambda i: (jnp.minimum((i + 1) * r8, nb * r8 - 1), cb)),
                  pl.BlockSpec((8, QKV_W), lambda i: (0, 0))],
        out_specs=tuple(pl.BlockSpec((tm, BRANCH_W), lambda i: (i, 0)) for _ in range(3)),
        compiler_params=_cparams(("parallel",)),
        name="gdn_prep",
    )(z, z, z, w8)


N_LEVELS = 7


def _gdn_kernel(qf, qb, kf, kb, vf, vb, smf, smb, par_ref, s0_ref, of_ref, ob_ref, s_ref):
    @pl.when(pl.program_id(0) == 0)
    def _():
        s_ref[...] = s0_ref[...]

    lane = lax.broadcasted_iota(jnp.int32, (CH, 128), 1)
    is_decay = lane < B_LANE
    ri = lax.broadcasted_iota(jnp.int32, (CH, CH), 0)
    ci = lax.broadcasted_iota(jnp.int32, (CH, CH), 1)
    eye = jnp.where(ri == ci, 1.0, 0.0)
    pair_masks = [((ri >> (l + 1)) == (ci >> (l + 1))) & ((ri >> l) != (ci >> l)) for l in range(N_LEVELS)]

    nsub = qf.shape[0] // CH
    chains = {}
    for d, (q_ref, k_ref, v_ref, sm_ref, o_ref) in enumerate(((qf, kf, vf, smf, of_ref), (qb, kb, vb, smb, ob_ref))):
        reverse = d == 1
        incl, strict = _masks(reverse)
        last = 0 if reverse else CH - 1
        for j in range(nsub):
            rows = slice(j * CH, (j + 1) * CH)
            sm = sm_ref[rows, :]
            log_a = -jnp.exp(par_ref[1:2]) * jax.nn.softplus(sm + par_ref[0:1])
            x = jnp.where(is_decay, log_a, jax.nn.sigmoid(sm))
            g = _cumsum_time(incl, x)
            g_t = g.T
            for h in range(N_HEADS):
                ga, gb = A_LANE + d * N_HEADS + h, B_LANE + d * N_HEADS + h
                sl = slice(h * GDN_DK, (h + 1) * GDN_DK)
                g_col, g_row = g[:, ga:ga + 1], g_t[ga:ga + 1, :]
                g_last = g[last:last + 1, ga:ga + 1]
                beta = x[:, gb:gb + 1]
                decay = jnp.where(incl, jnp.exp(jnp.minimum(g_col - g_row, 0.0)), 0.0)
                kh = k_ref[rows, sl]
                kf32 = kh.astype(F32)
                kbeta = kf32 * beta
                a = jnp.where(strict, _dot_nt(kbeta.astype(BF16), kh) * decay, 0.0)
                e_g = jnp.exp(g_col)
                rhs = jnp.concatenate([v_ref[rows, sl].astype(F32) * beta, kbeta * e_g], axis=1)
                chains[d, j, h] = dict(
                    rows=rows, sl=sl, o_ref=o_ref, a=a, rhs=rhs, s_decay=jnp.exp(g_last),
                    qk=(_dot_nt(q_ref[rows, sl], kh) * decay).astype(BF16),
                    q_in=(q_ref[rows, sl].astype(F32) * e_g).astype(BF16),
                    k_out_t=(kf32.T * jnp.exp(g_last - g_row)).astype(BF16))

    cl = list(chains.values())
    xs = [eye - jnp.where(pair_masks[0], c["a"], 0.0) for c in cl]
    for pm in pair_masks[1:]:
        ys = [_dot(jnp.where(pm, c["a"], 0.0).astype(BF16), x.astype(BF16)) for c, x in zip(cl, xs)]
        xs = [x - _dot(x.astype(BF16), y.astype(BF16)) for x, y in zip(xs, ys)]
    for c, x in zip(cl, xs):
        c["sol"] = c["rhs"] + _dot((x - eye).astype(BF16), c["rhs"].astype(BF16))
    states = {(d, h): s_ref[d, h] for d in range(2) for h in range(N_HEADS)}
    for i in range(nsub):
        cur = {(d, h): chains[d, _sweep(d, nsub)[i], h] for (d, h) in states}
        s16s = {key: s.astype(BF16) for key, s in states.items()}
        v_news = {key: (c["sol"][:, :HEAD_V] - _dot(c["sol"][:, HEAD_V:].astype(BF16), s16s[key])).astype(BF16)
                  for key, c in cur.items()}
        updates = {key: _dot(c["k_out_t"], v_news[key]) for key, c in cur.items()}
        for key, c in cur.items():
            c["o_ref"][c["rows"], c["sl"]] = _dot(c["q_in"], s16s[key]) + _dot(c["qk"], v_news[key])
        for key, c in cur.items():
            states[key] = states[key] * c["s_decay"] + updates[key]
    for (d, h), s in states.items():
        s_ref[d, h] = s


def gdn_scan(qn, kn, vn, small, par, s0):
    l = qn.shape[0]
    nsteps, rows = _scan_grid(l)
    specs = _dir_specs(nsteps, rows, BRANCH_W, 0) * 3 + _dir_specs(nsteps, rows, 128, 0)
    specs += [_whole(par), _whole(s0)]
    return pl.pallas_call(
        _gdn_kernel,
        out_shape=(jax.ShapeDtypeStruct((l, BRANCH_W), F32), jax.ShapeDtypeStruct((l, BRANCH_W), F32),
                   jax.ShapeDtypeStruct(s0.shape, F32)),
        grid=(nsteps,),
        in_specs=specs,
        out_specs=tuple(_dir_specs(nsteps, rows, BRANCH_W, 0)) + (_whole(s0),),
        compiler_params=_cparams(("arbitrary",)),
        name="gdn_scan",
    )(qn, qn, kn, kn, vn, vn, small, small, par, s0)


def _post_kernel(of_ref, ob_ref, gate_ref, g_ref, o_ref, *, centre, silu_gate):
    o = of_ref[...] + ob_ref[...]
    gate = gate_ref[...].astype(F32)
    sig = jax.nn.sigmoid(gate)
    act = gate * sig if silu_gate else sig
    for h in range(N_HEADS):
        sl = slice(h * HEAD_V, (h + 1) * HEAD_V)
        oh = o[:, sl]
        if centre:
            oh = oh - jnp.mean(oh, axis=-1, keepdims=True)
        oh = oh * lax.rsqrt(jnp.mean(oh * oh, axis=-1, keepdims=True) + EPS)
        o_ref[:, sl] = (oh * g_ref[:, sl] * act[:, sl]).astype(BF16)


def head_post(o_f, o_b, z, gate_off, norm_g, centre, silu_gate):
    l = o_f.shape[0]
    tm = min(l, 1024)
    row = pl.BlockSpec((tm, BRANCH_W), lambda i: (i, 0))
    return pl.pallas_call(
        functools.partial(_post_kernel, centre=centre, silu_gate=silu_gate),
        out_shape=jax.ShapeDtypeStruct((l, BRANCH_W), BF16),
        grid=(l // tm,),
        in_specs=[row, row, pl.BlockSpec((tm, BRANCH_W), lambda i: (i, gate_off // BRANCH_W)),
                  pl.BlockSpec((1, BRANCH_W), lambda i: (0, 0))],
        out_specs=row,
        compiler_params=_cparams(("parallel",)),
        name="head_post",
    )(o_f, o_b, z, norm_g.reshape(1, BRANCH_W).astype(F32))


def _lane_rows(entries):
    r = jnp.zeros((8, 128), F32)
    for row, lane, vals in entries:
        r = r.at[row, lane:lane + vals.shape[0]].set(vals.astype(F32))
    return r


def gdn_branch(zc, sc, zl, sl, conv_w, a_log, dt_bias, norm_g, with_ctx):
    par = _lane_rows([(0, A_LANE, dt_bias.reshape(-1)), (1, A_LANE, a_log.reshape(-1))])
    s0 = jnp.zeros((2, N_HEADS, GDN_DK, HEAD_V), F32)
    ofc, obc, s1 = gdn_scan(*gdn_prep(zc, conv_w), sc, par, s0)
    ofl, obl, _ = gdn_scan(*gdn_prep(zl, conv_w), sl, par, s1)
    out_c = head_post(ofc, obc, zc, OFF_GDN_Z, norm_g, False, True) if with_ctx else None
    return out_c, head_post(ofl, obl, zl, OFF_GDN_Z, norm_g, False, True)


def mlstm_branch(zc, sc, zl, sl, gate_b, norm_g, with_ctx):
    bias = _lane_rows([(0, I_LANE, gate_b[0].reshape(-1)), (0, F_LANE, gate_b[1].reshape(-1))])
    c0 = jnp.zeros((2, N_HEADS, MLSTM_DK, 2 * HEAD_V), F32)
    m0 = jnp.zeros((2, N_HEADS, 8, 128), F32)
    ofc, obc, c1, m1 = mlstm_scan(zc, sc, bias, c0, m0)
    ofl, obl, _, _ = mlstm_scan(zl, sl, bias, c1, m1)
    out_c = head_post(ofc, obc, zc, OFF_MLSTM_O, norm_g, False, False) if with_ctx else None
    return out_c, head_post(ofl, obl, zl, OFF_MLSTM_O, norm_g, False, False)


def retention_branch(zc, zl, tables, norm_g, with_ctx):
    s0 = jnp.zeros((2, N_HEADS, RET_DK, HEAD_V), F32)
    ofc, obc, s1 = retention_scan(zc, s0, None)
    ofl, obl, _ = retention_scan(zl, s1, tables)
    out_c = head_post(ofc, obc, zc, OFF_RET_G, norm_g, True, True) if with_ctx else None
    return out_c, head_post(ofl, obl, zl, OFF_RET_G, norm_g, True, True)


def _ret_rope_tables(n_lat):
    inv = ROPE_BASE ** (-jnp.arange(0, RET_DK, 2, dtype=F32) / RET_DK)
    ang = jnp.arange(n_lat, dtype=F32)[:, None] * inv[None, :]
    cos = jnp.concatenate([jnp.cos(ang), jnp.cos(ang)], axis=-1)
    sin = jnp.concatenate([-jnp.sin(ang), jnp.sin(ang)], axis=-1)
    return jnp.tile(cos, (1, N_HEADS)), jnp.tile(sin, (1, N_HEADS))


def _rope_tables(n_lat):
    q = MLA_ROPE // 4
    inv = ROPE_BASE ** (-jnp.arange(0, 2 * q, 2, dtype=F32) / (2 * q))
    t = jnp.arange(n_lat)
    row = (t // GRID_W).astype(F32)[:, None] * inv[None, :]
    col = (t % GRID_W).astype(F32)[:, None] * inv[None, :]
    cos = jnp.concatenate([jnp.cos(row), jnp.cos(row), jnp.cos(col), jnp.cos(col)], axis=-1)
    sin = jnp.concatenate([-jnp.sin(row), jnp.sin(row), -jnp.sin(col), jnp.sin(col)], axis=-1)
    return cos, sin


def _pad_rows(v, n=8):
    rows = [jnp.reshape(r, (1, -1)).astype(F32) for r in v]
    d = rows[0].shape[1]
    return jnp.concatenate(rows + [jnp.zeros((n - len(rows), d), F32)], axis=0)


def _mla_weights(w_uq, w_ukv):
    wq = w_uq.reshape(MLA_Q_RANK, N_HEADS, MLA_QK)
    rope = wq[:, :, MLA_NOPE:]
    swapped = rope[:, :, _rope_swap(np.arange(MLA_ROPE))]
    wq_ext = jnp.concatenate([wq[:, :, :MLA_NOPE], rope, swapped], axis=-1).reshape(MLA_Q_RANK, N_HEADS * 256)
    return wq_ext.astype(BF16), w_ukv.astype(BF16)


def _moe(x_l, pv2, g2, w_router, b_router, w_gu, w_down, final_g):
    n = x_l.shape[0]
    w_pad = jnp.concatenate([w_router, jnp.zeros((D_MODEL, 128 - N_EXPERTS), F32)], axis=1)
    b_pad = jnp.concatenate([b_router, jnp.full((128 - N_EXPERTS,), -1e30, F32)]).reshape(1, 128)
    h2, logits = moe_router(x_l, pv2, w_pad, b_pad)
    top_logit, top_e = lax.top_k(logits[:, :N_EXPERTS], TOP_K)
    weight = jax.nn.softmax(top_logit, axis=-1)
    n_assign = n * TOP_K
    flat_e = top_e.reshape(-1)
    onehot = (flat_e[:, None] == jnp.arange(N_EXPERTS)[None, :]).astype(jnp.int32)
    csum = jnp.cumsum(onehot, axis=0)
    counts = csum[-1]
    rank = jnp.take_along_axis(csum, flat_e[:, None], axis=1)[:, 0] - 1
    padded = (counts + MOE_ROWS - 1) // MOE_ROWS * MOE_ROWS
    pad_end = jnp.cumsum(padded)
    dest = (pad_end - padded)[flat_e] + rank
    nb = n_assign // MOE_ROWS + N_EXPERTS
    cap = nb * MOE_ROWS
    block_e = jnp.minimum(jnp.searchsorted(pad_end, jnp.arange(nb) * MOE_ROWS, side='right'),
                          N_EXPERTS - 1).astype(jnp.int32)
    n_used = (pad_end[-1] // MOE_ROWS).astype(jnp.int32).reshape(1)
    d2 = dest.reshape(n, TOP_K)
    slot_tok = (jnp.arange(cap, dtype=jnp.int32) % n).at[dest].set(jnp.arange(n_assign, dtype=jnp.int32) // TOP_K)
    yb = expert_ffn(block_e, n_used, h2[slot_tok], w_gu, w_down)
    w128 = jnp.concatenate([weight, jnp.zeros((n, 126), F32)], axis=1)
    return final_combine(x_l, yb[d2[:, 0]], yb[d2[:, 1]], w128, _pad_rows([g2, final_g]))


def kernel(x, c, ctx, c_ctx, w_mod, b_mod, norm1_g, norm2_g, w_in, gdn_conv_w, gdn_a_log, gdn_dt_bias, gdn_norm_g, mla_q_norm_g, mla_kv_norm_g, mla_w_uq, mla_w_ukv, mlstm_gate_b, mlstm_norm_g, ret_norm_g, w_branch, w_out, ffn_w_in, ffn_w_down, moe_w_router, moe_b_router, moe_w_in, moe_w_down, final_norm_g):
    n_lat = x.shape[1]
    n_ctx = ctx.shape[1]
    x_l, x_c = x[0], ctx[0]
    cond = _pad_rows([c_ctx, c[0]])
    cos_l, sin_l = _rope_tables(n_lat)
    cos_c, sin_c = jnp.ones((n_ctx, MLA_ROPE), F32), jnp.zeros((n_ctx, MLA_ROPE), F32)
    ret_tables = _ret_rope_tables(n_lat)
    out = None
    for li in range(DEPTH):
        last = li == DEPTH - 1
        mod = modulation_vectors(cond, w_mod[li], b_mod[li])
        csh1, csc1, cg1, csh2, csc2, cg2 = jnp.split(mod[0], 6)
        sh1, sc1, g1, sh2, sc2, g2 = jnp.split(mod[1], 6)
        w_main = w_in[li][:, _MAIN_PERM].astype(BF16)
        w_small = w_in[li][:, _SMALL_PERM].astype(BF16)
        pv_l = _pad_rows([norm1_g[li], 1 + sc1, sh1])
        pv_c = _pad_rows([norm1_g[li], 1 + csc1, csh1])
        zl = norm_proj(x_l, pv_l, w_main, 768, BF16)
        zc = norm_proj(x_c, pv_c, w_main, 768, BF16)
        sl = norm_proj(x_l, pv_l, w_small, 128, F32)
        sc = norm_proj(x_c, pv_c, w_small, 128, F32)

        a_c, a_l = gdn_branch(zc, sc, zl, sl, gdn_conv_w[li], gdn_a_log[li], gdn_dt_bias[li], gdn_norm_g[li], not last)
        c_c, c_l = mlstm_branch(zc, sc, zl, sl, mlstm_gate_b[li], mlstm_norm_g[li], not last)
        d_c, d_l = retention_branch(zc, zl, ret_tables, ret_norm_g[li], not last)

        wq_ext, wkv = _mla_weights(mla_w_uq[li], mla_w_ukv[li])
        gq, gkv = mla_q_norm_g[li].reshape(1, -1), mla_kv_norm_g[li].reshape(1, -1)
        ql, kl, vl = mla_project(zl, cos_l, sin_l, gq, gkv, wq_ext, wkv)
        qc, kc, vc = mla_project(zc, cos_c, sin_c, gq, gkv, wq_ext, wkv)
        b_l = attention(ql, [(kl, vl), (kc, vc)])

        wb = w_branch[li].astype(BF16)
        wo = w_out[li].astype(BF16)
        x_l = merge_branches(x_l, zl, (a_l, b_l, c_l, d_l), wb, wo, _pad_rows([g1]))
        if not last:
            b_c = attention(qc, [(kc, vc)])
            x_c = merge_branches(x_c, zc, (a_c, b_c, c_c, d_c), wb, wo, _pad_rows([cg1]))

        if li % 2 == 0:
            w_gu = ffn_w_in[li // 2].astype(BF16)
            w_dn = ffn_w_down[li // 2].astype(BF16)
            x_l = dense_ffn(x_l, _pad_rows([norm2_g[li], 1 + sc2, sh2, g2]), w_gu, w_dn)
            if not last:
                x_c = dense_ffn(x_c, _pad_rows([norm2_g[li], 1 + csc2, csh2, cg2]), w_gu, w_dn)
            if last:
                out = final_combine(x_l, jnp.zeros_like(x_l), jnp.zeros_like(x_l),
                                    jnp.zeros((n_lat, 128), F32), _pad_rows([jnp.zeros_like(g2), final_norm_g]))
        else:
            assert last
            out = _moe(x_l, _pad_rows([norm2_g[li], 1 + sc2, sh2]), g2, moe_w_router[li // 2],
                       moe_b_router[li // 2], moe_w_in[li // 2], moe_w_down[li // 2], final_norm_g)
    return out[None]
```

```python
import functools
import math

import numpy as np
import jax
import jax.numpy as jnp
from jax import lax
from jax.experimental import pallas as pl
from jax.experimental.pallas import tpu as pltpu

F32 = jnp.float32
BF16 = jnp.bfloat16

D_MODEL = 1024
DEPTH = 2
GRID_W = 64
N_BRANCH = 4
N_HEADS = 4
HEAD_V = 128
BRANCH_W = N_HEADS * HEAD_V
GDN_DK = 128
CONV_K = 5
MLA_Q_RANK = 384
MLA_KV_RANK = 256
MLA_NOPE = 128
MLA_ROPE = 64
MLA_QK = MLA_NOPE + MLA_ROPE
MLA_QK_PAD = 256
MLA_V_PAD = 256
MLSTM_DK = 64
RET_DK = 64
ROPE_BASE = 10000.0
D_FF = 3584
N_EXPERTS = 8
TOP_K = 2
EPS = 1e-6

IN_WIDTHS = (
    N_BRANCH * D_MODEL,
    N_HEADS * GDN_DK, N_HEADS * GDN_DK, BRANCH_W, BRANCH_W, 2 * N_HEADS, 2 * N_HEADS,
    MLA_Q_RANK, MLA_KV_RANK, MLA_ROPE,
    N_HEADS * MLSTM_DK, N_HEADS * MLSTM_DK, BRANCH_W, BRANCH_W, 2 * N_HEADS, 2 * N_HEADS,
    N_HEADS * RET_DK, N_HEADS * RET_DK, BRANCH_W, BRANCH_W,
)
_IN_OFF = [0] + [int(o) for o in np.cumsum(IN_WIDTHS)]

VMEM_LIMIT = 48 * 1024 * 1024
MERGE_VMEM_LIMIT = 56 * 1024 * 1024
MOE_ROWS = 1024


def _cols(group):
    return np.arange(_IN_OFF[group], _IN_OFF[group + 1])


def _rope_swap(cols):
    q = MLA_ROPE // 4
    return np.concatenate([cols[q:2 * q], cols[:q], cols[3 * q:], cols[2 * q:3 * q]])


_MAIN_PERM = np.concatenate([
    _cols(1), _cols(2), _cols(3),
    _cols(4),
    _cols(0),
    _cols(12), _cols(13),
    _cols(10), _cols(11),
    _cols(18), _cols(19),
    _cols(16), _cols(17),
    _cols(7), _cols(8), _cols(9), _rope_swap(_cols(9)),
])
N_MAIN = int(_MAIN_PERM.shape[0])
_SMALL_PERM = np.concatenate([_cols(5), _cols(6), _cols(14), _cols(15)])
N_SMALL = 128
A_LANE, B_LANE, I_LANE, F_LANE = 0, 8, 16, 24
OFF_GDN_QKV, OFF_GDN_Z, OFF_GATE = 0, 1536, 2048
OFF_MLSTM_V, OFF_MLSTM_O, OFF_MLSTM_Q, OFF_MLSTM_K = 6144, 6656, 7168, 7424
OFF_RET_V, OFF_RET_G, OFF_RET_Q, OFF_RET_K = 7680, 8192, 8704, 8960
OFF_MLA = 9216
MLA_IN_W = 768


def _cparams(sem):
    return pltpu.CompilerParams(dimension_semantics=sem, vmem_limit_bytes=VMEM_LIMIT)


def _rms(x):
    return x * lax.rsqrt(jnp.mean(x * x, axis=-1, keepdims=True) + EPS)


def _dot(a, b):
    return jnp.dot(a, b, preferred_element_type=F32)


def _dot_nt(a, b):
    return lax.dot_general(a, b, (((1,), (1,)), ((), ())), preferred_element_type=F32)


def _mod_kernel(c_ref, w_ref, b_ref, o_ref):
    c = c_ref[...]
    s = c * jax.nn.sigmoid(c)
    o_ref[...] = jnp.dot(s, w_ref[...], preferred_element_type=F32) + b_ref[...]


def modulation_vectors(cond, w_mod, b_mod):
    n = w_mod.shape[1]
    tn = 1536
    return pl.pallas_call(
        _mod_kernel,
        out_shape=jax.ShapeDtypeStruct((8, n), F32),
        grid=(n // tn,),
        in_specs=[pl.BlockSpec((8, D_MODEL), lambda j: (0, 0)),
                  pl.BlockSpec((D_MODEL, tn), lambda j: (0, j)),
                  pl.BlockSpec((1, tn), lambda j: (0, j))],
        out_specs=pl.BlockSpec((8, tn), lambda j: (0, j)),
        compiler_params=_cparams(("arbitrary",)),
        name="modulation",
    )(cond, w_mod, b_mod.reshape(1, n))


def _norm_proj_kernel(x_ref, pv_ref, w_ref, o_ref, h_ref):
    @pl.when(pl.program_id(1) == 0)
    def _():
        pv = pv_ref[...]
        h_ref[...] = (_rms(x_ref[...]) * pv[0:1] * pv[1:2] + pv[2:3]).astype(BF16)

    o_ref[...] = _dot(h_ref[...], w_ref[...]).astype(o_ref.dtype)


def norm_proj(x, pv, w, tn, out_dtype):
    m, d = x.shape
    n = w.shape[1]
    tm = min(m, 1024)
    return pl.pallas_call(
        _norm_proj_kernel,
        out_shape=jax.ShapeDtypeStruct((m, n), out_dtype),
        grid=(m // tm, n // tn),
        in_specs=[pl.BlockSpec((tm, d), lambda i, j: (i, 0)),
                  pl.BlockSpec((8, d), lambda i, j: (0, 0)),
                  pl.BlockSpec((d, tn), lambda i, j: (0, j))],
        out_specs=pl.BlockSpec((tm, tn), lambda i, j: (i, j)),
        scratch_shapes=[pltpu.VMEM((tm, d), BF16)],
        compiler_params=_cparams(("parallel", "arbitrary")),
        name="norm_proj",
    )(x, pv, w)


def _mla_proj_kernel(z_ref, cos_ref, sin_ref, gq_ref, gkv_ref, wq_ref, wkv_ref, q_ref, k_ref, v_ref):
    z = z_ref[...].astype(F32)
    tm = z.shape[0]
    cq = z[:, :MLA_Q_RANK]
    ckv = z[:, MLA_Q_RANK:MLA_Q_RANK + MLA_KV_RANK]
    kr = z[:, MLA_Q_RANK + MLA_KV_RANK:]
    cos = cos_ref[...]
    sin = sin_ref[...]
    qn = (_rms(cq) * gq_ref[...]).astype(BF16)
    kvn = (_rms(ckv) * gkv_ref[...]).astype(BF16)
    qf = _dot(qn, wq_ref[...]) * (MLA_QK ** -0.5 * math.log2(math.e))
    kvf = _dot(kvn, wkv_ref[...])
    kr_rot = kr[:, :MLA_ROPE] * cos + kr[:, MLA_ROPE:] * sin
    pad = jnp.zeros((tm, MLA_QK_PAD - MLA_QK), F32)
    lane = lax.broadcasted_iota(jnp.int32, (tm, MLA_V_PAD - HEAD_V), 1)
    ones_col = jnp.where(lane == 0, 1.0, 0.0).astype(BF16)
    for h in range(N_HEADS):
        b = h * 256
        q_rot = qf[:, b + 128:b + 192] * cos + qf[:, b + 192:b + 256] * sin
        q_ref[h] = jnp.concatenate([qf[:, b:b + 128], q_rot, pad], axis=-1).astype(BF16)
        k_ref[h] = jnp.concatenate([kvf[:, b:b + 128], kr_rot, pad], axis=-1).astype(BF16)
        v_ref[h] = jnp.concatenate([kvf[:, b + 128:b + 256].astype(BF16), ones_col], axis=-1)


def mla_project(zmain, cos, sin, gq, gkv, wq_ext, wkv):
    m = zmain.shape[0]
    tm = min(m, 1024)
    full = lambda shape: pl.BlockSpec(shape, lambda i: tuple(0 for _ in shape))
    return pl.pallas_call(
        _mla_proj_kernel,
        out_shape=(jax.ShapeDtypeStruct((N_HEADS, m, MLA_QK_PAD), BF16),
                   jax.ShapeDtypeStruct((N_HEADS, m, MLA_QK_PAD), BF16),
                   jax.ShapeDtypeStruct((N_HEADS, m, MLA_V_PAD), BF16)),
        grid=(m // tm,),
        in_specs=[pl.BlockSpec((tm, MLA_IN_W), lambda i: (i, OFF_MLA // MLA_IN_W)),
                  pl.BlockSpec((tm, MLA_ROPE), lambda i: (i, 0)),
                  pl.BlockSpec((tm, MLA_ROPE), lambda i: (i, 0)),
                  full((1, MLA_Q_RANK)), full((1, MLA_KV_RANK)),
                  full((MLA_Q_RANK, N_HEADS * 256)), full((MLA_KV_RANK, N_HEADS * 256))],
        out_specs=(pl.BlockSpec((N_HEADS, tm, MLA_QK_PAD), lambda i: (0, i, 0)),
                   pl.BlockSpec((N_HEADS, tm, MLA_QK_PAD), lambda i: (0, i, 0)),
                   pl.BlockSpec((N_HEADS, tm, MLA_V_PAD), lambda i: (0, i, 0))),
        compiler_params=_cparams(("parallel",)),
        name="mla_project",
    )(zmain, cos, sin, gq, gkv, wq_ext, wkv)


ATTN_TQ, ATTN_TK = 512, 512
ATTN_UNROLL = 32


def _attn_kernel(*refs, segs):
    q_ref, o_ref = refs[0], refs[-1]
    q = q_ref[0]
    tq = q.shape[0]
    carry = (jnp.full((tq, 1), -1e30, F32), jnp.zeros((tq, MLA_V_PAD), F32))
    for si, (tk, nk) in enumerate(segs):
        k_ref, v_ref = refs[1 + 2 * si], refs[2 + 2 * si]

        def body(c, carry, k_ref=k_ref, v_ref=v_ref, tk=tk):
            m, acc = carry
            start = pl.multiple_of(c * tk, tk)
            s = _dot_nt(q, k_ref[0, pl.ds(start, tk), :])
            m_new = jnp.maximum(m, jnp.max(s, axis=-1, keepdims=True))
            p = jnp.exp2(s - m_new).astype(BF16)
            acc = jnp.exp2(m - m_new) * acc + _dot(p, v_ref[0, pl.ds(start, tk), :])
            return m_new, acc

        carry = lax.fori_loop(0, nk, body, carry, unroll=min(ATTN_UNROLL, nk))
    _, acc = carry
    o_ref[...] = (acc[:, :HEAD_V] / acc[:, HEAD_V:HEAD_V + 1]).astype(o_ref.dtype)


def attention(q, kvs):
    _, lq, _ = q.shape
    tq = min(lq, ATTN_TQ)
    segs, args, specs = [], [], []
    for k, v in kvs:
        lk = k.shape[1]
        tk = min(lk, ATTN_TK)
        segs.append((tk, lk // tk))
        args += [k, v]
        specs += [pl.BlockSpec((1, lk, MLA_QK_PAD), lambda h, i: (h, 0, 0)),
                  pl.BlockSpec((1, lk, MLA_V_PAD), lambda h, i: (h, 0, 0))]
    return pl.pallas_call(
        functools.partial(_attn_kernel, segs=tuple(segs)),
        out_shape=jax.ShapeDtypeStruct((lq, N_HEADS * HEAD_V), BF16),
        grid=(N_HEADS, lq // tq),
        in_specs=[pl.BlockSpec((1, tq, MLA_QK_PAD), lambda h, i: (h, i, 0))] + specs,
        out_specs=pl.BlockSpec((tq, HEAD_V), lambda h, i: (i, h)),
        compiler_params=_cparams(("parallel", "arbitrary")),
        name="attention",
    )(q, *args)


def _head_post(o, gate, gain, centre, silu_gate):
    sig = jax.nn.sigmoid(gate)
    act = gate * sig if silu_gate else sig
    outs = []
    for h in range(N_HEADS):
        sl = slice(h * HEAD_V, (h + 1) * HEAD_V)
        oh = o[:, sl]
        if centre:
            oh = oh - jnp.mean(oh, axis=-1, keepdims=True)
        oh = oh * lax.rsqrt(jnp.mean(oh * oh, axis=-1, keepdims=True) + EPS)
        outs.append((oh * gain[:, sl] * act[:, sl]).astype(BF16))
    return jnp.concatenate(outs, axis=1)


def _merge_kernel(x_ref, g0_ref, g1_ref, g2_ref, g3_ref, af_ref, ab_ref, b_ref, cf_ref, cb_ref, df_ref, db_ref,
                  za_ref, zc_ref, zd_ref, gain_ref, wb_ref, wo_ref, g_ref, o_ref):
    gain = gain_ref[...]
    branches = (
        _head_post(af_ref[...] + ab_ref[...], za_ref[...].astype(F32), gain[0:1], False, True),
        b_ref[...],
        _head_post(cf_ref[...] + cb_ref[...], zc_ref[...].astype(F32), gain[1:2], False, False),
        _head_post(df_ref[...] + db_ref[...], zd_ref[...].astype(F32), gain[2:3], True, True),
    )
    s = None
    for n, (br, gate_ref) in enumerate(zip(branches, (g0_ref, g1_ref, g2_ref, g3_ref))):
        proj = _dot(br, wb_ref[n])
        gate = jax.nn.sigmoid(gate_ref[...].astype(F32))
        s = gate * proj if s is None else s + gate * proj
    m = _dot(s.astype(BF16), wo_ref[...])
    o_ref[...] = x_ref[...] + g_ref[0:1] * m


def merge_branches(x, zmain, gdn, attn, mlstm, ret, gains, w_branch, w_out, gvec):
    m = x.shape[0]
    tm = min(m, 512)
    row = lambda w: pl.BlockSpec((tm, w), lambda i: (i, 0))
    zcol = lambda off: pl.BlockSpec((tm, BRANCH_W), lambda i: (i, off // BRANCH_W))
    return pl.pallas_call(
        _merge_kernel,
        out_shape=jax.ShapeDtypeStruct((m, D_MODEL), F32),
        grid=(m // tm,),
        in_specs=[row(D_MODEL)]
                 + [pl.BlockSpec((tm, D_MODEL), lambda i, n=n: (i, OFF_GATE // D_MODEL + n)) for n in range(N_BRANCH)]
                 + [row(BRANCH_W)] * 7
                 + [zcol(OFF_GDN_Z), zcol(OFF_MLSTM_O), zcol(OFF_RET_G),
                    pl.BlockSpec((8, BRANCH_W), lambda i: (0, 0)),
                    pl.BlockSpec((N_BRANCH, BRANCH_W, D_MODEL), lambda i: (0, 0, 0)),
                    pl.BlockSpec((D_MODEL, D_MODEL), lambda i: (0, 0)),
                    pl.BlockSpec((8, D_MODEL), lambda i: (0, 0))],
        out_specs=row(D_MODEL),
        compiler_params=pltpu.CompilerParams(dimension_semantics=("parallel",), vmem_limit_bytes=MERGE_VMEM_LIMIT),
        name="merge_branches",
    )(x, zmain, zmain, zmain, zmain, gdn[0], gdn[1], attn, mlstm[0], mlstm[1], ret[0], ret[1],
      zmain, zmain, zmain, gains, w_branch, w_out, gvec)


def _ffn_kernel(x_ref, pv_ref, wg_ref, wu_ref, wd_ref, o_ref, h_ref, acc_ref):
    f = pl.program_id(1)

    @pl.when(f == 0)
    def _():
        pv = pv_ref[...]
        h_ref[...] = (_rms(x_ref[...]) * pv[0:1] * pv[1:2] + pv[2:3]).astype(BF16)
        acc_ref[...] = jnp.zeros_like(acc_ref)

    h = h_ref[...]
    g = _dot(h, wg_ref[...])
    u = _dot(h, wu_ref[...])
    act = (g * jax.nn.sigmoid(g) * u).astype(BF16)
    acc_ref[...] += _dot(act, wd_ref[...])

    @pl.when(f == pl.num_programs(1) - 1)
    def _():
        o_ref[...] = x_ref[...] + pv_ref[3:4] * acc_ref[...]


def dense_ffn(x, pv, w_gu, w_down):
    m = x.shape[0]
    tm = min(m, 1024)
    tf = 512
    nf = D_FF // tf
    return pl.pallas_call(
        _ffn_kernel,
        out_shape=jax.ShapeDtypeStruct((m, D_MODEL), F32),
        grid=(m // tm, nf),
        in_specs=[pl.BlockSpec((tm, D_MODEL), lambda i, f: (i, 0)),
                  pl.BlockSpec((8, D_MODEL), lambda i, f: (0, 0)),
                  pl.BlockSpec((D_MODEL, tf), lambda i, f: (0, f)),
                  pl.BlockSpec((D_MODEL, tf), lambda i, f: (0, f + nf)),
                  pl.BlockSpec((tf, D_MODEL), lambda i, f: (f, 0))],
        out_specs=pl.BlockSpec((tm, D_MODEL), lambda i, f: (i, 0)),
        scratch_shapes=[pltpu.VMEM((tm, D_MODEL), BF16), pltpu.VMEM((tm, D_MODEL), F32)],
        compiler_params=_cparams(("parallel", "arbitrary")),
        name="dense_ffn",
    )(x, pv, w_gu, w_gu, w_down)


def _router_kernel(x_ref, pv_ref, w_ref, b_ref, h_ref, logit_ref):
    pv = pv_ref[...]
    h = _rms(x_ref[...]) * pv[0:1] * pv[1:2] + pv[2:3]
    h_hi = h.astype(BF16)
    h_lo = (h - h_hi.astype(F32)).astype(BF16)
    w = w_ref[...]
    w_hi = w.astype(BF16)
    w_lo = (w - w_hi.astype(F32)).astype(BF16)
    logits = _dot(h_hi, w_hi) + _dot(h_hi, w_lo) + _dot(h_lo, w_hi)
    h_ref[...] = h
    logit_ref[...] = logits + b_ref[...]


def moe_router(x, pv, w_router_pad, b_router_pad):
    m = x.shape[0]
    tm = min(m, 1024)
    return pl.pallas_call(
        _router_kernel,
        out_shape=(jax.ShapeDtypeStruct((m, D_MODEL), F32), jax.ShapeDtypeStruct((m, 128), F32)),
        grid=(m // tm,),
        in_specs=[pl.BlockSpec((tm, D_MODEL), lambda i: (i, 0)),
                  pl.BlockSpec((8, D_MODEL), lambda i: (0, 0)),
                  pl.BlockSpec((D_MODEL, 128), lambda i: (0, 0)),
                  pl.BlockSpec((1, 128), lambda i: (0, 0))],
        out_specs=(pl.BlockSpec((tm, D_MODEL), lambda i: (i, 0)), pl.BlockSpec((tm, 128), lambda i: (i, 0))),
        compiler_params=_cparams(("parallel",)),
        name="moe_router",
    )(x, pv, w_router_pad, b_router_pad)


def _expert_kernel(be_ref, nb_ref, x_ref, wg_ref, wu_ref, wd_ref, o_ref, acc_ref):
    b = pl.program_id(0)
    f = pl.program_id(1)

    @pl.when(b < nb_ref[0])
    def _():
        @pl.when(f == 0)
        def _():
            acc_ref[...] = jnp.zeros_like(acc_ref)

        x = x_ref[...].astype(BF16)
        g = _dot(x, wg_ref[0].astype(BF16))
        u = _dot(x, wu_ref[0].astype(BF16))
        act = (g * jax.nn.sigmoid(g) * u).astype(BF16)
        acc_ref[...] += _dot(act, wd_ref[0].astype(BF16))

        @pl.when(f == pl.num_programs(1) - 1)
        def _():
            o_ref[...] = acc_ref[...]

    @pl.when((b >= nb_ref[0]) & (f == pl.num_programs(1) - 1))
    def _():
        o_ref[...] = jnp.zeros_like(o_ref)


def expert_ffn(block_e, n_used, xb, w_gu, w_down):
    cap = xb.shape[0]
    nb = cap // MOE_ROWS
    tf = 512
    nf = D_FF // tf

    def live(b, nbr):
        return jnp.minimum(b, nbr[0] - 1)

    def fsel(b, f, nbr):
        return jnp.where(b < nbr[0], f, nf - 1)

    grid_spec = pltpu.PrefetchScalarGridSpec(
        num_scalar_prefetch=2,
        grid=(nb, nf),
        in_specs=[pl.BlockSpec((MOE_ROWS, D_MODEL), lambda b, f, be, nbr: (live(b, nbr), 0)),
                  pl.BlockSpec((1, D_MODEL, tf), lambda b, f, be, nbr: (be[live(b, nbr)], 0, fsel(b, f, nbr))),
                  pl.BlockSpec((1, D_MODEL, tf), lambda b, f, be, nbr: (be[live(b, nbr)], 0, fsel(b, f, nbr) + nf)),
                  pl.BlockSpec((1, tf, D_MODEL), lambda b, f, be, nbr: (be[live(b, nbr)], fsel(b, f, nbr), 0))],
        out_specs=pl.BlockSpec((MOE_ROWS, D_MODEL), lambda b, f, be, nbr: (b, 0)),
        scratch_shapes=[pltpu.VMEM((MOE_ROWS, D_MODEL), F32)],
    )
    return pl.pallas_call(
        _expert_kernel,
        out_shape=jax.ShapeDtypeStruct((cap, D_MODEL), F32),
        grid_spec=grid_spec,
        compiler_params=_cparams(("arbitrary", "arbitrary")),
        name="expert_ffn",
    )(block_e, n_used, xb, w_gu, w_gu, w_down)


def _final_kernel(x_ref, y0_ref, y1_ref, w_ref, pv_ref, o_ref):
    w = w_ref[...]
    f = w[:, 0:1] * y0_ref[...] + w[:, 1:2] * y1_ref[...]
    x = x_ref[...] + pv_ref[0:1] * f
    o_ref[...] = _rms(x) * pv_ref[1:2]


def final_combine(x, y2, w, pv):
    m = x.shape[0]
    tm = min(m, 1024)
    row = lambda wd: pl.BlockSpec((tm, wd), lambda i: (i, 0))
    return pl.pallas_call(
        _final_kernel,
        out_shape=jax.ShapeDtypeStruct((m, D_MODEL), F32),
        grid=(m // tm,),
        in_specs=[row(D_MODEL), row(D_MODEL), pl.BlockSpec((tm, D_MODEL), lambda i: (i, 1)), row(128),
                  pl.BlockSpec((8, D_MODEL), lambda i: (0, 0))],
        out_specs=row(D_MODEL),
        compiler_params=_cparams(("parallel",)),
        name="final_combine",
    )(x, y2, y2, w, pv)


CH = 128
NEG = -1e30


def _masks(reverse):
    r = lax.broadcasted_iota(jnp.int32, (CH, CH), 0)
    c = lax.broadcasted_iota(jnp.int32, (CH, CH), 1)
    return (r <= c, r < c) if reverse else (r >= c, r > c)


def _cumsum_time(incl, x):
    m = jnp.where(incl, 1.0, 0.0).astype(BF16)
    hi = x.astype(BF16)
    lo = (x - hi.astype(F32)).astype(BF16)
    return _dot(m, hi) + _dot(m, lo)


SCAN_SUB = 4


def _scan_grid(l):
    rows = min(l, SCAN_SUB * CH)
    return l // rows, rows


def _sweep(d, nsub):
    return list(range(nsub)) if d == 0 else list(range(nsub - 1, -1, -1))


def _dir_specs(nsteps, rows, width, col_block):
    return [pl.BlockSpec((rows, width), lambda n: (n, col_block)),
            pl.BlockSpec((rows, width), lambda n: (nsteps - 1 - n, col_block))]


def _whole(a):
    return pl.BlockSpec(a.shape, lambda n: tuple(0 for _ in a.shape))


def _ret_consts():
    log_gamma = np.log(1.0 - 2.0 ** (-5.0 - np.arange(N_HEADS, dtype=np.float64)))
    pos = np.arange(CH, dtype=np.float64)
    diff = pos[:, None] - pos[None, :]
    dec_f = np.where(diff >= 0, np.exp(log_gamma[:, None, None] * diff), 0.0)
    dec = np.stack([dec_f, np.transpose(dec_f, (0, 2, 1))])
    qs_f = np.exp(log_gamma[None, :] * (pos[:, None] + 1.0))
    ks_f = np.exp(log_gamma[None, :] * (CH - 1.0 - pos[:, None]))
    qs_b = np.exp(log_gamma[None, :] * (CH - pos[:, None]))
    ks_b = np.exp(log_gamma[None, :] * pos[:, None])
    rep = lambda a: np.repeat(a, RET_DK, axis=1)
    qs = np.stack([rep(qs_f), rep(qs_b)])
    ks = np.stack([rep(ks_f), rep(ks_b)])
    chunk_decay = [float(np.exp(lg * CH)) for lg in log_gamma]
    return (jnp.asarray(dec, F32), jnp.asarray(qs, F32), jnp.asarray(ks, F32)), chunk_decay


def _ret_kernel(*refs, rotary, chunk_decay):
    if rotary:
        (qf, qb, kf, kb, vf, vb, cosf, cosb, sinf, sinb, dec_ref, qs_ref, ks_ref, s0_ref,
         of_ref, ob_ref, s_ref) = refs
        tabs = ((cosf, sinf), (cosb, sinb))
    else:
        qf, qb, kf, kb, vf, vb, dec_ref, qs_ref, ks_ref, s0_ref, of_ref, ob_ref, s_ref = refs
        tabs = (None, None)

    @pl.when(pl.program_id(0) == 0)
    def _():
        s_ref[...] = s0_ref[...]

    lane = lax.broadcasted_iota(jnp.int32, (CH, N_HEADS * RET_DK), 1)
    first_half = (lane & (RET_DK - 1)) < RET_DK // 2

    nsub = qf.shape[0] // CH
    chains = {}
    for d, (q_ref, k_ref, v_ref, o_ref) in enumerate(((qf, kf, vf, of_ref), (qb, kb, vb, ob_ref))):
        for j in range(nsub):
            rows = slice(j * CH, (j + 1) * CH)
            q = q_ref[rows, :].astype(F32)
            k = k_ref[rows, :].astype(F32) * RET_DK ** -0.5
            if rotary:
                cos, sin = tabs[d][0][rows, :], tabs[d][1][rows, :]

                def rot(x, cos=cos, sin=sin):
                    swapped = jnp.where(first_half, pltpu.roll(x, N_HEADS * RET_DK - RET_DK // 2, 1),
                                        pltpu.roll(x, RET_DK // 2, 1))
                    return x * cos + swapped * sin

                q, k = rot(q), rot(k)
            qb16, kb16 = q.astype(BF16), k.astype(BF16)
            q_in = (q * qs_ref[d]).astype(BF16)
            k_out = k * ks_ref[d]
            k_t = [k_out[:, :128].T, k_out[:, 128:].T]
            for h in range(N_HEADS):
                sl = slice(h * RET_DK, (h + 1) * RET_DK)
                chains[d, j, h] = dict(
                    o_ref=o_ref, rows=rows, vh=v_ref[rows, h * HEAD_V:(h + 1) * HEAD_V], q=qb16[:, sl], k=kb16[:, sl],
                    q_in=q_in[:, sl], k_th=k_t[h // 2][(h % 2) * RET_DK:(h % 2 + 1) * RET_DK, :].astype(BF16))

    for (d, j, h), c in chains.items():
        c["p"] = (_dot_nt(c["q"], c["k"]) * dec_ref[d, h]).astype(BF16)
    for c in chains.values():
        c["intra"] = _dot(c["p"], c["vh"])
        c["update"] = _dot(c["k_th"], c["vh"])
    states = {(d, h): s_ref[d, h] for d in range(2) for h in range(N_HEADS)}
    for i in range(nsub):
        for (d, h), s in list(states.items()):
            c = chains[d, _sweep(d, nsub)[i], h]
            c["o_ref"][c["rows"], h * HEAD_V:(h + 1) * HEAD_V] = c["intra"] + _dot(c["q_in"], s.astype(BF16))
            states[d, h] = chunk_decay[h] * s + c["update"]
    for (d, h), s in states.items():
        s_ref[d, h] = s


def retention_scan(z, s0, tables):
    l = z.shape[0]
    nsteps, rows = _scan_grid(l)
    consts, chunk_decay = _ret_consts()
    rotary = tables is not None
    w = N_HEADS * RET_DK
    specs = (_dir_specs(nsteps, rows, w, OFF_RET_Q // w) + _dir_specs(nsteps, rows, w, OFF_RET_K // w)
             + _dir_specs(nsteps, rows, BRANCH_W, OFF_RET_V // BRANCH_W))
    args = [z] * 6
    if rotary:
        specs += _dir_specs(nsteps, rows, w, 0) + _dir_specs(nsteps, rows, w, 0)
        args += [tables[0], tables[0], tables[1], tables[1]]
    specs += [_whole(c) for c in consts] + [_whole(s0)]
    args += list(consts) + [s0]
    return pl.pallas_call(
        functools.partial(_ret_kernel, rotary=rotary, chunk_decay=chunk_decay),
        out_shape=(jax.ShapeDtypeStruct((l, BRANCH_W), F32), jax.ShapeDtypeStruct((l, BRANCH_W), F32),
                   jax.ShapeDtypeStruct(s0.shape, F32)),
        grid=(nsteps,),
        in_specs=specs,
        out_specs=tuple(_dir_specs(nsteps, rows, BRANCH_W, 0)) + (_whole(s0),),
        compiler_params=_cparams(("arbitrary",)),
        name="retention_scan",
    )(*args)


def _mlstm_kernel(qf, qb, kf, kb, vf, vb, smf, smb, bias_ref, c0_ref, m0_ref, of_ref, ob_ref, c_ref, m_ref):
    @pl.when(pl.program_id(0) == 0)
    def _():
        c_ref[...] = c0_ref[...]
        m_ref[...] = m0_ref[...]

    lane = lax.broadcasted_iota(jnp.int32, (CH, 128), 1)
    is_forget = (lane >= F_LANE) & (lane < F_LANE + 2 * N_HEADS)
    ones_col = jnp.where(lane == 0, 1.0, 0.0).astype(BF16)

    nsub = qf.shape[0] // CH
    chains = {}
    for d, (q_ref, k_ref, v_ref, sm_ref, o_ref) in enumerate(((qf, kf, vf, smf, of_ref), (qb, kb, vb, smb, ob_ref))):
        reverse = d == 1
        incl, _ = _masks(reverse)
        last = 0 if reverse else CH - 1
        for j in range(nsub):
            rows = slice(j * CH, (j + 1) * CH)
            pre = sm_ref[rows, :] + bias_ref[0:1]
            x = jnp.where(is_forget, jax.nn.log_sigmoid(pre), pre)
            b = _cumsum_time(incl, x)
            x_t, b_t = x.T, b.T
            q = (q_ref[rows, :].astype(F32) * MLSTM_DK ** -0.5).astype(BF16)
            kf32 = k_ref[rows, :].astype(F32)
            k_t = [kf32[:, :128].T, kf32[:, 128:].T]
            for h in range(N_HEADS):
                gi, gf = I_LANE + d * N_HEADS + h, F_LANE + d * N_HEADS + h
                sl = slice(h * MLSTM_DK, (h + 1) * MLSTM_DK)
                b_col, b_row = b[:, gf:gf + 1], b_t[gf:gf + 1, :]
                li_row = x_t[gi:gi + 1, :]
                b_last = b[last:last + 1, gf:gf + 1]
                d_log = jnp.where(incl, b_col - b_row + li_row, NEG)
                m_intra = jnp.max(d_log, axis=1, keepdims=True)
                e_row = b_last - b_row + li_row
                chains[d, j, h] = dict(
                    o_ref=o_ref, rows=rows, q=q[:, sl], k=k_ref[rows, sl], b_col=b_col, b_last=b_last, m_intra=m_intra,
                    e_row=e_row, m_end=jnp.max(e_row, axis=1, keepdims=True),
                    w_intra=jnp.exp(d_log - m_intra),
                    k_th=k_t[h // 2][(h % 2) * MLSTM_DK:(h % 2 + 1) * MLSTM_DK, :],
                    v_aug=jnp.concatenate([v_ref[rows, h * HEAD_V:(h + 1) * HEAD_V], ones_col], axis=1))

    for c in chains.values():
        c["p"] = (c["w_intra"] * _dot_nt(c["q"], c["k"])).astype(BF16)
    for c in chains.values():
        c["pv"] = _dot(c["p"], c["v_aug"])
    c_augs = {(d, h): c_ref[d, h] for d in range(2) for h in range(N_HEADS)}
    m_ss = {(d, h): m_ref[d, h][0:1, 0:1] for d in range(2) for h in range(N_HEADS)}
    for i in range(nsub):
        cur = {(d, h): chains[d, _sweep(d, nsub)[i], h] for (d, h) in c_augs}
        qcs = {key: _dot(c["q"], c_augs[key].astype(BF16)) for key, c in cur.items()}
        m_news = {key: jnp.maximum(c["b_last"] + m_ss[key], c["m_end"]) for key, c in cur.items()}
        updates = {key: _dot((c["k_th"] * jnp.exp(c["e_row"] - m_news[key])).astype(BF16), c["v_aug"])
                   for key, c in cur.items()}
        for (d, h), c in cur.items():
            m_s = m_ss[d, h]
            m_t = jnp.maximum(c["b_col"] + m_s, c["m_intra"])
            inter = jnp.exp(c["b_col"] + m_s - m_t)
            r = jnp.exp(c["m_intra"] - m_t)
            tot = inter * qcs[d, h] + r * c["pv"]
            den = jnp.maximum(jnp.abs(tot[:, HEAD_V:HEAD_V + 1]), jnp.exp(-m_t))
            c["o_ref"][c["rows"], h * HEAD_V:(h + 1) * HEAD_V] = tot[:, :HEAD_V] / den
        for key, c in cur.items():
            c_augs[key] = jnp.exp(c["b_last"] + m_ss[key] - m_news[key]) * c_augs[key] + updates[key]
            m_ss[key] = m_news[key]
    for (d, h) in c_augs:
        c_ref[d, h] = c_augs[d, h]
        m_ref[d, h] = jnp.broadcast_to(m_ss[d, h], (8, 128))


def mlstm_scan(z, small, bias, c0, m0):
    l = z.shape[0]
    nsteps, rows = _scan_grid(l)
    w = N_HEADS * MLSTM_DK
    specs = (_dir_specs(nsteps, rows, w, OFF_MLSTM_Q // w) + _dir_specs(nsteps, rows, w, OFF_MLSTM_K // w)
             + _dir_specs(nsteps, rows, BRANCH_W, OFF_MLSTM_V // BRANCH_W) + _dir_specs(nsteps, rows, 128, 0))
    specs += [_whole(bias), _whole(c0), _whole(m0)]
    return pl.pallas_call(
        _mlstm_kernel,
        out_shape=(jax.ShapeDtypeStruct((l, BRANCH_W), F32), jax.ShapeDtypeStruct((l, BRANCH_W), F32),
                   jax.ShapeDtypeStruct(c0.shape, F32), jax.ShapeDtypeStruct(m0.shape, F32)),
        grid=(nsteps,),
        in_specs=specs,
        out_specs=tuple(_dir_specs(nsteps, rows, BRANCH_W, 0)) + (_whole(c0), _whole(m0)),
        compiler_params=_cparams(("arbitrary",)),
        name="mlstm_scan",
    )(z, z, z, z, z, z, small, small, bias, c0, m0)


QKV_W = 3 * N_HEADS * GDN_DK
HALO = 8


def _gdn_prep_kernel(x_ref, prev_ref, next_ref, w_ref, q_ref, k_ref, v_ref):
    i = pl.program_id(0)
    tm = x_ref.shape[0]
    x = x_ref[...].astype(F32)
    prev = jnp.where(i > 0, prev_ref[...].astype(F32), 0.0)
    nxt = jnp.where(i < pl.num_programs(0) - 1, next_ref[...].astype(F32), 0.0)
    xe = jnp.concatenate([prev, x, nxt], axis=0)
    w = w_ref[...]
    y = None
    for tap in range(CONV_K):
        off = HALO + tap - CONV_K // 2
        term = w[tap:tap + 1] * xe[off:off + tm]
        y = term if y is None else y + term
    y = y * jax.nn.sigmoid(y)
    hw = N_HEADS * GDN_DK
    for h in range(N_HEADS):
        sl = slice(h * GDN_DK, (h + 1) * GDN_DK)
        qh = y[:, sl]
        kh = y[:, hw + h * GDN_DK:hw + (h + 1) * GDN_DK]
        q_ref[:, sl] = (qh * lax.rsqrt(jnp.sum(qh * qh, axis=-1, keepdims=True) + EPS) * GDN_DK ** -0.5).astype(BF16)
        k_ref[:, sl] = (kh * lax.rsqrt(jnp.sum(kh * kh, axis=-1, keepdims=True) + EPS)).astype(BF16)
    v_ref[...] = y[:, 2 * hw:].astype(BF16)


def gdn_prep(z, conv_w):
    l = z.shape[0]
    tm = min(l, 256)
    nb = l // tm
    r8 = tm // HALO
    cb = OFF_GDN_QKV // QKV_W
    w8 = jnp.concatenate([conv_w.astype(F32), jnp.zeros((8 - CONV_K, QKV_W), F32)], axis=0)
    return pl.pallas_call(
        _gdn_prep_kernel,
        out_shape=tuple(jax.ShapeDtypeStruct((l, BRANCH_W), BF16) for _ in range(3)),
        grid=(nb,),
        in_specs=[pl.BlockSpec((tm, QKV_W), lambda i: (i, cb)),
                  pl.BlockSpec((HALO, QKV_W), lambda i: (jnp.maximum(i * r8 - 1, 0), cb)),
                  pl.BlockSpec((HALO, QKV_W), lambda i: (jnp.minimum((i + 1) * r8, nb * r8 - 1), cb)),
                  pl.BlockSpec((8, QKV_W), lambda i: (0, 0))],
        out_specs=tuple(pl.BlockSpec((tm, BRANCH_W), lambda i: (i, 0)) for _ in range(3)),
        compiler_params=_cparams(("parallel",)),
        name="gdn_prep",
    )(z, z, z, w8)


N_LEVELS = 7


def _gdn_kernel(qf, qb, kf, kb, vf, vb, smf, smb, par_ref, s0_ref, of_ref, ob_ref, s_ref):
    @pl.when(pl.program_id(0) == 0)
    def _():
        s_ref[...] = s0_ref[...]

    lane = lax.broadcasted_iota(jnp.int32, (CH, 128), 1)
    is_decay = lane < B_LANE
    ri = lax.broadcasted_iota(jnp.int32, (CH, CH), 0)
    ci = lax.broadcasted_iota(jnp.int32, (CH, CH), 1)
    eye = jnp.where(ri == ci, 1.0, 0.0)
    pair_masks = [((ri >> (l + 1)) == (ci >> (l + 1))) & ((ri >> l) != (ci >> l)) for l in range(N_LEVELS)]

    nsub = qf.shape[0] // CH
    chains = {}
    for d, (q_ref, k_ref, v_ref, sm_ref, o_ref) in enumerate(((qf, kf, vf, smf, of_ref), (qb, kb, vb, smb, ob_ref))):
        reverse = d == 1
        incl, strict = _masks(reverse)
        last = 0 if reverse else CH - 1
        for j in range(nsub):
            rows = slice(j * CH, (j + 1) * CH)
            sm = sm_ref[rows, :]
            log_a = -jnp.exp(par_ref[1:2]) * jax.nn.softplus(sm + par_ref[0:1])
            x = jnp.where(is_decay, log_a, jax.nn.sigmoid(sm))
            g = _cumsum_time(incl, x)
            g_t = g.T
            for h in range(N_HEADS):
                ga, gb = A_LANE + d * N_HEADS + h, B_LANE + d * N_HEADS + h
                sl = slice(h * GDN_DK, (h + 1) * GDN_DK)
                g_col, g_row = g[:, ga:ga + 1], g_t[ga:ga + 1, :]
                g_last = g[last:last + 1, ga:ga + 1]
                beta = x[:, gb:gb + 1]
                decay = jnp.where(incl, jnp.exp(jnp.minimum(g_col - g_row, 0.0)), 0.0)
                kh = k_ref[rows, sl]
                kf32 = kh.astype(F32)
                kbeta = kf32 * beta
                a = jnp.where(strict, _dot_nt(kbeta.astype(BF16), kh) * decay, 0.0)
                e_g = jnp.exp(g_col)
                rhs = jnp.concatenate([v_ref[rows, sl].astype(F32) * beta, kbeta * e_g], axis=1)
                chains[d, j, h] = dict(
                    rows=rows, sl=sl, o_ref=o_ref, a=a, rhs=rhs, s_decay=jnp.exp(g_last),
                    qk=(_dot_nt(q_ref[rows, sl], kh) * decay).astype(BF16),
                    q_in=(q_ref[rows, sl].astype(F32) * e_g).astype(BF16),
                    k_out_t=(kf32.T * jnp.exp(g_last - g_row)).astype(BF16))

    cl = list(chains.values())
    xs = [eye - jnp.where(pair_masks[0], c["a"], 0.0) for c in cl]
    for pm in pair_masks[1:]:
        ys = [_dot(jnp.where(pm, c["a"], 0.0).astype(BF16), x.astype(BF16)) for c, x in zip(cl, xs)]
        xs = [x - _dot(x.astype(BF16), y.astype(BF16)) for x, y in zip(xs, ys)]
    for c, x in zip(cl, xs):
        c["sol"] = c["rhs"] + _dot((x - eye).astype(BF16), c["rhs"].astype(BF16))
    states = {(d, h): s_ref[d, h] for d in range(2) for h in range(N_HEADS)}
    for i in range(nsub):
        cur = {(d, h): chains[d, _sweep(d, nsub)[i], h] for (d, h) in states}
        s16s = {key: s.astype(BF16) for key, s in states.items()}
        v_news = {key: (c["sol"][:, :HEAD_V] - _dot(c["sol"][:, HEAD_V:].astype(BF16), s16s[key])).astype(BF16)
                  for key, c in cur.items()}
        updates = {key: _dot(c["k_out_t"], v_news[key]) for key, c in cur.items()}
        for key, c in cur.items():
            c["o_ref"][c["rows"], c["sl"]] = _dot(c["q_in"], s16s[key]) + _dot(c["qk"], v_news[key])
        for key, c in cur.items():
            states[key] = states[key] * c["s_decay"] + updates[key]
    for (d, h), s in states.items():
        s_ref[d, h] = s


def gdn_scan(qn, kn, vn, small, par, s0):
    l = qn.shape[0]
    nsteps, rows = _scan_grid(l)
    specs = _dir_specs(nsteps, rows, BRANCH_W, 0) * 3 + _dir_specs(nsteps, rows, 128, 0)
    specs += [_whole(par), _whole(s0)]
    return pl.pallas_call(
        _gdn_kernel,
        out_shape=(jax.ShapeDtypeStruct((l, BRANCH_W), F32), jax.ShapeDtypeStruct((l, BRANCH_W), F32),
                   jax.ShapeDtypeStruct(s0.shape, F32)),
        grid=(nsteps,),
        in_specs=specs,
        out_specs=tuple(_dir_specs(nsteps, rows, BRANCH_W, 0)) + (_whole(s0),),
        compiler_params=_cparams(("arbitrary",)),
        name="gdn_scan",
    )(qn, qn, kn, kn, vn, vn, small, small, par, s0)


def _lane_rows(entries):
    r = jnp.zeros((8, 128), F32)
    for row, lane, vals in entries:
        r = r.at[row, lane:lane + vals.shape[0]].set(vals.astype(F32))
    return r


def gdn_branch(zc, sc, zl, sl, conv_w, a_log, dt_bias):
    par = _lane_rows([(0, A_LANE, dt_bias.reshape(-1)), (1, A_LANE, a_log.reshape(-1))])
    s0 = jnp.zeros((2, N_HEADS, GDN_DK, HEAD_V), F32)
    ofc, obc, s1 = gdn_scan(*gdn_prep(zc, conv_w), sc, par, s0)
    ofl, obl, _ = gdn_scan(*gdn_prep(zl, conv_w), sl, par, s1)
    return (ofc, obc), (ofl, obl)


def mlstm_branch(zc, sc, zl, sl, gate_b):
    bias = _lane_rows([(0, I_LANE, gate_b[0].reshape(-1)), (0, F_LANE, gate_b[1].reshape(-1))])
    c0 = jnp.zeros((2, N_HEADS, MLSTM_DK, 2 * HEAD_V), F32)
    m0 = jnp.zeros((2, N_HEADS, 8, 128), F32)
    ofc, obc, c1, m1 = mlstm_scan(zc, sc, bias, c0, m0)
    ofl, obl, _, _ = mlstm_scan(zl, sl, bias, c1, m1)
    return (ofc, obc), (ofl, obl)


def retention_branch(zc, zl, tables):
    s0 = jnp.zeros((2, N_HEADS, RET_DK, HEAD_V), F32)
    ofc, obc, s1 = retention_scan(zc, s0, None)
    ofl, obl, _ = retention_scan(zl, s1, tables)
    return (ofc, obc), (ofl, obl)


def _ret_rope_tables(n_lat):
    inv = ROPE_BASE ** (-jnp.arange(0, RET_DK, 2, dtype=F32) / RET_DK)
    ang = jnp.arange(n_lat, dtype=F32)[:, None] * inv[None, :]
    cos = jnp.concatenate([jnp.cos(ang), jnp.cos(ang)], axis=-1)
    sin = jnp.concatenate([-jnp.sin(ang), jnp.sin(ang)], axis=-1)
    return jnp.tile(cos, (1, N_HEADS)), jnp.tile(sin, (1, N_HEADS))


def _rope_tables(n_lat):
    q = MLA_ROPE // 4
    inv = ROPE_BASE ** (-jnp.arange(0, 2 * q, 2, dtype=F32) / (2 * q))
    t = jnp.arange(n_lat)
    row = (t // GRID_W).astype(F32)[:, None] * inv[None, :]
    col = (t % GRID_W).astype(F32)[:, None] * inv[None, :]
    cos = jnp.concatenate([jnp.cos(row), jnp.cos(row), jnp.cos(col), jnp.cos(col)], axis=-1)
    sin = jnp.concatenate([-jnp.sin(row), jnp.sin(row), -jnp.sin(col), jnp.sin(col)], axis=-1)
    return cos, sin


def _take_cols(w, perm):
    runs, start = [], 0
    for i in range(1, len(perm) + 1):
        if i == len(perm) or perm[i] != perm[i - 1] + 1:
            runs.append((int(perm[start]), int(perm[i - 1]) + 1))
            start = i
    return jnp.concatenate([w[:, a:b] for a, b in runs], axis=1)


def _pad_rows(v, n=8):
    rows = [jnp.reshape(r, (1, -1)).astype(F32) for r in v]
    d = rows[0].shape[1]
    return jnp.concatenate(rows + [jnp.zeros((n - len(rows), d), F32)], axis=0)


def _mla_weights(w_uq, w_ukv):
    wq = w_uq.reshape(MLA_Q_RANK, N_HEADS, MLA_QK)
    rope = wq[:, :, MLA_NOPE:]
    swapped = rope[:, :, _rope_swap(np.arange(MLA_ROPE))]
    wq_ext = jnp.concatenate([wq[:, :, :MLA_NOPE], rope, swapped], axis=-1).reshape(MLA_Q_RANK, N_HEADS * 256)
    return wq_ext.astype(BF16), w_ukv.astype(BF16)


def _moe(x_l, pv2, g2, w_router, b_router, w_gu, w_down, final_g):
    n = x_l.shape[0]
    w_pad = jnp.concatenate([w_router, jnp.zeros((D_MODEL, 128 - N_EXPERTS), F32)], axis=1)
    b_pad = jnp.concatenate([b_router, jnp.full((128 - N_EXPERTS,), -1e30, F32)]).reshape(1, 128)
    h2, logits = moe_router(x_l, pv2, w_pad, b_pad)
    top_logit, top_e = lax.top_k(logits[:, :N_EXPERTS], TOP_K)
    weight = jax.nn.softmax(top_logit, axis=-1)
    n_assign = n * TOP_K
    flat_e = top_e.reshape(-1)
    onehot = (flat_e[:, None] == jnp.arange(N_EXPERTS)[None, :]).astype(jnp.int32)
    csum = jnp.cumsum(onehot, axis=0)
    counts = csum[-1]
    rank = jnp.take_along_axis(csum, flat_e[:, None], axis=1)[:, 0] - 1
    padded = (counts + MOE_ROWS - 1) // MOE_ROWS * MOE_ROWS
    pad_end = jnp.cumsum(padded)
    dest = (pad_end - padded)[flat_e] + rank
    nb = n_assign // MOE_ROWS + N_EXPERTS
    cap = nb * MOE_ROWS
    block_e = jnp.minimum(jnp.searchsorted(pad_end, jnp.arange(nb) * MOE_ROWS, side='right'),
                          N_EXPERTS - 1).astype(jnp.int32)
    n_used = (pad_end[-1] // MOE_ROWS).astype(jnp.int32).reshape(1)
    slot_tok = (jnp.arange(cap, dtype=jnp.int32) % n).at[dest].set(jnp.arange(n_assign, dtype=jnp.int32) // TOP_K)
    yb = expert_ffn(block_e, n_used, h2[slot_tok], w_gu, w_down)
    w128 = jnp.concatenate([weight, jnp.zeros((n, 126), F32)], axis=1)
    y2 = yb[dest].reshape(n, TOP_K * D_MODEL)
    return final_combine(x_l, y2, w128, _pad_rows([g2, final_g]))


def kernel(x, c, ctx, c_ctx, w_mod, b_mod, norm1_g, norm2_g, w_in, gdn_conv_w, gdn_a_log, gdn_dt_bias, gdn_norm_g, mla_q_norm_g, mla_kv_norm_g, mla_w_uq, mla_w_ukv, mlstm_gate_b, mlstm_norm_g, ret_norm_g, w_branch, w_out, ffn_w_in, ffn_w_down, moe_w_router, moe_b_router, moe_w_in, moe_w_down, final_norm_g):
    n_lat = x.shape[1]
    n_ctx = ctx.shape[1]
    x_l, x_c = x[0], ctx[0]
    cond = _pad_rows([c_ctx, c[0]])
    cos_l, sin_l = _rope_tables(n_lat)
    cos_c, sin_c = jnp.ones((n_ctx, MLA_ROPE), F32), jnp.zeros((n_ctx, MLA_ROPE), F32)
    ret_tables = _ret_rope_tables(n_lat)
    out = None
    for li in range(DEPTH):
        last = li == DEPTH - 1
        mod = modulation_vectors(cond, w_mod[li], b_mod[li])
        csh1, csc1, cg1, csh2, csc2, cg2 = jnp.split(mod[0], 6)
        sh1, sc1, g1, sh2, sc2, g2 = jnp.split(mod[1], 6)
        w_main = _take_cols(w_in[li], _MAIN_PERM).astype(BF16)
        w_small = _take_cols(w_in[li], _SMALL_PERM).astype(BF16)
        w_small = jnp.concatenate([w_small, jnp.zeros((D_MODEL, N_SMALL - w_small.shape[1]), BF16)], axis=1)
        pv_l = _pad_rows([norm1_g[li], 1 + sc1, sh1])
        pv_c = _pad_rows([norm1_g[li], 1 + csc1, csh1])
        zl = norm_proj(x_l, pv_l, w_main, 1664, BF16)
        zc = norm_proj(x_c, pv_c, w_main, 1664, BF16)
        sl = norm_proj(x_l, pv_l, w_small, 128, F32)
        sc = norm_proj(x_c, pv_c, w_small, 128, F32)

        a_c, a_l = gdn_branch(zc, sc, zl, sl, gdn_conv_w[li], gdn_a_log[li], gdn_dt_bias[li])
        c_c, c_l = mlstm_branch(zc, sc, zl, sl, mlstm_gate_b[li])
        d_c, d_l = retention_branch(zc, zl, ret_tables)
        gains = _pad_rows([gdn_norm_g[li], mlstm_norm_g[li], ret_norm_g[li]])

        wq_ext, wkv = _mla_weights(mla_w_uq[li], mla_w_ukv[li])
        gq, gkv = mla_q_norm_g[li].reshape(1, -1), mla_kv_norm_g[li].reshape(1, -1)
        ql, kl, vl = mla_project(zl, cos_l, sin_l, gq, gkv, wq_ext, wkv)
        qc, kc, vc = mla_project(zc, cos_c, sin_c, gq, gkv, wq_ext, wkv)
        b_l = attention(ql, [(kl, vl), (kc, vc)])

        wb = w_branch[li].astype(BF16)
        wo = w_out[li].astype(BF16)
        x_l = merge_branches(x_l, zl, a_l, b_l, c_l, d_l, gains, wb, wo, _pad_rows([g1]))
        if not last:
            b_c = attention(qc, [(kc, vc)])
            x_c = merge_branches(x_c, zc, a_c, b_c, c_c, d_c, gains, wb, wo, _pad_rows([cg1]))

        if li % 2 == 0:
            w_gu = ffn_w_in[li // 2].astype(BF16)
            w_dn = ffn_w_down[li // 2].astype(BF16)
            assert not last
            x_l = dense_ffn(x_l, _pad_rows([norm2_g[li], 1 + sc2, sh2, g2]), w_gu, w_dn)
            x_c = dense_ffn(x_c, _pad_rows([norm2_g[li], 1 + csc2, csh2, cg2]), w_gu, w_dn)
        else:
            assert last
            out = _moe(x_l, _pad_rows([norm2_g[li], 1 + sc2, sh2]), g2, moe_w_router[li // 2],
                       moe_b_router[li // 2], moe_w_in[li // 2], moe_w_down[li // 2], final_norm_g)
    return out[None]
```

```python
import functools
import math

import numpy as np
import jax
import jax.numpy as jnp
from jax import lax
from jax.experimental import pallas as pl
from jax.experimental.pallas import tpu as pltpu

F32 = jnp.float32
BF16 = jnp.bfloat16

D_MODEL = 1024
DEPTH = 2
GRID_W = 64
N_BRANCH = 4
N_HEADS = 4
HEAD_V = 128
BRANCH_W = N_HEADS * HEAD_V
GDN_DK = 128
CONV_K = 5
MLA_Q_RANK = 384
MLA_KV_RANK = 256
MLA_NOPE = 128
MLA_ROPE = 64
MLA_QK = MLA_NOPE + MLA_ROPE
MLA_QK_PAD = 256
MLA_V_PAD = 256
MLSTM_DK = 64
RET_DK = 64
ROPE_BASE = 10000.0
D_FF = 3584
N_EXPERTS = 8
TOP_K = 2
EPS = 1e-6

IN_WIDTHS = (
    N_BRANCH * D_MODEL,
    N_HEADS * GDN_DK, N_HEADS * GDN_DK, BRANCH_W, BRANCH_W, 2 * N_HEADS, 2 * N_HEADS,
    MLA_Q_RANK, MLA_KV_RANK, MLA_ROPE,
    N_HEADS * MLSTM_DK, N_HEADS * MLSTM_DK, BRANCH_W, BRANCH_W, 2 * N_HEADS, 2 * N_HEADS,
    N_HEADS * RET_DK, N_HEADS * RET_DK, BRANCH_W, BRANCH_W,
)
_IN_OFF = [0] + [int(o) for o in np.cumsum(IN_WIDTHS)]

VMEM_LIMIT = 48 * 1024 * 1024
MERGE_VMEM_LIMIT = 56 * 1024 * 1024
MOE_ROWS = 1024


def _cols(group):
    return np.arange(_IN_OFF[group], _IN_OFF[group + 1])


def _rope_swap(cols):
    q = MLA_ROPE // 4
    return np.concatenate([cols[q:2 * q], cols[:q], cols[3 * q:], cols[2 * q:3 * q]])


_MAIN_PERM = np.concatenate([
    _cols(1), _cols(2), _cols(3),
    _cols(4),
    _cols(0),
    _cols(12), _cols(13),
    _cols(10), _cols(11),
    _cols(18), _cols(19),
    _cols(16), _cols(17),
    _cols(7), _cols(8), _cols(9), _rope_swap(_cols(9)),
])
N_MAIN = int(_MAIN_PERM.shape[0])
_SMALL_PERM = np.concatenate([_cols(5), _cols(6), _cols(14), _cols(15)])
N_SMALL = 128
A_LANE, B_LANE, I_LANE, F_LANE = 0, 8, 16, 24
OFF_GDN_QKV, OFF_GDN_Z, OFF_GATE = 0, 1536, 2048
OFF_MLSTM_V, OFF_MLSTM_O, OFF_MLSTM_Q, OFF_MLSTM_K = 6144, 6656, 7168, 7424
OFF_RET_V, OFF_RET_G, OFF_RET_Q, OFF_RET_K = 7680, 8192, 8704, 8960
OFF_MLA = 9216
MLA_IN_W = 768


def _cparams(sem):
    return pltpu.CompilerParams(dimension_semantics=sem, vmem_limit_bytes=VMEM_LIMIT)


def _rms(x):
    return x * lax.rsqrt(jnp.mean(x * x, axis=-1, keepdims=True) + EPS)


def _dot(a, b):
    return jnp.dot(a, b, preferred_element_type=F32)


def _dot_nt(a, b):
    return lax.dot_general(a, b, (((1,), (1,)), ((), ())), preferred_element_type=F32)


def _mod_kernel(c_ref, w_ref, b_ref, o_ref):
    c = c_ref[...]
    s = c * jax.nn.sigmoid(c)
    o_ref[...] = jnp.dot(s, w_ref[...], preferred_element_type=F32) + b_ref[...]


def modulation_vectors(cond, w_mod, b_mod):
    n = w_mod.shape[1]
    tn = 1536
    return pl.pallas_call(
        _mod_kernel,
        out_shape=jax.ShapeDtypeStruct((8, n), F32),
        grid=(n // tn,),
        in_specs=[pl.BlockSpec((8, D_MODEL), lambda j: (0, 0)),
                  pl.BlockSpec((D_MODEL, tn), lambda j: (0, j)),
                  pl.BlockSpec((1, tn), lambda j: (0, j))],
        out_specs=pl.BlockSpec((8, tn), lambda j: (0, j)),
        compiler_params=_cparams(("arbitrary",)),
        name="modulation",
    )(cond, w_mod, b_mod.reshape(1, n))


def _norm_proj_kernel(x_ref, pv_ref, w_ref, o_ref, h_ref):
    @pl.when(pl.program_id(1) == 0)
    def _():
        pv = pv_ref[...]
        h_ref[...] = (_rms(x_ref[...]) * pv[0:1] * pv[1:2] + pv[2:3]).astype(BF16)

    o_ref[...] = _dot(h_ref[...], w_ref[...]).astype(o_ref.dtype)


def norm_proj(x, pv, w, tn, out_dtype):
    m, d = x.shape
    n = w.shape[1]
    tm = min(m, 1024)
    return pl.pallas_call(
        _norm_proj_kernel,
        out_shape=jax.ShapeDtypeStruct((m, n), out_dtype),
        grid=(m // tm, n // tn),
        in_specs=[pl.BlockSpec((tm, d), lambda i, j: (i, 0)),
                  pl.BlockSpec((8, d), lambda i, j: (0, 0)),
                  pl.BlockSpec((d, tn), lambda i, j: (0, j))],
        out_specs=pl.BlockSpec((tm, tn), lambda i, j: (i, j)),
        scratch_shapes=[pltpu.VMEM((tm, d), BF16)],
        compiler_params=_cparams(("parallel", "arbitrary")),
        name="norm_proj",
    )(x, pv, w)


def _mla_proj_kernel(z_ref, cos_ref, sin_ref, gq_ref, gkv_ref, wq_ref, wkv_ref, q_ref, k_ref, v_ref):
    z = z_ref[...].astype(F32)
    tm = z.shape[0]
    cq = z[:, :MLA_Q_RANK]
    ckv = z[:, MLA_Q_RANK:MLA_Q_RANK + MLA_KV_RANK]
    kr = z[:, MLA_Q_RANK + MLA_KV_RANK:]
    cos = cos_ref[...]
    sin = sin_ref[...]
    qn = (_rms(cq) * gq_ref[...]).astype(BF16)
    kvn = (_rms(ckv) * gkv_ref[...]).astype(BF16)
    qf = _dot(qn, wq_ref[...]) * (MLA_QK ** -0.5 * math.log2(math.e))
    kvf = _dot(kvn, wkv_ref[...])
    kr_rot = kr[:, :MLA_ROPE] * cos + kr[:, MLA_ROPE:] * sin
    pad = jnp.zeros((tm, MLA_QK_PAD - MLA_QK), F32)
    lane = lax.broadcasted_iota(jnp.int32, (tm, MLA_V_PAD - HEAD_V), 1)
    ones_col = jnp.where(lane == 0, 1.0, 0.0).astype(BF16)
    for h in range(N_HEADS):
        b = h * 256
        q_rot = qf[:, b + 128:b + 192] * cos + qf[:, b + 192:b + 256] * sin
        q_ref[h] = jnp.concatenate([qf[:, b:b + 128], q_rot, pad], axis=-1).astype(BF16)
        k_ref[h] = jnp.concatenate([kvf[:, b:b + 128], kr_rot, pad], axis=-1).astype(BF16)
        v_ref[h] = jnp.concatenate([kvf[:, b + 128:b + 256].astype(BF16), ones_col], axis=-1)


def mla_project(zmain, cos, sin, gq, gkv, wq_ext, wkv):
    m = zmain.shape[0]
    tm = min(m, 1024)
    full = lambda shape: pl.BlockSpec(shape, lambda i: tuple(0 for _ in shape))
    return pl.pallas_call(
        _mla_proj_kernel,
        out_shape=(jax.ShapeDtypeStruct((N_HEADS, m, MLA_QK_PAD), BF16),
                   jax.ShapeDtypeStruct((N_HEADS, m, MLA_QK_PAD), BF16),
                   jax.ShapeDtypeStruct((N_HEADS, m, MLA_V_PAD), BF16)),
        grid=(m // tm,),
        in_specs=[pl.BlockSpec((tm, MLA_IN_W), lambda i: (i, OFF_MLA // MLA_IN_W)),
                  pl.BlockSpec((tm, MLA_ROPE), lambda i: (i, 0)),
                  pl.BlockSpec((tm, MLA_ROPE), lambda i: (i, 0)),
                  full((1, MLA_Q_RANK)), full((1, MLA_KV_RANK)),
                  full((MLA_Q_RANK, N_HEADS * 256)), full((MLA_KV_RANK, N_HEADS * 256))],
        out_specs=(pl.BlockSpec((N_HEADS, tm, MLA_QK_PAD), lambda i: (0, i, 0)),
                   pl.BlockSpec((N_HEADS, tm, MLA_QK_PAD), lambda i: (0, i, 0)),
                   pl.BlockSpec((N_HEADS, tm, MLA_V_PAD), lambda i: (0, i, 0))),
        compiler_params=_cparams(("parallel",)),
        name="mla_project",
    )(zmain, cos, sin, gq, gkv, wq_ext, wkv)


ATTN_TQ, ATTN_TK = 512, 512
ATTN_UNROLL = 32


def _attn_kernel(*refs, segs):
    q_ref, o_ref = refs[0], refs[-1]
    q = q_ref[0]
    tq = q.shape[0]
    carry = (jnp.full((tq, 1), -1e30, F32), jnp.zeros((tq, MLA_V_PAD), F32))
    for si, (tk, nk) in enumerate(segs):
        k_ref, v_ref = refs[1 + 2 * si], refs[2 + 2 * si]

        def body(c, carry, k_ref=k_ref, v_ref=v_ref, tk=tk):
            m, acc = carry
            start = pl.multiple_of(c * tk, tk)
            s = _dot_nt(q, k_ref[0, pl.ds(start, tk), :])
            m_new = jnp.maximum(m, jnp.max(s, axis=-1, keepdims=True))
            p = jnp.exp2(s - m_new).astype(BF16)
            acc = jnp.exp2(m - m_new) * acc + _dot(p, v_ref[0, pl.ds(start, tk), :])
            return m_new, acc

        carry = lax.fori_loop(0, nk, body, carry, unroll=min(ATTN_UNROLL, nk))
    _, acc = carry
    o_ref[...] = (acc[:, :HEAD_V] / acc[:, HEAD_V:HEAD_V + 1]).astype(o_ref.dtype)


def attention(q, kvs):
    _, lq, _ = q.shape
    tq = min(lq, ATTN_TQ)
    segs, args, specs = [], [], []
    for k, v in kvs:
        lk = k.shape[1]
        tk = min(lk, ATTN_TK)
        segs.append((tk, lk // tk))
        args += [k, v]
        specs += [pl.BlockSpec((1, lk, MLA_QK_PAD), lambda h, i: (h, 0, 0)),
                  pl.BlockSpec((1, lk, MLA_V_PAD), lambda h, i: (h, 0, 0))]
    return pl.pallas_call(
        functools.partial(_attn_kernel, segs=tuple(segs)),
        out_shape=jax.ShapeDtypeStruct((lq, N_HEADS * HEAD_V), BF16),
        grid=(N_HEADS, lq // tq),
        in_specs=[pl.BlockSpec((1, tq, MLA_QK_PAD), lambda h, i: (h, i, 0))] + specs,
        out_specs=pl.BlockSpec((tq, HEAD_V), lambda h, i: (i, h)),
        compiler_params=_cparams(("parallel", "arbitrary")),
        name="attention",
    )(q, *args)


def _head_post(o, gate, gain, centre, silu_gate):
    sig = jax.nn.sigmoid(gate)
    act = gate * sig if silu_gate else sig
    outs = []
    for h in range(N_HEADS):
        sl = slice(h * HEAD_V, (h + 1) * HEAD_V)
        oh = o[:, sl]
        if centre:
            oh = oh - jnp.mean(oh, axis=-1, keepdims=True)
        oh = oh * lax.rsqrt(jnp.mean(oh * oh, axis=-1, keepdims=True) + EPS)
        outs.append((oh * gain[:, sl] * act[:, sl]).astype(BF16))
    return jnp.concatenate(outs, axis=1)


def _merge_kernel(x_ref, g0_ref, g1_ref, g2_ref, g3_ref, af_ref, ab_ref, b_ref, cf_ref, cb_ref, df_ref, db_ref,
                  za_ref, zc_ref, zd_ref, gain_ref, wb_ref, wo_ref, g_ref, o_ref):
    gain = gain_ref[...]
    branches = (
        _head_post(af_ref[...] + ab_ref[...], za_ref[...].astype(F32), gain[0:1], False, True),
        b_ref[...],
        _head_post(cf_ref[...] + cb_ref[...], zc_ref[...].astype(F32), gain[1:2], False, False),
        _head_post(df_ref[...] + db_ref[...], zd_ref[...].astype(F32), gain[2:3], True, True),
    )
    s = None
    for n, (br, gate_ref) in enumerate(zip(branches, (g0_ref, g1_ref, g2_ref, g3_ref))):
        proj = _dot(br, wb_ref[n])
        gate = jax.nn.sigmoid(gate_ref[...].astype(F32))
        s = gate * proj if s is None else s + gate * proj
    m = _dot(s.astype(BF16), wo_ref[...])
    o_ref[...] = x_ref[...] + g_ref[0:1] * m


def merge_branches(x, zmain, gdn, attn, mlstm, ret, gains, w_branch, w_out, gvec):
    m = x.shape[0]
    tm = min(m, 512)
    row = lambda w: pl.BlockSpec((tm, w), lambda i: (i, 0))
    zcol = lambda off: pl.BlockSpec((tm, BRANCH_W), lambda i: (i, off // BRANCH_W))
    return pl.pallas_call(
        _merge_kernel,
        out_shape=jax.ShapeDtypeStruct((m, D_MODEL), F32),
        grid=(m // tm,),
        in_specs=[row(D_MODEL)]
                 + [pl.BlockSpec((tm, D_MODEL), lambda i, n=n: (i, OFF_GATE // D_MODEL + n)) for n in range(N_BRANCH)]
                 + [row(BRANCH_W)] * 7
                 + [zcol(OFF_GDN_Z), zcol(OFF_MLSTM_O), zcol(OFF_RET_G),
                    pl.BlockSpec((8, BRANCH_W), lambda i: (0, 0)),
                    pl.BlockSpec((N_BRANCH, BRANCH_W, D_MODEL), lambda i: (0, 0, 0)),
                    pl.BlockSpec((D_MODEL, D_MODEL), lambda i: (0, 0)),
                    pl.BlockSpec((8, D_MODEL), lambda i: (0, 0))],
        out_specs=row(D_MODEL),
        compiler_params=pltpu.CompilerParams(dimension_semantics=("parallel",), vmem_limit_bytes=MERGE_VMEM_LIMIT),
        name="merge_branches",
    )(x, zmain, zmain, zmain, zmain, gdn[0], gdn[1], attn, mlstm[0], mlstm[1], ret[0], ret[1],
      zmain, zmain, zmain, gains, w_branch, w_out, gvec)


def _ffn_kernel(x_ref, pv_ref, wg_ref, wu_ref, wd_ref, o_ref, h_ref, acc_ref):
    f = pl.program_id(1)

    @pl.when(f == 0)
    def _():
        pv = pv_ref[...]
        h_ref[...] = (_rms(x_ref[...]) * pv[0:1] * pv[1:2] + pv[2:3]).astype(BF16)
        acc_ref[...] = jnp.zeros_like(acc_ref)

    h = h_ref[...]
    g = _dot(h, wg_ref[...])
    u = _dot(h, wu_ref[...])
    act = (g * jax.nn.sigmoid(g) * u).astype(BF16)
    acc_ref[...] += _dot(act, wd_ref[...])

    @pl.when(f == pl.num_programs(1) - 1)
    def _():
        o_ref[...] = x_ref[...] + pv_ref[3:4] * acc_ref[...]


def dense_ffn(x, pv, w_gu, w_down):
    m = x.shape[0]
    tm = min(m, 1024)
    tf = 512
    nf = D_FF // tf
    return pl.pallas_call(
        _ffn_kernel,
        out_shape=jax.ShapeDtypeStruct((m, D_MODEL), F32),
        grid=(m // tm, nf),
        in_specs=[pl.BlockSpec((tm, D_MODEL), lambda i, f: (i, 0)),
                  pl.BlockSpec((8, D_MODEL), lambda i, f: (0, 0)),
                  pl.BlockSpec((D_MODEL, tf), lambda i, f: (0, f)),
                  pl.BlockSpec((D_MODEL, tf), lambda i, f: (0, f + nf)),
                  pl.BlockSpec((tf, D_MODEL), lambda i, f: (f, 0))],
        out_specs=pl.BlockSpec((tm, D_MODEL), lambda i, f: (i, 0)),
        scratch_shapes=[pltpu.VMEM((tm, D_MODEL), BF16), pltpu.VMEM((tm, D_MODEL), F32)],
        compiler_params=_cparams(("parallel", "arbitrary")),
        name="dense_ffn",
    )(x, pv, w_gu, w_gu, w_down)


def _router_kernel(x_ref, pv_ref, w_ref, b_ref, h_ref, route_ref):
    pv = pv_ref[...]
    h = _rms(x_ref[...]) * pv[0:1] * pv[1:2] + pv[2:3]
    h_hi = h.astype(BF16)
    h_lo = (h - h_hi.astype(F32)).astype(BF16)
    w = w_ref[...]
    w_hi = w.astype(BF16)
    w_lo = (w - w_hi.astype(F32)).astype(BF16)
    logits = _dot(h_hi, w_hi) + _dot(h_hi, w_lo) + _dot(h_lo, w_hi) + b_ref[...]
    h_ref[...] = h
    lane = lax.broadcasted_iota(jnp.int32, logits.shape, 1).astype(F32)
    m1 = jnp.max(logits, axis=1, keepdims=True)
    e1 = jnp.min(jnp.where(logits == m1, lane, 128.0), axis=1, keepdims=True)
    rest = jnp.where(lane == e1, -jnp.inf, logits)
    m2 = jnp.max(rest, axis=1, keepdims=True)
    e2 = jnp.min(jnp.where(rest == m2, lane, 128.0), axis=1, keepdims=True)
    z2 = jnp.exp(m2 - m1)
    w1 = 1.0 / (1.0 + z2)
    route_ref[...] = jnp.where(lane == 0.0, w1, jnp.where(lane == 1.0, z2 * w1, jnp.where(
        lane == 2.0, e1, jnp.where(lane == 3.0, e2, 0.0))))


def moe_router(x, pv, w_router_pad, b_router_pad):
    m = x.shape[0]
    tm = min(m, 1024)
    return pl.pallas_call(
        _router_kernel,
        out_shape=(jax.ShapeDtypeStruct((m, D_MODEL), F32), jax.ShapeDtypeStruct((m, 128), F32)),
        grid=(m // tm,),
        in_specs=[pl.BlockSpec((tm, D_MODEL), lambda i: (i, 0)),
                  pl.BlockSpec((8, D_MODEL), lambda i: (0, 0)),
                  pl.BlockSpec((D_MODEL, 128), lambda i: (0, 0)),
                  pl.BlockSpec((1, 128), lambda i: (0, 0))],
        out_specs=(pl.BlockSpec((tm, D_MODEL), lambda i: (i, 0)), pl.BlockSpec((tm, 128), lambda i: (i, 0))),
        compiler_params=_cparams(("parallel",)),
        name="moe_router",
    )(x, pv, w_router_pad, b_router_pad)


def _expert_kernel(be_ref, nb_ref, x_ref, wg_ref, wu_ref, wd_ref, o_ref, acc_ref):
    b = pl.program_id(0)
    f = pl.program_id(1)

    @pl.when(b < nb_ref[0])
    def _():
        @pl.when(f == 0)
        def _():
            acc_ref[...] = jnp.zeros_like(acc_ref)

        x = x_ref[...].astype(BF16)
        g = _dot(x, wg_ref[0].astype(BF16))
        u = _dot(x, wu_ref[0].astype(BF16))
        act = (g * jax.nn.sigmoid(g) * u).astype(BF16)
        acc_ref[...] += _dot(act, wd_ref[0].astype(BF16))

        @pl.when(f == pl.num_programs(1) - 1)
        def _():
            o_ref[...] = acc_ref[...]

    @pl.when((b >= nb_ref[0]) & (f == pl.num_programs(1) - 1))
    def _():
        o_ref[...] = jnp.zeros_like(o_ref)


def expert_ffn(block_e, n_used, xb, w_gu, w_down):
    cap = xb.shape[0]
    nb = cap // MOE_ROWS
    tf = 512
    nf = D_FF // tf

    def live(b, nbr):
        return jnp.minimum(b, nbr[0] - 1)

    def fsel(b, f, nbr):
        return jnp.where(b < nbr[0], f, nf - 1)

    grid_spec = pltpu.PrefetchScalarGridSpec(
        num_scalar_prefetch=2,
        grid=(nb, nf),
        in_specs=[pl.BlockSpec((MOE_ROWS, D_MODEL), lambda b, f, be, nbr: (live(b, nbr), 0)),
                  pl.BlockSpec((1, D_MODEL, tf), lambda b, f, be, nbr: (be[live(b, nbr)], 0, fsel(b, f, nbr))),
                  pl.BlockSpec((1, D_MODEL, tf), lambda b, f, be, nbr: (be[live(b, nbr)], 0, fsel(b, f, nbr) + nf)),
                  pl.BlockSpec((1, tf, D_MODEL), lambda b, f, be, nbr: (be[live(b, nbr)], fsel(b, f, nbr), 0))],
        out_specs=pl.BlockSpec((MOE_ROWS, D_MODEL), lambda b, f, be, nbr: (b, 0)),
        scratch_shapes=[pltpu.VMEM((MOE_ROWS, D_MODEL), F32)],
    )
    return pl.pallas_call(
        _expert_kernel,
        out_shape=jax.ShapeDtypeStruct((cap, D_MODEL), F32),
        grid_spec=grid_spec,
        compiler_params=_cparams(("arbitrary", "arbitrary")),
        name="expert_ffn",
    )(block_e, n_used, xb, w_gu, w_gu, w_down)


def _final_kernel(x_ref, y0_ref, y1_ref, w_ref, pv_ref, o_ref):
    w = w_ref[...]
    f = w[:, 0:1] * y0_ref[...] + w[:, 1:2] * y1_ref[...]
    x = x_ref[...] + pv_ref[0:1] * f
    o_ref[...] = _rms(x) * pv_ref[1:2]


def final_combine(x, y2, w, pv):
    m = x.shape[0]
    tm = min(m, 1024)
    row = lambda wd: pl.BlockSpec((tm, wd), lambda i: (i, 0))
    return pl.pallas_call(
        _final_kernel,
        out_shape=jax.ShapeDtypeStruct((m, D_MODEL), F32),
        grid=(m // tm,),
        in_specs=[row(D_MODEL), row(D_MODEL), pl.BlockSpec((tm, D_MODEL), lambda i: (i + m // tm, 0)), row(128),
                  pl.BlockSpec((8, D_MODEL), lambda i: (0, 0))],
        out_specs=row(D_MODEL),
        compiler_params=_cparams(("parallel",)),
        name="final_combine",
    )(x, y2, y2, w, pv)


CH = 128
NEG = -1e30


def _masks(reverse):
    r = lax.broadcasted_iota(jnp.int32, (CH, CH), 0)
    c = lax.broadcasted_iota(jnp.int32, (CH, CH), 1)
    return (r <= c, r < c) if reverse else (r >= c, r > c)


def _cumsum_time(incl, x):
    m = jnp.where(incl, 1.0, 0.0).astype(BF16)
    hi = x.astype(BF16)
    lo = (x - hi.astype(F32)).astype(BF16)
    return _dot(m, hi) + _dot(m, lo)


SCAN_SUB = 4


def _scan_grid(l):
    rows = min(l, SCAN_SUB * CH)
    return l // rows, rows


def _sweep(d, nsub):
    return list(range(nsub)) if d == 0 else list(range(nsub - 1, -1, -1))


def _dir_specs(nsteps, rows, width, col_block):
    return [pl.BlockSpec((rows, width), lambda n: (n, col_block)),
            pl.BlockSpec((rows, width), lambda n: (nsteps - 1 - n, col_block))]


def _whole(a):
    return pl.BlockSpec(a.shape, lambda n: tuple(0 for _ in a.shape))


def _ret_consts():
    log_gamma = np.log(1.0 - 2.0 ** (-5.0 - np.arange(N_HEADS, dtype=np.float64)))
    pos = np.arange(CH, dtype=np.float64)
    diff = pos[:, None] - pos[None, :]
    dec_f = np.where(diff >= 0, np.exp(log_gamma[:, None, None] * diff), 0.0)
    dec = np.stack([dec_f, np.transpose(dec_f, (0, 2, 1))])
    qs_f = np.exp(log_gamma[None, :] * (pos[:, None] + 1.0))
    ks_f = np.exp(log_gamma[None, :] * (CH - 1.0 - pos[:, None]))
    qs_b = np.exp(log_gamma[None, :] * (CH - pos[:, None]))
    ks_b = np.exp(log_gamma[None, :] * pos[:, None])
    rep = lambda a: np.repeat(a, RET_DK, axis=1)
    qs = np.stack([rep(qs_f), rep(qs_b)])
    ks = np.stack([rep(ks_f), rep(ks_b)])
    chunk_decay = [float(np.exp(lg * CH)) for lg in log_gamma]
    return (jnp.asarray(dec, F32), jnp.asarray(qs, F32), jnp.asarray(ks, F32)), chunk_decay


def _ret_kernel(*refs, rotary, chunk_decay):
    if rotary:
        (qf, qb, kf, kb, vf, vb, cosf, cosb, sinf, sinb, dec_ref, qs_ref, ks_ref, s0_ref,
         of_ref, ob_ref, s_ref) = refs
        tabs = ((cosf, sinf), (cosb, sinb))
    else:
        qf, qb, kf, kb, vf, vb, dec_ref, qs_ref, ks_ref, s0_ref, of_ref, ob_ref, s_ref = refs
        tabs = (None, None)

    @pl.when(pl.program_id(0) == 0)
    def _():
        s_ref[...] = s0_ref[...]

    lane = lax.broadcasted_iota(jnp.int32, (CH, N_HEADS * RET_DK), 1)
    first_half = (lane & (RET_DK - 1)) < RET_DK // 2

    nsub = qf.shape[0] // CH
    chains = {}
    for d, (q_ref, k_ref, v_ref, o_ref) in enumerate(((qf, kf, vf, of_ref), (qb, kb, vb, ob_ref))):
        for j in range(nsub):
            rows = slice(j * CH, (j + 1) * CH)
            q = q_ref[rows, :].astype(F32)
            k = k_ref[rows, :].astype(F32) * RET_DK ** -0.5
            if rotary:
                cos, sin = tabs[d][0][rows, :], tabs[d][1][rows, :]

                def rot(x, cos=cos, sin=sin):
                    swapped = jnp.where(first_half, pltpu.roll(x, N_HEADS * RET_DK - RET_DK // 2, 1),
                                        pltpu.roll(x, RET_DK // 2, 1))
                    return x * cos + swapped * sin

                q, k = rot(q), rot(k)
            qb16, kb16 = q.astype(BF16), k.astype(BF16)
            q_in = (q * qs_ref[d]).astype(BF16)
            k_out = k * ks_ref[d]
            k_t = [k_out[:, :128].T, k_out[:, 128:].T]
            for h in range(N_HEADS):
                sl = slice(h * RET_DK, (h + 1) * RET_DK)
                chains[d, j, h] = dict(
                    o_ref=o_ref, rows=rows, vh=v_ref[rows, h * HEAD_V:(h + 1) * HEAD_V], q=qb16[:, sl], k=kb16[:, sl],
                    q_in=q_in[:, sl], k_th=k_t[h // 2][(h % 2) * RET_DK:(h % 2 + 1) * RET_DK, :].astype(BF16))

    for (d, j, h), c in chains.items():
        c["p"] = (_dot_nt(c["q"], c["k"]) * dec_ref[d, h]).astype(BF16)
    for c in chains.values():
        c["intra"] = _dot(c["p"], c["vh"])
        c["update"] = _dot(c["k_th"], c["vh"])
    states = {(d, h): s_ref[d, h] for d in range(2) for h in range(N_HEADS)}
    for i in range(nsub):
        for (d, h), s in list(states.items()):
            c = chains[d, _sweep(d, nsub)[i], h]
            c["o_ref"][c["rows"], h * HEAD_V:(h + 1) * HEAD_V] = c["intra"] + _dot(c["q_in"], s.astype(BF16))
            states[d, h] = chunk_decay[h] * s + c["update"]
    for (d, h), s in states.items():
        s_ref[d, h] = s


def retention_scan(z, s0, tables):
    l = z.shape[0]
    nsteps, rows = _scan_grid(l)
    consts, chunk_decay = _ret_consts()
    rotary = tables is not None
    w = N_HEADS * RET_DK
    specs = (_dir_specs(nsteps, rows, w, OFF_RET_Q // w) + _dir_specs(nsteps, rows, w, OFF_RET_K // w)
             + _dir_specs(nsteps, rows, BRANCH_W, OFF_RET_V // BRANCH_W))
    args = [z] * 6
    if rotary:
        specs += _dir_specs(nsteps, rows, w, 0) + _dir_specs(nsteps, rows, w, 0)
        args += [tables[0], tables[0], tables[1], tables[1]]
    specs += [_whole(c) for c in consts] + [_whole(s0)]
    args += list(consts) + [s0]
    return pl.pallas_call(
        functools.partial(_ret_kernel, rotary=rotary, chunk_decay=chunk_decay),
        out_shape=(jax.ShapeDtypeStruct((l, BRANCH_W), F32), jax.ShapeDtypeStruct((l, BRANCH_W), F32),
                   jax.ShapeDtypeStruct(s0.shape, F32)),
        grid=(nsteps,),
        in_specs=specs,
        out_specs=tuple(_dir_specs(nsteps, rows, BRANCH_W, 0)) + (_whole(s0),),
        compiler_params=_cparams(("arbitrary",)),
        name="retention_scan",
    )(*args)


def _mlstm_kernel(qf, qb, kf, kb, vf, vb, smf, smb, bias_ref, c0_ref, m0_ref, of_ref, ob_ref, c_ref, m_ref):
    @pl.when(pl.program_id(0) == 0)
    def _():
        c_ref[...] = c0_ref[...]
        m_ref[...] = m0_ref[...]

    lane = lax.broadcasted_iota(jnp.int32, (CH, 128), 1)
    is_forget = (lane >= F_LANE) & (lane < F_LANE + 2 * N_HEADS)
    ones_col = jnp.where(lane == 0, 1.0, 0.0).astype(BF16)

    nsub = qf.shape[0] // CH
    chains = {}
    for d, (q_ref, k_ref, v_ref, sm_ref, o_ref) in enumerate(((qf, kf, vf, smf, of_ref), (qb, kb, vb, smb, ob_ref))):
        reverse = d == 1
        incl, _ = _masks(reverse)
        last = 0 if reverse else CH - 1
        for j in range(nsub):
            rows = slice(j * CH, (j + 1) * CH)
            pre = sm_ref[rows, :] + bias_ref[0:1]
            x = jnp.where(is_forget, jax.nn.log_sigmoid(pre), pre)
            b = _cumsum_time(incl, x)
            x_t, b_t = x.T, b.T
            q = (q_ref[rows, :].astype(F32) * MLSTM_DK ** -0.5).astype(BF16)
            kf32 = k_ref[rows, :].astype(F32)
            k_t = [kf32[:, :128].T, kf32[:, 128:].T]
            for h in range(N_HEADS):
                gi, gf = I_LANE + d * N_HEADS + h, F_LANE + d * N_HEADS + h
                sl = slice(h * MLSTM_DK, (h + 1) * MLSTM_DK)
                b_col, b_row = b[:, gf:gf + 1], b_t[gf:gf + 1, :]
                li_row = x_t[gi:gi + 1, :]
                b_last = b[last:last + 1, gf:gf + 1]
                d_log = jnp.where(incl, b_col - b_row + li_row, NEG)
                m_intra = jnp.max(d_log, axis=1, keepdims=True)
                e_row = b_last - b_row + li_row
                chains[d, j, h] = dict(
                    o_ref=o_ref, rows=rows, q=q[:, sl], k=k_ref[rows, sl], b_col=b_col, b_last=b_last, m_intra=m_intra,
                    e_row=e_row, m_end=jnp.max(e_row, axis=1, keepdims=True),
                    w_intra=jnp.exp(d_log - m_intra),
                    k_th=k_t[h // 2][(h % 2) * MLSTM_DK:(h % 2 + 1) * MLSTM_DK, :],
                    v_aug=jnp.concatenate([v_ref[rows, h * HEAD_V:(h + 1) * HEAD_V], ones_col], axis=1))

    for c in chains.values():
        c["p"] = (c["w_intra"] * _dot_nt(c["q"], c["k"])).astype(BF16)
    for c in chains.values():
        c["pv"] = _dot(c["p"], c["v_aug"])
    c_augs = {(d, h): c_ref[d, h] for d in range(2) for h in range(N_HEADS)}
    m_ss = {(d, h): m_ref[d, h][0:1, 0:1] for d in range(2) for h in range(N_HEADS)}
    for i in range(nsub):
        cur = {(d, h): chains[d, _sweep(d, nsub)[i], h] for (d, h) in c_augs}
        qcs = {key: _dot(c["q"], c_augs[key].astype(BF16)) for key, c in cur.items()}
        m_news = {key: jnp.maximum(c["b_last"] + m_ss[key], c["m_end"]) for key, c in cur.items()}
        updates = {key: _dot((c["k_th"] * jnp.exp(c["e_row"] - m_news[key])).astype(BF16), c["v_aug"])
                   for key, c in cur.items()}
        for (d, h), c in cur.items():
            m_s = m_ss[d, h]
            m_t = jnp.maximum(c["b_col"] + m_s, c["m_intra"])
            inter = jnp.exp(c["b_col"] + m_s - m_t)
            r = jnp.exp(c["m_intra"] - m_t)
            tot = inter * qcs[d, h] + r * c["pv"]
            den = jnp.maximum(jnp.abs(tot[:, HEAD_V:HEAD_V + 1]), jnp.exp(-m_t))
            c["o_ref"][c["rows"], h * HEAD_V:(h + 1) * HEAD_V] = tot[:, :HEAD_V] / den
        for key, c in cur.items():
            c_augs[key] = jnp.exp(c["b_last"] + m_ss[key] - m_news[key]) * c_augs[key] + updates[key]
            m_ss[key] = m_news[key]
    for (d, h) in c_augs:
        c_ref[d, h] = c_augs[d, h]
        m_ref[d, h] = jnp.broadcast_to(m_ss[d, h], (8, 128))


def mlstm_scan(z, small, bias, c0, m0):
    l = z.shape[0]
    nsteps, rows = _scan_grid(l)
    w = N_HEADS * MLSTM_DK
    specs = (_dir_specs(nsteps, rows, w, OFF_MLSTM_Q // w) + _dir_specs(nsteps, rows, w, OFF_MLSTM_K // w)
             + _dir_specs(nsteps, rows, BRANCH_W, OFF_MLSTM_V // BRANCH_W) + _dir_specs(nsteps, rows, 128, 0))
    specs += [_whole(bias), _whole(c0), _whole(m0)]
    return pl.pallas_call(
        _mlstm_kernel,
        out_shape=(jax.ShapeDtypeStruct((l, BRANCH_W), F32), jax.ShapeDtypeStruct((l, BRANCH_W), F32),
                   jax.ShapeDtypeStruct(c0.shape, F32), jax.ShapeDtypeStruct(m0.shape, F32)),
        grid=(nsteps,),
        in_specs=specs,
        out_specs=tuple(_dir_specs(nsteps, rows, BRANCH_W, 0)) + (_whole(c0), _whole(m0)),
        compiler_params=_cparams(("arbitrary",)),
        name="mlstm_scan",
    )(z, z, z, z, z, z, small, small, bias, c0, m0)


QKV_W = 3 * N_HEADS * GDN_DK
HALO = 8


def _gdn_prep_kernel(x_ref, prev_ref, next_ref, w_ref, q_ref, k_ref, v_ref):
    i = pl.program_id(0)
    tm = x_ref.shape[0]
    x = x_ref[...].astype(F32)
    prev = jnp.where(i > 0, prev_ref[...].astype(F32), 0.0)
    nxt = jnp.where(i < pl.num_programs(0) - 1, next_ref[...].astype(F32), 0.0)
    xe = jnp.concatenate([prev, x, nxt], axis=0)
    w = w_ref[...]
    y = None
    for tap in range(CONV_K):
        off = HALO + tap - CONV_K // 2
        term = w[tap:tap + 1] * xe[off:off + tm]
        y = term if y is None else y + term
    y = y * jax.nn.sigmoid(y)
    hw = N_HEADS * GDN_DK
    for h in range(N_HEADS):
        sl = slice(h * GDN_DK, (h + 1) * GDN_DK)
        qh = y[:, sl]
        kh = y[:, hw + h * GDN_DK:hw + (h + 1) * GDN_DK]
        q_ref[:, sl] = (qh * lax.rsqrt(jnp.sum(qh * qh, axis=-1, keepdims=True) + EPS) * GDN_DK ** -0.5).astype(BF16)
        k_ref[:, sl] = (kh * lax.rsqrt(jnp.sum(kh * kh, axis=-1, keepdims=True) + EPS)).astype(BF16)
    v_ref[...] = y[:, 2 * hw:].astype(BF16)


def gdn_prep(z, conv_w):
    l = z.shape[0]
    tm = min(l, 256)
    nb = l // tm
    r8 = tm // HALO
    cb = OFF_GDN_QKV // QKV_W
    w8 = jnp.concatenate([conv_w.astype(F32), jnp.zeros((8 - CONV_K, QKV_W), F32)], axis=0)
    return pl.pallas_call(
        _gdn_prep_kernel,
        out_shape=tuple(jax.ShapeDtypeStruct((l, BRANCH_W), BF16) for _ in range(3)),
        grid=(nb,),
        in_specs=[pl.BlockSpec((tm, QKV_W), lambda i: (i, cb)),
                  pl.BlockSpec((HALO, QKV_W), lambda i: (jnp.maximum(i * r8 - 1, 0), cb)),
                  pl.BlockSpec((HALO, QKV_W), lambda i: (jnp.minimum((i + 1) * r8, nb * r8 - 1), cb)),
                  pl.BlockSpec((8, QKV_W), lambda i: (0, 0))],
        out_specs=tuple(pl.BlockSpec((tm, BRANCH_W), lambda i: (i, 0)) for _ in range(3)),
        compiler_params=_cparams(("parallel",)),
        name="gdn_prep",
    )(z, z, z, w8)


N_LEVELS = 7


def _gdn_kernel(qf, qb, kf, kb, vf, vb, smf, smb, par_ref, s0_ref, of_ref, ob_ref, s_ref):
    @pl.when(pl.program_id(0) == 0)
    def _():
        s_ref[...] = s0_ref[...]

    lane = lax.broadcasted_iota(jnp.int32, (CH, 128), 1)
    is_decay = lane < B_LANE
    ri = lax.broadcasted_iota(jnp.int32, (CH, CH), 0)
    ci = lax.broadcasted_iota(jnp.int32, (CH, CH), 1)
    eye = jnp.where(ri == ci, 1.0, 0.0)
    pair_masks = [((ri >> (l + 1)) == (ci >> (l + 1))) & ((ri >> l) != (ci >> l)) for l in range(N_LEVELS)]

    nsub = qf.shape[0] // CH
    chains = {}
    for d, (q_ref, k_ref, v_ref, sm_ref, o_ref) in enumerate(((qf, kf, vf, smf, of_ref), (qb, kb, vb, smb, ob_ref))):
        reverse = d == 1
        incl, strict = _masks(reverse)
        last = 0 if reverse else CH - 1
        for j in range(nsub):
            rows = slice(j * CH, (j + 1) * CH)
            sm = sm_ref[rows, :]
            log_a = -jnp.exp(par_ref[1:2]) * jax.nn.softplus(sm + par_ref[0:1])
            x = jnp.where(is_decay, log_a, jax.nn.sigmoid(sm))
            g = _cumsum_time(incl, x)
            g_t = g.T
            for h in range(N_HEADS):
                ga, gb = A_LANE + d * N_HEADS + h, B_LANE + d * N_HEADS + h
                sl = slice(h * GDN_DK, (h + 1) * GDN_DK)
                g_col, g_row = g[:, ga:ga + 1], g_t[ga:ga + 1, :]
                g_last = g[last:last + 1, ga:ga + 1]
                beta = x[:, gb:gb + 1]
                decay = jnp.where(incl, jnp.exp(jnp.minimum(g_col - g_row, 0.0)), 0.0)
                kh = k_ref[rows, sl]
                kf32 = kh.astype(F32)
                kbeta = kf32 * beta
                a = jnp.where(strict, _dot_nt(kbeta.astype(BF16), kh) * decay, 0.0)
                e_g = jnp.exp(g_col)
                rhs = jnp.concatenate([v_ref[rows, sl].astype(F32) * beta, kbeta * e_g], axis=1)
                chains[d, j, h] = dict(
                    rows=rows, sl=sl, o_ref=o_ref, a=a, rhs=rhs, s_decay=jnp.exp(g_last),
                    qk=(_dot_nt(q_ref[rows, sl], kh) * decay).astype(BF16),
                    q_in=(q_ref[rows, sl].astype(F32) * e_g).astype(BF16),
                    k_out_t=(kf32.T * jnp.exp(g_last - g_row)).astype(BF16))

    cl = list(chains.values())
    xs = [eye - jnp.where(pair_masks[0], c["a"], 0.0) for c in cl]
    for pm in pair_masks[1:]:
        ys = [_dot(jnp.where(pm, c["a"], 0.0).astype(BF16), x.astype(BF16)) for c, x in zip(cl, xs)]
        xs = [x - _dot(x.astype(BF16), y.astype(BF16)) for x, y in zip(xs, ys)]
    for c, x in zip(cl, xs):
        c["sol"] = c["rhs"] + _dot((x - eye).astype(BF16), c["rhs"].astype(BF16))
    states = {(d, h): s_ref[d, h] for d in range(2) for h in range(N_HEADS)}
    for i in range(nsub):
        cur = {(d, h): chains[d, _sweep(d, nsub)[i], h] for (d, h) in states}
        s16s = {key: s.astype(BF16) for key, s in states.items()}
        v_news = {key: (c["sol"][:, :HEAD_V] - _dot(c["sol"][:, HEAD_V:].astype(BF16), s16s[key])).astype(BF16)
                  for key, c in cur.items()}
        updates = {key: _dot(c["k_out_t"], v_news[key]) for key, c in cur.items()}
        for key, c in cur.items():
            c["o_ref"][c["rows"], c["sl"]] = _dot(c["q_in"], s16s[key]) + _dot(c["qk"], v_news[key])
        for key, c in cur.items():
            states[key] = states[key] * c["s_decay"] + updates[key]
    for (d, h), s in states.items():
        s_ref[d, h] = s


def gdn_scan(qn, kn, vn, small, par, s0):
    l = qn.shape[0]
    nsteps, rows = _scan_grid(l)
    specs = _dir_specs(nsteps, rows, BRANCH_W, 0) * 3 + _dir_specs(nsteps, rows, 128, 0)
    specs += [_whole(par), _whole(s0)]
    return pl.pallas_call(
        _gdn_kernel,
        out_shape=(jax.ShapeDtypeStruct((l, BRANCH_W), F32), jax.ShapeDtypeStruct((l, BRANCH_W), F32),
                   jax.ShapeDtypeStruct(s0.shape, F32)),
        grid=(nsteps,),
        in_specs=specs,
        out_specs=tuple(_dir_specs(nsteps, rows, BRANCH_W, 0)) + (_whole(s0),),
        compiler_params=_cparams(("arbitrary",)),
        name="gdn_scan",
    )(qn, qn, kn, kn, vn, vn, small, small, par, s0)


def _lane_rows(entries):
    r = jnp.zeros((8, 128), F32)
    for row, lane, vals in entries:
        r = r.at[row, lane:lane + vals.shape[0]].set(vals.astype(F32))
    return r


def gdn_branch(zc, sc, zl, sl, conv_w, a_log, dt_bias):
    par = _lane_rows([(0, A_LANE, dt_bias.reshape(-1)), (1, A_LANE, a_log.reshape(-1))])
    s0 = jnp.zeros((2, N_HEADS, GDN_DK, HEAD_V), F32)
    ofc, obc, s1 = gdn_scan(*gdn_prep(zc, conv_w), sc, par, s0)
    ofl, obl, _ = gdn_scan(*gdn_prep(zl, conv_w), sl, par, s1)
    return (ofc, obc), (ofl, obl)


def mlstm_branch(zc, sc, zl, sl, gate_b):
    bias = _lane_rows([(0, I_LANE, gate_b[0].reshape(-1)), (0, F_LANE, gate_b[1].reshape(-1))])
    c0 = jnp.zeros((2, N_HEADS, MLSTM_DK, 2 * HEAD_V), F32)
    m0 = jnp.zeros((2, N_HEADS, 8, 128), F32)
    ofc, obc, c1, m1 = mlstm_scan(zc, sc, bias, c0, m0)
    ofl, obl, _, _ = mlstm_scan(zl, sl, bias, c1, m1)
    return (ofc, obc), (ofl, obl)


def retention_branch(zc, zl, tables):
    s0 = jnp.zeros((2, N_HEADS, RET_DK, HEAD_V), F32)
    ofc, obc, s1 = retention_scan(zc, s0, None)
    ofl, obl, _ = retention_scan(zl, s1, tables)
    return (ofc, obc), (ofl, obl)


def _ret_rope_tables(n_lat):
    inv = ROPE_BASE ** (-jnp.arange(0, RET_DK, 2, dtype=F32) / RET_DK)
    ang = jnp.arange(n_lat, dtype=F32)[:, None] * inv[None, :]
    cos = jnp.concatenate([jnp.cos(ang), jnp.cos(ang)], axis=-1)
    sin = jnp.concatenate([-jnp.sin(ang), jnp.sin(ang)], axis=-1)
    return jnp.tile(cos, (1, N_HEADS)), jnp.tile(sin, (1, N_HEADS))


def _rope_tables(n_lat):
    q = MLA_ROPE // 4
    inv = ROPE_BASE ** (-jnp.arange(0, 2 * q, 2, dtype=F32) / (2 * q))
    t = jnp.arange(n_lat)
    row = (t // GRID_W).astype(F32)[:, None] * inv[None, :]
    col = (t % GRID_W).astype(F32)[:, None] * inv[None, :]
    cos = jnp.concatenate([jnp.cos(row), jnp.cos(row), jnp.cos(col), jnp.cos(col)], axis=-1)
    sin = jnp.concatenate([-jnp.sin(row), jnp.sin(row), -jnp.sin(col), jnp.sin(col)], axis=-1)
    return cos, sin


def _take_cols(w, perm):
    runs, start = [], 0
    for i in range(1, len(perm) + 1):
        if i == len(perm) or perm[i] != perm[i - 1] + 1:
            runs.append((int(perm[start]), int(perm[i - 1]) + 1))
            start = i
    return jnp.concatenate([w[:, a:b] for a, b in runs], axis=1)


def _pad_rows(v, n=8):
    rows = [jnp.reshape(r, (1, -1)).astype(F32) for r in v]
    d = rows[0].shape[1]
    return jnp.concatenate(rows + [jnp.zeros((n - len(rows), d), F32)], axis=0)


def _mla_weights(w_uq, w_ukv):
    wq = w_uq.reshape(MLA_Q_RANK, N_HEADS, MLA_QK)
    rope = wq[:, :, MLA_NOPE:]
    swapped = rope[:, :, _rope_swap(np.arange(MLA_ROPE))]
    wq_ext = jnp.concatenate([wq[:, :, :MLA_NOPE], rope, swapped], axis=-1).reshape(MLA_Q_RANK, N_HEADS * 256)
    return wq_ext.astype(BF16), w_ukv.astype(BF16)


def _moe(x_l, pv2, g2, w_router, b_router, w_gu, w_down, final_g):
    n = x_l.shape[0]
    w_pad = jnp.concatenate([w_router, jnp.zeros((D_MODEL, 128 - N_EXPERTS), F32)], axis=1)
    b_pad = jnp.concatenate([b_router, jnp.full((128 - N_EXPERTS,), -1e30, F32)]).reshape(1, 128)
    h2, route = moe_router(x_l, pv2, w_pad, b_pad)
    top_e = route[:, TOP_K:2 * TOP_K].astype(jnp.int32)
    n_assign = n * TOP_K
    flat_e = top_e.reshape(-1)
    onehot = (flat_e[:, None] == jnp.arange(N_EXPERTS)[None, :]).astype(jnp.int32)
    csum = jnp.cumsum(onehot, axis=0)
    counts = csum[-1]
    rank = jnp.take_along_axis(csum, flat_e[:, None], axis=1)[:, 0] - 1
    padded = (counts + MOE_ROWS - 1) // MOE_ROWS * MOE_ROWS
    pad_end = jnp.cumsum(padded)
    dest = (pad_end - padded)[flat_e] + rank
    nb = n_assign // MOE_ROWS + N_EXPERTS
    cap = nb * MOE_ROWS
    block_e = jnp.minimum(jnp.searchsorted(pad_end, jnp.arange(nb) * MOE_ROWS, side='right'),
                          N_EXPERTS - 1).astype(jnp.int32)
    n_used = (pad_end[-1] // MOE_ROWS).astype(jnp.int32).reshape(1)
    slot_tok = (jnp.arange(cap, dtype=jnp.int32) % n).at[dest].set(jnp.arange(n_assign, dtype=jnp.int32) // TOP_K)
    yb = expert_ffn(block_e, n_used, h2[slot_tok], w_gu, w_down)
    y2 = yb[dest.reshape(n, TOP_K).T.reshape(-1)]
    return final_combine(x_l, y2, route, _pad_rows([g2, final_g]))


def kernel(x, c, ctx, c_ctx, w_mod, b_mod, norm1_g, norm2_g, w_in, gdn_conv_w, gdn_a_log, gdn_dt_bias, gdn_norm_g, mla_q_norm_g, mla_kv_norm_g, mla_w_uq, mla_w_ukv, mlstm_gate_b, mlstm_norm_g, ret_norm_g, w_branch, w_out, ffn_w_in, ffn_w_down, moe_w_router, moe_b_router, moe_w_in, moe_w_down, final_norm_g):
    n_lat = x.shape[1]
    n_ctx = ctx.shape[1]
    x_l, x_c = x[0], ctx[0]
    cond = _pad_rows([c_ctx, c[0]])
    cos_l, sin_l = _rope_tables(n_lat)
    cos_c, sin_c = jnp.ones((n_ctx, MLA_ROPE), F32), jnp.zeros((n_ctx, MLA_ROPE), F32)
    ret_tables = _ret_rope_tables(n_lat)
    out = None
    for li in range(DEPTH):
        last = li == DEPTH - 1
        mod = modulation_vectors(cond, w_mod[li], b_mod[li])
        csh1, csc1, cg1, csh2, csc2, cg2 = jnp.split(mod[0], 6)
        sh1, sc1, g1, sh2, sc2, g2 = jnp.split(mod[1], 6)
        w_main = _take_cols(w_in[li], _MAIN_PERM).astype(BF16)
        w_small = _take_cols(w_in[li], _SMALL_PERM).astype(BF16)
        w_small = jnp.concatenate([w_small, jnp.zeros((D_MODEL, N_SMALL - w_small.shape[1]), BF16)], axis=1)
        pv_l = _pad_rows([norm1_g[li], 1 + sc1, sh1])
        pv_c = _pad_rows([norm1_g[li], 1 + csc1, csh1])
        zl = norm_proj(x_l, pv_l, w_main, 1664, BF16)
        zc = norm_proj(x_c, pv_c, w_main, 1664, BF16)
        sl = norm_proj(x_l, pv_l, w_small, 128, F32)
        sc = norm_proj(x_c, pv_c, w_small, 128, F32)

        a_c, a_l = gdn_branch(zc, sc, zl, sl, gdn_conv_w[li], gdn_a_log[li], gdn_dt_bias[li])
        c_c, c_l = mlstm_branch(zc, sc, zl, sl, mlstm_gate_b[li])
        d_c, d_l = retention_branch(zc, zl, ret_tables)
        gains = _pad_rows([gdn_norm_g[li], mlstm_norm_g[li], ret_norm_g[li]])

        wq_ext, wkv = _mla_weights(mla_w_uq[li], mla_w_ukv[li])
        gq, gkv = mla_q_norm_g[li].reshape(1, -1), mla_kv_norm_g[li].reshape(1, -1)
        ql, kl, vl = mla_project(zl, cos_l, sin_l, gq, gkv, wq_ext, wkv)
        qc, kc, vc = mla_project(zc, cos_c, sin_c, gq, gkv, wq_ext, wkv)
        b_l = attention(ql, [(kl, vl), (kc, vc)])

        wb = w_branch[li].astype(BF16)
        wo = w_out[li].astype(BF16)
        x_l = merge_branches(x_l, zl, a_l, b_l, c_l, d_l, gains, wb, wo, _pad_rows([g1]))
        if not last:
            b_c = attention(qc, [(kc, vc)])
            x_c = merge_branches(x_c, zc, a_c, b_c, c_c, d_c, gains, wb, wo, _pad_rows([cg1]))

        if li % 2 == 0:
            w_gu = ffn_w_in[li // 2].astype(BF16)
            w_dn = ffn_w_down[li // 2].astype(BF16)
            assert not last
            x_l = dense_ffn(x_l, _pad_rows([norm2_g[li], 1 + sc2, sh2, g2]), w_gu, w_dn)
            x_c = dense_ffn(x_c, _pad_rows([norm2_g[li], 1 + csc2, csh2, cg2]), w_gu, w_dn)
        else:
            assert last
            out = _moe(x_l, _pad_rows([norm2_g[li], 1 + sc2, sh2]), g2, moe_w_router[li // 2],
                       moe_b_router[li // 2], moe_w_in[li // 2], moe_w_down[li // 2], final_norm_g)
    return out[None]
```

```python
import functools
import math

import numpy as np
import jax
import jax.numpy as jnp
from jax import lax
from jax.experimental import pallas as pl
from jax.experimental.pallas import tpu as pltpu

F32 = jnp.float32
BF16 = jnp.bfloat16

D_MODEL = 1024
DEPTH = 2
GRID_W = 64
N_BRANCH = 4
N_HEADS = 4
HEAD_V = 128
BRANCH_W = N_HEADS * HEAD_V
GDN_DK = 128
CONV_K = 5
MLA_Q_RANK = 384
MLA_KV_RANK = 256
MLA_NOPE = 128
MLA_ROPE = 64
MLA_QK = MLA_NOPE + MLA_ROPE
MLA_QK_PAD = 256
MLA_V_PAD = 256
MLSTM_DK = 64
RET_DK = 64
ROPE_BASE = 10000.0
D_FF = 3584
N_EXPERTS = 8
TOP_K = 2
EPS = 1e-6

IN_WIDTHS = (
    N_BRANCH * D_MODEL,
    N_HEADS * GDN_DK, N_HEADS * GDN_DK, BRANCH_W, BRANCH_W, 2 * N_HEADS, 2 * N_HEADS,
    MLA_Q_RANK, MLA_KV_RANK, MLA_ROPE,
    N_HEADS * MLSTM_DK, N_HEADS * MLSTM_DK, BRANCH_W, BRANCH_W, 2 * N_HEADS, 2 * N_HEADS,
    N_HEADS * RET_DK, N_HEADS * RET_DK, BRANCH_W, BRANCH_W,
)
_IN_OFF = [0] + [int(o) for o in np.cumsum(IN_WIDTHS)]

VMEM_LIMIT = 48 * 1024 * 1024
MERGE_VMEM_LIMIT = 56 * 1024 * 1024
MOE_ROWS = 1024


def _cols(group):
    return np.arange(_IN_OFF[group], _IN_OFF[group + 1])


def _rope_swap(cols):
    q = MLA_ROPE // 4
    return np.concatenate([cols[q:2 * q], cols[:q], cols[3 * q:], cols[2 * q:3 * q]])


_MAIN_PERM = np.concatenate([
    _cols(1), _cols(2), _cols(3),
    _cols(4),
    _cols(0),
    _cols(12), _cols(13),
    _cols(10), _cols(11),
    _cols(18), _cols(19),
    _cols(16), _cols(17),
    _cols(7), _cols(8), _cols(9), _rope_swap(_cols(9)),
])
N_MAIN = int(_MAIN_PERM.shape[0])
_SMALL_PERM = np.concatenate([_cols(5), _cols(6), _cols(14), _cols(15)])
N_SMALL = 128
A_LANE, B_LANE, I_LANE, F_LANE = 0, 8, 16, 24
OFF_GDN_QKV, OFF_GDN_Z, OFF_GATE = 0, 1536, 2048
OFF_MLSTM_V, OFF_MLSTM_O, OFF_MLSTM_Q, OFF_MLSTM_K = 6144, 6656, 7168, 7424
OFF_RET_V, OFF_RET_G, OFF_RET_Q, OFF_RET_K = 7680, 8192, 8704, 8960
OFF_MLA = 9216
MLA_IN_W = 768


def _cparams(sem):
    return pltpu.CompilerParams(dimension_semantics=sem, vmem_limit_bytes=VMEM_LIMIT)


def _rms(x):
    return x * lax.rsqrt(jnp.mean(x * x, axis=-1, keepdims=True) + EPS)


def _dot(a, b):
    return jnp.dot(a, b, preferred_element_type=F32)


def _dot_nt(a, b):
    return lax.dot_general(a, b, (((1,), (1,)), ((), ())), preferred_element_type=F32)


def _mod_kernel(c_ref, w_ref, b_ref, o_ref):
    c = c_ref[...]
    s = c * jax.nn.sigmoid(c)
    o_ref[...] = jnp.dot(s, w_ref[...], preferred_element_type=F32) + b_ref[...]


def modulation_vectors(cond, w_mod, b_mod):
    n = w_mod.shape[1]
    tn = 1536
    return pl.pallas_call(
        _mod_kernel,
        out_shape=jax.ShapeDtypeStruct((8, n), F32),
        grid=(n // tn,),
        in_specs=[pl.BlockSpec((8, D_MODEL), lambda j: (0, 0)),
                  pl.BlockSpec((D_MODEL, tn), lambda j: (0, j)),
                  pl.BlockSpec((1, tn), lambda j: (0, j))],
        out_specs=pl.BlockSpec((8, tn), lambda j: (0, j)),
        compiler_params=_cparams(("arbitrary",)),
        name="modulation",
    )(cond, w_mod, b_mod.reshape(1, n))


def _norm_proj_kernel(x_ref, pv_ref, w_ref, o_ref, h_ref):
    @pl.when(pl.program_id(1) == 0)
    def _():
        pv = pv_ref[...]
        h_ref[...] = (_rms(x_ref[...]) * pv[0:1] * pv[1:2] + pv[2:3]).astype(BF16)

    o_ref[...] = _dot(h_ref[...], w_ref[...]).astype(o_ref.dtype)


def norm_proj(x, pv, w, tn, out_dtype):
    m, d = x.shape
    n = w.shape[1]
    tm = min(m, 1024)
    return pl.pallas_call(
        _norm_proj_kernel,
        out_shape=jax.ShapeDtypeStruct((m, n), out_dtype),
        grid=(m // tm, n // tn),
        in_specs=[pl.BlockSpec((tm, d), lambda i, j: (i, 0)),
                  pl.BlockSpec((8, d), lambda i, j: (0, 0)),
                  pl.BlockSpec((d, tn), lambda i, j: (0, j))],
        out_specs=pl.BlockSpec((tm, tn), lambda i, j: (i, j)),
        scratch_shapes=[pltpu.VMEM((tm, d), BF16)],
        compiler_params=_cparams(("parallel", "arbitrary")),
        name="norm_proj",
    )(x, pv, w)


def _mla_proj_kernel(z_ref, cos_ref, sin_ref, gq_ref, gkv_ref, wq_ref, wkv_ref, q_ref, k_ref, v_ref):
    z = z_ref[...].astype(F32)
    tm = z.shape[0]
    cq = z[:, :MLA_Q_RANK]
    ckv = z[:, MLA_Q_RANK:MLA_Q_RANK + MLA_KV_RANK]
    kr = z[:, MLA_Q_RANK + MLA_KV_RANK:]
    cos = cos_ref[...]
    sin = sin_ref[...]
    qn = (_rms(cq) * gq_ref[...]).astype(BF16)
    kvn = (_rms(ckv) * gkv_ref[...]).astype(BF16)
    qf = _dot(qn, wq_ref[...]) * (MLA_QK ** -0.5 * math.log2(math.e))
    kvf = _dot(kvn, wkv_ref[...])
    kr_rot = kr[:, :MLA_ROPE] * cos + kr[:, MLA_ROPE:] * sin
    pad = jnp.zeros((tm, MLA_QK_PAD - MLA_QK), F32)
    lane = lax.broadcasted_iota(jnp.int32, (tm, MLA_V_PAD - HEAD_V), 1)
    ones_col = jnp.where(lane == 0, 1.0, 0.0).astype(BF16)
    for h in range(N_HEADS):
        b = h * 256
        q_rot = qf[:, b + 128:b + 192] * cos + qf[:, b + 192:b + 256] * sin
        q_ref[h] = jnp.concatenate([qf[:, b:b + 128], q_rot, pad], axis=-1).astype(BF16)
        k_ref[h] = jnp.concatenate([kvf[:, b:b + 128], kr_rot, pad], axis=-1).astype(BF16)
        v_ref[h] = jnp.concatenate([kvf[:, b + 128:b + 256].astype(BF16), ones_col], axis=-1)


def mla_project(zmain, cos, sin, gq, gkv, wq_ext, wkv):
    m = zmain.shape[0]
    tm = min(m, 1024)
    full = lambda shape: pl.BlockSpec(shape, lambda i: tuple(0 for _ in shape))
    return pl.pallas_call(
        _mla_proj_kernel,
        out_shape=(jax.ShapeDtypeStruct((N_HEADS, m, MLA_QK_PAD), BF16),
                   jax.ShapeDtypeStruct((N_HEADS, m, MLA_QK_PAD), BF16),
                   jax.ShapeDtypeStruct((N_HEADS, m, MLA_V_PAD), BF16)),
        grid=(m // tm,),
        in_specs=[pl.BlockSpec((tm, MLA_IN_W), lambda i: (i, OFF_MLA // MLA_IN_W)),
                  pl.BlockSpec((tm, MLA_ROPE), lambda i: (i, 0)),
                  pl.BlockSpec((tm, MLA_ROPE), lambda i: (i, 0)),
                  full((1, MLA_Q_RANK)), full((1, MLA_KV_RANK)),
                  full((MLA_Q_RANK, N_HEADS * 256)), full((MLA_KV_RANK, N_HEADS * 256))],
        out_specs=(pl.BlockSpec((N_HEADS, tm, MLA_QK_PAD), lambda i: (0, i, 0)),
                   pl.BlockSpec((N_HEADS, tm, MLA_QK_PAD), lambda i: (0, i, 0)),
                   pl.BlockSpec((N_HEADS, tm, MLA_V_PAD), lambda i: (0, i, 0))),
        compiler_params=_cparams(("parallel",)),
        name="mla_project",
    )(zmain, cos, sin, gq, gkv, wq_ext, wkv)


ATTN_TQ, ATTN_TK = 512, 512
ATTN_UNROLL = 32


def _attn_kernel(*refs, segs):
    q_ref, o_ref = refs[0], refs[-1]
    q = q_ref[0]
    tq = q.shape[0]
    carry = (jnp.full((tq, 1), -1e30, F32), jnp.zeros((tq, MLA_V_PAD), F32))
    for si, (tk, nk) in enumerate(segs):
        k_ref, v_ref = refs[1 + 2 * si], refs[2 + 2 * si]

        def body(c, carry, k_ref=k_ref, v_ref=v_ref, tk=tk):
            m, acc = carry
            start = pl.multiple_of(c * tk, tk)
            s = _dot_nt(q, k_ref[0, pl.ds(start, tk), :])
            m_new = jnp.maximum(m, jnp.max(s, axis=-1, keepdims=True))
            p = jnp.exp2(s - m_new).astype(BF16)
            acc = jnp.exp2(m - m_new) * acc + _dot(p, v_ref[0, pl.ds(start, tk), :])
            return m_new, acc

        carry = lax.fori_loop(0, nk, body, carry, unroll=min(ATTN_UNROLL, nk))
    _, acc = carry
    o_ref[...] = (acc[:, :HEAD_V] / acc[:, HEAD_V:HEAD_V + 1]).astype(o_ref.dtype)


def attention(q, kvs):
    _, lq, _ = q.shape
    tq = min(lq, ATTN_TQ)
    segs, args, specs = [], [], []
    for k, v in kvs:
        lk = k.shape[1]
        tk = min(lk, ATTN_TK)
        segs.append((tk, lk // tk))
        args += [k, v]
        specs += [pl.BlockSpec((1, lk, MLA_QK_PAD), lambda h, i: (h, 0, 0)),
                  pl.BlockSpec((1, lk, MLA_V_PAD), lambda h, i: (h, 0, 0))]
    return pl.pallas_call(
        functools.partial(_attn_kernel, segs=tuple(segs)),
        out_shape=jax.ShapeDtypeStruct((lq, N_HEADS * HEAD_V), BF16),
        grid=(N_HEADS, lq // tq),
        in_specs=[pl.BlockSpec((1, tq, MLA_QK_PAD), lambda h, i: (h, i, 0))] + specs,
        out_specs=pl.BlockSpec((tq, HEAD_V), lambda h, i: (i, h)),
        compiler_params=_cparams(("parallel", "arbitrary")),
        name="attention",
    )(q, *args)


def _head_post(o, gate, gain, centre, silu_gate):
    sig = jax.nn.sigmoid(gate)
    act = gate * sig if silu_gate else sig
    outs = []
    for h in range(N_HEADS):
        sl = slice(h * HEAD_V, (h + 1) * HEAD_V)
        oh = o[:, sl]
        if centre:
            oh = oh - jnp.mean(oh, axis=-1, keepdims=True)
        oh = oh * lax.rsqrt(jnp.mean(oh * oh, axis=-1, keepdims=True) + EPS)
        outs.append((oh * gain[:, sl] * act[:, sl]).astype(BF16))
    return jnp.concatenate(outs, axis=1)


def _merge_kernel(x_ref, g0_ref, g1_ref, g2_ref, g3_ref, af_ref, ab_ref, b_ref, cf_ref, cb_ref, df_ref, db_ref,
                  za_ref, zc_ref, zd_ref, gain_ref, wb_ref, wo_ref, g_ref, o_ref):
    gain = gain_ref[...]
    branches = (
        _head_post(af_ref[...] + ab_ref[...], za_ref[...].astype(F32), gain[0:1], False, True),
        b_ref[...],
        _head_post(cf_ref[...] + cb_ref[...], zc_ref[...].astype(F32), gain[1:2], False, False),
        _head_post(df_ref[...] + db_ref[...], zd_ref[...].astype(F32), gain[2:3], True, True),
    )
    s = None
    for n, (br, gate_ref) in enumerate(zip(branches, (g0_ref, g1_ref, g2_ref, g3_ref))):
        proj = _dot(br, wb_ref[n])
        gate = jax.nn.sigmoid(gate_ref[...].astype(F32))
        s = gate * proj if s is None else s + gate * proj
    m = _dot(s.astype(BF16), wo_ref[...])
    o_ref[...] = x_ref[...] + g_ref[0:1] * m


def merge_branches(x, zmain, gdn, attn, mlstm, ret, gains, w_branch, w_out, gvec):
    m = x.shape[0]
    tm = min(m, 512)
    row = lambda w: pl.BlockSpec((tm, w), lambda i: (i, 0))
    zcol = lambda off: pl.BlockSpec((tm, BRANCH_W), lambda i: (i, off // BRANCH_W))
    return pl.pallas_call(
        _merge_kernel,
        out_shape=jax.ShapeDtypeStruct((m, D_MODEL), F32),
        grid=(m // tm,),
        in_specs=[row(D_MODEL)]
                 + [pl.BlockSpec((tm, D_MODEL), lambda i, n=n: (i, OFF_GATE // D_MODEL + n)) for n in range(N_BRANCH)]
                 + [row(BRANCH_W)] * 7
                 + [zcol(OFF_GDN_Z), zcol(OFF_MLSTM_O), zcol(OFF_RET_G),
                    pl.BlockSpec((8, BRANCH_W), lambda i: (0, 0)),
                    pl.BlockSpec((N_BRANCH, BRANCH_W, D_MODEL), lambda i: (0, 0, 0)),
                    pl.BlockSpec((D_MODEL, D_MODEL), lambda i: (0, 0)),
                    pl.BlockSpec((8, D_MODEL), lambda i: (0, 0))],
        out_specs=row(D_MODEL),
        compiler_params=pltpu.CompilerParams(dimension_semantics=("parallel",), vmem_limit_bytes=MERGE_VMEM_LIMIT),
        name="merge_branches",
    )(x, zmain, zmain, zmain, zmain, gdn[0], gdn[1], attn, mlstm[0], mlstm[1], ret[0], ret[1],
      zmain, zmain, zmain, gains, w_branch, w_out, gvec)


def _ffn_kernel(x_ref, pv_ref, wg_ref, wu_ref, wd_ref, o_ref, h_ref, acc_ref):
    f = pl.program_id(1)

    @pl.when(f == 0)
    def _():
        pv = pv_ref[...]
        h_ref[...] = (_rms(x_ref[...]) * pv[0:1] * pv[1:2] + pv[2:3]).astype(BF16)
        acc_ref[...] = jnp.zeros_like(acc_ref)

    h = h_ref[...]
    g = _dot(h, wg_ref[...])
    u = _dot(h, wu_ref[...])
    act = (g * jax.nn.sigmoid(g) * u).astype(BF16)
    acc_ref[...] += _dot(act, wd_ref[...])

    @pl.when(f == pl.num_programs(1) - 1)
    def _():
        o_ref[...] = x_ref[...] + pv_ref[3:4] * acc_ref[...]


def dense_ffn(x, pv, w_gu, w_down):
    m = x.shape[0]
    tm = min(m, 1024)
    tf = 512
    nf = D_FF // tf
    return pl.pallas_call(
        _ffn_kernel,
        out_shape=jax.ShapeDtypeStruct((m, D_MODEL), F32),
        grid=(m // tm, nf),
        in_specs=[pl.BlockSpec((tm, D_MODEL), lambda i, f: (i, 0)),
                  pl.BlockSpec((8, D_MODEL), lambda i, f: (0, 0)),
                  pl.BlockSpec((D_MODEL, tf), lambda i, f: (0, f)),
                  pl.BlockSpec((D_MODEL, tf), lambda i, f: (0, f + nf)),
                  pl.BlockSpec((tf, D_MODEL), lambda i, f: (f, 0))],
        out_specs=pl.BlockSpec((tm, D_MODEL), lambda i, f: (i, 0)),
        scratch_shapes=[pltpu.VMEM((tm, D_MODEL), BF16), pltpu.VMEM((tm, D_MODEL), F32)],
        compiler_params=_cparams(("parallel", "arbitrary")),
        name="dense_ffn",
    )(x, pv, w_gu, w_gu, w_down)


def _router_kernel(x_ref, pv_ref, w_ref, b_ref, h_ref, route_ref):
    pv = pv_ref[...]
    h = _rms(x_ref[...]) * pv[0:1] * pv[1:2] + pv[2:3]
    h_hi = h.astype(BF16)
    h_lo = (h - h_hi.astype(F32)).astype(BF16)
    w = w_ref[...]
    w_hi = w.astype(BF16)
    w_lo = (w - w_hi.astype(F32)).astype(BF16)
    logits = _dot(h_hi, w_hi) + _dot(h_hi, w_lo) + _dot(h_lo, w_hi) + b_ref[...]
    h_ref[...] = h
    lane = lax.broadcasted_iota(jnp.int32, logits.shape, 1).astype(F32)
    m1 = jnp.max(logits, axis=1, keepdims=True)
    e1 = jnp.min(jnp.where(logits == m1, lane, 128.0), axis=1, keepdims=True)
    rest = jnp.where(lane == e1, -jnp.inf, logits)
    m2 = jnp.max(rest, axis=1, keepdims=True)
    e2 = jnp.min(jnp.where(rest == m2, lane, 128.0), axis=1, keepdims=True)
    z2 = jnp.exp(m2 - m1)
    w1 = 1.0 / (1.0 + z2)
    route_ref[...] = jnp.where(lane == 0.0, w1, jnp.where(lane == 1.0, z2 * w1, jnp.where(
        lane == 2.0, e1, jnp.where(lane == 3.0, e2, 0.0))))


def moe_router(x, pv, w_router_pad, b_router_pad):
    m = x.shape[0]
    tm = min(m, 1024)
    return pl.pallas_call(
        _router_kernel,
        out_shape=(jax.ShapeDtypeStruct((m, D_MODEL), F32), jax.ShapeDtypeStruct((m, 128), F32)),
        grid=(m // tm,),
        in_specs=[pl.BlockSpec((tm, D_MODEL), lambda i: (i, 0)),
                  pl.BlockSpec((8, D_MODEL), lambda i: (0, 0)),
                  pl.BlockSpec((D_MODEL, 128), lambda i: (0, 0)),
                  pl.BlockSpec((1, 128), lambda i: (0, 0))],
        out_specs=(pl.BlockSpec((tm, D_MODEL), lambda i: (i, 0)), pl.BlockSpec((tm, 128), lambda i: (i, 0))),
        compiler_params=_cparams(("parallel",)),
        name="moe_router",
    )(x, pv, w_router_pad, b_router_pad)


def _expert_kernel(be_ref, nb_ref, x_ref, wg_ref, wu_ref, wd_ref, o_ref, acc_ref):
    b = pl.program_id(0)
    f = pl.program_id(1)

    @pl.when(b < nb_ref[0])
    def _():
        @pl.when(f == 0)
        def _():
            acc_ref[...] = jnp.zeros_like(acc_ref)

        x = x_ref[...].astype(BF16)
        g = _dot(x, wg_ref[0].astype(BF16))
        u = _dot(x, wu_ref[0].astype(BF16))
        act = (g * jax.nn.sigmoid(g) * u).astype(BF16)
        acc_ref[...] += _dot(act, wd_ref[0].astype(BF16))

        @pl.when(f == pl.num_programs(1) - 1)
        def _():
            o_ref[...] = acc_ref[...]

    @pl.when((b >= nb_ref[0]) & (f == pl.num_programs(1) - 1))
    def _():
        o_ref[...] = jnp.zeros_like(o_ref)


def expert_ffn(block_e, n_used, xb, w_gu, w_down):
    cap = xb.shape[0]
    nb = cap // MOE_ROWS
    tf = 512
    nf = D_FF // tf

    def live(b, nbr):
        return jnp.minimum(b, nbr[0] - 1)

    def fsel(b, f, nbr):
        return jnp.where(b < nbr[0], f, nf - 1)

    grid_spec = pltpu.PrefetchScalarGridSpec(
        num_scalar_prefetch=2,
        grid=(nb, nf),
        in_specs=[pl.BlockSpec((MOE_ROWS, D_MODEL), lambda b, f, be, nbr: (live(b, nbr), 0)),
                  pl.BlockSpec((1, D_MODEL, tf), lambda b, f, be, nbr: (be[live(b, nbr)], 0, fsel(b, f, nbr))),
                  pl.BlockSpec((1, D_MODEL, tf), lambda b, f, be, nbr: (be[live(b, nbr)], 0, fsel(b, f, nbr) + nf)),
                  pl.BlockSpec((1, tf, D_MODEL), lambda b, f, be, nbr: (be[live(b, nbr)], fsel(b, f, nbr), 0))],
        out_specs=pl.BlockSpec((MOE_ROWS, D_MODEL), lambda b, f, be, nbr: (b, 0)),
        scratch_shapes=[pltpu.VMEM((MOE_ROWS, D_MODEL), F32)],
    )
    return pl.pallas_call(
        _expert_kernel,
        out_shape=jax.ShapeDtypeStruct((cap, D_MODEL), F32),
        grid_spec=grid_spec,
        compiler_params=_cparams(("arbitrary", "arbitrary")),
        name="expert_ffn",
    )(block_e, n_used, xb, w_gu, w_gu, w_down)


def _final_kernel(x_ref, y0_ref, y1_ref, w_ref, pv_ref, o_ref):
    w = w_ref[...]
    f = w[:, 0:1] * y0_ref[...] + w[:, 1:2] * y1_ref[...]
    x = x_ref[...] + pv_ref[0:1] * f
    o_ref[...] = _rms(x) * pv_ref[1:2]


def final_combine(x, y2, w, pv):
    m = x.shape[0]
    tm = min(m, 1024)
    row = lambda wd: pl.BlockSpec((tm, wd), lambda i: (i, 0))
    return pl.pallas_call(
        _final_kernel,
        out_shape=jax.ShapeDtypeStruct((m, D_MODEL), F32),
        grid=(m // tm,),
        in_specs=[row(D_MODEL), row(D_MODEL), pl.BlockSpec((tm, D_MODEL), lambda i: (i + m // tm, 0)), row(128),
                  pl.BlockSpec((8, D_MODEL), lambda i: (0, 0))],
        out_specs=row(D_MODEL),
        compiler_params=_cparams(("parallel",)),
        name="final_combine",
    )(x, y2, y2, w, pv)


CH = 128
NEG = -1e30


def _masks(reverse):
    r = lax.broadcasted_iota(jnp.int32, (CH, CH), 0)
    c = lax.broadcasted_iota(jnp.int32, (CH, CH), 1)
    return (r <= c, r < c) if reverse else (r >= c, r > c)


def _cumsum_time(incl, x):
    m = jnp.where(incl, 1.0, 0.0).astype(BF16)
    hi = x.astype(BF16)
    lo = (x - hi.astype(F32)).astype(BF16)
    return _dot(m, hi) + _dot(m, lo)


SCAN_SUB = 4


def _scan_grid(l):
    rows = min(l, SCAN_SUB * CH)
    return l // rows, rows


def _sweep(d, nsub):
    return list(range(nsub)) if d == 0 else list(range(nsub - 1, -1, -1))


def _dir_specs(nsteps, rows, width, col_block):
    return [pl.BlockSpec((rows, width), lambda n: (n, col_block)),
            pl.BlockSpec((rows, width), lambda n: (nsteps - 1 - n, col_block))]


def _whole(a):
    return pl.BlockSpec(a.shape, lambda n: tuple(0 for _ in a.shape))


def _ret_consts():
    log_gamma = np.log(1.0 - 2.0 ** (-5.0 - np.arange(N_HEADS, dtype=np.float64)))
    pos = np.arange(CH, dtype=np.float64)
    diff = pos[:, None] - pos[None, :]
    dec_f = np.where(diff >= 0, np.exp(log_gamma[:, None, None] * diff), 0.0)
    dec = np.stack([dec_f, np.transpose(dec_f, (0, 2, 1))])
    qs_f = np.exp(log_gamma[None, :] * (pos[:, None] + 1.0))
    ks_f = np.exp(log_gamma[None, :] * (CH - 1.0 - pos[:, None]))
    qs_b = np.exp(log_gamma[None, :] * (CH - pos[:, None]))
    ks_b = np.exp(log_gamma[None, :] * pos[:, None])
    rep = lambda a: np.repeat(a, RET_DK, axis=1)
    qs = np.stack([rep(qs_f), rep(qs_b)])
    ks = np.stack([rep(ks_f), rep(ks_b)])
    chunk_decay = [float(np.exp(lg * CH)) for lg in log_gamma]
    return (jnp.asarray(dec, F32), jnp.asarray(qs, F32), jnp.asarray(ks, F32)), chunk_decay


def _ret_kernel(*refs, rotary, chunk_decay):
    if rotary:
        (qf, qb, kf, kb, vf, vb, cosf, cosb, sinf, sinb, dec_ref, qs_ref, ks_ref, s0_ref,
         of_ref, ob_ref, s_ref) = refs
        tabs = ((cosf, sinf), (cosb, sinb))
    else:
        qf, qb, kf, kb, vf, vb, dec_ref, qs_ref, ks_ref, s0_ref, of_ref, ob_ref, s_ref = refs
        tabs = (None, None)

    @pl.when(pl.program_id(0) == 0)
    def _():
        s_ref[...] = s0_ref[...]

    lane = lax.broadcasted_iota(jnp.int32, (CH, N_HEADS * RET_DK), 1)
    first_half = (lane & (RET_DK - 1)) < RET_DK // 2

    nsub = qf.shape[0] // CH
    chains = {}
    for d, (q_ref, k_ref, v_ref, o_ref) in enumerate(((qf, kf, vf, of_ref), (qb, kb, vb, ob_ref))):
        for j in range(nsub):
            rows = slice(j * CH, (j + 1) * CH)
            q = q_ref[rows, :].astype(F32)
            k = k_ref[rows, :].astype(F32) * RET_DK ** -0.5
            if rotary:
                cos, sin = tabs[d][0][rows, :], tabs[d][1][rows, :]

                def rot(x, cos=cos, sin=sin):
                    swapped = jnp.where(first_half, pltpu.roll(x, N_HEADS * RET_DK - RET_DK // 2, 1),
                                        pltpu.roll(x, RET_DK // 2, 1))
                    return x * cos + swapped * sin

                q, k = rot(q), rot(k)
            qb16, kb16 = q.astype(BF16), k.astype(BF16)
            q_in = (q * qs_ref[d]).astype(BF16)
            k_out = k * ks_ref[d]
            k_t = [k_out[:, :128].T, k_out[:, 128:].T]
            for h in range(N_HEADS):
                sl = slice(h * RET_DK, (h + 1) * RET_DK)
                chains[d, j, h] = dict(
                    o_ref=o_ref, rows=rows, vh=v_ref[rows, h * HEAD_V:(h + 1) * HEAD_V], q=qb16[:, sl], k=kb16[:, sl],
                    q_in=q_in[:, sl], k_th=k_t[h // 2][(h % 2) * RET_DK:(h % 2 + 1) * RET_DK, :].astype(BF16))

    for (d, j, h), c in chains.items():
        c["p"] = (_dot_nt(c["q"], c["k"]) * dec_ref[d, h]).astype(BF16)
    for c in chains.values():
        c["intra"] = _dot(c["p"], c["vh"])
        c["update"] = _dot(c["k_th"], c["vh"])
    states = {(d, h): s_ref[d, h] for d in range(2) for h in range(N_HEADS)}
    for i in range(nsub):
        for (d, h), s in list(states.items()):
            c = chains[d, _sweep(d, nsub)[i], h]
            c["o_ref"][c["rows"], h * HEAD_V:(h + 1) * HEAD_V] = c["intra"] + _dot(c["q_in"], s.astype(BF16))
            states[d, h] = chunk_decay[h] * s + c["update"]
    for (d, h), s in states.items():
        s_ref[d, h] = s


def retention_scan(z, s0, tables):
    l = z.shape[0]
    nsteps, rows = _scan_grid(l)
    consts, chunk_decay = _ret_consts()
    rotary = tables is not None
    w = N_HEADS * RET_DK
    specs = (_dir_specs(nsteps, rows, w, OFF_RET_Q // w) + _dir_specs(nsteps, rows, w, OFF_RET_K // w)
             + _dir_specs(nsteps, rows, BRANCH_W, OFF_RET_V // BRANCH_W))
    args = [z] * 6
    if rotary:
        specs += _dir_specs(nsteps, rows, w, 0) + _dir_specs(nsteps, rows, w, 0)
        args += [tables[0], tables[0], tables[1], tables[1]]
    specs += [_whole(c) for c in consts] + [_whole(s0)]
    args += list(consts) + [s0]
    return pl.pallas_call(
        functools.partial(_ret_kernel, rotary=rotary, chunk_decay=chunk_decay),
        out_shape=(jax.ShapeDtypeStruct((l, BRANCH_W), F32), jax.ShapeDtypeStruct((l, BRANCH_W), F32),
                   jax.ShapeDtypeStruct(s0.shape, F32)),
        grid=(nsteps,),
        in_specs=specs,
        out_specs=tuple(_dir_specs(nsteps, rows, BRANCH_W, 0)) + (_whole(s0),),
        compiler_params=_cparams(("arbitrary",)),
        name="retention_scan",
    )(*args)


def _mlstm_kernel(qf, qb, kf, kb, vf, vb, smf, smb, bias_ref, c0_ref, m0_ref, of_ref, ob_ref, c_ref, m_ref):
    @pl.when(pl.program_id(0) == 0)
    def _():
        c_ref[...] = c0_ref[...]
        m_ref[...] = m0_ref[...]

    lane = lax.broadcasted_iota(jnp.int32, (CH, 128), 1)
    is_forget = (lane >= F_LANE) & (lane < F_LANE + 2 * N_HEADS)
    ones_col = jnp.where(lane == 0, 1.0, 0.0).astype(BF16)

    nsub = qf.shape[0] // CH
    chains = {}
    for d, (q_ref, k_ref, v_ref, sm_ref, o_ref) in enumerate(((qf, kf, vf, smf, of_ref), (qb, kb, vb, smb, ob_ref))):
        reverse = d == 1
        incl, _ = _masks(reverse)
        last = 0 if reverse else CH - 1
        for j in range(nsub):
            rows = slice(j * CH, (j + 1) * CH)
            pre = sm_ref[rows, :] + bias_ref[0:1]
            x = jnp.where(is_forget, jax.nn.log_sigmoid(pre), pre)
            b = _cumsum_time(incl, x)
            x_t, b_t = x.T, b.T
            q = (q_ref[rows, :].astype(F32) * MLSTM_DK ** -0.5).astype(BF16)
            kf32 = k_ref[rows, :].astype(F32)
            k_t = [kf32[:, :128].T, kf32[:, 128:].T]
            for h in range(N_HEADS):
                gi, gf = I_LANE + d * N_HEADS + h, F_LANE + d * N_HEADS + h
                sl = slice(h * MLSTM_DK, (h + 1) * MLSTM_DK)
                b_col, b_row = b[:, gf:gf + 1], b_t[gf:gf + 1, :]
                li_row = x_t[gi:gi + 1, :]
                b_last = b[last:last + 1, gf:gf + 1]
                d_log = jnp.where(incl, b_col - b_row + li_row, NEG)
                m_intra = jnp.max(d_log, axis=1, keepdims=True)
                e_row = b_last - b_row + li_row
                chains[d, j, h] = dict(
                    o_ref=o_ref, rows=rows, q=q[:, sl], k=k_ref[rows, sl], b_col=b_col, b_last=b_last, m_intra=m_intra,
                    e_row=e_row, m_end=jnp.max(e_row, axis=1, keepdims=True),
                    w_intra=jnp.exp(d_log - m_intra),
                    k_th=k_t[h // 2][(h % 2) * MLSTM_DK:(h % 2 + 1) * MLSTM_DK, :],
                    v_aug=jnp.concatenate([v_ref[rows, h * HEAD_V:(h + 1) * HEAD_V], ones_col], axis=1))

    for c in chains.values():
        c["p"] = (c["w_intra"] * _dot_nt(c["q"], c["k"])).astype(BF16)
    for c in chains.values():
        c["pv"] = _dot(c["p"], c["v_aug"])
    c_augs = {(d, h): c_ref[d, h] for d in range(2) for h in range(N_HEADS)}
    m_ss = {(d, h): m_ref[d, h][0:1, 0:1] for d in range(2) for h in range(N_HEADS)}
    for i in range(nsub):
        cur = {(d, h): chains[d, _sweep(d, nsub)[i], h] for (d, h) in c_augs}
        qcs = {key: _dot(c["q"], c_augs[key].astype(BF16)) for key, c in cur.items()}
        m_news = {key: jnp.maximum(c["b_last"] + m_ss[key], c["m_end"]) for key, c in cur.items()}
        updates = {key: _dot((c["k_th"] * jnp.exp(c["e_row"] - m_news[key])).astype(BF16), c["v_aug"])
                   for key, c in cur.items()}
        for (d, h), c in cur.items():
            m_s = m_ss[d, h]
            m_t = jnp.maximum(c["b_col"] + m_s, c["m_intra"])
            inter = jnp.exp(c["b_col"] + m_s - m_t)
            r = jnp.exp(c["m_intra"] - m_t)
            tot = inter * qcs[d, h] + r * c["pv"]
            den = jnp.maximum(jnp.abs(tot[:, HEAD_V:HEAD_V + 1]), jnp.exp(-m_t))
            c["o_ref"][c["rows"], h * HEAD_V:(h + 1) * HEAD_V] = tot[:, :HEAD_V] / den
        for key, c in cur.items():
            c_augs[key] = jnp.exp(c["b_last"] + m_ss[key] - m_news[key]) * c_augs[key] + updates[key]
            m_ss[key] = m_news[key]
    for (d, h) in c_augs:
        c_ref[d, h] = c_augs[d, h]
        m_ref[d, h] = jnp.broadcast_to(m_ss[d, h], (8, 128))


def mlstm_scan(z, small, bias, c0, m0):
    l = z.shape[0]
    nsteps, rows = _scan_grid(l)
    w = N_HEADS * MLSTM_DK
    specs = (_dir_specs(nsteps, rows, w, OFF_MLSTM_Q // w) + _dir_specs(nsteps, rows, w, OFF_MLSTM_K // w)
             + _dir_specs(nsteps, rows, BRANCH_W, OFF_MLSTM_V // BRANCH_W) + _dir_specs(nsteps, rows, 128, 0))
    specs += [_whole(bias), _whole(c0), _whole(m0)]
    return pl.pallas_call(
        _mlstm_kernel,
        out_shape=(jax.ShapeDtypeStruct((l, BRANCH_W), F32), jax.ShapeDtypeStruct((l, BRANCH_W), F32),
                   jax.ShapeDtypeStruct(c0.shape, F32), jax.ShapeDtypeStruct(m0.shape, F32)),
        grid=(nsteps,),
        in_specs=specs,
        out_specs=tuple(_dir_specs(nsteps, rows, BRANCH_W, 0)) + (_whole(c0), _whole(m0)),
        compiler_params=_cparams(("arbitrary",)),
        name="mlstm_scan",
    )(z, z, z, z, z, z, small, small, bias, c0, m0)


QKV_W = 3 * N_HEADS * GDN_DK
HALO = 8


def _gdn_prep_kernel(x_ref, prev_ref, next_ref, w_ref, q_ref, k_ref, v_ref):
    i = pl.program_id(0)
    tm = x_ref.shape[0]
    x = x_ref[...].astype(F32)
    prev = jnp.where(i > 0, prev_ref[...].astype(F32), 0.0)
    nxt = jnp.where(i < pl.num_programs(0) - 1, next_ref[...].astype(F32), 0.0)
    xe = jnp.concatenate([prev, x, nxt], axis=0)
    w = w_ref[...]
    y = None
    for tap in range(CONV_K):
        off = HALO + tap - CONV_K // 2
        term = w[tap:tap + 1] * xe[off:off + tm]
        y = term if y is None else y + term
    y = y * jax.nn.sigmoid(y)
    hw = N_HEADS * GDN_DK
    for h in range(N_HEADS):
        sl = slice(h * GDN_DK, (h + 1) * GDN_DK)
        qh = y[:, sl]
        kh = y[:, hw + h * GDN_DK:hw + (h + 1) * GDN_DK]
        q_ref[:, sl] = (qh * lax.rsqrt(jnp.sum(qh * qh, axis=-1, keepdims=True) + EPS) * GDN_DK ** -0.5).astype(BF16)
        k_ref[:, sl] = (kh * lax.rsqrt(jnp.sum(kh * kh, axis=-1, keepdims=True) + EPS)).astype(BF16)
    v_ref[...] = y[:, 2 * hw:].astype(BF16)


def gdn_prep(z, conv_w):
    l = z.shape[0]
    tm = min(l, 256)
    nb = l // tm
    r8 = tm // HALO
    cb = OFF_GDN_QKV // QKV_W
    w8 = jnp.concatenate([conv_w.astype(F32), jnp.zeros((8 - CONV_K, QKV_W), F32)], axis=0)
    return pl.pallas_call(
        _gdn_prep_kernel,
        out_shape=tuple(jax.ShapeDtypeStruct((l, BRANCH_W), BF16) for _ in range(3)),
        grid=(nb,),
        in_specs=[pl.BlockSpec((tm, QKV_W), lambda i: (i, cb)),
                  pl.BlockSpec((HALO, QKV_W), lambda i: (jnp.maximum(i * r8 - 1, 0), cb)),
                  pl.BlockSpec((HALO, QKV_W), lambda i: (jnp.minimum((i + 1) * r8, nb * r8 - 1), cb)),
                  pl.BlockSpec((8, QKV_W), lambda i: (0, 0))],
        out_specs=tuple(pl.BlockSpec((tm, BRANCH_W), lambda i: (i, 0)) for _ in range(3)),
        compiler_params=_cparams(("parallel",)),
        name="gdn_prep",
    )(z, z, z, w8)


N_LEVELS = 7


def _gdn_kernel(qf, qb, kf, kb, vf, vb, smf, smb, par_ref, s0_ref, of_ref, ob_ref, s_ref):
    @pl.when(pl.program_id(0) == 0)
    def _():
        s_ref[...] = s0_ref[...]

    lane = lax.broadcasted_iota(jnp.int32, (CH, 128), 1)
    is_decay = lane < B_LANE
    ri = lax.broadcasted_iota(jnp.int32, (CH, CH), 0)
    ci = lax.broadcasted_iota(jnp.int32, (CH, CH), 1)
    eye = jnp.where(ri == ci, 1.0, 0.0)
    pair_masks = [((ri >> (l + 1)) == (ci >> (l + 1))) & ((ri >> l) != (ci >> l)) for l in range(N_LEVELS)]

    nsub = qf.shape[0] // CH
    chains = {}
    for d, (q_ref, k_ref, v_ref, sm_ref, o_ref) in enumerate(((qf, kf, vf, smf, of_ref), (qb, kb, vb, smb, ob_ref))):
        reverse = d == 1
        incl, strict = _masks(reverse)
        last = 0 if reverse else CH - 1
        for j in range(nsub):
            rows = slice(j * CH, (j + 1) * CH)
            sm = sm_ref[rows, :]
            log_a = -jnp.exp(par_ref[1:2]) * jax.nn.softplus(sm + par_ref[0:1])
            x = jnp.where(is_decay, log_a, jax.nn.sigmoid(sm))
            g = _cumsum_time(incl, x)
            g_t = g.T
            for h in range(N_HEADS):
                ga, gb = A_LANE + d * N_HEADS + h, B_LANE + d * N_HEADS + h
                sl = slice(h * GDN_DK, (h + 1) * GDN_DK)
                g_col, g_row = g[:, ga:ga + 1], g_t[ga:ga + 1, :]
                g_last = g[last:last + 1, ga:ga + 1]
                beta = x[:, gb:gb + 1]
                decay = jnp.where(incl, jnp.exp(jnp.minimum(g_col - g_row, 0.0)), 0.0)
                kh = k_ref[rows, sl]
                kf32 = kh.astype(F32)
                kbeta = kf32 * beta
                a = jnp.where(strict, _dot_nt(kbeta.astype(BF16), kh) * decay, 0.0)
                e_g = jnp.exp(g_col)
                rhs = jnp.concatenate([v_ref[rows, sl].astype(F32) * beta, kbeta * e_g], axis=1)
                chains[d, j, h] = dict(
                    rows=rows, sl=sl, o_ref=o_ref, a=a, rhs=rhs, s_decay=jnp.exp(g_last),
                    qk=(_dot_nt(q_ref[rows, sl], kh) * decay).astype(BF16),
                    q_in=(q_ref[rows, sl].astype(F32) * e_g).astype(BF16),
                    k_out_t=(kf32.T * jnp.exp(g_last - g_row)).astype(BF16))

    cl = list(chains.values())
    xs = [eye - jnp.where(pair_masks[0], c["a"], 0.0) for c in cl]
    for pm in pair_masks[1:]:
        ys = [_dot(jnp.where(pm, c["a"], 0.0).astype(BF16), x.astype(BF16)) for c, x in zip(cl, xs)]
        xs = [x - _dot(x.astype(BF16), y.astype(BF16)) for x, y in zip(xs, ys)]
    for c, x in zip(cl, xs):
        c["sol"] = c["rhs"] + _dot((x - eye).astype(BF16), c["rhs"].astype(BF16))
    states = {(d, h): s_ref[d, h] for d in range(2) for h in range(N_HEADS)}
    for i in range(nsub):
        cur = {(d, h): chains[d, _sweep(d, nsub)[i], h] for (d, h) in states}
        s16s = {key: s.astype(BF16) for key, s in states.items()}
        v_news = {key: (c["sol"][:, :HEAD_V] - _dot(c["sol"][:, HEAD_V:].astype(BF16), s16s[key])).astype(BF16)
                  for key, c in cur.items()}
        updates = {key: _dot(c["k_out_t"], v_news[key]) for key, c in cur.items()}
        for key, c in cur.items():
            c["o_ref"][c["rows"], c["sl"]] = _dot(c["q_in"], s16s[key]) + _dot(c["qk"], v_news[key])
        for key, c in cur.items():
            states[key] = states[key] * c["s_decay"] + updates[key]
    for (d, h), s in states.items():
        s_ref[d, h] = s


def gdn_scan(qn, kn, vn, small, par, s0):
    l = qn.shape[0]
    nsteps, rows = _scan_grid(l)
    specs = _dir_specs(nsteps, rows, BRANCH_W, 0) * 3 + _dir_specs(nsteps, rows, 128, 0)
    specs += [_whole(par), _whole(s0)]
    return pl.pallas_call(
        _gdn_kernel,
        out_shape=(jax.ShapeDtypeStruct((l, BRANCH_W), F32), jax.ShapeDtypeStruct((l, BRANCH_W), F32),
                   jax.ShapeDtypeStruct(s0.shape, F32)),
        grid=(nsteps,),
        in_specs=specs,
        out_specs=tuple(_dir_specs(nsteps, rows, BRANCH_W, 0)) + (_whole(s0),),
        compiler_params=_cparams(("arbitrary",)),
        name="gdn_scan",
    )(qn, qn, kn, kn, vn, vn, small, small, par, s0)


def _lane_rows(entries):
    r = jnp.zeros((8, 128), F32)
    for row, lane, vals in entries:
        r = r.at[row, lane:lane + vals.shape[0]].set(vals.astype(F32))
    return r


def gdn_branch(zc, sc, zl, sl, conv_w, a_log, dt_bias):
    par = _lane_rows([(0, A_LANE, dt_bias.reshape(-1)), (1, A_LANE, a_log.reshape(-1))])
    s0 = jnp.zeros((2, N_HEADS, GDN_DK, HEAD_V), F32)
    ofc, obc, s1 = gdn_scan(*gdn_prep(zc, conv_w), sc, par, s0)
    ofl, obl, _ = gdn_scan(*gdn_prep(zl, conv_w), sl, par, s1)
    return (ofc, obc), (ofl, obl)


def mlstm_branch(zc, sc, zl, sl, gate_b):
    bias = _lane_rows([(0, I_LANE, gate_b[0].reshape(-1)), (0, F_LANE, gate_b[1].reshape(-1))])
    c0 = jnp.zeros((2, N_HEADS, MLSTM_DK, 2 * HEAD_V), F32)
    m0 = jnp.zeros((2, N_HEADS, 8, 128), F32)
    ofc, obc, c1, m1 = mlstm_scan(zc, sc, bias, c0, m0)
    ofl, obl, _, _ = mlstm_scan(zl, sl, bias, c1, m1)
    return (ofc, obc), (ofl, obl)


def retention_branch(zc, zl, tables):
    s0 = jnp.zeros((2, N_HEADS, RET_DK, HEAD_V), F32)
    ofc, obc, s1 = retention_scan(zc, s0, None)
    ofl, obl, _ = retention_scan(zl, s1, tables)
    return (ofc, obc), (ofl, obl)


def _ret_rope_tables(n_lat):
    inv = np.float32(ROPE_BASE) ** (-np.arange(0, RET_DK, 2, dtype=np.float32) / np.float32(RET_DK))
    ang = np.arange(n_lat, dtype=np.float32)[:, None] * inv[None, :]
    cos = np.concatenate([np.cos(ang), np.cos(ang)], axis=-1)
    sin = np.concatenate([-np.sin(ang), np.sin(ang)], axis=-1)
    return jnp.asarray(np.tile(cos, (1, N_HEADS)), F32), jnp.asarray(np.tile(sin, (1, N_HEADS)), F32)


def _rope_tables(n_lat):
    q = MLA_ROPE // 4
    inv = np.float32(ROPE_BASE) ** (-np.arange(0, 2 * q, 2, dtype=np.float32) / np.float32(2 * q))
    t = np.arange(n_lat)
    row = (t // GRID_W).astype(np.float32)[:, None] * inv[None, :]
    col = (t % GRID_W).astype(np.float32)[:, None] * inv[None, :]
    cos = np.concatenate([np.cos(row), np.cos(row), np.cos(col), np.cos(col)], axis=-1)
    sin = np.concatenate([-np.sin(row), np.sin(row), -np.sin(col), np.sin(col)], axis=-1)
    return jnp.asarray(cos, F32), jnp.asarray(sin, F32)


def _take_cols(w, perm):
    runs, start = [], 0
    for i in range(1, len(perm) + 1):
        if i == len(perm) or perm[i] != perm[i - 1] + 1:
            runs.append((int(perm[start]), int(perm[i - 1]) + 1))
            start = i
    return jnp.concatenate([w[:, a:b] for a, b in runs], axis=1)


def _pad_rows(v, n=8):
    rows = [jnp.reshape(r, (1, -1)).astype(F32) for r in v]
    d = rows[0].shape[1]
    return jnp.concatenate(rows + [jnp.zeros((n - len(rows), d), F32)], axis=0)


def _mla_weights(w_uq, w_ukv):
    wq = w_uq.reshape(MLA_Q_RANK, N_HEADS, MLA_QK)
    rope = wq[:, :, MLA_NOPE:]
    swapped = rope[:, :, _rope_swap(np.arange(MLA_ROPE))]
    wq_ext = jnp.concatenate([wq[:, :, :MLA_NOPE], rope, swapped], axis=-1).reshape(MLA_Q_RANK, N_HEADS * 256)
    return wq_ext.astype(BF16), w_ukv.astype(BF16)


def _moe(x_l, pv2, g2, w_router, b_router, w_gu, w_down, final_g):
    n = x_l.shape[0]
    w_pad = jnp.concatenate([w_router, jnp.zeros((D_MODEL, 128 - N_EXPERTS), F32)], axis=1)
    b_pad = jnp.concatenate([b_router, jnp.full((128 - N_EXPERTS,), -1e30, F32)]).reshape(1, 128)
    h2, route = moe_router(x_l, pv2, w_pad, b_pad)
    top_e = route[:, TOP_K:2 * TOP_K].astype(jnp.int32)
    n_assign = n * TOP_K
    flat_e = top_e.reshape(-1)
    onehot = (flat_e[:, None] == jnp.arange(N_EXPERTS)[None, :]).astype(jnp.int32)
    csum = jnp.cumsum(onehot, axis=0)
    counts = csum[-1]
    rank = jnp.take_along_axis(csum, flat_e[:, None], axis=1)[:, 0] - 1
    padded = (counts + MOE_ROWS - 1) // MOE_ROWS * MOE_ROWS
    pad_end = jnp.cumsum(padded)
    dest = (pad_end - padded)[flat_e] + rank
    nb = n_assign // MOE_ROWS + N_EXPERTS
    cap = nb * MOE_ROWS
    block_start = jnp.arange(nb, dtype=pad_end.dtype) * MOE_ROWS
    block_e = jnp.minimum(jnp.sum(block_start[:, None] >= pad_end[None, :], axis=1), N_EXPERTS - 1).astype(jnp.int32)
    n_used = (pad_end[-1] // MOE_ROWS).astype(jnp.int32).reshape(1)
    slot_tok = (jnp.arange(cap, dtype=jnp.int32) % n).at[dest].set(jnp.arange(n_assign, dtype=jnp.int32) // TOP_K)
    yb = expert_ffn(block_e, n_used, h2[slot_tok], w_gu, w_down)
    y2 = yb[dest.reshape(n, TOP_K).T.reshape(-1)]
    return final_combine(x_l, y2, route, _pad_rows([g2, final_g]))


def kernel(x, c, ctx, c_ctx, w_mod, b_mod, norm1_g, norm2_g, w_in, gdn_conv_w, gdn_a_log, gdn_dt_bias, gdn_norm_g, mla_q_norm_g, mla_kv_norm_g, mla_w_uq, mla_w_ukv, mlstm_gate_b, mlstm_norm_g, ret_norm_g, w_branch, w_out, ffn_w_in, ffn_w_down, moe_w_router, moe_b_router, moe_w_in, moe_w_down, final_norm_g):
    n_lat = x.shape[1]
    n_ctx = ctx.shape[1]
    x_l, x_c = x[0], ctx[0]
    cond = _pad_rows([c_ctx, c[0]])
    cos_l, sin_l = _rope_tables(n_lat)
    cos_c, sin_c = jnp.ones((n_ctx, MLA_ROPE), F32), jnp.zeros((n_ctx, MLA_ROPE), F32)
    ret_tables = _ret_rope_tables(n_lat)
    out = None
    for li in range(DEPTH):
        last = li == DEPTH - 1
        mod = modulation_vectors(cond, w_mod[li], b_mod[li])
        csh1, csc1, cg1, csh2, csc2, cg2 = jnp.split(mod[0], 6)
        sh1, sc1, g1, sh2, sc2, g2 = jnp.split(mod[1], 6)
        w_main = _take_cols(w_in[li], _MAIN_PERM).astype(BF16)
        w_small = _take_cols(w_in[li], _SMALL_PERM).astype(BF16)
        w_small = jnp.concatenate([w_small, jnp.zeros((D_MODEL, N_SMALL - w_small.shape[1]), BF16)], axis=1)
        pv_l = _pad_rows([norm1_g[li], 1 + sc1, sh1])
        pv_c = _pad_rows([norm1_g[li], 1 + csc1, csh1])
        zl = norm_proj(x_l, pv_l, w_main, 1664, BF16)
        zc = norm_proj(x_c, pv_c, w_main, 1664, BF16)
        sl = norm_proj(x_l, pv_l, w_small, 128, F32)
        sc = norm_proj(x_c, pv_c, w_small, 128, F32)

        a_c, a_l = gdn_branch(zc, sc, zl, sl, gdn_conv_w[li], gdn_a_log[li], gdn_dt_bias[li])
        c_c, c_l = mlstm_branch(zc, sc, zl, sl, mlstm_gate_b[li])
        d_c, d_l = retention_branch(zc, zl, ret_tables)
        gains = _pad_rows([gdn_norm_g[li], mlstm_norm_g[li], ret_norm_g[li]])

        wq_ext, wkv = _mla_weights(mla_w_uq[li], mla_w_ukv[li])
        gq, gkv = mla_q_norm_g[li].reshape(1, -1), mla_kv_norm_g[li].reshape(1, -1)
        ql, kl, vl = mla_project(zl, cos_l, sin_l, gq, gkv, wq_ext, wkv)
        qc, kc, vc = mla_project(zc, cos_c, sin_c, gq, gkv, wq_ext, wkv)
        b_l = attention(ql, [(kl, vl), (kc, vc)])

        wb = w_branch[li].astype(BF16)
        wo = w_out[li].astype(BF16)
        x_l = merge_branches(x_l, zl, a_l, b_l, c_l, d_l, gains, wb, wo, _pad_rows([g1]))
        if not last:
            b_c = attention(qc, [(kc, vc)])
            x_c = merge_branches(x_c, zc, a_c, b_c, c_c, d_c, gains, wb, wo, _pad_rows([cg1]))

        if li % 2 == 0:
            w_gu = ffn_w_in[li // 2].astype(BF16)
            w_dn = ffn_w_down[li // 2].astype(BF16)
            assert not last
            x_l = dense_ffn(x_l, _pad_rows([norm2_g[li], 1 + sc2, sh2, g2]), w_gu, w_dn)
            x_c = dense_ffn(x_c, _pad_rows([norm2_g[li], 1 + csc2, csh2, cg2]), w_gu, w_dn)
        else:
            assert last
            out = _moe(x_l, _pad_rows([norm2_g[li], 1 + sc2, sh2]), g2, moe_w_router[li // 2],
                       moe_b_router[li // 2], moe_w_in[li // 2], moe_w_down[li // 2], final_norm_g)
    return out[None]
```

```python
import functools
import math

import numpy as np
import jax
import jax.numpy as jnp
from jax import lax
from jax.experimental import pallas as pl
from jax.experimental.pallas import tpu as pltpu

F32 = jnp.float32
BF16 = jnp.bfloat16

D_MODEL = 1024
DEPTH = 2
GRID_W = 64
N_BRANCH = 4
N_HEADS = 4
HEAD_V = 128
BRANCH_W = N_HEADS * HEAD_V
GDN_DK = 128
CONV_K = 5
MLA_Q_RANK = 384
MLA_KV_RANK = 256
MLA_NOPE = 128
MLA_ROPE = 64
MLA_QK = MLA_NOPE + MLA_ROPE
MLA_QK_PAD = 256
MLA_V_PAD = 256
MLSTM_DK = 64
RET_DK = 64
ROPE_BASE = 10000.0
D_FF = 3584
N_EXPERTS = 8
TOP_K = 2
EPS = 1e-6

IN_WIDTHS = (
    N_BRANCH * D_MODEL,
    N_HEADS * GDN_DK, N_HEADS * GDN_DK, BRANCH_W, BRANCH_W, 2 * N_HEADS, 2 * N_HEADS,
    MLA_Q_RANK, MLA_KV_RANK, MLA_ROPE,
    N_HEADS * MLSTM_DK, N_HEADS * MLSTM_DK, BRANCH_W, BRANCH_W, 2 * N_HEADS, 2 * N_HEADS,
    N_HEADS * RET_DK, N_HEADS * RET_DK, BRANCH_W, BRANCH_W,
)
_IN_OFF = [0] + [int(o) for o in np.cumsum(IN_WIDTHS)]

VMEM_LIMIT = 48 * 1024 * 1024
MERGE_VMEM_LIMIT = 56 * 1024 * 1024
MOE_ROWS = 1024


def _cols(group):
    return np.arange(_IN_OFF[group], _IN_OFF[group + 1])


def _rope_swap(cols):
    q = MLA_ROPE // 4
    return np.concatenate([cols[q:2 * q], cols[:q], cols[3 * q:], cols[2 * q:3 * q]])


_MAIN_PERM = np.concatenate([
    _cols(1), _cols(2), _cols(3),
    _cols(4),
    _cols(0),
    _cols(12), _cols(13),
    _cols(10), _cols(11),
    _cols(18), _cols(19),
    _cols(16), _cols(17),
    _cols(7), _cols(8), _cols(9), _rope_swap(_cols(9)),
])
N_MAIN = int(_MAIN_PERM.shape[0])
_SMALL_PERM = np.concatenate([_cols(5), _cols(6), _cols(14), _cols(15)])
N_SMALL = 128
A_LANE, B_LANE, I_LANE, F_LANE = 0, 8, 16, 24
OFF_GDN_QKV, OFF_GDN_Z, OFF_GATE = 0, 1536, 2048
OFF_MLSTM_V, OFF_MLSTM_O, OFF_MLSTM_Q, OFF_MLSTM_K = 6144, 6656, 7168, 7424
OFF_RET_V, OFF_RET_G, OFF_RET_Q, OFF_RET_K = 7680, 8192, 8704, 8960
OFF_MLA = 9216
MLA_IN_W = 768


def _cparams(sem):
    return pltpu.CompilerParams(dimension_semantics=sem, vmem_limit_bytes=VMEM_LIMIT)


def _rms(x):
    return x * lax.rsqrt(jnp.mean(x * x, axis=-1, keepdims=True) + EPS)


def _dot(a, b):
    return jnp.dot(a, b, preferred_element_type=F32)


def _dot_nt(a, b):
    return lax.dot_general(a, b, (((1,), (1,)), ((), ())), preferred_element_type=F32)


def _mod_kernel(c_ref, w_ref, b_ref, o_ref):
    c = c_ref[...]
    s = c * jax.nn.sigmoid(c)
    o_ref[...] = jnp.dot(s, w_ref[...], preferred_element_type=F32) + b_ref[...]


def modulation_vectors(cond, w_mod, b_mod):
    n = w_mod.shape[1]
    tn = 1536
    return pl.pallas_call(
        _mod_kernel,
        out_shape=jax.ShapeDtypeStruct((8, n), F32),
        grid=(n // tn,),
        in_specs=[pl.BlockSpec((8, D_MODEL), lambda j: (0, 0)),
                  pl.BlockSpec((D_MODEL, tn), lambda j: (0, j)),
                  pl.BlockSpec((1, tn), lambda j: (0, j))],
        out_specs=pl.BlockSpec((8, tn), lambda j: (0, j)),
        compiler_params=_cparams(("arbitrary",)),
        name="modulation",
    )(cond, w_mod, b_mod.reshape(1, n))


def _norm_proj_kernel(x_ref, pv_ref, w_ref, o_ref, h_ref):
    @pl.when(pl.program_id(1) == 0)
    def _():
        pv = pv_ref[...]
        h_ref[...] = (_rms(x_ref[...]) * pv[0:1] * pv[1:2] + pv[2:3]).astype(BF16)

    o_ref[...] = _dot(h_ref[...], w_ref[...]).astype(o_ref.dtype)


def norm_proj(x, pv, w, tn, out_dtype):
    m, d = x.shape
    n = w.shape[1]
    tm = min(m, 1024)
    return pl.pallas_call(
        _norm_proj_kernel,
        out_shape=jax.ShapeDtypeStruct((m, n), out_dtype),
        grid=(m // tm, n // tn),
        in_specs=[pl.BlockSpec((tm, d), lambda i, j: (i, 0)),
                  pl.BlockSpec((8, d), lambda i, j: (0, 0)),
                  pl.BlockSpec((d, tn), lambda i, j: (0, j))],
        out_specs=pl.BlockSpec((tm, tn), lambda i, j: (i, j)),
        scratch_shapes=[pltpu.VMEM((tm, d), BF16)],
        compiler_params=_cparams(("parallel", "arbitrary")),
        name="norm_proj",
    )(x, pv, w)


def _mla_proj_kernel(z_ref, cos_ref, sin_ref, gq_ref, gkv_ref, wq_ref, wkv_ref, q_ref, k_ref, v_ref):
    z = z_ref[...].astype(F32)
    tm = z.shape[0]
    cq = z[:, :MLA_Q_RANK]
    ckv = z[:, MLA_Q_RANK:MLA_Q_RANK + MLA_KV_RANK]
    kr = z[:, MLA_Q_RANK + MLA_KV_RANK:]
    cos = cos_ref[...]
    sin = sin_ref[...]
    qn = (_rms(cq) * gq_ref[...]).astype(BF16)
    kvn = (_rms(ckv) * gkv_ref[...]).astype(BF16)
    qf = _dot(qn, wq_ref[...]) * (MLA_QK ** -0.5 * math.log2(math.e))
    kvf = _dot(kvn, wkv_ref[...])
    kr_rot = kr[:, :MLA_ROPE] * cos + kr[:, MLA_ROPE:] * sin
    pad = jnp.zeros((tm, MLA_QK_PAD - MLA_QK), F32)
    lane = lax.broadcasted_iota(jnp.int32, (tm, MLA_V_PAD - HEAD_V), 1)
    ones_col = jnp.where(lane == 0, 1.0, 0.0).astype(BF16)
    for h in range(N_HEADS):
        b = h * 256
        q_rot = qf[:, b + 128:b + 192] * cos + qf[:, b + 192:b + 256] * sin
        q_ref[h] = jnp.concatenate([qf[:, b:b + 128], q_rot, pad], axis=-1).astype(BF16)
        k_ref[h] = jnp.concatenate([kvf[:, b:b + 128], kr_rot, pad], axis=-1).astype(BF16)
        v_ref[h] = jnp.concatenate([kvf[:, b + 128:b + 256].astype(BF16), ones_col], axis=-1)


def mla_project(zmain, cos, sin, gq, gkv, wq_ext, wkv):
    m = zmain.shape[0]
    tm = min(m, 1024)
    full = lambda shape: pl.BlockSpec(shape, lambda i: tuple(0 for _ in shape))
    return pl.pallas_call(
        _mla_proj_kernel,
        out_shape=(jax.ShapeDtypeStruct((N_HEADS, m, MLA_QK_PAD), BF16),
                   jax.ShapeDtypeStruct((N_HEADS, m, MLA_QK_PAD), BF16),
                   jax.ShapeDtypeStruct((N_HEADS, m, MLA_V_PAD), BF16)),
        grid=(m // tm,),
        in_specs=[pl.BlockSpec((tm, MLA_IN_W), lambda i: (i, OFF_MLA // MLA_IN_W)),
                  pl.BlockSpec((tm, MLA_ROPE), lambda i: (i, 0)),
                  pl.BlockSpec((tm, MLA_ROPE), lambda i: (i, 0)),
                  full((1, MLA_Q_RANK)), full((1, MLA_KV_RANK)),
                  full((MLA_Q_RANK, N_HEADS * 256)), full((MLA_KV_RANK, N_HEADS * 256))],
        out_specs=(pl.BlockSpec((N_HEADS, tm, MLA_QK_PAD), lambda i: (0, i, 0)),
                   pl.BlockSpec((N_HEADS, tm, MLA_QK_PAD), lambda i: (0, i, 0)),
                   pl.BlockSpec((N_HEADS, tm, MLA_V_PAD), lambda i: (0, i, 0))),
        compiler_params=_cparams(("parallel",)),
        name="mla_project",
    )(zmain, cos, sin, gq, gkv, wq_ext, wkv)


ATTN_TQ, ATTN_TK = 512, 512
ATTN_UNROLL = 32


def _attn_kernel(*refs, segs):
    q_ref, o_ref = refs[0], refs[-1]
    q = q_ref[0]
    tq = q.shape[0]
    carry = (jnp.full((tq, 1), -1e30, F32), jnp.zeros((tq, MLA_V_PAD), F32))
    for si, (tk, nk) in enumerate(segs):
        k_ref, v_ref = refs[1 + 2 * si], refs[2 + 2 * si]

        def body(c, carry, k_ref=k_ref, v_ref=v_ref, tk=tk):
            m, acc = carry
            start = pl.multiple_of(c * tk, tk)
            s = _dot_nt(q, k_ref[0, pl.ds(start, tk), :])
            m_new = jnp.maximum(m, jnp.max(s, axis=-1, keepdims=True))
            p = jnp.exp2(s - m_new).astype(BF16)
            acc = jnp.exp2(m - m_new) * acc + _dot(p, v_ref[0, pl.ds(start, tk), :])
            return m_new, acc

        carry = lax.fori_loop(0, nk, body, carry, unroll=min(ATTN_UNROLL, nk))
    _, acc = carry
    o_ref[...] = (acc[:, :HEAD_V] / acc[:, HEAD_V:HEAD_V + 1]).astype(o_ref.dtype)


def attention(q, kvs):
    _, lq, _ = q.shape
    tq = min(lq, ATTN_TQ)
    segs, args, specs = [], [], []
    for k, v in kvs:
        lk = k.shape[1]
        tk = min(lk, ATTN_TK)
        segs.append((tk, lk // tk))
        args += [k, v]
        specs += [pl.BlockSpec((1, lk, MLA_QK_PAD), lambda h, i: (h, 0, 0)),
                  pl.BlockSpec((1, lk, MLA_V_PAD), lambda h, i: (h, 0, 0))]
    return pl.pallas_call(
        functools.partial(_attn_kernel, segs=tuple(segs)),
        out_shape=jax.ShapeDtypeStruct((lq, N_HEADS * HEAD_V), BF16),
        grid=(N_HEADS, lq // tq),
        in_specs=[pl.BlockSpec((1, tq, MLA_QK_PAD), lambda h, i: (h, i, 0))] + specs,
        out_specs=pl.BlockSpec((tq, HEAD_V), lambda h, i: (i, h)),
        compiler_params=_cparams(("parallel", "arbitrary")),
        name="attention",
    )(q, *args)


def _head_post(o, gate, gain, centre, silu_gate):
    sig = jax.nn.sigmoid(gate)
    act = gate * sig if silu_gate else sig
    outs = []
    for h in range(N_HEADS):
        sl = slice(h * HEAD_V, (h + 1) * HEAD_V)
        oh = o[:, sl]
        if centre:
            oh = oh - jnp.mean(oh, axis=-1, keepdims=True)
        oh = oh * lax.rsqrt(jnp.mean(oh * oh, axis=-1, keepdims=True) + EPS)
        outs.append((oh * gain[:, sl] * act[:, sl]).astype(BF16))
    return jnp.concatenate(outs, axis=1)


def _merge_kernel(x_ref, g0_ref, g1_ref, g2_ref, g3_ref, af_ref, ab_ref, b_ref, cf_ref, cb_ref, df_ref, db_ref,
                  za_ref, zc_ref, zd_ref, gain_ref, wb_ref, wo_ref, g_ref, o_ref):
    gain = gain_ref[...]
    branches = (
        _head_post(af_ref[...] + ab_ref[...], za_ref[...].astype(F32), gain[0:1], False, True),
        b_ref[...],
        _head_post(cf_ref[...] + cb_ref[...], zc_ref[...].astype(F32), gain[1:2], False, False),
        _head_post(df_ref[...] + db_ref[...], zd_ref[...].astype(F32), gain[2:3], True, True),
    )
    s = None
    for n, (br, gate_ref) in enumerate(zip(branches, (g0_ref, g1_ref, g2_ref, g3_ref))):
        proj = _dot(br, wb_ref[n])
        gate = jax.nn.sigmoid(gate_ref[...].astype(F32))
        s = gate * proj if s is None else s + gate * proj
    m = _dot(s.astype(BF16), wo_ref[...])
    o_ref[...] = x_ref[...] + g_ref[0:1] * m


def merge_branches(x, zmain, gdn, attn, mlstm, ret, gains, w_branch, w_out, gvec):
    m = x.shape[0]
    tm = min(m, 512)
    row = lambda w: pl.BlockSpec((tm, w), lambda i: (i, 0))
    zcol = lambda off: pl.BlockSpec((tm, BRANCH_W), lambda i: (i, off // BRANCH_W))
    return pl.pallas_call(
        _merge_kernel,
        out_shape=jax.ShapeDtypeStruct((m, D_MODEL), F32),
        grid=(m // tm,),
        in_specs=[row(D_MODEL)]
                 + [pl.BlockSpec((tm, D_MODEL), lambda i, n=n: (i, OFF_GATE // D_MODEL + n)) for n in range(N_BRANCH)]
                 + [row(BRANCH_W)] * 7
                 + [zcol(OFF_GDN_Z), zcol(OFF_MLSTM_O), zcol(OFF_RET_G),
                    pl.BlockSpec((8, BRANCH_W), lambda i: (0, 0)),
                    pl.BlockSpec((N_BRANCH, BRANCH_W, D_MODEL), lambda i: (0, 0, 0)),
                    pl.BlockSpec((D_MODEL, D_MODEL), lambda i: (0, 0)),
                    pl.BlockSpec((8, D_MODEL), lambda i: (0, 0))],
        out_specs=row(D_MODEL),
        compiler_params=pltpu.CompilerParams(dimension_semantics=("parallel",), vmem_limit_bytes=MERGE_VMEM_LIMIT),
        name="merge_branches",
    )(x, zmain, zmain, zmain, zmain, gdn[0], gdn[1], attn, mlstm[0], mlstm[1], ret[0], ret[1],
      zmain, zmain, zmain, gains, w_branch, w_out, gvec)


def _ffn_kernel(x_ref, pv_ref, wg_ref, wu_ref, wd_ref, o_ref, h_ref):
    f = pl.program_id(1)

    def partial_out(h):
        g = _dot(h, wg_ref[...])
        u = _dot(h, wu_ref[...])
        act = (g * jax.nn.sigmoid(g) * u).astype(BF16)
        return pv_ref[3:4] * _dot(act, wd_ref[...])

    @pl.when(f == 0)
    def _():
        pv = pv_ref[...]
        h = (_rms(x_ref[...]) * pv[0:1] * pv[1:2] + pv[2:3]).astype(BF16)
        h_ref[...] = h
        o_ref[...] = x_ref[...] + partial_out(h)

    @pl.when(f > 0)
    def _():
        o_ref[...] += partial_out(h_ref[...])


def dense_ffn(x, pv, w_gu, w_down):
    m = x.shape[0]
    tm = min(m, 1024)
    tf = 512
    nf = D_FF // tf
    return pl.pallas_call(
        _ffn_kernel,
        out_shape=jax.ShapeDtypeStruct((m, D_MODEL), F32),
        grid=(m // tm, nf),
        in_specs=[pl.BlockSpec((tm, D_MODEL), lambda i, f: (i, 0)),
                  pl.BlockSpec((8, D_MODEL), lambda i, f: (0, 0)),
                  pl.BlockSpec((D_MODEL, tf), lambda i, f: (0, f)),
                  pl.BlockSpec((D_MODEL, tf), lambda i, f: (0, f + nf)),
                  pl.BlockSpec((tf, D_MODEL), lambda i, f: (f, 0))],
        out_specs=pl.BlockSpec((tm, D_MODEL), lambda i, f: (i, 0)),
        scratch_shapes=[pltpu.VMEM((tm, D_MODEL), BF16)],
        compiler_params=_cparams(("parallel", "arbitrary")),
        name="dense_ffn",
    )(x, pv, w_gu, w_gu, w_down)


def _router_kernel(x_ref, pv_ref, w_ref, b_ref, h_ref, route_ref):
    pv = pv_ref[...]
    h = _rms(x_ref[...]) * pv[0:1] * pv[1:2] + pv[2:3]
    h_hi = h.astype(BF16)
    h_lo = (h - h_hi.astype(F32)).astype(BF16)
    w = w_ref[...]
    w_hi = w.astype(BF16)
    w_lo = (w - w_hi.astype(F32)).astype(BF16)
    logits = _dot(h_hi, w_hi) + _dot(h_hi, w_lo) + _dot(h_lo, w_hi) + b_ref[...]
    h_ref[...] = h
    lane = lax.broadcasted_iota(jnp.int32, logits.shape, 1).astype(F32)
    m1 = jnp.max(logits, axis=1, keepdims=True)
    e1 = jnp.min(jnp.where(logits == m1, lane, 128.0), axis=1, keepdims=True)
    rest = jnp.where(lane == e1, -jnp.inf, logits)
    m2 = jnp.max(rest, axis=1, keepdims=True)
    e2 = jnp.min(jnp.where(rest == m2, lane, 128.0), axis=1, keepdims=True)
    z2 = jnp.exp(m2 - m1)
    w1 = 1.0 / (1.0 + z2)
    route_ref[...] = jnp.where(lane == 0.0, w1, jnp.where(lane == 1.0, z2 * w1, jnp.where(
        lane == 2.0, e1, jnp.where(lane == 3.0, e2, 0.0))))


def moe_router(x, pv, w_router_pad, b_router_pad):
    m = x.shape[0]
    tm = min(m, 1024)
    return pl.pallas_call(
        _router_kernel,
        out_shape=(jax.ShapeDtypeStruct((m, D_MODEL), F32), jax.ShapeDtypeStruct((m, 128), F32)),
        grid=(m // tm,),
        in_specs=[pl.BlockSpec((tm, D_MODEL), lambda i: (i, 0)),
                  pl.BlockSpec((8, D_MODEL), lambda i: (0, 0)),
                  pl.BlockSpec((D_MODEL, 128), lambda i: (0, 0)),
                  pl.BlockSpec((1, 128), lambda i: (0, 0))],
        out_specs=(pl.BlockSpec((tm, D_MODEL), lambda i: (i, 0)), pl.BlockSpec((tm, 128), lambda i: (i, 0))),
        compiler_params=_cparams(("parallel",)),
        name="moe_router",
    )(x, pv, w_router_pad, b_router_pad)


def _expert_kernel(be_ref, nb_ref, x_ref, wg_ref, wu_ref, wd_ref, o_ref, x16_ref):
    b = pl.program_id(0)
    f = pl.program_id(1)

    live = b < nb_ref[0]

    def partial_out(x):
        g = _dot(x, wg_ref[0].astype(BF16))
        u = _dot(x, wu_ref[0].astype(BF16))
        act = (g * jax.nn.sigmoid(g) * u).astype(BF16)
        return _dot(act, wd_ref[0].astype(BF16))

    @pl.when(live & (f == 0))
    def _():
        x = x_ref[...].astype(BF16)
        x16_ref[...] = x
        o_ref[...] = partial_out(x)

    @pl.when(live & (f > 0))
    def _():
        o_ref[...] += partial_out(x16_ref[...])

    @pl.when(jnp.logical_not(live) & (f == 0))
    def _():
        o_ref[...] = jnp.zeros_like(o_ref)


def expert_ffn(block_e, n_used, xb, w_gu, w_down):
    cap = xb.shape[0]
    nb = cap // MOE_ROWS
    tf = 512
    nf = D_FF // tf

    def live(b, nbr):
        return jnp.minimum(b, nbr[0] - 1)

    def fsel(b, f, nbr):
        return jnp.where(b < nbr[0], f, nf - 1)

    grid_spec = pltpu.PrefetchScalarGridSpec(
        num_scalar_prefetch=2,
        grid=(nb, nf),
        in_specs=[pl.BlockSpec((MOE_ROWS, D_MODEL), lambda b, f, be, nbr: (live(b, nbr), 0)),
                  pl.BlockSpec((1, D_MODEL, tf), lambda b, f, be, nbr: (be[live(b, nbr)], 0, fsel(b, f, nbr))),
                  pl.BlockSpec((1, D_MODEL, tf), lambda b, f, be, nbr: (be[live(b, nbr)], 0, fsel(b, f, nbr) + nf)),
                  pl.BlockSpec((1, tf, D_MODEL), lambda b, f, be, nbr: (be[live(b, nbr)], fsel(b, f, nbr), 0))],
        out_specs=pl.BlockSpec((MOE_ROWS, D_MODEL), lambda b, f, be, nbr: (b, 0)),
        scratch_shapes=[pltpu.VMEM((MOE_ROWS, D_MODEL), BF16)],
    )
    return pl.pallas_call(
        _expert_kernel,
        out_shape=jax.ShapeDtypeStruct((cap, D_MODEL), F32),
        grid_spec=grid_spec,
        compiler_params=_cparams(("arbitrary", "arbitrary")),
        name="expert_ffn",
    )(block_e, n_used, xb, w_gu, w_gu, w_down)


def _final_kernel(x_ref, y0_ref, y1_ref, w_ref, pv_ref, o_ref):
    w = w_ref[...]
    f = w[:, 0:1] * y0_ref[...] + w[:, 1:2] * y1_ref[...]
    x = x_ref[...] + pv_ref[0:1] * f
    o_ref[...] = _rms(x) * pv_ref[1:2]


def final_combine(x, y2, w, pv):
    m = x.shape[0]
    tm = min(m, 1024)
    row = lambda wd: pl.BlockSpec((tm, wd), lambda i: (i, 0))
    return pl.pallas_call(
        _final_kernel,
        out_shape=jax.ShapeDtypeStruct((m, D_MODEL), F32),
        grid=(m // tm,),
        in_specs=[row(D_MODEL), row(D_MODEL), pl.BlockSpec((tm, D_MODEL), lambda i: (i + m // tm, 0)), row(128),
                  pl.BlockSpec((8, D_MODEL), lambda i: (0, 0))],
        out_specs=row(D_MODEL),
        compiler_params=_cparams(("parallel",)),
        name="final_combine",
    )(x, y2, y2, w, pv)


CH = 128
NEG = -1e30


def _masks(reverse):
    r = lax.broadcasted_iota(jnp.int32, (CH, CH), 0)
    c = lax.broadcasted_iota(jnp.int32, (CH, CH), 1)
    return (r <= c, r < c) if reverse else (r >= c, r > c)


def _cumsum_time(incl, x):
    m = jnp.where(incl, 1.0, 0.0).astype(BF16)
    hi = x.astype(BF16)
    lo = (x - hi.astype(F32)).astype(BF16)
    return _dot(m, hi) + _dot(m, lo)


def _cummax_time(a, reverse):
    row = lax.broadcasted_iota(jnp.int32, a.shape, 0)
    k = 1
    while k < CH:
        if reverse:
            a = jnp.maximum(a, jnp.where(row < CH - k, pltpu.roll(a, CH - k, 0), NEG))
        else:
            a = jnp.maximum(a, jnp.where(row >= k, pltpu.roll(a, k, 0), NEG))
        k *= 2
    return a


SCAN_SUB = 4


def _scan_grid(l):
    rows = min(l, SCAN_SUB * CH)
    return l // rows, rows


def _sweep(d, nsub):
    return list(range(nsub)) if d == 0 else list(range(nsub - 1, -1, -1))


def _dir_specs(nsteps, rows, width, col_block):
    return [pl.BlockSpec((rows, width), lambda n: (n, col_block)),
            pl.BlockSpec((rows, width), lambda n: (nsteps - 1 - n, col_block))]


def _whole(a):
    return pl.BlockSpec(a.shape, lambda n: tuple(0 for _ in a.shape))


def _ret_consts():
    log_gamma = np.log(1.0 - 2.0 ** (-5.0 - np.arange(N_HEADS, dtype=np.float64)))
    pos = np.arange(CH, dtype=np.float64)
    diff = pos[:, None] - pos[None, :]
    dec_f = np.where(diff >= 0, np.exp(log_gamma[:, None, None] * diff), 0.0)
    dec = np.stack([dec_f, np.transpose(dec_f, (0, 2, 1))])
    qs_f = np.exp(log_gamma[None, :] * (pos[:, None] + 1.0))
    ks_f = np.exp(log_gamma[None, :] * (CH - 1.0 - pos[:, None]))
    qs_b = np.exp(log_gamma[None, :] * (CH - pos[:, None]))
    ks_b = np.exp(log_gamma[None, :] * pos[:, None])
    rep = lambda a: np.repeat(a, RET_DK, axis=1)
    qs = np.stack([rep(qs_f), rep(qs_b)])
    ks = np.stack([rep(ks_f), rep(ks_b)])
    chunk_decay = [float(np.exp(lg * CH)) for lg in log_gamma]
    return (jnp.asarray(dec, F32), jnp.asarray(qs, F32), jnp.asarray(ks, F32)), chunk_decay


def _ret_kernel(*refs, rotary, chunk_decay):
    if rotary:
        (qf, qb, kf, kb, vf, vb, cosf, cosb, sinf, sinb, dec_ref, qs_ref, ks_ref, s0_ref,
         of_ref, ob_ref, s_ref) = refs
        tabs = ((cosf, sinf), (cosb, sinb))
    else:
        qf, qb, kf, kb, vf, vb, dec_ref, qs_ref, ks_ref, s0_ref, of_ref, ob_ref, s_ref = refs
        tabs = (None, None)

    @pl.when(pl.program_id(0) == 0)
    def _():
        s_ref[...] = s0_ref[...]

    lane = lax.broadcasted_iota(jnp.int32, (CH, N_HEADS * RET_DK), 1)
    first_half = (lane & (RET_DK - 1)) < RET_DK // 2

    nsub = qf.shape[0] // CH
    chains = {}
    for d, (q_ref, k_ref, v_ref, o_ref) in enumerate(((qf, kf, vf, of_ref), (qb, kb, vb, ob_ref))):
        for j in range(nsub):
            rows = slice(j * CH, (j + 1) * CH)
            q = q_ref[rows, :].astype(F32)
            k = k_ref[rows, :].astype(F32) * RET_DK ** -0.5
            if rotary:
                cos, sin = tabs[d][0][rows, :], tabs[d][1][rows, :]

                def rot(x, cos=cos, sin=sin):
                    swapped = jnp.where(first_half, pltpu.roll(x, N_HEADS * RET_DK - RET_DK // 2, 1),
                                        pltpu.roll(x, RET_DK // 2, 1))
                    return x * cos + swapped * sin

                q, k = rot(q), rot(k)
            qb16, kb16 = q.astype(BF16), k.astype(BF16)
            q_in = (q * qs_ref[d]).astype(BF16)
            k_out = k * ks_ref[d]
            k_t = [k_out[:, :128].T, k_out[:, 128:].T]
            for h in range(N_HEADS):
                sl = slice(h * RET_DK, (h + 1) * RET_DK)
                chains[d, j, h] = dict(
                    o_ref=o_ref, rows=rows, vh=v_ref[rows, h * HEAD_V:(h + 1) * HEAD_V], q=qb16[:, sl], k=kb16[:, sl],
                    q_in=q_in[:, sl], k_th=k_t[h // 2][(h % 2) * RET_DK:(h % 2 + 1) * RET_DK, :].astype(BF16))

    for (d, j, h), c in chains.items():
        c["p"] = (_dot_nt(c["q"], c["k"]) * dec_ref[d, h]).astype(BF16)
    for c in chains.values():
        c["intra"] = _dot(c["p"], c["vh"])
        c["update"] = _dot(c["k_th"], c["vh"])
    states = {(d, h): s_ref[d, h] for d in range(2) for h in range(N_HEADS)}
    for i in range(nsub):
        for (d, h), s in list(states.items()):
            c = chains[d, _sweep(d, nsub)[i], h]
            c["o_ref"][c["rows"], h * HEAD_V:(h + 1) * HEAD_V] = c["intra"] + _dot(c["q_in"], s.astype(BF16))
            states[d, h] = chunk_decay[h] * s + c["update"]
    for (d, h), s in states.items():
        s_ref[d, h] = s


def retention_scan(z, s0, tables):
    l = z.shape[0]
    nsteps, rows = _scan_grid(l)
    consts, chunk_decay = _ret_consts()
    rotary = tables is not None
    w = N_HEADS * RET_DK
    specs = (_dir_specs(nsteps, rows, w, OFF_RET_Q // w) + _dir_specs(nsteps, rows, w, OFF_RET_K // w)
             + _dir_specs(nsteps, rows, BRANCH_W, OFF_RET_V // BRANCH_W))
    args = [z] * 6
    if rotary:
        specs += _dir_specs(nsteps, rows, w, 0) + _dir_specs(nsteps, rows, w, 0)
        args += [tables[0], tables[0], tables[1], tables[1]]
    specs += [_whole(c) for c in consts] + [_whole(s0)]
    args += list(consts) + [s0]
    return pl.pallas_call(
        functools.partial(_ret_kernel, rotary=rotary, chunk_decay=chunk_decay),
        out_shape=(jax.ShapeDtypeStruct((l, BRANCH_W), F32), jax.ShapeDtypeStruct((l, BRANCH_W), F32),
                   jax.ShapeDtypeStruct(s0.shape, F32)),
        grid=(nsteps,),
        in_specs=specs,
        out_specs=tuple(_dir_specs(nsteps, rows, BRANCH_W, 0)) + (_whole(s0),),
        compiler_params=_cparams(("arbitrary",)),
        name="retention_scan",
    )(*args)


def _mlstm_kernel(qf, qb, kf, kb, vf, vb, smf, smb, bias_ref, c0_ref, m0_ref, of_ref, ob_ref, c_ref, m_ref):
    @pl.when(pl.program_id(0) == 0)
    def _():
        c_ref[...] = c0_ref[...]
        m_ref[...] = m0_ref[...]

    lane = lax.broadcasted_iota(jnp.int32, (CH, 128), 1)
    is_forget = (lane >= F_LANE) & (lane < F_LANE + 2 * N_HEADS)
    ones_col = jnp.where(lane == 0, 1.0, 0.0).astype(BF16)

    nsub = qf.shape[0] // CH
    chains = {}
    for d, (q_ref, k_ref, v_ref, sm_ref, o_ref) in enumerate(((qf, kf, vf, smf, of_ref), (qb, kb, vb, smb, ob_ref))):
        reverse = d == 1
        incl, _ = _masks(reverse)
        last = 0 if reverse else CH - 1
        for j in range(nsub):
            rows = slice(j * CH, (j + 1) * CH)
            pre = sm_ref[rows, :] + bias_ref[0:1]
            x = jnp.where(is_forget, jax.nn.log_sigmoid(pre), pre)
            b = _cumsum_time(incl, x)
            a = pltpu.roll(x, F_LANE - I_LANE, 1) - b
            cm = _cummax_time(a, reverse)
            a_t = a.T
            q = (q_ref[rows, :].astype(F32) * MLSTM_DK ** -0.5).astype(BF16)
            kf32 = k_ref[rows, :].astype(F32)
            k_t = [kf32[:, :128].T, kf32[:, 128:].T]
            for h in range(N_HEADS):
                gf = F_LANE + d * N_HEADS + h
                sl = slice(h * MLSTM_DK, (h + 1) * MLSTM_DK)
                cm_col, a_row = cm[:, gf:gf + 1], a_t[gf:gf + 1, :]
                chains[d, j, h] = dict(
                    o_ref=o_ref, rows=rows, q=q[:, sl], k=k_ref[rows, sl], a_row=a_row, cm_col=cm_col,
                    b_col=b[:, gf:gf + 1], b_last=b[last:last + 1, gf:gf + 1], cm_last=cm[last:last + 1, gf:gf + 1],
                    w_intra=jnp.where(incl, jnp.exp(jnp.minimum(a_row - cm_col, 0.0)), 0.0),
                    k_th=k_t[h // 2][(h % 2) * MLSTM_DK:(h % 2 + 1) * MLSTM_DK, :],
                    v_aug=jnp.concatenate([v_ref[rows, h * HEAD_V:(h + 1) * HEAD_V], ones_col], axis=1))

    for c in chains.values():
        c["p"] = (c["w_intra"] * _dot_nt(c["q"], c["k"])).astype(BF16)
    for c in chains.values():
        c["pv"] = _dot(c["p"], c["v_aug"])
    c_augs = {(d, h): c_ref[d, h] for d in range(2) for h in range(N_HEADS)}
    m_ss = {(d, h): m_ref[d, h][0:1, 0:1] for d in range(2) for h in range(N_HEADS)}
    for i in range(nsub):
        cur = {(d, h): chains[d, _sweep(d, nsub)[i], h] for (d, h) in c_augs}
        qcs = {key: _dot(c["q"], c_augs[key].astype(BF16)) for key, c in cur.items()}
        tops = {key: jnp.maximum(m_ss[key], c["cm_last"]) for key, c in cur.items()}
        updates = {key: _dot((c["k_th"] * jnp.exp(c["a_row"] - tops[key])).astype(BF16), c["v_aug"])
                   for key, c in cur.items()}
        for (d, h), c in cur.items():
            m_s = m_ss[d, h]
            mx = jnp.maximum(m_s, c["cm_col"])
            tot = jnp.exp(m_s - mx) * qcs[d, h] + jnp.exp(c["cm_col"] - mx) * c["pv"]
            den = jnp.maximum(jnp.abs(tot[:, HEAD_V:HEAD_V + 1]), jnp.exp(-(c["b_col"] + mx)))
            c["o_ref"][c["rows"], h * HEAD_V:(h + 1) * HEAD_V] = tot[:, :HEAD_V] / den
        for key, c in cur.items():
            c_augs[key] = jnp.exp(m_ss[key] - tops[key]) * c_augs[key] + updates[key]
            m_ss[key] = c["b_last"] + tops[key]
    for (d, h) in c_augs:
        c_ref[d, h] = c_augs[d, h]
        m_ref[d, h] = jnp.broadcast_to(m_ss[d, h], (8, 128))


def mlstm_scan(z, small, bias, c0, m0):
    l = z.shape[0]
    nsteps, rows = _scan_grid(l)
    w = N_HEADS * MLSTM_DK
    specs = (_dir_specs(nsteps, rows, w, OFF_MLSTM_Q // w) + _dir_specs(nsteps, rows, w, OFF_MLSTM_K // w)
             + _dir_specs(nsteps, rows, BRANCH_W, OFF_MLSTM_V // BRANCH_W) + _dir_specs(nsteps, rows, 128, 0))
    specs += [_whole(bias), _whole(c0), _whole(m0)]
    return pl.pallas_call(
        _mlstm_kernel,
        out_shape=(jax.ShapeDtypeStruct((l, BRANCH_W), F32), jax.ShapeDtypeStruct((l, BRANCH_W), F32),
                   jax.ShapeDtypeStruct(c0.shape, F32), jax.ShapeDtypeStruct(m0.shape, F32)),
        grid=(nsteps,),
        in_specs=specs,
        out_specs=tuple(_dir_specs(nsteps, rows, BRANCH_W, 0)) + (_whole(c0), _whole(m0)),
        compiler_params=_cparams(("arbitrary",)),
        name="mlstm_scan",
    )(z, z, z, z, z, z, small, small, bias, c0, m0)


QKV_W = 3 * N_HEADS * GDN_DK
HALO = 8


def _gdn_prep_kernel(x_ref, prev_ref, next_ref, w_ref, q_ref, k_ref, v_ref):
    i = pl.program_id(0)
    tm = x_ref.shape[0]
    x = x_ref[...].astype(F32)
    prev = jnp.where(i > 0, prev_ref[...].astype(F32), 0.0)
    nxt = jnp.where(i < pl.num_programs(0) - 1, next_ref[...].astype(F32), 0.0)
    xe = jnp.concatenate([prev, x, nxt], axis=0)
    w = w_ref[...]
    y = None
    for tap in range(CONV_K):
        off = HALO + tap - CONV_K // 2
        term = w[tap:tap + 1] * xe[off:off + tm]
        y = term if y is None else y + term
    y = y * jax.nn.sigmoid(y)
    hw = N_HEADS * GDN_DK
    for h in range(N_HEADS):
        sl = slice(h * GDN_DK, (h + 1) * GDN_DK)
        qh = y[:, sl]
        kh = y[:, hw + h * GDN_DK:hw + (h + 1) * GDN_DK]
        q_ref[:, sl] = (qh * lax.rsqrt(jnp.sum(qh * qh, axis=-1, keepdims=True) + EPS) * GDN_DK ** -0.5).astype(BF16)
        k_ref[:, sl] = (kh * lax.rsqrt(jnp.sum(kh * kh, axis=-1, keepdims=True) + EPS)).astype(BF16)
    v_ref[...] = y[:, 2 * hw:].astype(BF16)


def gdn_prep(z, conv_w):
    l = z.shape[0]
    tm = min(l, 256)
    nb = l // tm
    r8 = tm // HALO
    cb = OFF_GDN_QKV // QKV_W
    w8 = jnp.concatenate([conv_w.astype(F32), jnp.zeros((8 - CONV_K, QKV_W), F32)], axis=0)
    return pl.pallas_call(
        _gdn_prep_kernel,
        out_shape=tuple(jax.ShapeDtypeStruct((l, BRANCH_W), BF16) for _ in range(3)),
        grid=(nb,),
        in_specs=[pl.BlockSpec((tm, QKV_W), lambda i: (i, cb)),
                  pl.BlockSpec((HALO, QKV_W), lambda i: (jnp.maximum(i * r8 - 1, 0), cb)),
                  pl.BlockSpec((HALO, QKV_W), lambda i: (jnp.minimum((i + 1) * r8, nb * r8 - 1), cb)),
                  pl.BlockSpec((8, QKV_W), lambda i: (0, 0))],
        out_specs=tuple(pl.BlockSpec((tm, BRANCH_W), lambda i: (i, 0)) for _ in range(3)),
        compiler_params=_cparams(("parallel",)),
        name="gdn_prep",
    )(z, z, z, w8)


N_LEVELS = 7


def _gdn_kernel(qf, qb, kf, kb, vf, vb, smf, smb, par_ref, s0_ref, of_ref, ob_ref, s_ref):
    @pl.when(pl.program_id(0) == 0)
    def _():
        s_ref[...] = s0_ref[...]

    lane = lax.broadcasted_iota(jnp.int32, (CH, 128), 1)
    is_decay = lane < B_LANE
    ri = lax.broadcasted_iota(jnp.int32, (CH, CH), 0)
    ci = lax.broadcasted_iota(jnp.int32, (CH, CH), 1)
    eye = jnp.where(ri == ci, 1.0, 0.0)
    pair_masks = [((ri >> (l + 1)) == (ci >> (l + 1))) & ((ri >> l) != (ci >> l)) for l in range(N_LEVELS)]

    nsub = qf.shape[0] // CH
    chains = {}
    for d, (q_ref, k_ref, v_ref, sm_ref, o_ref) in enumerate(((qf, kf, vf, smf, of_ref), (qb, kb, vb, smb, ob_ref))):
        reverse = d == 1
        incl, strict = _masks(reverse)
        last = 0 if reverse else CH - 1
        for j in range(nsub):
            rows = slice(j * CH, (j + 1) * CH)
            sm = sm_ref[rows, :]
            log_a = -jnp.exp(par_ref[1:2]) * jax.nn.softplus(sm + par_ref[0:1])
            x = jnp.where(is_decay, log_a, jax.nn.sigmoid(sm))
            g = _cumsum_time(incl, x)
            g_t = g.T
            for h in range(N_HEADS):
                ga, gb = A_LANE + d * N_HEADS + h, B_LANE + d * N_HEADS + h
                sl = slice(h * GDN_DK, (h + 1) * GDN_DK)
                g_col, g_row = g[:, ga:ga + 1], g_t[ga:ga + 1, :]
                g_last = g[last:last + 1, ga:ga + 1]
                beta = x[:, gb:gb + 1]
                decay = jnp.where(incl, jnp.exp(jnp.minimum(g_col - g_row, 0.0)), 0.0)
                kh = k_ref[rows, sl]
                kf32 = kh.astype(F32)
                kbeta = kf32 * beta
                a = jnp.where(strict, _dot_nt(kbeta.astype(BF16), kh) * decay, 0.0)
                e_g = jnp.exp(g_col)
                rhs = jnp.concatenate([v_ref[rows, sl].astype(F32) * beta, kbeta * e_g], axis=1)
                chains[d, j, h] = dict(
                    rows=rows, sl=sl, o_ref=o_ref, a=a, rhs=rhs, s_decay=jnp.exp(g_last),
                    qk=(_dot_nt(q_ref[rows, sl], kh) * decay).astype(BF16),
                    q_in=(q_ref[rows, sl].astype(F32) * e_g).astype(BF16),
                    k_out_t=(kf32.T * jnp.exp(g_last - g_row)).astype(BF16))

    cl = list(chains.values())
    xs = [eye - jnp.where(pair_masks[0], c["a"], 0.0) for c in cl]
    for pm in pair_masks[1:]:
        ys = [_dot(jnp.where(pm, c["a"], 0.0).astype(BF16), x.astype(BF16)) for c, x in zip(cl, xs)]
        xs = [x - _dot(x.astype(BF16), y.astype(BF16)) for x, y in zip(xs, ys)]
    for c, x in zip(cl, xs):
        c["sol"] = c["rhs"] + _dot((x - eye).astype(BF16), c["rhs"].astype(BF16))
    states = {(d, h): s_ref[d, h] for d in range(2) for h in range(N_HEADS)}
    for i in range(nsub):
        cur = {(d, h): chains[d, _sweep(d, nsub)[i], h] for (d, h) in states}
        s16s = {key: s.astype(BF16) for key, s in states.items()}
        v_news = {key: (c["sol"][:, :HEAD_V] - _dot(c["sol"][:, HEAD_V:].astype(BF16), s16s[key])).astype(BF16)
                  for key, c in cur.items()}
        updates = {key: _dot(c["k_out_t"], v_news[key]) for key, c in cur.items()}
        for key, c in cur.items():
            c["o_ref"][c["rows"], c["sl"]] = _dot(c["q_in"], s16s[key]) + _dot(c["qk"], v_news[key])
        for key, c in cur.items():
            states[key] = states[key] * c["s_decay"] + updates[key]
    for (d, h), s in states.items():
        s_ref[d, h] = s


def gdn_scan(qn, kn, vn, small, par, s0):
    l = qn.shape[0]
    nsteps, rows = _scan_grid(l)
    specs = _dir_specs(nsteps, rows, BRANCH_W, 0) * 3 + _dir_specs(nsteps, rows, 128, 0)
    specs += [_whole(par), _whole(s0)]
    return pl.pallas_call(
        _gdn_kernel,
        out_shape=(jax.ShapeDtypeStruct((l, BRANCH_W), F32), jax.ShapeDtypeStruct((l, BRANCH_W), F32),
                   jax.ShapeDtypeStruct(s0.shape, F32)),
        grid=(nsteps,),
        in_specs=specs,
        out_specs=tuple(_dir_specs(nsteps, rows, BRANCH_W, 0)) + (_whole(s0),),
        compiler_params=_cparams(("arbitrary",)),
        name="gdn_scan",
    )(qn, qn, kn, kn, vn, vn, small, small, par, s0)


def _lane_rows(entries):
    r = jnp.zeros((8, 128), F32)
    for row, lane, vals in entries:
        r = r.at[row, lane:lane + vals.shape[0]].set(vals.astype(F32))
    return r


def gdn_branch(zc, sc, zl, sl, conv_w, a_log, dt_bias):
    par = _lane_rows([(0, A_LANE, dt_bias.reshape(-1)), (1, A_LANE, a_log.reshape(-1))])
    s0 = jnp.zeros((2, N_HEADS, GDN_DK, HEAD_V), F32)
    ofc, obc, s1 = gdn_scan(*gdn_prep(zc, conv_w), sc, par, s0)
    ofl, obl, _ = gdn_scan(*gdn_prep(zl, conv_w), sl, par, s1)
    return (ofc, obc), (ofl, obl)


def mlstm_branch(zc, sc, zl, sl, gate_b):
    bias = _lane_rows([(0, I_LANE, gate_b[0].reshape(-1)), (0, F_LANE, gate_b[1].reshape(-1))])
    c0 = jnp.zeros((2, N_HEADS, MLSTM_DK, 2 * HEAD_V), F32)
    m0 = jnp.zeros((2, N_HEADS, 8, 128), F32)
    ofc, obc, c1, m1 = mlstm_scan(zc, sc, bias, c0, m0)
    ofl, obl, _, _ = mlstm_scan(zl, sl, bias, c1, m1)
    return (ofc, obc), (ofl, obl)


def retention_branch(zc, zl, tables):
    s0 = jnp.zeros((2, N_HEADS, RET_DK, HEAD_V), F32)
    ofc, obc, s1 = retention_scan(zc, s0, None)
    ofl, obl, _ = retention_scan(zl, s1, tables)
    return (ofc, obc), (ofl, obl)


def _ret_rope_tables(n_lat):
    inv = np.float32(ROPE_BASE) ** (-np.arange(0, RET_DK, 2, dtype=np.float32) / np.float32(RET_DK))
    ang = np.arange(n_lat, dtype=np.float32)[:, None] * inv[None, :]
    cos = np.concatenate([np.cos(ang), np.cos(ang)], axis=-1)
    sin = np.concatenate([-np.sin(ang), np.sin(ang)], axis=-1)
    return jnp.asarray(np.tile(cos, (1, N_HEADS)), F32), jnp.asarray(np.tile(sin, (1, N_HEADS)), F32)


def _rope_tables(n_lat):
    q = MLA_ROPE // 4
    inv = np.float32(ROPE_BASE) ** (-np.arange(0, 2 * q, 2, dtype=np.float32) / np.float32(2 * q))
    t = np.arange(n_lat)
    row = (t // GRID_W).astype(np.float32)[:, None] * inv[None, :]
    col = (t % GRID_W).astype(np.float32)[:, None] * inv[None, :]
    cos = np.concatenate([np.cos(row), np.cos(row), np.cos(col), np.cos(col)], axis=-1)
    sin = np.concatenate([-np.sin(row), np.sin(row), -np.sin(col), np.sin(col)], axis=-1)
    return jnp.asarray(cos, F32), jnp.asarray(sin, F32)


def _take_cols(w, perm):
    runs, start = [], 0
    for i in range(1, len(perm) + 1):
        if i == len(perm) or perm[i] != perm[i - 1] + 1:
            runs.append((int(perm[start]), int(perm[i - 1]) + 1))
            start = i
    return jnp.concatenate([w[:, a:b] for a, b in runs], axis=1)


def _pad_rows(v, n=8):
    rows = [jnp.reshape(r, (1, -1)).astype(F32) for r in v]
    d = rows[0].shape[1]
    return jnp.concatenate(rows + [jnp.zeros((n - len(rows), d), F32)], axis=0)


def _mla_weights(w_uq, w_ukv):
    wq = w_uq.reshape(MLA_Q_RANK, N_HEADS, MLA_QK)
    rope = wq[:, :, MLA_NOPE:]
    swapped = rope[:, :, _rope_swap(np.arange(MLA_ROPE))]
    wq_ext = jnp.concatenate([wq[:, :, :MLA_NOPE], rope, swapped], axis=-1).reshape(MLA_Q_RANK, N_HEADS * 256)
    return wq_ext.astype(BF16), w_ukv.astype(BF16)


def _moe(x_l, pv2, g2, w_router, b_router, w_gu, w_down, final_g):
    n = x_l.shape[0]
    w_pad = jnp.concatenate([w_router, jnp.zeros((D_MODEL, 128 - N_EXPERTS), F32)], axis=1)
    b_pad = jnp.concatenate([b_router, jnp.full((128 - N_EXPERTS,), -1e30, F32)]).reshape(1, 128)
    h2, route = moe_router(x_l, pv2, w_pad, b_pad)
    top_e = route[:, TOP_K:2 * TOP_K].astype(jnp.int32)
    n_assign = n * TOP_K
    flat_e = top_e.reshape(-1)
    onehot = (flat_e[:, None] == jnp.arange(N_EXPERTS)[None, :]).astype(jnp.int32)
    csum = jnp.cumsum(onehot, axis=0)
    counts = csum[-1]
    rank = jnp.take_along_axis(csum, flat_e[:, None], axis=1)[:, 0] - 1
    padded = (counts + MOE_ROWS - 1) // MOE_ROWS * MOE_ROWS
    pad_end = jnp.cumsum(padded)
    dest = (pad_end - padded)[flat_e] + rank
    nb = n_assign // MOE_ROWS + N_EXPERTS
    cap = nb * MOE_ROWS
    block_start = jnp.arange(nb, dtype=pad_end.dtype) * MOE_ROWS
    block_e = jnp.minimum(jnp.sum(block_start[:, None] >= pad_end[None, :], axis=1), N_EXPERTS - 1).astype(jnp.int32)
    n_used = (pad_end[-1] // MOE_ROWS).astype(jnp.int32).reshape(1)
    slot_tok = (jnp.arange(cap, dtype=jnp.int32) % n).at[dest].set(jnp.arange(n_assign, dtype=jnp.int32) // TOP_K)
    yb = expert_ffn(block_e, n_used, h2[slot_tok], w_gu, w_down)
    y2 = yb[dest.reshape(n, TOP_K).T.reshape(-1)]
    return final_combine(x_l, y2, route, _pad_rows([g2, final_g]))


def kernel(x, c, ctx, c_ctx, w_mod, b_mod, norm1_g, norm2_g, w_in, gdn_conv_w, gdn_a_log, gdn_dt_bias, gdn_norm_g, mla_q_norm_g, mla_kv_norm_g, mla_w_uq, mla_w_ukv, mlstm_gate_b, mlstm_norm_g, ret_norm_g, w_branch, w_out, ffn_w_in, ffn_w_down, moe_w_router, moe_b_router, moe_w_in, moe_w_down, final_norm_g):
    n_lat = x.shape[1]
    n_ctx = ctx.shape[1]
    x_l, x_c = x[0], ctx[0]
    cond = _pad_rows([c_ctx, c[0]])
    cos_l, sin_l = _rope_tables(n_lat)
    cos_c, sin_c = jnp.ones((n_ctx, MLA_ROPE), F32), jnp.zeros((n_ctx, MLA_ROPE), F32)
    ret_tables = _ret_rope_tables(n_lat)
    out = None
    for li in range(DEPTH):
        last = li == DEPTH - 1
        mod = modulation_vectors(cond, w_mod[li], b_mod[li])
        csh1, csc1, cg1, csh2, csc2, cg2 = jnp.split(mod[0], 6)
        sh1, sc1, g1, sh2, sc2, g2 = jnp.split(mod[1], 6)
        w_main = _take_cols(w_in[li], _MAIN_PERM).astype(BF16)
        w_small = _take_cols(w_in[li], _SMALL_PERM).astype(BF16)
        w_small = jnp.concatenate([w_small, jnp.zeros((D_MODEL, N_SMALL - w_small.shape[1]), BF16)], axis=1)
        pv_l = _pad_rows([norm1_g[li], 1 + sc1, sh1])
        pv_c = _pad_rows([norm1_g[li], 1 + csc1, csh1])
        zl = norm_proj(x_l, pv_l, w_main, 1664, BF16)
        zc = norm_proj(x_c, pv_c, w_main, 1664, BF16)
        sl = norm_proj(x_l, pv_l, w_small, 128, F32)
        sc = norm_proj(x_c, pv_c, w_small, 128, F32)

        a_c, a_l = gdn_branch(zc, sc, zl, sl, gdn_conv_w[li], gdn_a_log[li], gdn_dt_bias[li])
        c_c, c_l = mlstm_branch(zc, sc, zl, sl, mlstm_gate_b[li])
        d_c, d_l = retention_branch(zc, zl, ret_tables)
        gains = _pad_rows([gdn_norm_g[li], mlstm_norm_g[li], ret_norm_g[li]])

        wq_ext, wkv = _mla_weights(mla_w_uq[li], mla_w_ukv[li])
        gq, gkv = mla_q_norm_g[li].reshape(1, -1), mla_kv_norm_g[li].reshape(1, -1)
        ql, kl, vl = mla_project(zl, cos_l, sin_l, gq, gkv, wq_ext, wkv)
        qc, kc, vc = mla_project(zc, cos_c, sin_c, gq, gkv, wq_ext, wkv)
        b_l = attention(ql, [(kl, vl), (kc, vc)])

        wb = w_branch[li].astype(BF16)
        wo = w_out[li].astype(BF16)
        x_l = merge_branches(x_l, zl, a_l, b_l, c_l, d_l, gains, wb, wo, _pad_rows([g1]))
        if not last:
            b_c = attention(qc, [(kc, vc)])
            x_c = merge_branches(x_c, zc, a_c, b_c, c_c, d_c, gains, wb, wo, _pad_rows([cg1]))

        if li % 2 == 0:
            w_gu = ffn_w_in[li // 2].astype(BF16)
            w_dn = ffn_w_down[li // 2].astype(BF16)
            assert not last
            x_l = dense_ffn(x_l, _pad_rows([norm2_g[li], 1 + sc2, sh2, g2]), w_gu, w_dn)
            x_c = dense_ffn(x_c, _pad_rows([norm2_g[li], 1 + csc2, csh2, cg2]), w_gu, w_dn)
        else:
            assert last
            out = _moe(x_l, _pad_rows([norm2_g[li], 1 + sc2, sh2]), g2, moe_w_router[li // 2],
                       moe_b_router[li // 2], moe_w_in[li // 2], moe_w_down[li // 2], final_norm_g)
    return out[None]
```

```python
import functools
import math

import numpy as np
import jax
import jax.numpy as jnp
from jax import lax
from jax.experimental import pallas as pl
from jax.experimental.pallas import tpu as pltpu

F32 = jnp.float32
BF16 = jnp.bfloat16

D_MODEL = 1024
DEPTH = 2
GRID_W = 64
N_BRANCH = 4
N_HEADS = 4
HEAD_V = 128
BRANCH_W = N_HEADS * HEAD_V
GDN_DK = 128
CONV_K = 5
MLA_Q_RANK = 384
MLA_KV_RANK = 256
MLA_NOPE = 128
MLA_ROPE = 64
MLA_QK = MLA_NOPE + MLA_ROPE
MLA_QK_PAD = 256
MLA_V_PAD = 256
MLSTM_DK = 64
RET_DK = 64
ROPE_BASE = 10000.0
D_FF = 3584
N_EXPERTS = 8
TOP_K = 2
EPS = 1e-6

IN_WIDTHS = (
    N_BRANCH * D_MODEL,
    N_HEADS * GDN_DK, N_HEADS * GDN_DK, BRANCH_W, BRANCH_W, 2 * N_HEADS, 2 * N_HEADS,
    MLA_Q_RANK, MLA_KV_RANK, MLA_ROPE,
    N_HEADS * MLSTM_DK, N_HEADS * MLSTM_DK, BRANCH_W, BRANCH_W, 2 * N_HEADS, 2 * N_HEADS,
    N_HEADS * RET_DK, N_HEADS * RET_DK, BRANCH_W, BRANCH_W,
)
_IN_OFF = [0] + [int(o) for o in np.cumsum(IN_WIDTHS)]

VMEM_LIMIT = 48 * 1024 * 1024
MERGE_VMEM_LIMIT = 56 * 1024 * 1024
MOE_ROWS = 1024


def _cols(group):
    return np.arange(_IN_OFF[group], _IN_OFF[group + 1])


def _rope_swap(cols):
    q = MLA_ROPE // 4
    return np.concatenate([cols[q:2 * q], cols[:q], cols[3 * q:], cols[2 * q:3 * q]])


_MAIN_PERM = np.concatenate([
    _cols(1), _cols(2), _cols(3),
    _cols(4),
    _cols(0),
    _cols(12), _cols(13),
    _cols(10), _cols(11),
    _cols(18), _cols(19),
    _cols(16), _cols(17),
    _cols(7), _cols(8), _cols(9), _rope_swap(_cols(9)),
])
N_MAIN = int(_MAIN_PERM.shape[0])
_SMALL_PERM = np.concatenate([_cols(5), _cols(6), _cols(14), _cols(15)])
N_SMALL = 128
A_LANE, B_LANE, I_LANE, F_LANE = 0, 8, 16, 24
OFF_GDN_QKV, OFF_GDN_Z, OFF_GATE = 0, 1536, 2048
OFF_MLSTM_V, OFF_MLSTM_O, OFF_MLSTM_Q, OFF_MLSTM_K = 6144, 6656, 7168, 7424
OFF_RET_V, OFF_RET_G, OFF_RET_Q, OFF_RET_K = 7680, 8192, 8704, 8960
OFF_MLA = 9216
MLA_IN_W = 768


def _cparams(sem):
    return pltpu.CompilerParams(dimension_semantics=sem, vmem_limit_bytes=VMEM_LIMIT)


def _rms(x):
    return x * lax.rsqrt(jnp.mean(x * x, axis=-1, keepdims=True) + EPS)


def _dot(a, b):
    return jnp.dot(a, b, preferred_element_type=F32)


def _dot_nt(a, b):
    return lax.dot_general(a, b, (((1,), (1,)), ((), ())), preferred_element_type=F32)


def _mod_kernel(c_ref, w_ref, b_ref, o_ref):
    c = c_ref[...]
    s = c * jax.nn.sigmoid(c)
    o_ref[...] = jnp.dot(s, w_ref[...], preferred_element_type=F32) + b_ref[...]


def modulation_vectors(cond, w_mod, b_mod):
    n = w_mod.shape[1]
    tn = 1536
    return pl.pallas_call(
        _mod_kernel,
        out_shape=jax.ShapeDtypeStruct((8, n), F32),
        grid=(n // tn,),
        in_specs=[pl.BlockSpec((8, D_MODEL), lambda j: (0, 0)),
                  pl.BlockSpec((D_MODEL, tn), lambda j: (0, j)),
                  pl.BlockSpec((1, tn), lambda j: (0, j))],
        out_specs=pl.BlockSpec((8, tn), lambda j: (0, j)),
        compiler_params=_cparams(("arbitrary",)),
        name="modulation",
    )(cond, w_mod, b_mod.reshape(1, n))


def _norm_proj_kernel(x_ref, pv_ref, w_ref, ws_ref, o_ref, os_ref, h_ref):
    @pl.when(pl.program_id(1) == 0)
    def _():
        pv = pv_ref[...]
        h = (_rms(x_ref[...]) * pv[0:1] * pv[1:2] + pv[2:3]).astype(BF16)
        h_ref[...] = h
        os_ref[...] = _dot(h, ws_ref[...])

    o_ref[...] = _dot(h_ref[...], w_ref[...]).astype(o_ref.dtype)


def norm_proj(x, pv, w, w_small, tn):
    m, d = x.shape
    n = w.shape[1]
    tm = min(m, 1024)
    return pl.pallas_call(
        _norm_proj_kernel,
        out_shape=(jax.ShapeDtypeStruct((m, n), BF16), jax.ShapeDtypeStruct((m, N_SMALL), F32)),
        grid=(m // tm, n // tn),
        in_specs=[pl.BlockSpec((tm, d), lambda i, j: (i, 0)),
                  pl.BlockSpec((8, d), lambda i, j: (0, 0)),
                  pl.BlockSpec((d, tn), lambda i, j: (0, j)),
                  pl.BlockSpec((d, N_SMALL), lambda i, j: (0, 0))],
        out_specs=(pl.BlockSpec((tm, tn), lambda i, j: (i, j)), pl.BlockSpec((tm, N_SMALL), lambda i, j: (i, 0))),
        scratch_shapes=[pltpu.VMEM((tm, d), BF16)],
        compiler_params=_cparams(("parallel", "arbitrary")),
        name="norm_proj",
    )(x, pv, w, w_small)


def _mla_proj_kernel(z_ref, cos_ref, sin_ref, gq_ref, gkv_ref, wq_ref, wkv_ref, q_ref, k_ref, v_ref):
    z = z_ref[...].astype(F32)
    tm = z.shape[0]
    cq = z[:, :MLA_Q_RANK]
    ckv = z[:, MLA_Q_RANK:MLA_Q_RANK + MLA_KV_RANK]
    kr = z[:, MLA_Q_RANK + MLA_KV_RANK:]
    cos = cos_ref[...]
    sin = sin_ref[...]
    qn = (_rms(cq) * gq_ref[...]).astype(BF16)
    kvn = (_rms(ckv) * gkv_ref[...]).astype(BF16)
    qf = _dot(qn, wq_ref[...]) * (MLA_QK ** -0.5 * math.log2(math.e))
    kvf = _dot(kvn, wkv_ref[...])
    kr_rot = kr[:, :MLA_ROPE] * cos + kr[:, MLA_ROPE:] * sin
    pad = jnp.zeros((tm, MLA_QK_PAD - MLA_QK), F32)
    lane = lax.broadcasted_iota(jnp.int32, (tm, MLA_V_PAD - HEAD_V), 1)
    ones_col = jnp.where(lane == 0, 1.0, 0.0).astype(BF16)
    for h in range(N_HEADS):
        b = h * 256
        q_rot = qf[:, b + 128:b + 192] * cos + qf[:, b + 192:b + 256] * sin
        q_ref[h] = jnp.concatenate([qf[:, b:b + 128], q_rot, pad], axis=-1).astype(BF16)
        k_ref[h] = jnp.concatenate([kvf[:, b:b + 128], kr_rot, pad], axis=-1).astype(BF16)
        v_ref[h] = jnp.concatenate([kvf[:, b + 128:b + 256].astype(BF16), ones_col], axis=-1)


def mla_project(zmain, cos, sin, gq, gkv, wq_ext, wkv):
    m = zmain.shape[0]
    tm = min(m, 1024)
    full = lambda shape: pl.BlockSpec(shape, lambda i: tuple(0 for _ in shape))
    return pl.pallas_call(
        _mla_proj_kernel,
        out_shape=(jax.ShapeDtypeStruct((N_HEADS, m, MLA_QK_PAD), BF16),
                   jax.ShapeDtypeStruct((N_HEADS, m, MLA_QK_PAD), BF16),
                   jax.ShapeDtypeStruct((N_HEADS, m, MLA_V_PAD), BF16)),
        grid=(m // tm,),
        in_specs=[pl.BlockSpec((tm, MLA_IN_W), lambda i: (i, OFF_MLA // MLA_IN_W)),
                  pl.BlockSpec((tm, MLA_ROPE), lambda i: (i, 0)),
                  pl.BlockSpec((tm, MLA_ROPE), lambda i: (i, 0)),
                  full((1, MLA_Q_RANK)), full((1, MLA_KV_RANK)),
                  full((MLA_Q_RANK, N_HEADS * 256)), full((MLA_KV_RANK, N_HEADS * 256))],
        out_specs=(pl.BlockSpec((N_HEADS, tm, MLA_QK_PAD), lambda i: (0, i, 0)),
                   pl.BlockSpec((N_HEADS, tm, MLA_QK_PAD), lambda i: (0, i, 0)),
                   pl.BlockSpec((N_HEADS, tm, MLA_V_PAD), lambda i: (0, i, 0))),
        compiler_params=_cparams(("parallel",)),
        name="mla_project",
    )(zmain, cos, sin, gq, gkv, wq_ext, wkv)


ATTN_TQ, ATTN_TK = 512, 512
ATTN_UNROLL = 32


def _attn_kernel(*refs, segs):
    q_ref, o_ref = refs[0], refs[-1]
    q = q_ref[0]
    tq = q.shape[0]
    carry = (jnp.full((tq, 1), -1e30, F32), jnp.zeros((tq, MLA_V_PAD), F32))
    for si, (tk, nk) in enumerate(segs):
        k_ref, v_ref = refs[1 + 2 * si], refs[2 + 2 * si]

        def body(c, carry, k_ref=k_ref, v_ref=v_ref, tk=tk):
            m, acc = carry
            start = pl.multiple_of(c * tk, tk)
            s = _dot_nt(q, k_ref[0, pl.ds(start, tk), :])
            m_new = jnp.maximum(m, jnp.max(s, axis=-1, keepdims=True))
            p = jnp.exp2(s - m_new).astype(BF16)
            acc = jnp.exp2(m - m_new) * acc + _dot(p, v_ref[0, pl.ds(start, tk), :])
            return m_new, acc

        carry = lax.fori_loop(0, nk, body, carry, unroll=min(ATTN_UNROLL, nk))
    _, acc = carry
    o_ref[...] = (acc[:, :HEAD_V] / acc[:, HEAD_V:HEAD_V + 1]).astype(o_ref.dtype)


def attention(q, kvs):
    _, lq, _ = q.shape
    tq = min(lq, ATTN_TQ)
    segs, args, specs = [], [], []
    for k, v in kvs:
        lk = k.shape[1]
        tk = min(lk, ATTN_TK)
        segs.append((tk, lk // tk))
        args += [k, v]
        specs += [pl.BlockSpec((1, lk, MLA_QK_PAD), lambda h, i: (h, 0, 0)),
                  pl.BlockSpec((1, lk, MLA_V_PAD), lambda h, i: (h, 0, 0))]
    return pl.pallas_call(
        functools.partial(_attn_kernel, segs=tuple(segs)),
        out_shape=jax.ShapeDtypeStruct((lq, N_HEADS * HEAD_V), BF16),
        grid=(N_HEADS, lq // tq),
        in_specs=[pl.BlockSpec((1, tq, MLA_QK_PAD), lambda h, i: (h, i, 0))] + specs,
        out_specs=pl.BlockSpec((tq, HEAD_V), lambda h, i: (i, h)),
        compiler_params=_cparams(("parallel", "arbitrary")),
        name="attention",
    )(q, *args)


def _head_post(o, gate, gain, centre, silu_gate):
    sig = jax.nn.sigmoid(gate)
    act = gate * sig if silu_gate else sig
    outs = []
    for h in range(N_HEADS):
        sl = slice(h * HEAD_V, (h + 1) * HEAD_V)
        oh = o[:, sl]
        if centre:
            oh = oh - jnp.mean(oh, axis=-1, keepdims=True)
        oh = oh * lax.rsqrt(jnp.mean(oh * oh, axis=-1, keepdims=True) + EPS)
        outs.append((oh * gain[:, sl] * act[:, sl]).astype(BF16))
    return jnp.concatenate(outs, axis=1)


def _merge_kernel(x_ref, g0_ref, g1_ref, g2_ref, g3_ref, af_ref, ab_ref, b_ref, cf_ref, cb_ref, df_ref, db_ref,
                  za_ref, zc_ref, zd_ref, gain_ref, wb_ref, wo_ref, g_ref, o_ref):
    gain = gain_ref[...]
    branches = (
        _head_post(af_ref[...] + ab_ref[...], za_ref[...].astype(F32), gain[0:1], False, True),
        b_ref[...],
        _head_post(cf_ref[...] + cb_ref[...], zc_ref[...].astype(F32), gain[1:2], False, False),
        _head_post(df_ref[...] + db_ref[...], zd_ref[...].astype(F32), gain[2:3], True, True),
    )
    s = None
    for n, (br, gate_ref) in enumerate(zip(branches, (g0_ref, g1_ref, g2_ref, g3_ref))):
        proj = _dot(br, wb_ref[n])
        gate = jax.nn.sigmoid(gate_ref[...].astype(F32))
        s = gate * proj if s is None else s + gate * proj
    m = _dot(s.astype(BF16), wo_ref[...])
    o_ref[...] = x_ref[...] + g_ref[0:1] * m


def merge_branches(x, zmain, gdn, attn, mlstm, ret, gains, w_branch, w_out, gvec):
    m = x.shape[0]
    tm = min(m, 512)
    row = lambda w: pl.BlockSpec((tm, w), lambda i: (i, 0))
    zcol = lambda off: pl.BlockSpec((tm, BRANCH_W), lambda i: (i, off // BRANCH_W))
    return pl.pallas_call(
        _merge_kernel,
        out_shape=jax.ShapeDtypeStruct((m, D_MODEL), F32),
        grid=(m // tm,),
        in_specs=[row(D_MODEL)]
                 + [pl.BlockSpec((tm, D_MODEL), lambda i, n=n: (i, OFF_GATE // D_MODEL + n)) for n in range(N_BRANCH)]
                 + [row(BRANCH_W)] * 7
                 + [zcol(OFF_GDN_Z), zcol(OFF_MLSTM_O), zcol(OFF_RET_G),
                    pl.BlockSpec((8, BRANCH_W), lambda i: (0, 0)),
                    pl.BlockSpec((N_BRANCH, BRANCH_W, D_MODEL), lambda i: (0, 0, 0)),
                    pl.BlockSpec((D_MODEL, D_MODEL), lambda i: (0, 0)),
                    pl.BlockSpec((8, D_MODEL), lambda i: (0, 0))],
        out_specs=row(D_MODEL),
        compiler_params=pltpu.CompilerParams(dimension_semantics=("parallel",), vmem_limit_bytes=MERGE_VMEM_LIMIT),
        name="merge_branches",
    )(x, zmain, zmain, zmain, zmain, gdn[0], gdn[1], attn, mlstm[0], mlstm[1], ret[0], ret[1],
      zmain, zmain, zmain, gains, w_branch, w_out, gvec)


def _ffn_kernel(x_ref, pv_ref, wg_ref, wu_ref, wd_ref, o_ref, h_ref):
    f = pl.program_id(1)

    def partial_out(h):
        g = _dot(h, wg_ref[...])
        u = _dot(h, wu_ref[...])
        act = (g * jax.nn.sigmoid(g) * u).astype(BF16)
        return pv_ref[3:4] * _dot(act, wd_ref[...])

    @pl.when(f == 0)
    def _():
        pv = pv_ref[...]
        h = (_rms(x_ref[...]) * pv[0:1] * pv[1:2] + pv[2:3]).astype(BF16)
        h_ref[...] = h
        o_ref[...] = x_ref[...] + partial_out(h)

    @pl.when(f > 0)
    def _():
        o_ref[...] += partial_out(h_ref[...])


def dense_ffn(x, pv, w_gu, w_down):
    m = x.shape[0]
    tm = min(m, 1024)
    tf = 512
    nf = D_FF // tf
    return pl.pallas_call(
        _ffn_kernel,
        out_shape=jax.ShapeDtypeStruct((m, D_MODEL), F32),
        grid=(m // tm, nf),
        in_specs=[pl.BlockSpec((tm, D_MODEL), lambda i, f: (i, 0)),
                  pl.BlockSpec((8, D_MODEL), lambda i, f: (0, 0)),
                  pl.BlockSpec((D_MODEL, tf), lambda i, f: (0, f)),
                  pl.BlockSpec((D_MODEL, tf), lambda i, f: (0, f + nf)),
                  pl.BlockSpec((tf, D_MODEL), lambda i, f: (f, 0))],
        out_specs=pl.BlockSpec((tm, D_MODEL), lambda i, f: (i, 0)),
        scratch_shapes=[pltpu.VMEM((tm, D_MODEL), BF16)],
        compiler_params=_cparams(("parallel", "arbitrary")),
        name="dense_ffn",
    )(x, pv, w_gu, w_gu, w_down)


def _router_kernel(x_ref, pv_ref, w_ref, b_ref, h_ref, route_ref):
    pv = pv_ref[...]
    h = _rms(x_ref[...]) * pv[0:1] * pv[1:2] + pv[2:3]
    h_hi = h.astype(BF16)
    h_lo = (h - h_hi.astype(F32)).astype(BF16)
    w = w_ref[...]
    w_hi = w.astype(BF16)
    w_lo = (w - w_hi.astype(F32)).astype(BF16)
    logits = _dot(h_hi, w_hi) + _dot(h_hi, w_lo) + _dot(h_lo, w_hi) + b_ref[...]
    h_ref[...] = h
    lane = lax.broadcasted_iota(jnp.int32, logits.shape, 1).astype(F32)
    m1 = jnp.max(logits, axis=1, keepdims=True)
    e1 = jnp.min(jnp.where(logits == m1, lane, 128.0), axis=1, keepdims=True)
    rest = jnp.where(lane == e1, -jnp.inf, logits)
    m2 = jnp.max(rest, axis=1, keepdims=True)
    e2 = jnp.min(jnp.where(rest == m2, lane, 128.0), axis=1, keepdims=True)
    z2 = jnp.exp(m2 - m1)
    w1 = 1.0 / (1.0 + z2)
    route_ref[...] = jnp.where(lane == 0.0, w1, jnp.where(lane == 1.0, z2 * w1, jnp.where(
        lane == 2.0, e1, jnp.where(lane == 3.0, e2, 0.0))))


def moe_router(x, pv, w_router_pad, b_router_pad):
    m = x.shape[0]
    tm = min(m, 1024)
    return pl.pallas_call(
        _router_kernel,
        out_shape=(jax.ShapeDtypeStruct((m, D_MODEL), F32), jax.ShapeDtypeStruct((m, 128), F32)),
        grid=(m // tm,),
        in_specs=[pl.BlockSpec((tm, D_MODEL), lambda i: (i, 0)),
                  pl.BlockSpec((8, D_MODEL), lambda i: (0, 0)),
                  pl.BlockSpec((D_MODEL, 128), lambda i: (0, 0)),
                  pl.BlockSpec((1, 128), lambda i: (0, 0))],
        out_specs=(pl.BlockSpec((tm, D_MODEL), lambda i: (i, 0)), pl.BlockSpec((tm, 128), lambda i: (i, 0))),
        compiler_params=_cparams(("parallel",)),
        name="moe_router",
    )(x, pv, w_router_pad, b_router_pad)


def _expert_kernel(be_ref, nb_ref, x_ref, wg_ref, wu_ref, wd_ref, o_ref, x16_ref):
    b = pl.program_id(0)
    f = pl.program_id(1)

    live = b < nb_ref[0]

    def partial_out(x):
        g = _dot(x, wg_ref[0].astype(BF16))
        u = _dot(x, wu_ref[0].astype(BF16))
        act = (g * jax.nn.sigmoid(g) * u).astype(BF16)
        return _dot(act, wd_ref[0].astype(BF16))

    @pl.when(live & (f == 0))
    def _():
        x = x_ref[...].astype(BF16)
        x16_ref[...] = x
        o_ref[...] = partial_out(x)

    @pl.when(live & (f > 0))
    def _():
        o_ref[...] += partial_out(x16_ref[...])

    @pl.when(jnp.logical_not(live) & (f == 0))
    def _():
        o_ref[...] = jnp.zeros_like(o_ref)


def expert_ffn(block_e, n_used, xb, w_gu, w_down):
    cap = xb.shape[0]
    nb = cap // MOE_ROWS
    tf = 512
    nf = D_FF // tf

    def live(b, nbr):
        return jnp.minimum(b, nbr[0] - 1)

    def fsel(b, f, nbr):
        return jnp.where(b < nbr[0], f, nf - 1)

    grid_spec = pltpu.PrefetchScalarGridSpec(
        num_scalar_prefetch=2,
        grid=(nb, nf),
        in_specs=[pl.BlockSpec((MOE_ROWS, D_MODEL), lambda b, f, be, nbr: (live(b, nbr), 0)),
                  pl.BlockSpec((1, D_MODEL, tf), lambda b, f, be, nbr: (be[live(b, nbr)], 0, fsel(b, f, nbr))),
                  pl.BlockSpec((1, D_MODEL, tf), lambda b, f, be, nbr: (be[live(b, nbr)], 0, fsel(b, f, nbr) + nf)),
                  pl.BlockSpec((1, tf, D_MODEL), lambda b, f, be, nbr: (be[live(b, nbr)], fsel(b, f, nbr), 0))],
        out_specs=pl.BlockSpec((MOE_ROWS, D_MODEL), lambda b, f, be, nbr: (b, 0)),
        scratch_shapes=[pltpu.VMEM((MOE_ROWS, D_MODEL), BF16)],
    )
    return pl.pallas_call(
        _expert_kernel,
        out_shape=jax.ShapeDtypeStruct((cap, D_MODEL), F32),
        grid_spec=grid_spec,
        compiler_params=_cparams(("arbitrary", "arbitrary")),
        name="expert_ffn",
    )(block_e, n_used, xb, w_gu, w_gu, w_down)


def _final_kernel(x_ref, y0_ref, y1_ref, w_ref, pv_ref, o_ref):
    w = w_ref[...]
    f = w[:, 0:1] * y0_ref[...] + w[:, 1:2] * y1_ref[...]
    x = x_ref[...] + pv_ref[0:1] * f
    o_ref[...] = _rms(x) * pv_ref[1:2]


def final_combine(x, y2, w, pv):
    m = x.shape[0]
    tm = min(m, 1024)
    row = lambda wd: pl.BlockSpec((tm, wd), lambda i: (i, 0))
    return pl.pallas_call(
        _final_kernel,
        out_shape=jax.ShapeDtypeStruct((m, D_MODEL), F32),
        grid=(m // tm,),
        in_specs=[row(D_MODEL), row(D_MODEL), pl.BlockSpec((tm, D_MODEL), lambda i: (i + m // tm, 0)), row(128),
                  pl.BlockSpec((8, D_MODEL), lambda i: (0, 0))],
        out_specs=row(D_MODEL),
        compiler_params=_cparams(("parallel",)),
        name="final_combine",
    )(x, y2, y2, w, pv)


CH = 128
NEG = -1e30


def _masks(reverse):
    r = lax.broadcasted_iota(jnp.int32, (CH, CH), 0)
    c = lax.broadcasted_iota(jnp.int32, (CH, CH), 1)
    return (r <= c, r < c) if reverse else (r >= c, r > c)


def _cumsum_time(incl, x):
    m = jnp.where(incl, 1.0, 0.0).astype(BF16)
    hi = x.astype(BF16)
    lo = (x - hi.astype(F32)).astype(BF16)
    return _dot(m, hi) + _dot(m, lo)


def _cummax_time(a, reverse):
    row = lax.broadcasted_iota(jnp.int32, a.shape, 0)
    k = 1
    while k < CH:
        if reverse:
            a = jnp.maximum(a, jnp.where(row < CH - k, pltpu.roll(a, CH - k, 0), NEG))
        else:
            a = jnp.maximum(a, jnp.where(row >= k, pltpu.roll(a, k, 0), NEG))
        k *= 2
    return a


SCAN_SUB = 4


def _scan_grid(l):
    rows = min(l, SCAN_SUB * CH)
    return l // rows, rows


def _sweep(d, nsub):
    return list(range(nsub)) if d == 0 else list(range(nsub - 1, -1, -1))


def _dir_specs(nsteps, rows, width, col_block):
    return [pl.BlockSpec((rows, width), lambda n: (n, col_block)),
            pl.BlockSpec((rows, width), lambda n: (nsteps - 1 - n, col_block))]


def _whole(a):
    return pl.BlockSpec(a.shape, lambda n: tuple(0 for _ in a.shape))


def _ret_consts():
    log_gamma = np.log(1.0 - 2.0 ** (-5.0 - np.arange(N_HEADS, dtype=np.float64)))
    pos = np.arange(CH, dtype=np.float64)
    diff = pos[:, None] - pos[None, :]
    dec_f = np.where(diff >= 0, np.exp(log_gamma[:, None, None] * diff), 0.0)
    dec = np.stack([dec_f, np.transpose(dec_f, (0, 2, 1))])
    qs_f = np.exp(log_gamma[None, :] * (pos[:, None] + 1.0))
    ks_f = np.exp(log_gamma[None, :] * (CH - 1.0 - pos[:, None]))
    qs_b = np.exp(log_gamma[None, :] * (CH - pos[:, None]))
    ks_b = np.exp(log_gamma[None, :] * pos[:, None])
    rep = lambda a: np.repeat(a, RET_DK, axis=1)
    qs = np.stack([rep(qs_f), rep(qs_b)])
    ks = np.stack([rep(ks_f), rep(ks_b)])
    chunk_decay = [float(np.exp(lg * CH)) for lg in log_gamma]
    return (jnp.asarray(dec, F32), jnp.asarray(qs, F32), jnp.asarray(ks, F32)), chunk_decay


def _ret_kernel(*refs, rotary, chunk_decay):
    if rotary:
        (qf, qb, kf, kb, vf, vb, cosf, cosb, sinf, sinb, dec_ref, qs_ref, ks_ref, s0_ref,
         of_ref, ob_ref, s_ref) = refs
        tabs = ((cosf, sinf), (cosb, sinb))
    else:
        qf, qb, kf, kb, vf, vb, dec_ref, qs_ref, ks_ref, s0_ref, of_ref, ob_ref, s_ref = refs
        tabs = (None, None)

    @pl.when(pl.program_id(0) == 0)
    def _():
        s_ref[...] = s0_ref[...]

    lane = lax.broadcasted_iota(jnp.int32, (CH, N_HEADS * RET_DK), 1)
    first_half = (lane & (RET_DK - 1)) < RET_DK // 2

    nsub = qf.shape[0] // CH
    chains = {}
    for d, (q_ref, k_ref, v_ref, o_ref) in enumerate(((qf, kf, vf, of_ref), (qb, kb, vb, ob_ref))):
        for j in range(nsub):
            rows = slice(j * CH, (j + 1) * CH)
            q = q_ref[rows, :].astype(F32)
            k = k_ref[rows, :].astype(F32) * RET_DK ** -0.5
            if rotary:
                cos, sin = tabs[d][0][rows, :], tabs[d][1][rows, :]

                def rot(x, cos=cos, sin=sin):
                    swapped = jnp.where(first_half, pltpu.roll(x, N_HEADS * RET_DK - RET_DK // 2, 1),
                                        pltpu.roll(x, RET_DK // 2, 1))
                    return x * cos + swapped * sin

                q, k = rot(q), rot(k)
            qb16, kb16 = q.astype(BF16), k.astype(BF16)
            q_in = (q * qs_ref[d]).astype(BF16)
            k_out = k * ks_ref[d]
            k_t = [k_out[:, :128].T, k_out[:, 128:].T]
            for h in range(N_HEADS):
                sl = slice(h * RET_DK, (h + 1) * RET_DK)
                chains[d, j, h] = dict(
                    o_ref=o_ref, rows=rows, vh=v_ref[rows, h * HEAD_V:(h + 1) * HEAD_V], q=qb16[:, sl], k=kb16[:, sl],
                    q_in=q_in[:, sl], k_th=k_t[h // 2][(h % 2) * RET_DK:(h % 2 + 1) * RET_DK, :].astype(BF16))

    for (d, j, h), c in chains.items():
        c["p"] = (_dot_nt(c["q"], c["k"]) * dec_ref[d, h]).astype(BF16)
    for c in chains.values():
        c["intra"] = _dot(c["p"], c["vh"])
        c["update"] = _dot(c["k_th"], c["vh"])
    states = {(d, h): s_ref[d, h] for d in range(2) for h in range(N_HEADS)}
    for i in range(nsub):
        for (d, h), s in list(states.items()):
            c = chains[d, _sweep(d, nsub)[i], h]
            c["o_ref"][c["rows"], h * HEAD_V:(h + 1) * HEAD_V] = c["intra"] + _dot(c["q_in"], s.astype(BF16))
            states[d, h] = chunk_decay[h] * s + c["update"]
    for (d, h), s in states.items():
        s_ref[d, h] = s


def retention_scan(z, s0, tables):
    l = z.shape[0]
    nsteps, rows = _scan_grid(l)
    consts, chunk_decay = _ret_consts()
    rotary = tables is not None
    w = N_HEADS * RET_DK
    specs = (_dir_specs(nsteps, rows, w, OFF_RET_Q // w) + _dir_specs(nsteps, rows, w, OFF_RET_K // w)
             + _dir_specs(nsteps, rows, BRANCH_W, OFF_RET_V // BRANCH_W))
    args = [z] * 6
    if rotary:
        specs += _dir_specs(nsteps, rows, w, 0) + _dir_specs(nsteps, rows, w, 0)
        args += [tables[0], tables[0], tables[1], tables[1]]
    specs += [_whole(c) for c in consts] + [_whole(s0)]
    args += list(consts) + [s0]
    return pl.pallas_call(
        functools.partial(_ret_kernel, rotary=rotary, chunk_decay=chunk_decay),
        out_shape=(jax.ShapeDtypeStruct((l, BRANCH_W), F32), jax.ShapeDtypeStruct((l, BRANCH_W), F32),
                   jax.ShapeDtypeStruct(s0.shape, F32)),
        grid=(nsteps,),
        in_specs=specs,
        out_specs=tuple(_dir_specs(nsteps, rows, BRANCH_W, 0)) + (_whole(s0),),
        compiler_params=_cparams(("arbitrary",)),
        name="retention_scan",
    )(*args)


def _mlstm_kernel(qf, qb, kf, kb, vf, vb, smf, smb, bias_ref, c0_ref, m0_ref, of_ref, ob_ref, c_ref, m_ref):
    @pl.when(pl.program_id(0) == 0)
    def _():
        c_ref[...] = c0_ref[...]
        m_ref[...] = m0_ref[...]

    lane = lax.broadcasted_iota(jnp.int32, (CH, 128), 1)
    is_forget = (lane >= F_LANE) & (lane < F_LANE + 2 * N_HEADS)
    ones_col = jnp.where(lane == 0, 1.0, 0.0).astype(BF16)

    nsub = qf.shape[0] // CH
    chains = {}
    for d, (q_ref, k_ref, v_ref, sm_ref, o_ref) in enumerate(((qf, kf, vf, smf, of_ref), (qb, kb, vb, smb, ob_ref))):
        reverse = d == 1
        incl, _ = _masks(reverse)
        last = 0 if reverse else CH - 1
        for j in range(nsub):
            rows = slice(j * CH, (j + 1) * CH)
            pre = sm_ref[rows, :] + bias_ref[0:1]
            x = jnp.where(is_forget, jax.nn.log_sigmoid(pre), pre)
            b = _cumsum_time(incl, x)
            a = pltpu.roll(x, F_LANE - I_LANE, 1) - b
            cm = _cummax_time(a, reverse)
            a_t = a.T
            q = (q_ref[rows, :].astype(F32) * MLSTM_DK ** -0.5).astype(BF16)
            kf32 = k_ref[rows, :].astype(F32)
            k_t = [kf32[:, :128].T, kf32[:, 128:].T]
            for h in range(N_HEADS):
                gf = F_LANE + d * N_HEADS + h
                sl = slice(h * MLSTM_DK, (h + 1) * MLSTM_DK)
                cm_col, a_row = cm[:, gf:gf + 1], a_t[gf:gf + 1, :]
                chains[d, j, h] = dict(
                    o_ref=o_ref, rows=rows, q=q[:, sl], k=k_ref[rows, sl], a_row=a_row, cm_col=cm_col,
                    b_col=b[:, gf:gf + 1], b_last=b[last:last + 1, gf:gf + 1], cm_last=cm[last:last + 1, gf:gf + 1],
                    w_intra=jnp.where(incl, jnp.exp(jnp.minimum(a_row - cm_col, 0.0)), 0.0),
                    k_th=k_t[h // 2][(h % 2) * MLSTM_DK:(h % 2 + 1) * MLSTM_DK, :],
                    v_aug=jnp.concatenate([v_ref[rows, h * HEAD_V:(h + 1) * HEAD_V], ones_col], axis=1))

    for c in chains.values():
        c["p"] = (c["w_intra"] * _dot_nt(c["q"], c["k"])).astype(BF16)
    for c in chains.values():
        c["pv"] = _dot(c["p"], c["v_aug"])
    c_augs = {(d, h): c_ref[d, h] for d in range(2) for h in range(N_HEADS)}
    m_ss = {(d, h): m_ref[d, h][0:1, 0:1] for d in range(2) for h in range(N_HEADS)}
    for i in range(nsub):
        cur = {(d, h): chains[d, _sweep(d, nsub)[i], h] for (d, h) in c_augs}
        qcs = {key: _dot(c["q"], c_augs[key].astype(BF16)) for key, c in cur.items()}
        tops = {key: jnp.maximum(m_ss[key], c["cm_last"]) for key, c in cur.items()}
        updates = {key: _dot((c["k_th"] * jnp.exp(c["a_row"] - tops[key])).astype(BF16), c["v_aug"])
                   for key, c in cur.items()}
        for (d, h), c in cur.items():
            m_s = m_ss[d, h]
            mx = jnp.maximum(m_s, c["cm_col"])
            tot = jnp.exp(m_s - mx) * qcs[d, h] + jnp.exp(c["cm_col"] - mx) * c["pv"]
            den = jnp.maximum(jnp.abs(tot[:, HEAD_V:HEAD_V + 1]), jnp.exp(-(c["b_col"] + mx)))
            c["o_ref"][c["rows"], h * HEAD_V:(h + 1) * HEAD_V] = tot[:, :HEAD_V] / den
        for key, c in cur.items():
            c_augs[key] = jnp.exp(m_ss[key] - tops[key]) * c_augs[key] + updates[key]
            m_ss[key] = c["b_last"] + tops[key]
    for (d, h) in c_augs:
        c_ref[d, h] = c_augs[d, h]
        m_ref[d, h] = jnp.broadcast_to(m_ss[d, h], (8, 128))


def mlstm_scan(z, small, bias, c0, m0):
    l = z.shape[0]
    nsteps, rows = _scan_grid(l)
    w = N_HEADS * MLSTM_DK
    specs = (_dir_specs(nsteps, rows, w, OFF_MLSTM_Q // w) + _dir_specs(nsteps, rows, w, OFF_MLSTM_K // w)
             + _dir_specs(nsteps, rows, BRANCH_W, OFF_MLSTM_V // BRANCH_W) + _dir_specs(nsteps, rows, 128, 0))
    specs += [_whole(bias), _whole(c0), _whole(m0)]
    return pl.pallas_call(
        _mlstm_kernel,
        out_shape=(jax.ShapeDtypeStruct((l, BRANCH_W), F32), jax.ShapeDtypeStruct((l, BRANCH_W), F32),
                   jax.ShapeDtypeStruct(c0.shape, F32), jax.ShapeDtypeStruct(m0.shape, F32)),
        grid=(nsteps,),
        in_specs=specs,
        out_specs=tuple(_dir_specs(nsteps, rows, BRANCH_W, 0)) + (_whole(c0), _whole(m0)),
        compiler_params=_cparams(("arbitrary",)),
        name="mlstm_scan",
    )(z, z, z, z, z, z, small, small, bias, c0, m0)


QKV_W = 3 * N_HEADS * GDN_DK
HALO = 8


def _gdn_prep_kernel(x_ref, prev_ref, next_ref, w_ref, q_ref, k_ref, v_ref):
    i = pl.program_id(0)
    tm = x_ref.shape[0]
    x = x_ref[...].astype(F32)
    prev = jnp.where(i > 0, prev_ref[...].astype(F32), 0.0)
    nxt = jnp.where(i < pl.num_programs(0) - 1, next_ref[...].astype(F32), 0.0)
    xe = jnp.concatenate([prev, x, nxt], axis=0)
    w = w_ref[...]
    y = None
    for tap in range(CONV_K):
        off = HALO + tap - CONV_K // 2
        term = w[tap:tap + 1] * xe[off:off + tm]
        y = term if y is None else y + term
    y = y * jax.nn.sigmoid(y)
    hw = N_HEADS * GDN_DK
    for h in range(N_HEADS):
        sl = slice(h * GDN_DK, (h + 1) * GDN_DK)
        qh = y[:, sl]
        kh = y[:, hw + h * GDN_DK:hw + (h + 1) * GDN_DK]
        q_ref[:, sl] = (qh * lax.rsqrt(jnp.sum(qh * qh, axis=-1, keepdims=True) + EPS) * GDN_DK ** -0.5).astype(BF16)
        k_ref[:, sl] = (kh * lax.rsqrt(jnp.sum(kh * kh, axis=-1, keepdims=True) + EPS)).astype(BF16)
    v_ref[...] = y[:, 2 * hw:].astype(BF16)


def gdn_prep(z, conv_w):
    l = z.shape[0]
    tm = min(l, 256)
    nb = l // tm
    r8 = tm // HALO
    cb = OFF_GDN_QKV // QKV_W
    w8 = jnp.concatenate([conv_w.astype(F32), jnp.zeros((8 - CONV_K, QKV_W), F32)], axis=0)
    return pl.pallas_call(
        _gdn_prep_kernel,
        out_shape=tuple(jax.ShapeDtypeStruct((l, BRANCH_W), BF16) for _ in range(3)),
        grid=(nb,),
        in_specs=[pl.BlockSpec((tm, QKV_W), lambda i: (i, cb)),
                  pl.BlockSpec((HALO, QKV_W), lambda i: (jnp.maximum(i * r8 - 1, 0), cb)),
                  pl.BlockSpec((HALO, QKV_W), lambda i: (jnp.minimum((i + 1) * r8, nb * r8 - 1), cb)),
                  pl.BlockSpec((8, QKV_W), lambda i: (0, 0))],
        out_specs=tuple(pl.BlockSpec((tm, BRANCH_W), lambda i: (i, 0)) for _ in range(3)),
        compiler_params=_cparams(("parallel",)),
        name="gdn_prep",
    )(z, z, z, w8)


N_LEVELS = 7


def _gdn_kernel(qf, qb, kf, kb, vf, vb, smf, smb, par_ref, s0_ref, of_ref, ob_ref, s_ref):
    @pl.when(pl.program_id(0) == 0)
    def _():
        s_ref[...] = s0_ref[...]

    lane = lax.broadcasted_iota(jnp.int32, (CH, 128), 1)
    is_decay = lane < B_LANE
    ri = lax.broadcasted_iota(jnp.int32, (CH, CH), 0)
    ci = lax.broadcasted_iota(jnp.int32, (CH, CH), 1)
    eye = jnp.where(ri == ci, 1.0, 0.0)
    pair_masks = [((ri >> (l + 1)) == (ci >> (l + 1))) & ((ri >> l) != (ci >> l)) for l in range(N_LEVELS)]

    nsub = qf.shape[0] // CH
    chains = {}
    for d, (q_ref, k_ref, v_ref, sm_ref, o_ref) in enumerate(((qf, kf, vf, smf, of_ref), (qb, kb, vb, smb, ob_ref))):
        reverse = d == 1
        incl, strict = _masks(reverse)
        last = 0 if reverse else CH - 1
        for j in range(nsub):
            rows = slice(j * CH, (j + 1) * CH)
            sm = sm_ref[rows, :]
            log_a = -jnp.exp(par_ref[1:2]) * jax.nn.softplus(sm + par_ref[0:1])
            x = jnp.where(is_decay, log_a, jax.nn.sigmoid(sm))
            g = _cumsum_time(incl, x)
            g_t = g.T
            for h in range(N_HEADS):
                ga, gb = A_LANE + d * N_HEADS + h, B_LANE + d * N_HEADS + h
                sl = slice(h * GDN_DK, (h + 1) * GDN_DK)
                g_col, g_row = g[:, ga:ga + 1], g_t[ga:ga + 1, :]
                g_last = g[last:last + 1, ga:ga + 1]
                beta = x[:, gb:gb + 1]
                decay = jnp.where(incl, jnp.exp(jnp.minimum(g_col - g_row, 0.0)), 0.0)
                kh = k_ref[rows, sl]
                kf32 = kh.astype(F32)
                kbeta = kf32 * beta
                a = jnp.where(strict, _dot_nt(kbeta.astype(BF16), kh) * decay, 0.0)
                e_g = jnp.exp(g_col)
                rhs = jnp.concatenate([v_ref[rows, sl].astype(F32) * beta, kbeta * e_g], axis=1)
                chains[d, j, h] = dict(
                    rows=rows, sl=sl, o_ref=o_ref, a=a, rhs=rhs, s_decay=jnp.exp(g_last),
                    qk=(_dot_nt(q_ref[rows, sl], kh) * decay).astype(BF16),
                    q_in=(q_ref[rows, sl].astype(F32) * e_g).astype(BF16),
                    k_out_t=(kf32.T * jnp.exp(g_last - g_row)).astype(BF16))

    cl = list(chains.values())
    xs = [eye - jnp.where(pair_masks[0], c["a"], 0.0) for c in cl]
    for pm in pair_masks[1:]:
        ys = [_dot(jnp.where(pm, c["a"], 0.0).astype(BF16), x.astype(BF16)) for c, x in zip(cl, xs)]
        xs = [x - _dot(x.astype(BF16), y.astype(BF16)) for x, y in zip(xs, ys)]
    for c, x in zip(cl, xs):
        c["sol"] = c["rhs"] + _dot((x - eye).astype(BF16), c["rhs"].astype(BF16))
    states = {(d, h): s_ref[d, h] for d in range(2) for h in range(N_HEADS)}
    for i in range(nsub):
        cur = {(d, h): chains[d, _sweep(d, nsub)[i], h] for (d, h) in states}
        s16s = {key: s.astype(BF16) for key, s in states.items()}
        v_news = {key: (c["sol"][:, :HEAD_V] - _dot(c["sol"][:, HEAD_V:].astype(BF16), s16s[key])).astype(BF16)
                  for key, c in cur.items()}
        updates = {key: _dot(c["k_out_t"], v_news[key]) for key, c in cur.items()}
        for key, c in cur.items():
            c["o_ref"][c["rows"], c["sl"]] = _dot(c["q_in"], s16s[key]) + _dot(c["qk"], v_news[key])
        for key, c in cur.items():
            states[key] = states[key] * c["s_decay"] + updates[key]
    for (d, h), s in states.items():
        s_ref[d, h] = s


def gdn_scan(qn, kn, vn, small, par, s0):
    l = qn.shape[0]
    nsteps, rows = _scan_grid(l)
    specs = _dir_specs(nsteps, rows, BRANCH_W, 0) * 3 + _dir_specs(nsteps, rows, 128, 0)
    specs += [_whole(par), _whole(s0)]
    return pl.pallas_call(
        _gdn_kernel,
        out_shape=(jax.ShapeDtypeStruct((l, BRANCH_W), F32), jax.ShapeDtypeStruct((l, BRANCH_W), F32),
                   jax.ShapeDtypeStruct(s0.shape, F32)),
        grid=(nsteps,),
        in_specs=specs,
        out_specs=tuple(_dir_specs(nsteps, rows, BRANCH_W, 0)) + (_whole(s0),),
        compiler_params=_cparams(("arbitrary",)),
        name="gdn_scan",
    )(qn, qn, kn, kn, vn, vn, small, small, par, s0)


def _lane_rows(entries):
    r = jnp.zeros((8, 128), F32)
    for row, lane, vals in entries:
        r = r.at[row, lane:lane + vals.shape[0]].set(vals.astype(F32))
    return r


def gdn_branch(zc, sc, zl, sl, conv_w, a_log, dt_bias):
    par = _lane_rows([(0, A_LANE, dt_bias.reshape(-1)), (1, A_LANE, a_log.reshape(-1))])
    s0 = jnp.zeros((2, N_HEADS, GDN_DK, HEAD_V), F32)
    ofc, obc, s1 = gdn_scan(*gdn_prep(zc, conv_w), sc, par, s0)
    ofl, obl, _ = gdn_scan(*gdn_prep(zl, conv_w), sl, par, s1)
    return (ofc, obc), (ofl, obl)


def mlstm_branch(zc, sc, zl, sl, gate_b):
    bias = _lane_rows([(0, I_LANE, gate_b[0].reshape(-1)), (0, F_LANE, gate_b[1].reshape(-1))])
    c0 = jnp.zeros((2, N_HEADS, MLSTM_DK, 2 * HEAD_V), F32)
    m0 = jnp.zeros((2, N_HEADS, 8, 128), F32)
    ofc, obc, c1, m1 = mlstm_scan(zc, sc, bias, c0, m0)
    ofl, obl, _, _ = mlstm_scan(zl, sl, bias, c1, m1)
    return (ofc, obc), (ofl, obl)


def retention_branch(zc, zl, tables):
    s0 = jnp.zeros((2, N_HEADS, RET_DK, HEAD_V), F32)
    ofc, obc, s1 = retention_scan(zc, s0, None)
    ofl, obl, _ = retention_scan(zl, s1, tables)
    return (ofc, obc), (ofl, obl)


def _ret_rope_tables(n_lat):
    inv = np.float32(ROPE_BASE) ** (-np.arange(0, RET_DK, 2, dtype=np.float32) / np.float32(RET_DK))
    ang = np.arange(n_lat, dtype=np.float32)[:, None] * inv[None, :]
    cos = np.concatenate([np.cos(ang), np.cos(ang)], axis=-1)
    sin = np.concatenate([-np.sin(ang), np.sin(ang)], axis=-1)
    return jnp.asarray(np.tile(cos, (1, N_HEADS)), F32), jnp.asarray(np.tile(sin, (1, N_HEADS)), F32)


def _rope_tables(n_lat):
    q = MLA_ROPE // 4
    inv = np.float32(ROPE_BASE) ** (-np.arange(0, 2 * q, 2, dtype=np.float32) / np.float32(2 * q))
    t = np.arange(n_lat)
    row = (t // GRID_W).astype(np.float32)[:, None] * inv[None, :]
    col = (t % GRID_W).astype(np.float32)[:, None] * inv[None, :]
    cos = np.concatenate([np.cos(row), np.cos(row), np.cos(col), np.cos(col)], axis=-1)
    sin = np.concatenate([-np.sin(row), np.sin(row), -np.sin(col), np.sin(col)], axis=-1)
    return jnp.asarray(cos, F32), jnp.asarray(sin, F32)


def _take_cols(w, perm):
    runs, start = [], 0
    for i in range(1, len(perm) + 1):
        if i == len(perm) or perm[i] != perm[i - 1] + 1:
            runs.append((int(perm[start]), int(perm[i - 1]) + 1))
            start = i
    return jnp.concatenate([w[:, a:b] for a, b in runs], axis=1)


def _pad_rows(v, n=8):
    rows = [jnp.reshape(r, (1, -1)).astype(F32) for r in v]
    d = rows[0].shape[1]
    return jnp.concatenate(rows + [jnp.zeros((n - len(rows), d), F32)], axis=0)


def _mla_weights(w_uq, w_ukv):
    wq = w_uq.reshape(MLA_Q_RANK, N_HEADS, MLA_QK)
    rope = wq[:, :, MLA_NOPE:]
    swapped = rope[:, :, _rope_swap(np.arange(MLA_ROPE))]
    wq_ext = jnp.concatenate([wq[:, :, :MLA_NOPE], rope, swapped], axis=-1).reshape(MLA_Q_RANK, N_HEADS * 256)
    return wq_ext.astype(BF16), w_ukv.astype(BF16)


def _moe(x_l, pv2, g2, w_router, b_router, w_gu, w_down, final_g):
    n = x_l.shape[0]
    w_pad = jnp.concatenate([w_router, jnp.zeros((D_MODEL, 128 - N_EXPERTS), F32)], axis=1)
    b_pad = jnp.concatenate([b_router, jnp.full((128 - N_EXPERTS,), -1e30, F32)]).reshape(1, 128)
    h2, route = moe_router(x_l, pv2, w_pad, b_pad)
    n_assign = n * TOP_K
    flat_e = jnp.concatenate([route[:, TOP_K + j] for j in range(TOP_K)]).astype(jnp.int32)
    onehot = (flat_e[:, None] == jnp.arange(N_EXPERTS)[None, :]).astype(jnp.int32)
    csum = jnp.cumsum(onehot, axis=0)
    counts = csum[-1]
    rank = jnp.take_along_axis(csum, flat_e[:, None], axis=1)[:, 0] - 1
    padded = (counts + MOE_ROWS - 1) // MOE_ROWS * MOE_ROWS
    pad_end = jnp.cumsum(padded)
    dest = (pad_end - padded)[flat_e] + rank
    nb = n_assign // MOE_ROWS + N_EXPERTS
    cap = nb * MOE_ROWS
    block_start = jnp.arange(nb, dtype=pad_end.dtype) * MOE_ROWS
    block_e = jnp.minimum(jnp.sum(block_start[:, None] >= pad_end[None, :], axis=1), N_EXPERTS - 1).astype(jnp.int32)
    n_used = (pad_end[-1] // MOE_ROWS).astype(jnp.int32).reshape(1)
    slot_tok = (jnp.arange(cap, dtype=jnp.int32) % n).at[dest].set(jnp.arange(n_assign, dtype=jnp.int32) % n)
    yb = expert_ffn(block_e, n_used, h2[slot_tok], w_gu, w_down)
    y2 = yb[dest]
    return final_combine(x_l, y2, route, _pad_rows([g2, final_g]))


def kernel(x, c, ctx, c_ctx, w_mod, b_mod, norm1_g, norm2_g, w_in, gdn_conv_w, gdn_a_log, gdn_dt_bias, gdn_norm_g, mla_q_norm_g, mla_kv_norm_g, mla_w_uq, mla_w_ukv, mlstm_gate_b, mlstm_norm_g, ret_norm_g, w_branch, w_out, ffn_w_in, ffn_w_down, moe_w_router, moe_b_router, moe_w_in, moe_w_down, final_norm_g):
    n_lat = x.shape[1]
    n_ctx = ctx.shape[1]
    x_l, x_c = x[0], ctx[0]
    cond = _pad_rows([c_ctx, c[0]])
    cos_l, sin_l = _rope_tables(n_lat)
    cos_c, sin_c = jnp.ones((n_ctx, MLA_ROPE), F32), jnp.zeros((n_ctx, MLA_ROPE), F32)
    ret_tables = _ret_rope_tables(n_lat)
    out = None
    for li in range(DEPTH):
        last = li == DEPTH - 1
        mod = modulation_vectors(cond, w_mod[li], b_mod[li])
        csh1, csc1, cg1, csh2, csc2, cg2 = jnp.split(mod[0], 6)
        sh1, sc1, g1, sh2, sc2, g2 = jnp.split(mod[1], 6)
        w_main = _take_cols(w_in[li], _MAIN_PERM).astype(BF16)
        w_small = _take_cols(w_in[li], _SMALL_PERM).astype(BF16)
        w_small = jnp.concatenate([w_small, jnp.zeros((D_MODEL, N_SMALL - w_small.shape[1]), BF16)], axis=1)
        pv_l = _pad_rows([norm1_g[li], 1 + sc1, sh1])
        pv_c = _pad_rows([norm1_g[li], 1 + csc1, csh1])
        zl, sl = norm_proj(x_l, pv_l, w_main, w_small, 1664)
        zc, sc = norm_proj(x_c, pv_c, w_main, w_small, 1664)

        a_c, a_l = gdn_branch(zc, sc, zl, sl, gdn_conv_w[li], gdn_a_log[li], gdn_dt_bias[li])
        c_c, c_l = mlstm_branch(zc, sc, zl, sl, mlstm_gate_b[li])
        d_c, d_l = retention_branch(zc, zl, ret_tables)
        gains = _pad_rows([gdn_norm_g[li], mlstm_norm_g[li], ret_norm_g[li]])

        wq_ext, wkv = _mla_weights(mla_w_uq[li], mla_w_ukv[li])
        gq, gkv = mla_q_norm_g[li].reshape(1, -1), mla_kv_norm_g[li].reshape(1, -1)
        ql, kl, vl = mla_project(zl, cos_l, sin_l, gq, gkv, wq_ext, wkv)
        qc, kc, vc = mla_project(zc, cos_c, sin_c, gq, gkv, wq_ext, wkv)
        b_l = attention(ql, [(kl, vl), (kc, vc)])

        wb = w_branch[li].astype(BF16)
        wo = w_out[li].astype(BF16)
        x_l = merge_branches(x_l, zl, a_l, b_l, c_l, d_l, gains, wb, wo, _pad_rows([g1]))
        if not last:
            b_c = attention(qc, [(kc, vc)])
            x_c = merge_branches(x_c, zc, a_c, b_c, c_c, d_c, gains, wb, wo, _pad_rows([cg1]))

        if li % 2 == 0:
            w_gu = ffn_w_in[li // 2].astype(BF16)
            w_dn = ffn_w_down[li // 2].astype(BF16)
            assert not last
            x_l = dense_ffn(x_l, _pad_rows([norm2_g[li], 1 + sc2, sh2, g2]), w_gu, w_dn)
            x_c = dense_ffn(x_c, _pad_rows([norm2_g[li], 1 + csc2, csh2, cg2]), w_gu, w_dn)
        else:
            assert last
            out = _moe(x_l, _pad_rows([norm2_g[li], 1 + sc2, sh2]), g2, moe_w_router[li // 2],
                       moe_b_router[li // 2], moe_w_in[li // 2], moe_w_down[li // 2], final_norm_g)
    return out[None]
```

```python
import functools
import math

import numpy as np
import jax
import jax.numpy as jnp
from jax import lax
from jax.experimental import pallas as pl
from jax.experimental.pallas import tpu as pltpu

F32 = jnp.float32
BF16 = jnp.bfloat16

D_MODEL = 1024
DEPTH = 2
GRID_W = 64
N_BRANCH = 4
N_HEADS = 4
HEAD_V = 128
BRANCH_W = N_HEADS * HEAD_V
GDN_DK = 128
CONV_K = 5
MLA_Q_RANK = 384
MLA_KV_RANK = 256
MLA_NOPE = 128
MLA_ROPE = 64
MLA_QK = MLA_NOPE + MLA_ROPE
MLA_QK_PAD = 256
MLA_V_PAD = 256
MLSTM_DK = 64
RET_DK = 64
ROPE_BASE = 10000.0
D_FF = 3584
N_EXPERTS = 8
TOP_K = 2
EPS = 1e-6

IN_WIDTHS = (
    N_BRANCH * D_MODEL,
    N_HEADS * GDN_DK, N_HEADS * GDN_DK, BRANCH_W, BRANCH_W, 2 * N_HEADS, 2 * N_HEADS,
    MLA_Q_RANK, MLA_KV_RANK, MLA_ROPE,
    N_HEADS * MLSTM_DK, N_HEADS * MLSTM_DK, BRANCH_W, BRANCH_W, 2 * N_HEADS, 2 * N_HEADS,
    N_HEADS * RET_DK, N_HEADS * RET_DK, BRANCH_W, BRANCH_W,
)
_IN_OFF = [0] + [int(o) for o in np.cumsum(IN_WIDTHS)]

VMEM_LIMIT = 48 * 1024 * 1024
MERGE_VMEM_LIMIT = 56 * 1024 * 1024
MOE_ROWS = 1024


def _cols(group):
    return np.arange(_IN_OFF[group], _IN_OFF[group + 1])


def _rope_swap(cols):
    q = MLA_ROPE // 4
    return np.concatenate([cols[q:2 * q], cols[:q], cols[3 * q:], cols[2 * q:3 * q]])


_MAIN_PERM = np.concatenate([
    _cols(1), _cols(2), _cols(3),
    _cols(4),
    _cols(0),
    _cols(12), _cols(13),
    _cols(10), _cols(11),
    _cols(18), _cols(19),
    _cols(16), _cols(17),
    _cols(7), _cols(8), _cols(9), _rope_swap(_cols(9)),
])
N_MAIN = int(_MAIN_PERM.shape[0])
_SMALL_PERM = np.concatenate([_cols(5), _cols(6), _cols(14), _cols(15)])
N_SMALL = 128
A_LANE, B_LANE, I_LANE, F_LANE = 0, 8, 16, 24
OFF_GDN_QKV, OFF_GDN_Z, OFF_GATE = 0, 1536, 2048
OFF_MLSTM_V, OFF_MLSTM_O, OFF_MLSTM_Q, OFF_MLSTM_K = 6144, 6656, 7168, 7424
OFF_RET_V, OFF_RET_G, OFF_RET_Q, OFF_RET_K = 7680, 8192, 8704, 8960
OFF_MLA = 9216
MLA_IN_W = 768


def _cparams(sem):
    return pltpu.CompilerParams(dimension_semantics=sem, vmem_limit_bytes=VMEM_LIMIT)


def _rms(x):
    return x * lax.rsqrt(jnp.mean(x * x, axis=-1, keepdims=True) + EPS)


def _dot(a, b):
    return jnp.dot(a, b, preferred_element_type=F32)


def _dot_nt(a, b):
    return lax.dot_general(a, b, (((1,), (1,)), ((), ())), preferred_element_type=F32)


def _mod_kernel(c_ref, w_ref, b_ref, o_ref):
    c = c_ref[...]
    s = c * jax.nn.sigmoid(c)
    o_ref[...] = jnp.dot(s, w_ref[...], preferred_element_type=F32) + b_ref[...]


def modulation_vectors(cond, w_mod, b_mod):
    n = w_mod.shape[1]
    tn = 1536
    return pl.pallas_call(
        _mod_kernel,
        out_shape=jax.ShapeDtypeStruct((8, n), F32),
        grid=(n // tn,),
        in_specs=[pl.BlockSpec((8, D_MODEL), lambda j: (0, 0)),
                  pl.BlockSpec((D_MODEL, tn), lambda j: (0, j)),
                  pl.BlockSpec((1, tn), lambda j: (0, j))],
        out_specs=pl.BlockSpec((8, tn), lambda j: (0, j)),
        compiler_params=_cparams(("arbitrary",)),
        name="modulation",
    )(cond, w_mod, b_mod.reshape(1, n))


def _norm_proj_kernel(x_ref, pv_ref, w_ref, ws_ref, o_ref, os_ref, h_ref):
    @pl.when(pl.program_id(1) == 0)
    def _():
        pv = pv_ref[...]
        h = (_rms(x_ref[...]) * pv[0:1] * pv[1:2] + pv[2:3]).astype(BF16)
        h_ref[...] = h
        os_ref[...] = _dot(h, ws_ref[...])

    o_ref[...] = _dot(h_ref[...], w_ref[...]).astype(o_ref.dtype)


def norm_proj(x, pv, w, w_small, tn):
    m, d = x.shape
    n = w.shape[1]
    tm = min(m, 1024)
    return pl.pallas_call(
        _norm_proj_kernel,
        out_shape=(jax.ShapeDtypeStruct((m, n), BF16), jax.ShapeDtypeStruct((m, N_SMALL), F32)),
        grid=(m // tm, n // tn),
        in_specs=[pl.BlockSpec((tm, d), lambda i, j: (i, 0)),
                  pl.BlockSpec((8, d), lambda i, j: (0, 0)),
                  pl.BlockSpec((d, tn), lambda i, j: (0, j)),
                  pl.BlockSpec((d, N_SMALL), lambda i, j: (0, 0))],
        out_specs=(pl.BlockSpec((tm, tn), lambda i, j: (i, j)), pl.BlockSpec((tm, N_SMALL), lambda i, j: (i, 0))),
        scratch_shapes=[pltpu.VMEM((tm, d), BF16)],
        compiler_params=_cparams(("parallel", "arbitrary")),
        name="norm_proj",
    )(x, pv, w, w_small)


def _mla_proj_kernel(z_ref, cos_ref, sin_ref, gq_ref, gkv_ref, wq_ref, wkv_ref, q_ref, k_ref, v_ref):
    z = z_ref[...].astype(F32)
    tm = z.shape[0]
    cq = z[:, :MLA_Q_RANK]
    ckv = z[:, MLA_Q_RANK:MLA_Q_RANK + MLA_KV_RANK]
    kr = z[:, MLA_Q_RANK + MLA_KV_RANK:]
    cos = cos_ref[...]
    sin = sin_ref[...]
    qn = (_rms(cq) * gq_ref[...]).astype(BF16)
    kvn = (_rms(ckv) * gkv_ref[...]).astype(BF16)
    qf = _dot(qn, wq_ref[...]) * (MLA_QK ** -0.5 * math.log2(math.e))
    kvf = _dot(kvn, wkv_ref[...])
    kr_rot = kr[:, :MLA_ROPE] * cos + kr[:, MLA_ROPE:] * sin
    pad = jnp.zeros((tm, MLA_QK_PAD - MLA_QK), F32)
    lane = lax.broadcasted_iota(jnp.int32, (tm, MLA_V_PAD - HEAD_V), 1)
    ones_col = jnp.where(lane == 0, 1.0, 0.0).astype(BF16)
    for h in range(N_HEADS):
        b = h * 256
        q_rot = qf[:, b + 128:b + 192] * cos + qf[:, b + 192:b + 256] * sin
        q_ref[h] = jnp.concatenate([qf[:, b:b + 128], q_rot, pad], axis=-1).astype(BF16)
        k_ref[h] = jnp.concatenate([kvf[:, b:b + 128], kr_rot, pad], axis=-1).astype(BF16)
        v_ref[h] = jnp.concatenate([kvf[:, b + 128:b + 256].astype(BF16), ones_col], axis=-1)


def mla_project(zmain, cos, sin, gq, gkv, wq_ext, wkv):
    m = zmain.shape[0]
    tm = min(m, 1024)
    full = lambda shape: pl.BlockSpec(shape, lambda i: tuple(0 for _ in shape))
    return pl.pallas_call(
        _mla_proj_kernel,
        out_shape=(jax.ShapeDtypeStruct((N_HEADS, m, MLA_QK_PAD), BF16),
                   jax.ShapeDtypeStruct((N_HEADS, m, MLA_QK_PAD), BF16),
                   jax.ShapeDtypeStruct((N_HEADS, m, MLA_V_PAD), BF16)),
        grid=(m // tm,),
        in_specs=[pl.BlockSpec((tm, MLA_IN_W), lambda i: (i, OFF_MLA // MLA_IN_W)),
                  pl.BlockSpec((tm, MLA_ROPE), lambda i: (i, 0)),
                  pl.BlockSpec((tm, MLA_ROPE), lambda i: (i, 0)),
                  full((1, MLA_Q_RANK)), full((1, MLA_KV_RANK)),
                  full((MLA_Q_RANK, N_HEADS * 256)), full((MLA_KV_RANK, N_HEADS * 256))],
        out_specs=(pl.BlockSpec((N_HEADS, tm, MLA_QK_PAD), lambda i: (0, i, 0)),
                   pl.BlockSpec((N_HEADS, tm, MLA_QK_PAD), lambda i: (0, i, 0)),
                   pl.BlockSpec((N_HEADS, tm, MLA_V_PAD), lambda i: (0, i, 0))),
        compiler_params=_cparams(("parallel",)),
        name="mla_project",
    )(zmain, cos, sin, gq, gkv, wq_ext, wkv)


ATTN_TQ, ATTN_TK = 512, 512
ATTN_UNROLL = 32


def _attn_kernel(*refs, segs):
    q_ref, o_ref = refs[0], refs[-1]
    q = q_ref[0]
    tq = q.shape[0]
    carry = (jnp.full((tq, 1), -1e30, F32), jnp.zeros((tq, MLA_V_PAD), F32))
    for si, (tk, nk) in enumerate(segs):
        k_ref, v_ref = refs[1 + 2 * si], refs[2 + 2 * si]

        def body(c, carry, k_ref=k_ref, v_ref=v_ref, tk=tk):
            m, acc = carry
            start = pl.multiple_of(c * tk, tk)
            s = _dot_nt(q, k_ref[0, pl.ds(start, tk), :])
            m_new = jnp.maximum(m, jnp.max(s, axis=-1, keepdims=True))
            p = jnp.exp2(s - m_new).astype(BF16)
            acc = jnp.exp2(m - m_new) * acc + _dot(p, v_ref[0, pl.ds(start, tk), :])
            return m_new, acc

        carry = lax.fori_loop(0, nk, body, carry, unroll=min(ATTN_UNROLL, nk))
    _, acc = carry
    o_ref[...] = (acc[:, :HEAD_V] / acc[:, HEAD_V:HEAD_V + 1]).astype(o_ref.dtype)


def attention(q, kvs):
    _, lq, _ = q.shape
    tq = min(lq, ATTN_TQ)
    segs, args, specs = [], [], []
    for k, v in kvs:
        lk = k.shape[1]
        tk = min(lk, ATTN_TK)
        segs.append((tk, lk // tk))
        args += [k, v]
        specs += [pl.BlockSpec((1, lk, MLA_QK_PAD), lambda h, i: (h, 0, 0)),
                  pl.BlockSpec((1, lk, MLA_V_PAD), lambda h, i: (h, 0, 0))]
    return pl.pallas_call(
        functools.partial(_attn_kernel, segs=tuple(segs)),
        out_shape=jax.ShapeDtypeStruct((lq, N_HEADS * HEAD_V), BF16),
        grid=(N_HEADS, lq // tq),
        in_specs=[pl.BlockSpec((1, tq, MLA_QK_PAD), lambda h, i: (h, i, 0))] + specs,
        out_specs=pl.BlockSpec((tq, HEAD_V), lambda h, i: (i, h)),
        compiler_params=_cparams(("parallel", "arbitrary")),
        name="attention",
    )(q, *args)


def _head_post(o, gate, gain, centre, silu_gate):
    sig = jax.nn.sigmoid(gate)
    act = gate * sig if silu_gate else sig
    outs = []
    for h in range(N_HEADS):
        sl = slice(h * HEAD_V, (h + 1) * HEAD_V)
        oh = o[:, sl]
        if centre:
            oh = oh - jnp.mean(oh, axis=-1, keepdims=True)
        oh = oh * lax.rsqrt(jnp.mean(oh * oh, axis=-1, keepdims=True) + EPS)
        outs.append((oh * gain[:, sl] * act[:, sl]).astype(BF16))
    return jnp.concatenate(outs, axis=1)


def _merge_kernel(x_ref, g0_ref, g1_ref, g2_ref, g3_ref, af_ref, ab_ref, b_ref, cf_ref, cb_ref, df_ref, db_ref,
                  za_ref, zc_ref, zd_ref, gain_ref, wb_ref, wo_ref, g_ref, o_ref):
    gain = gain_ref[...]
    branches = (
        _head_post(af_ref[...] + ab_ref[...], za_ref[...].astype(F32), gain[0:1], False, True),
        b_ref[...],
        _head_post(cf_ref[...] + cb_ref[...], zc_ref[...].astype(F32), gain[1:2], False, False),
        _head_post(df_ref[...] + db_ref[...], zd_ref[...].astype(F32), gain[2:3], True, True),
    )
    s = None
    for n, (br, gate_ref) in enumerate(zip(branches, (g0_ref, g1_ref, g2_ref, g3_ref))):
        proj = _dot(br, wb_ref[n])
        gate = jax.nn.sigmoid(gate_ref[...].astype(F32))
        s = gate * proj if s is None else s + gate * proj
    m = _dot(s.astype(BF16), wo_ref[...])
    o_ref[...] = x_ref[...] + g_ref[0:1] * m


def merge_branches(x, zmain, gdn, attn, mlstm, ret, gains, w_branch, w_out, gvec):
    m = x.shape[0]
    tm = min(m, 512)
    row = lambda w: pl.BlockSpec((tm, w), lambda i: (i, 0))
    zcol = lambda off: pl.BlockSpec((tm, BRANCH_W), lambda i: (i, off // BRANCH_W))
    return pl.pallas_call(
        _merge_kernel,
        out_shape=jax.ShapeDtypeStruct((m, D_MODEL), F32),
        grid=(m // tm,),
        in_specs=[row(D_MODEL)]
                 + [pl.BlockSpec((tm, D_MODEL), lambda i, n=n: (i, OFF_GATE // D_MODEL + n)) for n in range(N_BRANCH)]
                 + [row(BRANCH_W)] * 7
                 + [zcol(OFF_GDN_Z), zcol(OFF_MLSTM_O), zcol(OFF_RET_G),
                    pl.BlockSpec((8, BRANCH_W), lambda i: (0, 0)),
                    pl.BlockSpec((N_BRANCH, BRANCH_W, D_MODEL), lambda i: (0, 0, 0)),
                    pl.BlockSpec((D_MODEL, D_MODEL), lambda i: (0, 0)),
                    pl.BlockSpec((8, D_MODEL), lambda i: (0, 0))],
        out_specs=row(D_MODEL),
        compiler_params=pltpu.CompilerParams(dimension_semantics=("parallel",), vmem_limit_bytes=MERGE_VMEM_LIMIT),
        name="merge_branches",
    )(x, zmain, zmain, zmain, zmain, gdn[0], gdn[1], attn, mlstm[0], mlstm[1], ret[0], ret[1],
      zmain, zmain, zmain, gains, w_branch, w_out, gvec)


def _ffn_kernel(x_ref, pv_ref, wg_ref, wu_ref, wd_ref, o_ref, h_ref):
    f = pl.program_id(1)

    def partial_out(h):
        g = _dot(h, wg_ref[...])
        u = _dot(h, wu_ref[...])
        act = (g * jax.nn.sigmoid(g) * u).astype(BF16)
        return pv_ref[3:4] * _dot(act, wd_ref[...])

    @pl.when(f == 0)
    def _():
        pv = pv_ref[...]
        h = (_rms(x_ref[...]) * pv[0:1] * pv[1:2] + pv[2:3]).astype(BF16)
        h_ref[...] = h
        o_ref[...] = x_ref[...] + partial_out(h)

    @pl.when(f > 0)
    def _():
        o_ref[...] += partial_out(h_ref[...])


def dense_ffn(x, pv, w_gu, w_down):
    m = x.shape[0]
    tm = min(m, 1024)
    tf = 512
    nf = D_FF // tf
    return pl.pallas_call(
        _ffn_kernel,
        out_shape=jax.ShapeDtypeStruct((m, D_MODEL), F32),
        grid=(m // tm, nf),
        in_specs=[pl.BlockSpec((tm, D_MODEL), lambda i, f: (i, 0)),
                  pl.BlockSpec((8, D_MODEL), lambda i, f: (0, 0)),
                  pl.BlockSpec((D_MODEL, tf), lambda i, f: (0, f)),
                  pl.BlockSpec((D_MODEL, tf), lambda i, f: (0, f + nf)),
                  pl.BlockSpec((tf, D_MODEL), lambda i, f: (f, 0))],
        out_specs=pl.BlockSpec((tm, D_MODEL), lambda i, f: (i, 0)),
        scratch_shapes=[pltpu.VMEM((tm, D_MODEL), BF16)],
        compiler_params=_cparams(("parallel", "arbitrary")),
        name="dense_ffn",
    )(x, pv, w_gu, w_gu, w_down)


def _router_kernel(x_ref, pv_ref, w_ref, b_ref, h_ref, route_ref):
    pv = pv_ref[...]
    h = _rms(x_ref[...]) * pv[0:1] * pv[1:2] + pv[2:3]
    h_hi = h.astype(BF16)
    h_lo = (h - h_hi.astype(F32)).astype(BF16)
    w = w_ref[...]
    w_hi = w.astype(BF16)
    w_lo = (w - w_hi.astype(F32)).astype(BF16)
    logits = _dot(h_hi, w_hi) + _dot(h_hi, w_lo) + _dot(h_lo, w_hi) + b_ref[...]
    h_ref[...] = h
    lane = lax.broadcasted_iota(jnp.int32, logits.shape, 1).astype(F32)
    m1 = jnp.max(logits, axis=1, keepdims=True)
    e1 = jnp.min(jnp.where(logits == m1, lane, 128.0), axis=1, keepdims=True)
    rest = jnp.where(lane == e1, -jnp.inf, logits)
    m2 = jnp.max(rest, axis=1, keepdims=True)
    e2 = jnp.min(jnp.where(rest == m2, lane, 128.0), axis=1, keepdims=True)
    z2 = jnp.exp(m2 - m1)
    w1 = 1.0 / (1.0 + z2)
    route_ref[...] = jnp.where(lane == 0.0, w1, jnp.where(lane == 1.0, z2 * w1, jnp.where(
        lane == 2.0, e1, jnp.where(lane == 3.0, e2, 0.0))))


def moe_router(x, pv, w_router_pad, b_router_pad):
    m = x.shape[0]
    tm = min(m, 1024)
    return pl.pallas_call(
        _router_kernel,
        out_shape=(jax.ShapeDtypeStruct((m, D_MODEL), F32), jax.ShapeDtypeStruct((m, 128), F32)),
        grid=(m // tm,),
        in_specs=[pl.BlockSpec((tm, D_MODEL), lambda i: (i, 0)),
                  pl.BlockSpec((8, D_MODEL), lambda i: (0, 0)),
                  pl.BlockSpec((D_MODEL, 128), lambda i: (0, 0)),
                  pl.BlockSpec((1, 128), lambda i: (0, 0))],
        out_specs=(pl.BlockSpec((tm, D_MODEL), lambda i: (i, 0)), pl.BlockSpec((tm, 128), lambda i: (i, 0))),
        compiler_params=_cparams(("parallel",)),
        name="moe_router",
    )(x, pv, w_router_pad, b_router_pad)


def _expert_kernel(be_ref, nb_ref, x_ref, wg_ref, wu_ref, wd_ref, o_ref, x16_ref):
    b = pl.program_id(0)
    f = pl.program_id(1)

    live = b < nb_ref[0]

    def partial_out(x):
        g = _dot(x, wg_ref[0].astype(BF16))
        u = _dot(x, wu_ref[0].astype(BF16))
        act = (g * jax.nn.sigmoid(g) * u).astype(BF16)
        return _dot(act, wd_ref[0].astype(BF16))

    @pl.when(live & (f == 0))
    def _():
        x = x_ref[...].astype(BF16)
        x16_ref[...] = x
        o_ref[...] = partial_out(x)

    @pl.when(live & (f > 0))
    def _():
        o_ref[...] += partial_out(x16_ref[...])

    @pl.when(jnp.logical_not(live) & (f == 0))
    def _():
        o_ref[...] = jnp.zeros_like(o_ref)


def expert_ffn(block_e, n_used, xb, w_gu, w_down):
    cap = xb.shape[0]
    nb = cap // MOE_ROWS
    tf = 512
    nf = D_FF // tf

    def live(b, nbr):
        return jnp.minimum(b, nbr[0] - 1)

    def fsel(b, f, nbr):
        return jnp.where(b < nbr[0], f, nf - 1)

    grid_spec = pltpu.PrefetchScalarGridSpec(
        num_scalar_prefetch=2,
        grid=(nb, nf),
        in_specs=[pl.BlockSpec((MOE_ROWS, D_MODEL), lambda b, f, be, nbr: (live(b, nbr), 0)),
                  pl.BlockSpec((1, D_MODEL, tf), lambda b, f, be, nbr: (be[live(b, nbr)], 0, fsel(b, f, nbr))),
                  pl.BlockSpec((1, D_MODEL, tf), lambda b, f, be, nbr: (be[live(b, nbr)], 0, fsel(b, f, nbr) + nf)),
                  pl.BlockSpec((1, tf, D_MODEL), lambda b, f, be, nbr: (be[live(b, nbr)], fsel(b, f, nbr), 0))],
        out_specs=pl.BlockSpec((MOE_ROWS, D_MODEL), lambda b, f, be, nbr: (b, 0)),
        scratch_shapes=[pltpu.VMEM((MOE_ROWS, D_MODEL), BF16)],
    )
    return pl.pallas_call(
        _expert_kernel,
        out_shape=jax.ShapeDtypeStruct((cap, D_MODEL), F32),
        grid_spec=grid_spec,
        compiler_params=_cparams(("arbitrary", "arbitrary")),
        name="expert_ffn",
    )(block_e, n_used, xb, w_gu, w_gu, w_down)


def _final_kernel(x_ref, y0_ref, y1_ref, w_ref, pv_ref, o_ref):
    w = w_ref[...]
    f = w[:, 0:1] * y0_ref[...] + w[:, 1:2] * y1_ref[...]
    x = x_ref[...] + pv_ref[0:1] * f
    o_ref[...] = _rms(x) * pv_ref[1:2]


def final_combine(x, y2, w, pv):
    m = x.shape[0]
    tm = min(m, 1024)
    row = lambda wd: pl.BlockSpec((tm, wd), lambda i: (i, 0))
    return pl.pallas_call(
        _final_kernel,
        out_shape=jax.ShapeDtypeStruct((m, D_MODEL), F32),
        grid=(m // tm,),
        in_specs=[row(D_MODEL), row(D_MODEL), pl.BlockSpec((tm, D_MODEL), lambda i: (i + m // tm, 0)), row(128),
                  pl.BlockSpec((8, D_MODEL), lambda i: (0, 0))],
        out_specs=row(D_MODEL),
        compiler_params=_cparams(("parallel",)),
        name="final_combine",
    )(x, y2, y2, w, pv)


CH = 128
NEG = -1e30


def _masks(reverse):
    r = lax.broadcasted_iota(jnp.int32, (CH, CH), 0)
    c = lax.broadcasted_iota(jnp.int32, (CH, CH), 1)
    return (r <= c, r < c) if reverse else (r >= c, r > c)


def _cumsum_time(incl, x):
    m = jnp.where(incl, 1.0, 0.0).astype(BF16)
    hi = x.astype(BF16)
    lo = (x - hi.astype(F32)).astype(BF16)
    return _dot(m, hi) + _dot(m, lo)


def _cummax_time(a, reverse):
    row = lax.broadcasted_iota(jnp.int32, a.shape, 0)
    k = 1
    while k < CH:
        if reverse:
            a = jnp.maximum(a, jnp.where(row < CH - k, pltpu.roll(a, CH - k, 0), NEG))
        else:
            a = jnp.maximum(a, jnp.where(row >= k, pltpu.roll(a, k, 0), NEG))
        k *= 2
    return a


SCAN_SUB = 4


def _scan_grid(l):
    rows = min(l, SCAN_SUB * CH)
    return l // rows, rows


def _sweep(d, nsub):
    return list(range(nsub)) if d == 0 else list(range(nsub - 1, -1, -1))


def _dir_specs(nsteps, rows, width, col_block):
    return [pl.BlockSpec((rows, width), lambda n: (n, col_block)),
            pl.BlockSpec((rows, width), lambda n: (nsteps - 1 - n, col_block))]


def _whole(a):
    return pl.BlockSpec(a.shape, lambda n: tuple(0 for _ in a.shape))


def _ret_consts():
    log_gamma = np.log(1.0 - 2.0 ** (-5.0 - np.arange(N_HEADS, dtype=np.float64)))
    pos = np.arange(CH, dtype=np.float64)
    diff = pos[:, None] - pos[None, :]
    dec_f = np.where(diff >= 0, np.exp(log_gamma[:, None, None] * diff), 0.0)
    dec = np.stack([dec_f, np.transpose(dec_f, (0, 2, 1))])
    qs_f = np.exp(log_gamma[None, :] * (pos[:, None] + 1.0))
    ks_f = np.exp(log_gamma[None, :] * (CH - 1.0 - pos[:, None]))
    qs_b = np.exp(log_gamma[None, :] * (CH - pos[:, None]))
    ks_b = np.exp(log_gamma[None, :] * pos[:, None])
    rep = lambda a: np.repeat(a, RET_DK, axis=1)
    qs = np.stack([rep(qs_f), rep(qs_b)])
    ks = np.stack([rep(ks_f), rep(ks_b)])
    chunk_decay = [float(np.exp(lg * CH)) for lg in log_gamma]
    return (jnp.asarray(dec, F32), jnp.asarray(qs, F32), jnp.asarray(ks, F32)), chunk_decay


def _ret_kernel(*refs, rotary, chunk_decay):
    if rotary:
        (qf, qb, kf, kb, vf, vb, cosf, cosb, sinf, sinb, dec_ref, qs_ref, ks_ref, s0_ref,
         of_ref, ob_ref, s_ref) = refs
        tabs = ((cosf, sinf), (cosb, sinb))
    else:
        qf, qb, kf, kb, vf, vb, dec_ref, qs_ref, ks_ref, s0_ref, of_ref, ob_ref, s_ref = refs
        tabs = (None, None)

    @pl.when(pl.program_id(0) == 0)
    def _():
        s_ref[...] = s0_ref[...]

    lane = lax.broadcasted_iota(jnp.int32, (CH, N_HEADS * RET_DK), 1)
    first_half = (lane & (RET_DK - 1)) < RET_DK // 2

    nsub = qf.shape[0] // CH
    chains = {}
    for d, (q_ref, k_ref, v_ref, o_ref) in enumerate(((qf, kf, vf, of_ref), (qb, kb, vb, ob_ref))):
        for j in range(nsub):
            rows = slice(j * CH, (j + 1) * CH)
            q = q_ref[rows, :].astype(F32)
            k = k_ref[rows, :].astype(F32) * RET_DK ** -0.5
            if rotary:
                cos, sin = tabs[d][0][rows, :], tabs[d][1][rows, :]

                def rot(x, cos=cos, sin=sin):
                    swapped = jnp.where(first_half, pltpu.roll(x, N_HEADS * RET_DK - RET_DK // 2, 1),
                                        pltpu.roll(x, RET_DK // 2, 1))
                    return x * cos + swapped * sin

                q, k = rot(q), rot(k)
            qb16, kb16 = q.astype(BF16), k.astype(BF16)
            q_in = (q * qs_ref[d]).astype(BF16)
            k_out = k * ks_ref[d]
            k_t = [k_out[:, :128].T, k_out[:, 128:].T]
            for h in range(N_HEADS):
                sl = slice(h * RET_DK, (h + 1) * RET_DK)
                chains[d, j, h] = dict(
                    o_ref=o_ref, rows=rows, vh=v_ref[rows, h * HEAD_V:(h + 1) * HEAD_V], q=qb16[:, sl], k=kb16[:, sl],
                    q_in=q_in[:, sl], k_th=k_t[h // 2][(h % 2) * RET_DK:(h % 2 + 1) * RET_DK, :].astype(BF16))

    for (d, j, h), c in chains.items():
        c["p"] = (_dot_nt(c["q"], c["k"]) * dec_ref[d, h]).astype(BF16)
    for c in chains.values():
        c["intra"] = _dot(c["p"], c["vh"])
        c["update"] = _dot(c["k_th"], c["vh"])
    states = {(d, h): s_ref[d, h] for d in range(2) for h in range(N_HEADS)}
    for i in range(nsub):
        for (d, h), s in list(states.items()):
            c = chains[d, _sweep(d, nsub)[i], h]
            c["o_ref"][c["rows"], h * HEAD_V:(h + 1) * HEAD_V] = c["intra"] + _dot(c["q_in"], s.astype(BF16))
            states[d, h] = chunk_decay[h] * s + c["update"]
    for (d, h), s in states.items():
        s_ref[d, h] = s


def retention_scan(z, s0, tables):
    l = z.shape[0]
    nsteps, rows = _scan_grid(l)
    consts, chunk_decay = _ret_consts()
    rotary = tables is not None
    w = N_HEADS * RET_DK
    specs = (_dir_specs(nsteps, rows, w, OFF_RET_Q // w) + _dir_specs(nsteps, rows, w, OFF_RET_K // w)
             + _dir_specs(nsteps, rows, BRANCH_W, OFF_RET_V // BRANCH_W))
    args = [z] * 6
    if rotary:
        specs += _dir_specs(nsteps, rows, w, 0) + _dir_specs(nsteps, rows, w, 0)
        args += [tables[0], tables[0], tables[1], tables[1]]
    specs += [_whole(c) for c in consts] + [_whole(s0)]
    args += list(consts) + [s0]
    return pl.pallas_call(
        functools.partial(_ret_kernel, rotary=rotary, chunk_decay=chunk_decay),
        out_shape=(jax.ShapeDtypeStruct((l, BRANCH_W), F32), jax.ShapeDtypeStruct((l, BRANCH_W), F32),
                   jax.ShapeDtypeStruct(s0.shape, F32)),
        grid=(nsteps,),
        in_specs=specs,
        out_specs=tuple(_dir_specs(nsteps, rows, BRANCH_W, 0)) + (_whole(s0),),
        compiler_params=_cparams(("arbitrary",)),
        name="retention_scan",
    )(*args)


def _mlstm_kernel(qf, qb, kf, kb, vf, vb, smf, smb, bias_ref, c0_ref, m0_ref, of_ref, ob_ref, c_ref, m_ref):
    @pl.when(pl.program_id(0) == 0)
    def _():
        c_ref[...] = c0_ref[...]
        m_ref[...] = m0_ref[...]

    lane = lax.broadcasted_iota(jnp.int32, (CH, 128), 1)
    is_forget = (lane >= F_LANE) & (lane < F_LANE + 2 * N_HEADS)
    ones_col = jnp.where(lane == 0, 1.0, 0.0).astype(BF16)

    nsub = qf.shape[0] // CH
    chains = {}
    for d, (q_ref, k_ref, v_ref, sm_ref, o_ref) in enumerate(((qf, kf, vf, smf, of_ref), (qb, kb, vb, smb, ob_ref))):
        reverse = d == 1
        incl, _ = _masks(reverse)
        last = 0 if reverse else CH - 1
        for j in range(nsub):
            rows = slice(j * CH, (j + 1) * CH)
            pre = sm_ref[rows, :] + bias_ref[0:1]
            x = jnp.where(is_forget, jax.nn.log_sigmoid(pre), pre)
            b = _cumsum_time(incl, x)
            a = pltpu.roll(x, F_LANE - I_LANE, 1) - b
            cm = _cummax_time(a, reverse)
            a_t = a.T
            q = (q_ref[rows, :].astype(F32) * MLSTM_DK ** -0.5).astype(BF16)
            kf32 = k_ref[rows, :].astype(F32)
            k_t = [kf32[:, :128].T, kf32[:, 128:].T]
            for h in range(N_HEADS):
                gf = F_LANE + d * N_HEADS + h
                sl = slice(h * MLSTM_DK, (h + 1) * MLSTM_DK)
                cm_col, a_row = cm[:, gf:gf + 1], a_t[gf:gf + 1, :]
                chains[d, j, h] = dict(
                    o_ref=o_ref, rows=rows, q=q[:, sl], k=k_ref[rows, sl], a_row=a_row, cm_col=cm_col,
                    b_col=b[:, gf:gf + 1], b_last=b[last:last + 1, gf:gf + 1], cm_last=cm[last:last + 1, gf:gf + 1],
                    w_intra=jnp.where(incl, jnp.exp(jnp.minimum(a_row - cm_col, 0.0)), 0.0),
                    k_th=k_t[h // 2][(h % 2) * MLSTM_DK:(h % 2 + 1) * MLSTM_DK, :],
                    v_aug=jnp.concatenate([v_ref[rows, h * HEAD_V:(h + 1) * HEAD_V], ones_col], axis=1))

    for c in chains.values():
        c["p"] = (c["w_intra"] * _dot_nt(c["q"], c["k"])).astype(BF16)
    for c in chains.values():
        c["pv"] = _dot(c["p"], c["v_aug"])
    c_augs = {(d, h): c_ref[d, h] for d in range(2) for h in range(N_HEADS)}
    m_ss = {(d, h): m_ref[d, h][0:1, 0:1] for d in range(2) for h in range(N_HEADS)}
    for i in range(nsub):
        cur = {(d, h): chains[d, _sweep(d, nsub)[i], h] for (d, h) in c_augs}
        qcs = {key: _dot(c["q"], c_augs[key].astype(BF16)) for key, c in cur.items()}
        tops = {key: jnp.maximum(m_ss[key], c["cm_last"]) for key, c in cur.items()}
        updates = {key: _dot((c["k_th"] * jnp.exp(c["a_row"] - tops[key])).astype(BF16), c["v_aug"])
                   for key, c in cur.items()}
        for (d, h), c in cur.items():
            m_s = m_ss[d, h]
            mx = jnp.maximum(m_s, c["cm_col"])
            tot = jnp.exp(m_s - mx) * qcs[d, h] + jnp.exp(c["cm_col"] - mx) * c["pv"]
            den = jnp.maximum(jnp.abs(tot[:, HEAD_V:HEAD_V + 1]), jnp.exp(-(c["b_col"] + mx)))
            c["o_ref"][c["rows"], h * HEAD_V:(h + 1) * HEAD_V] = tot[:, :HEAD_V] / den
        for key, c in cur.items():
            c_augs[key] = jnp.exp(m_ss[key] - tops[key]) * c_augs[key] + updates[key]
            m_ss[key] = c["b_last"] + tops[key]
    for (d, h) in c_augs:
        c_ref[d, h] = c_augs[d, h]
        m_ref[d, h] = jnp.broadcast_to(m_ss[d, h], (8, 128))


def mlstm_scan(z, small, bias, c0, m0):
    l = z.shape[0]
    nsteps, rows = _scan_grid(l)
    w = N_HEADS * MLSTM_DK
    specs = (_dir_specs(nsteps, rows, w, OFF_MLSTM_Q // w) + _dir_specs(nsteps, rows, w, OFF_MLSTM_K // w)
             + _dir_specs(nsteps, rows, BRANCH_W, OFF_MLSTM_V // BRANCH_W) + _dir_specs(nsteps, rows, 128, 0))
    specs += [_whole(bias), _whole(c0), _whole(m0)]
    return pl.pallas_call(
        _mlstm_kernel,
        out_shape=(jax.ShapeDtypeStruct((l, BRANCH_W), F32), jax.ShapeDtypeStruct((l, BRANCH_W), F32),
                   jax.ShapeDtypeStruct(c0.shape, F32), jax.ShapeDtypeStruct(m0.shape, F32)),
        grid=(nsteps,),
        in_specs=specs,
        out_specs=tuple(_dir_specs(nsteps, rows, BRANCH_W, 0)) + (_whole(c0), _whole(m0)),
        compiler_params=_cparams(("arbitrary",)),
        name="mlstm_scan",
    )(z, z, z, z, z, z, small, small, bias, c0, m0)


QKV_W = 3 * N_HEADS * GDN_DK
HALO = 8


def _gdn_prep_kernel(x_ref, prev_ref, next_ref, w_ref, q_ref, k_ref, v_ref):
    i = pl.program_id(0)
    tm = x_ref.shape[0]
    x = x_ref[...].astype(F32)
    prev = jnp.where(i > 0, prev_ref[...].astype(F32), 0.0)
    nxt = jnp.where(i < pl.num_programs(0) - 1, next_ref[...].astype(F32), 0.0)
    xe = jnp.concatenate([prev, x, nxt], axis=0)
    w = w_ref[...]
    y = None
    for tap in range(CONV_K):
        off = HALO + tap - CONV_K // 2
        term = w[tap:tap + 1] * xe[off:off + tm]
        y = term if y is None else y + term
    y = y * jax.nn.sigmoid(y)
    hw = N_HEADS * GDN_DK
    for h in range(N_HEADS):
        sl = slice(h * GDN_DK, (h + 1) * GDN_DK)
        qh = y[:, sl]
        kh = y[:, hw + h * GDN_DK:hw + (h + 1) * GDN_DK]
        q_ref[:, sl] = (qh * lax.rsqrt(jnp.sum(qh * qh, axis=-1, keepdims=True) + EPS) * GDN_DK ** -0.5).astype(BF16)
        k_ref[:, sl] = (kh * lax.rsqrt(jnp.sum(kh * kh, axis=-1, keepdims=True) + EPS)).astype(BF16)
    v_ref[...] = y[:, 2 * hw:].astype(BF16)


def gdn_prep(z, conv_w):
    l = z.shape[0]
    tm = min(l, 256)
    nb = l // tm
    r8 = tm // HALO
    cb = OFF_GDN_QKV // QKV_W
    w8 = jnp.concatenate([conv_w.astype(F32), jnp.zeros((8 - CONV_K, QKV_W), F32)], axis=0)
    return pl.pallas_call(
        _gdn_prep_kernel,
        out_shape=tuple(jax.ShapeDtypeStruct((l, BRANCH_W), BF16) for _ in range(3)),
        grid=(nb,),
        in_specs=[pl.BlockSpec((tm, QKV_W), lambda i: (i, cb)),
                  pl.BlockSpec((HALO, QKV_W), lambda i: (jnp.maximum(i * r8 - 1, 0), cb)),
                  pl.BlockSpec((HALO, QKV_W), lambda i: (jnp.minimum((i + 1) * r8, nb * r8 - 1), cb)),
                  pl.BlockSpec((8, QKV_W), lambda i: (0, 0))],
        out_specs=tuple(pl.BlockSpec((tm, BRANCH_W), lambda i: (i, 0)) for _ in range(3)),
        compiler_params=_cparams(("parallel",)),
        name="gdn_prep",
    )(z, z, z, w8)


N_LEVELS = 7


def _gdn_kernel(qf, qb, kf, kb, vf, vb, smf, smb, par_ref, s0_ref, of_ref, ob_ref, s_ref):
    @pl.when(pl.program_id(0) == 0)
    def _():
        s_ref[...] = s0_ref[...]

    lane = lax.broadcasted_iota(jnp.int32, (CH, 128), 1)
    is_decay = lane < B_LANE
    ri = lax.broadcasted_iota(jnp.int32, (CH, CH), 0)
    ci = lax.broadcasted_iota(jnp.int32, (CH, CH), 1)
    eye = jnp.where(ri == ci, 1.0, 0.0)
    pair_masks = [((ri >> (l + 1)) == (ci >> (l + 1))) & ((ri >> l) != (ci >> l)) for l in range(N_LEVELS)]

    nsub = qf.shape[0] // CH
    chains = {}
    for d, (q_ref, k_ref, v_ref, sm_ref, o_ref) in enumerate(((qf, kf, vf, smf, of_ref), (qb, kb, vb, smb, ob_ref))):
        reverse = d == 1
        incl, strict = _masks(reverse)
        last = 0 if reverse else CH - 1
        for j in range(nsub):
            rows = slice(j * CH, (j + 1) * CH)
            sm = sm_ref[rows, :]
            log_a = -jnp.exp(par_ref[1:2]) * jax.nn.softplus(sm + par_ref[0:1])
            x = jnp.where(is_decay, log_a, jax.nn.sigmoid(sm))
            g = _cumsum_time(incl, x)
            g_t = g.T
            for h in range(N_HEADS):
                ga, gb = A_LANE + d * N_HEADS + h, B_LANE + d * N_HEADS + h
                sl = slice(h * GDN_DK, (h + 1) * GDN_DK)
                g_col, g_row = g[:, ga:ga + 1], g_t[ga:ga + 1, :]
                g_last = g[last:last + 1, ga:ga + 1]
                beta = x[:, gb:gb + 1]
                decay = jnp.where(incl, jnp.exp(jnp.minimum(g_col - g_row, 0.0)), 0.0)
                kh = k_ref[rows, sl]
                kf32 = kh.astype(F32)
                kbeta = kf32 * beta
                a = jnp.where(strict, _dot_nt(kbeta.astype(BF16), kh) * decay, 0.0)
                e_g = jnp.exp(g_col)
                rhs = jnp.concatenate([v_ref[rows, sl].astype(F32) * beta, kbeta * e_g], axis=1)
                chains[d, j, h] = dict(
                    rows=rows, sl=sl, o_ref=o_ref, a=a, rhs=rhs, s_decay=jnp.exp(g_last),
                    qk=(_dot_nt(q_ref[rows, sl], kh) * decay).astype(BF16),
                    q_in=(q_ref[rows, sl].astype(F32) * e_g).astype(BF16),
                    k_out_t=(kf32.T * jnp.exp(g_last - g_row)).astype(BF16))

    cl = list(chains.values())
    xs = [eye - jnp.where(pair_masks[0], c["a"], 0.0) for c in cl]
    for pm in pair_masks[1:]:
        ys = [_dot(jnp.where(pm, c["a"], 0.0).astype(BF16), x.astype(BF16)) for c, x in zip(cl, xs)]
        xs = [x - _dot(x.astype(BF16), y.astype(BF16)) for x, y in zip(xs, ys)]
    for c, x in zip(cl, xs):
        c["sol"] = c["rhs"] + _dot((x - eye).astype(BF16), c["rhs"].astype(BF16))
    states = {(d, h): s_ref[d, h] for d in range(2) for h in range(N_HEADS)}
    for i in range(nsub):
        cur = {(d, h): chains[d, _sweep(d, nsub)[i], h] for (d, h) in states}
        s16s = {key: s.astype(BF16) for key, s in states.items()}
        v_news = {key: (c["sol"][:, :HEAD_V] - _dot(c["sol"][:, HEAD_V:].astype(BF16), s16s[key])).astype(BF16)
                  for key, c in cur.items()}
        updates = {key: _dot(c["k_out_t"], v_news[key]) for key, c in cur.items()}
        for key, c in cur.items():
            c["o_ref"][c["rows"], c["sl"]] = _dot(c["q_in"], s16s[key]) + _dot(c["qk"], v_news[key])
        for key, c in cur.items():
            states[key] = states[key] * c["s_decay"] + updates[key]
    for (d, h), s in states.items():
        s_ref[d, h] = s


def gdn_scan(qn, kn, vn, small, par, s0):
    l = qn.shape[0]
    nsteps, rows = _scan_grid(l)
    specs = _dir_specs(nsteps, rows, BRANCH_W, 0) * 3 + _dir_specs(nsteps, rows, 128, 0)
    specs += [_whole(par), _whole(s0)]
    return pl.pallas_call(
        _gdn_kernel,
        out_shape=(jax.ShapeDtypeStruct((l, BRANCH_W), F32), jax.ShapeDtypeStruct((l, BRANCH_W), F32),
                   jax.ShapeDtypeStruct(s0.shape, F32)),
        grid=(nsteps,),
        in_specs=specs,
        out_specs=tuple(_dir_specs(nsteps, rows, BRANCH_W, 0)) + (_whole(s0),),
        compiler_params=_cparams(("arbitrary",)),
        name="gdn_scan",
    )(qn, qn, kn, kn, vn, vn, small, small, par, s0)


def _lane_rows(entries):
    r = jnp.zeros((8, 128), F32)
    for row, lane, vals in entries:
        r = r.at[row, lane:lane + vals.shape[0]].set(vals.astype(F32))
    return r


def gdn_branch(zc, sc, zl, sl, conv_w, a_log, dt_bias):
    par = _lane_rows([(0, A_LANE, dt_bias.reshape(-1)), (1, A_LANE, a_log.reshape(-1))])
    s0 = jnp.zeros((2, N_HEADS, GDN_DK, HEAD_V), F32)
    ofc, obc, s1 = gdn_scan(*gdn_prep(zc, conv_w), sc, par, s0)
    ofl, obl, _ = gdn_scan(*gdn_prep(zl, conv_w), sl, par, s1)
    return (ofc, obc), (ofl, obl)


def mlstm_branch(zc, sc, zl, sl, gate_b):
    bias = _lane_rows([(0, I_LANE, gate_b[0].reshape(-1)), (0, F_LANE, gate_b[1].reshape(-1))])
    c0 = jnp.zeros((2, N_HEADS, MLSTM_DK, 2 * HEAD_V), F32)
    m0 = jnp.zeros((2, N_HEADS, 8, 128), F32)
    ofc, obc, c1, m1 = mlstm_scan(zc, sc, bias, c0, m0)
    ofl, obl, _, _ = mlstm_scan(zl, sl, bias, c1, m1)
    return (ofc, obc), (ofl, obl)


def retention_branch(zc, zl, tables):
    s0 = jnp.zeros((2, N_HEADS, RET_DK, HEAD_V), F32)
    ofc, obc, s1 = retention_scan(zc, s0, None)
    ofl, obl, _ = retention_scan(zl, s1, tables)
    return (ofc, obc), (ofl, obl)


def _ret_rope_tables(n_lat):
    inv = np.float32(ROPE_BASE) ** (-np.arange(0, RET_DK, 2, dtype=np.float32) / np.float32(RET_DK))
    ang = np.arange(n_lat, dtype=np.float32)[:, None] * inv[None, :]
    cos = np.concatenate([np.cos(ang), np.cos(ang)], axis=-1)
    sin = np.concatenate([-np.sin(ang), np.sin(ang)], axis=-1)
    return jnp.asarray(np.tile(cos, (1, N_HEADS)), F32), jnp.asarray(np.tile(sin, (1, N_HEADS)), F32)


def _rope_tables(n_lat):
    q = MLA_ROPE // 4
    inv = np.float32(ROPE_BASE) ** (-np.arange(0, 2 * q, 2, dtype=np.float32) / np.float32(2 * q))
    t = np.arange(n_lat)
    row = (t // GRID_W).astype(np.float32)[:, None] * inv[None, :]
    col = (t % GRID_W).astype(np.float32)[:, None] * inv[None, :]
    cos = np.concatenate([np.cos(row), np.cos(row), np.cos(col), np.cos(col)], axis=-1)
    sin = np.concatenate([-np.sin(row), np.sin(row), -np.sin(col), np.sin(col)], axis=-1)
    return jnp.asarray(cos, F32), jnp.asarray(sin, F32)


def _take_cols(w, perm):
    runs, start = [], 0
    for i in range(1, len(perm) + 1):
        if i == len(perm) or perm[i] != perm[i - 1] + 1:
            runs.append((int(perm[start]), int(perm[i - 1]) + 1))
            start = i
    return jnp.concatenate([w[:, a:b].astype(BF16) for a, b in runs], axis=1)


def _pad_rows(v, n=8):
    rows = [jnp.reshape(r, (1, -1)).astype(F32) for r in v]
    d = rows[0].shape[1]
    return jnp.concatenate(rows + [jnp.zeros((n - len(rows), d), F32)], axis=0)


def _mla_weights(w_uq, w_ukv):
    wq = w_uq.reshape(MLA_Q_RANK, N_HEADS, MLA_QK)
    rope = wq[:, :, MLA_NOPE:]
    swapped = rope[:, :, _rope_swap(np.arange(MLA_ROPE))]
    wq_ext = jnp.concatenate([wq[:, :, :MLA_NOPE], rope, swapped], axis=-1).reshape(MLA_Q_RANK, N_HEADS * 256)
    return wq_ext.astype(BF16), w_ukv.astype(BF16)


def _moe(x_l, pv2, g2, w_router, b_router, w_gu, w_down, final_g):
    n = x_l.shape[0]
    w_pad = jnp.concatenate([w_router, jnp.zeros((D_MODEL, 128 - N_EXPERTS), F32)], axis=1)
    b_pad = jnp.concatenate([b_router, jnp.full((128 - N_EXPERTS,), -1e30, F32)]).reshape(1, 128)
    h2, route = moe_router(x_l, pv2, w_pad, b_pad)
    n_assign = n * TOP_K
    experts = jnp.arange(N_EXPERTS)[None, :]
    e0, e1 = route[:, TOP_K].astype(jnp.int32), route[:, TOP_K + 1].astype(jnp.int32)
    hot0, hot1 = (e0[:, None] == experts).astype(jnp.int32), (e1[:, None] == experts).astype(jnp.int32)
    c0, c1 = jnp.cumsum(hot0, axis=0), jnp.cumsum(hot1, axis=0)
    before1 = c1 - hot1
    counts = c0[-1] + c1[-1]
    rank0 = jnp.sum(hot0 * (c0 - hot0 + before1), axis=1)
    rank1 = jnp.sum(hot1 * (c0 + before1), axis=1)
    padded = (counts + MOE_ROWS - 1) // MOE_ROWS * MOE_ROWS
    pad_end = jnp.cumsum(padded)
    pad_start = pad_end - padded
    dest = jnp.concatenate([pad_start[e0] + rank0, pad_start[e1] + rank1])
    nb = n_assign // MOE_ROWS + N_EXPERTS
    cap = nb * MOE_ROWS
    block_start = jnp.arange(nb, dtype=pad_end.dtype) * MOE_ROWS
    block_e = jnp.minimum(jnp.sum(block_start[:, None] >= pad_end[None, :], axis=1), N_EXPERTS - 1).astype(jnp.int32)
    n_used = (pad_end[-1] // MOE_ROWS).astype(jnp.int32).reshape(1)
    slot_tok = (jnp.arange(cap, dtype=jnp.int32) % n).at[dest].set(jnp.arange(n_assign, dtype=jnp.int32) % n)
    yb = expert_ffn(block_e, n_used, h2[slot_tok], w_gu, w_down)
    y2 = yb[dest]
    return final_combine(x_l, y2, route, _pad_rows([g2, final_g]))


def kernel(x, c, ctx, c_ctx, w_mod, b_mod, norm1_g, norm2_g, w_in, gdn_conv_w, gdn_a_log, gdn_dt_bias, gdn_norm_g, mla_q_norm_g, mla_kv_norm_g, mla_w_uq, mla_w_ukv, mlstm_gate_b, mlstm_norm_g, ret_norm_g, w_branch, w_out, ffn_w_in, ffn_w_down, moe_w_router, moe_b_router, moe_w_in, moe_w_down, final_norm_g):
    n_lat = x.shape[1]
    n_ctx = ctx.shape[1]
    x_l, x_c = x[0], ctx[0]
    cond = _pad_rows([c_ctx, c[0]])
    cos_l, sin_l = _rope_tables(n_lat)
    cos_c, sin_c = jnp.ones((n_ctx, MLA_ROPE), F32), jnp.zeros((n_ctx, MLA_ROPE), F32)
    ret_tables = _ret_rope_tables(n_lat)
    out = None
    for li in range(DEPTH):
        last = li == DEPTH - 1
        mod = modulation_vectors(cond, w_mod[li], b_mod[li])
        csh1, csc1, cg1, csh2, csc2, cg2 = jnp.split(mod[0], 6)
        sh1, sc1, g1, sh2, sc2, g2 = jnp.split(mod[1], 6)
        w_main = _take_cols(w_in[li], _MAIN_PERM)
        w_small = _take_cols(w_in[li], _SMALL_PERM)
        w_small = jnp.concatenate([w_small, jnp.zeros((D_MODEL, N_SMALL - w_small.shape[1]), BF16)], axis=1)
        pv_l = _pad_rows([norm1_g[li], 1 + sc1, sh1])
        pv_c = _pad_rows([norm1_g[li], 1 + csc1, csh1])
        zl, sl = norm_proj(x_l, pv_l, w_main, w_small, 1664)
        zc, sc = norm_proj(x_c, pv_c, w_main, w_small, 1664)

        a_c, a_l = gdn_branch(zc, sc, zl, sl, gdn_conv_w[li], gdn_a_log[li], gdn_dt_bias[li])
        c_c, c_l = mlstm_branch(zc, sc, zl, sl, mlstm_gate_b[li])
        d_c, d_l = retention_branch(zc, zl, ret_tables)
        gains = _pad_rows([gdn_norm_g[li], mlstm_norm_g[li], ret_norm_g[li]])

        wq_ext, wkv = _mla_weights(mla_w_uq[li], mla_w_ukv[li])
        gq, gkv = mla_q_norm_g[li].reshape(1, -1), mla_kv_norm_g[li].reshape(1, -1)
        ql, kl, vl = mla_project(zl, cos_l, sin_l, gq, gkv, wq_ext, wkv)
        qc, kc, vc = mla_project(zc, cos_c, sin_c, gq, gkv, wq_ext, wkv)
        b_l = attention(ql, [(kl, vl), (kc, vc)])

        wb = w_branch[li].astype(BF16)
        wo = w_out[li].astype(BF16)
        x_l = merge_branches(x_l, zl, a_l, b_l, c_l, d_l, gains, wb, wo, _pad_rows([g1]))
        if not last:
            b_c = attention(qc, [(kc, vc)])
            x_c = merge_branches(x_c, zc, a_c, b_c, c_c, d_c, gains, wb, wo, _pad_rows([cg1]))

        if li % 2 == 0:
            w_gu = ffn_w_in[li // 2].astype(BF16)
            w_dn = ffn_w_down[li // 2].astype(BF16)
            assert not last
            x_l = dense_ffn(x_l, _pad_rows([norm2_g[li], 1 + sc2, sh2, g2]), w_gu, w_dn)
            x_c = dense_ffn(x_c, _pad_rows([norm2_g[li], 1 + csc2, csh2, cg2]), w_gu, w_dn)
        else:
            assert last
            out = _moe(x_l, _pad_rows([norm2_g[li], 1 + sc2, sh2]), g2, moe_w_router[li // 2],
                       moe_b_router[li // 2], moe_w_in[li // 2], moe_w_down[li // 2], final_norm_g)
    return out[None]
```

```python
import functools
import math

import numpy as np
import jax
import jax.numpy as jnp
from jax import lax
from jax.experimental import pallas as pl
from jax.experimental.pallas import tpu as pltpu
from jax.experimental.pallas import tpu_sc as plsc

F32 = jnp.float32
BF16 = jnp.bfloat16

D_MODEL = 1024
DEPTH = 2
GRID_W = 64
N_BRANCH = 4
N_HEADS = 4
HEAD_V = 128
BRANCH_W = N_HEADS * HEAD_V
GDN_DK = 128
CONV_K = 5
MLA_Q_RANK = 384
MLA_KV_RANK = 256
MLA_NOPE = 128
MLA_ROPE = 64
MLA_QK = MLA_NOPE + MLA_ROPE
MLA_QK_PAD = 256
MLA_V_PAD = 256
MLSTM_DK = 64
RET_DK = 64
ROPE_BASE = 10000.0
D_FF = 3584
N_EXPERTS = 8
TOP_K = 2
EPS = 1e-6

IN_WIDTHS = (
    N_BRANCH * D_MODEL,
    N_HEADS * GDN_DK, N_HEADS * GDN_DK, BRANCH_W, BRANCH_W, 2 * N_HEADS, 2 * N_HEADS,
    MLA_Q_RANK, MLA_KV_RANK, MLA_ROPE,
    N_HEADS * MLSTM_DK, N_HEADS * MLSTM_DK, BRANCH_W, BRANCH_W, 2 * N_HEADS, 2 * N_HEADS,
    N_HEADS * RET_DK, N_HEADS * RET_DK, BRANCH_W, BRANCH_W,
)
_IN_OFF = [0] + [int(o) for o in np.cumsum(IN_WIDTHS)]

VMEM_LIMIT = 48 * 1024 * 1024
MERGE_VMEM_LIMIT = 56 * 1024 * 1024
MOE_ROWS = 1024


def _cols(group):
    return np.arange(_IN_OFF[group], _IN_OFF[group + 1])


def _rope_swap(cols):
    q = MLA_ROPE // 4
    return np.concatenate([cols[q:2 * q], cols[:q], cols[3 * q:], cols[2 * q:3 * q]])


_MAIN_PERM = np.concatenate([
    _cols(1), _cols(2), _cols(3),
    _cols(4),
    _cols(0),
    _cols(12), _cols(13),
    _cols(10), _cols(11),
    _cols(18), _cols(19),
    _cols(16), _cols(17),
    _cols(7), _cols(8), _cols(9), _rope_swap(_cols(9)),
])
N_MAIN = int(_MAIN_PERM.shape[0])
_SMALL_PERM = np.concatenate([_cols(5), _cols(6), _cols(14), _cols(15)])
N_SMALL = 128
A_LANE, B_LANE, I_LANE, F_LANE = 0, 8, 16, 24
OFF_GDN_QKV, OFF_GDN_Z, OFF_GATE = 0, 1536, 2048
OFF_MLSTM_V, OFF_MLSTM_O, OFF_MLSTM_Q, OFF_MLSTM_K = 6144, 6656, 7168, 7424
OFF_RET_V, OFF_RET_G, OFF_RET_Q, OFF_RET_K = 7680, 8192, 8704, 8960
OFF_MLA = 9216
MLA_IN_W = 768


def _cparams(sem):
    return pltpu.CompilerParams(dimension_semantics=sem, vmem_limit_bytes=VMEM_LIMIT)


def _rms(x):
    return x * lax.rsqrt(jnp.mean(x * x, axis=-1, keepdims=True) + EPS)


def _dot(a, b):
    return jnp.dot(a, b, preferred_element_type=F32)


def _dot_nt(a, b):
    return lax.dot_general(a, b, (((1,), (1,)), ((), ())), preferred_element_type=F32)


def _mod_kernel(c_ref, w_ref, b_ref, o_ref):
    c = c_ref[...]
    s = c * jax.nn.sigmoid(c)
    o_ref[...] = jnp.dot(s, w_ref[...], preferred_element_type=F32) + b_ref[...]


def modulation_vectors(cond, w_mod, b_mod):
    n = w_mod.shape[1]
    tn = 1536
    return pl.pallas_call(
        _mod_kernel,
        out_shape=jax.ShapeDtypeStruct((8, n), F32),
        grid=(n // tn,),
        in_specs=[pl.BlockSpec((8, D_MODEL), lambda j: (0, 0)),
                  pl.BlockSpec((D_MODEL, tn), lambda j: (0, j)),
                  pl.BlockSpec((1, tn), lambda j: (0, j))],
        out_specs=pl.BlockSpec((8, tn), lambda j: (0, j)),
        compiler_params=_cparams(("arbitrary",)),
        name="modulation",
    )(cond, w_mod, b_mod.reshape(1, n))


def _norm_proj_kernel(x_ref, pv_ref, w_ref, ws_ref, o_ref, os_ref, h_ref):
    @pl.when(pl.program_id(1) == 0)
    def _():
        pv = pv_ref[...]
        h = (_rms(x_ref[...]) * pv[0:1] * pv[1:2] + pv[2:3]).astype(BF16)
        h_ref[...] = h
        os_ref[...] = _dot(h, ws_ref[...])

    o_ref[...] = _dot(h_ref[...], w_ref[...]).astype(o_ref.dtype)


def norm_proj(x, pv, w, w_small, tn):
    m, d = x.shape
    n = w.shape[1]
    tm = min(m, 1024)
    return pl.pallas_call(
        _norm_proj_kernel,
        out_shape=(jax.ShapeDtypeStruct((m, n), BF16), jax.ShapeDtypeStruct((m, N_SMALL), F32)),
        grid=(m // tm, n // tn),
        in_specs=[pl.BlockSpec((tm, d), lambda i, j: (i, 0)),
                  pl.BlockSpec((8, d), lambda i, j: (0, 0)),
                  pl.BlockSpec((d, tn), lambda i, j: (0, j)),
                  pl.BlockSpec((d, N_SMALL), lambda i, j: (0, 0))],
        out_specs=(pl.BlockSpec((tm, tn), lambda i, j: (i, j)), pl.BlockSpec((tm, N_SMALL), lambda i, j: (i, 0))),
        scratch_shapes=[pltpu.VMEM((tm, d), BF16)],
        compiler_params=_cparams(("parallel", "arbitrary")),
        name="norm_proj",
    )(x, pv, w, w_small)


def _mla_proj_kernel(z_ref, cos_ref, sin_ref, gq_ref, gkv_ref, wq_ref, wkv_ref, q_ref, k_ref, v_ref):
    z = z_ref[...].astype(F32)
    tm = z.shape[0]
    cq = z[:, :MLA_Q_RANK]
    ckv = z[:, MLA_Q_RANK:MLA_Q_RANK + MLA_KV_RANK]
    kr = z[:, MLA_Q_RANK + MLA_KV_RANK:]
    cos = cos_ref[...]
    sin = sin_ref[...]
    qn = (_rms(cq) * gq_ref[...]).astype(BF16)
    kvn = (_rms(ckv) * gkv_ref[...]).astype(BF16)
    qf = _dot(qn, wq_ref[...]) * (MLA_QK ** -0.5 * math.log2(math.e))
    kvf = _dot(kvn, wkv_ref[...])
    kr_rot = kr[:, :MLA_ROPE] * cos + kr[:, MLA_ROPE:] * sin
    pad = jnp.zeros((tm, MLA_QK_PAD - MLA_QK), F32)
    lane = lax.broadcasted_iota(jnp.int32, (tm, MLA_V_PAD - HEAD_V), 1)
    ones_col = jnp.where(lane == 0, 1.0, 0.0).astype(BF16)
    for h in range(N_HEADS):
        b = h * 256
        q_rot = qf[:, b + 128:b + 192] * cos + qf[:, b + 192:b + 256] * sin
        q_ref[h] = jnp.concatenate([qf[:, b:b + 128], q_rot, pad], axis=-1).astype(BF16)
        k_ref[h] = jnp.concatenate([kvf[:, b:b + 128], kr_rot, pad], axis=-1).astype(BF16)
        v_ref[h] = jnp.concatenate([kvf[:, b + 128:b + 256].astype(BF16), ones_col], axis=-1)


def mla_project(zmain, cos, sin, gq, gkv, wq_ext, wkv):
    m = zmain.shape[0]
    tm = min(m, 1024)
    full = lambda shape: pl.BlockSpec(shape, lambda i: tuple(0 for _ in shape))
    return pl.pallas_call(
        _mla_proj_kernel,
        out_shape=(jax.ShapeDtypeStruct((N_HEADS, m, MLA_QK_PAD), BF16),
                   jax.ShapeDtypeStruct((N_HEADS, m, MLA_QK_PAD), BF16),
                   jax.ShapeDtypeStruct((N_HEADS, m, MLA_V_PAD), BF16)),
        grid=(m // tm,),
        in_specs=[pl.BlockSpec((tm, MLA_IN_W), lambda i: (i, OFF_MLA // MLA_IN_W)),
                  pl.BlockSpec((tm, MLA_ROPE), lambda i: (i, 0)),
                  pl.BlockSpec((tm, MLA_ROPE), lambda i: (i, 0)),
                  full((1, MLA_Q_RANK)), full((1, MLA_KV_RANK)),
                  full((MLA_Q_RANK, N_HEADS * 256)), full((MLA_KV_RANK, N_HEADS * 256))],
        out_specs=(pl.BlockSpec((N_HEADS, tm, MLA_QK_PAD), lambda i: (0, i, 0)),
                   pl.BlockSpec((N_HEADS, tm, MLA_QK_PAD), lambda i: (0, i, 0)),
                   pl.BlockSpec((N_HEADS, tm, MLA_V_PAD), lambda i: (0, i, 0))),
        compiler_params=_cparams(("parallel",)),
        name="mla_project",
    )(zmain, cos, sin, gq, gkv, wq_ext, wkv)


ATTN_TQ, ATTN_TK = 512, 512
ATTN_UNROLL = 32


def _attn_kernel(*refs, segs):
    q_ref, o_ref = refs[0], refs[-1]
    q = q_ref[0]
    tq = q.shape[0]
    carry = (jnp.full((tq, 1), -1e30, F32), jnp.zeros((tq, MLA_V_PAD), F32))
    for si, (tk, nk) in enumerate(segs):
        k_ref, v_ref = refs[1 + 2 * si], refs[2 + 2 * si]

        def body(c, carry, k_ref=k_ref, v_ref=v_ref, tk=tk):
            m, acc = carry
            start = pl.multiple_of(c * tk, tk)
            s = _dot_nt(q, k_ref[0, pl.ds(start, tk), :])
            m_new = jnp.maximum(m, jnp.max(s, axis=-1, keepdims=True))
            p = jnp.exp2(s - m_new).astype(BF16)
            acc = jnp.exp2(m - m_new) * acc + _dot(p, v_ref[0, pl.ds(start, tk), :])
            return m_new, acc

        carry = lax.fori_loop(0, nk, body, carry, unroll=min(ATTN_UNROLL, nk))
    _, acc = carry
    o_ref[...] = (acc[:, :HEAD_V] / acc[:, HEAD_V:HEAD_V + 1]).astype(o_ref.dtype)


def attention(q, kvs):
    _, lq, _ = q.shape
    tq = min(lq, ATTN_TQ)
    segs, args, specs = [], [], []
    for k, v in kvs:
        lk = k.shape[1]
        tk = min(lk, ATTN_TK)
        segs.append((tk, lk // tk))
        args += [k, v]
        specs += [pl.BlockSpec((1, lk, MLA_QK_PAD), lambda h, i: (h, 0, 0)),
                  pl.BlockSpec((1, lk, MLA_V_PAD), lambda h, i: (h, 0, 0))]
    return pl.pallas_call(
        functools.partial(_attn_kernel, segs=tuple(segs)),
        out_shape=jax.ShapeDtypeStruct((lq, N_HEADS * HEAD_V), BF16),
        grid=(N_HEADS, lq // tq),
        in_specs=[pl.BlockSpec((1, tq, MLA_QK_PAD), lambda h, i: (h, i, 0))] + specs,
        out_specs=pl.BlockSpec((tq, HEAD_V), lambda h, i: (i, h)),
        compiler_params=_cparams(("parallel", "arbitrary")),
        name="attention",
    )(q, *args)


def _head_post(o, gate, gain, centre, silu_gate):
    sig = jax.nn.sigmoid(gate)
    act = gate * sig if silu_gate else sig
    outs = []
    for h in range(N_HEADS):
        sl = slice(h * HEAD_V, (h + 1) * HEAD_V)
        oh = o[:, sl]
        if centre:
            oh = oh - jnp.mean(oh, axis=-1, keepdims=True)
        oh = oh * lax.rsqrt(jnp.mean(oh * oh, axis=-1, keepdims=True) + EPS)
        outs.append((oh * gain[:, sl] * act[:, sl]).astype(BF16))
    return jnp.concatenate(outs, axis=1)


def _merge_kernel(x_ref, g0_ref, g1_ref, g2_ref, g3_ref, af_ref, ab_ref, b_ref, cf_ref, cb_ref, df_ref, db_ref,
                  za_ref, zc_ref, zd_ref, gain_ref, wb_ref, wo_ref, g_ref, o_ref):
    gain = gain_ref[...]
    branches = (
        _head_post(af_ref[...] + ab_ref[...], za_ref[...].astype(F32), gain[0:1], False, True),
        b_ref[...],
        _head_post(cf_ref[...] + cb_ref[...], zc_ref[...].astype(F32), gain[1:2], False, False),
        _head_post(df_ref[...] + db_ref[...], zd_ref[...].astype(F32), gain[2:3], True, True),
    )
    s = None
    for n, (br, gate_ref) in enumerate(zip(branches, (g0_ref, g1_ref, g2_ref, g3_ref))):
        proj = _dot(br, wb_ref[n])
        gate = jax.nn.sigmoid(gate_ref[...].astype(F32))
        s = gate * proj if s is None else s + gate * proj
    m = _dot(s.astype(BF16), wo_ref[...])
    o_ref[...] = x_ref[...] + g_ref[0:1] * m


def merge_branches(x, zmain, gdn, attn, mlstm, ret, gains, w_branch, w_out, gvec):
    m = x.shape[0]
    tm = min(m, 512)
    row = lambda w: pl.BlockSpec((tm, w), lambda i: (i, 0))
    zcol = lambda off: pl.BlockSpec((tm, BRANCH_W), lambda i: (i, off // BRANCH_W))
    return pl.pallas_call(
        _merge_kernel,
        out_shape=jax.ShapeDtypeStruct((m, D_MODEL), F32),
        grid=(m // tm,),
        in_specs=[row(D_MODEL)]
                 + [pl.BlockSpec((tm, D_MODEL), lambda i, n=n: (i, OFF_GATE // D_MODEL + n)) for n in range(N_BRANCH)]
                 + [row(BRANCH_W)] * 7
                 + [zcol(OFF_GDN_Z), zcol(OFF_MLSTM_O), zcol(OFF_RET_G),
                    pl.BlockSpec((8, BRANCH_W), lambda i: (0, 0)),
                    pl.BlockSpec((N_BRANCH, BRANCH_W, D_MODEL), lambda i: (0, 0, 0)),
                    pl.BlockSpec((D_MODEL, D_MODEL), lambda i: (0, 0)),
                    pl.BlockSpec((8, D_MODEL), lambda i: (0, 0))],
        out_specs=row(D_MODEL),
        compiler_params=pltpu.CompilerParams(dimension_semantics=("parallel",), vmem_limit_bytes=MERGE_VMEM_LIMIT),
        name="merge_branches",
    )(x, zmain, zmain, zmain, zmain, gdn[0], gdn[1], attn, mlstm[0], mlstm[1], ret[0], ret[1],
      zmain, zmain, zmain, gains, w_branch, w_out, gvec)


def _ffn_kernel(x_ref, pv_ref, wg_ref, wu_ref, wd_ref, o_ref, h_ref):
    f = pl.program_id(1)

    def partial_out(h):
        g = _dot(h, wg_ref[...])
        u = _dot(h, wu_ref[...])
        act = (g * jax.nn.sigmoid(g) * u).astype(BF16)
        return pv_ref[3:4] * _dot(act, wd_ref[...])

    @pl.when(f == 0)
    def _():
        pv = pv_ref[...]
        h = (_rms(x_ref[...]) * pv[0:1] * pv[1:2] + pv[2:3]).astype(BF16)
        h_ref[...] = h
        o_ref[...] = x_ref[...] + partial_out(h)

    @pl.when(f > 0)
    def _():
        o_ref[...] += partial_out(h_ref[...])


def dense_ffn(x, pv, w_gu, w_down):
    m = x.shape[0]
    tm = min(m, 1024)
    tf = 512
    nf = D_FF // tf
    return pl.pallas_call(
        _ffn_kernel,
        out_shape=jax.ShapeDtypeStruct((m, D_MODEL), F32),
        grid=(m // tm, nf),
        in_specs=[pl.BlockSpec((tm, D_MODEL), lambda i, f: (i, 0)),
                  pl.BlockSpec((8, D_MODEL), lambda i, f: (0, 0)),
                  pl.BlockSpec((D_MODEL, tf), lambda i, f: (0, f)),
                  pl.BlockSpec((D_MODEL, tf), lambda i, f: (0, f + nf)),
                  pl.BlockSpec((tf, D_MODEL), lambda i, f: (f, 0))],
        out_specs=pl.BlockSpec((tm, D_MODEL), lambda i, f: (i, 0)),
        scratch_shapes=[pltpu.VMEM((tm, D_MODEL), BF16)],
        compiler_params=_cparams(("parallel", "arbitrary")),
        name="dense_ffn",
    )(x, pv, w_gu, w_gu, w_down)


def _router_kernel(x_ref, pv_ref, w_ref, b_ref, h_ref, route_ref):
    pv = pv_ref[...]
    h = _rms(x_ref[...]) * pv[0:1] * pv[1:2] + pv[2:3]
    h_hi = h.astype(BF16)
    h_lo = (h - h_hi.astype(F32)).astype(BF16)
    w = w_ref[...]
    w_hi = w.astype(BF16)
    w_lo = (w - w_hi.astype(F32)).astype(BF16)
    logits = _dot(h_hi, w_hi) + _dot(h_hi, w_lo) + _dot(h_lo, w_hi) + b_ref[...]
    h_ref[...] = h
    lane = lax.broadcasted_iota(jnp.int32, logits.shape, 1).astype(F32)
    m1 = jnp.max(logits, axis=1, keepdims=True)
    e1 = jnp.min(jnp.where(logits == m1, lane, 128.0), axis=1, keepdims=True)
    rest = jnp.where(lane == e1, -jnp.inf, logits)
    m2 = jnp.max(rest, axis=1, keepdims=True)
    e2 = jnp.min(jnp.where(rest == m2, lane, 128.0), axis=1, keepdims=True)
    z2 = jnp.exp(m2 - m1)
    w1 = 1.0 / (1.0 + z2)
    route_ref[...] = jnp.where(lane == 0.0, w1, jnp.where(lane == 1.0, z2 * w1, jnp.where(
        lane == 2.0, e1, jnp.where(lane == 3.0, e2, 0.0))))


def moe_router(x, pv, w_router_pad, b_router_pad):
    m = x.shape[0]
    tm = min(m, 1024)
    return pl.pallas_call(
        _router_kernel,
        out_shape=(jax.ShapeDtypeStruct((m, D_MODEL), F32), jax.ShapeDtypeStruct((m, 128), F32)),
        grid=(m // tm,),
        in_specs=[pl.BlockSpec((tm, D_MODEL), lambda i: (i, 0)),
                  pl.BlockSpec((8, D_MODEL), lambda i: (0, 0)),
                  pl.BlockSpec((D_MODEL, 128), lambda i: (0, 0)),
                  pl.BlockSpec((1, 128), lambda i: (0, 0))],
        out_specs=(pl.BlockSpec((tm, D_MODEL), lambda i: (i, 0)), pl.BlockSpec((tm, 128), lambda i: (i, 0))),
        compiler_params=_cparams(("parallel",)),
        name="moe_router",
    )(x, pv, w_router_pad, b_router_pad)


SC_CORES, SC_SUBCORES = 2, 16
SC_ROWS = 64


def sc_gather(table, idx):
    b, d = idx.shape[0], table.shape[1]
    nw = SC_CORES * SC_SUBCORES
    assert b % (nw * SC_ROWS) == 0
    per_w = b // nw
    mesh = plsc.VectorSubcoreMesh(core_axis_name="c", subcore_axis_name="s")

    @functools.partial(
        pl.kernel, mesh=mesh, out_type=jax.ShapeDtypeStruct((b, d), table.dtype),
        scratch_types=[pltpu.VMEM((SC_ROWS,), jnp.int32), pltpu.VMEM((SC_ROWS, d), table.dtype),
                       pltpu.SemaphoreType.DMA])
    def gather_rows(table_hbm, idx_hbm, out_hbm, idx_v, rows_v, sem):
        wid = lax.axis_index("s") * SC_CORES + lax.axis_index("c")

        def body(i, carry):
            base = wid * per_w + i * SC_ROWS
            pltpu.sync_copy(idx_hbm.at[pl.ds(base, SC_ROWS)], idx_v)
            pltpu.async_copy(table_hbm.at[idx_v], rows_v, sem).wait()
            pltpu.sync_copy(rows_v, out_hbm.at[pl.ds(base, SC_ROWS)])
            return carry

        lax.fori_loop(0, per_w // SC_ROWS, body, 0)

    return gather_rows(table, idx)


def _expert_kernel(be_ref, nb_ref, x_ref, wg_ref, wu_ref, wd_ref, o_ref, x16_ref):
    b = pl.program_id(0)
    f = pl.program_id(1)

    live = b < nb_ref[0]

    def partial_out(x):
        g = _dot(x, wg_ref[0].astype(BF16))
        u = _dot(x, wu_ref[0].astype(BF16))
        act = (g * jax.nn.sigmoid(g) * u).astype(BF16)
        return _dot(act, wd_ref[0].astype(BF16))

    @pl.when(live & (f == 0))
    def _():
        x = x_ref[...].astype(BF16)
        x16_ref[...] = x
        o_ref[...] = partial_out(x)

    @pl.when(live & (f > 0))
    def _():
        o_ref[...] += partial_out(x16_ref[...])

    @pl.when(jnp.logical_not(live) & (f == 0))
    def _():
        o_ref[...] = jnp.zeros_like(o_ref)


def expert_ffn(block_e, n_used, xb, w_gu, w_down):
    cap = xb.shape[0]
    nb = cap // MOE_ROWS
    tf = 512
    nf = D_FF // tf

    def live(b, nbr):
        return jnp.minimum(b, nbr[0] - 1)

    def fsel(b, f, nbr):
        return jnp.where(b < nbr[0], f, nf - 1)

    grid_spec = pltpu.PrefetchScalarGridSpec(
        num_scalar_prefetch=2,
        grid=(nb, nf),
        in_specs=[pl.BlockSpec((MOE_ROWS, D_MODEL), lambda b, f, be, nbr: (live(b, nbr), 0)),
                  pl.BlockSpec((1, D_MODEL, tf), lambda b, f, be, nbr: (be[live(b, nbr)], 0, fsel(b, f, nbr))),
                  pl.BlockSpec((1, D_MODEL, tf), lambda b, f, be, nbr: (be[live(b, nbr)], 0, fsel(b, f, nbr) + nf)),
                  pl.BlockSpec((1, tf, D_MODEL), lambda b, f, be, nbr: (be[live(b, nbr)], fsel(b, f, nbr), 0))],
        out_specs=pl.BlockSpec((MOE_ROWS, D_MODEL), lambda b, f, be, nbr: (b, 0)),
        scratch_shapes=[pltpu.VMEM((MOE_ROWS, D_MODEL), BF16)],
    )
    return pl.pallas_call(
        _expert_kernel,
        out_shape=jax.ShapeDtypeStruct((cap, D_MODEL), F32),
        grid_spec=grid_spec,
        compiler_params=_cparams(("arbitrary", "arbitrary")),
        name="expert_ffn",
    )(block_e, n_used, xb, w_gu, w_gu, w_down)


def _final_kernel(x_ref, y0_ref, y1_ref, w_ref, pv_ref, o_ref):
    w = w_ref[...]
    f = w[:, 0:1] * y0_ref[...] + w[:, 1:2] * y1_ref[...]
    x = x_ref[...] + pv_ref[0:1] * f
    o_ref[...] = _rms(x) * pv_ref[1:2]


def final_combine(x, y2, w, pv):
    m = x.shape[0]
    tm = min(m, 1024)
    row = lambda wd: pl.BlockSpec((tm, wd), lambda i: (i, 0))
    return pl.pallas_call(
        _final_kernel,
        out_shape=jax.ShapeDtypeStruct((m, D_MODEL), F32),
        grid=(m // tm,),
        in_specs=[row(D_MODEL), row(D_MODEL), pl.BlockSpec((tm, D_MODEL), lambda i: (i + m // tm, 0)), row(128),
                  pl.BlockSpec((8, D_MODEL), lambda i: (0, 0))],
        out_specs=row(D_MODEL),
        compiler_params=_cparams(("parallel",)),
        name="final_combine",
    )(x, y2, y2, w, pv)


CH = 128
NEG = -1e30


def _masks(reverse):
    r = lax.broadcasted_iota(jnp.int32, (CH, CH), 0)
    c = lax.broadcasted_iota(jnp.int32, (CH, CH), 1)
    return (r <= c, r < c) if reverse else (r >= c, r > c)


def _cumsum_time(incl, x):
    m = jnp.where(incl, 1.0, 0.0).astype(BF16)
    hi = x.astype(BF16)
    lo = (x - hi.astype(F32)).astype(BF16)
    return _dot(m, hi) + _dot(m, lo)


def _cummax_time(a, reverse):
    row = lax.broadcasted_iota(jnp.int32, a.shape, 0)
    k = 1
    while k < CH:
        if reverse:
            a = jnp.maximum(a, jnp.where(row < CH - k, pltpu.roll(a, CH - k, 0), NEG))
        else:
            a = jnp.maximum(a, jnp.where(row >= k, pltpu.roll(a, k, 0), NEG))
        k *= 2
    return a


SCAN_SUB = 4


def _scan_grid(l):
    rows = min(l, SCAN_SUB * CH)
    return l // rows, rows


def _sweep(d, nsub):
    return list(range(nsub)) if d == 0 else list(range(nsub - 1, -1, -1))


def _dir_specs(nsteps, rows, width, col_block):
    return [pl.BlockSpec((rows, width), lambda n: (n, col_block)),
            pl.BlockSpec((rows, width), lambda n: (nsteps - 1 - n, col_block))]


def _whole(a):
    return pl.BlockSpec(a.shape, lambda n: tuple(0 for _ in a.shape))


def _ret_consts():
    log_gamma = np.log(1.0 - 2.0 ** (-5.0 - np.arange(N_HEADS, dtype=np.float64)))
    pos = np.arange(CH, dtype=np.float64)
    diff = pos[:, None] - pos[None, :]
    dec_f = np.where(diff >= 0, np.exp(log_gamma[:, None, None] * diff), 0.0)
    dec = np.stack([dec_f, np.transpose(dec_f, (0, 2, 1))])
    qs_f = np.exp(log_gamma[None, :] * (pos[:, None] + 1.0))
    ks_f = np.exp(log_gamma[None, :] * (CH - 1.0 - pos[:, None]))
    qs_b = np.exp(log_gamma[None, :] * (CH - pos[:, None]))
    ks_b = np.exp(log_gamma[None, :] * pos[:, None])
    rep = lambda a: np.repeat(a, RET_DK, axis=1)
    qs = np.stack([rep(qs_f), rep(qs_b)])
    ks = np.stack([rep(ks_f), rep(ks_b)])
    chunk_decay = [float(np.exp(lg * CH)) for lg in log_gamma]
    return (jnp.asarray(dec, F32), jnp.asarray(qs, F32), jnp.asarray(ks, F32)), chunk_decay


def _ret_kernel(*refs, rotary, chunk_decay):
    if rotary:
        (qf, qb, kf, kb, vf, vb, cosf, cosb, sinf, sinb, dec_ref, qs_ref, ks_ref, s0_ref,
         of_ref, ob_ref, s_ref) = refs
        tabs = ((cosf, sinf), (cosb, sinb))
    else:
        qf, qb, kf, kb, vf, vb, dec_ref, qs_ref, ks_ref, s0_ref, of_ref, ob_ref, s_ref = refs
        tabs = (None, None)

    @pl.when(pl.program_id(0) == 0)
    def _():
        s_ref[...] = s0_ref[...]

    lane = lax.broadcasted_iota(jnp.int32, (CH, N_HEADS * RET_DK), 1)
    first_half = (lane & (RET_DK - 1)) < RET_DK // 2

    nsub = qf.shape[0] // CH
    chains = {}
    for d, (q_ref, k_ref, v_ref, o_ref) in enumerate(((qf, kf, vf, of_ref), (qb, kb, vb, ob_ref))):
        for j in range(nsub):
            rows = slice(j * CH, (j + 1) * CH)
            q = q_ref[rows, :].astype(F32)
            k = k_ref[rows, :].astype(F32) * RET_DK ** -0.5
            if rotary:
                cos, sin = tabs[d][0][rows, :], tabs[d][1][rows, :]

                def rot(x, cos=cos, sin=sin):
                    swapped = jnp.where(first_half, pltpu.roll(x, N_HEADS * RET_DK - RET_DK // 2, 1),
                                        pltpu.roll(x, RET_DK // 2, 1))
                    return x * cos + swapped * sin

                q, k = rot(q), rot(k)
            qb16, kb16 = q.astype(BF16), k.astype(BF16)
            q_in = (q * qs_ref[d]).astype(BF16)
            k_out = k * ks_ref[d]
            k_t = [k_out[:, :128].T, k_out[:, 128:].T]
            for h in range(N_HEADS):
                sl = slice(h * RET_DK, (h + 1) * RET_DK)
                chains[d, j, h] = dict(
                    o_ref=o_ref, rows=rows, vh=v_ref[rows, h * HEAD_V:(h + 1) * HEAD_V], q=qb16[:, sl], k=kb16[:, sl],
                    q_in=q_in[:, sl], k_th=k_t[h // 2][(h % 2) * RET_DK:(h % 2 + 1) * RET_DK, :].astype(BF16))

    for (d, j, h), c in chains.items():
        c["p"] = (_dot_nt(c["q"], c["k"]) * dec_ref[d, h]).astype(BF16)
    for c in chains.values():
        c["intra"] = _dot(c["p"], c["vh"])
        c["update"] = _dot(c["k_th"], c["vh"])
    states = {(d, h): s_ref[d, h] for d in range(2) for h in range(N_HEADS)}
    for i in range(nsub):
        for (d, h), s in list(states.items()):
            c = chains[d, _sweep(d, nsub)[i], h]
            c["o_ref"][c["rows"], h * HEAD_V:(h + 1) * HEAD_V] = c["intra"] + _dot(c["q_in"], s.astype(BF16))
            states[d, h] = chunk_decay[h] * s + c["update"]
    for (d, h), s in states.items():
        s_ref[d, h] = s


def retention_scan(z, s0, tables):
    l = z.shape[0]
    nsteps, rows = _scan_grid(l)
    consts, chunk_decay = _ret_consts()
    rotary = tables is not None
    w = N_HEADS * RET_DK
    specs = (_dir_specs(nsteps, rows, w, OFF_RET_Q // w) + _dir_specs(nsteps, rows, w, OFF_RET_K // w)
             + _dir_specs(nsteps, rows, BRANCH_W, OFF_RET_V // BRANCH_W))
    args = [z] * 6
    if rotary:
        specs += _dir_specs(nsteps, rows, w, 0) + _dir_specs(nsteps, rows, w, 0)
        args += [tables[0], tables[0], tables[1], tables[1]]
    specs += [_whole(c) for c in consts] + [_whole(s0)]
    args += list(consts) + [s0]
    return pl.pallas_call(
        functools.partial(_ret_kernel, rotary=rotary, chunk_decay=chunk_decay),
        out_shape=(jax.ShapeDtypeStruct((l, BRANCH_W), F32), jax.ShapeDtypeStruct((l, BRANCH_W), F32),
                   jax.ShapeDtypeStruct(s0.shape, F32)),
        grid=(nsteps,),
        in_specs=specs,
        out_specs=tuple(_dir_specs(nsteps, rows, BRANCH_W, 0)) + (_whole(s0),),
        compiler_params=_cparams(("arbitrary",)),
        name="retention_scan",
    )(*args)


def _mlstm_kernel(qf, qb, kf, kb, vf, vb, smf, smb, bias_ref, c0_ref, m0_ref, of_ref, ob_ref, c_ref, m_ref):
    @pl.when(pl.program_id(0) == 0)
    def _():
        c_ref[...] = c0_ref[...]
        m_ref[...] = m0_ref[...]

    lane = lax.broadcasted_iota(jnp.int32, (CH, 128), 1)
    is_forget = (lane >= F_LANE) & (lane < F_LANE + 2 * N_HEADS)
    ones_col = jnp.where(lane == 0, 1.0, 0.0).astype(BF16)

    nsub = qf.shape[0] // CH
    chains = {}
    for d, (q_ref, k_ref, v_ref, sm_ref, o_ref) in enumerate(((qf, kf, vf, smf, of_ref), (qb, kb, vb, smb, ob_ref))):
        reverse = d == 1
        incl, _ = _masks(reverse)
        last = 0 if reverse else CH - 1
        for j in range(nsub):
            rows = slice(j * CH, (j + 1) * CH)
            pre = sm_ref[rows, :] + bias_ref[0:1]
            x = jnp.where(is_forget, jax.nn.log_sigmoid(pre), pre)
            b = _cumsum_time(incl, x)
            a = pltpu.roll(x, F_LANE - I_LANE, 1) - b
            cm = _cummax_time(a, reverse)
            a_t = a.T
            q = (q_ref[rows, :].astype(F32) * MLSTM_DK ** -0.5).astype(BF16)
            kf32 = k_ref[rows, :].astype(F32)
            k_t = [kf32[:, :128].T, kf32[:, 128:].T]
            for h in range(N_HEADS):
                gf = F_LANE + d * N_HEADS + h
                sl = slice(h * MLSTM_DK, (h + 1) * MLSTM_DK)
                cm_col, a_row = cm[:, gf:gf + 1], a_t[gf:gf + 1, :]
                chains[d, j, h] = dict(
                    o_ref=o_ref, rows=rows, q=q[:, sl], k=k_ref[rows, sl], a_row=a_row, cm_col=cm_col,
                    b_col=b[:, gf:gf + 1], b_last=b[last:last + 1, gf:gf + 1], cm_last=cm[last:last + 1, gf:gf + 1],
                    w_intra=jnp.where(incl, jnp.exp(jnp.minimum(a_row - cm_col, 0.0)), 0.0),
                    k_th=k_t[h // 2][(h % 2) * MLSTM_DK:(h % 2 + 1) * MLSTM_DK, :],
                    v_aug=jnp.concatenate([v_ref[rows, h * HEAD_V:(h + 1) * HEAD_V], ones_col], axis=1))

    for c in chains.values():
        c["p"] = (c["w_intra"] * _dot_nt(c["q"], c["k"])).astype(BF16)
    for c in chains.values():
        c["pv"] = _dot(c["p"], c["v_aug"])
    c_augs = {(d, h): c_ref[d, h] for d in range(2) for h in range(N_HEADS)}
    m_ss = {(d, h): m_ref[d, h][0:1, 0:1] for d in range(2) for h in range(N_HEADS)}
    for i in range(nsub):
        cur = {(d, h): chains[d, _sweep(d, nsub)[i], h] for (d, h) in c_augs}
        qcs = {key: _dot(c["q"], c_augs[key].astype(BF16)) for key, c in cur.items()}
        tops = {key: jnp.maximum(m_ss[key], c["cm_last"]) for key, c in cur.items()}
        updates = {key: _dot((c["k_th"] * jnp.exp(c["a_row"] - tops[key])).astype(BF16), c["v_aug"])
                   for key, c in cur.items()}
        for (d, h), c in cur.items():
            m_s = m_ss[d, h]
            mx = jnp.maximum(m_s, c["cm_col"])
            tot = jnp.exp(m_s - mx) * qcs[d, h] + jnp.exp(c["cm_col"] - mx) * c["pv"]
            den = jnp.maximum(jnp.abs(tot[:, HEAD_V:HEAD_V + 1]), jnp.exp(-(c["b_col"] + mx)))
            c["o_ref"][c["rows"], h * HEAD_V:(h + 1) * HEAD_V] = tot[:, :HEAD_V] / den
        for key, c in cur.items():
            c_augs[key] = jnp.exp(m_ss[key] - tops[key]) * c_augs[key] + updates[key]
            m_ss[key] = c["b_last"] + tops[key]
    for (d, h) in c_augs:
        c_ref[d, h] = c_augs[d, h]
        m_ref[d, h] = jnp.broadcast_to(m_ss[d, h], (8, 128))


def mlstm_scan(z, small, bias, c0, m0):
    l = z.shape[0]
    nsteps, rows = _scan_grid(l)
    w = N_HEADS * MLSTM_DK
    specs = (_dir_specs(nsteps, rows, w, OFF_MLSTM_Q // w) + _dir_specs(nsteps, rows, w, OFF_MLSTM_K // w)
             + _dir_specs(nsteps, rows, BRANCH_W, OFF_MLSTM_V // BRANCH_W) + _dir_specs(nsteps, rows, 128, 0))
    specs += [_whole(bias), _whole(c0), _whole(m0)]
    return pl.pallas_call(
        _mlstm_kernel,
        out_shape=(jax.ShapeDtypeStruct((l, BRANCH_W), F32), jax.ShapeDtypeStruct((l, BRANCH_W), F32),
                   jax.ShapeDtypeStruct(c0.shape, F32), jax.ShapeDtypeStruct(m0.shape, F32)),
        grid=(nsteps,),
        in_specs=specs,
        out_specs=tuple(_dir_specs(nsteps, rows, BRANCH_W, 0)) + (_whole(c0), _whole(m0)),
        compiler_params=_cparams(("arbitrary",)),
        name="mlstm_scan",
    )(z, z, z, z, z, z, small, small, bias, c0, m0)


QKV_W = 3 * N_HEADS * GDN_DK
HALO = 8


def _gdn_prep_kernel(x_ref, prev_ref, next_ref, w_ref, q_ref, k_ref, v_ref):
    i = pl.program_id(0)
    tm = x_ref.shape[0]
    x = x_ref[...].astype(F32)
    prev = jnp.where(i > 0, prev_ref[...].astype(F32), 0.0)
    nxt = jnp.where(i < pl.num_programs(0) - 1, next_ref[...].astype(F32), 0.0)
    xe = jnp.concatenate([prev, x, nxt], axis=0)
    w = w_ref[...]
    y = None
    for tap in range(CONV_K):
        off = HALO + tap - CONV_K // 2
        term = w[tap:tap + 1] * xe[off:off + tm]
        y = term if y is None else y + term
    y = y * jax.nn.sigmoid(y)
    hw = N_HEADS * GDN_DK
    for h in range(N_HEADS):
        sl = slice(h * GDN_DK, (h + 1) * GDN_DK)
        qh = y[:, sl]
        kh = y[:, hw + h * GDN_DK:hw + (h + 1) * GDN_DK]
        q_ref[:, sl] = (qh * lax.rsqrt(jnp.sum(qh * qh, axis=-1, keepdims=True) + EPS) * GDN_DK ** -0.5).astype(BF16)
        k_ref[:, sl] = (kh * lax.rsqrt(jnp.sum(kh * kh, axis=-1, keepdims=True) + EPS)).astype(BF16)
    v_ref[...] = y[:, 2 * hw:].astype(BF16)


def gdn_prep(z, conv_w):
    l = z.shape[0]
    tm = min(l, 256)
    nb = l // tm
    r8 = tm // HALO
    cb = OFF_GDN_QKV // QKV_W
    w8 = jnp.concatenate([conv_w.astype(F32), jnp.zeros((8 - CONV_K, QKV_W), F32)], axis=0)
    return pl.pallas_call(
        _gdn_prep_kernel,
        out_shape=tuple(jax.ShapeDtypeStruct((l, BRANCH_W), BF16) for _ in range(3)),
        grid=(nb,),
        in_specs=[pl.BlockSpec((tm, QKV_W), lambda i: (i, cb)),
                  pl.BlockSpec((HALO, QKV_W), lambda i: (jnp.maximum(i * r8 - 1, 0), cb)),
                  pl.BlockSpec((HALO, QKV_W), lambda i: (jnp.minimum((i + 1) * r8, nb * r8 - 1), cb)),
                  pl.BlockSpec((8, QKV_W), lambda i: (0, 0))],
        out_specs=tuple(pl.BlockSpec((tm, BRANCH_W), lambda i: (i, 0)) for _ in range(3)),
        compiler_params=_cparams(("parallel",)),
        name="gdn_prep",
    )(z, z, z, w8)


N_LEVELS = 7


def _gdn_kernel(qf, qb, kf, kb, vf, vb, smf, smb, par_ref, s0_ref, of_ref, ob_ref, s_ref):
    @pl.when(pl.program_id(0) == 0)
    def _():
        s_ref[...] = s0_ref[...]

    lane = lax.broadcasted_iota(jnp.int32, (CH, 128), 1)
    is_decay = lane < B_LANE
    ri = lax.broadcasted_iota(jnp.int32, (CH, CH), 0)
    ci = lax.broadcasted_iota(jnp.int32, (CH, CH), 1)
    eye = jnp.where(ri == ci, 1.0, 0.0)
    pair_masks = [((ri >> (l + 1)) == (ci >> (l + 1))) & ((ri >> l) != (ci >> l)) for l in range(N_LEVELS)]

    nsub = qf.shape[0] // CH
    chains = {}
    for d, (q_ref, k_ref, v_ref, sm_ref, o_ref) in enumerate(((qf, kf, vf, smf, of_ref), (qb, kb, vb, smb, ob_ref))):
        reverse = d == 1
        incl, strict = _masks(reverse)
        last = 0 if reverse else CH - 1
        for j in range(nsub):
            rows = slice(j * CH, (j + 1) * CH)
            sm = sm_ref[rows, :]
            log_a = -jnp.exp(par_ref[1:2]) * jax.nn.softplus(sm + par_ref[0:1])
            x = jnp.where(is_decay, log_a, jax.nn.sigmoid(sm))
            g = _cumsum_time(incl, x)
            g_t = g.T
            for h in range(N_HEADS):
                ga, gb = A_LANE + d * N_HEADS + h, B_LANE + d * N_HEADS + h
                sl = slice(h * GDN_DK, (h + 1) * GDN_DK)
                g_col, g_row = g[:, ga:ga + 1], g_t[ga:ga + 1, :]
                g_last = g[last:last + 1, ga:ga + 1]
                beta = x[:, gb:gb + 1]
                decay = jnp.where(incl, jnp.exp(jnp.minimum(g_col - g_row, 0.0)), 0.0)
                kh = k_ref[rows, sl]
                kf32 = kh.astype(F32)
                kbeta = kf32 * beta
                a = jnp.where(strict, _dot_nt(kbeta.astype(BF16), kh) * decay, 0.0)
                e_g = jnp.exp(g_col)
                rhs = jnp.concatenate([v_ref[rows, sl].astype(F32) * beta, kbeta * e_g], axis=1)
                chains[d, j, h] = dict(
                    rows=rows, sl=sl, o_ref=o_ref, a=a, rhs=rhs, s_decay=jnp.exp(g_last),
                    qk=(_dot_nt(q_ref[rows, sl], kh) * decay).astype(BF16),
                    q_in=(q_ref[rows, sl].astype(F32) * e_g).astype(BF16),
                    k_out_t=(kf32.T * jnp.exp(g_last - g_row)).astype(BF16))

    cl = list(chains.values())
    xs = [eye - jnp.where(pair_masks[0], c["a"], 0.0) for c in cl]
    for pm in pair_masks[1:]:
        ys = [_dot(jnp.where(pm, c["a"], 0.0).astype(BF16), x.astype(BF16)) for c, x in zip(cl, xs)]
        xs = [x - _dot(x.astype(BF16), y.astype(BF16)) for x, y in zip(xs, ys)]
    for c, x in zip(cl, xs):
        c["sol"] = c["rhs"] + _dot((x - eye).astype(BF16), c["rhs"].astype(BF16))
    states = {(d, h): s_ref[d, h] for d in range(2) for h in range(N_HEADS)}
    for i in range(nsub):
        cur = {(d, h): chains[d, _sweep(d, nsub)[i], h] for (d, h) in states}
        s16s = {key: s.astype(BF16) for key, s in states.items()}
        v_news = {key: (c["sol"][:, :HEAD_V] - _dot(c["sol"][:, HEAD_V:].astype(BF16), s16s[key])).astype(BF16)
                  for key, c in cur.items()}
        updates = {key: _dot(c["k_out_t"], v_news[key]) for key, c in cur.items()}
        for key, c in cur.items():
            c["o_ref"][c["rows"], c["sl"]] = _dot(c["q_in"], s16s[key]) + _dot(c["qk"], v_news[key])
        for key, c in cur.items():
            states[key] = states[key] * c["s_decay"] + updates[key]
    for (d, h), s in states.items():
        s_ref[d, h] = s


def gdn_scan(qn, kn, vn, small, par, s0):
    l = qn.shape[0]
    nsteps, rows = _scan_grid(l)
    specs = _dir_specs(nsteps, rows, BRANCH_W, 0) * 3 + _dir_specs(nsteps, rows, 128, 0)
    specs += [_whole(par), _whole(s0)]
    return pl.pallas_call(
        _gdn_kernel,
        out_shape=(jax.ShapeDtypeStruct((l, BRANCH_W), F32), jax.ShapeDtypeStruct((l, BRANCH_W), F32),
                   jax.ShapeDtypeStruct(s0.shape, F32)),
        grid=(nsteps,),
        in_specs=specs,
        out_specs=tuple(_dir_specs(nsteps, rows, BRANCH_W, 0)) + (_whole(s0),),
        compiler_params=_cparams(("arbitrary",)),
        name="gdn_scan",
    )(qn, qn, kn, kn, vn, vn, small, small, par, s0)


def _lane_rows(entries):
    r = jnp.zeros((8, 128), F32)
    for row, lane, vals in entries:
        r = r.at[row, lane:lane + vals.shape[0]].set(vals.astype(F32))
    return r


def gdn_branch(zc, sc, zl, sl, conv_w, a_log, dt_bias):
    par = _lane_rows([(0, A_LANE, dt_bias.reshape(-1)), (1, A_LANE, a_log.reshape(-1))])
    s0 = jnp.zeros((2, N_HEADS, GDN_DK, HEAD_V), F32)
    ofc, obc, s1 = gdn_scan(*gdn_prep(zc, conv_w), sc, par, s0)
    ofl, obl, _ = gdn_scan(*gdn_prep(zl, conv_w), sl, par, s1)
    return (ofc, obc), (ofl, obl)


def mlstm_branch(zc, sc, zl, sl, gate_b):
    bias = _lane_rows([(0, I_LANE, gate_b[0].reshape(-1)), (0, F_LANE, gate_b[1].reshape(-1))])
    c0 = jnp.zeros((2, N_HEADS, MLSTM_DK, 2 * HEAD_V), F32)
    m0 = jnp.zeros((2, N_HEADS, 8, 128), F32)
    ofc, obc, c1, m1 = mlstm_scan(zc, sc, bias, c0, m0)
    ofl, obl, _, _ = mlstm_scan(zl, sl, bias, c1, m1)
    return (ofc, obc), (ofl, obl)


def retention_branch(zc, zl, tables):
    s0 = jnp.zeros((2, N_HEADS, RET_DK, HEAD_V), F32)
    ofc, obc, s1 = retention_scan(zc, s0, None)
    ofl, obl, _ = retention_scan(zl, s1, tables)
    return (ofc, obc), (ofl, obl)


def _ret_rope_tables(n_lat):
    inv = np.float32(ROPE_BASE) ** (-np.arange(0, RET_DK, 2, dtype=np.float32) / np.float32(RET_DK))
    ang = np.arange(n_lat, dtype=np.float32)[:, None] * inv[None, :]
    cos = np.concatenate([np.cos(ang), np.cos(ang)], axis=-1)
    sin = np.concatenate([-np.sin(ang), np.sin(ang)], axis=-1)
    return jnp.asarray(np.tile(cos, (1, N_HEADS)), F32), jnp.asarray(np.tile(sin, (1, N_HEADS)), F32)


def _rope_tables(n_lat):
    q = MLA_ROPE // 4
    inv = np.float32(ROPE_BASE) ** (-np.arange(0, 2 * q, 2, dtype=np.float32) / np.float32(2 * q))
    t = np.arange(n_lat)
    row = (t // GRID_W).astype(np.float32)[:, None] * inv[None, :]
    col = (t % GRID_W).astype(np.float32)[:, None] * inv[None, :]
    cos = np.concatenate([np.cos(row), np.cos(row), np.cos(col), np.cos(col)], axis=-1)
    sin = np.concatenate([-np.sin(row), np.sin(row), -np.sin(col), np.sin(col)], axis=-1)
    return jnp.asarray(cos, F32), jnp.asarray(sin, F32)


def _take_cols(w, perm):
    runs, start = [], 0
    for i in range(1, len(perm) + 1):
        if i == len(perm) or perm[i] != perm[i - 1] + 1:
            runs.append((int(perm[start]), int(perm[i - 1]) + 1))
            start = i
    return jnp.concatenate([w[:, a:b].astype(BF16) for a, b in runs], axis=1)


def _pad_rows(v, n=8):
    rows = [jnp.reshape(r, (1, -1)).astype(F32) for r in v]
    d = rows[0].shape[1]
    return jnp.concatenate(rows + [jnp.zeros((n - len(rows), d), F32)], axis=0)


def _mla_weights(w_uq, w_ukv):
    wq = w_uq.reshape(MLA_Q_RANK, N_HEADS, MLA_QK)
    rope = wq[:, :, MLA_NOPE:]
    swapped = rope[:, :, _rope_swap(np.arange(MLA_ROPE))]
    wq_ext = jnp.concatenate([wq[:, :, :MLA_NOPE], rope, swapped], axis=-1).reshape(MLA_Q_RANK, N_HEADS * 256)
    return wq_ext.astype(BF16), w_ukv.astype(BF16)


def _moe(x_l, pv2, g2, w_router, b_router, w_gu, w_down, final_g):
    n = x_l.shape[0]
    w_pad = jnp.concatenate([w_router, jnp.zeros((D_MODEL, 128 - N_EXPERTS), F32)], axis=1)
    b_pad = jnp.concatenate([b_router, jnp.full((128 - N_EXPERTS,), -1e30, F32)]).reshape(1, 128)
    h2, route = moe_router(x_l, pv2, w_pad, b_pad)
    n_assign = n * TOP_K
    experts = jnp.arange(N_EXPERTS)[None, :]
    e0, e1 = route[:, TOP_K].astype(jnp.int32), route[:, TOP_K + 1].astype(jnp.int32)
    hot0, hot1 = (e0[:, None] == experts).astype(jnp.int32), (e1[:, None] == experts).astype(jnp.int32)
    c0, c1 = jnp.cumsum(hot0, axis=0), jnp.cumsum(hot1, axis=0)
    before1 = c1 - hot1
    counts = c0[-1] + c1[-1]
    rank0 = jnp.sum(hot0 * (c0 - hot0 + before1), axis=1)
    rank1 = jnp.sum(hot1 * (c0 + before1), axis=1)
    padded = (counts + MOE_ROWS - 1) // MOE_ROWS * MOE_ROWS
    pad_end = jnp.cumsum(padded)
    pad_start = pad_end - padded
    dest = jnp.concatenate([pad_start[e0] + rank0, pad_start[e1] + rank1])
    nb = n_assign // MOE_ROWS + N_EXPERTS
    cap = nb * MOE_ROWS
    block_start = jnp.arange(nb, dtype=pad_end.dtype) * MOE_ROWS
    block_e = jnp.minimum(jnp.sum(block_start[:, None] >= pad_end[None, :], axis=1), N_EXPERTS - 1).astype(jnp.int32)
    n_used = (pad_end[-1] // MOE_ROWS).astype(jnp.int32).reshape(1)
    slot_tok = (jnp.arange(cap, dtype=jnp.int32) % n).at[dest].set(jnp.arange(n_assign, dtype=jnp.int32) % n)
    yb = expert_ffn(block_e, n_used, sc_gather(h2, slot_tok), w_gu, w_down)
    y2 = sc_gather(yb, dest.astype(jnp.int32))
    return final_combine(x_l, y2, route, _pad_rows([g2, final_g]))


def kernel(x, c, ctx, c_ctx, w_mod, b_mod, norm1_g, norm2_g, w_in, gdn_conv_w, gdn_a_log, gdn_dt_bias, gdn_norm_g, mla_q_norm_g, mla_kv_norm_g, mla_w_uq, mla_w_ukv, mlstm_gate_b, mlstm_norm_g, ret_norm_g, w_branch, w_out, ffn_w_in, ffn_w_down, moe_w_router, moe_b_router, moe_w_in, moe_w_down, final_norm_g):
    n_lat = x.shape[1]
    n_ctx = ctx.shape[1]
    x_l, x_c = x[0], ctx[0]
    cond = _pad_rows([c_ctx, c[0]])
    cos_l, sin_l = _rope_tables(n_lat)
    cos_c, sin_c = jnp.ones((n_ctx, MLA_ROPE), F32), jnp.zeros((n_ctx, MLA_ROPE), F32)
    ret_tables = _ret_rope_tables(n_lat)
    out = None
    for li in range(DEPTH):
        last = li == DEPTH - 1
        mod = modulation_vectors(cond, w_mod[li], b_mod[li])
        csh1, csc1, cg1, csh2, csc2, cg2 = jnp.split(mod[0], 6)
        sh1, sc1, g1, sh2, sc2, g2 = jnp.split(mod[1], 6)
        w_main = _take_cols(w_in[li], _MAIN_PERM)
        w_small = _take_cols(w_in[li], _SMALL_PERM)
        w_small = jnp.concatenate([w_small, jnp.zeros((D_MODEL, N_SMALL - w_small.shape[1]), BF16)], axis=1)
        pv_l = _pad_rows([norm1_g[li], 1 + sc1, sh1])
        pv_c = _pad_rows([norm1_g[li], 1 + csc1, csh1])
        zl, sl = norm_proj(x_l, pv_l, w_main, w_small, 1664)
        zc, sc = norm_proj(x_c, pv_c, w_main, w_small, 1664)

        a_c, a_l = gdn_branch(zc, sc, zl, sl, gdn_conv_w[li], gdn_a_log[li], gdn_dt_bias[li])
        c_c, c_l = mlstm_branch(zc, sc, zl, sl, mlstm_gate_b[li])
        d_c, d_l = retention_branch(zc, zl, ret_tables)
        gains = _pad_rows([gdn_norm_g[li], mlstm_norm_g[li], ret_norm_g[li]])

        wq_ext, wkv = _mla_weights(mla_w_uq[li], mla_w_ukv[li])
        gq, gkv = mla_q_norm_g[li].reshape(1, -1), mla_kv_norm_g[li].reshape(1, -1)
        ql, kl, vl = mla_project(zl, cos_l, sin_l, gq, gkv, wq_ext, wkv)
        qc, kc, vc = mla_project(zc, cos_c, sin_c, gq, gkv, wq_ext, wkv)
        b_l = attention(ql, [(kl, vl), (kc, vc)])

        wb = w_branch[li].astype(BF16)
        wo = w_out[li].astype(BF16)
        x_l = merge_branches(x_l, zl, a_l, b_l, c_l, d_l, gains, wb, wo, _pad_rows([g1]))
        if not last:
            b_c = attention(qc, [(kc, vc)])
            x_c = merge_branches(x_c, zc, a_c, b_c, c_c, d_c, gains, wb, wo, _pad_rows([cg1]))

        if li % 2 == 0:
            w_gu = ffn_w_in[li // 2].astype(BF16)
            w_dn = ffn_w_down[li // 2].astype(BF16)
            assert not last
            x_l = dense_ffn(x_l, _pad_rows([norm2_g[li], 1 + sc2, sh2, g2]), w_gu, w_dn)
            x_c = dense_ffn(x_c, _pad_rows([norm2_g[li], 1 + csc2, csh2, cg2]), w_gu, w_dn)
        else:
            assert last
            out = _moe(x_l, _pad_rows([norm2_g[li], 1 + sc2, sh2]), g2, moe_w_router[li // 2],
                       moe_b_router[li // 2], moe_w_in[li // 2], moe_w_down[li // 2], final_norm_g)
    return out[None]
```

```python
import functools
import math

import numpy as np
import jax
import jax.numpy as jnp
from jax import lax
from jax.experimental import pallas as pl
from jax.experimental.pallas import tpu as pltpu
from jax.experimental.pallas import tpu_sc as plsc

F32 = jnp.float32
BF16 = jnp.bfloat16

D_MODEL = 1024
DEPTH = 2
GRID_W = 64
N_BRANCH = 4
N_HEADS = 4
HEAD_V = 128
BRANCH_W = N_HEADS * HEAD_V
GDN_DK = 128
CONV_K = 5
MLA_Q_RANK = 384
MLA_KV_RANK = 256
MLA_NOPE = 128
MLA_ROPE = 64
MLA_QK = MLA_NOPE + MLA_ROPE
MLA_QK_PAD = 256
MLA_V_PAD = 256
MLSTM_DK = 64
RET_DK = 64
ROPE_BASE = 10000.0
D_FF = 3584
N_EXPERTS = 8
TOP_K = 2
EPS = 1e-6

IN_WIDTHS = (
    N_BRANCH * D_MODEL,
    N_HEADS * GDN_DK, N_HEADS * GDN_DK, BRANCH_W, BRANCH_W, 2 * N_HEADS, 2 * N_HEADS,
    MLA_Q_RANK, MLA_KV_RANK, MLA_ROPE,
    N_HEADS * MLSTM_DK, N_HEADS * MLSTM_DK, BRANCH_W, BRANCH_W, 2 * N_HEADS, 2 * N_HEADS,
    N_HEADS * RET_DK, N_HEADS * RET_DK, BRANCH_W, BRANCH_W,
)
_IN_OFF = [0] + [int(o) for o in np.cumsum(IN_WIDTHS)]

VMEM_LIMIT = 48 * 1024 * 1024
MERGE_VMEM_LIMIT = 56 * 1024 * 1024
MOE_ROWS = 1024


def _cols(group):
    return np.arange(_IN_OFF[group], _IN_OFF[group + 1])


def _rope_swap(cols):
    q = MLA_ROPE // 4
    return np.concatenate([cols[q:2 * q], cols[:q], cols[3 * q:], cols[2 * q:3 * q]])


_MAIN_PERM = np.concatenate([
    _cols(1), _cols(2), _cols(3),
    _cols(4),
    _cols(0),
    _cols(12), _cols(13),
    _cols(10), _cols(11),
    _cols(18), _cols(19),
    _cols(16), _cols(17),
    _cols(7), _cols(8), _cols(9), _rope_swap(_cols(9)),
])
N_MAIN = int(_MAIN_PERM.shape[0])
_SMALL_PERM = np.concatenate([_cols(5), _cols(6), _cols(14), _cols(15)])
N_SMALL = 128
A_LANE, B_LANE, I_LANE, F_LANE = 0, 8, 16, 24
OFF_GDN_QKV, OFF_GDN_Z, OFF_GATE = 0, 1536, 2048
OFF_MLSTM_V, OFF_MLSTM_O, OFF_MLSTM_Q, OFF_MLSTM_K = 6144, 6656, 7168, 7424
OFF_RET_V, OFF_RET_G, OFF_RET_Q, OFF_RET_K = 7680, 8192, 8704, 8960
OFF_MLA = 9216
MLA_IN_W = 768


def _cparams(sem):
    return pltpu.CompilerParams(dimension_semantics=sem, vmem_limit_bytes=VMEM_LIMIT)


def _rms(x):
    return x * lax.rsqrt(jnp.mean(x * x, axis=-1, keepdims=True) + EPS)


def _dot(a, b):
    return jnp.dot(a, b, preferred_element_type=F32)


def _dot_nt(a, b):
    return lax.dot_general(a, b, (((1,), (1,)), ((), ())), preferred_element_type=F32)


def _mod_kernel(c_ref, w_ref, b_ref, o_ref):
    c = c_ref[...]
    s = c * jax.nn.sigmoid(c)
    o_ref[...] = jnp.dot(s, w_ref[...], preferred_element_type=F32) + b_ref[...]


def modulation_vectors(cond, w_mod, b_mod):
    n = w_mod.shape[1]
    tn = 1536
    return pl.pallas_call(
        _mod_kernel,
        out_shape=jax.ShapeDtypeStruct((8, n), F32),
        grid=(n // tn,),
        in_specs=[pl.BlockSpec((8, D_MODEL), lambda j: (0, 0)),
                  pl.BlockSpec((D_MODEL, tn), lambda j: (0, j)),
                  pl.BlockSpec((1, tn), lambda j: (0, j))],
        out_specs=pl.BlockSpec((8, tn), lambda j: (0, j)),
        compiler_params=_cparams(("arbitrary",)),
        name="modulation",
    )(cond, w_mod, b_mod.reshape(1, n))


def _norm_proj_kernel(x_ref, pv_ref, w_ref, ws_ref, o_ref, os_ref, h_ref):
    @pl.when(pl.program_id(1) == 0)
    def _():
        pv = pv_ref[...]
        h = (_rms(x_ref[...]) * pv[0:1] * pv[1:2] + pv[2:3]).astype(BF16)
        h_ref[...] = h
        os_ref[...] = _dot(h, ws_ref[...])

    o_ref[...] = _dot(h_ref[...], w_ref[...]).astype(o_ref.dtype)


def norm_proj(x, pv, w, w_small, tn):
    m, d = x.shape
    n = w.shape[1]
    tm = min(m, 1024)
    return pl.pallas_call(
        _norm_proj_kernel,
        out_shape=(jax.ShapeDtypeStruct((m, n), BF16), jax.ShapeDtypeStruct((m, N_SMALL), F32)),
        grid=(m // tm, n // tn),
        in_specs=[pl.BlockSpec((tm, d), lambda i, j: (i, 0)),
                  pl.BlockSpec((8, d), lambda i, j: (0, 0)),
                  pl.BlockSpec((d, tn), lambda i, j: (0, j)),
                  pl.BlockSpec((d, N_SMALL), lambda i, j: (0, 0))],
        out_specs=(pl.BlockSpec((tm, tn), lambda i, j: (i, j)), pl.BlockSpec((tm, N_SMALL), lambda i, j: (i, 0))),
        scratch_shapes=[pltpu.VMEM((tm, d), BF16)],
        compiler_params=_cparams(("parallel", "arbitrary")),
        name="norm_proj",
    )(x, pv, w, w_small)


def _mla_proj_kernel(z_ref, cos_ref, sin_ref, gq_ref, gkv_ref, wq_ref, wkv_ref, q_ref, k_ref, v_ref):
    z = z_ref[...].astype(F32)
    tm = z.shape[0]
    cq = z[:, :MLA_Q_RANK]
    ckv = z[:, MLA_Q_RANK:MLA_Q_RANK + MLA_KV_RANK]
    kr = z[:, MLA_Q_RANK + MLA_KV_RANK:]
    cos = cos_ref[...]
    sin = sin_ref[...]
    qn = (_rms(cq) * gq_ref[...]).astype(BF16)
    kvn = (_rms(ckv) * gkv_ref[...]).astype(BF16)
    qf = _dot(qn, wq_ref[...]) * (MLA_QK ** -0.5 * math.log2(math.e))
    kvf = _dot(kvn, wkv_ref[...])
    kr_rot = kr[:, :MLA_ROPE] * cos + kr[:, MLA_ROPE:] * sin
    pad = jnp.zeros((tm, MLA_QK_PAD - MLA_QK), F32)
    lane = lax.broadcasted_iota(jnp.int32, (tm, MLA_V_PAD - HEAD_V), 1)
    ones_col = jnp.where(lane == 0, 1.0, 0.0).astype(BF16)
    for h in range(N_HEADS):
        b = h * 256
        q_rot = qf[:, b + 128:b + 192] * cos + qf[:, b + 192:b + 256] * sin
        q_ref[h] = jnp.concatenate([qf[:, b:b + 128], q_rot, pad], axis=-1).astype(BF16)
        k_ref[h] = jnp.concatenate([kvf[:, b:b + 128], kr_rot, pad], axis=-1).astype(BF16)
        v_ref[h] = jnp.concatenate([kvf[:, b + 128:b + 256].astype(BF16), ones_col], axis=-1)


def mla_project(zmain, cos, sin, gq, gkv, wq_ext, wkv):
    m = zmain.shape[0]
    tm = min(m, 1024)
    full = lambda shape: pl.BlockSpec(shape, lambda i: tuple(0 for _ in shape))
    return pl.pallas_call(
        _mla_proj_kernel,
        out_shape=(jax.ShapeDtypeStruct((N_HEADS, m, MLA_QK_PAD), BF16),
                   jax.ShapeDtypeStruct((N_HEADS, m, MLA_QK_PAD), BF16),
                   jax.ShapeDtypeStruct((N_HEADS, m, MLA_V_PAD), BF16)),
        grid=(m // tm,),
        in_specs=[pl.BlockSpec((tm, MLA_IN_W), lambda i: (i, OFF_MLA // MLA_IN_W)),
                  pl.BlockSpec((tm, MLA_ROPE), lambda i: (i, 0)),
                  pl.BlockSpec((tm, MLA_ROPE), lambda i: (i, 0)),
                  full((1, MLA_Q_RANK)), full((1, MLA_KV_RANK)),
                  full((MLA_Q_RANK, N_HEADS * 256)), full((MLA_KV_RANK, N_HEADS * 256))],
        out_specs=(pl.BlockSpec((N_HEADS, tm, MLA_QK_PAD), lambda i: (0, i, 0)),
                   pl.BlockSpec((N_HEADS, tm, MLA_QK_PAD), lambda i: (0, i, 0)),
                   pl.BlockSpec((N_HEADS, tm, MLA_V_PAD), lambda i: (0, i, 0))),
        compiler_params=_cparams(("parallel",)),
        name="mla_project",
    )(zmain, cos, sin, gq, gkv, wq_ext, wkv)


ATTN_TQ, ATTN_TK = 512, 512
ATTN_UNROLL = 32


def _attn_kernel(*refs, segs):
    q_ref, o_ref = refs[0], refs[-1]
    q = q_ref[0]
    tq = q.shape[0]
    carry = (jnp.full((tq, 1), -1e30, F32), jnp.zeros((tq, MLA_V_PAD), F32))
    for si, (tk, nk) in enumerate(segs):
        k_ref, v_ref = refs[1 + 2 * si], refs[2 + 2 * si]

        def body(c, carry, k_ref=k_ref, v_ref=v_ref, tk=tk):
            m, acc = carry
            start = pl.multiple_of(c * tk, tk)
            s = _dot_nt(q, k_ref[0, pl.ds(start, tk), :])
            m_new = jnp.maximum(m, jnp.max(s, axis=-1, keepdims=True))
            p = jnp.exp2(s - m_new).astype(BF16)
            acc = jnp.exp2(m - m_new) * acc + _dot(p, v_ref[0, pl.ds(start, tk), :])
            return m_new, acc

        carry = lax.fori_loop(0, nk, body, carry, unroll=min(ATTN_UNROLL, nk))
    _, acc = carry
    o_ref[...] = (acc[:, :HEAD_V] / acc[:, HEAD_V:HEAD_V + 1]).astype(o_ref.dtype)


def attention(q, kvs):
    _, lq, _ = q.shape
    tq = min(lq, ATTN_TQ)
    segs, args, specs = [], [], []
    for k, v in kvs:
        lk = k.shape[1]
        tk = min(lk, ATTN_TK)
        segs.append((tk, lk // tk))
        args += [k, v]
        specs += [pl.BlockSpec((1, lk, MLA_QK_PAD), lambda h, i: (h, 0, 0)),
                  pl.BlockSpec((1, lk, MLA_V_PAD), lambda h, i: (h, 0, 0))]
    return pl.pallas_call(
        functools.partial(_attn_kernel, segs=tuple(segs)),
        out_shape=jax.ShapeDtypeStruct((lq, N_HEADS * HEAD_V), BF16),
        grid=(N_HEADS, lq // tq),
        in_specs=[pl.BlockSpec((1, tq, MLA_QK_PAD), lambda h, i: (h, i, 0))] + specs,
        out_specs=pl.BlockSpec((tq, HEAD_V), lambda h, i: (i, h)),
        compiler_params=_cparams(("parallel", "arbitrary")),
        name="attention",
    )(q, *args)


def _head_post(o, gate, gain, centre, silu_gate):
    sig = jax.nn.sigmoid(gate)
    act = gate * sig if silu_gate else sig
    outs = []
    for h in range(N_HEADS):
        sl = slice(h * HEAD_V, (h + 1) * HEAD_V)
        oh = o[:, sl]
        if centre:
            oh = oh - jnp.mean(oh, axis=-1, keepdims=True)
        oh = oh * lax.rsqrt(jnp.mean(oh * oh, axis=-1, keepdims=True) + EPS)
        outs.append((oh * gain[:, sl] * act[:, sl]).astype(BF16))
    return jnp.concatenate(outs, axis=1)


def _merge_kernel(x_ref, g0_ref, g1_ref, g2_ref, g3_ref, af_ref, ab_ref, b_ref, cf_ref, cb_ref, df_ref, db_ref,
                  za_ref, zc_ref, zd_ref, gain_ref, wb_ref, wo_ref, g_ref, o_ref):
    gain = gain_ref[...]
    branches = (
        _head_post(af_ref[...] + ab_ref[...], za_ref[...].astype(F32), gain[0:1], False, True),
        b_ref[...],
        _head_post(cf_ref[...] + cb_ref[...], zc_ref[...].astype(F32), gain[1:2], False, False),
        _head_post(df_ref[...] + db_ref[...], zd_ref[...].astype(F32), gain[2:3], True, True),
    )
    s = None
    for n, (br, gate_ref) in enumerate(zip(branches, (g0_ref, g1_ref, g2_ref, g3_ref))):
        proj = _dot(br, wb_ref[n])
        gate = jax.nn.sigmoid(gate_ref[...].astype(F32))
        s = gate * proj if s is None else s + gate * proj
    m = _dot(s.astype(BF16), wo_ref[...])
    o_ref[...] = x_ref[...] + g_ref[0:1] * m


def merge_branches(x, zmain, gdn, attn, mlstm, ret, gains, w_branch, w_out, gvec):
    m = x.shape[0]
    tm = min(m, 512)
    row = lambda w: pl.BlockSpec((tm, w), lambda i: (i, 0))
    zcol = lambda off: pl.BlockSpec((tm, BRANCH_W), lambda i: (i, off // BRANCH_W))
    return pl.pallas_call(
        _merge_kernel,
        out_shape=jax.ShapeDtypeStruct((m, D_MODEL), F32),
        grid=(m // tm,),
        in_specs=[row(D_MODEL)]
                 + [pl.BlockSpec((tm, D_MODEL), lambda i, n=n: (i, OFF_GATE // D_MODEL + n)) for n in range(N_BRANCH)]
                 + [row(BRANCH_W)] * 7
                 + [zcol(OFF_GDN_Z), zcol(OFF_MLSTM_O), zcol(OFF_RET_G),
                    pl.BlockSpec((8, BRANCH_W), lambda i: (0, 0)),
                    pl.BlockSpec((N_BRANCH, BRANCH_W, D_MODEL), lambda i: (0, 0, 0)),
                    pl.BlockSpec((D_MODEL, D_MODEL), lambda i: (0, 0)),
                    pl.BlockSpec((8, D_MODEL), lambda i: (0, 0))],
        out_specs=row(D_MODEL),
        compiler_params=pltpu.CompilerParams(dimension_semantics=("parallel",), vmem_limit_bytes=MERGE_VMEM_LIMIT),
        name="merge_branches",
    )(x, zmain, zmain, zmain, zmain, gdn[0], gdn[1], attn, mlstm[0], mlstm[1], ret[0], ret[1],
      zmain, zmain, zmain, gains, w_branch, w_out, gvec)


def _ffn_kernel(x_ref, pv_ref, wg_ref, wu_ref, wd_ref, o_ref, h_ref):
    f = pl.program_id(1)

    def partial_out(h):
        g = _dot(h, wg_ref[...])
        u = _dot(h, wu_ref[...])
        act = (g * jax.nn.sigmoid(g) * u).astype(BF16)
        return pv_ref[3:4] * _dot(act, wd_ref[...])

    @pl.when(f == 0)
    def _():
        pv = pv_ref[...]
        h = (_rms(x_ref[...]) * pv[0:1] * pv[1:2] + pv[2:3]).astype(BF16)
        h_ref[...] = h
        o_ref[...] = x_ref[...] + partial_out(h)

    @pl.when(f > 0)
    def _():
        o_ref[...] += partial_out(h_ref[...])


def dense_ffn(x, pv, w_gu, w_down):
    m = x.shape[0]
    tm = min(m, 1024)
    tf = 512
    nf = D_FF // tf
    return pl.pallas_call(
        _ffn_kernel,
        out_shape=jax.ShapeDtypeStruct((m, D_MODEL), F32),
        grid=(m // tm, nf),
        in_specs=[pl.BlockSpec((tm, D_MODEL), lambda i, f: (i, 0)),
                  pl.BlockSpec((8, D_MODEL), lambda i, f: (0, 0)),
                  pl.BlockSpec((D_MODEL, tf), lambda i, f: (0, f)),
                  pl.BlockSpec((D_MODEL, tf), lambda i, f: (0, f + nf)),
                  pl.BlockSpec((tf, D_MODEL), lambda i, f: (f, 0))],
        out_specs=pl.BlockSpec((tm, D_MODEL), lambda i, f: (i, 0)),
        scratch_shapes=[pltpu.VMEM((tm, D_MODEL), BF16)],
        compiler_params=_cparams(("parallel", "arbitrary")),
        name="dense_ffn",
    )(x, pv, w_gu, w_gu, w_down)


def _router_kernel(x_ref, pv_ref, w_ref, b_ref, h_ref, route_ref):
    pv = pv_ref[...]
    h = _rms(x_ref[...]) * pv[0:1] * pv[1:2] + pv[2:3]
    h_hi = h.astype(BF16)
    h_lo = (h - h_hi.astype(F32)).astype(BF16)
    w = w_ref[...]
    w_hi = w.astype(BF16)
    w_lo = (w - w_hi.astype(F32)).astype(BF16)
    logits = _dot(h_hi, w_hi) + _dot(h_hi, w_lo) + _dot(h_lo, w_hi) + b_ref[...]
    h_ref[...] = h
    lane = lax.broadcasted_iota(jnp.int32, logits.shape, 1).astype(F32)
    m1 = jnp.max(logits, axis=1, keepdims=True)
    e1 = jnp.min(jnp.where(logits == m1, lane, 128.0), axis=1, keepdims=True)
    rest = jnp.where(lane == e1, -jnp.inf, logits)
    m2 = jnp.max(rest, axis=1, keepdims=True)
    e2 = jnp.min(jnp.where(rest == m2, lane, 128.0), axis=1, keepdims=True)
    z2 = jnp.exp(m2 - m1)
    w1 = 1.0 / (1.0 + z2)
    route_ref[...] = jnp.where(lane == 0.0, w1, jnp.where(lane == 1.0, z2 * w1, jnp.where(
        lane == 2.0, e1, jnp.where(lane == 3.0, e2, 0.0))))


def moe_router(x, pv, w_router_pad, b_router_pad):
    m = x.shape[0]
    tm = min(m, 1024)
    return pl.pallas_call(
        _router_kernel,
        out_shape=(jax.ShapeDtypeStruct((m, D_MODEL), F32), jax.ShapeDtypeStruct((m, 128), F32)),
        grid=(m // tm,),
        in_specs=[pl.BlockSpec((tm, D_MODEL), lambda i: (i, 0)),
                  pl.BlockSpec((8, D_MODEL), lambda i: (0, 0)),
                  pl.BlockSpec((D_MODEL, 128), lambda i: (0, 0)),
                  pl.BlockSpec((1, 128), lambda i: (0, 0))],
        out_specs=(pl.BlockSpec((tm, D_MODEL), lambda i: (i, 0)), pl.BlockSpec((tm, 128), lambda i: (i, 0))),
        compiler_params=_cparams(("parallel",)),
        name="moe_router",
    )(x, pv, w_router_pad, b_router_pad)


SC_CORES, SC_SUBCORES = 2, 16
SC_ROWS = 32


def sc_gather(table, idx):
    b, d = idx.shape[0], table.shape[1]
    nw = SC_CORES * SC_SUBCORES
    assert b % (nw * 2 * SC_ROWS) == 0
    per_w = b // nw
    npairs = per_w // (2 * SC_ROWS)
    mesh = plsc.VectorSubcoreMesh(core_axis_name="c", subcore_axis_name="s")

    @functools.partial(
        pl.kernel, mesh=mesh, out_type=jax.ShapeDtypeStruct((b, d), table.dtype),
        scratch_types=[pltpu.VMEM((SC_ROWS,), jnp.int32), pltpu.VMEM((SC_ROWS,), jnp.int32),
                       pltpu.VMEM((SC_ROWS, d), table.dtype), pltpu.VMEM((SC_ROWS, d), table.dtype),
                       pltpu.SemaphoreType.DMA, pltpu.SemaphoreType.DMA])
    def gather_rows(table_hbm, idx_hbm, out_hbm, idx0, idx1, rows0, rows1, sem0, sem1):
        start = (lax.axis_index("s") * SC_CORES + lax.axis_index("c")) * per_w

        def row_copy(idx_v, rows_v, sem):
            return pltpu.make_async_copy(table_hbm.at[idx_v], rows_v, sem)

        def fetch(chunk, idx_v, rows_v, sem):
            pltpu.sync_copy(idx_hbm.at[pl.ds(start + chunk * SC_ROWS, SC_ROWS)], idx_v)
            row_copy(idx_v, rows_v, sem).start()

        def write(chunk, rows_v):
            pltpu.sync_copy(rows_v, out_hbm.at[pl.ds(start + chunk * SC_ROWS, SC_ROWS)])

        fetch(0, idx0, rows0, sem0)

        def body(j, carry):
            fetch(2 * j + 1, idx1, rows1, sem1)
            row_copy(idx0, rows0, sem0).wait()
            write(2 * j, rows0)

            @pl.when(j + 1 < npairs)
            def _():
                fetch(2 * j + 2, idx0, rows0, sem0)

            row_copy(idx1, rows1, sem1).wait()
            write(2 * j + 1, rows1)
            return carry

        lax.fori_loop(0, npairs, body, 0)

    return gather_rows(table, idx)


def _expert_kernel(be_ref, nb_ref, x_ref, wg_ref, wu_ref, wd_ref, o_ref, x16_ref):
    b = pl.program_id(0)
    f = pl.program_id(1)

    live = b < nb_ref[0]

    def partial_out(x):
        g = _dot(x, wg_ref[0].astype(BF16))
        u = _dot(x, wu_ref[0].astype(BF16))
        act = (g * jax.nn.sigmoid(g) * u).astype(BF16)
        return _dot(act, wd_ref[0].astype(BF16))

    @pl.when(live & (f == 0))
    def _():
        x = x_ref[...].astype(BF16)
        x16_ref[...] = x
        o_ref[...] = partial_out(x)

    @pl.when(live & (f > 0))
    def _():
        o_ref[...] += partial_out(x16_ref[...])

    @pl.when(jnp.logical_not(live) & (f == 0))
    def _():
        o_ref[...] = jnp.zeros_like(o_ref)


def expert_ffn(block_e, n_used, xb, w_gu, w_down):
    cap = xb.shape[0]
    nb = cap // MOE_ROWS
    tf = 512
    nf = D_FF // tf

    def live(b, nbr):
        return jnp.minimum(b, nbr[0] - 1)

    def fsel(b, f, nbr):
        return jnp.where(b < nbr[0], f, nf - 1)

    grid_spec = pltpu.PrefetchScalarGridSpec(
        num_scalar_prefetch=2,
        grid=(nb, nf),
        in_specs=[pl.BlockSpec((MOE_ROWS, D_MODEL), lambda b, f, be, nbr: (live(b, nbr), 0)),
                  pl.BlockSpec((1, D_MODEL, tf), lambda b, f, be, nbr: (be[live(b, nbr)], 0, fsel(b, f, nbr))),
                  pl.BlockSpec((1, D_MODEL, tf), lambda b, f, be, nbr: (be[live(b, nbr)], 0, fsel(b, f, nbr) + nf)),
                  pl.BlockSpec((1, tf, D_MODEL), lambda b, f, be, nbr: (be[live(b, nbr)], fsel(b, f, nbr), 0))],
        out_specs=pl.BlockSpec((MOE_ROWS, D_MODEL), lambda b, f, be, nbr: (b, 0)),
        scratch_shapes=[pltpu.VMEM((MOE_ROWS, D_MODEL), BF16)],
    )
    return pl.pallas_call(
        _expert_kernel,
        out_shape=jax.ShapeDtypeStruct((cap, D_MODEL), F32),
        grid_spec=grid_spec,
        compiler_params=_cparams(("arbitrary", "arbitrary")),
        name="expert_ffn",
    )(block_e, n_used, xb, w_gu, w_gu, w_down)


def _final_kernel(x_ref, y0_ref, y1_ref, w_ref, pv_ref, o_ref):
    w = w_ref[...]
    f = w[:, 0:1] * y0_ref[...] + w[:, 1:2] * y1_ref[...]
    x = x_ref[...] + pv_ref[0:1] * f
    o_ref[...] = _rms(x) * pv_ref[1:2]


def final_combine(x, y2, w, pv):
    m = x.shape[0]
    tm = min(m, 1024)
    row = lambda wd: pl.BlockSpec((tm, wd), lambda i: (i, 0))
    return pl.pallas_call(
        _final_kernel,
        out_shape=jax.ShapeDtypeStruct((m, D_MODEL), F32),
        grid=(m // tm,),
        in_specs=[row(D_MODEL), row(D_MODEL), pl.BlockSpec((tm, D_MODEL), lambda i: (i + m // tm, 0)), row(128),
                  pl.BlockSpec((8, D_MODEL), lambda i: (0, 0))],
        out_specs=row(D_MODEL),
        compiler_params=_cparams(("parallel",)),
        name="final_combine",
    )(x, y2, y2, w, pv)


CH = 128
NEG = -1e30


def _masks(reverse):
    r = lax.broadcasted_iota(jnp.int32, (CH, CH), 0)
    c = lax.broadcasted_iota(jnp.int32, (CH, CH), 1)
    return (r <= c, r < c) if reverse else (r >= c, r > c)


def _cumsum_time(incl, x):
    m = jnp.where(incl, 1.0, 0.0).astype(BF16)
    hi = x.astype(BF16)
    lo = (x - hi.astype(F32)).astype(BF16)
    return _dot(m, hi) + _dot(m, lo)


def _cummax_time(a, reverse):
    row = lax.broadcasted_iota(jnp.int32, a.shape, 0)
    k = 1
    while k < CH:
        if reverse:
            a = jnp.maximum(a, jnp.where(row < CH - k, pltpu.roll(a, CH - k, 0), NEG))
        else:
            a = jnp.maximum(a, jnp.where(row >= k, pltpu.roll(a, k, 0), NEG))
        k *= 2
    return a


SCAN_SUB = 4


def _scan_grid(l):
    rows = min(l, SCAN_SUB * CH)
    return l // rows, rows


def _sweep(d, nsub):
    return list(range(nsub)) if d == 0 else list(range(nsub - 1, -1, -1))


def _dir_specs(nsteps, rows, width, col_block):
    return [pl.BlockSpec((rows, width), lambda n: (n, col_block)),
            pl.BlockSpec((rows, width), lambda n: (nsteps - 1 - n, col_block))]


def _whole(a):
    return pl.BlockSpec(a.shape, lambda n: tuple(0 for _ in a.shape))


def _ret_consts():
    log_gamma = np.log(1.0 - 2.0 ** (-5.0 - np.arange(N_HEADS, dtype=np.float64)))
    pos = np.arange(CH, dtype=np.float64)
    diff = pos[:, None] - pos[None, :]
    dec_f = np.where(diff >= 0, np.exp(log_gamma[:, None, None] * diff), 0.0)
    dec = np.stack([dec_f, np.transpose(dec_f, (0, 2, 1))])
    qs_f = np.exp(log_gamma[None, :] * (pos[:, None] + 1.0))
    ks_f = np.exp(log_gamma[None, :] * (CH - 1.0 - pos[:, None]))
    qs_b = np.exp(log_gamma[None, :] * (CH - pos[:, None]))
    ks_b = np.exp(log_gamma[None, :] * pos[:, None])
    rep = lambda a: np.repeat(a, RET_DK, axis=1)
    qs = np.stack([rep(qs_f), rep(qs_b)])
    ks = np.stack([rep(ks_f), rep(ks_b)])
    chunk_decay = [float(np.exp(lg * CH)) for lg in log_gamma]
    return (jnp.asarray(dec, F32), jnp.asarray(qs, F32), jnp.asarray(ks, F32)), chunk_decay


def _ret_kernel(*refs, rotary, chunk_decay):
    if rotary:
        (qf, qb, kf, kb, vf, vb, cosf, cosb, sinf, sinb, dec_ref, qs_ref, ks_ref, s0_ref,
         of_ref, ob_ref, s_ref) = refs
        tabs = ((cosf, sinf), (cosb, sinb))
    else:
        qf, qb, kf, kb, vf, vb, dec_ref, qs_ref, ks_ref, s0_ref, of_ref, ob_ref, s_ref = refs
        tabs = (None, None)

    @pl.when(pl.program_id(0) == 0)
    def _():
        s_ref[...] = s0_ref[...]

    lane = lax.broadcasted_iota(jnp.int32, (CH, N_HEADS * RET_DK), 1)
    first_half = (lane & (RET_DK - 1)) < RET_DK // 2

    nsub = qf.shape[0] // CH
    chains = {}
    for d, (q_ref, k_ref, v_ref, o_ref) in enumerate(((qf, kf, vf, of_ref), (qb, kb, vb, ob_ref))):
        for j in range(nsub):
            rows = slice(j * CH, (j + 1) * CH)
            q = q_ref[rows, :].astype(F32)
            k = k_ref[rows, :].astype(F32) * RET_DK ** -0.5
            if rotary:
                cos, sin = tabs[d][0][rows, :], tabs[d][1][rows, :]

                def rot(x, cos=cos, sin=sin):
                    swapped = jnp.where(first_half, pltpu.roll(x, N_HEADS * RET_DK - RET_DK // 2, 1),
                                        pltpu.roll(x, RET_DK // 2, 1))
                    return x * cos + swapped * sin

                q, k = rot(q), rot(k)
            qb16, kb16 = q.astype(BF16), k.astype(BF16)
            q_in = (q * qs_ref[d]).astype(BF16)
            k_out = k * ks_ref[d]
            k_t = [k_out[:, :128].T, k_out[:, 128:].T]
            for h in range(N_HEADS):
                sl = slice(h * RET_DK, (h + 1) * RET_DK)
                chains[d, j, h] = dict(
                    o_ref=o_ref, rows=rows, vh=v_ref[rows, h * HEAD_V:(h + 1) * HEAD_V], q=qb16[:, sl], k=kb16[:, sl],
                    q_in=q_in[:, sl], k_th=k_t[h // 2][(h % 2) * RET_DK:(h % 2 + 1) * RET_DK, :].astype(BF16))

    for (d, j, h), c in chains.items():
        c["p"] = (_dot_nt(c["q"], c["k"]) * dec_ref[d, h]).astype(BF16)
    for c in chains.values():
        c["intra"] = _dot(c["p"], c["vh"])
        c["update"] = _dot(c["k_th"], c["vh"])
    states = {(d, h): s_ref[d, h] for d in range(2) for h in range(N_HEADS)}
    for i in range(nsub):
        for (d, h), s in list(states.items()):
            c = chains[d, _sweep(d, nsub)[i], h]
            c["o_ref"][c["rows"], h * HEAD_V:(h + 1) * HEAD_V] = c["intra"] + _dot(c["q_in"], s.astype(BF16))
            states[d, h] = chunk_decay[h] * s + c["update"]
    for (d, h), s in states.items():
        s_ref[d, h] = s


def retention_scan(z, s0, tables):
    l = z.shape[0]
    nsteps, rows = _scan_grid(l)
    consts, chunk_decay = _ret_consts()
    rotary = tables is not None
    w = N_HEADS * RET_DK
    specs = (_dir_specs(nsteps, rows, w, OFF_RET_Q // w) + _dir_specs(nsteps, rows, w, OFF_RET_K // w)
             + _dir_specs(nsteps, rows, BRANCH_W, OFF_RET_V // BRANCH_W))
    args = [z] * 6
    if rotary:
        specs += _dir_specs(nsteps, rows, w, 0) + _dir_specs(nsteps, rows, w, 0)
        args += [tables[0], tables[0], tables[1], tables[1]]
    specs += [_whole(c) for c in consts] + [_whole(s0)]
    args += list(consts) + [s0]
    return pl.pallas_call(
        functools.partial(_ret_kernel, rotary=rotary, chunk_decay=chunk_decay),
        out_shape=(jax.ShapeDtypeStruct((l, BRANCH_W), F32), jax.ShapeDtypeStruct((l, BRANCH_W), F32),
                   jax.ShapeDtypeStruct(s0.shape, F32)),
        grid=(nsteps,),
        in_specs=specs,
        out_specs=tuple(_dir_specs(nsteps, rows, BRANCH_W, 0)) + (_whole(s0),),
        compiler_params=_cparams(("arbitrary",)),
        name="retention_scan",
    )(*args)


def _mlstm_kernel(qf, qb, kf, kb, vf, vb, smf, smb, bias_ref, c0_ref, m0_ref, of_ref, ob_ref, c_ref, m_ref):
    @pl.when(pl.program_id(0) == 0)
    def _():
        c_ref[...] = c0_ref[...]
        m_ref[...] = m0_ref[...]

    lane = lax.broadcasted_iota(jnp.int32, (CH, 128), 1)
    is_forget = (lane >= F_LANE) & (lane < F_LANE + 2 * N_HEADS)
    ones_col = jnp.where(lane == 0, 1.0, 0.0).astype(BF16)

    nsub = qf.shape[0] // CH
    chains = {}
    for d, (q_ref, k_ref, v_ref, sm_ref, o_ref) in enumerate(((qf, kf, vf, smf, of_ref), (qb, kb, vb, smb, ob_ref))):
        reverse = d == 1
        incl, _ = _masks(reverse)
        last = 0 if reverse else CH - 1
        for j in range(nsub):
            rows = slice(j * CH, (j + 1) * CH)
            pre = sm_ref[rows, :] + bias_ref[0:1]
            x = jnp.where(is_forget, jax.nn.log_sigmoid(pre), pre)
            b = _cumsum_time(incl, x)
            a = pltpu.roll(x, F_LANE - I_LANE, 1) - b
            cm = _cummax_time(a, reverse)
            a_t = a.T
            q = (q_ref[rows, :].astype(F32) * MLSTM_DK ** -0.5).astype(BF16)
            kf32 = k_ref[rows, :].astype(F32)
            k_t = [kf32[:, :128].T, kf32[:, 128:].T]
            for h in range(N_HEADS):
                gf = F_LANE + d * N_HEADS + h
                sl = slice(h * MLSTM_DK, (h + 1) * MLSTM_DK)
                cm_col, a_row = cm[:, gf:gf + 1], a_t[gf:gf + 1, :]
                chains[d, j, h] = dict(
                    o_ref=o_ref, rows=rows, q=q[:, sl], k=k_ref[rows, sl], a_row=a_row, cm_col=cm_col,
                    b_col=b[:, gf:gf + 1], b_last=b[last:last + 1, gf:gf + 1], cm_last=cm[last:last + 1, gf:gf + 1],
                    w_intra=jnp.where(incl, jnp.exp(jnp.minimum(a_row - cm_col, 0.0)), 0.0),
                    k_th=k_t[h // 2][(h % 2) * MLSTM_DK:(h % 2 + 1) * MLSTM_DK, :],
                    v_aug=jnp.concatenate([v_ref[rows, h * HEAD_V:(h + 1) * HEAD_V], ones_col], axis=1))

    for c in chains.values():
        c["p"] = (c["w_intra"] * _dot_nt(c["q"], c["k"])).astype(BF16)
    for c in chains.values():
        c["pv"] = _dot(c["p"], c["v_aug"])
    c_augs = {(d, h): c_ref[d, h] for d in range(2) for h in range(N_HEADS)}
    m_ss = {(d, h): m_ref[d, h][0:1, 0:1] for d in range(2) for h in range(N_HEADS)}
    for i in range(nsub):
        cur = {(d, h): chains[d, _sweep(d, nsub)[i], h] for (d, h) in c_augs}
        qcs = {key: _dot(c["q"], c_augs[key].astype(BF16)) for key, c in cur.items()}
        tops = {key: jnp.maximum(m_ss[key], c["cm_last"]) for key, c in cur.items()}
        updates = {key: _dot((c["k_th"] * jnp.exp(c["a_row"] - tops[key])).astype(BF16), c["v_aug"])
                   for key, c in cur.items()}
        for (d, h), c in cur.items():
            m_s = m_ss[d, h]
            mx = jnp.maximum(m_s, c["cm_col"])
            tot = jnp.exp(m_s - mx) * qcs[d, h] + jnp.exp(c["cm_col"] - mx) * c["pv"]
            den = jnp.maximum(jnp.abs(tot[:, HEAD_V:HEAD_V + 1]), jnp.exp(-(c["b_col"] + mx)))
            c["o_ref"][c["rows"], h * HEAD_V:(h + 1) * HEAD_V] = tot[:, :HEAD_V] / den
        for key, c in cur.items():
            c_augs[key] = jnp.exp(m_ss[key] - tops[key]) * c_augs[key] + updates[key]
            m_ss[key] = c["b_last"] + tops[key]
    for (d, h) in c_augs:
        c_ref[d, h] = c_augs[d, h]
        m_ref[d, h] = jnp.broadcast_to(m_ss[d, h], (8, 128))


def mlstm_scan(z, small, bias, c0, m0):
    l = z.shape[0]
    nsteps, rows = _scan_grid(l)
    w = N_HEADS * MLSTM_DK
    specs = (_dir_specs(nsteps, rows, w, OFF_MLSTM_Q // w) + _dir_specs(nsteps, rows, w, OFF_MLSTM_K // w)
             + _dir_specs(nsteps, rows, BRANCH_W, OFF_MLSTM_V // BRANCH_W) + _dir_specs(nsteps, rows, 128, 0))
    specs += [_whole(bias), _whole(c0), _whole(m0)]
    return pl.pallas_call(
        _mlstm_kernel,
        out_shape=(jax.ShapeDtypeStruct((l, BRANCH_W), F32), jax.ShapeDtypeStruct((l, BRANCH_W), F32),
                   jax.ShapeDtypeStruct(c0.shape, F32), jax.ShapeDtypeStruct(m0.shape, F32)),
        grid=(nsteps,),
        in_specs=specs,
        out_specs=tuple(_dir_specs(nsteps, rows, BRANCH_W, 0)) + (_whole(c0), _whole(m0)),
        compiler_params=_cparams(("arbitrary",)),
        name="mlstm_scan",
    )(z, z, z, z, z, z, small, small, bias, c0, m0)


QKV_W = 3 * N_HEADS * GDN_DK
HALO = 8


def _gdn_prep_kernel(x_ref, prev_ref, next_ref, w_ref, q_ref, k_ref, v_ref):
    i = pl.program_id(0)
    tm = x_ref.shape[0]
    x = x_ref[...].astype(F32)
    prev = jnp.where(i > 0, prev_ref[...].astype(F32), 0.0)
    nxt = jnp.where(i < pl.num_programs(0) - 1, next_ref[...].astype(F32), 0.0)
    xe = jnp.concatenate([prev, x, nxt], axis=0)
    w = w_ref[...]
    y = None
    for tap in range(CONV_K):
        off = HALO + tap - CONV_K // 2
        term = w[tap:tap + 1] * xe[off:off + tm]
        y = term if y is None else y + term
    y = y * jax.nn.sigmoid(y)
    hw = N_HEADS * GDN_DK
    for h in range(N_HEADS):
        sl = slice(h * GDN_DK, (h + 1) * GDN_DK)
        qh = y[:, sl]
        kh = y[:, hw + h * GDN_DK:hw + (h + 1) * GDN_DK]
        q_ref[:, sl] = (qh * lax.rsqrt(jnp.sum(qh * qh, axis=-1, keepdims=True) + EPS) * GDN_DK ** -0.5).astype(BF16)
        k_ref[:, sl] = (kh * lax.rsqrt(jnp.sum(kh * kh, axis=-1, keepdims=True) + EPS)).astype(BF16)
    v_ref[...] = y[:, 2 * hw:].astype(BF16)


def gdn_prep(z, conv_w):
    l = z.shape[0]
    tm = min(l, 256)
    nb = l // tm
    r8 = tm // HALO
    cb = OFF_GDN_QKV // QKV_W
    w8 = jnp.concatenate([conv_w.astype(F32), jnp.zeros((8 - CONV_K, QKV_W), F32)], axis=0)
    return pl.pallas_call(
        _gdn_prep_kernel,
        out_shape=tuple(jax.ShapeDtypeStruct((l, BRANCH_W), BF16) for _ in range(3)),
        grid=(nb,),
        in_specs=[pl.BlockSpec((tm, QKV_W), lambda i: (i, cb)),
                  pl.BlockSpec((HALO, QKV_W), lambda i: (jnp.maximum(i * r8 - 1, 0), cb)),
                  pl.BlockSpec((HALO, QKV_W), lambda i: (jnp.minimum((i + 1) * r8, nb * r8 - 1), cb)),
                  pl.BlockSpec((8, QKV_W), lambda i: (0, 0))],
        out_specs=tuple(pl.BlockSpec((tm, BRANCH_W), lambda i: (i, 0)) for _ in range(3)),
        compiler_params=_cparams(("parallel",)),
        name="gdn_prep",
    )(z, z, z, w8)


N_LEVELS = 7


def _gdn_kernel(qf, qb, kf, kb, vf, vb, smf, smb, par_ref, s0_ref, of_ref, ob_ref, s_ref):
    @pl.when(pl.program_id(0) == 0)
    def _():
        s_ref[...] = s0_ref[...]

    lane = lax.broadcasted_iota(jnp.int32, (CH, 128), 1)
    is_decay = lane < B_LANE
    ri = lax.broadcasted_iota(jnp.int32, (CH, CH), 0)
    ci = lax.broadcasted_iota(jnp.int32, (CH, CH), 1)
    eye = jnp.where(ri == ci, 1.0, 0.0)
    pair_masks = [((ri >> (l + 1)) == (ci >> (l + 1))) & ((ri >> l) != (ci >> l)) for l in range(N_LEVELS)]

    nsub = qf.shape[0] // CH
    chains = {}
    for d, (q_ref, k_ref, v_ref, sm_ref, o_ref) in enumerate(((qf, kf, vf, smf, of_ref), (qb, kb, vb, smb, ob_ref))):
        reverse = d == 1
        incl, strict = _masks(reverse)
        last = 0 if reverse else CH - 1
        for j in range(nsub):
            rows = slice(j * CH, (j + 1) * CH)
            sm = sm_ref[rows, :]
            log_a = -jnp.exp(par_ref[1:2]) * jax.nn.softplus(sm + par_ref[0:1])
            x = jnp.where(is_decay, log_a, jax.nn.sigmoid(sm))
            g = _cumsum_time(incl, x)
            g_t = g.T
            for h in range(N_HEADS):
                ga, gb = A_LANE + d * N_HEADS + h, B_LANE + d * N_HEADS + h
                sl = slice(h * GDN_DK, (h + 1) * GDN_DK)
                g_col, g_row = g[:, ga:ga + 1], g_t[ga:ga + 1, :]
                g_last = g[last:last + 1, ga:ga + 1]
                beta = x[:, gb:gb + 1]
                decay = jnp.where(incl, jnp.exp(jnp.minimum(g_col - g_row, 0.0)), 0.0)
                kh = k_ref[rows, sl]
                kf32 = kh.astype(F32)
                kbeta = kf32 * beta
                a = jnp.where(strict, _dot_nt(kbeta.astype(BF16), kh) * decay, 0.0)
                e_g = jnp.exp(g_col)
                rhs = jnp.concatenate([v_ref[rows, sl].astype(F32) * beta, kbeta * e_g], axis=1)
                chains[d, j, h] = dict(
                    rows=rows, sl=sl, o_ref=o_ref, a=a, rhs=rhs, s_decay=jnp.exp(g_last),
                    qk=(_dot_nt(q_ref[rows, sl], kh) * decay).astype(BF16),
                    q_in=(q_ref[rows, sl].astype(F32) * e_g).astype(BF16),
                    k_out_t=(kf32.T * jnp.exp(g_last - g_row)).astype(BF16))

    cl = list(chains.values())
    xs = [eye - jnp.where(pair_masks[0], c["a"], 0.0) for c in cl]
    for pm in pair_masks[1:]:
        ys = [_dot(jnp.where(pm, c["a"], 0.0).astype(BF16), x.astype(BF16)) for c, x in zip(cl, xs)]
        xs = [x - _dot(x.astype(BF16), y.astype(BF16)) for x, y in zip(xs, ys)]
    for c, x in zip(cl, xs):
        c["sol"] = c["rhs"] + _dot((x - eye).astype(BF16), c["rhs"].astype(BF16))
    states = {(d, h): s_ref[d, h] for d in range(2) for h in range(N_HEADS)}
    for i in range(nsub):
        cur = {(d, h): chains[d, _sweep(d, nsub)[i], h] for (d, h) in states}
        s16s = {key: s.astype(BF16) for key, s in states.items()}
        v_news = {key: (c["sol"][:, :HEAD_V] - _dot(c["sol"][:, HEAD_V:].astype(BF16), s16s[key])).astype(BF16)
                  for key, c in cur.items()}
        updates = {key: _dot(c["k_out_t"], v_news[key]) for key, c in cur.items()}
        for key, c in cur.items():
            c["o_ref"][c["rows"], c["sl"]] = _dot(c["q_in"], s16s[key]) + _dot(c["qk"], v_news[key])
        for key, c in cur.items():
            states[key] = states[key] * c["s_decay"] + updates[key]
    for (d, h), s in states.items():
        s_ref[d, h] = s


def gdn_scan(qn, kn, vn, small, par, s0):
    l = qn.shape[0]
    nsteps, rows = _scan_grid(l)
    specs = _dir_specs(nsteps, rows, BRANCH_W, 0) * 3 + _dir_specs(nsteps, rows, 128, 0)
    specs += [_whole(par), _whole(s0)]
    return pl.pallas_call(
        _gdn_kernel,
        out_shape=(jax.ShapeDtypeStruct((l, BRANCH_W), F32), jax.ShapeDtypeStruct((l, BRANCH_W), F32),
                   jax.ShapeDtypeStruct(s0.shape, F32)),
        grid=(nsteps,),
        in_specs=specs,
        out_specs=tuple(_dir_specs(nsteps, rows, BRANCH_W, 0)) + (_whole(s0),),
        compiler_params=_cparams(("arbitrary",)),
        name="gdn_scan",
    )(qn, qn, kn, kn, vn, vn, small, small, par, s0)


def _lane_rows(entries):
    r = jnp.zeros((8, 128), F32)
    for row, lane, vals in entries:
        r = r.at[row, lane:lane + vals.shape[0]].set(vals.astype(F32))
    return r


def gdn_branch(zc, sc, zl, sl, conv_w, a_log, dt_bias):
    par = _lane_rows([(0, A_LANE, dt_bias.reshape(-1)), (1, A_LANE, a_log.reshape(-1))])
    s0 = jnp.zeros((2, N_HEADS, GDN_DK, HEAD_V), F32)
    ofc, obc, s1 = gdn_scan(*gdn_prep(zc, conv_w), sc, par, s0)
    ofl, obl, _ = gdn_scan(*gdn_prep(zl, conv_w), sl, par, s1)
    return (ofc, obc), (ofl, obl)


def mlstm_branch(zc, sc, zl, sl, gate_b):
    bias = _lane_rows([(0, I_LANE, gate_b[0].reshape(-1)), (0, F_LANE, gate_b[1].reshape(-1))])
    c0 = jnp.zeros((2, N_HEADS, MLSTM_DK, 2 * HEAD_V), F32)
    m0 = jnp.zeros((2, N_HEADS, 8, 128), F32)
    ofc, obc, c1, m1 = mlstm_scan(zc, sc, bias, c0, m0)
    ofl, obl, _, _ = mlstm_scan(zl, sl, bias, c1, m1)
    return (ofc, obc), (ofl, obl)


def retention_branch(zc, zl, tables):
    s0 = jnp.zeros((2, N_HEADS, RET_DK, HEAD_V), F32)
    ofc, obc, s1 = retention_scan(zc, s0, None)
    ofl, obl, _ = retention_scan(zl, s1, tables)
    return (ofc, obc), (ofl, obl)


def _ret_rope_tables(n_lat):
    inv = np.float32(ROPE_BASE) ** (-np.arange(0, RET_DK, 2, dtype=np.float32) / np.float32(RET_DK))
    ang = np.arange(n_lat, dtype=np.float32)[:, None] * inv[None, :]
    cos = np.concatenate([np.cos(ang), np.cos(ang)], axis=-1)
    sin = np.concatenate([-np.sin(ang), np.sin(ang)], axis=-1)
    return jnp.asarray(np.tile(cos, (1, N_HEADS)), F32), jnp.asarray(np.tile(sin, (1, N_HEADS)), F32)


def _rope_tables(n_lat):
    q = MLA_ROPE // 4
    inv = np.float32(ROPE_BASE) ** (-np.arange(0, 2 * q, 2, dtype=np.float32) / np.float32(2 * q))
    t = np.arange(n_lat)
    row = (t // GRID_W).astype(np.float32)[:, None] * inv[None, :]
    col = (t % GRID_W).astype(np.float32)[:, None] * inv[None, :]
    cos = np.concatenate([np.cos(row), np.cos(row), np.cos(col), np.cos(col)], axis=-1)
    sin = np.concatenate([-np.sin(row), np.sin(row), -np.sin(col), np.sin(col)], axis=-1)
    return jnp.asarray(cos, F32), jnp.asarray(sin, F32)


def _take_cols(w, perm):
    runs, start = [], 0
    for i in range(1, len(perm) + 1):
        if i == len(perm) or perm[i] != perm[i - 1] + 1:
            runs.append((int(perm[start]), int(perm[i - 1]) + 1))
            start = i
    return jnp.concatenate([w[:, a:b].astype(BF16) for a, b in runs], axis=1)


def _pad_rows(v, n=8):
    rows = [jnp.reshape(r, (1, -1)).astype(F32) for r in v]
    d = rows[0].shape[1]
    return jnp.concatenate(rows + [jnp.zeros((n - len(rows), d), F32)], axis=0)


def _mla_weights(w_uq, w_ukv):
    wq = w_uq.reshape(MLA_Q_RANK, N_HEADS, MLA_QK)
    rope = wq[:, :, MLA_NOPE:]
    swapped = rope[:, :, _rope_swap(np.arange(MLA_ROPE))]
    wq_ext = jnp.concatenate([wq[:, :, :MLA_NOPE], rope, swapped], axis=-1).reshape(MLA_Q_RANK, N_HEADS * 256)
    return wq_ext.astype(BF16), w_ukv.astype(BF16)


def _moe(x_l, pv2, g2, w_router, b_router, w_gu, w_down, final_g):
    n = x_l.shape[0]
    w_pad = jnp.concatenate([w_router, jnp.zeros((D_MODEL, 128 - N_EXPERTS), F32)], axis=1)
    b_pad = jnp.concatenate([b_router, jnp.full((128 - N_EXPERTS,), -1e30, F32)]).reshape(1, 128)
    h2, route = moe_router(x_l, pv2, w_pad, b_pad)
    n_assign = n * TOP_K
    experts = jnp.arange(N_EXPERTS)[None, :]
    e0, e1 = route[:, TOP_K].astype(jnp.int32), route[:, TOP_K + 1].astype(jnp.int32)
    hot0, hot1 = (e0[:, None] == experts).astype(jnp.int32), (e1[:, None] == experts).astype(jnp.int32)
    c0, c1 = jnp.cumsum(hot0, axis=0), jnp.cumsum(hot1, axis=0)
    before1 = c1 - hot1
    counts = c0[-1] + c1[-1]
    rank0 = jnp.sum(hot0 * (c0 - hot0 + before1), axis=1)
    rank1 = jnp.sum(hot1 * (c0 + before1), axis=1)
    padded = (counts + MOE_ROWS - 1) // MOE_ROWS * MOE_ROWS
    pad_end = jnp.cumsum(padded)
    pad_start = pad_end - padded
    dest = jnp.concatenate([pad_start[e0] + rank0, pad_start[e1] + rank1])
    nb = n_assign // MOE_ROWS + N_EXPERTS
    cap = nb * MOE_ROWS
    block_start = jnp.arange(nb, dtype=pad_end.dtype) * MOE_ROWS
    block_e = jnp.minimum(jnp.sum(block_start[:, None] >= pad_end[None, :], axis=1), N_EXPERTS - 1).astype(jnp.int32)
    n_used = (pad_end[-1] // MOE_ROWS).astype(jnp.int32).reshape(1)
    slot_tok = (jnp.arange(cap, dtype=jnp.int32) % n).at[dest].set(jnp.arange(n_assign, dtype=jnp.int32) % n)
    yb = expert_ffn(block_e, n_used, sc_gather(h2, slot_tok), w_gu, w_down)
    y2 = sc_gather(yb, dest.astype(jnp.int32))
    return final_combine(x_l, y2, route, _pad_rows([g2, final_g]))


def kernel(x, c, ctx, c_ctx, w_mod, b_mod, norm1_g, norm2_g, w_in, gdn_conv_w, gdn_a_log, gdn_dt_bias, gdn_norm_g, mla_q_norm_g, mla_kv_norm_g, mla_w_uq, mla_w_ukv, mlstm_gate_b, mlstm_norm_g, ret_norm_g, w_branch, w_out, ffn_w_in, ffn_w_down, moe_w_router, moe_b_router, moe_w_in, moe_w_down, final_norm_g):
    n_lat = x.shape[1]
    n_ctx = ctx.shape[1]
    x_l, x_c = x[0], ctx[0]
    cond = _pad_rows([c_ctx, c[0]])
    cos_l, sin_l = _rope_tables(n_lat)
    cos_c, sin_c = jnp.ones((n_ctx, MLA_ROPE), F32), jnp.zeros((n_ctx, MLA_ROPE), F32)
    ret_tables = _ret_rope_tables(n_lat)
    out = None
    for li in range(DEPTH):
        last = li == DEPTH - 1
        mod = modulation_vectors(cond, w_mod[li], b_mod[li])
        csh1, csc1, cg1, csh2, csc2, cg2 = jnp.split(mod[0], 6)
        sh1, sc1, g1, sh2, sc2, g2 = jnp.split(mod[1], 6)
        w_main = _take_cols(w_in[li], _MAIN_PERM)
        w_small = _take_cols(w_in[li], _SMALL_PERM)
        w_small = jnp.concatenate([w_small, jnp.zeros((D_MODEL, N_SMALL - w_small.shape[1]), BF16)], axis=1)
        pv_l = _pad_rows([norm1_g[li], 1 + sc1, sh1])
        pv_c = _pad_rows([norm1_g[li], 1 + csc1, csh1])
        zl, sl = norm_proj(x_l, pv_l, w_main, w_small, 1664)
        zc, sc = norm_proj(x_c, pv_c, w_main, w_small, 1664)

        a_c, a_l = gdn_branch(zc, sc, zl, sl, gdn_conv_w[li], gdn_a_log[li], gdn_dt_bias[li])
        c_c, c_l = mlstm_branch(zc, sc, zl, sl, mlstm_gate_b[li])
        d_c, d_l = retention_branch(zc, zl, ret_tables)
        gains = _pad_rows([gdn_norm_g[li], mlstm_norm_g[li], ret_norm_g[li]])

        wq_ext, wkv = _mla_weights(mla_w_uq[li], mla_w_ukv[li])
        gq, gkv = mla_q_norm_g[li].reshape(1, -1), mla_kv_norm_g[li].reshape(1, -1)
        ql, kl, vl = mla_project(zl, cos_l, sin_l, gq, gkv, wq_ext, wkv)
        qc, kc, vc = mla_project(zc, cos_c, sin_c, gq, gkv, wq_ext, wkv)
        b_l = attention(ql, [(kl, vl), (kc, vc)])

        wb = w_branch[li].astype(BF16)
        wo = w_out[li].astype(BF16)
        x_l = merge_branches(x_l, zl, a_l, b_l, c_l, d_l, gains, wb, wo, _pad_rows([g1]))
        if not last:
            b_c = attention(qc, [(kc, vc)])
            x_c = merge_branches(x_c, zc, a_c, b_c, c_c, d_c, gains, wb, wo, _pad_rows([cg1]))

        if li % 2 == 0:
            w_gu = ffn_w_in[li // 2].astype(BF16)
            w_dn = ffn_w_down[li // 2].astype(BF16)
            assert not last
            x_l = dense_ffn(x_l, _pad_rows([norm2_g[li], 1 + sc2, sh2, g2]), w_gu, w_dn)
            x_c = dense_ffn(x_c, _pad_rows([norm2_g[li], 1 + csc2, csh2, cg2]), w_gu, w_dn)
        else:
            assert last
            out = _moe(x_l, _pad_rows([norm2_g[li], 1 + sc2, sh2]), g2, moe_w_router[li // 2],
                       moe_b_router[li // 2], moe_w_in[li // 2], moe_w_down[li // 2], final_norm_g)
    return out[None]
```

```python
import functools
import math

import numpy as np
import jax
import jax.numpy as jnp
from jax import lax
from jax.experimental import pallas as pl
from jax.experimental.pallas import tpu as pltpu
from jax.experimental.pallas import tpu_sc as plsc

F32 = jnp.float32
BF16 = jnp.bfloat16

D_MODEL = 1024
DEPTH = 2
GRID_W = 64
N_BRANCH = 4
N_HEADS = 4
HEAD_V = 128
BRANCH_W = N_HEADS * HEAD_V
GDN_DK = 128
CONV_K = 5
MLA_Q_RANK = 384
MLA_KV_RANK = 256
MLA_NOPE = 128
MLA_ROPE = 64
MLA_QK = MLA_NOPE + MLA_ROPE
MLA_QK_PAD = 256
MLA_V_PAD = 256
MLSTM_DK = 64
RET_DK = 64
ROPE_BASE = 10000.0
D_FF = 3584
N_EXPERTS = 8
TOP_K = 2
EPS = 1e-6

IN_WIDTHS = (
    N_BRANCH * D_MODEL,
    N_HEADS * GDN_DK, N_HEADS * GDN_DK, BRANCH_W, BRANCH_W, 2 * N_HEADS, 2 * N_HEADS,
    MLA_Q_RANK, MLA_KV_RANK, MLA_ROPE,
    N_HEADS * MLSTM_DK, N_HEADS * MLSTM_DK, BRANCH_W, BRANCH_W, 2 * N_HEADS, 2 * N_HEADS,
    N_HEADS * RET_DK, N_HEADS * RET_DK, BRANCH_W, BRANCH_W,
)
_IN_OFF = [0] + [int(o) for o in np.cumsum(IN_WIDTHS)]

VMEM_LIMIT = 48 * 1024 * 1024
MERGE_VMEM_LIMIT = 56 * 1024 * 1024
MOE_ROWS = 1024


def _cols(group):
    return np.arange(_IN_OFF[group], _IN_OFF[group + 1])


def _rope_swap(cols):
    q = MLA_ROPE // 4
    return np.concatenate([cols[q:2 * q], cols[:q], cols[3 * q:], cols[2 * q:3 * q]])


_MAIN_PERM = np.concatenate([
    _cols(1), _cols(2), _cols(3),
    _cols(4),
    _cols(0),
    _cols(12), _cols(13),
    _cols(10), _cols(11),
    _cols(18), _cols(19),
    _cols(16), _cols(17),
    _cols(7), _cols(8), _cols(9), _rope_swap(_cols(9)),
])
N_MAIN = int(_MAIN_PERM.shape[0])
_SMALL_PERM = np.concatenate([_cols(5), _cols(6), _cols(14), _cols(15)])
N_SMALL = 128
A_LANE, B_LANE, I_LANE, F_LANE = 0, 8, 16, 24
OFF_GDN_QKV, OFF_GDN_Z, OFF_GATE = 0, 1536, 2048
OFF_MLSTM_V, OFF_MLSTM_O, OFF_MLSTM_Q, OFF_MLSTM_K = 6144, 6656, 7168, 7424
OFF_RET_V, OFF_RET_G, OFF_RET_Q, OFF_RET_K = 7680, 8192, 8704, 8960
OFF_MLA = 9216
MLA_IN_W = 768


def _cparams(sem):
    return pltpu.CompilerParams(dimension_semantics=sem, vmem_limit_bytes=VMEM_LIMIT)


def _rms(x):
    return x * lax.rsqrt(jnp.mean(x * x, axis=-1, keepdims=True) + EPS)


def _dot(a, b):
    return jnp.dot(a, b, preferred_element_type=F32)


def _dot_nt(a, b):
    return lax.dot_general(a, b, (((1,), (1,)), ((), ())), preferred_element_type=F32)


def _mod_kernel(c_ref, w_ref, b_ref, o_ref):
    c = c_ref[...]
    s = c * jax.nn.sigmoid(c)
    o_ref[...] = jnp.dot(s, w_ref[...], preferred_element_type=F32) + b_ref[...]


def modulation_vectors(cond, w_mod, b_mod):
    n = w_mod.shape[1]
    tn = 1536
    return pl.pallas_call(
        _mod_kernel,
        out_shape=jax.ShapeDtypeStruct((8, n), F32),
        grid=(n // tn,),
        in_specs=[pl.BlockSpec((8, D_MODEL), lambda j: (0, 0)),
                  pl.BlockSpec((D_MODEL, tn), lambda j: (0, j)),
                  pl.BlockSpec((1, tn), lambda j: (0, j))],
        out_specs=pl.BlockSpec((8, tn), lambda j: (0, j)),
        compiler_params=_cparams(("arbitrary",)),
        name="modulation",
    )(cond, w_mod, b_mod.reshape(1, n))


def _norm_proj_kernel(x_ref, pv_ref, w_ref, ws_ref, o_ref, os_ref, h_ref):
    @pl.when(pl.program_id(1) == 0)
    def _():
        pv = pv_ref[...]
        h = (_rms(x_ref[...]) * pv[0:1] * pv[1:2] + pv[2:3]).astype(BF16)
        h_ref[...] = h
        os_ref[...] = _dot(h, ws_ref[...])

    o_ref[...] = _dot(h_ref[...], w_ref[...]).astype(o_ref.dtype)


def norm_proj(x, pv, w, w_small, tn):
    m, d = x.shape
    n = w.shape[1]
    tm = min(m, 1024)
    return pl.pallas_call(
        _norm_proj_kernel,
        out_shape=(jax.ShapeDtypeStruct((m, n), BF16), jax.ShapeDtypeStruct((m, N_SMALL), F32)),
        grid=(m // tm, n // tn),
        in_specs=[pl.BlockSpec((tm, d), lambda i, j: (i, 0)),
                  pl.BlockSpec((8, d), lambda i, j: (0, 0)),
                  pl.BlockSpec((d, tn), lambda i, j: (0, j)),
                  pl.BlockSpec((d, N_SMALL), lambda i, j: (0, 0))],
        out_specs=(pl.BlockSpec((tm, tn), lambda i, j: (i, j)), pl.BlockSpec((tm, N_SMALL), lambda i, j: (i, 0))),
        scratch_shapes=[pltpu.VMEM((tm, d), BF16)],
        compiler_params=pltpu.CompilerParams(dimension_semantics=("parallel", "arbitrary"),
                                             vmem_limit_bytes=MERGE_VMEM_LIMIT),
        name="norm_proj",
    )(x, pv, w, w_small)


def _mla_proj_kernel(z_ref, cos_ref, sin_ref, gq_ref, gkv_ref, wq_ref, wkv_ref, q_ref, k_ref, v_ref):
    z = z_ref[...].astype(F32)
    tm = z.shape[0]
    cq = z[:, :MLA_Q_RANK]
    ckv = z[:, MLA_Q_RANK:MLA_Q_RANK + MLA_KV_RANK]
    kr = z[:, MLA_Q_RANK + MLA_KV_RANK:]
    cos = cos_ref[...]
    sin = sin_ref[...]
    qn = (_rms(cq) * gq_ref[...]).astype(BF16)
    kvn = (_rms(ckv) * gkv_ref[...]).astype(BF16)
    qf = _dot(qn, wq_ref[...]) * (MLA_QK ** -0.5 * math.log2(math.e))
    kvf = _dot(kvn, wkv_ref[...])
    kr_rot = kr[:, :MLA_ROPE] * cos + kr[:, MLA_ROPE:] * sin
    pad = jnp.zeros((tm, MLA_QK_PAD - MLA_QK), F32)
    lane = lax.broadcasted_iota(jnp.int32, (tm, MLA_V_PAD - HEAD_V), 1)
    ones_col = jnp.where(lane == 0, 1.0, 0.0).astype(BF16)
    for h in range(N_HEADS):
        b = h * 256
        q_rot = qf[:, b + 128:b + 192] * cos + qf[:, b + 192:b + 256] * sin
        q_ref[h] = jnp.concatenate([qf[:, b:b + 128], q_rot, pad], axis=-1).astype(BF16)
        k_ref[h] = jnp.concatenate([kvf[:, b:b + 128], kr_rot, pad], axis=-1).astype(BF16)
        v_ref[h] = jnp.concatenate([kvf[:, b + 128:b + 256].astype(BF16), ones_col], axis=-1)


def mla_project(zmain, cos, sin, gq, gkv, wq_ext, wkv):
    m = zmain.shape[0]
    tm = min(m, 1024)
    full = lambda shape: pl.BlockSpec(shape, lambda i: tuple(0 for _ in shape))
    return pl.pallas_call(
        _mla_proj_kernel,
        out_shape=(jax.ShapeDtypeStruct((N_HEADS, m, MLA_QK_PAD), BF16),
                   jax.ShapeDtypeStruct((N_HEADS, m, MLA_QK_PAD), BF16),
                   jax.ShapeDtypeStruct((N_HEADS, m, MLA_V_PAD), BF16)),
        grid=(m // tm,),
        in_specs=[pl.BlockSpec((tm, MLA_IN_W), lambda i: (i, OFF_MLA // MLA_IN_W)),
                  pl.BlockSpec((tm, MLA_ROPE), lambda i: (i, 0)),
                  pl.BlockSpec((tm, MLA_ROPE), lambda i: (i, 0)),
                  full((1, MLA_Q_RANK)), full((1, MLA_KV_RANK)),
                  full((MLA_Q_RANK, N_HEADS * 256)), full((MLA_KV_RANK, N_HEADS * 256))],
        out_specs=(pl.BlockSpec((N_HEADS, tm, MLA_QK_PAD), lambda i: (0, i, 0)),
                   pl.BlockSpec((N_HEADS, tm, MLA_QK_PAD), lambda i: (0, i, 0)),
                   pl.BlockSpec((N_HEADS, tm, MLA_V_PAD), lambda i: (0, i, 0))),
        compiler_params=_cparams(("parallel",)),
        name="mla_project",
    )(zmain, cos, sin, gq, gkv, wq_ext, wkv)


ATTN_TQ, ATTN_TK = 512, 512
ATTN_UNROLL = 32


def _attn_kernel(*refs, segs):
    q_ref, o_ref = refs[0], refs[-1]
    q = q_ref[0]
    tq = q.shape[0]
    carry = (jnp.full((tq, 1), -1e30, F32), jnp.zeros((tq, MLA_V_PAD), F32))
    for si, (tk, nk) in enumerate(segs):
        k_ref, v_ref = refs[1 + 2 * si], refs[2 + 2 * si]

        def body(c, carry, k_ref=k_ref, v_ref=v_ref, tk=tk):
            m, acc = carry
            start = pl.multiple_of(c * tk, tk)
            s = _dot_nt(q, k_ref[0, pl.ds(start, tk), :])
            m_new = jnp.maximum(m, jnp.max(s, axis=-1, keepdims=True))
            p = jnp.exp2(s - m_new).astype(BF16)
            acc = jnp.exp2(m - m_new) * acc + _dot(p, v_ref[0, pl.ds(start, tk), :])
            return m_new, acc

        carry = lax.fori_loop(0, nk, body, carry, unroll=min(ATTN_UNROLL, nk))
    _, acc = carry
    o_ref[...] = (acc[:, :HEAD_V] / acc[:, HEAD_V:HEAD_V + 1]).astype(o_ref.dtype)


def attention(q, kvs):
    _, lq, _ = q.shape
    tq = min(lq, ATTN_TQ)
    segs, args, specs = [], [], []
    for k, v in kvs:
        lk = k.shape[1]
        tk = min(lk, ATTN_TK)
        segs.append((tk, lk // tk))
        args += [k, v]
        specs += [pl.BlockSpec((1, lk, MLA_QK_PAD), lambda h, i: (h, 0, 0)),
                  pl.BlockSpec((1, lk, MLA_V_PAD), lambda h, i: (h, 0, 0))]
    return pl.pallas_call(
        functools.partial(_attn_kernel, segs=tuple(segs)),
        out_shape=jax.ShapeDtypeStruct((lq, N_HEADS * HEAD_V), BF16),
        grid=(N_HEADS, lq // tq),
        in_specs=[pl.BlockSpec((1, tq, MLA_QK_PAD), lambda h, i: (h, i, 0))] + specs,
        out_specs=pl.BlockSpec((tq, HEAD_V), lambda h, i: (i, h)),
        compiler_params=_cparams(("parallel", "arbitrary")),
        name="attention",
    )(q, *args)


def _head_post(o, gate, gain, centre, silu_gate):
    sig = jax.nn.sigmoid(gate)
    act = gate * sig if silu_gate else sig
    outs = []
    for h in range(N_HEADS):
        sl = slice(h * HEAD_V, (h + 1) * HEAD_V)
        oh = o[:, sl]
        if centre:
            oh = oh - jnp.mean(oh, axis=-1, keepdims=True)
        oh = oh * lax.rsqrt(jnp.mean(oh * oh, axis=-1, keepdims=True) + EPS)
        outs.append((oh * gain[:, sl] * act[:, sl]).astype(BF16))
    return jnp.concatenate(outs, axis=1)


def _merge_kernel(x_ref, g0_ref, g1_ref, g2_ref, g3_ref, af_ref, ab_ref, b_ref, cf_ref, cb_ref, df_ref, db_ref,
                  za_ref, zc_ref, zd_ref, gain_ref, wb_ref, wo_ref, g_ref, o_ref):
    gain = gain_ref[...]
    branches = (
        _head_post(af_ref[...] + ab_ref[...], za_ref[...].astype(F32), gain[0:1], False, True),
        b_ref[...],
        _head_post(cf_ref[...] + cb_ref[...], zc_ref[...].astype(F32), gain[1:2], False, False),
        _head_post(df_ref[...] + db_ref[...], zd_ref[...].astype(F32), gain[2:3], True, True),
    )
    s = None
    for n, (br, gate_ref) in enumerate(zip(branches, (g0_ref, g1_ref, g2_ref, g3_ref))):
        proj = _dot(br, wb_ref[n])
        gate = jax.nn.sigmoid(gate_ref[...].astype(F32))
        s = gate * proj if s is None else s + gate * proj
    m = _dot(s.astype(BF16), wo_ref[...])
    o_ref[...] = x_ref[...] + g_ref[0:1] * m


def merge_branches(x, zmain, gdn, attn, mlstm, ret, gains, w_branch, w_out, gvec):
    m = x.shape[0]
    tm = min(m, 512)
    row = lambda w: pl.BlockSpec((tm, w), lambda i: (i, 0))
    zcol = lambda off: pl.BlockSpec((tm, BRANCH_W), lambda i: (i, off // BRANCH_W))
    return pl.pallas_call(
        _merge_kernel,
        out_shape=jax.ShapeDtypeStruct((m, D_MODEL), F32),
        grid=(m // tm,),
        in_specs=[row(D_MODEL)]
                 + [pl.BlockSpec((tm, D_MODEL), lambda i, n=n: (i, OFF_GATE // D_MODEL + n)) for n in range(N_BRANCH)]
                 + [row(BRANCH_W)] * 7
                 + [zcol(OFF_GDN_Z), zcol(OFF_MLSTM_O), zcol(OFF_RET_G),
                    pl.BlockSpec((8, BRANCH_W), lambda i: (0, 0)),
                    pl.BlockSpec((N_BRANCH, BRANCH_W, D_MODEL), lambda i: (0, 0, 0)),
                    pl.BlockSpec((D_MODEL, D_MODEL), lambda i: (0, 0)),
                    pl.BlockSpec((8, D_MODEL), lambda i: (0, 0))],
        out_specs=row(D_MODEL),
        compiler_params=pltpu.CompilerParams(dimension_semantics=("parallel",), vmem_limit_bytes=MERGE_VMEM_LIMIT),
        name="merge_branches",
    )(x, zmain, zmain, zmain, zmain, gdn[0], gdn[1], attn, mlstm[0], mlstm[1], ret[0], ret[1],
      zmain, zmain, zmain, gains, w_branch, w_out, gvec)


def _ffn_kernel(x_ref, pv_ref, wg_ref, wu_ref, wd_ref, o_ref, h_ref):
    f = pl.program_id(1)

    def partial_out(h):
        g = _dot(h, wg_ref[...])
        u = _dot(h, wu_ref[...])
        act = (g * jax.nn.sigmoid(g) * u).astype(BF16)
        return pv_ref[3:4] * _dot(act, wd_ref[...])

    @pl.when(f == 0)
    def _():
        pv = pv_ref[...]
        h = (_rms(x_ref[...]) * pv[0:1] * pv[1:2] + pv[2:3]).astype(BF16)
        h_ref[...] = h
        o_ref[...] = x_ref[...] + partial_out(h)

    @pl.when(f > 0)
    def _():
        o_ref[...] += partial_out(h_ref[...])


def dense_ffn(x, pv, w_gu, w_down):
    m = x.shape[0]
    tm = min(m, 1024)
    tf = 512
    nf = D_FF // tf
    return pl.pallas_call(
        _ffn_kernel,
        out_shape=jax.ShapeDtypeStruct((m, D_MODEL), F32),
        grid=(m // tm, nf),
        in_specs=[pl.BlockSpec((tm, D_MODEL), lambda i, f: (i, 0)),
                  pl.BlockSpec((8, D_MODEL), lambda i, f: (0, 0)),
                  pl.BlockSpec((D_MODEL, tf), lambda i, f: (0, f)),
                  pl.BlockSpec((D_MODEL, tf), lambda i, f: (0, f + nf)),
                  pl.BlockSpec((tf, D_MODEL), lambda i, f: (f, 0))],
        out_specs=pl.BlockSpec((tm, D_MODEL), lambda i, f: (i, 0)),
        scratch_shapes=[pltpu.VMEM((tm, D_MODEL), BF16)],
        compiler_params=_cparams(("parallel", "arbitrary")),
        name="dense_ffn",
    )(x, pv, w_gu, w_gu, w_down)


def _router_kernel(x_ref, pv_ref, w_ref, b_ref, h_ref, route_ref):
    pv = pv_ref[...]
    h = _rms(x_ref[...]) * pv[0:1] * pv[1:2] + pv[2:3]
    h_hi = h.astype(BF16)
    h_lo = (h - h_hi.astype(F32)).astype(BF16)
    w = w_ref[...]
    w_hi = w.astype(BF16)
    w_lo = (w - w_hi.astype(F32)).astype(BF16)
    logits = _dot(h_hi, w_hi) + _dot(h_hi, w_lo) + _dot(h_lo, w_hi) + b_ref[...]
    h_ref[...] = h
    lane = lax.broadcasted_iota(jnp.int32, logits.shape, 1).astype(F32)
    m1 = jnp.max(logits, axis=1, keepdims=True)
    e1 = jnp.min(jnp.where(logits == m1, lane, 128.0), axis=1, keepdims=True)
    rest = jnp.where(lane == e1, -jnp.inf, logits)
    m2 = jnp.max(rest, axis=1, keepdims=True)
    e2 = jnp.min(jnp.where(rest == m2, lane, 128.0), axis=1, keepdims=True)
    z2 = jnp.exp(m2 - m1)
    w1 = 1.0 / (1.0 + z2)
    route_ref[...] = jnp.where(lane == 0.0, w1, jnp.where(lane == 1.0, z2 * w1, jnp.where(
        lane == 2.0, e1, jnp.where(lane == 3.0, e2, 0.0))))


def moe_router(x, pv, w_router_pad, b_router_pad):
    m = x.shape[0]
    tm = min(m, 1024)
    return pl.pallas_call(
        _router_kernel,
        out_shape=(jax.ShapeDtypeStruct((m, D_MODEL), F32), jax.ShapeDtypeStruct((m, 128), F32)),
        grid=(m // tm,),
        in_specs=[pl.BlockSpec((tm, D_MODEL), lambda i: (i, 0)),
                  pl.BlockSpec((8, D_MODEL), lambda i: (0, 0)),
                  pl.BlockSpec((D_MODEL, 128), lambda i: (0, 0)),
                  pl.BlockSpec((1, 128), lambda i: (0, 0))],
        out_specs=(pl.BlockSpec((tm, D_MODEL), lambda i: (i, 0)), pl.BlockSpec((tm, 128), lambda i: (i, 0))),
        compiler_params=_cparams(("parallel",)),
        name="moe_router",
    )(x, pv, w_router_pad, b_router_pad)


SC_CORES, SC_SUBCORES = 2, 16
SC_ROWS = 32


def sc_gather(table, idx):
    b, d = idx.shape[0], table.shape[1]
    nw = SC_CORES * SC_SUBCORES
    assert b % (nw * 2 * SC_ROWS) == 0
    per_w = b // nw
    npairs = per_w // (2 * SC_ROWS)
    mesh = plsc.VectorSubcoreMesh(core_axis_name="c", subcore_axis_name="s")

    @functools.partial(
        pl.kernel, mesh=mesh, out_type=jax.ShapeDtypeStruct((b, d), table.dtype),
        scratch_types=[pltpu.VMEM((SC_ROWS,), jnp.int32), pltpu.VMEM((SC_ROWS,), jnp.int32),
                       pltpu.VMEM((SC_ROWS, d), table.dtype), pltpu.VMEM((SC_ROWS, d), table.dtype),
                       pltpu.SemaphoreType.DMA, pltpu.SemaphoreType.DMA])
    def gather_rows(table_hbm, idx_hbm, out_hbm, idx0, idx1, rows0, rows1, sem0, sem1):
        start = (lax.axis_index("s") * SC_CORES + lax.axis_index("c")) * per_w

        def row_copy(idx_v, rows_v, sem):
            return pltpu.make_async_copy(table_hbm.at[idx_v], rows_v, sem)

        def fetch(chunk, idx_v, rows_v, sem):
            pltpu.sync_copy(idx_hbm.at[pl.ds(start + chunk * SC_ROWS, SC_ROWS)], idx_v)
            row_copy(idx_v, rows_v, sem).start()

        def write(chunk, rows_v):
            pltpu.sync_copy(rows_v, out_hbm.at[pl.ds(start + chunk * SC_ROWS, SC_ROWS)])

        fetch(0, idx0, rows0, sem0)

        def body(j, carry):
            fetch(2 * j + 1, idx1, rows1, sem1)
            row_copy(idx0, rows0, sem0).wait()
            write(2 * j, rows0)

            @pl.when(j + 1 < npairs)
            def _():
                fetch(2 * j + 2, idx0, rows0, sem0)

            row_copy(idx1, rows1, sem1).wait()
            write(2 * j + 1, rows1)
            return carry

        lax.fori_loop(0, npairs, body, 0)

    return gather_rows(table, idx)


def _expert_kernel(be_ref, nb_ref, x_ref, wg_ref, wu_ref, wd_ref, o_ref, x16_ref):
    b = pl.program_id(0)
    f = pl.program_id(1)

    live = b < nb_ref[0]

    def partial_out(x):
        g = _dot(x, wg_ref[0].astype(BF16))
        u = _dot(x, wu_ref[0].astype(BF16))
        act = (g * jax.nn.sigmoid(g) * u).astype(BF16)
        return _dot(act, wd_ref[0].astype(BF16))

    @pl.when(live & (f == 0))
    def _():
        x = x_ref[...].astype(BF16)
        x16_ref[...] = x
        o_ref[...] = partial_out(x)

    @pl.when(live & (f > 0))
    def _():
        o_ref[...] += partial_out(x16_ref[...])

    @pl.when(jnp.logical_not(live) & (f == 0))
    def _():
        o_ref[...] = jnp.zeros_like(o_ref)


def expert_ffn(block_e, n_used, xb, w_gu, w_down):
    cap = xb.shape[0]
    nb = cap // MOE_ROWS
    tf = 512
    nf = D_FF // tf

    def live(b, nbr):
        return jnp.minimum(b, nbr[0] - 1)

    def fsel(b, f, nbr):
        return jnp.where(b < nbr[0], f, nf - 1)

    grid_spec = pltpu.PrefetchScalarGridSpec(
        num_scalar_prefetch=2,
        grid=(nb, nf),
        in_specs=[pl.BlockSpec((MOE_ROWS, D_MODEL), lambda b, f, be, nbr: (live(b, nbr), 0)),
                  pl.BlockSpec((1, D_MODEL, tf), lambda b, f, be, nbr: (be[live(b, nbr)], 0, fsel(b, f, nbr))),
                  pl.BlockSpec((1, D_MODEL, tf), lambda b, f, be, nbr: (be[live(b, nbr)], 0, fsel(b, f, nbr) + nf)),
                  pl.BlockSpec((1, tf, D_MODEL), lambda b, f, be, nbr: (be[live(b, nbr)], fsel(b, f, nbr), 0))],
        out_specs=pl.BlockSpec((MOE_ROWS, D_MODEL), lambda b, f, be, nbr: (b, 0)),
        scratch_shapes=[pltpu.VMEM((MOE_ROWS, D_MODEL), BF16)],
    )
    return pl.pallas_call(
        _expert_kernel,
        out_shape=jax.ShapeDtypeStruct((cap, D_MODEL), F32),
        grid_spec=grid_spec,
        compiler_params=_cparams(("arbitrary", "arbitrary")),
        name="expert_ffn",
    )(block_e, n_used, xb, w_gu, w_gu, w_down)


def _final_kernel(x_ref, y0_ref, y1_ref, w_ref, pv_ref, o_ref):
    w = w_ref[...]
    f = w[:, 0:1] * y0_ref[...] + w[:, 1:2] * y1_ref[...]
    x = x_ref[...] + pv_ref[0:1] * f
    o_ref[...] = _rms(x) * pv_ref[1:2]


def final_combine(x, y2, w, pv):
    m = x.shape[0]
    tm = min(m, 1024)
    row = lambda wd: pl.BlockSpec((tm, wd), lambda i: (i, 0))
    return pl.pallas_call(
        _final_kernel,
        out_shape=jax.ShapeDtypeStruct((m, D_MODEL), F32),
        grid=(m // tm,),
        in_specs=[row(D_MODEL), row(D_MODEL), pl.BlockSpec((tm, D_MODEL), lambda i: (i + m // tm, 0)), row(128),
                  pl.BlockSpec((8, D_MODEL), lambda i: (0, 0))],
        out_specs=row(D_MODEL),
        compiler_params=_cparams(("parallel",)),
        name="final_combine",
    )(x, y2, y2, w, pv)


CH = 128
NEG = -1e30


def _masks(reverse):
    r = lax.broadcasted_iota(jnp.int32, (CH, CH), 0)
    c = lax.broadcasted_iota(jnp.int32, (CH, CH), 1)
    return (r <= c, r < c) if reverse else (r >= c, r > c)


def _cumsum_time(incl, x):
    m = jnp.where(incl, 1.0, 0.0).astype(BF16)
    hi = x.astype(BF16)
    lo = (x - hi.astype(F32)).astype(BF16)
    return _dot(m, hi) + _dot(m, lo)


def _cummax_time(a, reverse):
    row = lax.broadcasted_iota(jnp.int32, a.shape, 0)
    k = 1
    while k < CH:
        if reverse:
            a = jnp.maximum(a, jnp.where(row < CH - k, pltpu.roll(a, CH - k, 0), NEG))
        else:
            a = jnp.maximum(a, jnp.where(row >= k, pltpu.roll(a, k, 0), NEG))
        k *= 2
    return a


SCAN_SUB = 4


def _scan_grid(l):
    rows = min(l, SCAN_SUB * CH)
    return l // rows, rows


def _sweep(d, nsub):
    return list(range(nsub)) if d == 0 else list(range(nsub - 1, -1, -1))


def _dir_specs(nsteps, rows, width, col_block):
    return [pl.BlockSpec((rows, width), lambda n: (n, col_block)),
            pl.BlockSpec((rows, width), lambda n: (nsteps - 1 - n, col_block))]


def _whole(a):
    return pl.BlockSpec(a.shape, lambda n: tuple(0 for _ in a.shape))


def _ret_consts():
    log_gamma = np.log(1.0 - 2.0 ** (-5.0 - np.arange(N_HEADS, dtype=np.float64)))
    pos = np.arange(CH, dtype=np.float64)
    diff = pos[:, None] - pos[None, :]
    dec_f = np.where(diff >= 0, np.exp(log_gamma[:, None, None] * diff), 0.0)
    dec = np.stack([dec_f, np.transpose(dec_f, (0, 2, 1))])
    qs_f = np.exp(log_gamma[None, :] * (pos[:, None] + 1.0))
    ks_f = np.exp(log_gamma[None, :] * (CH - 1.0 - pos[:, None]))
    qs_b = np.exp(log_gamma[None, :] * (CH - pos[:, None]))
    ks_b = np.exp(log_gamma[None, :] * pos[:, None])
    rep = lambda a: np.repeat(a, RET_DK, axis=1)
    qs = np.stack([rep(qs_f), rep(qs_b)])
    ks = np.stack([rep(ks_f), rep(ks_b)])
    chunk_decay = [float(np.exp(lg * CH)) for lg in log_gamma]
    return (jnp.asarray(dec, F32), jnp.asarray(qs, F32), jnp.asarray(ks, F32)), chunk_decay


def _ret_kernel(*refs, rotary, chunk_decay):
    if rotary:
        (qf, qb, kf, kb, vf, vb, cosf, cosb, sinf, sinb, dec_ref, qs_ref, ks_ref, s0_ref,
         of_ref, ob_ref, s_ref) = refs
        tabs = ((cosf, sinf), (cosb, sinb))
    else:
        qf, qb, kf, kb, vf, vb, dec_ref, qs_ref, ks_ref, s0_ref, of_ref, ob_ref, s_ref = refs
        tabs = (None, None)

    @pl.when(pl.program_id(0) == 0)
    def _():
        s_ref[...] = s0_ref[...]

    lane = lax.broadcasted_iota(jnp.int32, (CH, N_HEADS * RET_DK), 1)
    first_half = (lane & (RET_DK - 1)) < RET_DK // 2

    nsub = qf.shape[0] // CH
    chains = {}
    for d, (q_ref, k_ref, v_ref, o_ref) in enumerate(((qf, kf, vf, of_ref), (qb, kb, vb, ob_ref))):
        for j in range(nsub):
            rows = slice(j * CH, (j + 1) * CH)
            q = q_ref[rows, :].astype(F32)
            k = k_ref[rows, :].astype(F32) * RET_DK ** -0.5
            if rotary:
                cos, sin = tabs[d][0][rows, :], tabs[d][1][rows, :]

                def rot(x, cos=cos, sin=sin):
                    swapped = jnp.where(first_half, pltpu.roll(x, N_HEADS * RET_DK - RET_DK // 2, 1),
                                        pltpu.roll(x, RET_DK // 2, 1))
                    return x * cos + swapped * sin

                q, k = rot(q), rot(k)
            qb16, kb16 = q.astype(BF16), k.astype(BF16)
            q_in = (q * qs_ref[d]).astype(BF16)
            k_out = k * ks_ref[d]
            k_t = [k_out[:, :128].T, k_out[:, 128:].T]
            for h in range(N_HEADS):
                sl = slice(h * RET_DK, (h + 1) * RET_DK)
                chains[d, j, h] = dict(
                    o_ref=o_ref, rows=rows, vh=v_ref[rows, h * HEAD_V:(h + 1) * HEAD_V], q=qb16[:, sl], k=kb16[:, sl],
                    q_in=q_in[:, sl], k_th=k_t[h // 2][(h % 2) * RET_DK:(h % 2 + 1) * RET_DK, :].astype(BF16))

    for (d, j, h), c in chains.items():
        c["p"] = (_dot_nt(c["q"], c["k"]) * dec_ref[d, h]).astype(BF16)
    for c in chains.values():
        c["intra"] = _dot(c["p"], c["vh"])
        c["update"] = _dot(c["k_th"], c["vh"])
    states = {(d, h): s_ref[d, h] for d in range(2) for h in range(N_HEADS)}
    for i in range(nsub):
        for (d, h), s in list(states.items()):
            c = chains[d, _sweep(d, nsub)[i], h]
            c["o_ref"][c["rows"], h * HEAD_V:(h + 1) * HEAD_V] = c["intra"] + _dot(c["q_in"], s.astype(BF16))
            states[d, h] = chunk_decay[h] * s + c["update"]
    for (d, h), s in states.items():
        s_ref[d, h] = s


def retention_scan(z, s0, tables):
    l = z.shape[0]
    nsteps, rows = _scan_grid(l)
    consts, chunk_decay = _ret_consts()
    rotary = tables is not None
    w = N_HEADS * RET_DK
    specs = (_dir_specs(nsteps, rows, w, OFF_RET_Q // w) + _dir_specs(nsteps, rows, w, OFF_RET_K // w)
             + _dir_specs(nsteps, rows, BRANCH_W, OFF_RET_V // BRANCH_W))
    args = [z] * 6
    if rotary:
        specs += _dir_specs(nsteps, rows, w, 0) + _dir_specs(nsteps, rows, w, 0)
        args += [tables[0], tables[0], tables[1], tables[1]]
    specs += [_whole(c) for c in consts] + [_whole(s0)]
    args += list(consts) + [s0]
    return pl.pallas_call(
        functools.partial(_ret_kernel, rotary=rotary, chunk_decay=chunk_decay),
        out_shape=(jax.ShapeDtypeStruct((l, BRANCH_W), F32), jax.ShapeDtypeStruct((l, BRANCH_W), F32),
                   jax.ShapeDtypeStruct(s0.shape, F32)),
        grid=(nsteps,),
        in_specs=specs,
        out_specs=tuple(_dir_specs(nsteps, rows, BRANCH_W, 0)) + (_whole(s0),),
        compiler_params=_cparams(("arbitrary",)),
        name="retention_scan",
    )(*args)


def _mlstm_kernel(qf, qb, kf, kb, vf, vb, smf, smb, bias_ref, c0_ref, m0_ref, of_ref, ob_ref, c_ref, m_ref):
    @pl.when(pl.program_id(0) == 0)
    def _():
        c_ref[...] = c0_ref[...]
        m_ref[...] = m0_ref[...]

    lane = lax.broadcasted_iota(jnp.int32, (CH, 128), 1)
    is_forget = (lane >= F_LANE) & (lane < F_LANE + 2 * N_HEADS)
    ones_col = jnp.where(lane == 0, 1.0, 0.0).astype(BF16)

    nsub = qf.shape[0] // CH
    chains = {}
    for d, (q_ref, k_ref, v_ref, sm_ref, o_ref) in enumerate(((qf, kf, vf, smf, of_ref), (qb, kb, vb, smb, ob_ref))):
        reverse = d == 1
        incl, _ = _masks(reverse)
        last = 0 if reverse else CH - 1
        for j in range(nsub):
            rows = slice(j * CH, (j + 1) * CH)
            pre = sm_ref[rows, :] + bias_ref[0:1]
            x = jnp.where(is_forget, jax.nn.log_sigmoid(pre), pre)
            b = _cumsum_time(incl, x)
            a = pltpu.roll(x, F_LANE - I_LANE, 1) - b
            cm = _cummax_time(a, reverse)
            a_t = a.T
            q = (q_ref[rows, :].astype(F32) * MLSTM_DK ** -0.5).astype(BF16)
            kf32 = k_ref[rows, :].astype(F32)
            k_t = [kf32[:, :128].T, kf32[:, 128:].T]
            for h in range(N_HEADS):
                gf = F_LANE + d * N_HEADS + h
                sl = slice(h * MLSTM_DK, (h + 1) * MLSTM_DK)
                cm_col, a_row = cm[:, gf:gf + 1], a_t[gf:gf + 1, :]
                chains[d, j, h] = dict(
                    o_ref=o_ref, rows=rows, q=q[:, sl], k=k_ref[rows, sl], a_row=a_row, cm_col=cm_col,
                    b_col=b[:, gf:gf + 1], b_last=b[last:last + 1, gf:gf + 1], cm_last=cm[last:last + 1, gf:gf + 1],
                    w_intra=jnp.where(incl, jnp.exp(jnp.minimum(a_row - cm_col, 0.0)), 0.0),
                    k_th=k_t[h // 2][(h % 2) * MLSTM_DK:(h % 2 + 1) * MLSTM_DK, :],
                    v_aug=jnp.concatenate([v_ref[rows, h * HEAD_V:(h + 1) * HEAD_V], ones_col], axis=1))

    for c in chains.values():
        c["p"] = (c["w_intra"] * _dot_nt(c["q"], c["k"])).astype(BF16)
    for c in chains.values():
        c["pv"] = _dot(c["p"], c["v_aug"])
    c_augs = {(d, h): c_ref[d, h] for d in range(2) for h in range(N_HEADS)}
    m_ss = {(d, h): m_ref[d, h][0:1, 0:1] for d in range(2) for h in range(N_HEADS)}
    for i in range(nsub):
        cur = {(d, h): chains[d, _sweep(d, nsub)[i], h] for (d, h) in c_augs}
        qcs = {key: _dot(c["q"], c_augs[key].astype(BF16)) for key, c in cur.items()}
        tops = {key: jnp.maximum(m_ss[key], c["cm_last"]) for key, c in cur.items()}
        updates = {key: _dot((c["k_th"] * jnp.exp(c["a_row"] - tops[key])).astype(BF16), c["v_aug"])
                   for key, c in cur.items()}
        for (d, h), c in cur.items():
            m_s = m_ss[d, h]
            mx = jnp.maximum(m_s, c["cm_col"])
            tot = jnp.exp(m_s - mx) * qcs[d, h] + jnp.exp(c["cm_col"] - mx) * c["pv"]
            den = jnp.maximum(jnp.abs(tot[:, HEAD_V:HEAD_V + 1]), jnp.exp(-(c["b_col"] + mx)))
            c["o_ref"][c["rows"], h * HEAD_V:(h + 1) * HEAD_V] = tot[:, :HEAD_V] / den
        for key, c in cur.items():
            c_augs[key] = jnp.exp(m_ss[key] - tops[key]) * c_augs[key] + updates[key]
            m_ss[key] = c["b_last"] + tops[key]
    for (d, h) in c_augs:
        c_ref[d, h] = c_augs[d, h]
        m_ref[d, h] = jnp.broadcast_to(m_ss[d, h], (8, 128))


def mlstm_scan(z, small, bias, c0, m0):
    l = z.shape[0]
    nsteps, rows = _scan_grid(l)
    w = N_HEADS * MLSTM_DK
    specs = (_dir_specs(nsteps, rows, w, OFF_MLSTM_Q // w) + _dir_specs(nsteps, rows, w, OFF_MLSTM_K // w)
             + _dir_specs(nsteps, rows, BRANCH_W, OFF_MLSTM_V // BRANCH_W) + _dir_specs(nsteps, rows, 128, 0))
    specs += [_whole(bias), _whole(c0), _whole(m0)]
    return pl.pallas_call(
        _mlstm_kernel,
        out_shape=(jax.ShapeDtypeStruct((l, BRANCH_W), F32), jax.ShapeDtypeStruct((l, BRANCH_W), F32),
                   jax.ShapeDtypeStruct(c0.shape, F32), jax.ShapeDtypeStruct(m0.shape, F32)),
        grid=(nsteps,),
        in_specs=specs,
        out_specs=tuple(_dir_specs(nsteps, rows, BRANCH_W, 0)) + (_whole(c0), _whole(m0)),
        compiler_params=_cparams(("arbitrary",)),
        name="mlstm_scan",
    )(z, z, z, z, z, z, small, small, bias, c0, m0)


QKV_W = 3 * N_HEADS * GDN_DK
HALO = 8


def _gdn_prep_kernel(x_ref, prev_ref, next_ref, w_ref, q_ref, k_ref, v_ref):
    i = pl.program_id(0)
    tm = x_ref.shape[0]
    x = x_ref[...].astype(F32)
    prev = jnp.where(i > 0, prev_ref[...].astype(F32), 0.0)
    nxt = jnp.where(i < pl.num_programs(0) - 1, next_ref[...].astype(F32), 0.0)
    xe = jnp.concatenate([prev, x, nxt], axis=0)
    w = w_ref[...]
    y = None
    for tap in range(CONV_K):
        off = HALO + tap - CONV_K // 2
        term = w[tap:tap + 1] * xe[off:off + tm]
        y = term if y is None else y + term
    y = y * jax.nn.sigmoid(y)
    hw = N_HEADS * GDN_DK
    for h in range(N_HEADS):
        sl = slice(h * GDN_DK, (h + 1) * GDN_DK)
        qh = y[:, sl]
        kh = y[:, hw + h * GDN_DK:hw + (h + 1) * GDN_DK]
        q_ref[:, sl] = (qh * lax.rsqrt(jnp.sum(qh * qh, axis=-1, keepdims=True) + EPS) * GDN_DK ** -0.5).astype(BF16)
        k_ref[:, sl] = (kh * lax.rsqrt(jnp.sum(kh * kh, axis=-1, keepdims=True) + EPS)).astype(BF16)
    v_ref[...] = y[:, 2 * hw:].astype(BF16)


def gdn_prep(z, conv_w):
    l = z.shape[0]
    tm = min(l, 256)
    nb = l // tm
    r8 = tm // HALO
    cb = OFF_GDN_QKV // QKV_W
    w8 = jnp.concatenate([conv_w.astype(F32), jnp.zeros((8 - CONV_K, QKV_W), F32)], axis=0)
    return pl.pallas_call(
        _gdn_prep_kernel,
        out_shape=tuple(jax.ShapeDtypeStruct((l, BRANCH_W), BF16) for _ in range(3)),
        grid=(nb,),
        in_specs=[pl.BlockSpec((tm, QKV_W), lambda i: (i, cb)),
                  pl.BlockSpec((HALO, QKV_W), lambda i: (jnp.maximum(i * r8 - 1, 0), cb)),
                  pl.BlockSpec((HALO, QKV_W), lambda i: (jnp.minimum((i + 1) * r8, nb * r8 - 1), cb)),
                  pl.BlockSpec((8, QKV_W), lambda i: (0, 0))],
        out_specs=tuple(pl.BlockSpec((tm, BRANCH_W), lambda i: (i, 0)) for _ in range(3)),
        compiler_params=_cparams(("parallel",)),
        name="gdn_prep",
    )(z, z, z, w8)


N_LEVELS = 7


def _gdn_kernel(qf, qb, kf, kb, vf, vb, smf, smb, par_ref, s0_ref, of_ref, ob_ref, s_ref):
    @pl.when(pl.program_id(0) == 0)
    def _():
        s_ref[...] = s0_ref[...]

    lane = lax.broadcasted_iota(jnp.int32, (CH, 128), 1)
    is_decay = lane < B_LANE
    ri = lax.broadcasted_iota(jnp.int32, (CH, CH), 0)
    ci = lax.broadcasted_iota(jnp.int32, (CH, CH), 1)
    eye = jnp.where(ri == ci, 1.0, 0.0)
    pair_masks = [((ri >> (l + 1)) == (ci >> (l + 1))) & ((ri >> l) != (ci >> l)) for l in range(N_LEVELS)]

    nsub = qf.shape[0] // CH
    chains = {}
    for d, (q_ref, k_ref, v_ref, sm_ref, o_ref) in enumerate(((qf, kf, vf, smf, of_ref), (qb, kb, vb, smb, ob_ref))):
        reverse = d == 1
        incl, strict = _masks(reverse)
        last = 0 if reverse else CH - 1
        for j in range(nsub):
            rows = slice(j * CH, (j + 1) * CH)
            sm = sm_ref[rows, :]
            log_a = -jnp.exp(par_ref[1:2]) * jax.nn.softplus(sm + par_ref[0:1])
            x = jnp.where(is_decay, log_a, jax.nn.sigmoid(sm))
            g = _cumsum_time(incl, x)
            g_t = g.T
            for h in range(N_HEADS):
                ga, gb = A_LANE + d * N_HEADS + h, B_LANE + d * N_HEADS + h
                sl = slice(h * GDN_DK, (h + 1) * GDN_DK)
                g_col, g_row = g[:, ga:ga + 1], g_t[ga:ga + 1, :]
                g_last = g[last:last + 1, ga:ga + 1]
                beta = x[:, gb:gb + 1]
                decay = jnp.where(incl, jnp.exp(jnp.minimum(g_col - g_row, 0.0)), 0.0)
                kh = k_ref[rows, sl]
                kf32 = kh.astype(F32)
                kbeta = kf32 * beta
                a = jnp.where(strict, _dot_nt(kbeta.astype(BF16), kh) * decay, 0.0)
                e_g = jnp.exp(g_col)
                rhs = jnp.concatenate([v_ref[rows, sl].astype(F32) * beta, kbeta * e_g], axis=1)
                chains[d, j, h] = dict(
                    rows=rows, sl=sl, o_ref=o_ref, a=a, rhs=rhs, s_decay=jnp.exp(g_last),
                    qk=(_dot_nt(q_ref[rows, sl], kh) * decay).astype(BF16),
                    q_in=(q_ref[rows, sl].astype(F32) * e_g).astype(BF16),
                    k_out_t=(kf32.T * jnp.exp(g_last - g_row)).astype(BF16))

    cl = list(chains.values())
    xs = [eye - jnp.where(pair_masks[0], c["a"], 0.0) for c in cl]
    for pm in pair_masks[1:]:
        ys = [_dot(jnp.where(pm, c["a"], 0.0).astype(BF16), x.astype(BF16)) for c, x in zip(cl, xs)]
        xs = [x - _dot(x.astype(BF16), y.astype(BF16)) for x, y in zip(xs, ys)]
    for c, x in zip(cl, xs):
        c["sol"] = c["rhs"] + _dot((x - eye).astype(BF16), c["rhs"].astype(BF16))
    states = {(d, h): s_ref[d, h] for d in range(2) for h in range(N_HEADS)}
    for i in range(nsub):
        cur = {(d, h): chains[d, _sweep(d, nsub)[i], h] for (d, h) in states}
        s16s = {key: s.astype(BF16) for key, s in states.items()}
        v_news = {key: (c["sol"][:, :HEAD_V] - _dot(c["sol"][:, HEAD_V:].astype(BF16), s16s[key])).astype(BF16)
                  for key, c in cur.items()}
        updates = {key: _dot(c["k_out_t"], v_news[key]) for key, c in cur.items()}
        for key, c in cur.items():
            c["o_ref"][c["rows"], c["sl"]] = _dot(c["q_in"], s16s[key]) + _dot(c["qk"], v_news[key])
        for key, c in cur.items():
            states[key] = states[key] * c["s_decay"] + updates[key]
    for (d, h), s in states.items():
        s_ref[d, h] = s


def gdn_scan(qn, kn, vn, small, par, s0):
    l = qn.shape[0]
    nsteps, rows = _scan_grid(l)
    specs = _dir_specs(nsteps, rows, BRANCH_W, 0) * 3 + _dir_specs(nsteps, rows, 128, 0)
    specs += [_whole(par), _whole(s0)]
    return pl.pallas_call(
        _gdn_kernel,
        out_shape=(jax.ShapeDtypeStruct((l, BRANCH_W), F32), jax.ShapeDtypeStruct((l, BRANCH_W), F32),
                   jax.ShapeDtypeStruct(s0.shape, F32)),
        grid=(nsteps,),
        in_specs=specs,
        out_specs=tuple(_dir_specs(nsteps, rows, BRANCH_W, 0)) + (_whole(s0),),
        compiler_params=_cparams(("arbitrary",)),
        name="gdn_scan",
    )(qn, qn, kn, kn, vn, vn, small, small, par, s0)


def _lane_rows(entries):
    r = jnp.zeros((8, 128), F32)
    for row, lane, vals in entries:
        r = r.at[row, lane:lane + vals.shape[0]].set(vals.astype(F32))
    return r


def gdn_branch(zc, sc, zl, sl, conv_w, a_log, dt_bias):
    par = _lane_rows([(0, A_LANE, dt_bias.reshape(-1)), (1, A_LANE, a_log.reshape(-1))])
    s0 = jnp.zeros((2, N_HEADS, GDN_DK, HEAD_V), F32)
    ofc, obc, s1 = gdn_scan(*gdn_prep(zc, conv_w), sc, par, s0)
    ofl, obl, _ = gdn_scan(*gdn_prep(zl, conv_w), sl, par, s1)
    return (ofc, obc), (ofl, obl)


def mlstm_branch(zc, sc, zl, sl, gate_b):
    bias = _lane_rows([(0, I_LANE, gate_b[0].reshape(-1)), (0, F_LANE, gate_b[1].reshape(-1))])
    c0 = jnp.zeros((2, N_HEADS, MLSTM_DK, 2 * HEAD_V), F32)
    m0 = jnp.zeros((2, N_HEADS, 8, 128), F32)
    ofc, obc, c1, m1 = mlstm_scan(zc, sc, bias, c0, m0)
    ofl, obl, _, _ = mlstm_scan(zl, sl, bias, c1, m1)
    return (ofc, obc), (ofl, obl)


def retention_branch(zc, zl, tables):
    s0 = jnp.zeros((2, N_HEADS, RET_DK, HEAD_V), F32)
    ofc, obc, s1 = retention_scan(zc, s0, None)
    ofl, obl, _ = retention_scan(zl, s1, tables)
    return (ofc, obc), (ofl, obl)


def _ret_rope_tables(n_lat):
    inv = np.float32(ROPE_BASE) ** (-np.arange(0, RET_DK, 2, dtype=np.float32) / np.float32(RET_DK))
    ang = np.arange(n_lat, dtype=np.float32)[:, None] * inv[None, :]
    cos = np.concatenate([np.cos(ang), np.cos(ang)], axis=-1)
    sin = np.concatenate([-np.sin(ang), np.sin(ang)], axis=-1)
    return jnp.asarray(np.tile(cos, (1, N_HEADS)), F32), jnp.asarray(np.tile(sin, (1, N_HEADS)), F32)


def _rope_tables(n_lat):
    q = MLA_ROPE // 4
    inv = np.float32(ROPE_BASE) ** (-np.arange(0, 2 * q, 2, dtype=np.float32) / np.float32(2 * q))
    t = np.arange(n_lat)
    row = (t // GRID_W).astype(np.float32)[:, None] * inv[None, :]
    col = (t % GRID_W).astype(np.float32)[:, None] * inv[None, :]
    cos = np.concatenate([np.cos(row), np.cos(row), np.cos(col), np.cos(col)], axis=-1)
    sin = np.concatenate([-np.sin(row), np.sin(row), -np.sin(col), np.sin(col)], axis=-1)
    return jnp.asarray(cos, F32), jnp.asarray(sin, F32)


def _take_cols(w, perm):
    runs, start = [], 0
    for i in range(1, len(perm) + 1):
        if i == len(perm) or perm[i] != perm[i - 1] + 1:
            runs.append((int(perm[start]), int(perm[i - 1]) + 1))
            start = i
    return jnp.concatenate([w[:, a:b].astype(BF16) for a, b in runs], axis=1)


def _pad_rows(v, n=8):
    rows = [jnp.reshape(r, (1, -1)).astype(F32) for r in v]
    d = rows[0].shape[1]
    return jnp.concatenate(rows + [jnp.zeros((n - len(rows), d), F32)], axis=0)


def _mla_weights(w_uq, w_ukv):
    wq = w_uq.reshape(MLA_Q_RANK, N_HEADS, MLA_QK)
    rope = wq[:, :, MLA_NOPE:]
    swapped = rope[:, :, _rope_swap(np.arange(MLA_ROPE))]
    wq_ext = jnp.concatenate([wq[:, :, :MLA_NOPE], rope, swapped], axis=-1).reshape(MLA_Q_RANK, N_HEADS * 256)
    return wq_ext.astype(BF16), w_ukv.astype(BF16)


def _moe(x_l, pv2, g2, w_router, b_router, w_gu, w_down, final_g):
    n = x_l.shape[0]
    w_pad = jnp.concatenate([w_router, jnp.zeros((D_MODEL, 128 - N_EXPERTS), F32)], axis=1)
    b_pad = jnp.concatenate([b_router, jnp.full((128 - N_EXPERTS,), -1e30, F32)]).reshape(1, 128)
    h2, route = moe_router(x_l, pv2, w_pad, b_pad)
    n_assign = n * TOP_K
    experts = jnp.arange(N_EXPERTS)[None, :]
    e0, e1 = route[:, TOP_K].astype(jnp.int32), route[:, TOP_K + 1].astype(jnp.int32)
    hot0, hot1 = (e0[:, None] == experts).astype(jnp.int32), (e1[:, None] == experts).astype(jnp.int32)
    c0, c1 = jnp.cumsum(hot0, axis=0), jnp.cumsum(hot1, axis=0)
    before1 = c1 - hot1
    counts = c0[-1] + c1[-1]
    rank0 = jnp.sum(hot0 * (c0 - hot0 + before1), axis=1)
    rank1 = jnp.sum(hot1 * (c0 + before1), axis=1)
    padded = (counts + MOE_ROWS - 1) // MOE_ROWS * MOE_ROWS
    pad_end = jnp.cumsum(padded)
    pad_start = pad_end - padded
    dest = jnp.concatenate([pad_start[e0] + rank0, pad_start[e1] + rank1])
    nb = n_assign // MOE_ROWS + N_EXPERTS
    cap = nb * MOE_ROWS
    block_start = jnp.arange(nb, dtype=pad_end.dtype) * MOE_ROWS
    block_e = jnp.minimum(jnp.sum(block_start[:, None] >= pad_end[None, :], axis=1), N_EXPERTS - 1).astype(jnp.int32)
    n_used = (pad_end[-1] // MOE_ROWS).astype(jnp.int32).reshape(1)
    slot_tok = (jnp.arange(cap, dtype=jnp.int32) % n).at[dest].set(jnp.arange(n_assign, dtype=jnp.int32) % n)
    yb = expert_ffn(block_e, n_used, sc_gather(h2, slot_tok), w_gu, w_down)
    y2 = sc_gather(yb, dest.astype(jnp.int32))
    return final_combine(x_l, y2, route, _pad_rows([g2, final_g]))


def kernel(x, c, ctx, c_ctx, w_mod, b_mod, norm1_g, norm2_g, w_in, gdn_conv_w, gdn_a_log, gdn_dt_bias, gdn_norm_g, mla_q_norm_g, mla_kv_norm_g, mla_w_uq, mla_w_ukv, mlstm_gate_b, mlstm_norm_g, ret_norm_g, w_branch, w_out, ffn_w_in, ffn_w_down, moe_w_router, moe_b_router, moe_w_in, moe_w_down, final_norm_g):
    n_lat = x.shape[1]
    n_ctx = ctx.shape[1]
    x_l, x_c = x[0], ctx[0]
    cond = _pad_rows([c_ctx, c[0]])
    cos_l, sin_l = _rope_tables(n_lat)
    cos_c, sin_c = jnp.ones((n_ctx, MLA_ROPE), F32), jnp.zeros((n_ctx, MLA_ROPE), F32)
    ret_tables = _ret_rope_tables(n_lat)
    out = None
    for li in range(DEPTH):
        last = li == DEPTH - 1
        mod = modulation_vectors(cond, w_mod[li], b_mod[li])
        csh1, csc1, cg1, csh2, csc2, cg2 = jnp.split(mod[0], 6)
        sh1, sc1, g1, sh2, sc2, g2 = jnp.split(mod[1], 6)
        w_main = _take_cols(w_in[li], _MAIN_PERM)
        w_small = _take_cols(w_in[li], _SMALL_PERM)
        w_small = jnp.concatenate([w_small, jnp.zeros((D_MODEL, N_SMALL - w_small.shape[1]), BF16)], axis=1)
        pv_l = _pad_rows([norm1_g[li], 1 + sc1, sh1])
        pv_c = _pad_rows([norm1_g[li], 1 + csc1, csh1])
        zl, sl = norm_proj(x_l, pv_l, w_main, w_small, 3328)
        zc, sc = norm_proj(x_c, pv_c, w_main, w_small, 3328)

        a_c, a_l = gdn_branch(zc, sc, zl, sl, gdn_conv_w[li], gdn_a_log[li], gdn_dt_bias[li])
        c_c, c_l = mlstm_branch(zc, sc, zl, sl, mlstm_gate_b[li])
        d_c, d_l = retention_branch(zc, zl, ret_tables)
        gains = _pad_rows([gdn_norm_g[li], mlstm_norm_g[li], ret_norm_g[li]])

        wq_ext, wkv = _mla_weights(mla_w_uq[li], mla_w_ukv[li])
        gq, gkv = mla_q_norm_g[li].reshape(1, -1), mla_kv_norm_g[li].reshape(1, -1)
        ql, kl, vl = mla_project(zl, cos_l, sin_l, gq, gkv, wq_ext, wkv)
        qc, kc, vc = mla_project(zc, cos_c, sin_c, gq, gkv, wq_ext, wkv)
        b_l = attention(ql, [(kl, vl), (kc, vc)])

        wb = w_branch[li].astype(BF16)
        wo = w_out[li].astype(BF16)
        x_l = merge_branches(x_l, zl, a_l, b_l, c_l, d_l, gains, wb, wo, _pad_rows([g1]))
        if not last:
            b_c = attention(qc, [(kc, vc)])
            x_c = merge_branches(x_c, zc, a_c, b_c, c_c, d_c, gains, wb, wo, _pad_rows([cg1]))

        if li % 2 == 0:
            w_gu = ffn_w_in[li // 2].astype(BF16)
            w_dn = ffn_w_down[li // 2].astype(BF16)
            assert not last
            x_l = dense_ffn(x_l, _pad_rows([norm2_g[li], 1 + sc2, sh2, g2]), w_gu, w_dn)
            x_c = dense_ffn(x_c, _pad_rows([norm2_g[li], 1 + csc2, csh2, cg2]), w_gu, w_dn)
        else:
            assert last
            out = _moe(x_l, _pad_rows([norm2_g[li], 1 + sc2, sh2]), g2, moe_w_router[li // 2],
                       moe_b_router[li // 2], moe_w_in[li // 2], moe_w_down[li // 2], final_norm_g)
    return out[None]
```

```python
import functools
import math

import numpy as np
import jax
import jax.numpy as jnp
from jax import lax
from jax.experimental import pallas as pl
from jax.experimental.pallas import tpu as pltpu
from jax.experimental.pallas import tpu_sc as plsc

F32 = jnp.float32
BF16 = jnp.bfloat16

D_MODEL = 1024
DEPTH = 2
GRID_W = 64
N_BRANCH = 4
N_HEADS = 4
HEAD_V = 128
BRANCH_W = N_HEADS * HEAD_V
GDN_DK = 128
CONV_K = 5
MLA_Q_RANK = 384
MLA_KV_RANK = 256
MLA_NOPE = 128
MLA_ROPE = 64
MLA_QK = MLA_NOPE + MLA_ROPE
MLA_QK_PAD = 256
MLA_V_PAD = 256
MLSTM_DK = 64
RET_DK = 64
ROPE_BASE = 10000.0
D_FF = 3584
N_EXPERTS = 8
TOP_K = 2
EPS = 1e-6

IN_WIDTHS = (
    N_BRANCH * D_MODEL,
    N_HEADS * GDN_DK, N_HEADS * GDN_DK, BRANCH_W, BRANCH_W, 2 * N_HEADS, 2 * N_HEADS,
    MLA_Q_RANK, MLA_KV_RANK, MLA_ROPE,
    N_HEADS * MLSTM_DK, N_HEADS * MLSTM_DK, BRANCH_W, BRANCH_W, 2 * N_HEADS, 2 * N_HEADS,
    N_HEADS * RET_DK, N_HEADS * RET_DK, BRANCH_W, BRANCH_W,
)
_IN_OFF = [0] + [int(o) for o in np.cumsum(IN_WIDTHS)]

VMEM_LIMIT = 48 * 1024 * 1024
MERGE_VMEM_LIMIT = 56 * 1024 * 1024
MOE_ROWS = 1024


def _cols(group):
    return np.arange(_IN_OFF[group], _IN_OFF[group + 1])


def _rope_swap(cols):
    q = MLA_ROPE // 4
    return np.concatenate([cols[q:2 * q], cols[:q], cols[3 * q:], cols[2 * q:3 * q]])


_MAIN_PERM = np.concatenate([
    _cols(1), _cols(2), _cols(3),
    _cols(4),
    _cols(0),
    _cols(12), _cols(13),
    _cols(10), _cols(11),
    _cols(18), _cols(19),
    _cols(16), _cols(17),
    _cols(7), _cols(8), _cols(9), _rope_swap(_cols(9)),
])
N_MAIN = int(_MAIN_PERM.shape[0])
_SMALL_PERM = np.concatenate([_cols(5), _cols(6), _cols(14), _cols(15)])
N_SMALL = 128
A_LANE, B_LANE, I_LANE, F_LANE = 0, 8, 16, 24
OFF_GDN_QKV, OFF_GDN_Z, OFF_GATE = 0, 1536, 2048
OFF_MLSTM_V, OFF_MLSTM_O, OFF_MLSTM_Q, OFF_MLSTM_K = 6144, 6656, 7168, 7424
OFF_RET_V, OFF_RET_G, OFF_RET_Q, OFF_RET_K = 7680, 8192, 8704, 8960
OFF_MLA = 9216
MLA_IN_W = 768


def _cparams(sem):
    return pltpu.CompilerParams(dimension_semantics=sem, vmem_limit_bytes=VMEM_LIMIT)


def _rms(x):
    return x * lax.rsqrt(jnp.mean(x * x, axis=-1, keepdims=True) + EPS)


def _dot(a, b):
    return jnp.dot(a, b, preferred_element_type=F32)


def _dot_nt(a, b):
    return lax.dot_general(a, b, (((1,), (1,)), ((), ())), preferred_element_type=F32)


def _mod_kernel(c_ref, w_ref, b_ref, o_ref):
    c = c_ref[...]
    s = c * jax.nn.sigmoid(c)
    o_ref[...] = jnp.dot(s, w_ref[...], preferred_element_type=F32) + b_ref[...]


def modulation_vectors(cond, w_mod, b_mod):
    n = w_mod.shape[1]
    tn = 1536
    return pl.pallas_call(
        _mod_kernel,
        out_shape=jax.ShapeDtypeStruct((8, n), F32),
        grid=(n // tn,),
        in_specs=[pl.BlockSpec((8, D_MODEL), lambda j: (0, 0)),
                  pl.BlockSpec((D_MODEL, tn), lambda j: (0, j)),
                  pl.BlockSpec((1, tn), lambda j: (0, j))],
        out_specs=pl.BlockSpec((8, tn), lambda j: (0, j)),
        compiler_params=_cparams(("arbitrary",)),
        name="modulation",
    )(cond, w_mod, b_mod.reshape(1, n))


def _norm_proj_kernel(x_ref, pv_ref, w_ref, ws_ref, o_ref, os_ref, h_ref):
    @pl.when(pl.program_id(1) == 0)
    def _():
        pv = pv_ref[...]
        h = (_rms(x_ref[...]) * pv[0:1] * pv[1:2] + pv[2:3]).astype(BF16)
        h_ref[...] = h
        os_ref[...] = _dot(h, ws_ref[...])

    o_ref[...] = _dot(h_ref[...], w_ref[...]).astype(o_ref.dtype)


def norm_proj(x, pv, w, w_small, tn):
    m, d = x.shape
    n = w.shape[1]
    tm = min(m, 1024)
    return pl.pallas_call(
        _norm_proj_kernel,
        out_shape=(jax.ShapeDtypeStruct((m, n), BF16), jax.ShapeDtypeStruct((m, N_SMALL), F32)),
        grid=(m // tm, n // tn),
        in_specs=[pl.BlockSpec((tm, d), lambda i, j: (i, 0)),
                  pl.BlockSpec((8, d), lambda i, j: (0, 0)),
                  pl.BlockSpec((d, tn), lambda i, j: (0, j)),
                  pl.BlockSpec((d, N_SMALL), lambda i, j: (0, 0))],
        out_specs=(pl.BlockSpec((tm, tn), lambda i, j: (i, j)), pl.BlockSpec((tm, N_SMALL), lambda i, j: (i, 0))),
        scratch_shapes=[pltpu.VMEM((tm, d), BF16)],
        compiler_params=pltpu.CompilerParams(dimension_semantics=("parallel", "arbitrary"),
                                             vmem_limit_bytes=MERGE_VMEM_LIMIT),
        name="norm_proj",
    )(x, pv, w, w_small)


def _mla_proj_kernel(z_ref, cos_ref, sin_ref, gq_ref, gkv_ref, wq_ref, wkv_ref, q_ref, k_ref, v_ref):
    z = z_ref[...].astype(F32)
    tm = z.shape[0]
    cq = z[:, :MLA_Q_RANK]
    ckv = z[:, MLA_Q_RANK:MLA_Q_RANK + MLA_KV_RANK]
    kr = z[:, MLA_Q_RANK + MLA_KV_RANK:]
    cos = cos_ref[...]
    sin = sin_ref[...]
    qn = (_rms(cq) * gq_ref[...]).astype(BF16)
    kvn = (_rms(ckv) * gkv_ref[...]).astype(BF16)
    qf = _dot(qn, wq_ref[...]) * (MLA_QK ** -0.5 * math.log2(math.e))
    kvf = _dot(kvn, wkv_ref[...])
    kr_rot = kr[:, :MLA_ROPE] * cos + kr[:, MLA_ROPE:] * sin
    pad = jnp.zeros((tm, MLA_QK_PAD - MLA_QK), F32)
    lane = lax.broadcasted_iota(jnp.int32, (tm, MLA_V_PAD - HEAD_V), 1)
    ones_col = jnp.where(lane == 0, 1.0, 0.0).astype(BF16)
    for h in range(N_HEADS):
        b = h * 256
        q_rot = qf[:, b + 128:b + 192] * cos + qf[:, b + 192:b + 256] * sin
        q_ref[h] = jnp.concatenate([qf[:, b:b + 128], q_rot, pad], axis=-1).astype(BF16)
        k_ref[h] = jnp.concatenate([kvf[:, b:b + 128], kr_rot, pad], axis=-1).astype(BF16)
        v_ref[h] = jnp.concatenate([kvf[:, b + 128:b + 256].astype(BF16), ones_col], axis=-1)


def mla_project(zmain, cos, sin, gq, gkv, wq_ext, wkv):
    m = zmain.shape[0]
    tm = min(m, 1024)
    full = lambda shape: pl.BlockSpec(shape, lambda i: tuple(0 for _ in shape))
    return pl.pallas_call(
        _mla_proj_kernel,
        out_shape=(jax.ShapeDtypeStruct((N_HEADS, m, MLA_QK_PAD), BF16),
                   jax.ShapeDtypeStruct((N_HEADS, m, MLA_QK_PAD), BF16),
                   jax.ShapeDtypeStruct((N_HEADS, m, MLA_V_PAD), BF16)),
        grid=(m // tm,),
        in_specs=[pl.BlockSpec((tm, MLA_IN_W), lambda i: (i, OFF_MLA // MLA_IN_W)),
                  pl.BlockSpec((tm, MLA_ROPE), lambda i: (i, 0)),
                  pl.BlockSpec((tm, MLA_ROPE), lambda i: (i, 0)),
                  full((1, MLA_Q_RANK)), full((1, MLA_KV_RANK)),
                  full((MLA_Q_RANK, N_HEADS * 256)), full((MLA_KV_RANK, N_HEADS * 256))],
        out_specs=(pl.BlockSpec((N_HEADS, tm, MLA_QK_PAD), lambda i: (0, i, 0)),
                   pl.BlockSpec((N_HEADS, tm, MLA_QK_PAD), lambda i: (0, i, 0)),
                   pl.BlockSpec((N_HEADS, tm, MLA_V_PAD), lambda i: (0, i, 0))),
        compiler_params=_cparams(("parallel",)),
        name="mla_project",
    )(zmain, cos, sin, gq, gkv, wq_ext, wkv)


ATTN_TQ, ATTN_TK = 512, 512
ATTN_UNROLL = 32


def _attn_kernel(*refs, segs):
    q_ref, o_ref = refs[0], refs[-1]
    q = q_ref[0]
    tq = q.shape[0]
    carry = (jnp.full((tq, 1), -1e30, F32), jnp.zeros((tq, MLA_V_PAD), F32))
    for si, (tk, nk) in enumerate(segs):
        k_ref, v_ref = refs[1 + 2 * si], refs[2 + 2 * si]

        def body(c, carry, k_ref=k_ref, v_ref=v_ref, tk=tk):
            m, acc = carry
            start = pl.multiple_of(c * tk, tk)
            s = _dot_nt(q, k_ref[0, pl.ds(start, tk), :])
            m_new = jnp.maximum(m, jnp.max(s, axis=-1, keepdims=True))
            p = jnp.exp2(s - m_new).astype(BF16)
            acc = jnp.exp2(m - m_new) * acc + _dot(p, v_ref[0, pl.ds(start, tk), :])
            return m_new, acc

        carry = lax.fori_loop(0, nk, body, carry, unroll=min(ATTN_UNROLL, nk))
    _, acc = carry
    o_ref[...] = (acc[:, :HEAD_V] / acc[:, HEAD_V:HEAD_V + 1]).astype(o_ref.dtype)


def attention(q, kvs):
    _, lq, _ = q.shape
    tq = min(lq, ATTN_TQ)
    segs, args, specs = [], [], []
    for k, v in kvs:
        lk = k.shape[1]
        tk = min(lk, ATTN_TK)
        segs.append((tk, lk // tk))
        args += [k, v]
        specs += [pl.BlockSpec((1, lk, MLA_QK_PAD), lambda h, i: (h, 0, 0)),
                  pl.BlockSpec((1, lk, MLA_V_PAD), lambda h, i: (h, 0, 0))]
    return pl.pallas_call(
        functools.partial(_attn_kernel, segs=tuple(segs)),
        out_shape=jax.ShapeDtypeStruct((lq, N_HEADS * HEAD_V), BF16),
        grid=(N_HEADS, lq // tq),
        in_specs=[pl.BlockSpec((1, tq, MLA_QK_PAD), lambda h, i: (h, i, 0))] + specs,
        out_specs=pl.BlockSpec((tq, HEAD_V), lambda h, i: (i, h)),
        compiler_params=_cparams(("parallel", "arbitrary")),
        name="attention",
    )(q, *args)


def _head_post(o, gate, gain, centre, silu_gate):
    sig = jax.nn.sigmoid(gate)
    act = gate * sig if silu_gate else sig
    outs = []
    for h in range(N_HEADS):
        sl = slice(h * HEAD_V, (h + 1) * HEAD_V)
        oh = o[:, sl]
        if centre:
            oh = oh - jnp.mean(oh, axis=-1, keepdims=True)
        oh = oh * lax.rsqrt(jnp.mean(oh * oh, axis=-1, keepdims=True) + EPS)
        outs.append((oh * gain[:, sl] * act[:, sl]).astype(BF16))
    return jnp.concatenate(outs, axis=1)


def _merge_kernel(x_ref, g0_ref, g1_ref, g2_ref, g3_ref, af_ref, ab_ref, b_ref, cf_ref, cb_ref, df_ref, db_ref,
                  za_ref, zc_ref, zd_ref, gain_ref, wb_ref, wo_ref, g_ref, o_ref):
    gain = gain_ref[...]
    branches = (
        _head_post(af_ref[...] + ab_ref[...], za_ref[...].astype(F32), gain[0:1], False, True),
        b_ref[...],
        _head_post(cf_ref[...] + cb_ref[...], zc_ref[...].astype(F32), gain[1:2], False, False),
        _head_post(df_ref[...] + db_ref[...], zd_ref[...].astype(F32), gain[2:3], True, True),
    )
    s = None
    for n, (br, gate_ref) in enumerate(zip(branches, (g0_ref, g1_ref, g2_ref, g3_ref))):
        proj = _dot(br, wb_ref[n])
        gate = jax.nn.sigmoid(gate_ref[...].astype(F32))
        s = gate * proj if s is None else s + gate * proj
    m = _dot(s.astype(BF16), wo_ref[...])
    o_ref[...] = x_ref[...] + g_ref[0:1] * m


def merge_branches(x, zmain, gdn, attn, mlstm, ret, gains, w_branch, w_out, gvec):
    m = x.shape[0]
    tm = min(m, 512)
    row = lambda w: pl.BlockSpec((tm, w), lambda i: (i, 0))
    zcol = lambda off: pl.BlockSpec((tm, BRANCH_W), lambda i: (i, off // BRANCH_W))
    return pl.pallas_call(
        _merge_kernel,
        out_shape=jax.ShapeDtypeStruct((m, D_MODEL), F32),
        grid=(m // tm,),
        in_specs=[row(D_MODEL)]
                 + [pl.BlockSpec((tm, D_MODEL), lambda i, n=n: (i, OFF_GATE // D_MODEL + n)) for n in range(N_BRANCH)]
                 + [row(BRANCH_W)] * 7
                 + [zcol(OFF_GDN_Z), zcol(OFF_MLSTM_O), zcol(OFF_RET_G),
                    pl.BlockSpec((8, BRANCH_W), lambda i: (0, 0)),
                    pl.BlockSpec((N_BRANCH, BRANCH_W, D_MODEL), lambda i: (0, 0, 0)),
                    pl.BlockSpec((D_MODEL, D_MODEL), lambda i: (0, 0)),
                    pl.BlockSpec((8, D_MODEL), lambda i: (0, 0))],
        out_specs=row(D_MODEL),
        compiler_params=pltpu.CompilerParams(dimension_semantics=("parallel",), vmem_limit_bytes=MERGE_VMEM_LIMIT),
        name="merge_branches",
    )(x, zmain, zmain, zmain, zmain, gdn[0], gdn[1], attn, mlstm[0], mlstm[1], ret[0], ret[1],
      zmain, zmain, zmain, gains, w_branch, w_out, gvec)


def _ffn_kernel(x_ref, pv_ref, wg_ref, wu_ref, wd_ref, o_ref, h_ref):
    f = pl.program_id(1)

    def partial_out(h):
        g = _dot(h, wg_ref[...])
        u = _dot(h, wu_ref[...])
        act = (g * jax.nn.sigmoid(g) * u).astype(BF16)
        return pv_ref[3:4] * _dot(act, wd_ref[...])

    @pl.when(f == 0)
    def _():
        pv = pv_ref[...]
        h = (_rms(x_ref[...]) * pv[0:1] * pv[1:2] + pv[2:3]).astype(BF16)
        h_ref[...] = h
        o_ref[...] = x_ref[...] + partial_out(h)

    @pl.when(f > 0)
    def _():
        o_ref[...] += partial_out(h_ref[...])


def dense_ffn(x, pv, w_gu, w_down):
    m = x.shape[0]
    tm = min(m, 1024)
    tf = 512
    nf = D_FF // tf
    return pl.pallas_call(
        _ffn_kernel,
        out_shape=jax.ShapeDtypeStruct((m, D_MODEL), F32),
        grid=(m // tm, nf),
        in_specs=[pl.BlockSpec((tm, D_MODEL), lambda i, f: (i, 0)),
                  pl.BlockSpec((8, D_MODEL), lambda i, f: (0, 0)),
                  pl.BlockSpec((D_MODEL, tf), lambda i, f: (0, f)),
                  pl.BlockSpec((D_MODEL, tf), lambda i, f: (0, f + nf)),
                  pl.BlockSpec((tf, D_MODEL), lambda i, f: (f, 0))],
        out_specs=pl.BlockSpec((tm, D_MODEL), lambda i, f: (i, 0)),
        scratch_shapes=[pltpu.VMEM((tm, D_MODEL), BF16)],
        compiler_params=_cparams(("parallel", "arbitrary")),
        name="dense_ffn",
    )(x, pv, w_gu, w_gu, w_down)


HALF_D = D_MODEL // 2


def _pack_bf16_pairs(h):
    def rne_bits(v):
        u = pltpu.bitcast(v, jnp.uint32)
        return (u + jnp.uint32(0x7FFF) + ((u >> 16) & jnp.uint32(1))) & jnp.uint32(0xFFFF0000)

    return pltpu.bitcast(rne_bits(h[:, HALF_D:]) | (rne_bits(h[:, :HALF_D]) >> 16), F32)


def _unpack_bf16_pairs(p):
    u = pltpu.bitcast(p, jnp.uint32)
    lo = pltpu.bitcast(u << 16, F32).astype(BF16)
    hi = pltpu.bitcast(u & jnp.uint32(0xFFFF0000), F32).astype(BF16)
    return jnp.concatenate([lo, hi], axis=1)


def _router_kernel(x_ref, pv_ref, w_ref, b_ref, h_ref, route_ref):
    pv = pv_ref[...]
    h = _rms(x_ref[...]) * pv[0:1] * pv[1:2] + pv[2:3]
    h_hi = h.astype(BF16)
    h_lo = (h - h_hi.astype(F32)).astype(BF16)
    w = w_ref[...]
    w_hi = w.astype(BF16)
    w_lo = (w - w_hi.astype(F32)).astype(BF16)
    logits = _dot(h_hi, w_hi) + _dot(h_hi, w_lo) + _dot(h_lo, w_hi) + b_ref[...]
    h_ref[...] = _pack_bf16_pairs(h)
    lane = lax.broadcasted_iota(jnp.int32, logits.shape, 1).astype(F32)
    m1 = jnp.max(logits, axis=1, keepdims=True)
    e1 = jnp.min(jnp.where(logits == m1, lane, 128.0), axis=1, keepdims=True)
    rest = jnp.where(lane == e1, -jnp.inf, logits)
    m2 = jnp.max(rest, axis=1, keepdims=True)
    e2 = jnp.min(jnp.where(rest == m2, lane, 128.0), axis=1, keepdims=True)
    z2 = jnp.exp(m2 - m1)
    w1 = 1.0 / (1.0 + z2)
    route_ref[...] = jnp.where(lane == 0.0, w1, jnp.where(lane == 1.0, z2 * w1, jnp.where(
        lane == 2.0, e1, jnp.where(lane == 3.0, e2, 0.0))))


def moe_router(x, pv, w_router_pad, b_router_pad):
    m = x.shape[0]
    tm = min(m, 1024)
    return pl.pallas_call(
        _router_kernel,
        out_shape=(jax.ShapeDtypeStruct((m, HALF_D), F32), jax.ShapeDtypeStruct((m, 128), F32)),
        grid=(m // tm,),
        in_specs=[pl.BlockSpec((tm, D_MODEL), lambda i: (i, 0)),
                  pl.BlockSpec((8, D_MODEL), lambda i: (0, 0)),
                  pl.BlockSpec((D_MODEL, 128), lambda i: (0, 0)),
                  pl.BlockSpec((1, 128), lambda i: (0, 0))],
        out_specs=(pl.BlockSpec((tm, HALF_D), lambda i: (i, 0)), pl.BlockSpec((tm, 128), lambda i: (i, 0))),
        compiler_params=_cparams(("parallel",)),
        name="moe_router",
    )(x, pv, w_router_pad, b_router_pad)


SC_CORES, SC_SUBCORES = 2, 16
SC_ROWS = 32


def sc_gather(table, idx):
    b, d = idx.shape[0], table.shape[1]
    nw = SC_CORES * SC_SUBCORES
    assert b % (nw * 2 * SC_ROWS) == 0
    per_w = b // nw
    npairs = per_w // (2 * SC_ROWS)
    mesh = plsc.VectorSubcoreMesh(core_axis_name="c", subcore_axis_name="s")

    @functools.partial(
        pl.kernel, mesh=mesh, out_type=jax.ShapeDtypeStruct((b, d), table.dtype),
        scratch_types=[pltpu.VMEM((SC_ROWS,), jnp.int32), pltpu.VMEM((SC_ROWS,), jnp.int32),
                       pltpu.VMEM((SC_ROWS, d), table.dtype), pltpu.VMEM((SC_ROWS, d), table.dtype),
                       pltpu.SemaphoreType.DMA, pltpu.SemaphoreType.DMA])
    def gather_rows(table_hbm, idx_hbm, out_hbm, idx0, idx1, rows0, rows1, sem0, sem1):
        start = (lax.axis_index("s") * SC_CORES + lax.axis_index("c")) * per_w

        def row_copy(idx_v, rows_v, sem):
            return pltpu.make_async_copy(table_hbm.at[idx_v], rows_v, sem)

        def fetch(chunk, idx_v, rows_v, sem):
            pltpu.sync_copy(idx_hbm.at[pl.ds(start + chunk * SC_ROWS, SC_ROWS)], idx_v)
            row_copy(idx_v, rows_v, sem).start()

        def write(chunk, rows_v):
            pltpu.sync_copy(rows_v, out_hbm.at[pl.ds(start + chunk * SC_ROWS, SC_ROWS)])

        fetch(0, idx0, rows0, sem0)

        def body(j, carry):
            fetch(2 * j + 1, idx1, rows1, sem1)
            row_copy(idx0, rows0, sem0).wait()
            write(2 * j, rows0)

            @pl.when(j + 1 < npairs)
            def _():
                fetch(2 * j + 2, idx0, rows0, sem0)

            row_copy(idx1, rows1, sem1).wait()
            write(2 * j + 1, rows1)
            return carry

        lax.fori_loop(0, npairs, body, 0)

    return gather_rows(table, idx)


def _expert_kernel(be_ref, nb_ref, x_ref, wg_ref, wu_ref, wd_ref, o_ref, x16_ref):
    b = pl.program_id(0)
    f = pl.program_id(1)

    live = b < nb_ref[0]

    def partial_out(x):
        g = _dot(x, wg_ref[0].astype(BF16))
        u = _dot(x, wu_ref[0].astype(BF16))
        act = (g * jax.nn.sigmoid(g) * u).astype(BF16)
        return _dot(act, wd_ref[0].astype(BF16))

    @pl.when(live & (f == 0))
    def _():
        x = _unpack_bf16_pairs(x_ref[...])
        x16_ref[...] = x
        o_ref[...] = partial_out(x)

    @pl.when(live & (f > 0))
    def _():
        o_ref[...] += partial_out(x16_ref[...])

    @pl.when(jnp.logical_not(live) & (f == 0))
    def _():
        o_ref[...] = jnp.zeros_like(o_ref)


def expert_ffn(block_e, n_used, xb, w_gu, w_down):
    cap = xb.shape[0]
    nb = cap // MOE_ROWS
    tf = 512
    nf = D_FF // tf

    def live(b, nbr):
        return jnp.minimum(b, nbr[0] - 1)

    def fsel(b, f, nbr):
        return jnp.where(b < nbr[0], f, nf - 1)

    grid_spec = pltpu.PrefetchScalarGridSpec(
        num_scalar_prefetch=2,
        grid=(nb, nf),
        in_specs=[pl.BlockSpec((MOE_ROWS, HALF_D), lambda b, f, be, nbr: (live(b, nbr), 0)),
                  pl.BlockSpec((1, D_MODEL, tf), lambda b, f, be, nbr: (be[live(b, nbr)], 0, fsel(b, f, nbr))),
                  pl.BlockSpec((1, D_MODEL, tf), lambda b, f, be, nbr: (be[live(b, nbr)], 0, fsel(b, f, nbr) + nf)),
                  pl.BlockSpec((1, tf, D_MODEL), lambda b, f, be, nbr: (be[live(b, nbr)], fsel(b, f, nbr), 0))],
        out_specs=pl.BlockSpec((MOE_ROWS, D_MODEL), lambda b, f, be, nbr: (b, 0)),
        scratch_shapes=[pltpu.VMEM((MOE_ROWS, D_MODEL), BF16)],
    )
    return pl.pallas_call(
        _expert_kernel,
        out_shape=jax.ShapeDtypeStruct((cap, D_MODEL), F32),
        grid_spec=grid_spec,
        compiler_params=_cparams(("arbitrary", "arbitrary")),
        name="expert_ffn",
    )(block_e, n_used, xb, w_gu, w_gu, w_down)


def _final_kernel(x_ref, y0_ref, y1_ref, w_ref, pv_ref, o_ref):
    w = w_ref[...]
    f = w[:, 0:1] * y0_ref[...] + w[:, 1:2] * y1_ref[...]
    x = x_ref[...] + pv_ref[0:1] * f
    o_ref[...] = _rms(x) * pv_ref[1:2]


def final_combine(x, y2, w, pv):
    m = x.shape[0]
    tm = min(m, 1024)
    row = lambda wd: pl.BlockSpec((tm, wd), lambda i: (i, 0))
    return pl.pallas_call(
        _final_kernel,
        out_shape=jax.ShapeDtypeStruct((m, D_MODEL), F32),
        grid=(m // tm,),
        in_specs=[row(D_MODEL), row(D_MODEL), pl.BlockSpec((tm, D_MODEL), lambda i: (i + m // tm, 0)), row(128),
                  pl.BlockSpec((8, D_MODEL), lambda i: (0, 0))],
        out_specs=row(D_MODEL),
        compiler_params=_cparams(("parallel",)),
        name="final_combine",
    )(x, y2, y2, w, pv)


CH = 128
NEG = -1e30


def _masks(reverse):
    r = lax.broadcasted_iota(jnp.int32, (CH, CH), 0)
    c = lax.broadcasted_iota(jnp.int32, (CH, CH), 1)
    return (r <= c, r < c) if reverse else (r >= c, r > c)


def _cumsum_time(incl, x):
    m = jnp.where(incl, 1.0, 0.0).astype(BF16)
    hi = x.astype(BF16)
    lo = (x - hi.astype(F32)).astype(BF16)
    return _dot(m, hi) + _dot(m, lo)


def _cummax_time(a, reverse):
    row = lax.broadcasted_iota(jnp.int32, a.shape, 0)
    k = 1
    while k < CH:
        if reverse:
            a = jnp.maximum(a, jnp.where(row < CH - k, pltpu.roll(a, CH - k, 0), NEG))
        else:
            a = jnp.maximum(a, jnp.where(row >= k, pltpu.roll(a, k, 0), NEG))
        k *= 2
    return a


SCAN_SUB = 4


def _scan_grid(l):
    rows = min(l, SCAN_SUB * CH)
    return l // rows, rows


def _sweep(d, nsub):
    return list(range(nsub)) if d == 0 else list(range(nsub - 1, -1, -1))


def _dir_specs(nsteps, rows, width, col_block):
    return [pl.BlockSpec((rows, width), lambda n: (n, col_block)),
            pl.BlockSpec((rows, width), lambda n: (nsteps - 1 - n, col_block))]


def _whole(a):
    return pl.BlockSpec(a.shape, lambda n: tuple(0 for _ in a.shape))


def _ret_consts():
    log_gamma = np.log(1.0 - 2.0 ** (-5.0 - np.arange(N_HEADS, dtype=np.float64)))
    pos = np.arange(CH, dtype=np.float64)
    diff = pos[:, None] - pos[None, :]
    dec_f = np.where(diff >= 0, np.exp(log_gamma[:, None, None] * diff), 0.0)
    dec = np.stack([dec_f, np.transpose(dec_f, (0, 2, 1))])
    qs_f = np.exp(log_gamma[None, :] * (pos[:, None] + 1.0))
    ks_f = np.exp(log_gamma[None, :] * (CH - 1.0 - pos[:, None]))
    qs_b = np.exp(log_gamma[None, :] * (CH - pos[:, None]))
    ks_b = np.exp(log_gamma[None, :] * pos[:, None])
    rep = lambda a: np.repeat(a, RET_DK, axis=1)
    qs = np.stack([rep(qs_f), rep(qs_b)])
    ks = np.stack([rep(ks_f), rep(ks_b)])
    chunk_decay = [float(np.exp(lg * CH)) for lg in log_gamma]
    return (jnp.asarray(dec, F32), jnp.asarray(qs, F32), jnp.asarray(ks, F32)), chunk_decay


def _ret_kernel(*refs, rotary, chunk_decay):
    if rotary:
        (qf, qb, kf, kb, vf, vb, cosf, cosb, sinf, sinb, dec_ref, qs_ref, ks_ref, s0_ref,
         of_ref, ob_ref, s_ref) = refs
        tabs = ((cosf, sinf), (cosb, sinb))
    else:
        qf, qb, kf, kb, vf, vb, dec_ref, qs_ref, ks_ref, s0_ref, of_ref, ob_ref, s_ref = refs
        tabs = (None, None)

    @pl.when(pl.program_id(0) == 0)
    def _():
        s_ref[...] = s0_ref[...]

    lane = lax.broadcasted_iota(jnp.int32, (CH, N_HEADS * RET_DK), 1)
    first_half = (lane & (RET_DK - 1)) < RET_DK // 2

    nsub = qf.shape[0] // CH
    chains = {}
    for d, (q_ref, k_ref, v_ref, o_ref) in enumerate(((qf, kf, vf, of_ref), (qb, kb, vb, ob_ref))):
        for j in range(nsub):
            rows = slice(j * CH, (j + 1) * CH)
            q = q_ref[rows, :].astype(F32)
            k = k_ref[rows, :].astype(F32) * RET_DK ** -0.5
            if rotary:
                cos, sin = tabs[d][0][rows, :], tabs[d][1][rows, :]

                def rot(x, cos=cos, sin=sin):
                    swapped = jnp.where(first_half, pltpu.roll(x, N_HEADS * RET_DK - RET_DK // 2, 1),
                                        pltpu.roll(x, RET_DK // 2, 1))
                    return x * cos + swapped * sin

                q, k = rot(q), rot(k)
            qb16, kb16 = q.astype(BF16), k.astype(BF16)
            q_in = (q * qs_ref[d]).astype(BF16)
            k_out = k * ks_ref[d]
            k_t = [k_out[:, :128].T, k_out[:, 128:].T]
            for h in range(N_HEADS):
                sl = slice(h * RET_DK, (h + 1) * RET_DK)
                chains[d, j, h] = dict(
                    o_ref=o_ref, rows=rows, vh=v_ref[rows, h * HEAD_V:(h + 1) * HEAD_V], q=qb16[:, sl], k=kb16[:, sl],
                    q_in=q_in[:, sl], k_th=k_t[h // 2][(h % 2) * RET_DK:(h % 2 + 1) * RET_DK, :].astype(BF16))

    for (d, j, h), c in chains.items():
        c["p"] = (_dot_nt(c["q"], c["k"]) * dec_ref[d, h]).astype(BF16)
    for c in chains.values():
        c["intra"] = _dot(c["p"], c["vh"])
        c["update"] = _dot(c["k_th"], c["vh"])
    states = {(d, h): s_ref[d, h] for d in range(2) for h in range(N_HEADS)}
    for i in range(nsub):
        for (d, h), s in list(states.items()):
            c = chains[d, _sweep(d, nsub)[i], h]
            c["o_ref"][c["rows"], h * HEAD_V:(h + 1) * HEAD_V] = c["intra"] + _dot(c["q_in"], s.astype(BF16))
            states[d, h] = chunk_decay[h] * s + c["update"]
    for (d, h), s in states.items():
        s_ref[d, h] = s


def retention_scan(z, s0, tables):
    l = z.shape[0]
    nsteps, rows = _scan_grid(l)
    consts, chunk_decay = _ret_consts()
    rotary = tables is not None
    w = N_HEADS * RET_DK
    specs = (_dir_specs(nsteps, rows, w, OFF_RET_Q // w) + _dir_specs(nsteps, rows, w, OFF_RET_K // w)
             + _dir_specs(nsteps, rows, BRANCH_W, OFF_RET_V // BRANCH_W))
    args = [z] * 6
    if rotary:
        specs += _dir_specs(nsteps, rows, w, 0) + _dir_specs(nsteps, rows, w, 0)
        args += [tables[0], tables[0], tables[1], tables[1]]
    specs += [_whole(c) for c in consts] + [_whole(s0)]
    args += list(consts) + [s0]
    return pl.pallas_call(
        functools.partial(_ret_kernel, rotary=rotary, chunk_decay=chunk_decay),
        out_shape=(jax.ShapeDtypeStruct((l, BRANCH_W), F32), jax.ShapeDtypeStruct((l, BRANCH_W), F32),
                   jax.ShapeDtypeStruct(s0.shape, F32)),
        grid=(nsteps,),
        in_specs=specs,
        out_specs=tuple(_dir_specs(nsteps, rows, BRANCH_W, 0)) + (_whole(s0),),
        compiler_params=_cparams(("arbitrary",)),
        name="retention_scan",
    )(*args)


def _mlstm_kernel(qf, qb, kf, kb, vf, vb, smf, smb, bias_ref, c0_ref, m0_ref, of_ref, ob_ref, c_ref, m_ref):
    @pl.when(pl.program_id(0) == 0)
    def _():
        c_ref[...] = c0_ref[...]
        m_ref[...] = m0_ref[...]

    lane = lax.broadcasted_iota(jnp.int32, (CH, 128), 1)
    is_forget = (lane >= F_LANE) & (lane < F_LANE + 2 * N_HEADS)
    ones_col = jnp.where(lane == 0, 1.0, 0.0).astype(BF16)

    nsub = qf.shape[0] // CH
    chains = {}
    for d, (q_ref, k_ref, v_ref, sm_ref, o_ref) in enumerate(((qf, kf, vf, smf, of_ref), (qb, kb, vb, smb, ob_ref))):
        reverse = d == 1
        incl, _ = _masks(reverse)
        last = 0 if reverse else CH - 1
        for j in range(nsub):
            rows = slice(j * CH, (j + 1) * CH)
            pre = sm_ref[rows, :] + bias_ref[0:1]
            x = jnp.where(is_forget, jax.nn.log_sigmoid(pre), pre)
            b = _cumsum_time(incl, x)
            a = pltpu.roll(x, F_LANE - I_LANE, 1) - b
            cm = _cummax_time(a, reverse)
            a_t = a.T
            q = (q_ref[rows, :].astype(F32) * MLSTM_DK ** -0.5).astype(BF16)
            kf32 = k_ref[rows, :].astype(F32)
            k_t = [kf32[:, :128].T, kf32[:, 128:].T]
            for h in range(N_HEADS):
                gf = F_LANE + d * N_HEADS + h
                sl = slice(h * MLSTM_DK, (h + 1) * MLSTM_DK)
                cm_col, a_row = cm[:, gf:gf + 1], a_t[gf:gf + 1, :]
                chains[d, j, h] = dict(
                    o_ref=o_ref, rows=rows, q=q[:, sl], k=k_ref[rows, sl], a_row=a_row, cm_col=cm_col,
                    b_col=b[:, gf:gf + 1], b_last=b[last:last + 1, gf:gf + 1], cm_last=cm[last:last + 1, gf:gf + 1],
                    w_intra=jnp.where(incl, jnp.exp(jnp.minimum(a_row - cm_col, 0.0)), 0.0),
                    k_th=k_t[h // 2][(h % 2) * MLSTM_DK:(h % 2 + 1) * MLSTM_DK, :],
                    v_aug=jnp.concatenate([v_ref[rows, h * HEAD_V:(h + 1) * HEAD_V], ones_col], axis=1))

    for c in chains.values():
        c["p"] = (c["w_intra"] * _dot_nt(c["q"], c["k"])).astype(BF16)
    for c in chains.values():
        c["pv"] = _dot(c["p"], c["v_aug"])
    c_augs = {(d, h): c_ref[d, h] for d in range(2) for h in range(N_HEADS)}
    m_ss = {(d, h): m_ref[d, h][0:1, 0:1] for d in range(2) for h in range(N_HEADS)}
    for i in range(nsub):
        cur = {(d, h): chains[d, _sweep(d, nsub)[i], h] for (d, h) in c_augs}
        qcs = {key: _dot(c["q"], c_augs[key].astype(BF16)) for key, c in cur.items()}
        tops = {key: jnp.maximum(m_ss[key], c["cm_last"]) for key, c in cur.items()}
        updates = {key: _dot((c["k_th"] * jnp.exp(c["a_row"] - tops[key])).astype(BF16), c["v_aug"])
                   for key, c in cur.items()}
        for (d, h), c in cur.items():
            m_s = m_ss[d, h]
            mx = jnp.maximum(m_s, c["cm_col"])
            tot = jnp.exp(m_s - mx) * qcs[d, h] + jnp.exp(c["cm_col"] - mx) * c["pv"]
            den = jnp.maximum(jnp.abs(tot[:, HEAD_V:HEAD_V + 1]), jnp.exp(-(c["b_col"] + mx)))
            c["o_ref"][c["rows"], h * HEAD_V:(h + 1) * HEAD_V] = tot[:, :HEAD_V] / den
        for key, c in cur.items():
            c_augs[key] = jnp.exp(m_ss[key] - tops[key]) * c_augs[key] + updates[key]
            m_ss[key] = c["b_last"] + tops[key]
    for (d, h) in c_augs:
        c_ref[d, h] = c_augs[d, h]
        m_ref[d, h] = jnp.broadcast_to(m_ss[d, h], (8, 128))


def mlstm_scan(z, small, bias, c0, m0):
    l = z.shape[0]
    nsteps, rows = _scan_grid(l)
    w = N_HEADS * MLSTM_DK
    specs = (_dir_specs(nsteps, rows, w, OFF_MLSTM_Q // w) + _dir_specs(nsteps, rows, w, OFF_MLSTM_K // w)
             + _dir_specs(nsteps, rows, BRANCH_W, OFF_MLSTM_V // BRANCH_W) + _dir_specs(nsteps, rows, 128, 0))
    specs += [_whole(bias), _whole(c0), _whole(m0)]
    return pl.pallas_call(
        _mlstm_kernel,
        out_shape=(jax.ShapeDtypeStruct((l, BRANCH_W), F32), jax.ShapeDtypeStruct((l, BRANCH_W), F32),
                   jax.ShapeDtypeStruct(c0.shape, F32), jax.ShapeDtypeStruct(m0.shape, F32)),
        grid=(nsteps,),
        in_specs=specs,
        out_specs=tuple(_dir_specs(nsteps, rows, BRANCH_W, 0)) + (_whole(c0), _whole(m0)),
        compiler_params=_cparams(("arbitrary",)),
        name="mlstm_scan",
    )(z, z, z, z, z, z, small, small, bias, c0, m0)


QKV_W = 3 * N_HEADS * GDN_DK
HALO = 8


def _gdn_prep_kernel(x_ref, prev_ref, next_ref, w_ref, q_ref, k_ref, v_ref):
    i = pl.program_id(0)
    tm = x_ref.shape[0]
    x = x_ref[...].astype(F32)
    prev = jnp.where(i > 0, prev_ref[...].astype(F32), 0.0)
    nxt = jnp.where(i < pl.num_programs(0) - 1, next_ref[...].astype(F32), 0.0)
    xe = jnp.concatenate([prev, x, nxt], axis=0)
    w = w_ref[...]
    y = None
    for tap in range(CONV_K):
        off = HALO + tap - CONV_K // 2
        term = w[tap:tap + 1] * xe[off:off + tm]
        y = term if y is None else y + term
    y = y * jax.nn.sigmoid(y)
    hw = N_HEADS * GDN_DK
    for h in range(N_HEADS):
        sl = slice(h * GDN_DK, (h + 1) * GDN_DK)
        qh = y[:, sl]
        kh = y[:, hw + h * GDN_DK:hw + (h + 1) * GDN_DK]
        q_ref[:, sl] = (qh * lax.rsqrt(jnp.sum(qh * qh, axis=-1, keepdims=True) + EPS) * GDN_DK ** -0.5).astype(BF16)
        k_ref[:, sl] = (kh * lax.rsqrt(jnp.sum(kh * kh, axis=-1, keepdims=True) + EPS)).astype(BF16)
    v_ref[...] = y[:, 2 * hw:].astype(BF16)


def gdn_prep(z, conv_w):
    l = z.shape[0]
    tm = min(l, 256)
    nb = l // tm
    r8 = tm // HALO
    cb = OFF_GDN_QKV // QKV_W
    w8 = jnp.concatenate([conv_w.astype(F32), jnp.zeros((8 - CONV_K, QKV_W), F32)], axis=0)
    return pl.pallas_call(
        _gdn_prep_kernel,
        out_shape=tuple(jax.ShapeDtypeStruct((l, BRANCH_W), BF16) for _ in range(3)),
        grid=(nb,),
        in_specs=[pl.BlockSpec((tm, QKV_W), lambda i: (i, cb)),
                  pl.BlockSpec((HALO, QKV_W), lambda i: (jnp.maximum(i * r8 - 1, 0), cb)),
                  pl.BlockSpec((HALO, QKV_W), lambda i: (jnp.minimum((i + 1) * r8, nb * r8 - 1), cb)),
                  pl.BlockSpec((8, QKV_W), lambda i: (0, 0))],
        out_specs=tuple(pl.BlockSpec((tm, BRANCH_W), lambda i: (i, 0)) for _ in range(3)),
        compiler_params=_cparams(("parallel",)),
        name="gdn_prep",
    )(z, z, z, w8)


N_LEVELS = 7


def _gdn_kernel(qf, qb, kf, kb, vf, vb, smf, smb, par_ref, s0_ref, of_ref, ob_ref, s_ref):
    @pl.when(pl.program_id(0) == 0)
    def _():
        s_ref[...] = s0_ref[...]

    lane = lax.broadcasted_iota(jnp.int32, (CH, 128), 1)
    is_decay = lane < B_LANE
    ri = lax.broadcasted_iota(jnp.int32, (CH, CH), 0)
    ci = lax.broadcasted_iota(jnp.int32, (CH, CH), 1)
    eye = jnp.where(ri == ci, 1.0, 0.0)
    pair_masks = [((ri >> (l + 1)) == (ci >> (l + 1))) & ((ri >> l) != (ci >> l)) for l in range(N_LEVELS)]

    nsub = qf.shape[0] // CH
    chains = {}
    for d, (q_ref, k_ref, v_ref, sm_ref, o_ref) in enumerate(((qf, kf, vf, smf, of_ref), (qb, kb, vb, smb, ob_ref))):
        reverse = d == 1
        incl, strict = _masks(reverse)
        last = 0 if reverse else CH - 1
        for j in range(nsub):
            rows = slice(j * CH, (j + 1) * CH)
            sm = sm_ref[rows, :]
            log_a = -jnp.exp(par_ref[1:2]) * jax.nn.softplus(sm + par_ref[0:1])
            x = jnp.where(is_decay, log_a, jax.nn.sigmoid(sm))
            g = _cumsum_time(incl, x)
            g_t = g.T
            for h in range(N_HEADS):
                ga, gb = A_LANE + d * N_HEADS + h, B_LANE + d * N_HEADS + h
                sl = slice(h * GDN_DK, (h + 1) * GDN_DK)
                g_col, g_row = g[:, ga:ga + 1], g_t[ga:ga + 1, :]
                g_last = g[last:last + 1, ga:ga + 1]
                beta = x[:, gb:gb + 1]
                decay = jnp.where(incl, jnp.exp(jnp.minimum(g_col - g_row, 0.0)), 0.0)
                kh = k_ref[rows, sl]
                kf32 = kh.astype(F32)
                kbeta = kf32 * beta
                a = jnp.where(strict, _dot_nt(kbeta.astype(BF16), kh) * decay, 0.0)
                e_g = jnp.exp(g_col)
                rhs = jnp.concatenate([v_ref[rows, sl].astype(F32) * beta, kbeta * e_g], axis=1)
                chains[d, j, h] = dict(
                    rows=rows, sl=sl, o_ref=o_ref, a=a, rhs=rhs, s_decay=jnp.exp(g_last),
                    qk=(_dot_nt(q_ref[rows, sl], kh) * decay).astype(BF16),
                    q_in=(q_ref[rows, sl].astype(F32) * e_g).astype(BF16),
                    k_out_t=(kf32.T * jnp.exp(g_last - g_row)).astype(BF16))

    cl = list(chains.values())
    xs = [eye - jnp.where(pair_masks[0], c["a"], 0.0) for c in cl]
    for pm in pair_masks[1:]:
        ys = [_dot(jnp.where(pm, c["a"], 0.0).astype(BF16), x.astype(BF16)) for c, x in zip(cl, xs)]
        xs = [x - _dot(x.astype(BF16), y.astype(BF16)) for x, y in zip(xs, ys)]
    for c, x in zip(cl, xs):
        c["sol"] = c["rhs"] + _dot((x - eye).astype(BF16), c["rhs"].astype(BF16))
    states = {(d, h): s_ref[d, h] for d in range(2) for h in range(N_HEADS)}
    for i in range(nsub):
        cur = {(d, h): chains[d, _sweep(d, nsub)[i], h] for (d, h) in states}
        s16s = {key: s.astype(BF16) for key, s in states.items()}
        v_news = {key: (c["sol"][:, :HEAD_V] - _dot(c["sol"][:, HEAD_V:].astype(BF16), s16s[key])).astype(BF16)
                  for key, c in cur.items()}
        updates = {key: _dot(c["k_out_t"], v_news[key]) for key, c in cur.items()}
        for key, c in cur.items():
            c["o_ref"][c["rows"], c["sl"]] = _dot(c["q_in"], s16s[key]) + _dot(c["qk"], v_news[key])
        for key, c in cur.items():
            states[key] = states[key] * c["s_decay"] + updates[key]
    for (d, h), s in states.items():
        s_ref[d, h] = s


def gdn_scan(qn, kn, vn, small, par, s0):
    l = qn.shape[0]
    nsteps, rows = _scan_grid(l)
    specs = _dir_specs(nsteps, rows, BRANCH_W, 0) * 3 + _dir_specs(nsteps, rows, 128, 0)
    specs += [_whole(par), _whole(s0)]
    return pl.pallas_call(
        _gdn_kernel,
        out_shape=(jax.ShapeDtypeStruct((l, BRANCH_W), F32), jax.ShapeDtypeStruct((l, BRANCH_W), F32),
                   jax.ShapeDtypeStruct(s0.shape, F32)),
        grid=(nsteps,),
        in_specs=specs,
        out_specs=tuple(_dir_specs(nsteps, rows, BRANCH_W, 0)) + (_whole(s0),),
        compiler_params=_cparams(("arbitrary",)),
        name="gdn_scan",
    )(qn, qn, kn, kn, vn, vn, small, small, par, s0)


def _lane_rows(entries):
    r = jnp.zeros((8, 128), F32)
    for row, lane, vals in entries:
        r = r.at[row, lane:lane + vals.shape[0]].set(vals.astype(F32))
    return r


def gdn_branch(zc, sc, zl, sl, conv_w, a_log, dt_bias):
    par = _lane_rows([(0, A_LANE, dt_bias.reshape(-1)), (1, A_LANE, a_log.reshape(-1))])
    s0 = jnp.zeros((2, N_HEADS, GDN_DK, HEAD_V), F32)
    ofc, obc, s1 = gdn_scan(*gdn_prep(zc, conv_w), sc, par, s0)
    ofl, obl, _ = gdn_scan(*gdn_prep(zl, conv_w), sl, par, s1)
    return (ofc, obc), (ofl, obl)


def mlstm_branch(zc, sc, zl, sl, gate_b):
    bias = _lane_rows([(0, I_LANE, gate_b[0].reshape(-1)), (0, F_LANE, gate_b[1].reshape(-1))])
    c0 = jnp.zeros((2, N_HEADS, MLSTM_DK, 2 * HEAD_V), F32)
    m0 = jnp.zeros((2, N_HEADS, 8, 128), F32)
    ofc, obc, c1, m1 = mlstm_scan(zc, sc, bias, c0, m0)
    ofl, obl, _, _ = mlstm_scan(zl, sl, bias, c1, m1)
    return (ofc, obc), (ofl, obl)


def retention_branch(zc, zl, tables):
    s0 = jnp.zeros((2, N_HEADS, RET_DK, HEAD_V), F32)
    ofc, obc, s1 = retention_scan(zc, s0, None)
    ofl, obl, _ = retention_scan(zl, s1, tables)
    return (ofc, obc), (ofl, obl)


def _ret_rope_tables(n_lat):
    inv = np.float32(ROPE_BASE) ** (-np.arange(0, RET_DK, 2, dtype=np.float32) / np.float32(RET_DK))
    ang = np.arange(n_lat, dtype=np.float32)[:, None] * inv[None, :]
    cos = np.concatenate([np.cos(ang), np.cos(ang)], axis=-1)
    sin = np.concatenate([-np.sin(ang), np.sin(ang)], axis=-1)
    return jnp.asarray(np.tile(cos, (1, N_HEADS)), F32), jnp.asarray(np.tile(sin, (1, N_HEADS)), F32)


def _rope_tables(n_lat):
    q = MLA_ROPE // 4
    inv = np.float32(ROPE_BASE) ** (-np.arange(0, 2 * q, 2, dtype=np.float32) / np.float32(2 * q))
    t = np.arange(n_lat)
    row = (t // GRID_W).astype(np.float32)[:, None] * inv[None, :]
    col = (t % GRID_W).astype(np.float32)[:, None] * inv[None, :]
    cos = np.concatenate([np.cos(row), np.cos(row), np.cos(col), np.cos(col)], axis=-1)
    sin = np.concatenate([-np.sin(row), np.sin(row), -np.sin(col), np.sin(col)], axis=-1)
    return jnp.asarray(cos, F32), jnp.asarray(sin, F32)


def _take_cols(w, perm):
    runs, start = [], 0
    for i in range(1, len(perm) + 1):
        if i == len(perm) or perm[i] != perm[i - 1] + 1:
            runs.append((int(perm[start]), int(perm[i - 1]) + 1))
            start = i
    return jnp.concatenate([w[:, a:b].astype(BF16) for a, b in runs], axis=1)


def _pad_rows(v, n=8):
    rows = [jnp.reshape(r, (1, -1)).astype(F32) for r in v]
    d = rows[0].shape[1]
    return jnp.concatenate(rows + [jnp.zeros((n - len(rows), d), F32)], axis=0)


def _mla_weights(w_uq, w_ukv):
    wq = w_uq.reshape(MLA_Q_RANK, N_HEADS, MLA_QK)
    rope = wq[:, :, MLA_NOPE:]
    swapped = rope[:, :, _rope_swap(np.arange(MLA_ROPE))]
    wq_ext = jnp.concatenate([wq[:, :, :MLA_NOPE], rope, swapped], axis=-1).reshape(MLA_Q_RANK, N_HEADS * 256)
    return wq_ext.astype(BF16), w_ukv.astype(BF16)


def _moe(x_l, pv2, g2, w_router, b_router, w_gu, w_down, final_g):
    n = x_l.shape[0]
    w_pad = jnp.concatenate([w_router, jnp.zeros((D_MODEL, 128 - N_EXPERTS), F32)], axis=1)
    b_pad = jnp.concatenate([b_router, jnp.full((128 - N_EXPERTS,), -1e30, F32)]).reshape(1, 128)
    h2, route = moe_router(x_l, pv2, w_pad, b_pad)
    n_assign = n * TOP_K
    experts = jnp.arange(N_EXPERTS)[None, :]
    e0, e1 = route[:, TOP_K].astype(jnp.int32), route[:, TOP_K + 1].astype(jnp.int32)
    hot0, hot1 = (e0[:, None] == experts).astype(jnp.int32), (e1[:, None] == experts).astype(jnp.int32)
    c0, c1 = jnp.cumsum(hot0, axis=0), jnp.cumsum(hot1, axis=0)
    before1 = c1 - hot1
    counts = c0[-1] + c1[-1]
    rank0 = jnp.sum(hot0 * (c0 - hot0 + before1), axis=1)
    rank1 = jnp.sum(hot1 * (c0 + before1), axis=1)
    padded = (counts + MOE_ROWS - 1) // MOE_ROWS * MOE_ROWS
    pad_end = jnp.cumsum(padded)
    pad_start = pad_end - padded
    dest = jnp.concatenate([pad_start[e0] + rank0, pad_start[e1] + rank1])
    nb = n_assign // MOE_ROWS + N_EXPERTS
    cap = nb * MOE_ROWS
    block_start = jnp.arange(nb, dtype=pad_end.dtype) * MOE_ROWS
    block_e = jnp.minimum(jnp.sum(block_start[:, None] >= pad_end[None, :], axis=1), N_EXPERTS - 1).astype(jnp.int32)
    n_used = (pad_end[-1] // MOE_ROWS).astype(jnp.int32).reshape(1)
    slot_tok = (jnp.arange(cap, dtype=jnp.int32) % n).at[dest].set(jnp.arange(n_assign, dtype=jnp.int32) % n)
    yb = expert_ffn(block_e, n_used, sc_gather(h2, slot_tok), w_gu, w_down)
    y2 = sc_gather(yb, dest.astype(jnp.int32))
    return final_combine(x_l, y2, route, _pad_rows([g2, final_g]))


def kernel(x, c, ctx, c_ctx, w_mod, b_mod, norm1_g, norm2_g, w_in, gdn_conv_w, gdn_a_log, gdn_dt_bias, gdn_norm_g, mla_q_norm_g, mla_kv_norm_g, mla_w_uq, mla_w_ukv, mlstm_gate_b, mlstm_norm_g, ret_norm_g, w_branch, w_out, ffn_w_in, ffn_w_down, moe_w_router, moe_b_router, moe_w_in, moe_w_down, final_norm_g):
    n_lat = x.shape[1]
    n_ctx = ctx.shape[1]
    x_l, x_c = x[0], ctx[0]
    cond = _pad_rows([c_ctx, c[0]])
    cos_l, sin_l = _rope_tables(n_lat)
    cos_c, sin_c = jnp.ones((n_ctx, MLA_ROPE), F32), jnp.zeros((n_ctx, MLA_ROPE), F32)
    ret_tables = _ret_rope_tables(n_lat)
    out = None
    for li in range(DEPTH):
        last = li == DEPTH - 1
        mod = modulation_vectors(cond, w_mod[li], b_mod[li])
        csh1, csc1, cg1, csh2, csc2, cg2 = jnp.split(mod[0], 6)
        sh1, sc1, g1, sh2, sc2, g2 = jnp.split(mod[1], 6)
        w_main = _take_cols(w_in[li], _MAIN_PERM)
        w_small = _take_cols(w_in[li], _SMALL_PERM)
        w_small = jnp.concatenate([w_small, jnp.zeros((D_MODEL, N_SMALL - w_small.shape[1]), BF16)], axis=1)
        pv_l = _pad_rows([norm1_g[li], 1 + sc1, sh1])
        pv_c = _pad_rows([norm1_g[li], 1 + csc1, csh1])
        zl, sl = norm_proj(x_l, pv_l, w_main, w_small, 3328)
        zc, sc = norm_proj(x_c, pv_c, w_main, w_small, 3328)

        a_c, a_l = gdn_branch(zc, sc, zl, sl, gdn_conv_w[li], gdn_a_log[li], gdn_dt_bias[li])
        c_c, c_l = mlstm_branch(zc, sc, zl, sl, mlstm_gate_b[li])
        d_c, d_l = retention_branch(zc, zl, ret_tables)
        gains = _pad_rows([gdn_norm_g[li], mlstm_norm_g[li], ret_norm_g[li]])

        wq_ext, wkv = _mla_weights(mla_w_uq[li], mla_w_ukv[li])
        gq, gkv = mla_q_norm_g[li].reshape(1, -1), mla_kv_norm_g[li].reshape(1, -1)
        ql, kl, vl = mla_project(zl, cos_l, sin_l, gq, gkv, wq_ext, wkv)
        qc, kc, vc = mla_project(zc, cos_c, sin_c, gq, gkv, wq_ext, wkv)
        b_l = attention(ql, [(kl, vl), (kc, vc)])

        wb = w_branch[li].astype(BF16)
        wo = w_out[li].astype(BF16)
        x_l = merge_branches(x_l, zl, a_l, b_l, c_l, d_l, gains, wb, wo, _pad_rows([g1]))
        if not last:
            b_c = attention(qc, [(kc, vc)])
            x_c = merge_branches(x_c, zc, a_c, b_c, c_c, d_c, gains, wb, wo, _pad_rows([cg1]))

        if li % 2 == 0:
            w_gu = ffn_w_in[li // 2].astype(BF16)
            w_dn = ffn_w_down[li // 2].astype(BF16)
            assert not last
            x_l = dense_ffn(x_l, _pad_rows([norm2_g[li], 1 + sc2, sh2, g2]), w_gu, w_dn)
            x_c = dense_ffn(x_c, _pad_rows([norm2_g[li], 1 + csc2, csh2, cg2]), w_gu, w_dn)
        else:
            assert last
            out = _moe(x_l, _pad_rows([norm2_g[li], 1 + sc2, sh2]), g2, moe_w_router[li // 2],
                       moe_b_router[li // 2], moe_w_in[li // 2], moe_w_down[li // 2], final_norm_g)
    return out[None]
```

```python
import functools
import math

import numpy as np
import jax
import jax.numpy as jnp
from jax import lax
from jax.experimental import pallas as pl
from jax.experimental.pallas import tpu as pltpu
from jax.experimental.pallas import tpu_sc as plsc

F32 = jnp.float32
BF16 = jnp.bfloat16

D_MODEL = 1024
DEPTH = 2
GRID_W = 64
N_BRANCH = 4
N_HEADS = 4
HEAD_V = 128
BRANCH_W = N_HEADS * HEAD_V
GDN_DK = 128
CONV_K = 5
MLA_Q_RANK = 384
MLA_KV_RANK = 256
MLA_NOPE = 128
MLA_ROPE = 64
MLA_QK = MLA_NOPE + MLA_ROPE
MLA_QK_PAD = 256
MLA_V_PAD = 256
MLSTM_DK = 64
RET_DK = 64
ROPE_BASE = 10000.0
D_FF = 3584
N_EXPERTS = 8
TOP_K = 2
EPS = 1e-6

IN_WIDTHS = (
    N_BRANCH * D_MODEL,
    N_HEADS * GDN_DK, N_HEADS * GDN_DK, BRANCH_W, BRANCH_W, 2 * N_HEADS, 2 * N_HEADS,
    MLA_Q_RANK, MLA_KV_RANK, MLA_ROPE,
    N_HEADS * MLSTM_DK, N_HEADS * MLSTM_DK, BRANCH_W, BRANCH_W, 2 * N_HEADS, 2 * N_HEADS,
    N_HEADS * RET_DK, N_HEADS * RET_DK, BRANCH_W, BRANCH_W,
)
_IN_OFF = [0] + [int(o) for o in np.cumsum(IN_WIDTHS)]

VMEM_LIMIT = 48 * 1024 * 1024
MERGE_VMEM_LIMIT = 56 * 1024 * 1024
MOE_ROWS = 1024


def _cols(group):
    return np.arange(_IN_OFF[group], _IN_OFF[group + 1])


def _rope_swap(cols):
    q = MLA_ROPE // 4
    return np.concatenate([cols[q:2 * q], cols[:q], cols[3 * q:], cols[2 * q:3 * q]])


_MAIN_PERM = np.concatenate([
    _cols(1), _cols(2), _cols(3),
    _cols(4),
    _cols(0),
    _cols(12), _cols(13),
    _cols(10), _cols(11),
    _cols(18), _cols(19),
    _cols(16), _cols(17),
    _cols(7), _cols(8), _cols(9), _rope_swap(_cols(9)),
])
N_MAIN = int(_MAIN_PERM.shape[0])
_SMALL_PERM = np.concatenate([_cols(5), _cols(6), _cols(14), _cols(15)])
N_SMALL = 128
A_LANE, B_LANE, I_LANE, F_LANE = 0, 8, 16, 24
OFF_GDN_QKV, OFF_GDN_Z, OFF_GATE = 0, 1536, 2048
OFF_MLSTM_V, OFF_MLSTM_O, OFF_MLSTM_Q, OFF_MLSTM_K = 6144, 6656, 7168, 7424
OFF_RET_V, OFF_RET_G, OFF_RET_Q, OFF_RET_K = 7680, 8192, 8704, 8960
OFF_MLA = 9216
MLA_IN_W = 768


def _cparams(sem):
    return pltpu.CompilerParams(dimension_semantics=sem, vmem_limit_bytes=VMEM_LIMIT)


def _rms(x):
    return x * lax.rsqrt(jnp.mean(x * x, axis=-1, keepdims=True) + EPS)


def _dot(a, b):
    return jnp.dot(a, b, preferred_element_type=F32)


def _dot_nt(a, b):
    return lax.dot_general(a, b, (((1,), (1,)), ((), ())), preferred_element_type=F32)


def _mod_kernel(c_ref, w_ref, b_ref, o_ref):
    c = c_ref[...]
    s = c * jax.nn.sigmoid(c)
    o_ref[...] = jnp.dot(s, w_ref[...], preferred_element_type=F32) + b_ref[...]


def modulation_vectors(cond, w_mod, b_mod):
    n = w_mod.shape[1]
    tn = 1536
    return pl.pallas_call(
        _mod_kernel,
        out_shape=jax.ShapeDtypeStruct((8, n), F32),
        grid=(n // tn,),
        in_specs=[pl.BlockSpec((8, D_MODEL), lambda j: (0, 0)),
                  pl.BlockSpec((D_MODEL, tn), lambda j: (0, j)),
                  pl.BlockSpec((1, tn), lambda j: (0, j))],
        out_specs=pl.BlockSpec((8, tn), lambda j: (0, j)),
        compiler_params=_cparams(("arbitrary",)),
        name="modulation",
    )(cond, w_mod, b_mod.reshape(1, n))


def _norm_proj_kernel(x_ref, pv_ref, w_ref, ws_ref, o_ref, os_ref, h_ref):
    @pl.when(pl.program_id(1) == 0)
    def _():
        pv = pv_ref[...]
        h = (_rms(x_ref[...]) * pv[0:1] * pv[1:2] + pv[2:3]).astype(BF16)
        h_ref[...] = h
        os_ref[...] = _dot(h, ws_ref[...])

    o_ref[...] = _dot(h_ref[...], w_ref[...]).astype(o_ref.dtype)


def norm_proj(x, pv, w, w_small, tn):
    m, d = x.shape
    n = w.shape[1]
    tm = min(m, 1024)
    return pl.pallas_call(
        _norm_proj_kernel,
        out_shape=(jax.ShapeDtypeStruct((m, n), BF16), jax.ShapeDtypeStruct((m, N_SMALL), F32)),
        grid=(m // tm, n // tn),
        in_specs=[pl.BlockSpec((tm, d), lambda i, j: (i, 0)),
                  pl.BlockSpec((8, d), lambda i, j: (0, 0)),
                  pl.BlockSpec((d, tn), lambda i, j: (0, j)),
                  pl.BlockSpec((d, N_SMALL), lambda i, j: (0, 0))],
        out_specs=(pl.BlockSpec((tm, tn), lambda i, j: (i, j)), pl.BlockSpec((tm, N_SMALL), lambda i, j: (i, 0))),
        scratch_shapes=[pltpu.VMEM((tm, d), BF16)],
        compiler_params=pltpu.CompilerParams(dimension_semantics=("parallel", "arbitrary"),
                                             vmem_limit_bytes=MERGE_VMEM_LIMIT),
        name="norm_proj",
    )(x, pv, w, w_small)


def _mla_proj_kernel(z_ref, cos_ref, sin_ref, gq_ref, gkv_ref, wq_ref, wkv_ref, q_ref, k_ref, v_ref):
    z = z_ref[...].astype(F32)
    tm = z.shape[0]
    cq = z[:, :MLA_Q_RANK]
    ckv = z[:, MLA_Q_RANK:MLA_Q_RANK + MLA_KV_RANK]
    kr = z[:, MLA_Q_RANK + MLA_KV_RANK:]
    cos = cos_ref[...]
    sin = sin_ref[...]
    qn = (_rms(cq) * gq_ref[...]).astype(BF16)
    kvn = (_rms(ckv) * gkv_ref[...]).astype(BF16)
    qf = _dot(qn, wq_ref[...]) * (MLA_QK ** -0.5 * math.log2(math.e))
    kvf = _dot(kvn, wkv_ref[...])
    kr_rot = kr[:, :MLA_ROPE] * cos + kr[:, MLA_ROPE:] * sin
    pad = jnp.zeros((tm, MLA_QK_PAD - MLA_QK), F32)
    lane = lax.broadcasted_iota(jnp.int32, (tm, MLA_V_PAD - HEAD_V), 1)
    ones_col = jnp.where(lane == 0, 1.0, 0.0).astype(BF16)
    for h in range(N_HEADS):
        b = h * 256
        q_rot = qf[:, b + 128:b + 192] * cos + qf[:, b + 192:b + 256] * sin
        q_ref[h] = jnp.concatenate([qf[:, b:b + 128], q_rot, pad], axis=-1).astype(BF16)
        k_ref[h] = jnp.concatenate([kvf[:, b:b + 128], kr_rot, pad], axis=-1).astype(BF16)
        v_ref[h] = jnp.concatenate([kvf[:, b + 128:b + 256].astype(BF16), ones_col], axis=-1)


def mla_project(zmain, cos, sin, gq, gkv, wq_ext, wkv):
    m = zmain.shape[0]
    tm = min(m, 1024)
    full = lambda shape: pl.BlockSpec(shape, lambda i: tuple(0 for _ in shape))
    return pl.pallas_call(
        _mla_proj_kernel,
        out_shape=(jax.ShapeDtypeStruct((N_HEADS, m, MLA_QK_PAD), BF16),
                   jax.ShapeDtypeStruct((N_HEADS, m, MLA_QK_PAD), BF16),
                   jax.ShapeDtypeStruct((N_HEADS, m, MLA_V_PAD), BF16)),
        grid=(m // tm,),
        in_specs=[pl.BlockSpec((tm, MLA_IN_W), lambda i: (i, OFF_MLA // MLA_IN_W)),
                  pl.BlockSpec((tm, MLA_ROPE), lambda i: (i, 0)),
                  pl.BlockSpec((tm, MLA_ROPE), lambda i: (i, 0)),
                  full((1, MLA_Q_RANK)), full((1, MLA_KV_RANK)),
                  full((MLA_Q_RANK, N_HEADS * 256)), full((MLA_KV_RANK, N_HEADS * 256))],
        out_specs=(pl.BlockSpec((N_HEADS, tm, MLA_QK_PAD), lambda i: (0, i, 0)),
                   pl.BlockSpec((N_HEADS, tm, MLA_QK_PAD), lambda i: (0, i, 0)),
                   pl.BlockSpec((N_HEADS, tm, MLA_V_PAD), lambda i: (0, i, 0))),
        compiler_params=_cparams(("parallel",)),
        name="mla_project",
    )(zmain, cos, sin, gq, gkv, wq_ext, wkv)


ATTN_TQ, ATTN_TK = 512, 512
ATTN_UNROLL = 32


def _attn_kernel(*refs, segs):
    q_ref, o_ref = refs[0], refs[-1]
    q = q_ref[0]
    tq = q.shape[0]
    carry = (jnp.full((tq, 1), -1e30, F32), jnp.zeros((tq, MLA_V_PAD), F32))
    for si, (tk, nk) in enumerate(segs):
        k_ref, v_ref = refs[1 + 2 * si], refs[2 + 2 * si]

        def body(c, carry, k_ref=k_ref, v_ref=v_ref, tk=tk):
            m, acc = carry
            start = pl.multiple_of(c * tk, tk)
            s = _dot_nt(q, k_ref[0, pl.ds(start, tk), :])
            m_new = jnp.maximum(m, jnp.max(s, axis=-1, keepdims=True))
            p = jnp.exp2(s - m_new).astype(BF16)
            acc = jnp.exp2(m - m_new) * acc + _dot(p, v_ref[0, pl.ds(start, tk), :])
            return m_new, acc

        carry = lax.fori_loop(0, nk, body, carry, unroll=min(ATTN_UNROLL, nk))
    _, acc = carry
    o_ref[...] = (acc[:, :HEAD_V] / acc[:, HEAD_V:HEAD_V + 1]).astype(o_ref.dtype)


def attention(q, kvs):
    _, lq, _ = q.shape
    tq = min(lq, ATTN_TQ)
    segs, args, specs = [], [], []
    for k, v in kvs:
        lk = k.shape[1]
        tk = min(lk, ATTN_TK)
        segs.append((tk, lk // tk))
        args += [k, v]
        specs += [pl.BlockSpec((1, lk, MLA_QK_PAD), lambda h, i: (h, 0, 0)),
                  pl.BlockSpec((1, lk, MLA_V_PAD), lambda h, i: (h, 0, 0))]
    return pl.pallas_call(
        functools.partial(_attn_kernel, segs=tuple(segs)),
        out_shape=jax.ShapeDtypeStruct((lq, N_HEADS * HEAD_V), BF16),
        grid=(N_HEADS, lq // tq),
        in_specs=[pl.BlockSpec((1, tq, MLA_QK_PAD), lambda h, i: (h, i, 0))] + specs,
        out_specs=pl.BlockSpec((tq, HEAD_V), lambda h, i: (i, h)),
        compiler_params=_cparams(("parallel", "arbitrary")),
        name="attention",
    )(q, *args)


def _head_post(o, gate, gain, centre, silu_gate):
    sig = jax.nn.sigmoid(gate)
    act = gate * sig if silu_gate else sig
    outs = []
    for h in range(N_HEADS):
        sl = slice(h * HEAD_V, (h + 1) * HEAD_V)
        oh = o[:, sl]
        if centre:
            oh = oh - jnp.mean(oh, axis=-1, keepdims=True)
        oh = oh * lax.rsqrt(jnp.mean(oh * oh, axis=-1, keepdims=True) + EPS)
        outs.append((oh * gain[:, sl] * act[:, sl]).astype(BF16))
    return jnp.concatenate(outs, axis=1)


def _merge_kernel(x_ref, g0_ref, g1_ref, g2_ref, g3_ref, af_ref, ab_ref, b_ref, cf_ref, cb_ref, df_ref, db_ref,
                  za_ref, zc_ref, zd_ref, gain_ref, wb_ref, wo_ref, g_ref, o_ref):
    gain = gain_ref[...]
    branches = (
        _head_post(af_ref[...] + ab_ref[...], za_ref[...].astype(F32), gain[0:1], False, True),
        b_ref[...],
        _head_post(cf_ref[...] + cb_ref[...], zc_ref[...].astype(F32), gain[1:2], False, False),
        _head_post(df_ref[...] + db_ref[...], zd_ref[...].astype(F32), gain[2:3], True, True),
    )
    s = None
    for n, (br, gate_ref) in enumerate(zip(branches, (g0_ref, g1_ref, g2_ref, g3_ref))):
        proj = _dot(br, wb_ref[n])
        gate = jax.nn.sigmoid(gate_ref[...].astype(F32))
        s = gate * proj if s is None else s + gate * proj
    m = _dot(s.astype(BF16), wo_ref[...])
    o_ref[...] = x_ref[...] + g_ref[0:1] * m


def merge_branches(x, zmain, gdn, attn, mlstm, ret, gains, w_branch, w_out, gvec):
    m = x.shape[0]
    tm = min(m, 512)
    row = lambda w: pl.BlockSpec((tm, w), lambda i: (i, 0))
    zcol = lambda off: pl.BlockSpec((tm, BRANCH_W), lambda i: (i, off // BRANCH_W))
    return pl.pallas_call(
        _merge_kernel,
        out_shape=jax.ShapeDtypeStruct((m, D_MODEL), F32),
        grid=(m // tm,),
        in_specs=[row(D_MODEL)]
                 + [pl.BlockSpec((tm, D_MODEL), lambda i, n=n: (i, OFF_GATE // D_MODEL + n)) for n in range(N_BRANCH)]
                 + [row(BRANCH_W)] * 7
                 + [zcol(OFF_GDN_Z), zcol(OFF_MLSTM_O), zcol(OFF_RET_G),
                    pl.BlockSpec((8, BRANCH_W), lambda i: (0, 0)),
                    pl.BlockSpec((N_BRANCH, BRANCH_W, D_MODEL), lambda i: (0, 0, 0)),
                    pl.BlockSpec((D_MODEL, D_MODEL), lambda i: (0, 0)),
                    pl.BlockSpec((8, D_MODEL), lambda i: (0, 0))],
        out_specs=row(D_MODEL),
        compiler_params=pltpu.CompilerParams(dimension_semantics=("parallel",), vmem_limit_bytes=MERGE_VMEM_LIMIT),
        name="merge_branches",
    )(x, zmain, zmain, zmain, zmain, gdn[0], gdn[1], attn, mlstm[0], mlstm[1], ret[0], ret[1],
      zmain, zmain, zmain, gains, w_branch, w_out, gvec)


def _ffn_kernel(x_ref, pv_ref, wg_ref, wu_ref, wd_ref, o_ref, h_ref):
    f = pl.program_id(1)

    def partial_out(h):
        g = _dot(h, wg_ref[...])
        u = _dot(h, wu_ref[...])
        act = (g * jax.nn.sigmoid(g) * u).astype(BF16)
        return pv_ref[3:4] * _dot(act, wd_ref[...])

    @pl.when(f == 0)
    def _():
        pv = pv_ref[...]
        h = (_rms(x_ref[...]) * pv[0:1] * pv[1:2] + pv[2:3]).astype(BF16)
        h_ref[...] = h
        o_ref[...] = x_ref[...] + partial_out(h)

    @pl.when(f > 0)
    def _():
        o_ref[...] += partial_out(h_ref[...])


def dense_ffn(x, pv, w_gu, w_down):
    m = x.shape[0]
    tm = min(m, 1024)
    tf = 512
    nf = D_FF // tf
    return pl.pallas_call(
        _ffn_kernel,
        out_shape=jax.ShapeDtypeStruct((m, D_MODEL), F32),
        grid=(m // tm, nf),
        in_specs=[pl.BlockSpec((tm, D_MODEL), lambda i, f: (i, 0)),
                  pl.BlockSpec((8, D_MODEL), lambda i, f: (0, 0)),
                  pl.BlockSpec((D_MODEL, tf), lambda i, f: (0, f)),
                  pl.BlockSpec((D_MODEL, tf), lambda i, f: (0, f + nf)),
                  pl.BlockSpec((tf, D_MODEL), lambda i, f: (f, 0))],
        out_specs=pl.BlockSpec((tm, D_MODEL), lambda i, f: (i, 0)),
        scratch_shapes=[pltpu.VMEM((tm, D_MODEL), BF16)],
        compiler_params=_cparams(("parallel", "arbitrary")),
        name="dense_ffn",
    )(x, pv, w_gu, w_gu, w_down)


HALF_D = D_MODEL // 2


def _pack_bf16_pairs(h):
    def rne_bits(v):
        u = pltpu.bitcast(v, jnp.uint32)
        return (u + jnp.uint32(0x7FFF) + ((u >> 16) & jnp.uint32(1))) & jnp.uint32(0xFFFF0000)

    return pltpu.bitcast(rne_bits(h[:, HALF_D:]) | (rne_bits(h[:, :HALF_D]) >> 16), F32)


def _unpack_bf16_pairs(p):
    u = pltpu.bitcast(p, jnp.uint32)
    lo = pltpu.bitcast(u << 16, F32).astype(BF16)
    hi = pltpu.bitcast(u & jnp.uint32(0xFFFF0000), F32).astype(BF16)
    return jnp.concatenate([lo, hi], axis=1)


def _router_kernel(x_ref, pv_ref, w_ref, b_ref, h_ref, route_ref):
    pv = pv_ref[...]
    h = _rms(x_ref[...]) * pv[0:1] * pv[1:2] + pv[2:3]
    h_hi = h.astype(BF16)
    h_lo = (h - h_hi.astype(F32)).astype(BF16)
    w = w_ref[...]
    w_hi = w.astype(BF16)
    w_lo = (w - w_hi.astype(F32)).astype(BF16)
    logits = _dot(h_hi, w_hi) + _dot(h_hi, w_lo) + _dot(h_lo, w_hi) + b_ref[...]
    h_ref[...] = _pack_bf16_pairs(h)
    lane = lax.broadcasted_iota(jnp.int32, logits.shape, 1).astype(F32)
    m1 = jnp.max(logits, axis=1, keepdims=True)
    e1 = jnp.min(jnp.where(logits == m1, lane, 128.0), axis=1, keepdims=True)
    rest = jnp.where(lane == e1, -jnp.inf, logits)
    m2 = jnp.max(rest, axis=1, keepdims=True)
    e2 = jnp.min(jnp.where(rest == m2, lane, 128.0), axis=1, keepdims=True)
    z2 = jnp.exp(m2 - m1)
    w1 = 1.0 / (1.0 + z2)
    route_ref[...] = jnp.where(lane == 0.0, w1, jnp.where(lane == 1.0, z2 * w1, jnp.where(
        lane == 2.0, e1, jnp.where(lane == 3.0, e2, 0.0))))


def moe_router(x, pv, w_router_pad, b_router_pad):
    m = x.shape[0]
    tm = min(m, 1024)
    return pl.pallas_call(
        _router_kernel,
        out_shape=(jax.ShapeDtypeStruct((m, HALF_D), F32), jax.ShapeDtypeStruct((m, 128), F32)),
        grid=(m // tm,),
        in_specs=[pl.BlockSpec((tm, D_MODEL), lambda i: (i, 0)),
                  pl.BlockSpec((8, D_MODEL), lambda i: (0, 0)),
                  pl.BlockSpec((D_MODEL, 128), lambda i: (0, 0)),
                  pl.BlockSpec((1, 128), lambda i: (0, 0))],
        out_specs=(pl.BlockSpec((tm, HALF_D), lambda i: (i, 0)), pl.BlockSpec((tm, 128), lambda i: (i, 0))),
        compiler_params=_cparams(("parallel",)),
        name="moe_router",
    )(x, pv, w_router_pad, b_router_pad)


SC_CORES, SC_SUBCORES = 2, 16
SC_ROWS = 32


def sc_gather(table, idx):
    b, d = idx.shape[0], table.shape[1]
    nw = SC_CORES * SC_SUBCORES
    assert b % (nw * 2 * SC_ROWS) == 0
    per_w = b // nw
    npairs = per_w // (2 * SC_ROWS)
    mesh = plsc.VectorSubcoreMesh(core_axis_name="c", subcore_axis_name="s")

    @functools.partial(
        pl.kernel, mesh=mesh, out_type=jax.ShapeDtypeStruct((b, d), table.dtype),
        scratch_types=[pltpu.VMEM((SC_ROWS,), jnp.int32), pltpu.VMEM((SC_ROWS,), jnp.int32),
                       pltpu.VMEM((SC_ROWS, d), table.dtype), pltpu.VMEM((SC_ROWS, d), table.dtype),
                       pltpu.SemaphoreType.DMA, pltpu.SemaphoreType.DMA])
    def gather_rows(table_hbm, idx_hbm, out_hbm, idx0, idx1, rows0, rows1, sem0, sem1):
        start = (lax.axis_index("s") * SC_CORES + lax.axis_index("c")) * per_w

        def row_copy(idx_v, rows_v, sem):
            return pltpu.make_async_copy(table_hbm.at[idx_v], rows_v, sem)

        def fetch(chunk, idx_v, rows_v, sem):
            pltpu.sync_copy(idx_hbm.at[pl.ds(start + chunk * SC_ROWS, SC_ROWS)], idx_v)
            row_copy(idx_v, rows_v, sem).start()

        def write(chunk, rows_v):
            pltpu.sync_copy(rows_v, out_hbm.at[pl.ds(start + chunk * SC_ROWS, SC_ROWS)])

        fetch(0, idx0, rows0, sem0)

        def body(j, carry):
            fetch(2 * j + 1, idx1, rows1, sem1)
            row_copy(idx0, rows0, sem0).wait()
            write(2 * j, rows0)

            @pl.when(j + 1 < npairs)
            def _():
                fetch(2 * j + 2, idx0, rows0, sem0)

            row_copy(idx1, rows1, sem1).wait()
            write(2 * j + 1, rows1)
            return carry

        lax.fori_loop(0, npairs, body, 0)

    return gather_rows(table, idx)


def _expert_kernel(be_ref, nb_ref, x_ref, wg_ref, wu_ref, wd_ref, o_ref, x16_ref, acc_ref):
    b = pl.program_id(0)
    f = pl.program_id(1)

    live = b < nb_ref[0]

    def partial_out(x):
        g = _dot(x, wg_ref[0].astype(BF16))
        u = _dot(x, wu_ref[0].astype(BF16))
        act = (g * jax.nn.sigmoid(g) * u).astype(BF16)
        return _dot(act, wd_ref[0].astype(BF16))

    last = pl.num_programs(1) - 1

    @pl.when(live & (f == 0))
    def _():
        x = _unpack_bf16_pairs(x_ref[...])
        x16_ref[...] = x
        acc_ref[...] = partial_out(x)

    @pl.when(live & (f > 0) & (f < last))
    def _():
        acc_ref[...] += partial_out(x16_ref[...])

    @pl.when(live & (f == last))
    def _():
        o_ref[...] = _pack_bf16_pairs(acc_ref[...] + partial_out(x16_ref[...]))

    @pl.when(jnp.logical_not(live) & (f == 0))
    def _():
        o_ref[...] = jnp.zeros_like(o_ref)


def expert_ffn(block_e, n_used, xb, w_gu, w_down):
    cap = xb.shape[0]
    nb = cap // MOE_ROWS
    tf = 512
    nf = D_FF // tf

    def live(b, nbr):
        return jnp.minimum(b, nbr[0] - 1)

    def fsel(b, f, nbr):
        return jnp.where(b < nbr[0], f, nf - 1)

    grid_spec = pltpu.PrefetchScalarGridSpec(
        num_scalar_prefetch=2,
        grid=(nb, nf),
        in_specs=[pl.BlockSpec((MOE_ROWS, HALF_D), lambda b, f, be, nbr: (live(b, nbr), 0)),
                  pl.BlockSpec((1, D_MODEL, tf), lambda b, f, be, nbr: (be[live(b, nbr)], 0, fsel(b, f, nbr))),
                  pl.BlockSpec((1, D_MODEL, tf), lambda b, f, be, nbr: (be[live(b, nbr)], 0, fsel(b, f, nbr) + nf)),
                  pl.BlockSpec((1, tf, D_MODEL), lambda b, f, be, nbr: (be[live(b, nbr)], fsel(b, f, nbr), 0))],
        out_specs=pl.BlockSpec((MOE_ROWS, HALF_D), lambda b, f, be, nbr: (b, 0)),
        scratch_shapes=[pltpu.VMEM((MOE_ROWS, D_MODEL), BF16), pltpu.VMEM((MOE_ROWS, D_MODEL), F32)],
    )
    return pl.pallas_call(
        _expert_kernel,
        out_shape=jax.ShapeDtypeStruct((cap, HALF_D), F32),
        grid_spec=grid_spec,
        compiler_params=_cparams(("arbitrary", "arbitrary")),
        name="expert_ffn",
    )(block_e, n_used, xb, w_gu, w_gu, w_down)


def _final_kernel(x_ref, y0_ref, y1_ref, w_ref, pv_ref, o_ref):
    w = w_ref[...]
    y0 = _unpack_bf16_pairs(y0_ref[...]).astype(F32)
    y1 = _unpack_bf16_pairs(y1_ref[...]).astype(F32)
    f = w[:, 0:1] * y0 + w[:, 1:2] * y1
    x = x_ref[...] + pv_ref[0:1] * f
    o_ref[...] = _rms(x) * pv_ref[1:2]


def final_combine(x, y2, w, pv):
    m = x.shape[0]
    tm = min(m, 1024)
    row = lambda wd: pl.BlockSpec((tm, wd), lambda i: (i, 0))
    return pl.pallas_call(
        _final_kernel,
        out_shape=jax.ShapeDtypeStruct((m, D_MODEL), F32),
        grid=(m // tm,),
        in_specs=[row(D_MODEL), row(HALF_D), pl.BlockSpec((tm, HALF_D), lambda i: (i + m // tm, 0)), row(128),
                  pl.BlockSpec((8, D_MODEL), lambda i: (0, 0))],
        out_specs=row(D_MODEL),
        compiler_params=_cparams(("parallel",)),
        name="final_combine",
    )(x, y2, y2, w, pv)


CH = 128
NEG = -1e30


def _masks(reverse):
    r = lax.broadcasted_iota(jnp.int32, (CH, CH), 0)
    c = lax.broadcasted_iota(jnp.int32, (CH, CH), 1)
    return (r <= c, r < c) if reverse else (r >= c, r > c)


def _cumsum_time(incl, x):
    m = jnp.where(incl, 1.0, 0.0).astype(BF16)
    hi = x.astype(BF16)
    lo = (x - hi.astype(F32)).astype(BF16)
    return _dot(m, hi) + _dot(m, lo)


def _cummax_time(a, reverse):
    row = lax.broadcasted_iota(jnp.int32, a.shape, 0)
    k = 1
    while k < CH:
        if reverse:
            a = jnp.maximum(a, jnp.where(row < CH - k, pltpu.roll(a, CH - k, 0), NEG))
        else:
            a = jnp.maximum(a, jnp.where(row >= k, pltpu.roll(a, k, 0), NEG))
        k *= 2
    return a


SCAN_SUB = 4


def _scan_grid(l):
    rows = min(l, SCAN_SUB * CH)
    return l // rows, rows


def _sweep(d, nsub):
    return list(range(nsub)) if d == 0 else list(range(nsub - 1, -1, -1))


def _dir_specs(nsteps, rows, width, col_block):
    return [pl.BlockSpec((rows, width), lambda n: (n, col_block)),
            pl.BlockSpec((rows, width), lambda n: (nsteps - 1 - n, col_block))]


def _whole(a):
    return pl.BlockSpec(a.shape, lambda n: tuple(0 for _ in a.shape))


def _ret_consts():
    log_gamma = np.log(1.0 - 2.0 ** (-5.0 - np.arange(N_HEADS, dtype=np.float64)))
    pos = np.arange(CH, dtype=np.float64)
    diff = pos[:, None] - pos[None, :]
    dec_f = np.where(diff >= 0, np.exp(log_gamma[:, None, None] * diff), 0.0)
    dec = np.stack([dec_f, np.transpose(dec_f, (0, 2, 1))])
    qs_f = np.exp(log_gamma[None, :] * (pos[:, None] + 1.0))
    ks_f = np.exp(log_gamma[None, :] * (CH - 1.0 - pos[:, None]))
    qs_b = np.exp(log_gamma[None, :] * (CH - pos[:, None]))
    ks_b = np.exp(log_gamma[None, :] * pos[:, None])
    rep = lambda a: np.repeat(a, RET_DK, axis=1)
    qs = np.stack([rep(qs_f), rep(qs_b)])
    ks = np.stack([rep(ks_f), rep(ks_b)])
    chunk_decay = [float(np.exp(lg * CH)) for lg in log_gamma]
    return (jnp.asarray(dec, F32), jnp.asarray(qs, F32), jnp.asarray(ks, F32)), chunk_decay


def _ret_kernel(*refs, rotary, chunk_decay):
    if rotary:
        (qf, qb, kf, kb, vf, vb, cosf, cosb, sinf, sinb, dec_ref, qs_ref, ks_ref, s0_ref,
         of_ref, ob_ref, s_ref) = refs
        tabs = ((cosf, sinf), (cosb, sinb))
    else:
        qf, qb, kf, kb, vf, vb, dec_ref, qs_ref, ks_ref, s0_ref, of_ref, ob_ref, s_ref = refs
        tabs = (None, None)

    @pl.when(pl.program_id(0) == 0)
    def _():
        s_ref[...] = s0_ref[...]

    lane = lax.broadcasted_iota(jnp.int32, (CH, N_HEADS * RET_DK), 1)
    first_half = (lane & (RET_DK - 1)) < RET_DK // 2

    nsub = qf.shape[0] // CH
    chains = {}
    for d, (q_ref, k_ref, v_ref, o_ref) in enumerate(((qf, kf, vf, of_ref), (qb, kb, vb, ob_ref))):
        for j in range(nsub):
            rows = slice(j * CH, (j + 1) * CH)
            q = q_ref[rows, :].astype(F32)
            k = k_ref[rows, :].astype(F32) * RET_DK ** -0.5
            if rotary:
                cos, sin = tabs[d][0][rows, :], tabs[d][1][rows, :]

                def rot(x, cos=cos, sin=sin):
                    swapped = jnp.where(first_half, pltpu.roll(x, N_HEADS * RET_DK - RET_DK // 2, 1),
                                        pltpu.roll(x, RET_DK // 2, 1))
                    return x * cos + swapped * sin

                q, k = rot(q), rot(k)
            qb16, kb16 = q.astype(BF16), k.astype(BF16)
            q_in = (q * qs_ref[d]).astype(BF16)
            k_out = k * ks_ref[d]
            k_t = [k_out[:, :128].T, k_out[:, 128:].T]
            for h in range(N_HEADS):
                sl = slice(h * RET_DK, (h + 1) * RET_DK)
                chains[d, j, h] = dict(
                    o_ref=o_ref, rows=rows, vh=v_ref[rows, h * HEAD_V:(h + 1) * HEAD_V], q=qb16[:, sl], k=kb16[:, sl],
                    q_in=q_in[:, sl], k_th=k_t[h // 2][(h % 2) * RET_DK:(h % 2 + 1) * RET_DK, :].astype(BF16))

    for (d, j, h), c in chains.items():
        c["p"] = (_dot_nt(c["q"], c["k"]) * dec_ref[d, h]).astype(BF16)
    for c in chains.values():
        c["intra"] = _dot(c["p"], c["vh"])
        c["update"] = _dot(c["k_th"], c["vh"])
    states = {(d, h): s_ref[d, h] for d in range(2) for h in range(N_HEADS)}
    for i in range(nsub):
        for (d, h), s in list(states.items()):
            c = chains[d, _sweep(d, nsub)[i], h]
            c["o_ref"][c["rows"], h * HEAD_V:(h + 1) * HEAD_V] = c["intra"] + _dot(c["q_in"], s.astype(BF16))
            states[d, h] = chunk_decay[h] * s + c["update"]
    for (d, h), s in states.items():
        s_ref[d, h] = s


def retention_scan(z, s0, tables):
    l = z.shape[0]
    nsteps, rows = _scan_grid(l)
    consts, chunk_decay = _ret_consts()
    rotary = tables is not None
    w = N_HEADS * RET_DK
    specs = (_dir_specs(nsteps, rows, w, OFF_RET_Q // w) + _dir_specs(nsteps, rows, w, OFF_RET_K // w)
             + _dir_specs(nsteps, rows, BRANCH_W, OFF_RET_V // BRANCH_W))
    args = [z] * 6
    if rotary:
        specs += _dir_specs(nsteps, rows, w, 0) + _dir_specs(nsteps, rows, w, 0)
        args += [tables[0], tables[0], tables[1], tables[1]]
    specs += [_whole(c) for c in consts] + [_whole(s0)]
    args += list(consts) + [s0]
    return pl.pallas_call(
        functools.partial(_ret_kernel, rotary=rotary, chunk_decay=chunk_decay),
        out_shape=(jax.ShapeDtypeStruct((l, BRANCH_W), F32), jax.ShapeDtypeStruct((l, BRANCH_W), F32),
                   jax.ShapeDtypeStruct(s0.shape, F32)),
        grid=(nsteps,),
        in_specs=specs,
        out_specs=tuple(_dir_specs(nsteps, rows, BRANCH_W, 0)) + (_whole(s0),),
        compiler_params=_cparams(("arbitrary",)),
        name="retention_scan",
    )(*args)


def _mlstm_kernel(qf, qb, kf, kb, vf, vb, smf, smb, bias_ref, c0_ref, m0_ref, of_ref, ob_ref, c_ref, m_ref):
    @pl.when(pl.program_id(0) == 0)
    def _():
        c_ref[...] = c0_ref[...]
        m_ref[...] = m0_ref[...]

    lane = lax.broadcasted_iota(jnp.int32, (CH, 128), 1)
    is_forget = (lane >= F_LANE) & (lane < F_LANE + 2 * N_HEADS)
    ones_col = jnp.where(lane == 0, 1.0, 0.0).astype(BF16)

    nsub = qf.shape[0] // CH
    chains = {}
    for d, (q_ref, k_ref, v_ref, sm_ref, o_ref) in enumerate(((qf, kf, vf, smf, of_ref), (qb, kb, vb, smb, ob_ref))):
        reverse = d == 1
        incl, _ = _masks(reverse)
        last = 0 if reverse else CH - 1
        for j in range(nsub):
            rows = slice(j * CH, (j + 1) * CH)
            pre = sm_ref[rows, :] + bias_ref[0:1]
            x = jnp.where(is_forget, jax.nn.log_sigmoid(pre), pre)
            b = _cumsum_time(incl, x)
            a = pltpu.roll(x, F_LANE - I_LANE, 1) - b
            cm = _cummax_time(a, reverse)
            a_t = a.T
            q = (q_ref[rows, :].astype(F32) * MLSTM_DK ** -0.5).astype(BF16)
            kf32 = k_ref[rows, :].astype(F32)
            k_t = [kf32[:, :128].T, kf32[:, 128:].T]
            for h in range(N_HEADS):
                gf = F_LANE + d * N_HEADS + h
                sl = slice(h * MLSTM_DK, (h + 1) * MLSTM_DK)
                cm_col, a_row = cm[:, gf:gf + 1], a_t[gf:gf + 1, :]
                chains[d, j, h] = dict(
                    o_ref=o_ref, rows=rows, q=q[:, sl], k=k_ref[rows, sl], a_row=a_row, cm_col=cm_col,
                    b_col=b[:, gf:gf + 1], b_last=b[last:last + 1, gf:gf + 1], cm_last=cm[last:last + 1, gf:gf + 1],
                    w_intra=jnp.where(incl, jnp.exp(jnp.minimum(a_row - cm_col, 0.0)), 0.0),
                    k_th=k_t[h // 2][(h % 2) * MLSTM_DK:(h % 2 + 1) * MLSTM_DK, :],
                    v_aug=jnp.concatenate([v_ref[rows, h * HEAD_V:(h + 1) * HEAD_V], ones_col], axis=1))

    for c in chains.values():
        c["p"] = (c["w_intra"] * _dot_nt(c["q"], c["k"])).astype(BF16)
    for c in chains.values():
        c["pv"] = _dot(c["p"], c["v_aug"])
    c_augs = {(d, h): c_ref[d, h] for d in range(2) for h in range(N_HEADS)}
    m_ss = {(d, h): m_ref[d, h][0:1, 0:1] for d in range(2) for h in range(N_HEADS)}
    for i in range(nsub):
        cur = {(d, h): chains[d, _sweep(d, nsub)[i], h] for (d, h) in c_augs}
        qcs = {key: _dot(c["q"], c_augs[key].astype(BF16)) for key, c in cur.items()}
        tops = {key: jnp.maximum(m_ss[key], c["cm_last"]) for key, c in cur.items()}
        updates = {key: _dot((c["k_th"] * jnp.exp(c["a_row"] - tops[key])).astype(BF16), c["v_aug"])
                   for key, c in cur.items()}
        for (d, h), c in cur.items():
            m_s = m_ss[d, h]
            mx = jnp.maximum(m_s, c["cm_col"])
            tot = jnp.exp(m_s - mx) * qcs[d, h] + jnp.exp(c["cm_col"] - mx) * c["pv"]
            den = jnp.maximum(jnp.abs(tot[:, HEAD_V:HEAD_V + 1]), jnp.exp(-(c["b_col"] + mx)))
            c["o_ref"][c["rows"], h * HEAD_V:(h + 1) * HEAD_V] = tot[:, :HEAD_V] / den
        for key, c in cur.items():
            c_augs[key] = jnp.exp(m_ss[key] - tops[key]) * c_augs[key] + updates[key]
            m_ss[key] = c["b_last"] + tops[key]
    for (d, h) in c_augs:
        c_ref[d, h] = c_augs[d, h]
        m_ref[d, h] = jnp.broadcast_to(m_ss[d, h], (8, 128))


def mlstm_scan(z, small, bias, c0, m0):
    l = z.shape[0]
    nsteps, rows = _scan_grid(l)
    w = N_HEADS * MLSTM_DK
    specs = (_dir_specs(nsteps, rows, w, OFF_MLSTM_Q // w) + _dir_specs(nsteps, rows, w, OFF_MLSTM_K // w)
             + _dir_specs(nsteps, rows, BRANCH_W, OFF_MLSTM_V // BRANCH_W) + _dir_specs(nsteps, rows, 128, 0))
    specs += [_whole(bias), _whole(c0), _whole(m0)]
    return pl.pallas_call(
        _mlstm_kernel,
        out_shape=(jax.ShapeDtypeStruct((l, BRANCH_W), F32), jax.ShapeDtypeStruct((l, BRANCH_W), F32),
                   jax.ShapeDtypeStruct(c0.shape, F32), jax.ShapeDtypeStruct(m0.shape, F32)),
        grid=(nsteps,),
        in_specs=specs,
        out_specs=tuple(_dir_specs(nsteps, rows, BRANCH_W, 0)) + (_whole(c0), _whole(m0)),
        compiler_params=_cparams(("arbitrary",)),
        name="mlstm_scan",
    )(z, z, z, z, z, z, small, small, bias, c0, m0)


QKV_W = 3 * N_HEADS * GDN_DK
HALO = 8


def _gdn_prep_kernel(x_ref, prev_ref, next_ref, w_ref, q_ref, k_ref, v_ref):
    i = pl.program_id(0)
    tm = x_ref.shape[0]
    x = x_ref[...].astype(F32)
    prev = jnp.where(i > 0, prev_ref[...].astype(F32), 0.0)
    nxt = jnp.where(i < pl.num_programs(0) - 1, next_ref[...].astype(F32), 0.0)
    xe = jnp.concatenate([prev, x, nxt], axis=0)
    w = w_ref[...]
    y = None
    for tap in range(CONV_K):
        off = HALO + tap - CONV_K // 2
        term = w[tap:tap + 1] * xe[off:off + tm]
        y = term if y is None else y + term
    y = y * jax.nn.sigmoid(y)
    hw = N_HEADS * GDN_DK
    for h in range(N_HEADS):
        sl = slice(h * GDN_DK, (h + 1) * GDN_DK)
        qh = y[:, sl]
        kh = y[:, hw + h * GDN_DK:hw + (h + 1) * GDN_DK]
        q_ref[:, sl] = (qh * lax.rsqrt(jnp.sum(qh * qh, axis=-1, keepdims=True) + EPS) * GDN_DK ** -0.5).astype(BF16)
        k_ref[:, sl] = (kh * lax.rsqrt(jnp.sum(kh * kh, axis=-1, keepdims=True) + EPS)).astype(BF16)
    v_ref[...] = y[:, 2 * hw:].astype(BF16)


def gdn_prep(z, conv_w):
    l = z.shape[0]
    tm = min(l, 256)
    nb = l // tm
    r8 = tm // HALO
    cb = OFF_GDN_QKV // QKV_W
    w8 = jnp.concatenate([conv_w.astype(F32), jnp.zeros((8 - CONV_K, QKV_W), F32)], axis=0)
    return pl.pallas_call(
        _gdn_prep_kernel,
        out_shape=tuple(jax.ShapeDtypeStruct((l, BRANCH_W), BF16) for _ in range(3)),
        grid=(nb,),
        in_specs=[pl.BlockSpec((tm, QKV_W), lambda i: (i, cb)),
                  pl.BlockSpec((HALO, QKV_W), lambda i: (jnp.maximum(i * r8 - 1, 0), cb)),
                  pl.BlockSpec((HALO, QKV_W), lambda i: (jnp.minimum((i + 1) * r8, nb * r8 - 1), cb)),
                  pl.BlockSpec((8, QKV_W), lambda i: (0, 0))],
        out_specs=tuple(pl.BlockSpec((tm, BRANCH_W), lambda i: (i, 0)) for _ in range(3)),
        compiler_params=_cparams(("parallel",)),
        name="gdn_prep",
    )(z, z, z, w8)


N_LEVELS = 7


def _gdn_kernel(qf, qb, kf, kb, vf, vb, smf, smb, par_ref, s0_ref, of_ref, ob_ref, s_ref):
    @pl.when(pl.program_id(0) == 0)
    def _():
        s_ref[...] = s0_ref[...]

    lane = lax.broadcasted_iota(jnp.int32, (CH, 128), 1)
    is_decay = lane < B_LANE
    ri = lax.broadcasted_iota(jnp.int32, (CH, CH), 0)
    ci = lax.broadcasted_iota(jnp.int32, (CH, CH), 1)
    eye = jnp.where(ri == ci, 1.0, 0.0)
    pair_masks = [((ri >> (l + 1)) == (ci >> (l + 1))) & ((ri >> l) != (ci >> l)) for l in range(N_LEVELS)]

    nsub = qf.shape[0] // CH
    chains = {}
    for d, (q_ref, k_ref, v_ref, sm_ref, o_ref) in enumerate(((qf, kf, vf, smf, of_ref), (qb, kb, vb, smb, ob_ref))):
        reverse = d == 1
        incl, strict = _masks(reverse)
        last = 0 if reverse else CH - 1
        for j in range(nsub):
            rows = slice(j * CH, (j + 1) * CH)
            sm = sm_ref[rows, :]
            log_a = -jnp.exp(par_ref[1:2]) * jax.nn.softplus(sm + par_ref[0:1])
            x = jnp.where(is_decay, log_a, jax.nn.sigmoid(sm))
            g = _cumsum_time(incl, x)
            g_t = g.T
            for h in range(N_HEADS):
                ga, gb = A_LANE + d * N_HEADS + h, B_LANE + d * N_HEADS + h
                sl = slice(h * GDN_DK, (h + 1) * GDN_DK)
                g_col, g_row = g[:, ga:ga + 1], g_t[ga:ga + 1, :]
                g_last = g[last:last + 1, ga:ga + 1]
                beta = x[:, gb:gb + 1]
                decay = jnp.where(incl, jnp.exp(jnp.minimum(g_col - g_row, 0.0)), 0.0)
                kh = k_ref[rows, sl]
                kf32 = kh.astype(F32)
                kbeta = kf32 * beta
                a = jnp.where(strict, _dot_nt(kbeta.astype(BF16), kh) * decay, 0.0)
                e_g = jnp.exp(g_col)
                rhs = jnp.concatenate([v_ref[rows, sl].astype(F32) * beta, kbeta * e_g], axis=1)
                chains[d, j, h] = dict(
                    rows=rows, sl=sl, o_ref=o_ref, a=a, rhs=rhs, s_decay=jnp.exp(g_last),
                    qk=(_dot_nt(q_ref[rows, sl], kh) * decay).astype(BF16),
                    q_in=(q_ref[rows, sl].astype(F32) * e_g).astype(BF16),
                    k_out_t=(kf32.T * jnp.exp(g_last - g_row)).astype(BF16))

    cl = list(chains.values())
    xs = [eye - jnp.where(pair_masks[0], c["a"], 0.0) for c in cl]
    for pm in pair_masks[1:]:
        ys = [_dot(jnp.where(pm, c["a"], 0.0).astype(BF16), x.astype(BF16)) for c, x in zip(cl, xs)]
        xs = [x - _dot(x.astype(BF16), y.astype(BF16)) for x, y in zip(xs, ys)]
    for c, x in zip(cl, xs):
        c["sol"] = c["rhs"] + _dot((x - eye).astype(BF16), c["rhs"].astype(BF16))
    states = {(d, h): s_ref[d, h] for d in range(2) for h in range(N_HEADS)}
    for i in range(nsub):
        cur = {(d, h): chains[d, _sweep(d, nsub)[i], h] for (d, h) in states}
        s16s = {key: s.astype(BF16) for key, s in states.items()}
        v_news = {key: (c["sol"][:, :HEAD_V] - _dot(c["sol"][:, HEAD_V:].astype(BF16), s16s[key])).astype(BF16)
                  for key, c in cur.items()}
        updates = {key: _dot(c["k_out_t"], v_news[key]) for key, c in cur.items()}
        for key, c in cur.items():
            c["o_ref"][c["rows"], c["sl"]] = _dot(c["q_in"], s16s[key]) + _dot(c["qk"], v_news[key])
        for key, c in cur.items():
            states[key] = states[key] * c["s_decay"] + updates[key]
    for (d, h), s in states.items():
        s_ref[d, h] = s


def gdn_scan(qn, kn, vn, small, par, s0):
    l = qn.shape[0]
    nsteps, rows = _scan_grid(l)
    specs = _dir_specs(nsteps, rows, BRANCH_W, 0) * 3 + _dir_specs(nsteps, rows, 128, 0)
    specs += [_whole(par), _whole(s0)]
    return pl.pallas_call(
        _gdn_kernel,
        out_shape=(jax.ShapeDtypeStruct((l, BRANCH_W), F32), jax.ShapeDtypeStruct((l, BRANCH_W), F32),
                   jax.ShapeDtypeStruct(s0.shape, F32)),
        grid=(nsteps,),
        in_specs=specs,
        out_specs=tuple(_dir_specs(nsteps, rows, BRANCH_W, 0)) + (_whole(s0),),
        compiler_params=_cparams(("arbitrary",)),
        name="gdn_scan",
    )(qn, qn, kn, kn, vn, vn, small, small, par, s0)


def _lane_rows(entries):
    r = jnp.zeros((8, 128), F32)
    for row, lane, vals in entries:
        r = r.at[row, lane:lane + vals.shape[0]].set(vals.astype(F32))
    return r


def gdn_branch(zc, sc, zl, sl, conv_w, a_log, dt_bias):
    par = _lane_rows([(0, A_LANE, dt_bias.reshape(-1)), (1, A_LANE, a_log.reshape(-1))])
    s0 = jnp.zeros((2, N_HEADS, GDN_DK, HEAD_V), F32)
    ofc, obc, s1 = gdn_scan(*gdn_prep(zc, conv_w), sc, par, s0)
    ofl, obl, _ = gdn_scan(*gdn_prep(zl, conv_w), sl, par, s1)
    return (ofc, obc), (ofl, obl)


def mlstm_branch(zc, sc, zl, sl, gate_b):
    bias = _lane_rows([(0, I_LANE, gate_b[0].reshape(-1)), (0, F_LANE, gate_b[1].reshape(-1))])
    c0 = jnp.zeros((2, N_HEADS, MLSTM_DK, 2 * HEAD_V), F32)
    m0 = jnp.zeros((2, N_HEADS, 8, 128), F32)
    ofc, obc, c1, m1 = mlstm_scan(zc, sc, bias, c0, m0)
    ofl, obl, _, _ = mlstm_scan(zl, sl, bias, c1, m1)
    return (ofc, obc), (ofl, obl)


def retention_branch(zc, zl, tables):
    s0 = jnp.zeros((2, N_HEADS, RET_DK, HEAD_V), F32)
    ofc, obc, s1 = retention_scan(zc, s0, None)
    ofl, obl, _ = retention_scan(zl, s1, tables)
    return (ofc, obc), (ofl, obl)


def _ret_rope_tables(n_lat):
    inv = np.float32(ROPE_BASE) ** (-np.arange(0, RET_DK, 2, dtype=np.float32) / np.float32(RET_DK))
    ang = np.arange(n_lat, dtype=np.float32)[:, None] * inv[None, :]
    cos = np.concatenate([np.cos(ang), np.cos(ang)], axis=-1)
    sin = np.concatenate([-np.sin(ang), np.sin(ang)], axis=-1)
    return jnp.asarray(np.tile(cos, (1, N_HEADS)), F32), jnp.asarray(np.tile(sin, (1, N_HEADS)), F32)


def _rope_tables(n_lat):
    q = MLA_ROPE // 4
    inv = np.float32(ROPE_BASE) ** (-np.arange(0, 2 * q, 2, dtype=np.float32) / np.float32(2 * q))
    t = np.arange(n_lat)
    row = (t // GRID_W).astype(np.float32)[:, None] * inv[None, :]
    col = (t % GRID_W).astype(np.float32)[:, None] * inv[None, :]
    cos = np.concatenate([np.cos(row), np.cos(row), np.cos(col), np.cos(col)], axis=-1)
    sin = np.concatenate([-np.sin(row), np.sin(row), -np.sin(col), np.sin(col)], axis=-1)
    return jnp.asarray(cos, F32), jnp.asarray(sin, F32)


def _take_cols(w, perm):
    runs, start = [], 0
    for i in range(1, len(perm) + 1):
        if i == len(perm) or perm[i] != perm[i - 1] + 1:
            runs.append((int(perm[start]), int(perm[i - 1]) + 1))
            start = i
    return jnp.concatenate([w[:, a:b].astype(BF16) for a, b in runs], axis=1)


def _pad_rows(v, n=8):
    rows = [jnp.reshape(r, (1, -1)).astype(F32) for r in v]
    d = rows[0].shape[1]
    return jnp.concatenate(rows + [jnp.zeros((n - len(rows), d), F32)], axis=0)


def _mla_weights(w_uq, w_ukv):
    wq = w_uq.reshape(MLA_Q_RANK, N_HEADS, MLA_QK)
    rope = wq[:, :, MLA_NOPE:]
    swapped = rope[:, :, _rope_swap(np.arange(MLA_ROPE))]
    wq_ext = jnp.concatenate([wq[:, :, :MLA_NOPE], rope, swapped], axis=-1).reshape(MLA_Q_RANK, N_HEADS * 256)
    return wq_ext.astype(BF16), w_ukv.astype(BF16)


def _moe(x_l, pv2, g2, w_router, b_router, w_gu, w_down, final_g):
    n = x_l.shape[0]
    w_pad = jnp.concatenate([w_router, jnp.zeros((D_MODEL, 128 - N_EXPERTS), F32)], axis=1)
    b_pad = jnp.concatenate([b_router, jnp.full((128 - N_EXPERTS,), -1e30, F32)]).reshape(1, 128)
    h2, route = moe_router(x_l, pv2, w_pad, b_pad)
    n_assign = n * TOP_K
    experts = jnp.arange(N_EXPERTS)[None, :]
    e0, e1 = route[:, TOP_K].astype(jnp.int32), route[:, TOP_K + 1].astype(jnp.int32)
    hot0, hot1 = (e0[:, None] == experts).astype(jnp.int32), (e1[:, None] == experts).astype(jnp.int32)
    c0, c1 = jnp.cumsum(hot0, axis=0), jnp.cumsum(hot1, axis=0)
    before1 = c1 - hot1
    counts = c0[-1] + c1[-1]
    rank0 = jnp.sum(hot0 * (c0 - hot0 + before1), axis=1)
    rank1 = jnp.sum(hot1 * (c0 + before1), axis=1)
    padded = (counts + MOE_ROWS - 1) // MOE_ROWS * MOE_ROWS
    pad_end = jnp.cumsum(padded)
    pad_start = pad_end - padded
    dest = jnp.concatenate([pad_start[e0] + rank0, pad_start[e1] + rank1])
    nb = n_assign // MOE_ROWS + N_EXPERTS
    cap = nb * MOE_ROWS
    block_start = jnp.arange(nb, dtype=pad_end.dtype) * MOE_ROWS
    block_e = jnp.minimum(jnp.sum(block_start[:, None] >= pad_end[None, :], axis=1), N_EXPERTS - 1).astype(jnp.int32)
    n_used = (pad_end[-1] // MOE_ROWS).astype(jnp.int32).reshape(1)
    slot_tok = (jnp.arange(cap, dtype=jnp.int32) % n).at[dest].set(jnp.arange(n_assign, dtype=jnp.int32) % n)
    yb = expert_ffn(block_e, n_used, sc_gather(h2, slot_tok), w_gu, w_down)
    y2 = sc_gather(yb, dest.astype(jnp.int32))
    return final_combine(x_l, y2, route, _pad_rows([g2, final_g]))


def kernel(x, c, ctx, c_ctx, w_mod, b_mod, norm1_g, norm2_g, w_in, gdn_conv_w, gdn_a_log, gdn_dt_bias, gdn_norm_g, mla_q_norm_g, mla_kv_norm_g, mla_w_uq, mla_w_ukv, mlstm_gate_b, mlstm_norm_g, ret_norm_g, w_branch, w_out, ffn_w_in, ffn_w_down, moe_w_router, moe_b_router, moe_w_in, moe_w_down, final_norm_g):
    n_lat = x.shape[1]
    n_ctx = ctx.shape[1]
    x_l, x_c = x[0], ctx[0]
    cond = _pad_rows([c_ctx, c[0]])
    cos_l, sin_l = _rope_tables(n_lat)
    cos_c, sin_c = jnp.ones((n_ctx, MLA_ROPE), F32), jnp.zeros((n_ctx, MLA_ROPE), F32)
    ret_tables = _ret_rope_tables(n_lat)
    out = None
    for li in range(DEPTH):
        last = li == DEPTH - 1
        mod = modulation_vectors(cond, w_mod[li], b_mod[li])
        csh1, csc1, cg1, csh2, csc2, cg2 = jnp.split(mod[0], 6)
        sh1, sc1, g1, sh2, sc2, g2 = jnp.split(mod[1], 6)
        w_main = _take_cols(w_in[li], _MAIN_PERM)
        w_small = _take_cols(w_in[li], _SMALL_PERM)
        w_small = jnp.concatenate([w_small, jnp.zeros((D_MODEL, N_SMALL - w_small.shape[1]), BF16)], axis=1)
        pv_l = _pad_rows([norm1_g[li], 1 + sc1, sh1])
        pv_c = _pad_rows([norm1_g[li], 1 + csc1, csh1])
        zl, sl = norm_proj(x_l, pv_l, w_main, w_small, 3328)
        zc, sc = norm_proj(x_c, pv_c, w_main, w_small, 3328)

        a_c, a_l = gdn_branch(zc, sc, zl, sl, gdn_conv_w[li], gdn_a_log[li], gdn_dt_bias[li])
        c_c, c_l = mlstm_branch(zc, sc, zl, sl, mlstm_gate_b[li])
        d_c, d_l = retention_branch(zc, zl, ret_tables)
        gains = _pad_rows([gdn_norm_g[li], mlstm_norm_g[li], ret_norm_g[li]])

        wq_ext, wkv = _mla_weights(mla_w_uq[li], mla_w_ukv[li])
        gq, gkv = mla_q_norm_g[li].reshape(1, -1), mla_kv_norm_g[li].reshape(1, -1)
        ql, kl, vl = mla_project(zl, cos_l, sin_l, gq, gkv, wq_ext, wkv)
        qc, kc, vc = mla_project(zc, cos_c, sin_c, gq, gkv, wq_ext, wkv)
        b_l = attention(ql, [(kl, vl), (kc, vc)])

        wb = w_branch[li].astype(BF16)
        wo = w_out[li].astype(BF16)
        x_l = merge_branches(x_l, zl, a_l, b_l, c_l, d_l, gains, wb, wo, _pad_rows([g1]))
        if not last:
            b_c = attention(qc, [(kc, vc)])
            x_c = merge_branches(x_c, zc, a_c, b_c, c_c, d_c, gains, wb, wo, _pad_rows([cg1]))

        if li % 2 == 0:
            w_gu = ffn_w_in[li // 2].astype(BF16)
            w_dn = ffn_w_down[li // 2].astype(BF16)
            assert not last
            x_l = dense_ffn(x_l, _pad_rows([norm2_g[li], 1 + sc2, sh2, g2]), w_gu, w_dn)
            x_c = dense_ffn(x_c, _pad_rows([norm2_g[li], 1 + csc2, csh2, cg2]), w_gu, w_dn)
        else:
            assert last
            out = _moe(x_l, _pad_rows([norm2_g[li], 1 + sc2, sh2]), g2, moe_w_router[li // 2],
                       moe_b_router[li // 2], moe_w_in[li // 2], moe_w_down[li // 2], final_norm_g)
    return out[None]
```
